```python
import math
import jax, jax.numpy as jnp
from jax import lax
import numpy as np


D_MODEL = 2048
BATCH = 8
SEQ = 2048
DEPTH = 2

CHUNK = 64
Q_BLOCK = 128
GROUP_W = 512
D_MIX = 4 * GROUP_W
EPS = 1e-6
NEG = -1e30

GMLP_BLOCK = 128
A_GROUPS = 4
A_GDIM = GROUP_W // A_GROUPS

B_HEADS = 8
B_HDIM = GROUP_W // B_HEADS
IDX_HEADS = 8
IDX_DIM = 64
TOPK_MAX = 256
T5_BUCKETS = 32
T5_MAX_DIST = 128

C_HEADS = 4
C_NOPE = 128
C_ROPE = 64
C_VDIM = GROUP_W // C_HEADS
C_QK = C_NOPE + C_ROPE
Q_LORA = 384
KV_LORA = 128
ROPE_BASE = 10000.0

D_HEADS = 8
D_HDIM = GROUP_W // D_HEADS
D_LEFT_CHUNKS = 8
D_BAND = (D_LEFT_CHUNKS + 1) * CHUNK
REL_CLIP = 128

IN_SIZES = (
    GROUP_W, GROUP_W, GROUP_W,
    GROUP_W, B_HDIM, B_HDIM, IDX_HEADS * IDX_DIM, IDX_DIM, IDX_HEADS, GROUP_W,
    Q_LORA, KV_LORA, C_ROPE, GROUP_W,
    GROUP_W, GROUP_W, GROUP_W, GROUP_W,
)
IN_COLS = sum(IN_SIZES)

kernel_name = "chunk_causal_hybrid_head_groups"


def rms_norm(x, g):
    xf = x.astype(jnp.float32)
    y = xf * lax.rsqrt(jnp.mean(xf * xf, axis=-1, keepdims=True) + EPS)
    return (y * g.astype(jnp.float32)).astype(x.dtype)


def softmax_f32(s):
    return jax.nn.softmax(s.astype(jnp.float32), axis=-1)


def split_cols(h, sizes):
    out, o = [], 0
    for n in sizes:
        out.append(h[..., o:o + n])
        o += n
    return out


def to_blocks(a, size):
    b, s = a.shape[0], a.shape[1]
    return a.reshape(b, s // size, size, *a.shape[2:]).swapaxes(0, 1)


def from_blocks(o):
    nb, b, size = o.shape[0], o.shape[1], o.shape[2]
    return o.swapaxes(0, 1).reshape(b, nb * size, -1)


def gmlp_mixer(u, v, v_gain, w_s, b_s):
    bsz, s, _ = u.shape
    nb = s // GMLP_BLOCK
    u = jax.nn.gelu(u)
    v = rms_norm(jax.nn.gelu(v), v_gain)
    pos_chunk = jnp.arange(GMLP_BLOCK) // CHUNK
    mask = pos_chunk[None, :] <= pos_chunk[:, None]
    w = jnp.where(mask[None], w_s, 0.0)
    vb = v.reshape(bsz, nb, GMLP_BLOCK, A_GROUPS, A_GDIM)
    sg = jnp.einsum('gij,bnjgc->bnigc', w, vb) + b_s.T[None, None, :, :, None]
    return u * sg.reshape(bsz, s, GROUP_W)


def t5_bucket(rel):
    nb = T5_BUCKETS // 2
    max_exact = nb // 2
    ret = jnp.where(rel > 0, nb, 0)
    n = jnp.abs(rel)
    nf = jnp.maximum(n, 1).astype(jnp.float32)
    large = max_exact + (jnp.log(nf / max_exact) / math.log(T5_MAX_DIST / max_exact)
                         * (nb - max_exact)).astype(jnp.int32)
    large = jnp.minimum(large, nb - 1)
    return ret + jnp.where(n < max_exact, n, large)


def dsa_mixer(q, k, v, iq, ik, iw, q_gain, k_gain, t5_bias):
    bsz, s, _ = q.shape
    topk = min(TOPK_MAX, s // 4)
    q = rms_norm(q.reshape(bsz, s, B_HEADS, B_HDIM), q_gain)
    k = rms_norm(k, k_gain)
    iq = iq.reshape(bsz, s, IDX_HEADS, IDX_DIM)
    key_chunk = jnp.arange(s) // CHUNK
    starts = jnp.arange(s // Q_BLOCK) * Q_BLOCK

    def block(args):
        qb, iqb, iwb, start = args
        qpos = start + jnp.arange(Q_BLOCK)
        qchunk = qpos // CHUNK
        adm = key_chunk[None, :] <= qchunk[:, None]
        logits = jnp.einsum('bthd,bsd->bths', iqb, ik).astype(jnp.float32) * (IDX_DIM ** -0.5)
        score = jnp.einsum('bth,bths->bts', iwb.astype(jnp.float32) * (IDX_HEADS ** -0.5),
                           jax.nn.relu(logits))
        score = jnp.where(adm[None], score, -jnp.inf)
        _, idx = lax.top_k(score, topk)
        valid = (idx // CHUNK) <= qchunk[None, :, None]
        ks = jax.vmap(lambda a, i: a[i])(k, idx)
        vs = jax.vmap(lambda a, i: a[i])(v, idx)
        sc = jnp.einsum('bthd,btkd->bhtk', qb, ks).astype(jnp.float32) * (B_HDIM ** -0.5)
        bias = t5_bias[t5_bucket(idx - qpos[None, :, None])]
        sc = jnp.where(valid[:, None], sc + bias.transpose(0, 3, 1, 2).astype(jnp.float32), NEG)
        p = softmax_f32(sc).astype(vs.dtype)
        return jnp.einsum('bhtk,btkd->bthd', p, vs)

    out = lax.map(block, (to_blocks(q, Q_BLOCK), to_blocks(iq, Q_BLOCK),
                          to_blocks(iw, Q_BLOCK), starts))
    return from_blocks(out)


def rope_tables(s):
    inv = ROPE_BASE ** (-jnp.arange(0, C_ROPE, 2, dtype=jnp.float32) / C_ROPE)
    ang = jnp.arange(s, dtype=jnp.float32)[:, None] * inv[None, :]
    return jnp.cos(ang), jnp.sin(ang)


def apply_rope(x, cos, sin):
    x1, x2 = jnp.split(x.astype(jnp.float32), 2, axis=-1)
    c, s_ = cos[None, :, None], sin[None, :, None]
    return jnp.concatenate([x1 * c - x2 * s_, x1 * s_ + x2 * c], axis=-1).astype(x.dtype)


def mla_mixer(cq, ckv, krope, qa_gain, kva_gain, w_qb, w_kvb, q_gain, k_gain):
    bsz, s, _ = cq.shape
    cq = rms_norm(cq, qa_gain)
    ckv = rms_norm(ckv, kva_gain)
    q = (cq @ w_qb).reshape(bsz, s, C_HEADS, C_QK)
    kv = (ckv @ w_kvb).reshape(bsz, s, C_HEADS, C_NOPE + C_VDIM)
    k_nope, v = kv[..., :C_NOPE], kv[..., C_NOPE:]
    k = jnp.concatenate([k_nope, jnp.broadcast_to(krope[:, :, None], (bsz, s, C_HEADS, C_ROPE))], axis=-1)
    q = rms_norm(q, q_gain)
    k = rms_norm(k, k_gain)
    cos, sin = rope_tables(s)
    q = jnp.concatenate([q[..., :C_NOPE], apply_rope(q[..., C_NOPE:], cos, sin)], axis=-1)
    k = jnp.concatenate([k[..., :C_NOPE], apply_rope(k[..., C_NOPE:], cos, sin)], axis=-1)
    key_chunk = jnp.arange(s) // CHUNK
    starts = jnp.arange(s // Q_BLOCK) * Q_BLOCK

    def block(args):
        qb, start = args
        qchunk = (start + jnp.arange(Q_BLOCK)) // CHUNK
        mask = key_chunk[None, :] <= qchunk[:, None]
        sc = jnp.einsum('bthd,bshd->bhts', qb, k).astype(jnp.float32) * (C_QK ** -0.5)
        sc = jnp.where(mask[None, None], sc, NEG)
        p = softmax_f32(sc).astype(v.dtype)
        return jnp.einsum('bhts,bshd->bthd', p, v)

    out = lax.map(block, (to_blocks(q, Q_BLOCK), starts))
    return from_blocks(out)


def band_mixer(q, k, v, q_gain, k_gain, rel_bias):
    bsz, s, _ = q.shape
    nc = s // CHUNK
    q = rms_norm(q.reshape(bsz, s, D_HEADS, D_HDIM), q_gain)
    k = rms_norm(k.reshape(bsz, s, D_HEADS, D_HDIM), k_gain)
    v = v.reshape(bsz, s, D_HEADS, D_HDIM)
    pad = D_LEFT_CHUNKS * CHUNK
    kp = jnp.pad(k, ((0, 0), (pad, 0), (0, 0), (0, 0)))
    vp = jnp.pad(v, ((0, 0), (pad, 0), (0, 0), (0, 0)))
    i = jnp.arange(CHUNK)
    j = jnp.arange(D_BAND)
    dist = (pad + i)[:, None] - j[None, :]
    bias = rel_bias[jnp.clip(dist, -REL_CLIP, REL_CLIP) + REL_CLIP].transpose(2, 0, 1)
    bias = bias.astype(jnp.float32)

    def chunk(args):
        qb, c = args
        kb = lax.dynamic_slice_in_dim(kp, c * CHUNK, D_BAND, axis=1)
        vb = lax.dynamic_slice_in_dim(vp, c * CHUNK, D_BAND, axis=1)
        valid = j >= (D_LEFT_CHUNKS - c) * CHUNK
        sc = jnp.einsum('bthd,bshd->bhts', qb, kb).astype(jnp.float32) * (D_HDIM ** -0.5) + bias[None]
        sc = jnp.where(valid[None, None, None], sc, NEG)
        p = softmax_f32(sc).astype(vb.dtype)
        return jnp.einsum('bhts,bshd->bthd', p, vb)

    out = lax.map(chunk, (to_blocks(q, CHUNK), jnp.arange(nc)))
    return from_blocks(out)


def setup_inputs(seed: int = 0) -> dict:
    key = jax.random.key(seed)
    ks = jax.random.split(key, 19)

    def nrm(k, shape, scale):
        return scale * jax.random.normal(k, shape, jnp.float32)

    def gain(k, shape):
        return 1.0 + 0.05 * jax.random.normal(k, shape, jnp.float32)

    return {
        "x": nrm(ks[0], (BATCH, SEQ, D_MODEL), 1.0),
        "t5_bias": nrm(ks[1], (T5_BUCKETS, B_HEADS), 0.5),
        "norm_g": gain(ks[2], (DEPTH, D_MODEL)),
        "w_in": nrm(ks[3], (DEPTH, D_MODEL, IN_COLS), D_MODEL ** -0.5),
        "a_v_gain": gain(ks[4], (DEPTH, GROUP_W)),
        "a_ws": nrm(ks[5], (DEPTH, A_GROUPS, GMLP_BLOCK, GMLP_BLOCK), GMLP_BLOCK ** -0.5),
        "a_bs": 1.0 + nrm(ks[6], (DEPTH, A_GROUPS, GMLP_BLOCK), 0.1),
        "b_q_gain": gain(ks[7], (DEPTH, B_HDIM)),
        "b_k_gain": gain(ks[8], (DEPTH, B_HDIM)),
        "c_qa_gain": gain(ks[9], (DEPTH, Q_LORA)),
        "c_kva_gain": gain(ks[10], (DEPTH, KV_LORA)),
        "c_w_qb": nrm(ks[11], (DEPTH, Q_LORA, C_HEADS * C_QK), Q_LORA ** -0.5),
        "c_w_kvb": nrm(ks[12], (DEPTH, KV_LORA, C_HEADS * (C_NOPE + C_VDIM)), KV_LORA ** -0.5),
        "c_q_gain": gain(ks[13], (DEPTH, C_QK)),
        "c_k_gain": gain(ks[14], (DEPTH, C_QK)),
        "d_q_gain": gain(ks[15], (DEPTH, D_HDIM)),
        "d_k_gain": gain(ks[16], (DEPTH, D_HDIM)),
        "d_rel_bias": nrm(ks[17], (DEPTH, 2 * REL_CLIP + 1, D_HEADS), 0.5),
        "w_out": nrm(ks[18], (DEPTH, D_MIX, D_MODEL), D_MIX ** -0.5),
    }


def reference(x, t5_bias, norm_g, w_in, a_v_gain, a_ws, a_bs, b_q_gain, b_k_gain,
              c_qa_gain, c_kva_gain, c_w_qb, c_w_kvb, c_q_gain, c_k_gain,
              d_q_gain, d_k_gain, d_rel_bias, w_out):
    for l in range(DEPTH):
        h = rms_norm(x, norm_g[l]) @ w_in[l]
        (a_u, a_v, a_z,
         b_q, b_k, b_v, b_iq, b_ik, b_iw, b_z,
         c_q, c_kv, c_kr, c_z,
         d_q, d_k, d_v, d_z) = split_cols(h, IN_SIZES)
        y_a = gmlp_mixer(a_u, a_v, a_v_gain[l], a_ws[l], a_bs[l]) * jax.nn.silu(a_z)
        y_b = dsa_mixer(b_q, b_k, b_v, b_iq, b_ik, b_iw, b_q_gain[l], b_k_gain[l], t5_bias) * jax.nn.silu(b_z)
        y_c = mla_mixer(c_q, c_kv, c_kr, c_qa_gain[l], c_kva_gain[l], c_w_qb[l], c_w_kvb[l],
                        c_q_gain[l], c_k_gain[l]) * jax.nn.silu(c_z)
        y_d = band_mixer(d_q, d_k, d_v, d_q_gain[l], d_k_gain[l], d_rel_bias[l]) * jax.nn.silu(d_z)
        x = x + jnp.concatenate([y_a, y_b, y_c, y_d], axis=-1) @ w_out[l]
    return x
```

```python
import functools
import math

import numpy as np
import jax
import jax.numpy as jnp
from jax import lax
from jax.experimental import pallas as pl
from jax.experimental.pallas import tpu as pltpu

F32 = jnp.float32
BF16 = jnp.bfloat16

EPS = 1e-6
NEG = -1e30
INT_MIN = -(2 ** 31)
CHUNK = 64
LANES = 128
GROUP_W = 512
A_GROUPS = 4
GMLP_BLOCK = 128
N_HEADS64 = 8
IDX_SCALE = (8 ** -0.5) * 0.125
TOPK_MAX = 256
T5_BUCKETS = 32
C_HEADS = 4
C_NOPE = 128
C_ROPE = 64
C_QK = 192
Q_LORA = 384
KV_LORA = 128
ROPE_BASE = 10000.0
D_LEFT = 8 * CHUNK
REL_CLIP = 128
VMEM_LIMIT = 56 * 1024 * 1024

BLK_A_U, BLK_A_V, BLK_A_Z = 0, 1, 2
BLK_B_Q, BLK_B_IQ, BLK_B_Z, BLK_B_SMALL = 3, 4, 5, 6
BLK_C_Q, BLK_C_SMALL, BLK_C_Z = 7, 8, 9
BLK_D_Q, BLK_D_K, BLK_D_V, BLK_D_Z = 10, 11, 12, 13
H_COLS = 14 * GROUP_W

_SRC = dict(a_u=0, a_v=512, a_z=1024, b_q=1536, b_k=2048, b_v=2112, b_iq=2176, b_ik=2688,
            b_iw=2752, b_z=2760, c_q=3272, c_kv=3656, c_kr=3784, c_z=3848,
            d_q=4360, d_k=4872, d_v=5384, d_z=5896)


def _params(n_axes):
    return pltpu.CompilerParams(dimension_semantics=("arbitrary",) * n_axes,
                                vmem_limit_bytes=VMEM_LIMIT)


def _gelu(x):
    c = math.sqrt(2.0 / math.pi)
    return x * (0.5 * (1.0 + jnp.tanh(c * (x + 0.044715 * (x * x * x)))))


def _silu(x):
    return x * (1.0 / (1.0 + jnp.exp(-x)))


def _dot_t(a, b):
    return lax.dot_general(a, b, (((1,), (1,)), ((), ())), preferred_element_type=F32)


def _lo_mask(rows):
    return lax.broadcasted_iota(jnp.int32, (rows, LANES), 1) < 64


def _rms_heads64(x, gain, ntiles):
    lo = _lo_mask(x.shape[0])
    tiles = []
    for t in range(ntiles):
        xt = x[:, t * LANES:(t + 1) * LANES]
        sq = xt * xt
        s_lo = jnp.sum(jnp.where(lo, sq, 0.0), axis=-1, keepdims=True)
        s_hi = jnp.sum(jnp.where(lo, 0.0, sq), axis=-1, keepdims=True)
        r = jnp.where(lo, lax.rsqrt(s_lo * (1.0 / 64) + EPS), lax.rsqrt(s_hi * (1.0 / 64) + EPS))
        tiles.append(xt * r * gain[:, t * LANES:(t + 1) * LANES])
    return tiles


def _toeplitz(base_row, rows, width):
    t = jnp.broadcast_to(base_row, (rows, base_row.shape[1]))
    t = pltpu.roll(t, 0, 1, stride=1, stride_axis=0)
    return t[:, :width]


def _inproj_kernel(x_ref, g_ref, w_ref, o_ref, xn_ref):
    @pl.when(pl.program_id(1) == 0)
    def _():
        x = x_ref[...]
        ms = jnp.mean(x * x, axis=-1, keepdims=True)
        xn_ref[...] = (x * lax.rsqrt(ms + EPS) * g_ref[...]).astype(BF16)

    o_ref[...] = jnp.dot(xn_ref[...], w_ref[...], preferred_element_type=F32)


def _inproj(x2, g, w, tm=512, tn=1792):
    n, d = x2.shape
    return pl.pallas_call(
        _inproj_kernel,
        grid=(n // tm, H_COLS // tn),
        in_specs=[pl.BlockSpec((tm, d), lambda i, j: (i, 0)),
                  pl.BlockSpec((1, d), lambda i, j: (0, 0)),
                  pl.BlockSpec((d, tn), lambda i, j: (0, j))],
        out_specs=pl.BlockSpec((tm, tn), lambda i, j: (i, j)),
        out_shape=jax.ShapeDtypeStruct((n, H_COLS), F32),
        scratch_shapes=[pltpu.VMEM((tm, d), BF16)],
        compiler_params=_params(2),
    )(x2, g, w)


def _outproj_kernel(x_ref, ya_ref, yb_ref, yc_ref, yd_ref, w_ref, o_ref):
    acc = x_ref[...]
    for g, y_ref in enumerate((ya_ref, yb_ref, yc_ref, yd_ref)):
        acc = acc + jnp.dot(y_ref[...], w_ref[g * GROUP_W:(g + 1) * GROUP_W, :],
                            preferred_element_type=F32)
    o_ref[...] = acc


def _outproj(x2, ys, w, tm=512):
    n, d = x2.shape
    yspec = pl.BlockSpec((tm, GROUP_W), lambda i: (i, 0))
    return pl.pallas_call(
        _outproj_kernel,
        grid=(n // tm,),
        in_specs=[pl.BlockSpec((tm, d), lambda i: (i, 0)), yspec, yspec, yspec, yspec,
                  pl.BlockSpec((4 * GROUP_W, d), lambda i: (0, 0))],
        out_specs=pl.BlockSpec((tm, d), lambda i: (i, 0)),
        out_shape=jax.ShapeDtypeStruct((n, d), F32),
        compiler_params=_params(1),
    )(x2, *ys, w)


def _mixer_a_kernel(u_ref, v_ref, z_ref, vg_ref, w_ref, b_ref, o_ref):
    tm = u_ref.shape[0]
    u = _gelu(u_ref[...])
    v = _gelu(v_ref[...])
    ms = jnp.mean(v * v, axis=-1, keepdims=True)
    vb = (v * lax.rsqrt(ms + EPS) * vg_ref[...]).astype(BF16)
    gate = _silu(z_ref[...])
    i = lax.broadcasted_iota(jnp.int32, (GMLP_BLOCK, GMLP_BLOCK), 0)
    j = lax.broadcasted_iota(jnp.int32, (GMLP_BLOCK, GMLP_BLOCK), 1)
    keep = (j // CHUNK) <= (i // CHUNK)
    for g in range(A_GROUPS):
        wg = jnp.where(keep, w_ref[g], 0.0).astype(BF16)
        cols = slice(g * LANES, (g + 1) * LANES)
        for blk in range(tm // GMLP_BLOCK):
            rows = slice(blk * GMLP_BLOCK, (blk + 1) * GMLP_BLOCK)
            sg = jnp.dot(wg, vb[rows, cols], preferred_element_type=F32) + b_ref[g]
            o_ref[rows, cols] = (u[rows, cols] * sg * gate[rows, cols]).astype(BF16)


def _mixer_a(h, vg, ws, bs, tm=512):
    n = h.shape[0]
    hspec = lambda blk: pl.BlockSpec((tm, GROUP_W), lambda i, blk=blk: (i, blk))
    return pl.pallas_call(
        _mixer_a_kernel,
        grid=(n // tm,),
        in_specs=[hspec(BLK_A_U), hspec(BLK_A_V), hspec(BLK_A_Z),
                  pl.BlockSpec((1, GROUP_W), lambda i: (0, 0)),
                  pl.BlockSpec((A_GROUPS, GMLP_BLOCK, GMLP_BLOCK), lambda i: (0, 0, 0)),
                  pl.BlockSpec((A_GROUPS, GMLP_BLOCK, 1), lambda i: (0, 0, 0))],
        out_specs=pl.BlockSpec((tm, GROUP_W), lambda i: (i, 0)),
        out_shape=jax.ShapeDtypeStruct((n, GROUP_W), BF16),
        compiler_params=_params(1),
    )(h, h, h, vg, ws, bs)


def _b_select_kernel(iq_ref, iw_ref, ik_ref, o_ref, ikb_ref, key_ref, *, seq, tq, topk):
    t_blk = pl.program_id(1)
    nkb = seq // 256

    @pl.when(t_blk == 0)
    def _():
        ikb_ref[...] = ik_ref[...].astype(BF16)

    lo = _lo_mask(tq)
    w = iw_ref[...][:, :N_HEADS64] * IDX_SCALE
    lhs = []
    for h in range(N_HEADS64):
        iqt = iq_ref[:, (h // 2) * LANES:(h // 2 + 1) * LANES]
        lhs.append(jnp.where(lo if h % 2 == 0 else jnp.logical_not(lo), iqt, 0.0).astype(BF16))

    qchunk = (t_blk * tq + lax.broadcasted_iota(jnp.int32, (tq, 256), 0)) // CHUNK
    kcol = lax.broadcasted_iota(jnp.int32, (tq, 256), 1)
    for kb in range(nkb):
        ikblk = ikb_ref[kb * 256:(kb + 1) * 256, :]
        score = jnp.zeros((tq, 256), F32)
        for h in range(N_HEADS64):
            score = score + w[:, h:h + 1] * jnp.maximum(_dot_t(lhs[h], ikblk), 0.0)
        score = jnp.where(score == 0.0, 0.0, score)
        bits = lax.bitcast_convert_type(score, jnp.int32)
        key = jnp.where(bits < 0, bits ^ 0x7FFFFFFF, bits)
        adm = ((kb * 256 + kcol) // CHUNK) <= qchunk
        key_ref[:, kb * 256:(kb + 1) * 256] = jnp.where(adm, key, INT_MIN)

    def count(pred):
        acc = jnp.zeros((tq, LANES), F32)
        for c in range(seq // LANES):
            acc = acc + jnp.where(pred(key_ref[:, c * LANES:(c + 1) * LANES], c), 1.0, 0.0)
        return jnp.sum(acc, axis=1, keepdims=True)

    kf = float(topk)
    zero = jnp.zeros((tq, 1), jnp.int32)
    thr0 = jnp.where(count(lambda k, c: k >= zero) >= kf, zero, INT_MIN)

    def bit_step(it, thr):
        cand = thr | lax.shift_left(jnp.int32(1), 30 - it)
        return jnp.where(count(lambda k, c: k >= cand) >= kf, cand, thr)

    thr = lax.fori_loop(0, 31, bit_step, thr0)

    need = kf - count(lambda k, c: k > thr)
    real = thr > INT_MIN
    ties = count(lambda k, c: (k == thr) & real)
    any_excess = jnp.max(ties - need) > 0.0

    @pl.when(jnp.logical_not(any_excess))
    def _():
        for kb in range(nkb):
            k = key_ref[:, kb * 256:(kb + 1) * 256]
            o_ref[0, kb] = jnp.where((k >= thr) & (k > INT_MIN), 1.0, 0.0)

    @pl.when(any_excess)
    def _():
        lane = lax.broadcasted_iota(jnp.int32, (tq, LANES), 1)

        def idx_step(it, jmax):
            cand = jmax | lax.shift_left(jnp.int32(1), 10 - it)
            below = count(lambda k, c: (k == thr) & real & ((c * LANES + lane) < cand))
            return jnp.where(below < need, cand, jmax)

        jmax = lax.fori_loop(0, 11, idx_step, zero)
        for kb in range(nkb):
            k = key_ref[:, kb * 256:(kb + 1) * 256]
            keep_tie = (k == thr) & real & ((kb * 256 + kcol) <= jmax)
            o_ref[0, kb] = jnp.where((k > thr) | keep_tie, 1.0, 0.0)


def _b_select(h, bsz, seq, tq=256):
    nt = seq // tq
    topk = min(TOPK_MAX, seq // 4)
    kern = functools.partial(_b_select_kernel, seq=seq, tq=tq, topk=topk)
    return pl.pallas_call(
        kern,
        grid=(bsz, nt),
        in_specs=[pl.BlockSpec((tq, GROUP_W), lambda b, t: (b * nt + t, BLK_B_IQ)),
                  pl.BlockSpec((tq, LANES), lambda b, t: (b * nt + t, BLK_B_SMALL * 4 + 3)),
                  pl.BlockSpec((seq, LANES), lambda b, t: (b, BLK_B_SMALL * 4 + 2))],
        out_specs=pl.BlockSpec((1, seq // 256, tq, 256), lambda b, t: (b * nt + t, 0, 0, 0)),
        out_shape=jax.ShapeDtypeStruct((bsz * nt, seq // 256, tq, 256), F32),
        scratch_shapes=[pltpu.VMEM((seq, LANES), BF16), pltpu.VMEM((tq, seq), jnp.int32)],
        compiler_params=_params(2),
    )(h, h, h)


def _b_attn_kernel(cfar_ref, q_ref, z_ref, k_ref, v_ref, msk_ref, qg_ref, kg_ref, base0_ref, base1_ref,
                   o_ref, kn_ref, vb_ref, d0_ref, d1_ref, qh_ref, m_ref, l_ref, acc_ref, *, seq, tq):
    b = pl.program_id(0)
    t_blk = pl.program_id(1)

    @pl.when((b == 0) & (t_blk == 0))
    def _():
        for h in range(N_HEADS64):
            d0_ref[h] = _toeplitz(base0_ref[h:h + 1, :], tq, 256)
            d1_ref[h] = _toeplitz(base1_ref[h:h + 1, :], tq, 256)

    @pl.when(t_blk == 0)
    def _():
        for r in range(seq // 256):
            rows = slice(r * 256, (r + 1) * 256)
            k = k_ref[rows, :]
            ms = jnp.mean(k * k, axis=-1, keepdims=True)
            kn_ref[rows, :] = (k * lax.rsqrt(ms + EPS) * kg_ref[...]).astype(BF16)
            vb_ref[rows, :] = v_ref[rows, :].astype(BF16)

    lo = _lo_mask(tq)
    qtiles = _rms_heads64(q_ref[...], qg_ref[...], 4)
    for h in range(N_HEADS64):
        sel = lo if h % 2 == 0 else jnp.logical_not(lo)
        qh_ref[h] = jnp.where(sel, qtiles[h // 2] * 0.125, 0.0).astype(BF16)
    m_ref[...] = jnp.full(m_ref.shape, NEG, F32)
    l_ref[...] = jnp.zeros(l_ref.shape, F32)
    acc_ref[...] = jnp.zeros(acc_ref.shape, F32)

    def update(kb, bias_of):
        off = pl.multiple_of(kb * 256, 256)
        kblk = kn_ref[pl.ds(off, 256), :]
        vblk = vb_ref[pl.ds(off, 256), :]
        keep = msk_ref[0, kb] > 0.5
        for h in range(N_HEADS64):
            s = jnp.where(keep, _dot_t(qh_ref[h], kblk) + bias_of(h), NEG)
            m_prev = m_ref[h][:, :1]
            m_new = jnp.maximum(m_prev, jnp.max(s, axis=-1, keepdims=True))
            alpha = jnp.exp(m_prev - m_new)
            p = jnp.where(keep, jnp.exp(s - m_new), 0.0)
            l_new = alpha * l_ref[h][:, :1] + jnp.sum(p, axis=-1, keepdims=True)
            acc_ref[h] = alpha * acc_ref[h] + jnp.dot(p.astype(BF16), vblk, preferred_element_type=F32)
            m_ref[h] = jnp.broadcast_to(m_new, (tq, LANES))
            l_ref[h] = jnp.broadcast_to(l_new, (tq, LANES))

    def far_body(kb, carry):
        update(kb, lambda h: cfar_ref[h])
        return carry

    lax.fori_loop(0, jnp.maximum(t_blk - 1, 0), far_body, 0)

    @pl.when(t_blk >= 1)
    def _():
        update(t_blk - 1, lambda h: d1_ref[h])

    update(t_blk, lambda h: d0_ref[h])

    gate = _silu(z_ref[...])
    for t in range(4):
        o_even = acc_ref[2 * t] / l_ref[2 * t][:, :1]
        o_odd = acc_ref[2 * t + 1] / l_ref[2 * t + 1][:, :1]
        cols = slice(t * LANES, (t + 1) * LANES)
        o_ref[:, cols] = (jnp.where(lo, o_even, o_odd) * gate[:, cols]).astype(BF16)


def _b_attn(h, mask, cfar, qg, kg, base0, base1, bsz, seq, tq=256):
    nt = seq // tq
    kern = functools.partial(_b_attn_kernel, seq=seq, tq=tq)
    full = lambda shape: pl.BlockSpec(shape, lambda b, t: (0,) * len(shape))
    return pl.pallas_call(
        kern,
        grid=(bsz, nt),
        in_specs=[pl.BlockSpec(memory_space=pltpu.SMEM),
                  pl.BlockSpec((tq, GROUP_W), lambda b, t: (b * nt + t, BLK_B_Q)),
                  pl.BlockSpec((tq, GROUP_W), lambda b, t: (b * nt + t, BLK_B_Z)),
                  pl.BlockSpec((seq, LANES), lambda b, t: (b, BLK_B_SMALL * 4)),
                  pl.BlockSpec((seq, LANES), lambda b, t: (b, BLK_B_SMALL * 4 + 1)),
                  pl.BlockSpec((1, seq // 256, tq, 256), lambda b, t: (b * nt + t, 0, 0, 0)),
                  full((1, GROUP_W)), full((1, LANES)), full((N_HEADS64, 512)), full((N_HEADS64, 512))],
        out_specs=pl.BlockSpec((tq, GROUP_W), lambda b, t: (b * nt + t, 0)),
        out_shape=jax.ShapeDtypeStruct((bsz * seq, GROUP_W), BF16),
        scratch_shapes=[pltpu.VMEM((seq, LANES), BF16), pltpu.VMEM((seq, LANES), BF16),
                        pltpu.VMEM((N_HEADS64, tq, 256), F32), pltpu.VMEM((N_HEADS64, tq, 256), F32),
                        pltpu.VMEM((N_HEADS64, tq, LANES), BF16),
                        pltpu.VMEM((N_HEADS64, tq, LANES), F32), pltpu.VMEM((N_HEADS64, tq, LANES), F32),
                        pltpu.VMEM((N_HEADS64, tq, LANES), F32)],
        compiler_params=_params(2),
    )(cfar, h, h, h, h, mask, qg, kg, base0, base1)


def _rope(tile, cos, sin):
    return tile * cos + pltpu.roll(tile, 64, 1) * sin


def _c_prep_kernel(cq_ref, sm_ref, wq_ref, wkv_ref, qag_ref, kvag_ref, qg_ref, kg_ref, cos_ref, sin_ref,
                   qo_ref, ko_ref, vo_ref):
    cq = cq_ref[...]
    ms = jnp.sum(cq * cq, axis=-1, keepdims=True) * (1.0 / Q_LORA)
    cqn = (cq * lax.rsqrt(ms + EPS) * qag_ref[...]).astype(BF16)
    qpre = jnp.dot(cqn, wq_ref[...], preferred_element_type=F32)
    ckv = sm_ref[:, 0:LANES]
    ms = jnp.mean(ckv * ckv, axis=-1, keepdims=True)
    ckvn = (ckv * lax.rsqrt(ms + EPS) * kvag_ref[...]).astype(BF16)
    kvpre = jnp.dot(ckvn, wkv_ref[...], preferred_element_type=F32)
    kr = sm_ref[:, LANES:2 * LANES]
    kr_ss = jnp.sum(kr * kr, axis=-1, keepdims=True)
    cos = cos_ref[...]
    sin = sin_ref[...]
    qg = qg_ref[...]
    kg = kg_ref[...]
    for h in range(C_HEADS):
        qh = qpre[:, h * 256:(h + 1) * 256]
        r = lax.rsqrt(jnp.sum(qh * qh, axis=-1, keepdims=True) * (1.0 / C_QK) + EPS)
        qn = qh * r * qg
        qo_ref[:, h * 256:h * 256 + LANES] = qn[:, :LANES].astype(BF16)
        qo_ref[:, h * 256 + LANES:(h + 1) * 256] = _rope(qn[:, LANES:], cos, sin).astype(BF16)
        kn = kvpre[:, h * LANES:(h + 1) * LANES]
        r = lax.rsqrt((jnp.sum(kn * kn, axis=-1, keepdims=True) + kr_ss) * (1.0 / C_QK) + EPS)
        ko_ref[:, h * 256:h * 256 + LANES] = (kn * r * kg[:, :LANES]).astype(BF16)
        ko_ref[:, h * 256 + LANES:(h + 1) * 256] = _rope(kr * r * kg[:, LANES:], cos, sin).astype(BF16)
    vo_ref[...] = kvpre[:, C_HEADS * LANES:].astype(BF16)


def _c_prep(h, wq, wkv, qag, kvag, qg, kg, cos, sin, seq, tm=512):
    n = h.shape[0]
    ns = seq // tm
    full = lambda shape: pl.BlockSpec(shape, lambda i: (0,) * len(shape))
    return pl.pallas_call(
        _c_prep_kernel,
        grid=(n // tm,),
        in_specs=[pl.BlockSpec((tm, GROUP_W), lambda i: (i, BLK_C_Q)),
                  pl.BlockSpec((tm, GROUP_W), lambda i: (i, BLK_C_SMALL)),
                  full((GROUP_W, 4 * 256)), full((KV_LORA, 8 * LANES)),
                  full((1, GROUP_W)), full((1, LANES)), full((1, 256)), full((1, 256)),
                  pl.BlockSpec((tm, LANES), lambda i: (i % ns, 0)),
                  pl.BlockSpec((tm, LANES), lambda i: (i % ns, 0))],
        out_specs=[pl.BlockSpec((tm, 4 * 256), lambda i: (i, 0)),
                   pl.BlockSpec((tm, 4 * 256), lambda i: (i, 0)),
                   pl.BlockSpec((tm, GROUP_W), lambda i: (i, 0))],
        out_shape=[jax.ShapeDtypeStruct((n, 4 * 256), BF16), jax.ShapeDtypeStruct((n, 4 * 256), BF16),
                   jax.ShapeDtypeStruct((n, GROUP_W), BF16)],
        compiler_params=_params(1),
    )(h, h, wq, wkv, qag, kvag, qg, kg, cos, sin)


def _c_attn_kernel(q_ref, k_ref, v_ref, z_ref, o_ref, *, tq):
    qt = pl.program_id(2)
    q = q_ref[...]
    scale = C_QK ** -0.5
    i = lax.broadcasted_iota(jnp.int32, (tq, tq), 0)
    j = lax.broadcasted_iota(jnp.int32, (tq, tq), 1)
    causal = (j // CHUNK) <= (i // CHUNK)

    def step(kb, carry, masked):
        m_prev, l_prev, acc = carry
        off = pl.multiple_of(kb * tq, tq)
        s = _dot_t(q, k_ref[pl.ds(off, tq), :]) * scale
        if masked:
            s = jnp.where(causal, s, NEG)
        m_new = jnp.maximum(m_prev, jnp.max(s, axis=-1, keepdims=True))
        alpha = jnp.exp(m_prev - m_new)
        p = jnp.exp(s - m_new)
        l_new = alpha * l_prev + jnp.sum(p, axis=-1, keepdims=True)
        acc = alpha * acc + jnp.dot(p.astype(BF16), v_ref[pl.ds(off, tq), :], preferred_element_type=F32)
        return m_new, l_new, acc

    init = (jnp.full((tq, 1), NEG, F32), jnp.zeros((tq, 1), F32), jnp.zeros((tq, LANES), F32))
    carry = lax.fori_loop(0, qt, lambda kb, c: step(kb, c, False), init)
    _, l_fin, acc = step(qt, carry, True)
    o_ref[...] = (acc / l_fin * _silu(z_ref[...])).astype(BF16)


def _c_attn(qc, kc, vc, h, bsz, seq, tq=256):
    nt = seq // tq
    kern = functools.partial(_c_attn_kernel, tq=tq)
    return pl.pallas_call(
        kern,
        grid=(bsz, C_HEADS, nt),
        in_specs=[pl.BlockSpec((tq, 256), lambda b, hh, t: (b * nt + t, hh)),
                  pl.BlockSpec((seq, 256), lambda b, hh, t: (b, hh)),
                  pl.BlockSpec((seq, LANES), lambda b, hh, t: (b, hh)),
                  pl.BlockSpec((tq, LANES), lambda b, hh, t: (b * nt + t, BLK_C_Z * 4 + hh))],
        out_specs=pl.BlockSpec((tq, LANES), lambda b, hh, t: (b * nt + t, hh)),
        out_shape=jax.ShapeDtypeStruct((bsz * seq, GROUP_W), BF16),
        compiler_params=_params(3),
    )(qc, kc, vc, h)


def _mixer_d_kernel(q_ref, k_ref, v_ref, z_ref, qg_ref, kg_ref, base_ref, o_ref,
                    kpad_ref, vpad_ref, bias_ref, *, seq, tq):
    b = pl.program_id(0)
    qt = pl.program_id(1)
    win = tq + D_LEFT

    @pl.when((b == 0) & (qt == 0))
    def _():
        for h in range(N_HEADS64):
            bias_ref[h] = _toeplitz(base_ref[h:h + 1, :], tq, win)

    @pl.when(qt == 0)
    def _():
        kpad_ref[0:D_LEFT, :] = jnp.zeros((D_LEFT, GROUP_W), BF16)
        vpad_ref[0:D_LEFT, :] = jnp.zeros((D_LEFT, GROUP_W), BF16)
        for r in range(seq // 256):
            rows = slice(r * 256, (r + 1) * 256)
            dst = slice(D_LEFT + r * 256, D_LEFT + (r + 1) * 256)
            tiles = _rms_heads64(k_ref[rows, :], kg_ref[...], 4)
            for t in range(4):
                kpad_ref[dst, t * LANES:(t + 1) * LANES] = tiles[t].astype(BF16)
            vpad_ref[dst, :] = v_ref[rows, :].astype(BF16)

    lo = _lo_mask(tq)
    qtiles = _rms_heads64(q_ref[...], qg_ref[...], 4)
    start = pl.multiple_of(qt * tq, tq)
    qc = lax.broadcasted_iota(jnp.int32, (tq, win), 0) // CHUNK
    j = lax.broadcasted_iota(jnp.int32, (tq, win), 1)
    kc = j // CHUNK
    valid = (kc >= qc) & (kc <= qc + D_LEFT // CHUNK) & (j + start >= D_LEFT)
    gate = _silu(z_ref[...])
    for t in range(4):
        cols = slice(t * LANES, (t + 1) * LANES)
        kwin = kpad_ref[pl.ds(start, win), cols]
        vwin = vpad_ref[pl.ds(start, win), cols]
        outs = []
        for half in range(2):
            sel = lo if half == 0 else jnp.logical_not(lo)
            qh = jnp.where(sel, qtiles[t] * 0.125, 0.0).astype(BF16)
            s = jnp.where(valid, _dot_t(qh, kwin) + bias_ref[2 * t + half], NEG)
            p = jnp.exp(s - jnp.max(s, axis=-1, keepdims=True))
            l = jnp.sum(p, axis=-1, keepdims=True)
            outs.append(jnp.dot(p.astype(BF16), vwin, preferred_element_type=F32) / l)
        o_ref[:, cols] = (jnp.where(lo, outs[0], outs[1]) * gate[:, cols]).astype(BF16)


def _mixer_d(h, qg, kg, base, bsz, seq, tq=256):
    nt = seq // tq
    kern = functools.partial(_mixer_d_kernel, seq=seq, tq=tq)
    full = lambda shape: pl.BlockSpec(shape, lambda b, t: (0,) * len(shape))
    return pl.pallas_call(
        kern,
        grid=(bsz, nt),
        in_specs=[pl.BlockSpec((tq, GROUP_W), lambda b, t: (b * nt + t, BLK_D_Q)),
                  pl.BlockSpec((seq, GROUP_W), lambda b, t: (b, BLK_D_K)),
                  pl.BlockSpec((seq, GROUP_W), lambda b, t: (b, BLK_D_V)),
                  pl.BlockSpec((tq, GROUP_W), lambda b, t: (b * nt + t, BLK_D_Z)),
                  full((1, GROUP_W)), full((1, GROUP_W)), full((N_HEADS64, 2 * tq + D_LEFT))],
        out_specs=pl.BlockSpec((tq, GROUP_W), lambda b, t: (b * nt + t, 0)),
        out_shape=jax.ShapeDtypeStruct((bsz * seq, GROUP_W), BF16),
        scratch_shapes=[pltpu.VMEM((seq + D_LEFT, GROUP_W), BF16), pltpu.VMEM((seq + D_LEFT, GROUP_W), BF16),
                        pltpu.VMEM((N_HEADS64, tq, tq + D_LEFT), F32)],
        compiler_params=_params(2),
    )(h, h, h, h, qg, kg, base)


def _zeros_cols(rows, n, dtype):
    return jnp.zeros((rows, n), dtype)


def _layout_w_in(w):
    w = w.astype(BF16)
    d = w.shape[0]
    c = lambda name, size, off=0: w[:, _SRC[name] + off:_SRC[name] + off + size]
    z = lambda n: _zeros_cols(d, n, BF16)
    pieces = [c("a_u", 512), c("a_v", 512), c("a_z", 512),
              c("b_q", 512), c("b_iq", 512), c("b_z", 512),
              c("b_k", 64), c("b_k", 64), c("b_v", 64), c("b_v", 64), c("b_ik", 64), c("b_ik", 64),
              c("b_iw", 8), z(120),
              c("c_q", 384), z(128),
              c("c_kv", 128), c("c_kr", 32), z(32), c("c_kr", 32, 32), z(32), z(256),
              c("c_z", 512),
              c("d_q", 512), c("d_k", 512), c("d_v", 512), c("d_z", 512)]
    return jnp.concatenate(pieces, axis=1)


def _rope_layout(v):
    z = jnp.zeros(v.shape[:-1] + (32,), v.dtype)
    return jnp.concatenate([v[..., :32], z, v[..., 32:], z], axis=-1)


def _layout_c(w_qb, w_kvb, q_gain, k_gain, qa_gain):
    wq = w_qb.reshape(Q_LORA, C_HEADS, C_QK)
    wq = jnp.concatenate([wq[..., :C_NOPE], _rope_layout(wq[..., C_NOPE:])], axis=-1)
    wq = wq.reshape(Q_LORA, C_HEADS * 256)
    wq = jnp.concatenate([wq, jnp.zeros((GROUP_W - Q_LORA, C_HEADS * 256), wq.dtype)], axis=0).astype(BF16)
    wkv = w_kvb.reshape(KV_LORA, C_HEADS, 2 * LANES)
    wkv = jnp.concatenate([wkv[..., :C_NOPE].reshape(KV_LORA, -1), wkv[..., C_NOPE:].reshape(KV_LORA, -1)],
                          axis=1).astype(BF16)
    lay = lambda g: jnp.concatenate([g[:C_NOPE], _rope_layout(g[C_NOPE:])])[None, :]
    qag = jnp.concatenate([qa_gain, jnp.zeros((GROUP_W - Q_LORA,), F32)])[None, :]
    return wq, wkv, lay(q_gain), lay(k_gain), qag


def _t5_bucket_static(rel):
    half = T5_BUCKETS // 2
    exact = half // 2
    n = abs(rel)
    if n < exact:
        val = n
    else:
        val = min(exact + (n * n // (exact * exact)).bit_length() - 1, half - 1)
    return (half if rel > 0 else 0) + val


def _t5_tables(t5_bias):
    m = np.arange(512)
    d0 = np.where(m < 256, m, m - 512)
    d1 = np.where(m <= 256, m - 256, m - 768)
    idx0 = np.array([_t5_bucket_static(int(d)) for d in d0], np.int32)
    idx1 = np.array([_t5_bucket_static(int(d)) for d in d1], np.int32)
    far = _t5_bucket_static(-512)
    return t5_bias[idx0].T, t5_bias[idx1].T, t5_bias[far]


def _band_table(rel_bias, tq):
    width = 2 * tq + D_LEFT
    m = np.arange(width)
    dist = np.where(m <= tq + D_LEFT, D_LEFT - m, D_LEFT + width - m)
    idx = np.clip(dist, -REL_CLIP, REL_CLIP) + REL_CLIP
    return rel_bias[idx.astype(np.int32)].T


def _rope_tables(seq):
    inv = ROPE_BASE ** (-jnp.arange(0, C_ROPE, 2, dtype=F32) / C_ROPE)
    ang = jnp.arange(seq, dtype=F32)[:, None] * inv[None, :]
    c, s = jnp.cos(ang), jnp.sin(ang)
    z = jnp.zeros_like(c)
    return jnp.concatenate([c, z, c, z], axis=1), jnp.concatenate([-s, z, s, z], axis=1)


def kernel(x, t5_bias, norm_g, w_in, a_v_gain, a_ws, a_bs, b_q_gain, b_k_gain, c_qa_gain, c_kva_gain,
           c_w_qb, c_w_kvb, c_q_gain, c_k_gain, d_q_gain, d_k_gain, d_rel_bias, w_out):
    bsz, seq, d_model = x.shape
    depth = w_in.shape[0]
    tq = 256
    x2 = x.reshape(bsz * seq, d_model)
    cos, sin = _rope_tables(seq)
    base0, base1, cfar = _t5_tables(t5_bias)
    for l in range(depth):
        h = _inproj(x2, norm_g[l][None, :], _layout_w_in(w_in[l]))
        y_a = _mixer_a(h, a_v_gain[l][None, :], a_ws[l], a_bs[l][:, :, None])
        mask = _b_select(h, bsz, seq, tq)
        y_b = _b_attn(h, mask, cfar, jnp.tile(b_q_gain[l], N_HEADS64)[None, :],
                      jnp.tile(b_k_gain[l], 2)[None, :], base0, base1, bsz, seq, tq)
        wq, wkv, qg, kg, qag = _layout_c(c_w_qb[l], c_w_kvb[l], c_q_gain[l], c_k_gain[l], c_qa_gain[l])
        qc, kc, vc = _c_prep(h, wq, wkv, qag, c_kva_gain[l][None, :], qg, kg, cos, sin, seq)
        y_c = _c_attn(qc, kc, vc, h, bsz, seq, tq)
        y_d = _mixer_d(h, jnp.tile(d_q_gain[l], N_HEADS64)[None, :], jnp.tile(d_k_gain[l], N_HEADS64)[None, :],
                       _band_table(d_rel_bias[l], tq), bsz, seq, tq)
        x2 = _outproj(x2, (y_a, y_b, y_c, y_d), w_out[l].astype(BF16))
    return x2.reshape(bsz, seq, d_model)
```

```python
import functools
import math

import numpy as np
import jax
import jax.numpy as jnp
from jax import lax
from jax.experimental import pallas as pl
from jax.experimental.pallas import tpu as pltpu

F32 = jnp.float32
BF16 = jnp.bfloat16

EPS = 1e-6
NEG = -1e30
INT_MIN = -(2 ** 31)
CHUNK = 64
LANES = 128
GROUP_W = 512
A_GROUPS = 4
GMLP_BLOCK = 128
N_HEADS64 = 8
IDX_SCALE = (8 ** -0.5) * 0.125
TOPK_MAX = 256
T5_BUCKETS = 32
C_HEADS = 4
C_NOPE = 128
C_ROPE = 64
C_QK = 192
Q_LORA = 384
KV_LORA = 128
ROPE_BASE = 10000.0
D_LEFT = 8 * CHUNK
REL_CLIP = 128
VMEM_LIMIT = 56 * 1024 * 1024

BLK_A_U, BLK_A_V, BLK_A_Z = 0, 1, 2
BLK_B_Q, BLK_B_IQ, BLK_B_Z, BLK_B_SMALL = 3, 4, 5, 6
BLK_C_Q, BLK_C_SMALL, BLK_C_Z = 7, 8, 9
BLK_D_Q, BLK_D_K, BLK_D_V, BLK_D_Z = 10, 11, 12, 13
H_COLS = 14 * GROUP_W

_SRC = dict(a_u=0, a_v=512, a_z=1024, b_q=1536, b_k=2048, b_v=2112, b_iq=2176, b_ik=2688,
            b_iw=2752, b_z=2760, c_q=3272, c_kv=3656, c_kr=3784, c_z=3848,
            d_q=4360, d_k=4872, d_v=5384, d_z=5896)


def _params(n_axes):
    return pltpu.CompilerParams(dimension_semantics=("arbitrary",) * n_axes,
                                vmem_limit_bytes=VMEM_LIMIT)


def _gelu(x):
    c = math.sqrt(2.0 / math.pi)
    return x * (0.5 * (1.0 + jnp.tanh(c * (x + 0.044715 * (x * x * x)))))


def _silu(x):
    return x * (1.0 / (1.0 + jnp.exp(-x)))


def _dot_t(a, b):
    return lax.dot_general(a, b, (((1,), (1,)), ((), ())), preferred_element_type=F32)


def _lo_mask(rows):
    return lax.broadcasted_iota(jnp.int32, (rows, LANES), 1) < 64


def _rms_heads64(x, gain, ntiles):
    lo = _lo_mask(x.shape[0])
    tiles = []
    for t in range(ntiles):
        xt = x[:, t * LANES:(t + 1) * LANES]
        sq = xt * xt
        s_lo = jnp.sum(jnp.where(lo, sq, 0.0), axis=-1, keepdims=True)
        s_hi = jnp.sum(jnp.where(lo, 0.0, sq), axis=-1, keepdims=True)
        r = jnp.where(lo, lax.rsqrt(s_lo * (1.0 / 64) + EPS), lax.rsqrt(s_hi * (1.0 / 64) + EPS))
        tiles.append(xt * r * gain[:, t * LANES:(t + 1) * LANES])
    return tiles


def _toeplitz(base_row, rows, width):
    t = jnp.broadcast_to(base_row, (rows, base_row.shape[1]))
    t = pltpu.roll(t, 0, 1, stride=1, stride_axis=0)
    return t[:, :width]


def _inproj_kernel(x_ref, g_ref, w_ref, o_ref, xn_ref):
    @pl.when(pl.program_id(1) == 0)
    def _():
        x = x_ref[...]
        ms = jnp.mean(x * x, axis=-1, keepdims=True)
        xn_ref[...] = (x * lax.rsqrt(ms + EPS) * g_ref[...]).astype(BF16)

    o_ref[...] = jnp.dot(xn_ref[...], w_ref[...], preferred_element_type=F32)


def _inproj(x2, g, w, tm=512, tn=1792):
    n, d = x2.shape
    return pl.pallas_call(
        _inproj_kernel,
        grid=(n // tm, H_COLS // tn),
        in_specs=[pl.BlockSpec((tm, d), lambda i, j: (i, 0)),
                  pl.BlockSpec((1, d), lambda i, j: (0, 0)),
                  pl.BlockSpec((d, tn), lambda i, j: (0, j))],
        out_specs=pl.BlockSpec((tm, tn), lambda i, j: (i, j)),
        out_shape=jax.ShapeDtypeStruct((n, H_COLS), F32),
        scratch_shapes=[pltpu.VMEM((tm, d), BF16)],
        compiler_params=_params(2),
    )(x2, g, w)


def _outproj_kernel(x_ref, ya_ref, yb_ref, yc_ref, yd_ref, w_ref, o_ref):
    acc = x_ref[...]
    for g, y_ref in enumerate((ya_ref, yb_ref, yc_ref, yd_ref)):
        acc = acc + jnp.dot(y_ref[...], w_ref[g * GROUP_W:(g + 1) * GROUP_W, :],
                            preferred_element_type=F32)
    o_ref[...] = acc


def _outproj(x2, ys, w, tm=512):
    n, d = x2.shape
    yspec = pl.BlockSpec((tm, GROUP_W), lambda i: (i, 0))
    return pl.pallas_call(
        _outproj_kernel,
        grid=(n // tm,),
        in_specs=[pl.BlockSpec((tm, d), lambda i: (i, 0)), yspec, yspec, yspec, yspec,
                  pl.BlockSpec((4 * GROUP_W, d), lambda i: (0, 0))],
        out_specs=pl.BlockSpec((tm, d), lambda i: (i, 0)),
        out_shape=jax.ShapeDtypeStruct((n, d), F32),
        compiler_params=_params(1),
    )(x2, *ys, w)


def _mixer_a_kernel(u_ref, v_ref, z_ref, vg_ref, w_ref, b_ref, o_ref):
    tm = u_ref.shape[0]
    u = _gelu(u_ref[...])
    v = _gelu(v_ref[...])
    ms = jnp.mean(v * v, axis=-1, keepdims=True)
    vb = (v * lax.rsqrt(ms + EPS) * vg_ref[...]).astype(BF16)
    gate = _silu(z_ref[...])
    i = lax.broadcasted_iota(jnp.int32, (GMLP_BLOCK, GMLP_BLOCK), 0)
    j = lax.broadcasted_iota(jnp.int32, (GMLP_BLOCK, GMLP_BLOCK), 1)
    keep = (j // CHUNK) <= (i // CHUNK)
    for g in range(A_GROUPS):
        wg = jnp.where(keep, w_ref[g], 0.0).astype(BF16)
        cols = slice(g * LANES, (g + 1) * LANES)
        for blk in range(tm // GMLP_BLOCK):
            rows = slice(blk * GMLP_BLOCK, (blk + 1) * GMLP_BLOCK)
            sg = jnp.dot(wg, vb[rows, cols], preferred_element_type=F32) + b_ref[g]
            o_ref[rows, cols] = (u[rows, cols] * sg * gate[rows, cols]).astype(BF16)


def _mixer_a(h, vg, ws, bs, tm=512):
    n = h.shape[0]
    hspec = lambda blk: pl.BlockSpec((tm, GROUP_W), lambda i, blk=blk: (i, blk))
    return pl.pallas_call(
        _mixer_a_kernel,
        grid=(n // tm,),
        in_specs=[hspec(BLK_A_U), hspec(BLK_A_V), hspec(BLK_A_Z),
                  pl.BlockSpec((1, GROUP_W), lambda i: (0, 0)),
                  pl.BlockSpec((A_GROUPS, GMLP_BLOCK, GMLP_BLOCK), lambda i: (0, 0, 0)),
                  pl.BlockSpec((A_GROUPS, GMLP_BLOCK, 1), lambda i: (0, 0, 0))],
        out_specs=pl.BlockSpec((tm, GROUP_W), lambda i: (i, 0)),
        out_shape=jax.ShapeDtypeStruct((n, GROUP_W), BF16),
        compiler_params=_params(1),
    )(h, h, h, vg, ws, bs)


def _b_select_kernel(iq_ref, iw_ref, ik_ref, o_ref, ikb_ref, key_ref, *, seq, tq, topk):
    t_blk = pl.program_id(1)
    nkb = seq // 256

    @pl.when(t_blk == 0)
    def _():
        ikb_ref[...] = ik_ref[...].astype(BF16)

    lo = _lo_mask(tq)
    w = iw_ref[...][:, :N_HEADS64] * IDX_SCALE
    lhs = []
    for h in range(N_HEADS64):
        iqt = iq_ref[:, (h // 2) * LANES:(h // 2 + 1) * LANES]
        lhs.append(jnp.where(lo if h % 2 == 0 else jnp.logical_not(lo), iqt, 0.0).astype(BF16))

    qchunk = (t_blk * tq + lax.broadcasted_iota(jnp.int32, (tq, 256), 0)) // CHUNK
    kcol = lax.broadcasted_iota(jnp.int32, (tq, 256), 1)
    for kb in range(nkb):
        ikblk = ikb_ref[kb * 256:(kb + 1) * 256, :]
        score = jnp.zeros((tq, 256), F32)
        for h in range(N_HEADS64):
            score = score + w[:, h:h + 1] * jnp.maximum(_dot_t(lhs[h], ikblk), 0.0)
        score = jnp.where(score == 0.0, 0.0, score)
        bits = lax.bitcast_convert_type(score, jnp.int32)
        key = jnp.where(bits < 0, bits ^ 0x7FFFFFFF, bits)
        adm = ((kb * 256 + kcol) // CHUNK) <= qchunk
        key_ref[:, kb * 256:(kb + 1) * 256] = jnp.where(adm, key, INT_MIN)

    def count(pred):
        acc = jnp.zeros((tq, LANES), F32)
        for c in range(seq // LANES):
            acc = acc + jnp.where(pred(key_ref[:, c * LANES:(c + 1) * LANES], c), 1.0, 0.0)
        return jnp.sum(acc, axis=1, keepdims=True)

    kf = float(topk)
    zero = jnp.zeros((tq, 1), jnp.int32)
    thr0 = jnp.where(count(lambda k, c: k >= zero) >= kf, zero, INT_MIN)

    def bit_step(it, thr):
        cand = thr | lax.shift_left(jnp.int32(1), 30 - it)
        return jnp.where(count(lambda k, c: k >= cand) >= kf, cand, thr)

    thr = lax.fori_loop(0, 31, bit_step, thr0)

    need = kf - count(lambda k, c: k > thr)
    real = thr > INT_MIN
    ties = count(lambda k, c: (k == thr) & real)
    any_excess = jnp.max(ties - need) > 0.0

    @pl.when(jnp.logical_not(any_excess))
    def _():
        for kb in range(nkb):
            k = key_ref[:, kb * 256:(kb + 1) * 256]
            o_ref[0, kb] = jnp.where((k >= thr) & (k > INT_MIN), 1.0, 0.0)

    @pl.when(any_excess)
    def _():
        lane = lax.broadcasted_iota(jnp.int32, (tq, LANES), 1)

        def idx_step(it, jmax):
            cand = jmax | lax.shift_left(jnp.int32(1), 10 - it)
            below = count(lambda k, c: (k == thr) & real & ((c * LANES + lane) < cand))
            return jnp.where(below < need, cand, jmax)

        jmax = lax.fori_loop(0, 11, idx_step, zero)
        for kb in range(nkb):
            k = key_ref[:, kb * 256:(kb + 1) * 256]
            keep_tie = (k == thr) & real & ((kb * 256 + kcol) <= jmax)
            o_ref[0, kb] = jnp.where((k > thr) | keep_tie, 1.0, 0.0)


def _b_select(h, bsz, seq, tq=256):
    nt = seq // tq
    topk = min(TOPK_MAX, seq // 4)
    kern = functools.partial(_b_select_kernel, seq=seq, tq=tq, topk=topk)
    return pl.pallas_call(
        kern,
        grid=(bsz, nt),
        in_specs=[pl.BlockSpec((tq, GROUP_W), lambda b, t: (b * nt + t, BLK_B_IQ)),
                  pl.BlockSpec((tq, LANES), lambda b, t: (b * nt + t, BLK_B_SMALL * 4 + 3)),
                  pl.BlockSpec((seq, LANES), lambda b, t: (b, BLK_B_SMALL * 4 + 2))],
        out_specs=pl.BlockSpec((1, seq // 256, tq, 256), lambda b, t: (b * nt + t, 0, 0, 0)),
        out_shape=jax.ShapeDtypeStruct((bsz * nt, seq // 256, tq, 256), F32),
        scratch_shapes=[pltpu.VMEM((seq, LANES), BF16), pltpu.VMEM((tq, seq), jnp.int32)],
        compiler_params=_params(2),
    )(h, h, h)


def _b_attn_kernel(cfar_ref, q_ref, z_ref, k_ref, v_ref, msk_ref, qg_ref, kg_ref, base0_ref, base1_ref,
                   o_ref, kn_ref, v1_ref, bias_ref, qall_ref, s_ref, mp_ref, acc_ref, *, seq, tq):
    b = pl.program_id(0)
    t_blk = pl.program_id(1)
    hrows = [slice(h * tq, (h + 1) * tq) for h in range(N_HEADS64)]

    @pl.when((b == 0) & (t_blk == 0))
    def _():
        for h in range(N_HEADS64):
            bias_ref[0, hrows[h], :] = jnp.full((tq, 256), cfar_ref[h], F32)
            bias_ref[1, hrows[h], :] = _toeplitz(base1_ref[h:h + 1, :], tq, 256)
            bias_ref[2, hrows[h], :] = _toeplitz(base0_ref[h:h + 1, :], tq, 256)

    @pl.when(t_blk == 0)
    def _():
        lo256 = _lo_mask(256)
        for r in range(seq // 256):
            rows = slice(r * 256, (r + 1) * 256)
            k = k_ref[rows, :]
            ms = jnp.mean(k * k, axis=-1, keepdims=True)
            kn_ref[rows, :] = (k * lax.rsqrt(ms + EPS) * kg_ref[...]).astype(BF16)
            v1_ref[rows, :] = jnp.where(lo256, v_ref[rows, :], 1.0).astype(BF16)

    lo = _lo_mask(tq)
    qtiles = _rms_heads64(q_ref[...], qg_ref[...], 4)
    for h in range(N_HEADS64):
        sel = lo if h % 2 == 0 else jnp.logical_not(lo)
        qall_ref[hrows[h], :] = jnp.where(sel, qtiles[h // 2] * 0.125, 0.0).astype(BF16)
    mp_ref[...] = jnp.full(mp_ref.shape, NEG, F32)
    acc_ref[...] = jnp.zeros(acc_ref.shape, F32)
    nblk = t_blk + 1

    def logits_pass(kb, carry):
        off = pl.multiple_of(kb * 256, 256)
        kblk = kn_ref[pl.ds(off, 256), :]
        which = jnp.clip(kb - (t_blk - 2), 0, 2)
        keep = msk_ref[0, kb] > 0.5
        for h in range(N_HEADS64):
            s = jnp.where(keep, _dot_t(qall_ref[hrows[h], :], kblk) + bias_ref[which, hrows[h], :], NEG)
            s_ref[kb, hrows[h], :] = s
            mp_ref[hrows[h], :] = jnp.maximum(mp_ref[hrows[h], :], jnp.maximum(s[:, :LANES], s[:, LANES:]))
        return carry

    lax.fori_loop(0, nblk, logits_pass, 0)
    for h in range(N_HEADS64):
        m = jnp.max(mp_ref[hrows[h], :], axis=-1, keepdims=True)
        mp_ref[hrows[h], :] = jnp.broadcast_to(m, (tq, LANES))

    def value_pass(kb, carry):
        off = pl.multiple_of(kb * 256, 256)
        v1 = v1_ref[pl.ds(off, 256), :]
        for h in range(N_HEADS64):
            m = mp_ref[hrows[h], :]
            p = jnp.exp(s_ref[kb, hrows[h], :] - jnp.concatenate([m, m], axis=1)).astype(BF16)
            acc_ref[hrows[h], :] += jnp.dot(p, v1, preferred_element_type=F32)
        return carry

    lax.fori_loop(0, nblk, value_pass, 0)

    gate = _silu(z_ref[...])
    for t in range(4):
        a_even = acc_ref[hrows[2 * t], :]
        a_odd = acc_ref[hrows[2 * t + 1], :]
        o_even = a_even / pltpu.roll(a_even, 64, 1)
        o_odd = pltpu.roll(a_odd, 64, 1) / a_odd
        cols = slice(t * LANES, (t + 1) * LANES)
        o_ref[:, cols] = (jnp.where(lo, o_even, o_odd) * gate[:, cols]).astype(BF16)


def _b_attn(h, mask, cfar, qg, kg, base0, base1, bsz, seq, tq=256):
    nt = seq // tq
    rows = N_HEADS64 * tq
    kern = functools.partial(_b_attn_kernel, seq=seq, tq=tq)
    full = lambda shape: pl.BlockSpec(shape, lambda b, t: (0,) * len(shape))
    return pl.pallas_call(
        kern,
        grid=(bsz, nt),
        in_specs=[pl.BlockSpec(memory_space=pltpu.SMEM),
                  pl.BlockSpec((tq, GROUP_W), lambda b, t: (b * nt + t, BLK_B_Q)),
                  pl.BlockSpec((tq, GROUP_W), lambda b, t: (b * nt + t, BLK_B_Z)),
                  pl.BlockSpec((seq, LANES), lambda b, t: (b, BLK_B_SMALL * 4)),
                  pl.BlockSpec((seq, LANES), lambda b, t: (b, BLK_B_SMALL * 4 + 1)),
                  pl.BlockSpec((1, seq // 256, tq, 256), lambda b, t: (b * nt + t, 0, 0, 0)),
                  full((1, GROUP_W)), full((1, LANES)), full((N_HEADS64, 512)), full((N_HEADS64, 512))],
        out_specs=pl.BlockSpec((tq, GROUP_W), lambda b, t: (b * nt + t, 0)),
        out_shape=jax.ShapeDtypeStruct((bsz * seq, GROUP_W), BF16),
        scratch_shapes=[pltpu.VMEM((seq, LANES), BF16), pltpu.VMEM((seq, LANES), BF16),
                        pltpu.VMEM((3, rows, 256), F32), pltpu.VMEM((rows, LANES), BF16),
                        pltpu.VMEM((seq // 256, rows, 256), F32),
                        pltpu.VMEM((rows, LANES), F32), pltpu.VMEM((rows, LANES), F32)],
        compiler_params=_params(2),
    )(cfar, h, h, h, h, mask, qg, kg, base0, base1)


def _rope(tile, cos, sin):
    return tile * cos + pltpu.roll(tile, 64, 1) * sin


def _c_prep_kernel(cq_ref, sm_ref, wq_ref, wkv_ref, qag_ref, kvag_ref, qg_ref, kg_ref, cos_ref, sin_ref,
                   qo_ref, ko_ref, vo_ref):
    cq = cq_ref[...]
    ms = jnp.sum(cq * cq, axis=-1, keepdims=True) * (1.0 / Q_LORA)
    cqn = (cq * lax.rsqrt(ms + EPS) * qag_ref[...]).astype(BF16)
    qpre = jnp.dot(cqn, wq_ref[...], preferred_element_type=F32)
    ckv = sm_ref[:, 0:LANES]
    ms = jnp.mean(ckv * ckv, axis=-1, keepdims=True)
    ckvn = (ckv * lax.rsqrt(ms + EPS) * kvag_ref[...]).astype(BF16)
    kvpre = jnp.dot(ckvn, wkv_ref[...], preferred_element_type=F32)
    kr = sm_ref[:, LANES:2 * LANES]
    kr_ss = jnp.sum(kr * kr, axis=-1, keepdims=True)
    cos = cos_ref[...]
    sin = sin_ref[...]
    qg = qg_ref[...]
    kg = kg_ref[...]
    for h in range(C_HEADS):
        qh = qpre[:, h * 256:(h + 1) * 256]
        r = lax.rsqrt(jnp.sum(qh * qh, axis=-1, keepdims=True) * (1.0 / C_QK) + EPS)
        qn = qh * r * qg
        qo_ref[:, h * 256:h * 256 + LANES] = qn[:, :LANES].astype(BF16)
        qo_ref[:, h * 256 + LANES:(h + 1) * 256] = _rope(qn[:, LANES:], cos, sin).astype(BF16)
        kn = kvpre[:, h * LANES:(h + 1) * LANES]
        r = lax.rsqrt((jnp.sum(kn * kn, axis=-1, keepdims=True) + kr_ss) * (1.0 / C_QK) + EPS)
        ko_ref[:, h * 256:h * 256 + LANES] = (kn * r * kg[:, :LANES]).astype(BF16)
        ko_ref[:, h * 256 + LANES:(h + 1) * 256] = _rope(kr * r * kg[:, LANES:], cos, sin).astype(BF16)
    vo_ref[...] = kvpre[:, C_HEADS * LANES:].astype(BF16)


def _c_prep(h, wq, wkv, qag, kvag, qg, kg, cos, sin, seq, tm=512):
    n = h.shape[0]
    ns = seq // tm
    full = lambda shape: pl.BlockSpec(shape, lambda i: (0,) * len(shape))
    return pl.pallas_call(
        _c_prep_kernel,
        grid=(n // tm,),
        in_specs=[pl.BlockSpec((tm, GROUP_W), lambda i: (i, BLK_C_Q)),
                  pl.BlockSpec((tm, GROUP_W), lambda i: (i, BLK_C_SMALL)),
                  full((GROUP_W, 4 * 256)), full((KV_LORA, 8 * LANES)),
                  full((1, GROUP_W)), full((1, LANES)), full((1, 256)), full((1, 256)),
                  pl.BlockSpec((tm, LANES), lambda i: (i % ns, 0)),
                  pl.BlockSpec((tm, LANES), lambda i: (i % ns, 0))],
        out_specs=[pl.BlockSpec((tm, 4 * 256), lambda i: (i, 0)),
                   pl.BlockSpec((tm, 4 * 256), lambda i: (i, 0)),
                   pl.BlockSpec((tm, GROUP_W), lambda i: (i, 0))],
        out_shape=[jax.ShapeDtypeStruct((n, 4 * 256), BF16), jax.ShapeDtypeStruct((n, 4 * 256), BF16),
                   jax.ShapeDtypeStruct((n, GROUP_W), BF16)],
        compiler_params=_params(1),
    )(h, h, wq, wkv, qag, kvag, qg, kg, cos, sin)


def _c_attn_kernel(q_ref, k_ref, v_ref, z_ref, o_ref, s_ref, mp_ref, lp_ref, acc_ref, *, tq):
    qt = pl.program_id(1)
    scale = C_QK ** -0.5
    hrows = [slice(h * tq, (h + 1) * tq) for h in range(C_HEADS)]
    qchunk = (qt * tq + lax.broadcasted_iota(jnp.int32, (tq, 256), 0)) // CHUNK
    kcol = lax.broadcasted_iota(jnp.int32, (tq, 256), 1)
    mp_ref[...] = jnp.full(mp_ref.shape, NEG, F32)
    lp_ref[...] = jnp.zeros(lp_ref.shape, F32)
    acc_ref[...] = jnp.zeros(acc_ref.shape, F32)
    nblk = (qt + 1) * (tq // 256)

    def logits_pass(kb, carry):
        off = pl.multiple_of(kb * 256, 256)
        keep = ((kb * 256 + kcol) // CHUNK) <= qchunk
        for h in range(C_HEADS):
            cols = slice(h * 256, (h + 1) * 256)
            s = jnp.where(keep, _dot_t(q_ref[:, cols], k_ref[pl.ds(off, 256), cols]) * scale, NEG)
            s_ref[kb, hrows[h], :] = s
            mp_ref[hrows[h], :] = jnp.maximum(mp_ref[hrows[h], :], jnp.maximum(s[:, :LANES], s[:, LANES:]))
        return carry

    lax.fori_loop(0, nblk, logits_pass, 0)
    for h in range(C_HEADS):
        m = jnp.max(mp_ref[hrows[h], :], axis=-1, keepdims=True)
        mp_ref[hrows[h], :] = jnp.broadcast_to(m, (tq, LANES))

    def value_pass(kb, carry):
        off = pl.multiple_of(kb * 256, 256)
        for h in range(C_HEADS):
            m = mp_ref[hrows[h], :]
            p = jnp.exp(s_ref[kb, hrows[h], :] - jnp.concatenate([m, m], axis=1))
            lp_ref[hrows[h], :] += p[:, :LANES] + p[:, LANES:]
            acc_ref[hrows[h], :] += jnp.dot(p.astype(BF16), v_ref[pl.ds(off, 256), h * LANES:(h + 1) * LANES],
                                            preferred_element_type=F32)
        return carry

    lax.fori_loop(0, nblk, value_pass, 0)
    gate = _silu(z_ref[...])
    for h in range(C_HEADS):
        cols = slice(h * LANES, (h + 1) * LANES)
        l = jnp.sum(lp_ref[hrows[h], :], axis=-1, keepdims=True)
        o_ref[:, cols] = (acc_ref[hrows[h], :] / l * gate[:, cols]).astype(BF16)


def _c_attn(qc, kc, vc, h, bsz, seq, tq=256):
    nt = seq // tq
    rows = C_HEADS * tq
    kern = functools.partial(_c_attn_kernel, tq=tq)
    return pl.pallas_call(
        kern,
        grid=(bsz, nt),
        in_specs=[pl.BlockSpec((tq, C_HEADS * 256), lambda b, t: (b * nt + t, 0)),
                  pl.BlockSpec((seq, C_HEADS * 256), lambda b, t: (b, 0)),
                  pl.BlockSpec((seq, GROUP_W), lambda b, t: (b, 0)),
                  pl.BlockSpec((tq, GROUP_W), lambda b, t: (b * nt + t, BLK_C_Z))],
        out_specs=pl.BlockSpec((tq, GROUP_W), lambda b, t: (b * nt + t, 0)),
        out_shape=jax.ShapeDtypeStruct((bsz * seq, GROUP_W), BF16),
        scratch_shapes=[pltpu.VMEM((seq // 256, rows, 256), F32), pltpu.VMEM((rows, LANES), F32),
                        pltpu.VMEM((rows, LANES), F32), pltpu.VMEM((rows, LANES), F32)],
        compiler_params=_params(2),
    )(qc, kc, vc, h)


def _mixer_d_kernel(q_ref, k_ref, v_ref, z_ref, qg_ref, kg_ref, base_ref, o_ref,
                    kpad_ref, vpad_ref, bias_ref, *, seq, tq):
    b = pl.program_id(0)
    qt = pl.program_id(1)
    win = tq + D_LEFT

    @pl.when((b == 0) & (qt == 0))
    def _():
        for h in range(N_HEADS64):
            bias_ref[h] = _toeplitz(base_ref[h:h + 1, :], tq, win)

    @pl.when(qt == 0)
    def _():
        kpad_ref[0:D_LEFT, :] = jnp.zeros((D_LEFT, GROUP_W), BF16)
        vpad_ref[0:D_LEFT, :] = jnp.zeros((D_LEFT, GROUP_W), BF16)
        for r in range(seq // 256):
            rows = slice(r * 256, (r + 1) * 256)
            dst = slice(D_LEFT + r * 256, D_LEFT + (r + 1) * 256)
            tiles = _rms_heads64(k_ref[rows, :], kg_ref[...], 4)
            for t in range(4):
                kpad_ref[dst, t * LANES:(t + 1) * LANES] = tiles[t].astype(BF16)
            vpad_ref[dst, :] = v_ref[rows, :].astype(BF16)

    lo = _lo_mask(tq)
    qtiles = _rms_heads64(q_ref[...], qg_ref[...], 4)
    start = pl.multiple_of(qt * tq, tq)
    qc = lax.broadcasted_iota(jnp.int32, (tq, win), 0) // CHUNK
    j = lax.broadcasted_iota(jnp.int32, (tq, win), 1)
    kc = j // CHUNK
    valid = (kc >= qc) & (kc <= qc + D_LEFT // CHUNK) & (j + start >= D_LEFT)
    gate = _silu(z_ref[...])
    for t in range(4):
        cols = slice(t * LANES, (t + 1) * LANES)
        kwin = kpad_ref[pl.ds(start, win), cols]
        vwin = vpad_ref[pl.ds(start, win), cols]
        outs = []
        for half in range(2):
            sel = lo if half == 0 else jnp.logical_not(lo)
            qh = jnp.where(sel, qtiles[t] * 0.125, 0.0).astype(BF16)
            s = jnp.where(valid, _dot_t(qh, kwin) + bias_ref[2 * t + half], NEG)
            p = jnp.exp(s - jnp.max(s, axis=-1, keepdims=True))
            l = jnp.sum(p, axis=-1, keepdims=True)
            outs.append(jnp.dot(p.astype(BF16), vwin, preferred_element_type=F32) / l)
        o_ref[:, cols] = (jnp.where(lo, outs[0], outs[1]) * gate[:, cols]).astype(BF16)


def _mixer_d(h, qg, kg, base, bsz, seq, tq=256):
    nt = seq // tq
    kern = functools.partial(_mixer_d_kernel, seq=seq, tq=tq)
    full = lambda shape: pl.BlockSpec(shape, lambda b, t: (0,) * len(shape))
    return pl.pallas_call(
        kern,
        grid=(bsz, nt),
        in_specs=[pl.BlockSpec((tq, GROUP_W), lambda b, t: (b * nt + t, BLK_D_Q)),
                  pl.BlockSpec((seq, GROUP_W), lambda b, t: (b, BLK_D_K)),
                  pl.BlockSpec((seq, GROUP_W), lambda b, t: (b, BLK_D_V)),
                  pl.BlockSpec((tq, GROUP_W), lambda b, t: (b * nt + t, BLK_D_Z)),
                  full((1, GROUP_W)), full((1, GROUP_W)), full((N_HEADS64, 2 * tq + D_LEFT))],
        out_specs=pl.BlockSpec((tq, GROUP_W), lambda b, t: (b * nt + t, 0)),
        out_shape=jax.ShapeDtypeStruct((bsz * seq, GROUP_W), BF16),
        scratch_shapes=[pltpu.VMEM((seq + D_LEFT, GROUP_W), BF16), pltpu.VMEM((seq + D_LEFT, GROUP_W), BF16),
                        pltpu.VMEM((N_HEADS64, tq, tq + D_LEFT), F32)],
        compiler_params=_params(2),
    )(h, h, h, h, qg, kg, base)


def _zeros_cols(rows, n, dtype):
    return jnp.zeros((rows, n), dtype)


def _layout_w_in(w):
    w = w.astype(BF16)
    d = w.shape[0]
    c = lambda name, size, off=0: w[:, _SRC[name] + off:_SRC[name] + off + size]
    z = lambda n: _zeros_cols(d, n, BF16)
    pieces = [c("a_u", 512), c("a_v", 512), c("a_z", 512),
              c("b_q", 512), c("b_iq", 512), c("b_z", 512),
              c("b_k", 64), c("b_k", 64), c("b_v", 64), c("b_v", 64), c("b_ik", 64), c("b_ik", 64),
              c("b_iw", 8), z(120),
              c("c_q", 384), z(128),
              c("c_kv", 128), c("c_kr", 32), z(32), c("c_kr", 32, 32), z(32), z(256),
              c("c_z", 512),
              c("d_q", 512), c("d_k", 512), c("d_v", 512), c("d_z", 512)]
    return jnp.concatenate(pieces, axis=1)


def _rope_layout(v):
    z = jnp.zeros(v.shape[:-1] + (32,), v.dtype)
    return jnp.concatenate([v[..., :32], z, v[..., 32:], z], axis=-1)


def _layout_c(w_qb, w_kvb, q_gain, k_gain, qa_gain):
    wq = w_qb.reshape(Q_LORA, C_HEADS, C_QK)
    wq = jnp.concatenate([wq[..., :C_NOPE], _rope_layout(wq[..., C_NOPE:])], axis=-1)
    wq = wq.reshape(Q_LORA, C_HEADS * 256)
    wq = jnp.concatenate([wq, jnp.zeros((GROUP_W - Q_LORA, C_HEADS * 256), wq.dtype)], axis=0).astype(BF16)
    wkv = w_kvb.reshape(KV_LORA, C_HEADS, 2 * LANES)
    wkv = jnp.concatenate([wkv[..., :C_NOPE].reshape(KV_LORA, -1), wkv[..., C_NOPE:].reshape(KV_LORA, -1)],
                          axis=1).astype(BF16)
    lay = lambda g: jnp.concatenate([g[:C_NOPE], _rope_layout(g[C_NOPE:])])[None, :]
    qag = jnp.concatenate([qa_gain, jnp.zeros((GROUP_W - Q_LORA,), F32)])[None, :]
    return wq, wkv, lay(q_gain), lay(k_gain), qag


def _t5_bucket_static(rel):
    half = T5_BUCKETS // 2
    exact = half // 2
    n = abs(rel)
    if n < exact:
        val = n
    else:
        val = min(exact + (n * n // (exact * exact)).bit_length() - 1, half - 1)
    return (half if rel > 0 else 0) + val


def _t5_tables(t5_bias):
    m = np.arange(512)
    d0 = np.where(m < 256, m, m - 512)
    d1 = np.where(m <= 256, m - 256, m - 768)
    idx0 = np.array([_t5_bucket_static(int(d)) for d in d0], np.int32)
    idx1 = np.array([_t5_bucket_static(int(d)) for d in d1], np.int32)
    far = _t5_bucket_static(-512)
    return t5_bias[idx0].T, t5_bias[idx1].T, t5_bias[far]


def _band_table(rel_bias, tq):
    width = 2 * tq + D_LEFT
    m = np.arange(width)
    dist = np.where(m <= tq + D_LEFT, D_LEFT - m, D_LEFT + width - m)
    idx = np.clip(dist, -REL_CLIP, REL_CLIP) + REL_CLIP
    return rel_bias[idx.astype(np.int32)].T


def _rope_tables(seq):
    inv = ROPE_BASE ** (-jnp.arange(0, C_ROPE, 2, dtype=F32) / C_ROPE)
    ang = jnp.arange(seq, dtype=F32)[:, None] * inv[None, :]
    c, s = jnp.cos(ang), jnp.sin(ang)
    z = jnp.zeros_like(c)
    return jnp.concatenate([c, z, c, z], axis=1), jnp.concatenate([-s, z, s, z], axis=1)


def kernel(x, t5_bias, norm_g, w_in, a_v_gain, a_ws, a_bs, b_q_gain, b_k_gain, c_qa_gain, c_kva_gain,
           c_w_qb, c_w_kvb, c_q_gain, c_k_gain, d_q_gain, d_k_gain, d_rel_bias, w_out):
    bsz, seq, d_model = x.shape
    depth = w_in.shape[0]
    tq = 256
    x2 = x.reshape(bsz * seq, d_model)
    cos, sin = _rope_tables(seq)
    base0, base1, cfar = _t5_tables(t5_bias)
    for l in range(depth):
        h = _inproj(x2, norm_g[l][None, :], _layout_w_in(w_in[l]))
        y_a = _mixer_a(h, a_v_gain[l][None, :], a_ws[l], a_bs[l][:, :, None])
        mask = _b_select(h, bsz, seq, tq)
        y_b = _b_attn(h, mask, cfar, jnp.tile(b_q_gain[l], N_HEADS64)[None, :],
                      jnp.tile(b_k_gain[l], 2)[None, :], base0, base1, bsz, seq, tq)
        wq, wkv, qg, kg, qag = _layout_c(c_w_qb[l], c_w_kvb[l], c_q_gain[l], c_k_gain[l], c_qa_gain[l])
        qc, kc, vc = _c_prep(h, wq, wkv, qag, c_kva_gain[l][None, :], qg, kg, cos, sin, seq)
        y_c = _c_attn(qc, kc, vc, h, bsz, seq, tq)
        y_d = _mixer_d(h, jnp.tile(d_q_gain[l], N_HEADS64)[None, :], jnp.tile(d_k_gain[l], N_HEADS64)[None, :],
                       _band_table(d_rel_bias[l], tq), bsz, seq, tq)
        x2 = _outproj(x2, (y_a, y_b, y_c, y_d), w_out[l].astype(BF16))
    return x2.reshape(bsz, seq, d_model)
```

```python
import functools
import math

import numpy as np
import jax
import jax.numpy as jnp
from jax import lax
from jax.experimental import pallas as pl
from jax.experimental.pallas import tpu as pltpu

F32 = jnp.float32
BF16 = jnp.bfloat16

EPS = 1e-6
NEG = -1e30
INT_MIN = -(2 ** 31)
CHUNK = 64
LANES = 128
GROUP_W = 512
A_GROUPS = 4
GMLP_BLOCK = 128
N_HEADS64 = 8
IDX_SCALE = (8 ** -0.5) * 0.125
TOPK_MAX = 256
T5_BUCKETS = 32
C_HEADS = 4
C_NOPE = 128
C_ROPE = 64
C_QK = 192
Q_LORA = 384
KV_LORA = 128
ROPE_BASE = 10000.0
D_LEFT = 8 * CHUNK
REL_CLIP = 128
VMEM_LIMIT = 56 * 1024 * 1024

BLK_A_U, BLK_A_V, BLK_A_Z = 0, 1, 2
BLK_B_Q, BLK_B_IQ, BLK_B_Z, BLK_B_SMALL = 3, 4, 5, 6
BLK_C_Q, BLK_C_SMALL, BLK_C_Z = 7, 8, 9
BLK_D_Q, BLK_D_K, BLK_D_V, BLK_D_Z = 10, 11, 12, 13
H_COLS = 14 * GROUP_W

_SRC = dict(a_u=0, a_v=512, a_z=1024, b_q=1536, b_k=2048, b_v=2112, b_iq=2176, b_ik=2688,
            b_iw=2752, b_z=2760, c_q=3272, c_kv=3656, c_kr=3784, c_z=3848,
            d_q=4360, d_k=4872, d_v=5384, d_z=5896)


def _params(n_axes):
    return pltpu.CompilerParams(dimension_semantics=("arbitrary",) * n_axes,
                                vmem_limit_bytes=VMEM_LIMIT)


def _gelu(x):
    c = math.sqrt(2.0 / math.pi)
    return x * (0.5 * (1.0 + jnp.tanh(c * (x + 0.044715 * (x * x * x)))))


def _silu(x):
    return x * (1.0 / (1.0 + jnp.exp(-x)))


def _dot_t(a, b):
    return lax.dot_general(a, b, (((1,), (1,)), ((), ())), preferred_element_type=F32)


def _lo_mask(rows):
    return lax.broadcasted_iota(jnp.int32, (rows, LANES), 1) < 64


def _rms_heads64(x, gain, ntiles):
    lo = _lo_mask(x.shape[0])
    tiles = []
    for t in range(ntiles):
        xt = x[:, t * LANES:(t + 1) * LANES]
        sq = xt * xt
        s_lo = jnp.sum(jnp.where(lo, sq, 0.0), axis=-1, keepdims=True)
        s_hi = jnp.sum(jnp.where(lo, 0.0, sq), axis=-1, keepdims=True)
        r = jnp.where(lo, lax.rsqrt(s_lo * (1.0 / 64) + EPS), lax.rsqrt(s_hi * (1.0 / 64) + EPS))
        tiles.append(xt * r * gain[:, t * LANES:(t + 1) * LANES])
    return tiles


def _toeplitz(base_row, rows, width):
    t = jnp.broadcast_to(base_row, (rows, base_row.shape[1]))
    t = pltpu.roll(t, 0, 1, stride=1, stride_axis=0)
    return t[:, :width]


def _inproj_kernel(x_ref, g_ref, w_ref, o_ref, xn_ref):
    @pl.when(pl.program_id(1) == 0)
    def _():
        x = x_ref[...]
        ms = jnp.mean(x * x, axis=-1, keepdims=True)
        xn_ref[...] = (x * lax.rsqrt(ms + EPS) * g_ref[...]).astype(BF16)

    o_ref[...] = jnp.dot(xn_ref[...], w_ref[...], preferred_element_type=F32)


def _inproj(x2, g, w, tm=512, tn=1792):
    n, d = x2.shape
    return pl.pallas_call(
        _inproj_kernel,
        grid=(n // tm, H_COLS // tn),
        in_specs=[pl.BlockSpec((tm, d), lambda i, j: (i, 0)),
                  pl.BlockSpec((1, d), lambda i, j: (0, 0)),
                  pl.BlockSpec((d, tn), lambda i, j: (0, j))],
        out_specs=pl.BlockSpec((tm, tn), lambda i, j: (i, j)),
        out_shape=jax.ShapeDtypeStruct((n, H_COLS), F32),
        scratch_shapes=[pltpu.VMEM((tm, d), BF16)],
        compiler_params=_params(2),
    )(x2, g, w)


def _outproj_kernel(x_ref, ya_ref, yb_ref, yc_ref, yd_ref, w_ref, o_ref):
    acc = x_ref[...]
    for g, y_ref in enumerate((ya_ref, yb_ref, yc_ref, yd_ref)):
        acc = acc + jnp.dot(y_ref[...], w_ref[g * GROUP_W:(g + 1) * GROUP_W, :],
                            preferred_element_type=F32)
    o_ref[...] = acc


def _outproj(x2, ys, w, tm=512):
    n, d = x2.shape
    yspec = pl.BlockSpec((tm, GROUP_W), lambda i: (i, 0))
    return pl.pallas_call(
        _outproj_kernel,
        grid=(n // tm,),
        in_specs=[pl.BlockSpec((tm, d), lambda i: (i, 0)), yspec, yspec, yspec, yspec,
                  pl.BlockSpec((4 * GROUP_W, d), lambda i: (0, 0))],
        out_specs=pl.BlockSpec((tm, d), lambda i: (i, 0)),
        out_shape=jax.ShapeDtypeStruct((n, d), F32),
        compiler_params=_params(1),
    )(x2, *ys, w)


def _mixer_a_kernel(u_ref, v_ref, z_ref, vg_ref, w_ref, b_ref, o_ref):
    tm = u_ref.shape[0]
    u = _gelu(u_ref[...])
    v = _gelu(v_ref[...])
    ms = jnp.mean(v * v, axis=-1, keepdims=True)
    vb = (v * lax.rsqrt(ms + EPS) * vg_ref[...]).astype(BF16)
    gate = _silu(z_ref[...])
    i = lax.broadcasted_iota(jnp.int32, (GMLP_BLOCK, GMLP_BLOCK), 0)
    j = lax.broadcasted_iota(jnp.int32, (GMLP_BLOCK, GMLP_BLOCK), 1)
    keep = (j // CHUNK) <= (i // CHUNK)
    for g in range(A_GROUPS):
        wg = jnp.where(keep, w_ref[g], 0.0).astype(BF16)
        cols = slice(g * LANES, (g + 1) * LANES)
        for blk in range(tm // GMLP_BLOCK):
            rows = slice(blk * GMLP_BLOCK, (blk + 1) * GMLP_BLOCK)
            sg = jnp.dot(wg, vb[rows, cols], preferred_element_type=F32) + b_ref[g]
            o_ref[rows, cols] = (u[rows, cols] * sg * gate[rows, cols]).astype(BF16)


def _mixer_a(h, vg, ws, bs, tm=512):
    n = h.shape[0]
    hspec = lambda blk: pl.BlockSpec((tm, GROUP_W), lambda i, blk=blk: (i, blk))
    return pl.pallas_call(
        _mixer_a_kernel,
        grid=(n // tm,),
        in_specs=[hspec(BLK_A_U), hspec(BLK_A_V), hspec(BLK_A_Z),
                  pl.BlockSpec((1, GROUP_W), lambda i: (0, 0)),
                  pl.BlockSpec((A_GROUPS, GMLP_BLOCK, GMLP_BLOCK), lambda i: (0, 0, 0)),
                  pl.BlockSpec((A_GROUPS, GMLP_BLOCK, 1), lambda i: (0, 0, 0))],
        out_specs=pl.BlockSpec((tm, GROUP_W), lambda i: (i, 0)),
        out_shape=jax.ShapeDtypeStruct((n, GROUP_W), BF16),
        compiler_params=_params(1),
    )(h, h, h, vg, ws, bs)


def _order_key(x):
    return jnp.where(x < 0, x ^ 0x7FFFFFFF, x)


def _b_select_kernel(iq_ref, iw_ref, ik_ref, o_ref, ikb_ref, lhs_ref, key_ref, *, seq, tq, topk):
    t_blk = pl.program_id(1)
    nkb = seq // 256
    nblk = ((t_blk + 1) * tq) // 256
    n_interp = 12
    hrows = [slice(h * tq, (h + 1) * tq) for h in range(N_HEADS64)]

    @pl.when(t_blk == 0)
    def _():
        ikb_ref[...] = ik_ref[...].astype(BF16)

    lo_half = _lo_mask(tq)
    w_t = (iw_ref[...] * IDX_SCALE).T
    for h in range(N_HEADS64):
        iqt = iq_ref[:, (h // 2) * LANES:(h // 2 + 1) * LANES]
        sel = lo_half if h % 2 == 0 else jnp.logical_not(lo_half)
        lhs_ref[hrows[h], :] = jnp.where(sel, iqt, 0.0).astype(BF16)

    qpos = t_blk * tq + lax.broadcasted_iota(jnp.int32, (256, tq), 1)
    krow = lax.broadcasted_iota(jnp.int32, (256, tq), 0)

    def score_pass(kb, carry):
        smin, smax = carry
        off = pl.multiple_of(kb * 256, 256)
        ikblk = ikb_ref[pl.ds(off, 256), :]
        score = jnp.zeros((256, tq), F32)
        for h in range(N_HEADS64):
            score = score + w_t[h:h + 1, :] * jnp.maximum(_dot_t(ikblk, lhs_ref[hrows[h], :]), 0.0)
        score = jnp.where(score == 0.0, 0.0, score)
        adm = ((kb * 256 + krow) // CHUNK) <= (qpos // CHUNK)
        key = _order_key(lax.bitcast_convert_type(score, jnp.int32))
        key_ref[kb] = jnp.where(adm, key, INT_MIN)
        smin = jnp.minimum(smin, jnp.min(jnp.where(adm, score, jnp.inf), axis=0, keepdims=True))
        smax = jnp.maximum(smax, jnp.max(jnp.where(adm, score, -jnp.inf), axis=0, keepdims=True))
        return smin, smax

    smin, smax = lax.fori_loop(0, nblk, score_pass,
                               (jnp.full((1, tq), jnp.inf, F32), jnp.full((1, tq), -jnp.inf, F32)))

    def count(pred):
        def body(kb, acc):
            return acc + jnp.sum(jnp.where(pred(key_ref[kb], kb), 1.0, 0.0), axis=0, keepdims=True)
        return lax.fori_loop(0, nblk, body, jnp.zeros((1, tq), F32))

    kf = float(topk)
    qrow = t_blk * tq + lax.broadcasted_iota(jnp.int32, (1, tq), 1)
    n_adm = ((qrow // CHUNK + 1) * CHUNK).astype(F32)
    one = jnp.ones((1, tq), jnp.int32)
    f_pos = count(lambda k, kb: k >= one)
    f_nn = count(lambda k, kb: k >= one - 1)
    pos = f_pos > kf
    neg = f_nn < kf
    lo0 = jnp.where(pos, one, _order_key(lax.bitcast_convert_type(smin, jnp.int32)))
    hi0 = jnp.where(neg, one - 1, _order_key(lax.bitcast_convert_type(smax, jnp.int32)) + 1)
    w_lo0 = jnp.where(pos, f_pos, n_adm) - kf
    w_hi0 = kf - jnp.where(neg, f_nn, 0.0)
    all_sel = n_adm <= kf
    at_zero = jnp.logical_not(pos | neg)
    done0 = jnp.where(all_sel | at_zero | (hi0 == lo0 + 1), 1.0, 0.0)
    thr0 = jnp.where(all_sel, INT_MIN + 1, jnp.where(at_zero, jnp.where(f_pos == kf, one, one - 1), lo0))

    def search_cond(st):
        return jnp.logical_and(st[0] < n_interp + 32, st[1] < 0.5)

    def search_step(st):
        it, _, lo, hi, w_lo, w_hi, side, done, thr = st
        lo_v = lax.bitcast_convert_type(_order_key(lo), F32)
        hi_v = lax.bitcast_convert_type(_order_key(hi), F32)
        c_v = lo_v + (hi_v - lo_v) * (w_lo / (w_lo + w_hi))
        c_interp = _order_key(lax.bitcast_convert_type(c_v, jnp.int32))
        c_mid = (lo >> 1) + (hi >> 1) + (lo & hi & 1)
        cand = jnp.where(it < n_interp, c_interp, c_mid)
        cand = jnp.minimum(jnp.maximum(cand, lo + 1), hi - 1)
        f = count(lambda k, kb: k >= cand)
        live = done < 0.5
        up = f > kf
        hit = f == kf
        new_lo = jnp.where(live & up, cand, lo)
        new_hi = jnp.where(live & jnp.logical_not(up), cand, hi)
        new_w_lo = jnp.where(up, f - kf, jnp.where(side < 0.0, 0.5 * w_lo, w_lo))
        new_w_hi = jnp.where(up, jnp.where(side > 0.0, 0.5 * w_hi, w_hi), kf - f)
        new_side = jnp.where(up, 1.0, -1.0)
        new_thr = jnp.where(live, jnp.where(hit, cand, new_lo), thr)
        new_done = jnp.where(live & (hit | (new_hi == new_lo + 1)), 1.0, done)
        return (it + 1, jnp.min(new_done), new_lo, new_hi, jnp.where(live, new_w_lo, w_lo),
                jnp.where(live, new_w_hi, w_hi), jnp.where(live, new_side, side), new_done, new_thr)

    state = (jnp.int32(0), jnp.min(done0), lo0, hi0, w_lo0, w_hi0, jnp.zeros((1, tq), F32), done0, thr0)
    thr = lax.while_loop(search_cond, search_step, state)[-1]

    need = kf - count(lambda k, kb: k > thr)
    ties = count(lambda k, kb: k == thr)
    any_excess = jnp.max(ties - need) > 0.0

    def write_unused(kb, carry):
        o_ref[0, kb] = jnp.zeros((tq, 256), F32)
        return carry

    lax.fori_loop(nblk, nkb, write_unused, 0)

    @pl.when(jnp.logical_not(any_excess))
    def _():
        def write(kb, carry):
            o_ref[0, kb] = jnp.where(key_ref[kb] >= thr, 1.0, 0.0).T
            return carry

        lax.fori_loop(0, nblk, write, 0)

    @pl.when(any_excess)
    def _():
        def idx_step(it, jmax):
            cand = jmax | lax.shift_left(jnp.int32(1), 10 - it)
            below = count(lambda k, kb: (k == thr) & ((kb * 256 + krow) < cand))
            return jnp.where(below < need, cand, jmax)

        jmax = lax.fori_loop(0, 11, idx_step, jnp.zeros((1, tq), jnp.int32))

        def write(kb, carry):
            k = key_ref[kb]
            keep_tie = (k == thr) & ((kb * 256 + krow) <= jmax)
            o_ref[0, kb] = jnp.where((k > thr) | keep_tie, 1.0, 0.0).T
            return carry

        lax.fori_loop(0, nblk, write, 0)


def _b_select(h, bsz, seq, tq=256):
    nt = seq // tq
    topk = min(TOPK_MAX, seq // 4)
    kern = functools.partial(_b_select_kernel, seq=seq, tq=tq, topk=topk)
    return pl.pallas_call(
        kern,
        grid=(bsz, nt),
        in_specs=[pl.BlockSpec((tq, GROUP_W), lambda b, t: (b * nt + t, BLK_B_IQ)),
                  pl.BlockSpec((tq, LANES), lambda b, t: (b * nt + t, BLK_B_SMALL * 4 + 3)),
                  pl.BlockSpec((seq, LANES), lambda b, t: (b, BLK_B_SMALL * 4 + 2))],
        out_specs=pl.BlockSpec((1, seq // 256, tq, 256), lambda b, t: (b * nt + t, 0, 0, 0)),
        out_shape=jax.ShapeDtypeStruct((bsz * nt, seq // 256, tq, 256), F32),
        scratch_shapes=[pltpu.VMEM((seq, LANES), BF16), pltpu.VMEM((N_HEADS64 * tq, LANES), BF16),
                        pltpu.VMEM((seq // 256, tq, 256), jnp.int32)],
        compiler_params=_params(2),
    )(h, h, h)


def _b_attn_kernel(cfar_ref, q_ref, z_ref, k_ref, v_ref, msk_ref, qg_ref, kg_ref, base0_ref, base1_ref,
                   o_ref, kn_ref, v1_ref, bias_ref, qall_ref, s_ref, mp_ref, acc_ref, *, seq, tq):
    b = pl.program_id(0)
    t_blk = pl.program_id(1)
    hrows = [slice(h * tq, (h + 1) * tq) for h in range(N_HEADS64)]

    @pl.when((b == 0) & (t_blk == 0))
    def _():
        for h in range(N_HEADS64):
            bias_ref[0, hrows[h], :] = jnp.full((tq, 256), cfar_ref[h], F32)
            bias_ref[1, hrows[h], :] = _toeplitz(base1_ref[h:h + 1, :], tq, 256)
            bias_ref[2, hrows[h], :] = _toeplitz(base0_ref[h:h + 1, :], tq, 256)

    @pl.when(t_blk == 0)
    def _():
        lo256 = _lo_mask(256)
        for r in range(seq // 256):
            rows = slice(r * 256, (r + 1) * 256)
            k = k_ref[rows, :]
            ms = jnp.mean(k * k, axis=-1, keepdims=True)
            kn_ref[rows, :] = (k * lax.rsqrt(ms + EPS) * kg_ref[...]).astype(BF16)
            v1_ref[rows, :] = jnp.where(lo256, v_ref[rows, :], 1.0).astype(BF16)

    lo = _lo_mask(tq)
    qtiles = _rms_heads64(q_ref[...], qg_ref[...], 4)
    for h in range(N_HEADS64):
        sel = lo if h % 2 == 0 else jnp.logical_not(lo)
        qall_ref[hrows[h], :] = jnp.where(sel, qtiles[h // 2] * 0.125, 0.0).astype(BF16)
    mp_ref[...] = jnp.full(mp_ref.shape, NEG, F32)
    acc_ref[...] = jnp.zeros(acc_ref.shape, F32)
    nblk = t_blk + 1

    def logits_pass(kb, carry):
        off = pl.multiple_of(kb * 256, 256)
        kblk = kn_ref[pl.ds(off, 256), :]
        which = jnp.clip(kb - (t_blk - 2), 0, 2)
        keep = msk_ref[0, kb] > 0.5
        for h in range(N_HEADS64):
            s = jnp.where(keep, _dot_t(qall_ref[hrows[h], :], kblk) + bias_ref[which, hrows[h], :], NEG)
            s_ref[kb, hrows[h], :] = s
            mp_ref[hrows[h], :] = jnp.maximum(mp_ref[hrows[h], :], jnp.maximum(s[:, :LANES], s[:, LANES:]))
        return carry

    lax.fori_loop(0, nblk, logits_pass, 0)
    for h in range(N_HEADS64):
        m = jnp.max(mp_ref[hrows[h], :], axis=-1, keepdims=True)
        mp_ref[hrows[h], :] = jnp.broadcast_to(m, (tq, LANES))

    def value_pass(kb, carry):
        off = pl.multiple_of(kb * 256, 256)
        v1 = v1_ref[pl.ds(off, 256), :]
        for h in range(N_HEADS64):
            m = mp_ref[hrows[h], :]
            p = jnp.exp(s_ref[kb, hrows[h], :] - jnp.concatenate([m, m], axis=1)).astype(BF16)
            acc_ref[hrows[h], :] += jnp.dot(p, v1, preferred_element_type=F32)
        return carry

    lax.fori_loop(0, nblk, value_pass, 0)

    gate = _silu(z_ref[...])
    for t in range(4):
        a_even = acc_ref[hrows[2 * t], :]
        a_odd = acc_ref[hrows[2 * t + 1], :]
        o_even = a_even / pltpu.roll(a_even, 64, 1)
        o_odd = pltpu.roll(a_odd, 64, 1) / a_odd
        cols = slice(t * LANES, (t + 1) * LANES)
        o_ref[:, cols] = (jnp.where(lo, o_even, o_odd) * gate[:, cols]).astype(BF16)


def _b_attn(h, mask, cfar, qg, kg, base0, base1, bsz, seq, tq=256):
    nt = seq // tq
    rows = N_HEADS64 * tq
    kern = functools.partial(_b_attn_kernel, seq=seq, tq=tq)
    full = lambda shape: pl.BlockSpec(shape, lambda b, t: (0,) * len(shape))
    return pl.pallas_call(
        kern,
        grid=(bsz, nt),
        in_specs=[pl.BlockSpec(memory_space=pltpu.SMEM),
                  pl.BlockSpec((tq, GROUP_W), lambda b, t: (b * nt + t, BLK_B_Q)),
                  pl.BlockSpec((tq, GROUP_W), lambda b, t: (b * nt + t, BLK_B_Z)),
                  pl.BlockSpec((seq, LANES), lambda b, t: (b, BLK_B_SMALL * 4)),
                  pl.BlockSpec((seq, LANES), lambda b, t: (b, BLK_B_SMALL * 4 + 1)),
                  pl.BlockSpec((1, seq // 256, tq, 256), lambda b, t: (b * nt + t, 0, 0, 0)),
                  full((1, GROUP_W)), full((1, LANES)), full((N_HEADS64, 512)), full((N_HEADS64, 512))],
        out_specs=pl.BlockSpec((tq, GROUP_W), lambda b, t: (b * nt + t, 0)),
        out_shape=jax.ShapeDtypeStruct((bsz * seq, GROUP_W), BF16),
        scratch_shapes=[pltpu.VMEM((seq, LANES), BF16), pltpu.VMEM((seq, LANES), BF16),
                        pltpu.VMEM((3, rows, 256), F32), pltpu.VMEM((rows, LANES), BF16),
                        pltpu.VMEM((seq // 256, rows, 256), F32),
                        pltpu.VMEM((rows, LANES), F32), pltpu.VMEM((rows, LANES), F32)],
        compiler_params=_params(2),
    )(cfar, h, h, h, h, mask, qg, kg, base0, base1)


def _rope(tile, cos, sin):
    return tile * cos + pltpu.roll(tile, 64, 1) * sin


def _c_prep_kernel(cq_ref, sm_ref, wq_ref, wkv_ref, qag_ref, kvag_ref, qg_ref, kg_ref, cos_ref, sin_ref,
                   qo_ref, ko_ref, vo_ref):
    cq = cq_ref[...]
    ms = jnp.sum(cq * cq, axis=-1, keepdims=True) * (1.0 / Q_LORA)
    cqn = (cq * lax.rsqrt(ms + EPS) * qag_ref[...]).astype(BF16)
    qpre = jnp.dot(cqn, wq_ref[...], preferred_element_type=F32)
    ckv = sm_ref[:, 0:LANES]
    ms = jnp.mean(ckv * ckv, axis=-1, keepdims=True)
    ckvn = (ckv * lax.rsqrt(ms + EPS) * kvag_ref[...]).astype(BF16)
    kvpre = jnp.dot(ckvn, wkv_ref[...], preferred_element_type=F32)
    kr = sm_ref[:, LANES:2 * LANES]
    kr_ss = jnp.sum(kr * kr, axis=-1, keepdims=True)
    cos = cos_ref[...]
    sin = sin_ref[...]
    qg = qg_ref[...]
    kg = kg_ref[...]
    for h in range(C_HEADS):
        qh = qpre[:, h * 256:(h + 1) * 256]
        r = lax.rsqrt(jnp.sum(qh * qh, axis=-1, keepdims=True) * (1.0 / C_QK) + EPS)
        qn = qh * r * qg
        qo_ref[:, h * 256:h * 256 + LANES] = qn[:, :LANES].astype(BF16)
        qo_ref[:, h * 256 + LANES:(h + 1) * 256] = _rope(qn[:, LANES:], cos, sin).astype(BF16)
        kn = kvpre[:, h * LANES:(h + 1) * LANES]
        r = lax.rsqrt((jnp.sum(kn * kn, axis=-1, keepdims=True) + kr_ss) * (1.0 / C_QK) + EPS)
        ko_ref[:, h * 256:h * 256 + LANES] = (kn * r * kg[:, :LANES]).astype(BF16)
        ko_ref[:, h * 256 + LANES:(h + 1) * 256] = _rope(kr * r * kg[:, LANES:], cos, sin).astype(BF16)
    vo_ref[...] = kvpre[:, C_HEADS * LANES:].astype(BF16)


def _c_prep(h, wq, wkv, qag, kvag, qg, kg, cos, sin, seq, tm=512):
    n = h.shape[0]
    ns = seq // tm
    full = lambda shape: pl.BlockSpec(shape, lambda i: (0,) * len(shape))
    return pl.pallas_call(
        _c_prep_kernel,
        grid=(n // tm,),
        in_specs=[pl.BlockSpec((tm, GROUP_W), lambda i: (i, BLK_C_Q)),
                  pl.BlockSpec((tm, GROUP_W), lambda i: (i, BLK_C_SMALL)),
                  full((GROUP_W, 4 * 256)), full((KV_LORA, 8 * LANES)),
                  full((1, GROUP_W)), full((1, LANES)), full((1, 256)), full((1, 256)),
                  pl.BlockSpec((tm, LANES), lambda i: (i % ns, 0)),
                  pl.BlockSpec((tm, LANES), lambda i: (i % ns, 0))],
        out_specs=[pl.BlockSpec((tm, 4 * 256), lambda i: (i, 0)),
                   pl.BlockSpec((tm, 4 * 256), lambda i: (i, 0)),
                   pl.BlockSpec((tm, GROUP_W), lambda i: (i, 0))],
        out_shape=[jax.ShapeDtypeStruct((n, 4 * 256), BF16), jax.ShapeDtypeStruct((n, 4 * 256), BF16),
                   jax.ShapeDtypeStruct((n, GROUP_W), BF16)],
        compiler_params=_params(1),
    )(h, h, wq, wkv, qag, kvag, qg, kg, cos, sin)


def _c_attn_kernel(q_ref, k_ref, v_ref, z_ref, o_ref, s_ref, mp_ref, lp_ref, acc_ref, *, tq):
    qt = pl.program_id(1)
    scale = C_QK ** -0.5
    hrows = [slice(h * tq, (h + 1) * tq) for h in range(C_HEADS)]
    qchunk = (qt * tq + lax.broadcasted_iota(jnp.int32, (tq, 256), 0)) // CHUNK
    kcol = lax.broadcasted_iota(jnp.int32, (tq, 256), 1)
    mp_ref[...] = jnp.full(mp_ref.shape, NEG, F32)
    lp_ref[...] = jnp.zeros(lp_ref.shape, F32)
    acc_ref[...] = jnp.zeros(acc_ref.shape, F32)
    nblk = (qt + 1) * (tq // 256)

    def logits_pass(kb, carry):
        off = pl.multiple_of(kb * 256, 256)
        keep = ((kb * 256 + kcol) // CHUNK) <= qchunk
        for h in range(C_HEADS):
            cols = slice(h * 256, (h + 1) * 256)
            s = jnp.where(keep, _dot_t(q_ref[:, cols], k_ref[pl.ds(off, 256), cols]) * scale, NEG)
            s_ref[kb, hrows[h], :] = s
            mp_ref[hrows[h], :] = jnp.maximum(mp_ref[hrows[h], :], jnp.maximum(s[:, :LANES], s[:, LANES:]))
        return carry

    lax.fori_loop(0, nblk, logits_pass, 0)
    for h in range(C_HEADS):
        m = jnp.max(mp_ref[hrows[h], :], axis=-1, keepdims=True)
        mp_ref[hrows[h], :] = jnp.broadcast_to(m, (tq, LANES))

    def value_pass(kb, carry):
        off = pl.multiple_of(kb * 256, 256)
        for h in range(C_HEADS):
            m = mp_ref[hrows[h], :]
            p = jnp.exp(s_ref[kb, hrows[h], :] - jnp.concatenate([m, m], axis=1))
            lp_ref[hrows[h], :] += p[:, :LANES] + p[:, LANES:]
            acc_ref[hrows[h], :] += jnp.dot(p.astype(BF16), v_ref[pl.ds(off, 256), h * LANES:(h + 1) * LANES],
                                            preferred_element_type=F32)
        return carry

    lax.fori_loop(0, nblk, value_pass, 0)
    gate = _silu(z_ref[...])
    for h in range(C_HEADS):
        cols = slice(h * LANES, (h + 1) * LANES)
        l = jnp.sum(lp_ref[hrows[h], :], axis=-1, keepdims=True)
        o_ref[:, cols] = (acc_ref[hrows[h], :] / l * gate[:, cols]).astype(BF16)


def _c_attn(qc, kc, vc, h, bsz, seq, tq=256):
    nt = seq // tq
    rows = C_HEADS * tq
    kern = functools.partial(_c_attn_kernel, tq=tq)
    return pl.pallas_call(
        kern,
        grid=(bsz, nt),
        in_specs=[pl.BlockSpec((tq, C_HEADS * 256), lambda b, t: (b * nt + t, 0)),
                  pl.BlockSpec((seq, C_HEADS * 256), lambda b, t: (b, 0)),
                  pl.BlockSpec((seq, GROUP_W), lambda b, t: (b, 0)),
                  pl.BlockSpec((tq, GROUP_W), lambda b, t: (b * nt + t, BLK_C_Z))],
        out_specs=pl.BlockSpec((tq, GROUP_W), lambda b, t: (b * nt + t, 0)),
        out_shape=jax.ShapeDtypeStruct((bsz * seq, GROUP_W), BF16),
        scratch_shapes=[pltpu.VMEM((seq // 256, rows, 256), F32), pltpu.VMEM((rows, LANES), F32),
                        pltpu.VMEM((rows, LANES), F32), pltpu.VMEM((rows, LANES), F32)],
        compiler_params=_params(2),
    )(qc, kc, vc, h)


def _mixer_d_kernel(q_ref, k_ref, v_ref, z_ref, qg_ref, kg_ref, base_ref, o_ref,
                    kpad_ref, vpad_ref, bias_ref, *, seq, tq):
    b = pl.program_id(0)
    qt = pl.program_id(1)
    win = tq + D_LEFT

    @pl.when((b == 0) & (qt == 0))
    def _():
        for h in range(N_HEADS64):
            bias_ref[h] = _toeplitz(base_ref[h:h + 1, :], tq, win)

    @pl.when(qt == 0)
    def _():
        kpad_ref[0:D_LEFT, :] = jnp.zeros((D_LEFT, GROUP_W), BF16)
        vpad_ref[0:D_LEFT, :] = jnp.zeros((D_LEFT, GROUP_W), BF16)
        for r in range(seq // 256):
            rows = slice(r * 256, (r + 1) * 256)
            dst = slice(D_LEFT + r * 256, D_LEFT + (r + 1) * 256)
            tiles = _rms_heads64(k_ref[rows, :], kg_ref[...], 4)
            for t in range(4):
                kpad_ref[dst, t * LANES:(t + 1) * LANES] = tiles[t].astype(BF16)
            vpad_ref[dst, :] = v_ref[rows, :].astype(BF16)

    lo = _lo_mask(tq)
    qtiles = _rms_heads64(q_ref[...], qg_ref[...], 4)
    start = pl.multiple_of(qt * tq, tq)
    qc = lax.broadcasted_iota(jnp.int32, (tq, win), 0) // CHUNK
    j = lax.broadcasted_iota(jnp.int32, (tq, win), 1)
    kc = j // CHUNK
    valid = (kc >= qc) & (kc <= qc + D_LEFT // CHUNK) & (j + start >= D_LEFT)
    gate = _silu(z_ref[...])
    for t in range(4):
        cols = slice(t * LANES, (t + 1) * LANES)
        kwin = kpad_ref[pl.ds(start, win), cols]
        vwin = vpad_ref[pl.ds(start, win), cols]
        outs = []
        for half in range(2):
            sel = lo if half == 0 else jnp.logical_not(lo)
            qh = jnp.where(sel, qtiles[t] * 0.125, 0.0).astype(BF16)
            s = jnp.where(valid, _dot_t(qh, kwin) + bias_ref[2 * t + half], NEG)
            p = jnp.exp(s - jnp.max(s, axis=-1, keepdims=True))
            l = jnp.sum(p, axis=-1, keepdims=True)
            outs.append(jnp.dot(p.astype(BF16), vwin, preferred_element_type=F32) / l)
        o_ref[:, cols] = (jnp.where(lo, outs[0], outs[1]) * gate[:, cols]).astype(BF16)


def _mixer_d(h, qg, kg, base, bsz, seq, tq=256):
    nt = seq // tq
    kern = functools.partial(_mixer_d_kernel, seq=seq, tq=tq)
    full = lambda shape: pl.BlockSpec(shape, lambda b, t: (0,) * len(shape))
    return pl.pallas_call(
        kern,
        grid=(bsz, nt),
        in_specs=[pl.BlockSpec((tq, GROUP_W), lambda b, t: (b * nt + t, BLK_D_Q)),
                  pl.BlockSpec((seq, GROUP_W), lambda b, t: (b, BLK_D_K)),
                  pl.BlockSpec((seq, GROUP_W), lambda b, t: (b, BLK_D_V)),
                  pl.BlockSpec((tq, GROUP_W), lambda b, t: (b * nt + t, BLK_D_Z)),
                  full((1, GROUP_W)), full((1, GROUP_W)), full((N_HEADS64, 2 * tq + D_LEFT))],
        out_specs=pl.BlockSpec((tq, GROUP_W), lambda b, t: (b * nt + t, 0)),
        out_shape=jax.ShapeDtypeStruct((bsz * seq, GROUP_W), BF16),
        scratch_shapes=[pltpu.VMEM((seq + D_LEFT, GROUP_W), BF16), pltpu.VMEM((seq + D_LEFT, GROUP_W), BF16),
                        pltpu.VMEM((N_HEADS64, tq, tq + D_LEFT), F32)],
        compiler_params=_params(2),
    )(h, h, h, h, qg, kg, base)


def _zeros_cols(rows, n, dtype):
    return jnp.zeros((rows, n), dtype)


def _layout_w_in(w):
    w = w.astype(BF16)
    d = w.shape[0]
    c = lambda name, size, off=0: w[:, _SRC[name] + off:_SRC[name] + off + size]
    z = lambda n: _zeros_cols(d, n, BF16)
    pieces = [c("a_u", 512), c("a_v", 512), c("a_z", 512),
              c("b_q", 512), c("b_iq", 512), c("b_z", 512),
              c("b_k", 64), c("b_k", 64), c("b_v", 64), c("b_v", 64), c("b_ik", 64), c("b_ik", 64),
              c("b_iw", 8), z(120),
              c("c_q", 384), z(128),
              c("c_kv", 128), c("c_kr", 32), z(32), c("c_kr", 32, 32), z(32), z(256),
              c("c_z", 512),
              c("d_q", 512), c("d_k", 512), c("d_v", 512), c("d_z", 512)]
    return jnp.concatenate(pieces, axis=1)


def _rope_layout(v):
    z = jnp.zeros(v.shape[:-1] + (32,), v.dtype)
    return jnp.concatenate([v[..., :32], z, v[..., 32:], z], axis=-1)


def _layout_c(w_qb, w_kvb, q_gain, k_gain, qa_gain):
    wq = w_qb.reshape(Q_LORA, C_HEADS, C_QK)
    wq = jnp.concatenate([wq[..., :C_NOPE], _rope_layout(wq[..., C_NOPE:])], axis=-1)
    wq = wq.reshape(Q_LORA, C_HEADS * 256)
    wq = jnp.concatenate([wq, jnp.zeros((GROUP_W - Q_LORA, C_HEADS * 256), wq.dtype)], axis=0).astype(BF16)
    wkv = w_kvb.reshape(KV_LORA, C_HEADS, 2 * LANES)
    wkv = jnp.concatenate([wkv[..., :C_NOPE].reshape(KV_LORA, -1), wkv[..., C_NOPE:].reshape(KV_LORA, -1)],
                          axis=1).astype(BF16)
    lay = lambda g: jnp.concatenate([g[:C_NOPE], _rope_layout(g[C_NOPE:])])[None, :]
    qag = jnp.concatenate([qa_gain, jnp.zeros((GROUP_W - Q_LORA,), F32)])[None, :]
    return wq, wkv, lay(q_gain), lay(k_gain), qag


def _t5_bucket_static(rel):
    half = T5_BUCKETS // 2
    exact = half // 2
    n = abs(rel)
    if n < exact:
        val = n
    else:
        val = min(exact + (n * n // (exact * exact)).bit_length() - 1, half - 1)
    return (half if rel > 0 else 0) + val


def _t5_tables(t5_bias):
    m = np.arange(512)
    d0 = np.where(m < 256, m, m - 512)
    d1 = np.where(m <= 256, m - 256, m - 768)
    idx0 = np.array([_t5_bucket_static(int(d)) for d in d0], np.int32)
    idx1 = np.array([_t5_bucket_static(int(d)) for d in d1], np.int32)
    far = _t5_bucket_static(-512)
    return t5_bias[idx0].T, t5_bias[idx1].T, t5_bias[far]


def _band_table(rel_bias, tq):
    width = 2 * tq + D_LEFT
    m = np.arange(width)
    dist = np.where(m <= tq + D_LEFT, D_LEFT - m, D_LEFT + width - m)
    idx = np.clip(dist, -REL_CLIP, REL_CLIP) + REL_CLIP
    return rel_bias[idx.astype(np.int32)].T


def _rope_tables(seq):
    inv = ROPE_BASE ** (-jnp.arange(0, C_ROPE, 2, dtype=F32) / C_ROPE)
    ang = jnp.arange(seq, dtype=F32)[:, None] * inv[None, :]
    c, s = jnp.cos(ang), jnp.sin(ang)
    z = jnp.zeros_like(c)
    return jnp.concatenate([c, z, c, z], axis=1), jnp.concatenate([-s, z, s, z], axis=1)


def kernel(x, t5_bias, norm_g, w_in, a_v_gain, a_ws, a_bs, b_q_gain, b_k_gain, c_qa_gain, c_kva_gain,
           c_w_qb, c_w_kvb, c_q_gain, c_k_gain, d_q_gain, d_k_gain, d_rel_bias, w_out):
    bsz, seq, d_model = x.shape
    depth = w_in.shape[0]
    tq = 256
    x2 = x.reshape(bsz * seq, d_model)
    cos, sin = _rope_tables(seq)
    base0, base1, cfar = _t5_tables(t5_bias)
    for l in range(depth):
        h = _inproj(x2, norm_g[l][None, :], _layout_w_in(w_in[l]))
        y_a = _mixer_a(h, a_v_gain[l][None, :], a_ws[l], a_bs[l][:, :, None])
        mask = _b_select(h, bsz, seq, tq)
        y_b = _b_attn(h, mask, cfar, jnp.tile(b_q_gain[l], N_HEADS64)[None, :],
                      jnp.tile(b_k_gain[l], 2)[None, :], base0, base1, bsz, seq, tq)
        wq, wkv, qg, kg, qag = _layout_c(c_w_qb[l], c_w_kvb[l], c_q_gain[l], c_k_gain[l], c_qa_gain[l])
        qc, kc, vc = _c_prep(h, wq, wkv, qag, c_kva_gain[l][None, :], qg, kg, cos, sin, seq)
        y_c = _c_attn(qc, kc, vc, h, bsz, seq, tq)
        y_d = _mixer_d(h, jnp.tile(d_q_gain[l], N_HEADS64)[None, :], jnp.tile(d_k_gain[l], N_HEADS64)[None, :],
                       _band_table(d_rel_bias[l], tq), bsz, seq, tq)
        x2 = _outproj(x2, (y_a, y_b, y_c, y_d), w_out[l].astype(BF16))
    return x2.reshape(bsz, seq, d_model)
```

```python
import functools
import math

import numpy as np
import jax
import jax.numpy as jnp
from jax import lax
from jax.experimental import pallas as pl
from jax.experimental.pallas import tpu as pltpu

F32 = jnp.float32
BF16 = jnp.bfloat16

EPS = 1e-6
NEG = -1e30
INT_MIN = -(2 ** 31)
CHUNK = 64
LANES = 128
GROUP_W = 512
A_GROUPS = 4
GMLP_BLOCK = 128
N_HEADS64 = 8
IDX_SCALE = (8 ** -0.5) * 0.125
TOPK_MAX = 256
T5_BUCKETS = 32
C_HEADS = 4
C_NOPE = 128
C_ROPE = 64
C_QK = 192
Q_LORA = 384
KV_LORA = 128
ROPE_BASE = 10000.0
D_LEFT = 8 * CHUNK
REL_CLIP = 128
VMEM_LIMIT = 56 * 1024 * 1024

BLK_A_U, BLK_A_V, BLK_A_Z = 0, 1, 2
BLK_B_Q, BLK_B_IQ, BLK_B_Z, BLK_B_SMALL = 3, 4, 5, 6
BLK_C_Q, BLK_C_SMALL, BLK_C_Z = 7, 8, 9
BLK_D_Q, BLK_D_K, BLK_D_V, BLK_D_Z = 10, 11, 12, 13
H_COLS = 14 * GROUP_W

_SRC = dict(a_u=0, a_v=512, a_z=1024, b_q=1536, b_k=2048, b_v=2112, b_iq=2176, b_ik=2688,
            b_iw=2752, b_z=2760, c_q=3272, c_kv=3656, c_kr=3784, c_z=3848,
            d_q=4360, d_k=4872, d_v=5384, d_z=5896)


def _params(n_axes):
    return pltpu.CompilerParams(dimension_semantics=("arbitrary",) * n_axes,
                                vmem_limit_bytes=VMEM_LIMIT)


def _gelu(x):
    c = math.sqrt(2.0 / math.pi)
    return x * (0.5 * (1.0 + jnp.tanh(c * (x + 0.044715 * (x * x * x)))))


def _silu(x):
    return x * (1.0 / (1.0 + jnp.exp(-x)))


def _dot_t(a, b):
    return lax.dot_general(a, b, (((1,), (1,)), ((), ())), preferred_element_type=F32)


def _lo_mask(rows):
    return lax.broadcasted_iota(jnp.int32, (rows, LANES), 1) < 64


def _rms_heads64(x, gain, ntiles):
    lo = _lo_mask(x.shape[0])
    tiles = []
    for t in range(ntiles):
        xt = x[:, t * LANES:(t + 1) * LANES]
        sq = xt * xt
        s_lo = jnp.sum(jnp.where(lo, sq, 0.0), axis=-1, keepdims=True)
        s_hi = jnp.sum(jnp.where(lo, 0.0, sq), axis=-1, keepdims=True)
        r = jnp.where(lo, lax.rsqrt(s_lo * (1.0 / 64) + EPS), lax.rsqrt(s_hi * (1.0 / 64) + EPS))
        tiles.append(xt * r * gain[:, t * LANES:(t + 1) * LANES])
    return tiles


def _toeplitz(base_row, rows, width):
    t = jnp.broadcast_to(base_row, (rows, base_row.shape[1]))
    t = pltpu.roll(t, 0, 1, stride=1, stride_axis=0)
    return t[:, :width]


def _inproj_kernel(x_ref, g_ref, w_ref, o_ref, xn_ref):
    @pl.when(pl.program_id(1) == 0)
    def _():
        x = x_ref[...]
        ms = jnp.mean(x * x, axis=-1, keepdims=True)
        xn_ref[...] = (x * lax.rsqrt(ms + EPS) * g_ref[...]).astype(BF16)

    o_ref[...] = jnp.dot(xn_ref[...], w_ref[...], preferred_element_type=F32).astype(BF16)


def _inproj(x2, g, w, tm=1024, tn=1792):
    n, d = x2.shape
    return pl.pallas_call(
        _inproj_kernel,
        grid=(n // tm, H_COLS // tn),
        in_specs=[pl.BlockSpec((tm, d), lambda i, j: (i, 0)),
                  pl.BlockSpec((1, d), lambda i, j: (0, 0)),
                  pl.BlockSpec((d, tn), lambda i, j: (0, j))],
        out_specs=pl.BlockSpec((tm, tn), lambda i, j: (i, j)),
        out_shape=jax.ShapeDtypeStruct((n, H_COLS), BF16),
        scratch_shapes=[pltpu.VMEM((tm, d), BF16)],
        compiler_params=_params(2),
    )(x2, g, w)


def _outproj_kernel(x_ref, ya_ref, yb_ref, yc_ref, yd_ref, w_ref, o_ref):
    acc = x_ref[...]
    for g, y_ref in enumerate((ya_ref, yb_ref, yc_ref, yd_ref)):
        acc = acc + jnp.dot(y_ref[...], w_ref[g * GROUP_W:(g + 1) * GROUP_W, :],
                            preferred_element_type=F32)
    o_ref[...] = acc


def _outproj(x2, ys, w, tm=512):
    n, d = x2.shape
    yspec = pl.BlockSpec((tm, GROUP_W), lambda i: (i, 0))
    return pl.pallas_call(
        _outproj_kernel,
        grid=(n // tm,),
        in_specs=[pl.BlockSpec((tm, d), lambda i: (i, 0)), yspec, yspec, yspec, yspec,
                  pl.BlockSpec((4 * GROUP_W, d), lambda i: (0, 0))],
        out_specs=pl.BlockSpec((tm, d), lambda i: (i, 0)),
        out_shape=jax.ShapeDtypeStruct((n, d), F32),
        compiler_params=_params(1),
    )(x2, *ys, w)


def _mixer_a_kernel(u_ref, v_ref, z_ref, vg_ref, w_ref, b_ref, o_ref):
    tm = u_ref.shape[0]
    u = _gelu(u_ref[...].astype(F32))
    v = _gelu(v_ref[...].astype(F32))
    ms = jnp.mean(v * v, axis=-1, keepdims=True)
    vb = (v * lax.rsqrt(ms + EPS) * vg_ref[...]).astype(BF16)
    gate = _silu(z_ref[...].astype(F32))
    i = lax.broadcasted_iota(jnp.int32, (GMLP_BLOCK, GMLP_BLOCK), 0)
    j = lax.broadcasted_iota(jnp.int32, (GMLP_BLOCK, GMLP_BLOCK), 1)
    keep = (j // CHUNK) <= (i // CHUNK)
    for g in range(A_GROUPS):
        wg = jnp.where(keep, w_ref[g], 0.0).astype(BF16)
        cols = slice(g * LANES, (g + 1) * LANES)
        for blk in range(tm // GMLP_BLOCK):
            rows = slice(blk * GMLP_BLOCK, (blk + 1) * GMLP_BLOCK)
            sg = jnp.dot(wg, vb[rows, cols], preferred_element_type=F32) + b_ref[g]
            o_ref[rows, cols] = (u[rows, cols] * sg * gate[rows, cols]).astype(BF16)


def _mixer_a(h, vg, ws, bs, tm=512):
    n = h.shape[0]
    hspec = lambda blk: pl.BlockSpec((tm, GROUP_W), lambda i, blk=blk: (i, blk))
    return pl.pallas_call(
        _mixer_a_kernel,
        grid=(n // tm,),
        in_specs=[hspec(BLK_A_U), hspec(BLK_A_V), hspec(BLK_A_Z),
                  pl.BlockSpec((1, GROUP_W), lambda i: (0, 0)),
                  pl.BlockSpec((A_GROUPS, GMLP_BLOCK, GMLP_BLOCK), lambda i: (0, 0, 0)),
                  pl.BlockSpec((A_GROUPS, GMLP_BLOCK, 1), lambda i: (0, 0, 0))],
        out_specs=pl.BlockSpec((tm, GROUP_W), lambda i: (i, 0)),
        out_shape=jax.ShapeDtypeStruct((n, GROUP_W), BF16),
        compiler_params=_params(1),
    )(h, h, h, vg, ws, bs)


def _order_key(x):
    return jnp.where(x < 0, x ^ 0x7FFFFFFF, x)


def _b_select_kernel(iq_ref, iw_ref, ik_ref, o_ref, lhs_ref, key_ref, *, seq, tq, topk):
    t_blk = pl.program_id(1)
    nkb = seq // 256
    nblk = ((t_blk + 1) * tq) // 256
    n_interp = 12
    hrows = [slice(h * tq, (h + 1) * tq) for h in range(N_HEADS64)]

    lo_half = _lo_mask(tq)
    w_t = (iw_ref[...].astype(F32) * IDX_SCALE).T
    for h in range(N_HEADS64):
        iqt = iq_ref[:, (h // 2) * LANES:(h // 2 + 1) * LANES]
        sel = lo_half if h % 2 == 0 else jnp.logical_not(lo_half)
        lhs_ref[hrows[h], :] = jnp.where(sel, iqt, jnp.zeros_like(iqt))

    qpos = t_blk * tq + lax.broadcasted_iota(jnp.int32, (256, tq), 1)
    krow = lax.broadcasted_iota(jnp.int32, (256, tq), 0)

    def score_pass(kb, carry):
        smin, smax = carry
        off = pl.multiple_of(kb * 256, 256)
        ikblk = ik_ref[pl.ds(off, 256), :]
        score = jnp.zeros((256, tq), F32)
        for h in range(N_HEADS64):
            score = score + w_t[h:h + 1, :] * jnp.maximum(_dot_t(ikblk, lhs_ref[hrows[h], :]), 0.0)
        score = jnp.where(score == 0.0, 0.0, score)
        adm = ((kb * 256 + krow) // CHUNK) <= (qpos // CHUNK)
        key = _order_key(lax.bitcast_convert_type(score, jnp.int32))
        key_ref[kb] = jnp.where(adm, key, INT_MIN)
        smin = jnp.minimum(smin, jnp.min(jnp.where(adm, score, jnp.inf), axis=0, keepdims=True))
        smax = jnp.maximum(smax, jnp.max(jnp.where(adm, score, -jnp.inf), axis=0, keepdims=True))
        return smin, smax

    smin, smax = lax.fori_loop(0, nblk, score_pass,
                               (jnp.full((1, tq), jnp.inf, F32), jnp.full((1, tq), -jnp.inf, F32)))

    def count(pred):
        def body(kb, acc):
            ones = jnp.where(pred(key_ref[kb], kb), 1.0, 0.0)
            return acc + jnp.sum(ones.reshape(256 // 8, 8, tq), axis=0)
        return jnp.sum(lax.fori_loop(0, nblk, body, jnp.zeros((8, tq), F32)), axis=0, keepdims=True)

    kf = float(topk)
    qrow = t_blk * tq + lax.broadcasted_iota(jnp.int32, (1, tq), 1)
    n_adm = ((qrow // CHUNK + 1) * CHUNK).astype(F32)
    one = jnp.ones((1, tq), jnp.int32)
    f_pos = count(lambda k, kb: k >= one)
    f_nn = count(lambda k, kb: k >= one - 1)
    pos = f_pos > kf
    neg = f_nn < kf
    lo0 = jnp.where(pos, one, _order_key(lax.bitcast_convert_type(smin, jnp.int32)))
    hi0 = jnp.where(neg, one - 1, _order_key(lax.bitcast_convert_type(smax, jnp.int32)) + 1)
    w_lo0 = jnp.where(pos, f_pos, n_adm) - kf
    w_hi0 = kf - jnp.where(neg, f_nn, 0.0)
    all_sel = n_adm <= kf
    at_zero = jnp.logical_not(pos | neg)
    done0 = jnp.where(all_sel | at_zero | (hi0 == lo0 + 1), 1.0, 0.0)
    thr0 = jnp.where(all_sel, INT_MIN + 1, jnp.where(at_zero, jnp.where(f_pos == kf, one, one - 1), lo0))

    def search_cond(st):
        return jnp.logical_and(st[0] < n_interp + 32, st[1] < 0.5)

    def search_step(st):
        it, _, lo, hi, w_lo, w_hi, side, done, thr = st
        lo_v = lax.bitcast_convert_type(_order_key(lo), F32)
        hi_v = lax.bitcast_convert_type(_order_key(hi), F32)
        c_v = lo_v + (hi_v - lo_v) * (w_lo / (w_lo + w_hi))
        c_interp = _order_key(lax.bitcast_convert_type(c_v, jnp.int32))
        c_mid = (lo >> 1) + (hi >> 1) + (lo & hi & 1)
        cand = jnp.where(it < n_interp, c_interp, c_mid)
        cand = jnp.minimum(jnp.maximum(cand, lo + 1), hi - 1)
        f = count(lambda k, kb: k >= cand)
        live = done < 0.5
        up = f > kf
        hit = f == kf
        new_lo = jnp.where(live & up, cand, lo)
        new_hi = jnp.where(live & jnp.logical_not(up), cand, hi)
        new_w_lo = jnp.where(up, f - kf, jnp.where(side < 0.0, 0.5 * w_lo, w_lo))
        new_w_hi = jnp.where(up, jnp.where(side > 0.0, 0.5 * w_hi, w_hi), kf - f)
        new_side = jnp.where(up, 1.0, -1.0)
        new_thr = jnp.where(live, jnp.where(hit, cand, new_lo), thr)
        new_done = jnp.where(live & (hit | (new_hi == new_lo + 1)), 1.0, done)
        return (it + 1, jnp.min(new_done), new_lo, new_hi, jnp.where(live, new_w_lo, w_lo),
                jnp.where(live, new_w_hi, w_hi), jnp.where(live, new_side, side), new_done, new_thr)

    state = (jnp.int32(0), jnp.min(done0), lo0, hi0, w_lo0, w_hi0, jnp.zeros((1, tq), F32), done0, thr0)
    thr = lax.while_loop(search_cond, search_step, state)[-1]

    need = kf - count(lambda k, kb: k > thr)
    ties = count(lambda k, kb: k == thr)
    any_excess = jnp.max(ties - need) > 0.0

    def write_unused(kb, carry):
        o_ref[0, kb] = jnp.zeros((tq, 256), F32)
        return carry

    lax.fori_loop(nblk, nkb, write_unused, 0)

    @pl.when(jnp.logical_not(any_excess))
    def _():
        def write(kb, carry):
            o_ref[0, kb] = jnp.where(key_ref[kb] >= thr, 1.0, 0.0).T
            return carry

        lax.fori_loop(0, nblk, write, 0)

    @pl.when(any_excess)
    def _():
        def idx_step(it, jmax):
            cand = jmax | lax.shift_left(jnp.int32(1), 10 - it)
            below = count(lambda k, kb: (k == thr) & ((kb * 256 + krow) < cand))
            return jnp.where(below < need, cand, jmax)

        jmax = lax.fori_loop(0, 11, idx_step, jnp.zeros((1, tq), jnp.int32))

        def write(kb, carry):
            k = key_ref[kb]
            keep_tie = (k == thr) & ((kb * 256 + krow) <= jmax)
            o_ref[0, kb] = jnp.where((k > thr) | keep_tie, 1.0, 0.0).T
            return carry

        lax.fori_loop(0, nblk, write, 0)


def _b_select(h, bsz, seq, tq=256):
    nt = seq // tq
    topk = min(TOPK_MAX, seq // 4)
    kern = functools.partial(_b_select_kernel, seq=seq, tq=tq, topk=topk)
    return pl.pallas_call(
        kern,
        grid=(bsz, nt),
        in_specs=[pl.BlockSpec((tq, GROUP_W), lambda b, t: (b * nt + t, BLK_B_IQ)),
                  pl.BlockSpec((tq, LANES), lambda b, t: (b * nt + t, BLK_B_SMALL * 4 + 3)),
                  pl.BlockSpec((seq, LANES), lambda b, t: (b, BLK_B_SMALL * 4 + 2))],
        out_specs=pl.BlockSpec((1, seq // 256, tq, 256), lambda b, t: (b * nt + t, 0, 0, 0)),
        out_shape=jax.ShapeDtypeStruct((bsz * nt, seq // 256, tq, 256), F32),
        scratch_shapes=[pltpu.VMEM((N_HEADS64 * tq, LANES), BF16),
                        pltpu.VMEM((seq // 256, 256, tq), jnp.int32)],
        compiler_params=_params(2),
    )(h, h, h)


def _b_attn_kernel(cfar_ref, q_ref, z_ref, k_ref, v_ref, msk_ref, qg_ref, kg_ref, base0_ref, base1_ref,
                   o_ref, kn_ref, v1_ref, bias_ref, qall_ref, s_ref, mp_ref, acc_ref, *, seq, tq):
    b = pl.program_id(0)
    t_blk = pl.program_id(1)
    hrows = [slice(h * tq, (h + 1) * tq) for h in range(N_HEADS64)]

    @pl.when((b == 0) & (t_blk == 0))
    def _():
        for h in range(N_HEADS64):
            bias_ref[0, hrows[h], :] = jnp.full((tq, 256), cfar_ref[h], F32)
            bias_ref[1, hrows[h], :] = _toeplitz(base1_ref[h:h + 1, :], tq, 256)
            bias_ref[2, hrows[h], :] = _toeplitz(base0_ref[h:h + 1, :], tq, 256)

    @pl.when(t_blk == 0)
    def _():
        lo256 = _lo_mask(256)
        for r in range(seq // 256):
            rows = slice(r * 256, (r + 1) * 256)
            k = k_ref[rows, :].astype(F32)
            ms = jnp.mean(k * k, axis=-1, keepdims=True)
            kn_ref[rows, :] = (k * lax.rsqrt(ms + EPS) * kg_ref[...]).astype(BF16)
            v = v_ref[rows, :]
            v1_ref[rows, :] = jnp.where(lo256, v, jnp.ones_like(v))

    lo = _lo_mask(tq)
    qtiles = _rms_heads64(q_ref[...].astype(F32), qg_ref[...], 4)
    for h in range(N_HEADS64):
        sel = lo if h % 2 == 0 else jnp.logical_not(lo)
        qall_ref[hrows[h], :] = jnp.where(sel, qtiles[h // 2] * 0.125, 0.0).astype(BF16)
    mp_ref[...] = jnp.full(mp_ref.shape, NEG, F32)
    acc_ref[...] = jnp.zeros(acc_ref.shape, F32)
    nblk = t_blk + 1

    def logits_pass(kb, carry):
        off = pl.multiple_of(kb * 256, 256)
        kblk = kn_ref[pl.ds(off, 256), :]
        which = jnp.clip(kb - (t_blk - 2), 0, 2)
        keep = msk_ref[0, kb] > 0.5
        for h in range(N_HEADS64):
            s = jnp.where(keep, _dot_t(qall_ref[hrows[h], :], kblk) + bias_ref[which, hrows[h], :], NEG)
            s_ref[kb, hrows[h], :] = s
            mp_ref[hrows[h], :] = jnp.maximum(mp_ref[hrows[h], :], jnp.maximum(s[:, :LANES], s[:, LANES:]))
        return carry

    lax.fori_loop(0, nblk, logits_pass, 0)
    for h in range(N_HEADS64):
        m = jnp.max(mp_ref[hrows[h], :], axis=-1, keepdims=True)
        mp_ref[hrows[h], :] = jnp.broadcast_to(m, (tq, LANES))

    def value_pass(kb, carry):
        off = pl.multiple_of(kb * 256, 256)
        v1 = v1_ref[pl.ds(off, 256), :]
        for h in range(N_HEADS64):
            m = mp_ref[hrows[h], :]
            p = jnp.exp(s_ref[kb, hrows[h], :] - jnp.concatenate([m, m], axis=1)).astype(BF16)
            acc_ref[hrows[h], :] += jnp.dot(p, v1, preferred_element_type=F32)
        return carry

    lax.fori_loop(0, nblk, value_pass, 0)

    gate = _silu(z_ref[...].astype(F32))
    for t in range(4):
        a_even = acc_ref[hrows[2 * t], :]
        a_odd = acc_ref[hrows[2 * t + 1], :]
        o_even = a_even / pltpu.roll(a_even, 64, 1)
        o_odd = pltpu.roll(a_odd, 64, 1) / a_odd
        cols = slice(t * LANES, (t + 1) * LANES)
        o_ref[:, cols] = (jnp.where(lo, o_even, o_odd) * gate[:, cols]).astype(BF16)


def _b_attn(h, mask, cfar, qg, kg, base0, base1, bsz, seq, tq=256):
    nt = seq // tq
    rows = N_HEADS64 * tq
    kern = functools.partial(_b_attn_kernel, seq=seq, tq=tq)
    full = lambda shape: pl.BlockSpec(shape, lambda b, t: (0,) * len(shape))
    return pl.pallas_call(
        kern,
        grid=(bsz, nt),
        in_specs=[pl.BlockSpec(memory_space=pltpu.SMEM),
                  pl.BlockSpec((tq, GROUP_W), lambda b, t: (b * nt + t, BLK_B_Q)),
                  pl.BlockSpec((tq, GROUP_W), lambda b, t: (b * nt + t, BLK_B_Z)),
                  pl.BlockSpec((seq, LANES), lambda b, t: (b, BLK_B_SMALL * 4)),
                  pl.BlockSpec((seq, LANES), lambda b, t: (b, BLK_B_SMALL * 4 + 1)),
                  pl.BlockSpec((1, seq // 256, tq, 256), lambda b, t: (b * nt + t, 0, 0, 0)),
                  full((1, GROUP_W)), full((1, LANES)), full((N_HEADS64, 512)), full((N_HEADS64, 512))],
        out_specs=pl.BlockSpec((tq, GROUP_W), lambda b, t: (b * nt + t, 0)),
        out_shape=jax.ShapeDtypeStruct((bsz * seq, GROUP_W), BF16),
        scratch_shapes=[pltpu.VMEM((seq, LANES), BF16), pltpu.VMEM((seq, LANES), BF16),
                        pltpu.VMEM((3, rows, 256), F32), pltpu.VMEM((rows, LANES), BF16),
                        pltpu.VMEM((seq // 256, rows, 256), F32),
                        pltpu.VMEM((rows, LANES), F32), pltpu.VMEM((rows, LANES), F32)],
        compiler_params=_params(2),
    )(cfar, h, h, h, h, mask, qg, kg, base0, base1)


def _rope(tile, cos, sin):
    return tile * cos + pltpu.roll(tile, 64, 1) * sin


def _c_prep_kernel(cq_ref, sm_ref, wq_ref, wkv_ref, qag_ref, kvag_ref, qg_ref, kg_ref, cos_ref, sin_ref,
                   qo_ref, ko_ref, vo_ref):
    cq = cq_ref[...].astype(F32)
    ms = jnp.sum(cq * cq, axis=-1, keepdims=True) * (1.0 / Q_LORA)
    cqn = (cq * lax.rsqrt(ms + EPS) * qag_ref[...]).astype(BF16)
    qpre = jnp.dot(cqn, wq_ref[...], preferred_element_type=F32)
    ckv = sm_ref[:, 0:LANES].astype(F32)
    ms = jnp.mean(ckv * ckv, axis=-1, keepdims=True)
    ckvn = (ckv * lax.rsqrt(ms + EPS) * kvag_ref[...]).astype(BF16)
    kvpre = jnp.dot(ckvn, wkv_ref[...], preferred_element_type=F32)
    kr = sm_ref[:, LANES:2 * LANES].astype(F32)
    kr_ss = jnp.sum(kr * kr, axis=-1, keepdims=True)
    cos = cos_ref[...]
    sin = sin_ref[...]
    qg = qg_ref[...]
    kg = kg_ref[...]
    for h in range(C_HEADS):
        qh = qpre[:, h * 256:(h + 1) * 256]
        r = lax.rsqrt(jnp.sum(qh * qh, axis=-1, keepdims=True) * (1.0 / C_QK) + EPS)
        qn = qh * r * qg
        qo_ref[:, h * 256:h * 256 + LANES] = qn[:, :LANES].astype(BF16)
        qo_ref[:, h * 256 + LANES:(h + 1) * 256] = _rope(qn[:, LANES:], cos, sin).astype(BF16)
        kn = kvpre[:, h * LANES:(h + 1) * LANES]
        r = lax.rsqrt((jnp.sum(kn * kn, axis=-1, keepdims=True) + kr_ss) * (1.0 / C_QK) + EPS)
        ko_ref[:, h * 256:h * 256 + LANES] = (kn * r * kg[:, :LANES]).astype(BF16)
        ko_ref[:, h * 256 + LANES:(h + 1) * 256] = _rope(kr * r * kg[:, LANES:], cos, sin).astype(BF16)
    vo_ref[...] = kvpre[:, C_HEADS * LANES:].astype(BF16)


def _c_prep(h, wq, wkv, qag, kvag, qg, kg, cos, sin, seq, tm=512):
    n = h.shape[0]
    ns = seq // tm
    full = lambda shape: pl.BlockSpec(shape, lambda i: (0,) * len(shape))
    return pl.pallas_call(
        _c_prep_kernel,
        grid=(n // tm,),
        in_specs=[pl.BlockSpec((tm, GROUP_W), lambda i: (i, BLK_C_Q)),
                  pl.BlockSpec((tm, GROUP_W), lambda i: (i, BLK_C_SMALL)),
                  full((GROUP_W, 4 * 256)), full((KV_LORA, 8 * LANES)),
                  full((1, GROUP_W)), full((1, LANES)), full((1, 256)), full((1, 256)),
                  pl.BlockSpec((tm, LANES), lambda i: (i % ns, 0)),
                  pl.BlockSpec((tm, LANES), lambda i: (i % ns, 0))],
        out_specs=[pl.BlockSpec((tm, 4 * 256), lambda i: (i, 0)),
                   pl.BlockSpec((tm, 4 * 256), lambda i: (i, 0)),
                   pl.BlockSpec((tm, GROUP_W), lambda i: (i, 0))],
        out_shape=[jax.ShapeDtypeStruct((n, 4 * 256), BF16), jax.ShapeDtypeStruct((n, 4 * 256), BF16),
                   jax.ShapeDtypeStruct((n, GROUP_W), BF16)],
        compiler_params=_params(1),
    )(h, h, wq, wkv, qag, kvag, qg, kg, cos, sin)


def _c_attn_kernel(q_ref, k_ref, v_ref, z_ref, o_ref, s_ref, mp_ref, lp_ref, acc_ref, *, tq):
    qt = pl.program_id(1)
    scale = C_QK ** -0.5
    hrows = [slice(h * tq, (h + 1) * tq) for h in range(C_HEADS)]
    qchunk = (qt * tq + lax.broadcasted_iota(jnp.int32, (tq, 256), 0)) // CHUNK
    kcol = lax.broadcasted_iota(jnp.int32, (tq, 256), 1)
    mp_ref[...] = jnp.full(mp_ref.shape, NEG, F32)
    lp_ref[...] = jnp.zeros(lp_ref.shape, F32)
    acc_ref[...] = jnp.zeros(acc_ref.shape, F32)
    nblk = (qt + 1) * (tq // 256)

    def logits_pass(kb, carry):
        off = pl.multiple_of(kb * 256, 256)
        keep = ((kb * 256 + kcol) // CHUNK) <= qchunk
        for h in range(C_HEADS):
            cols = slice(h * 256, (h + 1) * 256)
            s = jnp.where(keep, _dot_t(q_ref[:, cols], k_ref[pl.ds(off, 256), cols]) * scale, NEG)
            s_ref[kb, hrows[h], :] = s
            mp_ref[hrows[h], :] = jnp.maximum(mp_ref[hrows[h], :], jnp.maximum(s[:, :LANES], s[:, LANES:]))
        return carry

    lax.fori_loop(0, nblk, logits_pass, 0)
    for h in range(C_HEADS):
        m = jnp.max(mp_ref[hrows[h], :], axis=-1, keepdims=True)
        mp_ref[hrows[h], :] = jnp.broadcast_to(m, (tq, LANES))

    def value_pass(kb, carry):
        off = pl.multiple_of(kb * 256, 256)
        for h in range(C_HEADS):
            m = mp_ref[hrows[h], :]
            p = jnp.exp(s_ref[kb, hrows[h], :] - jnp.concatenate([m, m], axis=1))
            lp_ref[hrows[h], :] += p[:, :LANES] + p[:, LANES:]
            acc_ref[hrows[h], :] += jnp.dot(p.astype(BF16), v_ref[pl.ds(off, 256), h * LANES:(h + 1) * LANES],
                                            preferred_element_type=F32)
        return carry

    lax.fori_loop(0, nblk, value_pass, 0)
    gate = _silu(z_ref[...].astype(F32))
    for h in range(C_HEADS):
        cols = slice(h * LANES, (h + 1) * LANES)
        l = jnp.sum(lp_ref[hrows[h], :], axis=-1, keepdims=True)
        o_ref[:, cols] = (acc_ref[hrows[h], :] / l * gate[:, cols]).astype(BF16)


def _c_attn(qc, kc, vc, h, bsz, seq, tq=256):
    nt = seq // tq
    rows = C_HEADS * tq
    kern = functools.partial(_c_attn_kernel, tq=tq)
    return pl.pallas_call(
        kern,
        grid=(bsz, nt),
        in_specs=[pl.BlockSpec((tq, C_HEADS * 256), lambda b, t: (b * nt + t, 0)),
                  pl.BlockSpec((seq, C_HEADS * 256), lambda b, t: (b, 0)),
                  pl.BlockSpec((seq, GROUP_W), lambda b, t: (b, 0)),
                  pl.BlockSpec((tq, GROUP_W), lambda b, t: (b * nt + t, BLK_C_Z))],
        out_specs=pl.BlockSpec((tq, GROUP_W), lambda b, t: (b * nt + t, 0)),
        out_shape=jax.ShapeDtypeStruct((bsz * seq, GROUP_W), BF16),
        scratch_shapes=[pltpu.VMEM((seq // 256, rows, 256), F32), pltpu.VMEM((rows, LANES), F32),
                        pltpu.VMEM((rows, LANES), F32), pltpu.VMEM((rows, LANES), F32)],
        compiler_params=_params(2),
    )(qc, kc, vc, h)


def _mixer_d_kernel(q_ref, k_ref, v_ref, z_ref, qg_ref, kg_ref, base_ref, o_ref,
                    kpad_ref, vpad_ref, bias_ref, *, seq, tq):
    b = pl.program_id(0)
    qt = pl.program_id(1)
    win = tq + D_LEFT

    @pl.when((b == 0) & (qt == 0))
    def _():
        for h in range(N_HEADS64):
            bias_ref[h] = _toeplitz(base_ref[h:h + 1, :], tq, win)

    @pl.when(qt == 0)
    def _():
        kpad_ref[0:D_LEFT, :] = jnp.zeros((D_LEFT, GROUP_W), BF16)
        vpad_ref[0:D_LEFT, :] = jnp.zeros((D_LEFT, GROUP_W), BF16)
        for r in range(seq // 256):
            rows = slice(r * 256, (r + 1) * 256)
            dst = slice(D_LEFT + r * 256, D_LEFT + (r + 1) * 256)
            tiles = _rms_heads64(k_ref[rows, :].astype(F32), kg_ref[...], 4)
            for t in range(4):
                kpad_ref[dst, t * LANES:(t + 1) * LANES] = tiles[t].astype(BF16)
            vpad_ref[dst, :] = v_ref[rows, :]

    lo = _lo_mask(tq)
    qtiles = _rms_heads64(q_ref[...].astype(F32), qg_ref[...], 4)
    start = pl.multiple_of(qt * tq, tq)
    qc = lax.broadcasted_iota(jnp.int32, (tq, win), 0) // CHUNK
    j = lax.broadcasted_iota(jnp.int32, (tq, win), 1)
    kc = j // CHUNK
    valid = (kc >= qc) & (kc <= qc + D_LEFT // CHUNK) & (j + start >= D_LEFT)
    gate = _silu(z_ref[...].astype(F32))
    for t in range(4):
        cols = slice(t * LANES, (t + 1) * LANES)
        kwin = kpad_ref[pl.ds(start, win), cols]
        vwin = vpad_ref[pl.ds(start, win), cols]
        outs = []
        for half in range(2):
            sel = lo if half == 0 else jnp.logical_not(lo)
            qh = jnp.where(sel, qtiles[t] * 0.125, 0.0).astype(BF16)
            s = jnp.where(valid, _dot_t(qh, kwin) + bias_ref[2 * t + half], NEG)
            p = jnp.exp(s - jnp.max(s, axis=-1, keepdims=True))
            l = jnp.sum(p, axis=-1, keepdims=True)
            outs.append(jnp.dot(p.astype(BF16), vwin, preferred_element_type=F32) / l)
        o_ref[:, cols] = (jnp.where(lo, outs[0], outs[1]) * gate[:, cols]).astype(BF16)


def _mixer_d(h, qg, kg, base, bsz, seq, tq=256):
    nt = seq // tq
    kern = functools.partial(_mixer_d_kernel, seq=seq, tq=tq)
    full = lambda shape: pl.BlockSpec(shape, lambda b, t: (0,) * len(shape))
    return pl.pallas_call(
        kern,
        grid=(bsz, nt),
        in_specs=[pl.BlockSpec((tq, GROUP_W), lambda b, t: (b * nt + t, BLK_D_Q)),
                  pl.BlockSpec((seq, GROUP_W), lambda b, t: (b, BLK_D_K)),
                  pl.BlockSpec((seq, GROUP_W), lambda b, t: (b, BLK_D_V)),
                  pl.BlockSpec((tq, GROUP_W), lambda b, t: (b * nt + t, BLK_D_Z)),
                  full((1, GROUP_W)), full((1, GROUP_W)), full((N_HEADS64, 2 * tq + D_LEFT))],
        out_specs=pl.BlockSpec((tq, GROUP_W), lambda b, t: (b * nt + t, 0)),
        out_shape=jax.ShapeDtypeStruct((bsz * seq, GROUP_W), BF16),
        scratch_shapes=[pltpu.VMEM((seq + D_LEFT, GROUP_W), BF16), pltpu.VMEM((seq + D_LEFT, GROUP_W), BF16),
                        pltpu.VMEM((N_HEADS64, tq, tq + D_LEFT), F32)],
        compiler_params=_params(2),
    )(h, h, h, h, qg, kg, base)


def _zeros_cols(rows, n, dtype):
    return jnp.zeros((rows, n), dtype)


def _layout_w_in(w):
    w = w.astype(BF16)
    d = w.shape[0]
    c = lambda name, size, off=0: w[:, _SRC[name] + off:_SRC[name] + off + size]
    z = lambda n: _zeros_cols(d, n, BF16)
    pieces = [c("a_u", 512), c("a_v", 512), c("a_z", 512),
              c("b_q", 512), c("b_iq", 512), c("b_z", 512),
              c("b_k", 64), c("b_k", 64), c("b_v", 64), c("b_v", 64), c("b_ik", 64), c("b_ik", 64),
              c("b_iw", 8), z(120),
              c("c_q", 384), z(128),
              c("c_kv", 128), c("c_kr", 32), z(32), c("c_kr", 32, 32), z(32), z(256),
              c("c_z", 512),
              c("d_q", 512), c("d_k", 512), c("d_v", 512), c("d_z", 512)]
    return jnp.concatenate(pieces, axis=1)


def _rope_layout(v):
    z = jnp.zeros(v.shape[:-1] + (32,), v.dtype)
    return jnp.concatenate([v[..., :32], z, v[..., 32:], z], axis=-1)


def _layout_c(w_qb, w_kvb, q_gain, k_gain, qa_gain):
    wq = w_qb.reshape(Q_LORA, C_HEADS, C_QK)
    wq = jnp.concatenate([wq[..., :C_NOPE], _rope_layout(wq[..., C_NOPE:])], axis=-1)
    wq = wq.reshape(Q_LORA, C_HEADS * 256)
    wq = jnp.concatenate([wq, jnp.zeros((GROUP_W - Q_LORA, C_HEADS * 256), wq.dtype)], axis=0).astype(BF16)
    wkv = w_kvb.reshape(KV_LORA, C_HEADS, 2 * LANES)
    wkv = jnp.concatenate([wkv[..., :C_NOPE].reshape(KV_LORA, -1), wkv[..., C_NOPE:].reshape(KV_LORA, -1)],
                          axis=1).astype(BF16)
    lay = lambda g: jnp.concatenate([g[:C_NOPE], _rope_layout(g[C_NOPE:])])[None, :]
    qag = jnp.concatenate([qa_gain, jnp.zeros((GROUP_W - Q_LORA,), F32)])[None, :]
    return wq, wkv, lay(q_gain), lay(k_gain), qag


def _t5_bucket_static(rel):
    half = T5_BUCKETS // 2
    exact = half // 2
    n = abs(rel)
    if n < exact:
        val = n
    else:
        val = min(exact + (n * n // (exact * exact)).bit_length() - 1, half - 1)
    return (half if rel > 0 else 0) + val


def _t5_tables(t5_bias):
    m = np.arange(512)
    d0 = np.where(m < 256, m, m - 512)
    d1 = np.where(m <= 256, m - 256, m - 768)
    idx0 = np.array([_t5_bucket_static(int(d)) for d in d0], np.int32)
    idx1 = np.array([_t5_bucket_static(int(d)) for d in d1], np.int32)
    far = _t5_bucket_static(-512)
    return t5_bias[idx0].T, t5_bias[idx1].T, t5_bias[far]


def _band_table(rel_bias, tq):
    width = 2 * tq + D_LEFT
    m = np.arange(width)
    dist = np.where(m <= tq + D_LEFT, D_LEFT - m, D_LEFT + width - m)
    idx = np.clip(dist, -REL_CLIP, REL_CLIP) + REL_CLIP
    return rel_bias[idx.astype(np.int32)].T


def _rope_tables(seq):
    inv = ROPE_BASE ** (-jnp.arange(0, C_ROPE, 2, dtype=F32) / C_ROPE)
    ang = jnp.arange(seq, dtype=F32)[:, None] * inv[None, :]
    c, s = jnp.cos(ang), jnp.sin(ang)
    z = jnp.zeros_like(c)
    return jnp.concatenate([c, z, c, z], axis=1), jnp.concatenate([-s, z, s, z], axis=1)


def kernel(x, t5_bias, norm_g, w_in, a_v_gain, a_ws, a_bs, b_q_gain, b_k_gain, c_qa_gain, c_kva_gain,
           c_w_qb, c_w_kvb, c_q_gain, c_k_gain, d_q_gain, d_k_gain, d_rel_bias, w_out):
    bsz, seq, d_model = x.shape
    depth = w_in.shape[0]
    tq = 256
    x2 = x.reshape(bsz * seq, d_model)
    cos, sin = _rope_tables(seq)
    base0, base1, cfar = _t5_tables(t5_bias)
    for l in range(depth):
        h = _inproj(x2, norm_g[l][None, :], _layout_w_in(w_in[l]))
        y_a = _mixer_a(h, a_v_gain[l][None, :], a_ws[l], a_bs[l][:, :, None])
        mask = _b_select(h, bsz, seq, tq)
        y_b = _b_attn(h, mask, cfar, jnp.tile(b_q_gain[l], N_HEADS64)[None, :],
                      jnp.tile(b_k_gain[l], 2)[None, :], base0, base1, bsz, seq, tq)
        wq, wkv, qg, kg, qag = _layout_c(c_w_qb[l], c_w_kvb[l], c_q_gain[l], c_k_gain[l], c_qa_gain[l])
        qc, kc, vc = _c_prep(h, wq, wkv, qag, c_kva_gain[l][None, :], qg, kg, cos, sin, seq)
        y_c = _c_attn(qc, kc, vc, h, bsz, seq, tq)
        y_d = _mixer_d(h, jnp.tile(d_q_gain[l], N_HEADS64)[None, :], jnp.tile(d_k_gain[l], N_HEADS64)[None, :],
                       _band_table(d_rel_bias[l], tq), bsz, seq, tq)
        x2 = _outproj(x2, (y_a, y_b, y_c, y_d), w_out[l].astype(BF16))
    return x2.reshape(bsz, seq, d_model)
```

```python
import functools
import math

import numpy as np
import jax
import jax.numpy as jnp
from jax import lax
from jax.experimental import pallas as pl
from jax.experimental.pallas import tpu as pltpu

F32 = jnp.float32
BF16 = jnp.bfloat16

EPS = 1e-6
NEG = -1e30
INT_MIN = -(2 ** 31)
CHUNK = 64
LANES = 128
GROUP_W = 512
A_GROUPS = 4
GMLP_BLOCK = 128
N_HEADS64 = 8
IDX_SCALE = (8 ** -0.5) * 0.125
TOPK_MAX = 256
T5_BUCKETS = 32
C_HEADS = 4
C_NOPE = 128
C_ROPE = 64
C_QK = 192
Q_LORA = 384
KV_LORA = 128
ROPE_BASE = 10000.0
D_LEFT = 8 * CHUNK
REL_CLIP = 128
VMEM_LIMIT = 56 * 1024 * 1024

BLK_A_U, BLK_A_V, BLK_A_Z = 0, 1, 2
BLK_B_Q, BLK_B_IQ, BLK_B_Z, BLK_SMALL = 3, 4, 5, 6
BLK_C_QKV, BLK_C_Z = 7, 8
BLK_D_Q, BLK_D_K, BLK_D_V, BLK_D_Z = 9, 10, 11, 12
H_COLS = 13 * GROUP_W
UNIT_B_K, UNIT_B_V, UNIT_B_IK, UNIT_KR_IW = (BLK_SMALL * 4 + i for i in range(4))
IW_LANE = 32

_SRC = dict(a_u=0, a_v=512, a_z=1024, b_q=1536, b_k=2048, b_v=2112, b_iq=2176, b_ik=2688,
            b_iw=2752, b_z=2760, c_q=3272, c_kv=3656, c_kr=3784, c_z=3848,
            d_q=4360, d_k=4872, d_v=5384, d_z=5896)


def _params(n_axes):
    return pltpu.CompilerParams(dimension_semantics=("arbitrary",) * n_axes,
                                vmem_limit_bytes=VMEM_LIMIT)


def _gelu(x):
    c = math.sqrt(2.0 / math.pi)
    return x * (0.5 * (1.0 + jnp.tanh(c * (x + 0.044715 * (x * x * x)))))


def _silu(x):
    return x * (1.0 / (1.0 + jnp.exp(-x)))


def _dot_t(a, b):
    return lax.dot_general(a, b, (((1,), (1,)), ((), ())), preferred_element_type=F32)


def _lo_mask(rows):
    return lax.broadcasted_iota(jnp.int32, (rows, LANES), 1) < 64


def _rms_heads64(x, gain, ntiles):
    lo = _lo_mask(x.shape[0])
    tiles = []
    for t in range(ntiles):
        xt = x[:, t * LANES:(t + 1) * LANES]
        sq = xt * xt
        s_lo = jnp.sum(jnp.where(lo, sq, 0.0), axis=-1, keepdims=True)
        s_hi = jnp.sum(jnp.where(lo, 0.0, sq), axis=-1, keepdims=True)
        r = jnp.where(lo, lax.rsqrt(s_lo * (1.0 / 64) + EPS), lax.rsqrt(s_hi * (1.0 / 64) + EPS))
        tiles.append(xt * r * gain[:, t * LANES:(t + 1) * LANES])
    return tiles


def _toeplitz(base_row, rows, width):
    t = jnp.broadcast_to(base_row, (rows, base_row.shape[1]))
    t = pltpu.roll(t, 0, 1, stride=1, stride_axis=0)
    return t[:, :width]


def _inproj_kernel(x_ref, g_ref, w_ref, o_ref, xn_ref):
    @pl.when(pl.program_id(1) == 0)
    def _():
        x = x_ref[...]
        ms = jnp.mean(x * x, axis=-1, keepdims=True)
        xn_ref[...] = (x * lax.rsqrt(ms + EPS) * g_ref[...]).astype(BF16)

    o_ref[...] = jnp.dot(xn_ref[...], w_ref[...], preferred_element_type=F32).astype(BF16)


def _inproj(x2, g, w, tm=1024, tn=1664):
    n, d = x2.shape
    return pl.pallas_call(
        _inproj_kernel,
        grid=(n // tm, H_COLS // tn),
        in_specs=[pl.BlockSpec((tm, d), lambda i, j: (i, 0)),
                  pl.BlockSpec((1, d), lambda i, j: (0, 0)),
                  pl.BlockSpec((d, tn), lambda i, j: (0, j))],
        out_specs=pl.BlockSpec((tm, tn), lambda i, j: (i, j)),
        out_shape=jax.ShapeDtypeStruct((n, H_COLS), BF16),
        scratch_shapes=[pltpu.VMEM((tm, d), BF16)],
        compiler_params=_params(2),
    )(x2, g, w)


def _outproj_kernel(x_ref, ya_ref, yb_ref, yc_ref, yd_ref, w_ref, o_ref):
    acc = x_ref[...]
    for g, y_ref in enumerate((ya_ref, yb_ref, yc_ref, yd_ref)):
        acc = acc + jnp.dot(y_ref[...], w_ref[g * GROUP_W:(g + 1) * GROUP_W, :],
                            preferred_element_type=F32)
    o_ref[...] = acc


def _outproj(x2, ys, w, tm=512):
    n, d = x2.shape
    yspec = pl.BlockSpec((tm, GROUP_W), lambda i: (i, 0))
    return pl.pallas_call(
        _outproj_kernel,
        grid=(n // tm,),
        in_specs=[pl.BlockSpec((tm, d), lambda i: (i, 0)), yspec, yspec, yspec, yspec,
                  pl.BlockSpec((4 * GROUP_W, d), lambda i: (0, 0))],
        out_specs=pl.BlockSpec((tm, d), lambda i: (i, 0)),
        out_shape=jax.ShapeDtypeStruct((n, d), F32),
        compiler_params=_params(1),
    )(x2, *ys, w)


def _mixer_a_kernel(u_ref, v_ref, z_ref, vg_ref, w_ref, b_ref, o_ref):
    tm = u_ref.shape[0]
    u = _gelu(u_ref[...].astype(F32))
    v = _gelu(v_ref[...].astype(F32))
    ms = jnp.mean(v * v, axis=-1, keepdims=True)
    vb = (v * lax.rsqrt(ms + EPS) * vg_ref[...]).astype(BF16)
    gate = _silu(z_ref[...].astype(F32))
    i = lax.broadcasted_iota(jnp.int32, (GMLP_BLOCK, GMLP_BLOCK), 0)
    j = lax.broadcasted_iota(jnp.int32, (GMLP_BLOCK, GMLP_BLOCK), 1)
    keep = (j // CHUNK) <= (i // CHUNK)
    for g in range(A_GROUPS):
        wg = jnp.where(keep, w_ref[g], 0.0).astype(BF16)
        cols = slice(g * LANES, (g + 1) * LANES)
        for blk in range(tm // GMLP_BLOCK):
            rows = slice(blk * GMLP_BLOCK, (blk + 1) * GMLP_BLOCK)
            sg = jnp.dot(wg, vb[rows, cols], preferred_element_type=F32) + b_ref[g]
            o_ref[rows, cols] = (u[rows, cols] * sg * gate[rows, cols]).astype(BF16)


def _mixer_a(h, vg, ws, bs, tm=512):
    n = h.shape[0]
    hspec = lambda blk: pl.BlockSpec((tm, GROUP_W), lambda i, blk=blk: (i, blk))
    return pl.pallas_call(
        _mixer_a_kernel,
        grid=(n // tm,),
        in_specs=[hspec(BLK_A_U), hspec(BLK_A_V), hspec(BLK_A_Z),
                  pl.BlockSpec((1, GROUP_W), lambda i: (0, 0)),
                  pl.BlockSpec((A_GROUPS, GMLP_BLOCK, GMLP_BLOCK), lambda i: (0, 0, 0)),
                  pl.BlockSpec((A_GROUPS, GMLP_BLOCK, 1), lambda i: (0, 0, 0))],
        out_specs=pl.BlockSpec((tm, GROUP_W), lambda i: (i, 0)),
        out_shape=jax.ShapeDtypeStruct((n, GROUP_W), BF16),
        compiler_params=_params(1),
    )(h, h, h, vg, ws, bs)


def _order_key(x):
    return jnp.where(x < 0, x ^ 0x7FFFFFFF, x)


def _b_select_kernel(iq_ref, iw_ref, ik_ref, o_ref, lhs_ref, key_ref, *, seq, tq, topk):
    t_blk = pl.program_id(1)
    nkb = seq // 256
    nblk = ((t_blk + 1) * tq) // 256
    n_interp = 12
    hrows = [slice(h * tq, (h + 1) * tq) for h in range(N_HEADS64)]

    lo_half = _lo_mask(tq)
    w_t = (iw_ref[...].astype(F32) * IDX_SCALE).T
    for h in range(N_HEADS64):
        iqt = iq_ref[:, (h // 2) * LANES:(h // 2 + 1) * LANES]
        sel = lo_half if h % 2 == 0 else jnp.logical_not(lo_half)
        lhs_ref[hrows[h], :] = jnp.where(sel, iqt, jnp.zeros_like(iqt))

    qpos = t_blk * tq + lax.broadcasted_iota(jnp.int32, (256, tq), 1)
    krow = lax.broadcasted_iota(jnp.int32, (256, tq), 0)

    def score_block(kb, amax, masked):
        off = pl.multiple_of(kb * 256, 256)
        ikblk = ik_ref[pl.ds(off, 256), :]
        score = jnp.zeros((256, tq), F32)
        for h in range(N_HEADS64):
            w_h = w_t[IW_LANE + h:IW_LANE + h + 1, :]
            score = score + w_h * jnp.maximum(_dot_t(ikblk, lhs_ref[hrows[h], :]), 0.0)
        score = jnp.where(score == 0.0, 0.0, score)
        key = _order_key(lax.bitcast_convert_type(score, jnp.int32))
        if masked:
            key = jnp.where(((kb * 256 + krow) // CHUNK) <= (qpos // CHUNK), key, INT_MIN)
        key_ref[kb] = key
        return jnp.maximum(amax, jnp.max(jnp.abs(score).reshape(256 // 8, 8, tq), axis=0))

    nfull = (t_blk * tq) // 256
    amax = lax.fori_loop(0, nfull, lambda kb, a: score_block(kb, a, False), jnp.zeros((8, tq), F32))
    amax = lax.fori_loop(nfull, nblk, lambda kb, a: score_block(kb, a, True), amax)
    amax = jnp.max(amax, axis=0, keepdims=True)

    def count(pred):
        def body(kb, acc):
            ones = jnp.where(pred(key_ref[kb], kb), 1.0, 0.0)
            return acc + jnp.sum(ones.reshape(256 // 8, 8, tq), axis=0)
        return jnp.sum(lax.fori_loop(0, nblk, body, jnp.zeros((8, tq), F32)), axis=0, keepdims=True)

    kf = float(topk)
    qrow = t_blk * tq + lax.broadcasted_iota(jnp.int32, (1, tq), 1)
    n_adm = ((qrow // CHUNK + 1) * CHUNK).astype(F32)
    one = jnp.ones((1, tq), jnp.int32)
    f_pos = count(lambda k, kb: k >= one)
    f_nn = count(lambda k, kb: k >= one - 1)
    pos = f_pos > kf
    neg = f_nn < kf
    lo0 = jnp.where(pos, one, _order_key(lax.bitcast_convert_type(-amax, jnp.int32)))
    hi0 = jnp.where(neg, one - 1, _order_key(lax.bitcast_convert_type(amax, jnp.int32)) + 1)
    w_lo0 = jnp.where(pos, f_pos, n_adm) - kf
    w_hi0 = kf - jnp.where(neg, f_nn, 0.0)
    all_sel = n_adm <= kf
    at_zero = jnp.logical_not(pos | neg)
    done0 = jnp.where(all_sel | at_zero | (hi0 == lo0 + 1), 1.0, 0.0)
    thr0 = jnp.where(all_sel, INT_MIN + 1, jnp.where(at_zero, jnp.where(f_pos == kf, one, one - 1), lo0))

    def search_cond(st):
        return jnp.logical_and(st[0] < n_interp + 32, st[1] < 0.5)

    def search_step(st):
        it, _, lo, hi, w_lo, w_hi, side, done, thr = st
        lo_v = lax.bitcast_convert_type(_order_key(lo), F32)
        hi_v = lax.bitcast_convert_type(_order_key(hi), F32)
        c_v = lo_v + (hi_v - lo_v) * (w_lo / (w_lo + w_hi))
        c_interp = _order_key(lax.bitcast_convert_type(c_v, jnp.int32))
        c_mid = (lo >> 1) + (hi >> 1) + (lo & hi & 1)
        cand = jnp.where(it < n_interp, c_interp, c_mid)
        cand = jnp.minimum(jnp.maximum(cand, lo + 1), hi - 1)
        f = count(lambda k, kb: k >= cand)
        live = done < 0.5
        up = f > kf
        hit = f == kf
        new_lo = jnp.where(live & up, cand, lo)
        new_hi = jnp.where(live & jnp.logical_not(up), cand, hi)
        new_w_lo = jnp.where(up, f - kf, jnp.where(side < 0.0, 0.5 * w_lo, w_lo))
        new_w_hi = jnp.where(up, jnp.where(side > 0.0, 0.5 * w_hi, w_hi), kf - f)
        new_side = jnp.where(up, 1.0, -1.0)
        new_thr = jnp.where(live, jnp.where(hit, cand, new_lo), thr)
        new_done = jnp.where(live & (hit | (new_hi == new_lo + 1)), 1.0, done)
        return (it + 1, jnp.min(new_done), new_lo, new_hi, jnp.where(live, new_w_lo, w_lo),
                jnp.where(live, new_w_hi, w_hi), jnp.where(live, new_side, side), new_done, new_thr)

    state = (jnp.int32(0), jnp.min(done0), lo0, hi0, w_lo0, w_hi0, jnp.zeros((1, tq), F32), done0, thr0)
    thr = lax.while_loop(search_cond, search_step, state)[-1]

    need = kf - count(lambda k, kb: k > thr)
    ties = count(lambda k, kb: k == thr)
    any_excess = jnp.max(ties - need) > 0.0

    def write_unused(kb, carry):
        o_ref[0, kb] = jnp.zeros((tq, 256), F32)
        return carry

    lax.fori_loop(nblk, nkb, write_unused, 0)

    @pl.when(jnp.logical_not(any_excess))
    def _():
        def write(kb, carry):
            o_ref[0, kb] = jnp.where(key_ref[kb] >= thr, 1.0, 0.0).T
            return carry

        lax.fori_loop(0, nblk, write, 0)

    @pl.when(any_excess)
    def _():
        def idx_step(it, jmax):
            cand = jmax | lax.shift_left(jnp.int32(1), 10 - it)
            below = count(lambda k, kb: (k == thr) & ((kb * 256 + krow) < cand))
            return jnp.where(below < need, cand, jmax)

        jmax = lax.fori_loop(0, 11, idx_step, jnp.zeros((1, tq), jnp.int32))

        def write(kb, carry):
            k = key_ref[kb]
            keep_tie = (k == thr) & ((kb * 256 + krow) <= jmax)
            o_ref[0, kb] = jnp.where((k > thr) | keep_tie, 1.0, 0.0).T
            return carry

        lax.fori_loop(0, nblk, write, 0)


def _b_select(h, bsz, seq, tq=256):
    nt = seq // tq
    topk = min(TOPK_MAX, seq // 4)
    kern = functools.partial(_b_select_kernel, seq=seq, tq=tq, topk=topk)
    return pl.pallas_call(
        kern,
        grid=(bsz, nt),
        in_specs=[pl.BlockSpec((tq, GROUP_W), lambda b, t: (b * nt + t, BLK_B_IQ)),
                  pl.BlockSpec((tq, LANES), lambda b, t: (b * nt + t, UNIT_KR_IW)),
                  pl.BlockSpec((seq, LANES), lambda b, t: (b, UNIT_B_IK))],
        out_specs=pl.BlockSpec((1, seq // 256, tq, 256), lambda b, t: (b * nt + t, 0, 0, 0)),
        out_shape=jax.ShapeDtypeStruct((bsz * nt, seq // 256, tq, 256), F32),
        scratch_shapes=[pltpu.VMEM((N_HEADS64 * tq, LANES), BF16),
                        pltpu.VMEM((seq // 256, 256, tq), jnp.int32)],
        compiler_params=_params(2),
    )(h, h, h)


def _b_attn_kernel(cfar_ref, q_ref, z_ref, k_ref, v_ref, msk_ref, qg_ref, kg_ref, base0_ref, base1_ref,
                   o_ref, kn_ref, v1_ref, bias_ref, qall_ref, s_ref, mp_ref, acc_ref, *, seq, tq):
    b = pl.program_id(0)
    t_blk = pl.program_id(1)
    hrows = [slice(h * tq, (h + 1) * tq) for h in range(N_HEADS64)]

    @pl.when((b == 0) & (t_blk == 0))
    def _():
        for h in range(N_HEADS64):
            bias_ref[0, hrows[h], :] = jnp.full((tq, 256), cfar_ref[h], F32)
            bias_ref[1, hrows[h], :] = _toeplitz(base1_ref[h:h + 1, :], tq, 256)
            bias_ref[2, hrows[h], :] = _toeplitz(base0_ref[h:h + 1, :], tq, 256)

    @pl.when(t_blk == 0)
    def _():
        lo256 = _lo_mask(256)
        for r in range(seq // 256):
            rows = slice(r * 256, (r + 1) * 256)
            k = k_ref[rows, :].astype(F32)
            ms = jnp.mean(k * k, axis=-1, keepdims=True)
            kn_ref[rows, :] = (k * lax.rsqrt(ms + EPS) * kg_ref[...]).astype(BF16)
            v = v_ref[rows, :]
            v1_ref[rows, :] = jnp.where(lo256, v, jnp.ones_like(v))

    lo = _lo_mask(tq)
    qtiles = _rms_heads64(q_ref[...].astype(F32), qg_ref[...], 4)
    for h in range(N_HEADS64):
        sel = lo if h % 2 == 0 else jnp.logical_not(lo)
        qall_ref[hrows[h], :] = jnp.where(sel, qtiles[h // 2] * 0.125, 0.0).astype(BF16)
    mp_ref[...] = jnp.full(mp_ref.shape, NEG, F32)
    acc_ref[...] = jnp.zeros(acc_ref.shape, F32)
    nblk = t_blk + 1

    def logits_pass(kb, carry):
        off = pl.multiple_of(kb * 256, 256)
        kblk = kn_ref[pl.ds(off, 256), :]
        which = jnp.clip(kb - (t_blk - 2), 0, 2)
        keep = msk_ref[0, kb] > 0.5
        for h in range(N_HEADS64):
            s = jnp.where(keep, _dot_t(qall_ref[hrows[h], :], kblk) + bias_ref[which, hrows[h], :], NEG)
            s_ref[kb, hrows[h], :] = s
            mp_ref[hrows[h], :] = jnp.maximum(mp_ref[hrows[h], :], jnp.maximum(s[:, :LANES], s[:, LANES:]))
        return carry

    lax.fori_loop(0, nblk, logits_pass, 0)
    for h in range(N_HEADS64):
        m = jnp.max(mp_ref[hrows[h], :], axis=-1, keepdims=True)
        mp_ref[hrows[h], :] = jnp.broadcast_to(m, (tq, LANES))

    def value_pass(kb, carry):
        off = pl.multiple_of(kb * 256, 256)
        v1 = v1_ref[pl.ds(off, 256), :]
        for h in range(N_HEADS64):
            m = mp_ref[hrows[h], :]
            p = jnp.exp(s_ref[kb, hrows[h], :] - jnp.concatenate([m, m], axis=1)).astype(BF16)
            acc_ref[hrows[h], :] += jnp.dot(p, v1, preferred_element_type=F32)
        return carry

    lax.fori_loop(0, nblk, value_pass, 0)

    gate = _silu(z_ref[...].astype(F32))
    for t in range(4):
        a_even = acc_ref[hrows[2 * t], :]
        a_odd = acc_ref[hrows[2 * t + 1], :]
        o_even = a_even / pltpu.roll(a_even, 64, 1)
        o_odd = pltpu.roll(a_odd, 64, 1) / a_odd
        cols = slice(t * LANES, (t + 1) * LANES)
        o_ref[:, cols] = (jnp.where(lo, o_even, o_odd) * gate[:, cols]).astype(BF16)


def _b_attn(h, mask, cfar, qg, kg, base0, base1, bsz, seq, tq=256):
    nt = seq // tq
    rows = N_HEADS64 * tq
    kern = functools.partial(_b_attn_kernel, seq=seq, tq=tq)
    full = lambda shape: pl.BlockSpec(shape, lambda b, t: (0,) * len(shape))
    return pl.pallas_call(
        kern,
        grid=(bsz, nt),
        in_specs=[pl.BlockSpec(memory_space=pltpu.SMEM),
                  pl.BlockSpec((tq, GROUP_W), lambda b, t: (b * nt + t, BLK_B_Q)),
                  pl.BlockSpec((tq, GROUP_W), lambda b, t: (b * nt + t, BLK_B_Z)),
                  pl.BlockSpec((seq, LANES), lambda b, t: (b, UNIT_B_K)),
                  pl.BlockSpec((seq, LANES), lambda b, t: (b, UNIT_B_V)),
                  pl.BlockSpec((1, seq // 256, tq, 256), lambda b, t: (b * nt + t, 0, 0, 0)),
                  full((1, GROUP_W)), full((1, LANES)), full((N_HEADS64, 512)), full((N_HEADS64, 512))],
        out_specs=pl.BlockSpec((tq, GROUP_W), lambda b, t: (b * nt + t, 0)),
        out_shape=jax.ShapeDtypeStruct((bsz * seq, GROUP_W), BF16),
        scratch_shapes=[pltpu.VMEM((seq, LANES), BF16), pltpu.VMEM((seq, LANES), BF16),
                        pltpu.VMEM((3, rows, 256), F32), pltpu.VMEM((rows, LANES), BF16),
                        pltpu.VMEM((seq // 256, rows, 256), F32),
                        pltpu.VMEM((rows, LANES), F32), pltpu.VMEM((rows, LANES), F32)],
        compiler_params=_params(2),
    )(cfar, h, h, h, h, mask, qg, kg, base0, base1)


def _rope(tile, cos, sin):
    return tile * cos + pltpu.roll(tile, 64, 1) * sin


def _c_prep_kernel(lat_ref, kr_ref, wq_ref, wkv_ref, qag_ref, kvag_ref, qg_ref, kg_ref, cos_ref, sin_ref,
                   qo_ref, ko_ref, vo_ref):
    cq = lat_ref[:, :Q_LORA].astype(F32)
    ms = jnp.mean(cq * cq, axis=-1, keepdims=True)
    cqn = (cq * lax.rsqrt(ms + EPS) * qag_ref[...]).astype(BF16)
    qpre = jnp.dot(cqn, wq_ref[...], preferred_element_type=F32)
    ckv = lat_ref[:, Q_LORA:].astype(F32)
    ms = jnp.mean(ckv * ckv, axis=-1, keepdims=True)
    ckvn = (ckv * lax.rsqrt(ms + EPS) * kvag_ref[...]).astype(BF16)
    kvpre = jnp.dot(ckvn, wkv_ref[...], preferred_element_type=F32)
    lane = lax.broadcasted_iota(jnp.int32, kr_ref.shape, 1)
    kr = jnp.where((lane % 64) < 32, kr_ref[...].astype(F32), 0.0)
    kr_ss = jnp.sum(kr * kr, axis=-1, keepdims=True)
    cos = cos_ref[...]
    sin = sin_ref[...]
    qg = qg_ref[...]
    kg = kg_ref[...]
    for h in range(C_HEADS):
        qh = qpre[:, h * 256:(h + 1) * 256]
        r = lax.rsqrt(jnp.sum(qh * qh, axis=-1, keepdims=True) * (1.0 / C_QK) + EPS)
        qn = qh * r * qg
        qo_ref[:, h * 256:h * 256 + LANES] = qn[:, :LANES].astype(BF16)
        qo_ref[:, h * 256 + LANES:(h + 1) * 256] = _rope(qn[:, LANES:], cos, sin).astype(BF16)
        kn = kvpre[:, h * LANES:(h + 1) * LANES]
        r = lax.rsqrt((jnp.sum(kn * kn, axis=-1, keepdims=True) + kr_ss) * (1.0 / C_QK) + EPS)
        ko_ref[:, h * 256:h * 256 + LANES] = (kn * r * kg[:, :LANES]).astype(BF16)
        ko_ref[:, h * 256 + LANES:(h + 1) * 256] = _rope(kr * r * kg[:, LANES:], cos, sin).astype(BF16)
    vo_ref[...] = kvpre[:, C_HEADS * LANES:].astype(BF16)


def _c_prep(h, wq, wkv, qag, kvag, qg, kg, cos, sin, seq, tm=512):
    n = h.shape[0]
    ns = seq // tm
    full = lambda shape: pl.BlockSpec(shape, lambda i: (0,) * len(shape))
    return pl.pallas_call(
        _c_prep_kernel,
        grid=(n // tm,),
        in_specs=[pl.BlockSpec((tm, GROUP_W), lambda i: (i, BLK_C_QKV)),
                  pl.BlockSpec((tm, LANES), lambda i: (i, UNIT_KR_IW)),
                  full((Q_LORA, 4 * 256)), full((KV_LORA, 8 * LANES)),
                  full((1, Q_LORA)), full((1, LANES)), full((1, 256)), full((1, 256)),
                  pl.BlockSpec((tm, LANES), lambda i: (i % ns, 0)),
                  pl.BlockSpec((tm, LANES), lambda i: (i % ns, 0))],
        out_specs=[pl.BlockSpec((tm, 4 * 256), lambda i: (i, 0)),
                   pl.BlockSpec((tm, 4 * 256), lambda i: (i, 0)),
                   pl.BlockSpec((tm, GROUP_W), lambda i: (i, 0))],
        out_shape=[jax.ShapeDtypeStruct((n, 4 * 256), BF16), jax.ShapeDtypeStruct((n, 4 * 256), BF16),
                   jax.ShapeDtypeStruct((n, GROUP_W), BF16)],
        compiler_params=_params(1),
    )(h, h, wq, wkv, qag, kvag, qg, kg, cos, sin)


def _c_attn_kernel(q_ref, k_ref, v_ref, z_ref, o_ref, s_ref, mp_ref, lp_ref, acc_ref, *, tq):
    qt = pl.program_id(1)
    scale = C_QK ** -0.5
    hrows = [slice(h * tq, (h + 1) * tq) for h in range(C_HEADS)]
    qchunk = (qt * tq + lax.broadcasted_iota(jnp.int32, (tq, 256), 0)) // CHUNK
    kcol = lax.broadcasted_iota(jnp.int32, (tq, 256), 1)
    mp_ref[...] = jnp.full(mp_ref.shape, NEG, F32)
    lp_ref[...] = jnp.zeros(lp_ref.shape, F32)
    acc_ref[...] = jnp.zeros(acc_ref.shape, F32)
    nblk = (qt + 1) * (tq // 256)

    def logits_pass(kb, carry):
        off = pl.multiple_of(kb * 256, 256)
        keep = ((kb * 256 + kcol) // CHUNK) <= qchunk
        for h in range(C_HEADS):
            cols = slice(h * 256, (h + 1) * 256)
            s = jnp.where(keep, _dot_t(q_ref[:, cols], k_ref[pl.ds(off, 256), cols]) * scale, NEG)
            s_ref[kb, hrows[h], :] = s
            mp_ref[hrows[h], :] = jnp.maximum(mp_ref[hrows[h], :], jnp.maximum(s[:, :LANES], s[:, LANES:]))
        return carry

    lax.fori_loop(0, nblk, logits_pass, 0)
    for h in range(C_HEADS):
        m = jnp.max(mp_ref[hrows[h], :], axis=-1, keepdims=True)
        mp_ref[hrows[h], :] = jnp.broadcast_to(m, (tq, LANES))

    def value_pass(kb, carry):
        off = pl.multiple_of(kb * 256, 256)
        for h in range(C_HEADS):
            m = mp_ref[hrows[h], :]
            p = jnp.exp(s_ref[kb, hrows[h], :] - jnp.concatenate([m, m], axis=1))
            lp_ref[hrows[h], :] += p[:, :LANES] + p[:, LANES:]
            acc_ref[hrows[h], :] += jnp.dot(p.astype(BF16), v_ref[pl.ds(off, 256), h * LANES:(h + 1) * LANES],
                                            preferred_element_type=F32)
        return carry

    lax.fori_loop(0, nblk, value_pass, 0)
    gate = _silu(z_ref[...].astype(F32))
    for h in range(C_HEADS):
        cols = slice(h * LANES, (h + 1) * LANES)
        l = jnp.sum(lp_ref[hrows[h], :], axis=-1, keepdims=True)
        o_ref[:, cols] = (acc_ref[hrows[h], :] / l * gate[:, cols]).astype(BF16)


def _c_attn(qc, kc, vc, h, bsz, seq, tq=256):
    nt = seq // tq
    rows = C_HEADS * tq
    kern = functools.partial(_c_attn_kernel, tq=tq)
    return pl.pallas_call(
        kern,
        grid=(bsz, nt),
        in_specs=[pl.BlockSpec((tq, C_HEADS * 256), lambda b, t: (b * nt + t, 0)),
                  pl.BlockSpec((seq, C_HEADS * 256), lambda b, t: (b, 0)),
                  pl.BlockSpec((seq, GROUP_W), lambda b, t: (b, 0)),
                  pl.BlockSpec((tq, GROUP_W), lambda b, t: (b * nt + t, BLK_C_Z))],
        out_specs=pl.BlockSpec((tq, GROUP_W), lambda b, t: (b * nt + t, 0)),
        out_shape=jax.ShapeDtypeStruct((bsz * seq, GROUP_W), BF16),
        scratch_shapes=[pltpu.VMEM((seq // 256, rows, 256), F32), pltpu.VMEM((rows, LANES), F32),
                        pltpu.VMEM((rows, LANES), F32), pltpu.VMEM((rows, LANES), F32)],
        compiler_params=_params(2),
    )(qc, kc, vc, h)


def _mixer_d_kernel(q_ref, k_ref, v_ref, z_ref, qg_ref, kg_ref, base_ref, o_ref,
                    kpad_ref, vpad_ref, bias_ref, *, seq, tq):
    b = pl.program_id(0)
    qt = pl.program_id(1)
    win = tq + D_LEFT

    @pl.when((b == 0) & (qt == 0))
    def _():
        for h in range(N_HEADS64):
            bias_ref[h] = _toeplitz(base_ref[h:h + 1, :], tq, win)

    @pl.when(qt == 0)
    def _():
        kpad_ref[0:D_LEFT, :] = jnp.zeros((D_LEFT, GROUP_W), BF16)
        vpad_ref[0:D_LEFT, :] = jnp.zeros((D_LEFT, GROUP_W), BF16)
        for r in range(seq // 256):
            rows = slice(r * 256, (r + 1) * 256)
            dst = slice(D_LEFT + r * 256, D_LEFT + (r + 1) * 256)
            tiles = _rms_heads64(k_ref[rows, :].astype(F32), kg_ref[...], 4)
            for t in range(4):
                kpad_ref[dst, t * LANES:(t + 1) * LANES] = tiles[t].astype(BF16)
            vpad_ref[dst, :] = v_ref[rows, :]

    lo = _lo_mask(tq)
    qtiles = _rms_heads64(q_ref[...].astype(F32), qg_ref[...], 4)
    start = pl.multiple_of(qt * tq, tq)
    qc = lax.broadcasted_iota(jnp.int32, (tq, win), 0) // CHUNK
    j = lax.broadcasted_iota(jnp.int32, (tq, win), 1)
    kc = j // CHUNK
    valid = (kc >= qc) & (kc <= qc + D_LEFT // CHUNK) & (j + start >= D_LEFT)
    gate = _silu(z_ref[...].astype(F32))
    for t in range(4):
        cols = slice(t * LANES, (t + 1) * LANES)
        kwin = kpad_ref[pl.ds(start, win), cols]
        vwin = vpad_ref[pl.ds(start, win), cols]
        outs = []
        for half in range(2):
            sel = lo if half == 0 else jnp.logical_not(lo)
            qh = jnp.where(sel, qtiles[t] * 0.125, 0.0).astype(BF16)
            s = jnp.where(valid, _dot_t(qh, kwin) + bias_ref[2 * t + half], NEG)
            p = jnp.exp(s - jnp.max(s, axis=-1, keepdims=True))
            l = jnp.sum(p, axis=-1, keepdims=True)
            outs.append(jnp.dot(p.astype(BF16), vwin, preferred_element_type=F32) / l)
        o_ref[:, cols] = (jnp.where(lo, outs[0], outs[1]) * gate[:, cols]).astype(BF16)


def _mixer_d(h, qg, kg, base, bsz, seq, tq=256):
    nt = seq // tq
    kern = functools.partial(_mixer_d_kernel, seq=seq, tq=tq)
    full = lambda shape: pl.BlockSpec(shape, lambda b, t: (0,) * len(shape))
    return pl.pallas_call(
        kern,
        grid=(bsz, nt),
        in_specs=[pl.BlockSpec((tq, GROUP_W), lambda b, t: (b * nt + t, BLK_D_Q)),
                  pl.BlockSpec((seq, GROUP_W), lambda b, t: (b, BLK_D_K)),
                  pl.BlockSpec((seq, GROUP_W), lambda b, t: (b, BLK_D_V)),
                  pl.BlockSpec((tq, GROUP_W), lambda b, t: (b * nt + t, BLK_D_Z)),
                  full((1, GROUP_W)), full((1, GROUP_W)), full((N_HEADS64, 2 * tq + D_LEFT))],
        out_specs=pl.BlockSpec((tq, GROUP_W), lambda b, t: (b * nt + t, 0)),
        out_shape=jax.ShapeDtypeStruct((bsz * seq, GROUP_W), BF16),
        scratch_shapes=[pltpu.VMEM((seq + D_LEFT, GROUP_W), BF16), pltpu.VMEM((seq + D_LEFT, GROUP_W), BF16),
                        pltpu.VMEM((N_HEADS64, tq, tq + D_LEFT), F32)],
        compiler_params=_params(2),
    )(h, h, h, h, qg, kg, base)


def _zeros_cols(rows, n, dtype):
    return jnp.zeros((rows, n), dtype)


def _layout_w_in(w):
    d = w.shape[0]
    c = lambda name, size, off=0: w[:, _SRC[name] + off:_SRC[name] + off + size]
    z = lambda n: _zeros_cols(d, n, w.dtype)
    pieces = [c("a_u", 512), c("a_v", 512), c("a_z", 512),
              c("b_q", 512), c("b_iq", 512), c("b_z", 512),
              c("b_k", 64), c("b_k", 64), c("b_v", 64), c("b_v", 64), c("b_ik", 64), c("b_ik", 64),
              c("c_kr", 32), c("b_iw", 8), z(24), c("c_kr", 32, 32), z(32),
              c("c_q", 384), c("c_kv", 128), c("c_z", 512),
              c("d_q", 512), c("d_k", 512), c("d_v", 512), c("d_z", 512)]
    return jnp.concatenate(pieces, axis=1).astype(BF16)


def _rope_layout(v):
    z = jnp.zeros(v.shape[:-1] + (32,), v.dtype)
    return jnp.concatenate([v[..., :32], z, v[..., 32:], z], axis=-1)


def _layout_c(w_qb, w_kvb, q_gain, k_gain, qa_gain):
    wq = w_qb.reshape(Q_LORA, C_HEADS, C_QK)
    wq = jnp.concatenate([wq[..., :C_NOPE], _rope_layout(wq[..., C_NOPE:])], axis=-1)
    wq = wq.reshape(Q_LORA, C_HEADS * 256).astype(BF16)
    wkv = w_kvb.reshape(KV_LORA, C_HEADS, 2 * LANES)
    wkv = jnp.concatenate([wkv[..., :C_NOPE].reshape(KV_LORA, -1), wkv[..., C_NOPE:].reshape(KV_LORA, -1)],
                          axis=1).astype(BF16)
    lay = lambda g: jnp.concatenate([g[:C_NOPE], _rope_layout(g[C_NOPE:])])[None, :]
    return wq, wkv, lay(q_gain), lay(k_gain), qa_gain[None, :]


def _t5_bucket_static(rel):
    half = T5_BUCKETS // 2
    exact = half // 2
    n = abs(rel)
    if n < exact:
        val = n
    else:
        val = min(exact + (n * n // (exact * exact)).bit_length() - 1, half - 1)
    return (half if rel > 0 else 0) + val


def _t5_tables(t5_bias):
    m = np.arange(512)
    d0 = np.where(m < 256, m, m - 512)
    d1 = np.where(m <= 256, m - 256, m - 768)
    idx0 = np.array([_t5_bucket_static(int(d)) for d in d0], np.int32)
    idx1 = np.array([_t5_bucket_static(int(d)) for d in d1], np.int32)
    far = _t5_bucket_static(-512)
    return t5_bias[idx0].T, t5_bias[idx1].T, t5_bias[far]


def _band_table(rel_bias, tq):
    width = 2 * tq + D_LEFT
    m = np.arange(width)
    dist = np.where(m <= tq + D_LEFT, D_LEFT - m, D_LEFT + width - m)
    idx = np.clip(dist, -REL_CLIP, REL_CLIP) + REL_CLIP
    return rel_bias[idx.astype(np.int32)].T


def _rope_tables(seq):
    inv = ROPE_BASE ** (-jnp.arange(0, C_ROPE, 2, dtype=F32) / C_ROPE)
    ang = jnp.arange(seq, dtype=F32)[:, None] * inv[None, :]
    c, s = jnp.cos(ang), jnp.sin(ang)
    z = jnp.zeros_like(c)
    return jnp.concatenate([c, z, c, z], axis=1), jnp.concatenate([-s, z, s, z], axis=1)


def kernel(x, t5_bias, norm_g, w_in, a_v_gain, a_ws, a_bs, b_q_gain, b_k_gain, c_qa_gain, c_kva_gain,
           c_w_qb, c_w_kvb, c_q_gain, c_k_gain, d_q_gain, d_k_gain, d_rel_bias, w_out):
    bsz, seq, d_model = x.shape
    depth = w_in.shape[0]
    tq = 256
    x2 = x.reshape(bsz * seq, d_model)
    cos, sin = _rope_tables(seq)
    base0, base1, cfar = _t5_tables(t5_bias)
    for l in range(depth):
        h = _inproj(x2, norm_g[l][None, :], _layout_w_in(w_in[l]))
        y_a = _mixer_a(h, a_v_gain[l][None, :], a_ws[l], a_bs[l][:, :, None])
        mask = _b_select(h, bsz, seq, tq)
        y_b = _b_attn(h, mask, cfar, jnp.tile(b_q_gain[l], N_HEADS64)[None, :],
                      jnp.tile(b_k_gain[l], 2)[None, :], base0, base1, bsz, seq, tq)
        wq, wkv, qg, kg, qag = _layout_c(c_w_qb[l], c_w_kvb[l], c_q_gain[l], c_k_gain[l], c_qa_gain[l])
        qc, kc, vc = _c_prep(h, wq, wkv, qag, c_kva_gain[l][None, :], qg, kg, cos, sin, seq)
        y_c = _c_attn(qc, kc, vc, h, bsz, seq, tq)
        y_d = _mixer_d(h, jnp.tile(d_q_gain[l], N_HEADS64)[None, :], jnp.tile(d_k_gain[l], N_HEADS64)[None, :],
                       _band_table(d_rel_bias[l], tq), bsz, seq, tq)
        x2 = _outproj(x2, (y_a, y_b, y_c, y_d), w_out[l].astype(BF16))
    return x2.reshape(bsz, seq, d_model)
```

```python
import functools
import math

import numpy as np
import jax
import jax.numpy as jnp
from jax import lax
from jax.experimental import pallas as pl
from jax.experimental.pallas import tpu as pltpu

F32 = jnp.float32
BF16 = jnp.bfloat16

EPS = 1e-6
NEG = -1e30
INT_MIN = -(2 ** 31)
CHUNK = 64
LANES = 128
GROUP_W = 512
A_GROUPS = 4
GMLP_BLOCK = 128
N_HEADS64 = 8
IDX_SCALE = (8 ** -0.5) * 0.125
TOPK_MAX = 256
T5_BUCKETS = 32
C_HEADS = 4
C_NOPE = 128
C_ROPE = 64
C_QK = 192
Q_LORA = 384
KV_LORA = 128
ROPE_BASE = 10000.0
D_LEFT = 8 * CHUNK
REL_CLIP = 128
VMEM_LIMIT = 56 * 1024 * 1024

BLK_A_U, BLK_A_V, BLK_A_Z = 0, 1, 2
BLK_B_Q, BLK_B_IQ, BLK_B_Z, BLK_SMALL = 3, 4, 5, 6
BLK_C_QKV, BLK_C_Z = 7, 8
BLK_D_Q, BLK_D_K, BLK_D_V, BLK_D_Z = 9, 10, 11, 12
H_COLS = 13 * GROUP_W
UNIT_B_K, UNIT_B_V, UNIT_B_IK, UNIT_KR_IW = (BLK_SMALL * 4 + i for i in range(4))
IW_LANE = 32

_SRC = dict(a_u=0, a_v=512, a_z=1024, b_q=1536, b_k=2048, b_v=2112, b_iq=2176, b_ik=2688,
            b_iw=2752, b_z=2760, c_q=3272, c_kv=3656, c_kr=3784, c_z=3848,
            d_q=4360, d_k=4872, d_v=5384, d_z=5896)


def _params(n_axes):
    return pltpu.CompilerParams(dimension_semantics=("arbitrary",) * n_axes,
                                vmem_limit_bytes=VMEM_LIMIT)


def _gelu(x):
    c = math.sqrt(2.0 / math.pi)
    return x * (0.5 * (1.0 + jnp.tanh(c * (x + 0.044715 * (x * x * x)))))


def _silu(x):
    return x * (1.0 / (1.0 + jnp.exp(-x)))


def _dot_t(a, b):
    return lax.dot_general(a, b, (((1,), (1,)), ((), ())), preferred_element_type=F32)


def _lo_mask(rows):
    return lax.broadcasted_iota(jnp.int32, (rows, LANES), 1) < 64


def _rms_heads64(x, gain, ntiles):
    lo = _lo_mask(x.shape[0])
    tiles = []
    for t in range(ntiles):
        xt = x[:, t * LANES:(t + 1) * LANES]
        sq = xt * xt
        s_lo = jnp.sum(jnp.where(lo, sq, 0.0), axis=-1, keepdims=True)
        s_hi = jnp.sum(jnp.where(lo, 0.0, sq), axis=-1, keepdims=True)
        r = jnp.where(lo, lax.rsqrt(s_lo * (1.0 / 64) + EPS), lax.rsqrt(s_hi * (1.0 / 64) + EPS))
        tiles.append(xt * r * gain[:, t * LANES:(t + 1) * LANES])
    return tiles


def _toeplitz(base_row, rows, width):
    t = jnp.broadcast_to(base_row, (rows, base_row.shape[1]))
    t = pltpu.roll(t, 0, 1, stride=1, stride_axis=0)
    return t[:, :width]


def _inproj_kernel(x_ref, g_ref, w_ref, o_ref):
    x = x_ref[...]
    ms = jnp.mean(x * x, axis=-1, keepdims=True)
    xn = (x * lax.rsqrt(ms + EPS) * g_ref[...]).astype(BF16)
    for c in range(H_COLS // GROUP_W):
        cols = slice(c * GROUP_W, (c + 1) * GROUP_W)
        o_ref[:, cols] = jnp.dot(xn, w_ref[:, cols], preferred_element_type=F32).astype(BF16)


def _inproj(x2, g, w, tm=512):
    n, d = x2.shape
    return pl.pallas_call(
        _inproj_kernel,
        grid=(n // tm,),
        in_specs=[pl.BlockSpec((tm, d), lambda i: (i, 0)),
                  pl.BlockSpec((1, d), lambda i: (0, 0)),
                  pl.BlockSpec((d, H_COLS), lambda i: (0, 0), pipeline_mode=pl.Buffered(1))],
        out_specs=pl.BlockSpec((tm, H_COLS), lambda i: (i, 0)),
        out_shape=jax.ShapeDtypeStruct((n, H_COLS), BF16),
        compiler_params=_params(1),
    )(x2, g, w)


def _outproj_kernel(x_ref, ya_ref, yb_ref, yc_ref, yd_ref, w_ref, o_ref):
    acc = x_ref[...]
    for g, y_ref in enumerate((ya_ref, yb_ref, yc_ref, yd_ref)):
        acc = acc + jnp.dot(y_ref[...], w_ref[g * GROUP_W:(g + 1) * GROUP_W, :],
                            preferred_element_type=F32)
    o_ref[...] = acc


def _outproj(x2, ys, w, tm=512):
    n, d = x2.shape
    yspec = pl.BlockSpec((tm, GROUP_W), lambda i: (i, 0))
    return pl.pallas_call(
        _outproj_kernel,
        grid=(n // tm,),
        in_specs=[pl.BlockSpec((tm, d), lambda i: (i, 0)), yspec, yspec, yspec, yspec,
                  pl.BlockSpec((4 * GROUP_W, d), lambda i: (0, 0))],
        out_specs=pl.BlockSpec((tm, d), lambda i: (i, 0)),
        out_shape=jax.ShapeDtypeStruct((n, d), F32),
        compiler_params=_params(1),
    )(x2, *ys, w)


def _mixer_a_kernel(u_ref, v_ref, z_ref, vg_ref, w_ref, b_ref, o_ref):
    tm = u_ref.shape[0]
    u = _gelu(u_ref[...].astype(F32))
    v = _gelu(v_ref[...].astype(F32))
    ms = jnp.mean(v * v, axis=-1, keepdims=True)
    vb = (v * lax.rsqrt(ms + EPS) * vg_ref[...]).astype(BF16)
    gate = _silu(z_ref[...].astype(F32))
    i = lax.broadcasted_iota(jnp.int32, (GMLP_BLOCK, GMLP_BLOCK), 0)
    j = lax.broadcasted_iota(jnp.int32, (GMLP_BLOCK, GMLP_BLOCK), 1)
    keep = (j // CHUNK) <= (i // CHUNK)
    for g in range(A_GROUPS):
        wg = jnp.where(keep, w_ref[g], 0.0).astype(BF16)
        cols = slice(g * LANES, (g + 1) * LANES)
        for blk in range(tm // GMLP_BLOCK):
            rows = slice(blk * GMLP_BLOCK, (blk + 1) * GMLP_BLOCK)
            sg = jnp.dot(wg, vb[rows, cols], preferred_element_type=F32) + b_ref[g]
            o_ref[rows, cols] = (u[rows, cols] * sg * gate[rows, cols]).astype(BF16)


def _mixer_a(h, vg, ws, bs, tm=512):
    n = h.shape[0]
    hspec = lambda blk: pl.BlockSpec((tm, GROUP_W), lambda i, blk=blk: (i, blk))
    return pl.pallas_call(
        _mixer_a_kernel,
        grid=(n // tm,),
        in_specs=[hspec(BLK_A_U), hspec(BLK_A_V), hspec(BLK_A_Z),
                  pl.BlockSpec((1, GROUP_W), lambda i: (0, 0)),
                  pl.BlockSpec((A_GROUPS, GMLP_BLOCK, GMLP_BLOCK), lambda i: (0, 0, 0)),
                  pl.BlockSpec((A_GROUPS, GMLP_BLOCK, 1), lambda i: (0, 0, 0))],
        out_specs=pl.BlockSpec((tm, GROUP_W), lambda i: (i, 0)),
        out_shape=jax.ShapeDtypeStruct((n, GROUP_W), BF16),
        compiler_params=_params(1),
    )(h, h, h, vg, ws, bs)


def _order_key(x):
    return jnp.where(x < 0, x ^ 0x7FFFFFFF, x)


def _b_select_kernel(iq_ref, iw_ref, ik_ref, o_ref, lhs_ref, key_ref, *, seq, tq, topk):
    t_blk = pl.program_id(1)
    nkb = seq // 256
    nblk = ((t_blk + 1) * tq) // 256
    n_interp = 12
    hrows = [slice(h * tq, (h + 1) * tq) for h in range(N_HEADS64)]

    lo_half = _lo_mask(tq)
    w_t = (iw_ref[...].astype(F32) * IDX_SCALE).T
    for h in range(N_HEADS64):
        iqt = iq_ref[:, (h // 2) * LANES:(h // 2 + 1) * LANES]
        sel = lo_half if h % 2 == 0 else jnp.logical_not(lo_half)
        lhs_ref[hrows[h], :] = jnp.where(sel, iqt, jnp.zeros_like(iqt))

    qpos = t_blk * tq + lax.broadcasted_iota(jnp.int32, (256, tq), 1)
    krow = lax.broadcasted_iota(jnp.int32, (256, tq), 0)

    def score_block(kb, amax, masked):
        off = pl.multiple_of(kb * 256, 256)
        ikblk = ik_ref[pl.ds(off, 256), :]
        score = jnp.zeros((256, tq), F32)
        for h in range(N_HEADS64):
            w_h = w_t[IW_LANE + h:IW_LANE + h + 1, :]
            score = score + w_h * jnp.maximum(_dot_t(ikblk, lhs_ref[hrows[h], :]), 0.0)
        score = jnp.where(score == 0.0, 0.0, score)
        key = _order_key(lax.bitcast_convert_type(score, jnp.int32))
        if masked:
            key = jnp.where(((kb * 256 + krow) // CHUNK) <= (qpos // CHUNK), key, INT_MIN)
        key_ref[kb] = key
        return jnp.maximum(amax, jnp.max(jnp.abs(score).reshape(256 // 8, 8, tq), axis=0))

    nfull = (t_blk * tq) // 256
    amax = lax.fori_loop(0, nfull, lambda kb, a: score_block(kb, a, False), jnp.zeros((8, tq), F32))
    amax = lax.fori_loop(nfull, nblk, lambda kb, a: score_block(kb, a, True), amax)
    amax = jnp.max(amax, axis=0, keepdims=True)

    def count(pred):
        def body(kb, acc):
            ones = jnp.where(pred(key_ref[kb], kb), 1.0, 0.0)
            return acc + jnp.sum(ones.reshape(256 // 8, 8, tq), axis=0)
        return jnp.sum(lax.fori_loop(0, nblk, body, jnp.zeros((8, tq), F32)), axis=0, keepdims=True)

    kf = float(topk)
    qrow = t_blk * tq + lax.broadcasted_iota(jnp.int32, (1, tq), 1)
    n_adm = ((qrow // CHUNK + 1) * CHUNK).astype(F32)
    one = jnp.ones((1, tq), jnp.int32)
    f_pos = count(lambda k, kb: k >= one)
    f_nn = count(lambda k, kb: k >= one - 1)
    pos = f_pos > kf
    neg = f_nn < kf
    lo0 = jnp.where(pos, one, _order_key(lax.bitcast_convert_type(-amax, jnp.int32)))
    hi0 = jnp.where(neg, one - 1, _order_key(lax.bitcast_convert_type(amax, jnp.int32)) + 1)
    w_lo0 = jnp.where(pos, f_pos, n_adm) - kf
    w_hi0 = kf - jnp.where(neg, f_nn, 0.0)
    all_sel = n_adm <= kf
    at_zero = jnp.logical_not(pos | neg)
    done0 = jnp.where(all_sel | at_zero | (hi0 == lo0 + 1), 1.0, 0.0)
    thr0 = jnp.where(all_sel, INT_MIN + 1, jnp.where(at_zero, jnp.where(f_pos == kf, one, one - 1), lo0))

    def search_cond(st):
        return jnp.logical_and(st[0] < n_interp + 32, st[1] < 0.5)

    def search_step(st):
        it, _, lo, hi, w_lo, w_hi, side, done, thr = st
        lo_v = lax.bitcast_convert_type(_order_key(lo), F32)
        hi_v = lax.bitcast_convert_type(_order_key(hi), F32)
        c_v = lo_v + (hi_v - lo_v) * (w_lo / (w_lo + w_hi))
        c_interp = _order_key(lax.bitcast_convert_type(c_v, jnp.int32))
        c_mid = (lo >> 1) + (hi >> 1) + (lo & hi & 1)
        cand = jnp.where(it < n_interp, c_interp, c_mid)
        cand = jnp.minimum(jnp.maximum(cand, lo + 1), hi - 1)
        f = count(lambda k, kb: k >= cand)
        live = done < 0.5
        up = f > kf
        hit = f == kf
        new_lo = jnp.where(live & up, cand, lo)
        new_hi = jnp.where(live & jnp.logical_not(up), cand, hi)
        new_w_lo = jnp.where(up, f - kf, jnp.where(side < 0.0, 0.5 * w_lo, w_lo))
        new_w_hi = jnp.where(up, jnp.where(side > 0.0, 0.5 * w_hi, w_hi), kf - f)
        new_side = jnp.where(up, 1.0, -1.0)
        new_thr = jnp.where(live, jnp.where(hit, cand, new_lo), thr)
        new_done = jnp.where(live & (hit | (new_hi == new_lo + 1)), 1.0, done)
        return (it + 1, jnp.min(new_done), new_lo, new_hi, jnp.where(live, new_w_lo, w_lo),
                jnp.where(live, new_w_hi, w_hi), jnp.where(live, new_side, side), new_done, new_thr)

    state = (jnp.int32(0), jnp.min(done0), lo0, hi0, w_lo0, w_hi0, jnp.zeros((1, tq), F32), done0, thr0)
    thr = lax.while_loop(search_cond, search_step, state)[-1]

    need = kf - count(lambda k, kb: k > thr)
    ties = count(lambda k, kb: k == thr)
    any_excess = jnp.max(ties - need) > 0.0

    def write_unused(kb, carry):
        o_ref[0, kb] = jnp.zeros((tq, 256), F32)
        return carry

    lax.fori_loop(nblk, nkb, write_unused, 0)

    @pl.when(jnp.logical_not(any_excess))
    def _():
        def write(kb, carry):
            o_ref[0, kb] = jnp.where(key_ref[kb] >= thr, 1.0, 0.0).T
            return carry

        lax.fori_loop(0, nblk, write, 0)

    @pl.when(any_excess)
    def _():
        def idx_step(it, jmax):
            cand = jmax | lax.shift_left(jnp.int32(1), 10 - it)
            below = count(lambda k, kb: (k == thr) & ((kb * 256 + krow) < cand))
            return jnp.where(below < need, cand, jmax)

        jmax = lax.fori_loop(0, 11, idx_step, jnp.zeros((1, tq), jnp.int32))

        def write(kb, carry):
            k = key_ref[kb]
            keep_tie = (k == thr) & ((kb * 256 + krow) <= jmax)
            o_ref[0, kb] = jnp.where((k > thr) | keep_tie, 1.0, 0.0).T
            return carry

        lax.fori_loop(0, nblk, write, 0)


def _b_select(h, bsz, seq, tq=256):
    nt = seq // tq
    topk = min(TOPK_MAX, seq // 4)
    kern = functools.partial(_b_select_kernel, seq=seq, tq=tq, topk=topk)
    return pl.pallas_call(
        kern,
        grid=(bsz, nt),
        in_specs=[pl.BlockSpec((tq, GROUP_W), lambda b, t: (b * nt + t, BLK_B_IQ)),
                  pl.BlockSpec((tq, LANES), lambda b, t: (b * nt + t, UNIT_KR_IW)),
                  pl.BlockSpec((seq, LANES), lambda b, t: (b, UNIT_B_IK))],
        out_specs=pl.BlockSpec((1, seq // 256, tq, 256), lambda b, t: (b * nt + t, 0, 0, 0)),
        out_shape=jax.ShapeDtypeStruct((bsz * nt, seq // 256, tq, 256), F32),
        scratch_shapes=[pltpu.VMEM((N_HEADS64 * tq, LANES), BF16),
                        pltpu.VMEM((seq // 256, 256, tq), jnp.int32)],
        compiler_params=_params(2),
    )(h, h, h)


def _b_attn_kernel(cfar_ref, q_ref, z_ref, k_ref, v_ref, msk_ref, qg_ref, kg_ref, base0_ref, base1_ref,
                   o_ref, kn_ref, v1_ref, bias_ref, qall_ref, s_ref, mp_ref, acc_ref, *, seq, tq):
    b = pl.program_id(0)
    t_blk = pl.program_id(1)
    hrows = [slice(h * tq, (h + 1) * tq) for h in range(N_HEADS64)]

    @pl.when((b == 0) & (t_blk == 0))
    def _():
        for h in range(N_HEADS64):
            bias_ref[0, hrows[h], :] = jnp.full((tq, 256), cfar_ref[h], F32)
            bias_ref[1, hrows[h], :] = _toeplitz(base1_ref[h:h + 1, :], tq, 256)
            bias_ref[2, hrows[h], :] = _toeplitz(base0_ref[h:h + 1, :], tq, 256)

    @pl.when(t_blk == 0)
    def _():
        lo256 = _lo_mask(256)
        for r in range(seq // 256):
            rows = slice(r * 256, (r + 1) * 256)
            k = k_ref[rows, :].astype(F32)
            ms = jnp.mean(k * k, axis=-1, keepdims=True)
            kn_ref[rows, :] = (k * lax.rsqrt(ms + EPS) * kg_ref[...]).astype(BF16)
            v = v_ref[rows, :]
            v1_ref[rows, :] = jnp.where(lo256, v, jnp.ones_like(v))

    lo = _lo_mask(tq)
    qtiles = _rms_heads64(q_ref[...].astype(F32), qg_ref[...], 4)
    for h in range(N_HEADS64):
        sel = lo if h % 2 == 0 else jnp.logical_not(lo)
        qall_ref[hrows[h], :] = jnp.where(sel, qtiles[h // 2] * 0.125, 0.0).astype(BF16)
    mp_ref[...] = jnp.full(mp_ref.shape, NEG, F32)
    acc_ref[...] = jnp.zeros(acc_ref.shape, F32)
    nblk = t_blk + 1

    def logits_blocks(kb0, n):
        for i in range(n):
            kb = kb0 + i
            off = pl.multiple_of(kb * 256, 256)
            kblk = kn_ref[pl.ds(off, 256), :]
            which = jnp.clip(kb - (t_blk - 2), 0, 2)
            keep = msk_ref[0, kb] > 0.5
            for h in range(N_HEADS64):
                s = jnp.where(keep, _dot_t(qall_ref[hrows[h], :], kblk) + bias_ref[which, hrows[h], :], NEG)
                s_ref[kb, hrows[h], :] = s
                mp_ref[hrows[h], :] = jnp.maximum(mp_ref[hrows[h], :], jnp.maximum(s[:, :LANES], s[:, LANES:]))

    def value_blocks(kb0, n):
        off = pl.multiple_of(kb0 * 256, 256)
        v1 = v1_ref[pl.ds(off, n * 256), :]
        for h in range(N_HEADS64):
            m = mp_ref[hrows[h], :]
            mm = jnp.concatenate([m, m], axis=1)
            p = [jnp.exp(s_ref[kb0 + i, hrows[h], :] - mm).astype(BF16) for i in range(n)]
            p = p[0] if n == 1 else jnp.concatenate(p, axis=1)
            acc_ref[hrows[h], :] += jnp.dot(p, v1, preferred_element_type=F32)

    def pairs_then_rest(blocks):
        def pair(i, carry):
            blocks(2 * i, 2)
            return carry

        lax.fori_loop(0, nblk // 2, pair, 0)

        @pl.when(nblk % 2 == 1)
        def _():
            blocks(nblk - 1, 1)

    pairs_then_rest(logits_blocks)
    for h in range(N_HEADS64):
        m = jnp.max(mp_ref[hrows[h], :], axis=-1, keepdims=True)
        mp_ref[hrows[h], :] = jnp.broadcast_to(m, (tq, LANES))
    pairs_then_rest(value_blocks)

    gate = _silu(z_ref[...].astype(F32))
    for t in range(4):
        a_even = acc_ref[hrows[2 * t], :]
        a_odd = acc_ref[hrows[2 * t + 1], :]
        o_even = a_even / pltpu.roll(a_even, 64, 1)
        o_odd = pltpu.roll(a_odd, 64, 1) / a_odd
        cols = slice(t * LANES, (t + 1) * LANES)
        o_ref[:, cols] = (jnp.where(lo, o_even, o_odd) * gate[:, cols]).astype(BF16)


def _b_attn(h, mask, cfar, qg, kg, base0, base1, bsz, seq, tq=256):
    nt = seq // tq
    rows = N_HEADS64 * tq
    kern = functools.partial(_b_attn_kernel, seq=seq, tq=tq)
    full = lambda shape: pl.BlockSpec(shape, lambda b, t: (0,) * len(shape))
    return pl.pallas_call(
        kern,
        grid=(bsz, nt),
        in_specs=[pl.BlockSpec(memory_space=pltpu.SMEM),
                  pl.BlockSpec((tq, GROUP_W), lambda b, t: (b * nt + t, BLK_B_Q)),
                  pl.BlockSpec((tq, GROUP_W), lambda b, t: (b * nt + t, BLK_B_Z)),
                  pl.BlockSpec((seq, LANES), lambda b, t: (b, UNIT_B_K)),
                  pl.BlockSpec((seq, LANES), lambda b, t: (b, UNIT_B_V)),
                  pl.BlockSpec((1, seq // 256, tq, 256), lambda b, t: (b * nt + t, 0, 0, 0)),
                  full((1, GROUP_W)), full((1, LANES)), full((N_HEADS64, 512)), full((N_HEADS64, 512))],
        out_specs=pl.BlockSpec((tq, GROUP_W), lambda b, t: (b * nt + t, 0)),
        out_shape=jax.ShapeDtypeStruct((bsz * seq, GROUP_W), BF16),
        scratch_shapes=[pltpu.VMEM((seq, LANES), BF16), pltpu.VMEM((seq, LANES), BF16),
                        pltpu.VMEM((3, rows, 256), F32), pltpu.VMEM((rows, LANES), BF16),
                        pltpu.VMEM((seq // 256, rows, 256), F32),
                        pltpu.VMEM((rows, LANES), F32), pltpu.VMEM((rows, LANES), F32)],
        compiler_params=_params(2),
    )(cfar, h, h, h, h, mask, qg, kg, base0, base1)


def _rope(tile, cos, sin):
    return tile * cos + pltpu.roll(tile, 64, 1) * sin


def _c_prep_kernel(lat_ref, kr_ref, wq_ref, wkv_ref, qag_ref, kvag_ref, qg_ref, kg_ref, cos_ref, sin_ref,
                   qo_ref, ko_ref, vo_ref):
    cq = lat_ref[:, :Q_LORA].astype(F32)
    ms = jnp.mean(cq * cq, axis=-1, keepdims=True)
    cqn = (cq * lax.rsqrt(ms + EPS) * qag_ref[...]).astype(BF16)
    qpre = jnp.dot(cqn, wq_ref[...], preferred_element_type=F32)
    ckv = lat_ref[:, Q_LORA:].astype(F32)
    ms = jnp.mean(ckv * ckv, axis=-1, keepdims=True)
    ckvn = (ckv * lax.rsqrt(ms + EPS) * kvag_ref[...]).astype(BF16)
    kvpre = jnp.dot(ckvn, wkv_ref[...], preferred_element_type=F32)
    lane = lax.broadcasted_iota(jnp.int32, kr_ref.shape, 1)
    kr = jnp.where((lane % 64) < 32, kr_ref[...].astype(F32), 0.0)
    kr_ss = jnp.sum(kr * kr, axis=-1, keepdims=True)
    cos = cos_ref[...]
    sin = sin_ref[...]
    qg = qg_ref[...]
    kg = kg_ref[...]
    for h in range(C_HEADS):
        qh = qpre[:, h * 256:(h + 1) * 256]
        r = lax.rsqrt(jnp.sum(qh * qh, axis=-1, keepdims=True) * (1.0 / C_QK) + EPS)
        qn = qh * r * qg
        qo_ref[:, h * 256:h * 256 + LANES] = qn[:, :LANES].astype(BF16)
        qo_ref[:, h * 256 + LANES:(h + 1) * 256] = _rope(qn[:, LANES:], cos, sin).astype(BF16)
        kn = kvpre[:, h * LANES:(h + 1) * LANES]
        r = lax.rsqrt((jnp.sum(kn * kn, axis=-1, keepdims=True) + kr_ss) * (1.0 / C_QK) + EPS)
        ko_ref[:, h * 256:h * 256 + LANES] = (kn * r * kg[:, :LANES]).astype(BF16)
        ko_ref[:, h * 256 + LANES:(h + 1) * 256] = _rope(kr * r * kg[:, LANES:], cos, sin).astype(BF16)
    vo_ref[...] = kvpre[:, C_HEADS * LANES:].astype(BF16)


def _c_prep(h, wq, wkv, qag, kvag, qg, kg, cos, sin, seq, tm=512):
    n = h.shape[0]
    ns = seq // tm
    full = lambda shape: pl.BlockSpec(shape, lambda i: (0,) * len(shape))
    return pl.pallas_call(
        _c_prep_kernel,
        grid=(n // tm,),
        in_specs=[pl.BlockSpec((tm, GROUP_W), lambda i: (i, BLK_C_QKV)),
                  pl.BlockSpec((tm, LANES), lambda i: (i, UNIT_KR_IW)),
                  full((Q_LORA, 4 * 256)), full((KV_LORA, 8 * LANES)),
                  full((1, Q_LORA)), full((1, LANES)), full((1, 256)), full((1, 256)),
                  pl.BlockSpec((tm, LANES), lambda i: (i % ns, 0)),
                  pl.BlockSpec((tm, LANES), lambda i: (i % ns, 0))],
        out_specs=[pl.BlockSpec((tm, 4 * 256), lambda i: (i, 0)),
                   pl.BlockSpec((tm, 4 * 256), lambda i: (i, 0)),
                   pl.BlockSpec((tm, GROUP_W), lambda i: (i, 0))],
        out_shape=[jax.ShapeDtypeStruct((n, 4 * 256), BF16), jax.ShapeDtypeStruct((n, 4 * 256), BF16),
                   jax.ShapeDtypeStruct((n, GROUP_W), BF16)],
        compiler_params=_params(1),
    )(h, h, wq, wkv, qag, kvag, qg, kg, cos, sin)


def _c_attn_kernel(q_ref, k_ref, v_ref, z_ref, o_ref, s_ref, mp_ref, lp_ref, acc_ref, *, tq):
    qt = pl.program_id(1)
    scale = C_QK ** -0.5
    hrows = [slice(h * tq, (h + 1) * tq) for h in range(C_HEADS)]
    qchunk = (qt * tq + lax.broadcasted_iota(jnp.int32, (tq, 256), 0)) // CHUNK
    kcol = lax.broadcasted_iota(jnp.int32, (tq, 256), 1)
    mp_ref[...] = jnp.full(mp_ref.shape, NEG, F32)
    lp_ref[...] = jnp.zeros(lp_ref.shape, F32)
    acc_ref[...] = jnp.zeros(acc_ref.shape, F32)
    nfull = (qt * tq) // 256
    nblk = ((qt + 1) * tq) // 256

    def logits_blocks(kb0, n, masked):
        for i in range(n):
            kb = kb0 + i
            off = pl.multiple_of(kb * 256, 256)
            for h in range(C_HEADS):
                cols = slice(h * 256, (h + 1) * 256)
                s = _dot_t(q_ref[:, cols], k_ref[pl.ds(off, 256), cols]) * scale
                if masked:
                    s = jnp.where(((kb * 256 + kcol) // CHUNK) <= qchunk, s, NEG)
                s_ref[kb, hrows[h], :] = s
                mp_ref[hrows[h], :] = jnp.maximum(mp_ref[hrows[h], :], jnp.maximum(s[:, :LANES], s[:, LANES:]))

    def value_blocks(kb0, n):
        off = pl.multiple_of(kb0 * 256, 256)
        for h in range(C_HEADS):
            m = mp_ref[hrows[h], :]
            mm = jnp.concatenate([m, m], axis=1)
            p = [jnp.exp(s_ref[kb0 + i, hrows[h], :] - mm) for i in range(n)]
            lsum = p[0][:, :LANES] + p[0][:, LANES:]
            for pi in p[1:]:
                lsum = lsum + pi[:, :LANES] + pi[:, LANES:]
            lp_ref[hrows[h], :] += lsum
            pb = p[0].astype(BF16) if n == 1 else jnp.concatenate([pi.astype(BF16) for pi in p], axis=1)
            acc_ref[hrows[h], :] += jnp.dot(pb, v_ref[pl.ds(off, n * 256), h * LANES:(h + 1) * LANES],
                                            preferred_element_type=F32)

    def pairs_then_rest(blocks, count):
        def pair(i, carry):
            blocks(2 * i, 2)
            return carry

        lax.fori_loop(0, count // 2, pair, 0)

        @pl.when(count % 2 == 1)
        def _():
            blocks(count - 1, 1)

    pairs_then_rest(lambda kb0, n: logits_blocks(kb0, n, False), nfull)
    lax.fori_loop(nfull, nblk, lambda kb, c: (logits_blocks(kb, 1, True), c)[1], 0)
    for h in range(C_HEADS):
        m = jnp.max(mp_ref[hrows[h], :], axis=-1, keepdims=True)
        mp_ref[hrows[h], :] = jnp.broadcast_to(m, (tq, LANES))
    pairs_then_rest(value_blocks, nblk)
    gate = _silu(z_ref[...].astype(F32))
    for h in range(C_HEADS):
        cols = slice(h * LANES, (h + 1) * LANES)
        l = jnp.sum(lp_ref[hrows[h], :], axis=-1, keepdims=True)
        o_ref[:, cols] = (acc_ref[hrows[h], :] / l * gate[:, cols]).astype(BF16)


def _c_attn(qc, kc, vc, h, bsz, seq, tq=256):
    nt = seq // tq
    rows = C_HEADS * tq
    kern = functools.partial(_c_attn_kernel, tq=tq)
    return pl.pallas_call(
        kern,
        grid=(bsz, nt),
        in_specs=[pl.BlockSpec((tq, C_HEADS * 256), lambda b, t: (b * nt + t, 0)),
                  pl.BlockSpec((seq, C_HEADS * 256), lambda b, t: (b, 0)),
                  pl.BlockSpec((seq, GROUP_W), lambda b, t: (b, 0)),
                  pl.BlockSpec((tq, GROUP_W), lambda b, t: (b * nt + t, BLK_C_Z))],
        out_specs=pl.BlockSpec((tq, GROUP_W), lambda b, t: (b * nt + t, 0)),
        out_shape=jax.ShapeDtypeStruct((bsz * seq, GROUP_W), BF16),
        scratch_shapes=[pltpu.VMEM((seq // 256, rows, 256), F32), pltpu.VMEM((rows, LANES), F32),
                        pltpu.VMEM((rows, LANES), F32), pltpu.VMEM((rows, LANES), F32)],
        compiler_params=_params(2),
    )(qc, kc, vc, h)


def _mixer_d_kernel(q_ref, k_ref, v_ref, z_ref, qg_ref, kg_ref, base_ref, o_ref,
                    kpad_ref, vpad_ref, bias_ref, *, seq, tq):
    b = pl.program_id(0)
    qt = pl.program_id(1)
    win = tq + D_LEFT

    @pl.when((b == 0) & (qt == 0))
    def _():
        for h in range(N_HEADS64):
            bias_ref[h] = _toeplitz(base_ref[h:h + 1, :], tq, win)

    @pl.when(qt == 0)
    def _():
        kpad_ref[0:D_LEFT, :] = jnp.zeros((D_LEFT, GROUP_W), BF16)
        vpad_ref[0:D_LEFT, :] = jnp.zeros((D_LEFT, GROUP_W), BF16)
        for r in range(seq // 256):
            rows = slice(r * 256, (r + 1) * 256)
            dst = slice(D_LEFT + r * 256, D_LEFT + (r + 1) * 256)
            tiles = _rms_heads64(k_ref[rows, :].astype(F32), kg_ref[...], 4)
            for t in range(4):
                kpad_ref[dst, t * LANES:(t + 1) * LANES] = tiles[t].astype(BF16)
            vpad_ref[dst, :] = v_ref[rows, :]

    lo = _lo_mask(tq)
    qtiles = _rms_heads64(q_ref[...].astype(F32), qg_ref[...], 4)
    start = pl.multiple_of(qt * tq, tq)
    qc = lax.broadcasted_iota(jnp.int32, (tq, win), 0) // CHUNK
    j = lax.broadcasted_iota(jnp.int32, (tq, win), 1)
    kc = j // CHUNK
    valid = (kc >= qc) & (kc <= qc + D_LEFT // CHUNK) & (j + start >= D_LEFT)
    gate = _silu(z_ref[...].astype(F32))
    for t in range(4):
        cols = slice(t * LANES, (t + 1) * LANES)
        kwin = kpad_ref[pl.ds(start, win), cols]
        vwin = vpad_ref[pl.ds(start, win), cols]
        outs = []
        for half in range(2):
            sel = lo if half == 0 else jnp.logical_not(lo)
            qh = jnp.where(sel, qtiles[t] * 0.125, 0.0).astype(BF16)
            s = jnp.where(valid, _dot_t(qh, kwin) + bias_ref[2 * t + half], NEG)
            p = jnp.exp(s - jnp.max(s, axis=-1, keepdims=True))
            l = jnp.sum(p, axis=-1, keepdims=True)
            outs.append(jnp.dot(p.astype(BF16), vwin, preferred_element_type=F32) / l)
        o_ref[:, cols] = (jnp.where(lo, outs[0], outs[1]) * gate[:, cols]).astype(BF16)


def _mixer_d(h, qg, kg, base, bsz, seq, tq=256):
    nt = seq // tq
    kern = functools.partial(_mixer_d_kernel, seq=seq, tq=tq)
    full = lambda shape: pl.BlockSpec(shape, lambda b, t: (0,) * len(shape))
    return pl.pallas_call(
        kern,
        grid=(bsz, nt),
        in_specs=[pl.BlockSpec((tq, GROUP_W), lambda b, t: (b * nt + t, BLK_D_Q)),
                  pl.BlockSpec((seq, GROUP_W), lambda b, t: (b, BLK_D_K)),
                  pl.BlockSpec((seq, GROUP_W), lambda b, t: (b, BLK_D_V)),
                  pl.BlockSpec((tq, GROUP_W), lambda b, t: (b * nt + t, BLK_D_Z)),
                  full((1, GROUP_W)), full((1, GROUP_W)), full((N_HEADS64, 2 * tq + D_LEFT))],
        out_specs=pl.BlockSpec((tq, GROUP_W), lambda b, t: (b * nt + t, 0)),
        out_shape=jax.ShapeDtypeStruct((bsz * seq, GROUP_W), BF16),
        scratch_shapes=[pltpu.VMEM((seq + D_LEFT, GROUP_W), BF16), pltpu.VMEM((seq + D_LEFT, GROUP_W), BF16),
                        pltpu.VMEM((N_HEADS64, tq, tq + D_LEFT), F32)],
        compiler_params=_params(2),
    )(h, h, h, h, qg, kg, base)


def _zeros_cols(rows, n, dtype):
    return jnp.zeros((rows, n), dtype)


def _layout_w_in(w):
    d = w.shape[0]
    c = lambda name, size, off=0: w[:, _SRC[name] + off:_SRC[name] + off + size]
    z = lambda n: _zeros_cols(d, n, w.dtype)
    pieces = [c("a_u", 512), c("a_v", 512), c("a_z", 512),
              c("b_q", 512), c("b_iq", 512), c("b_z", 512),
              c("b_k", 64), c("b_k", 64), c("b_v", 64), c("b_v", 64), c("b_ik", 64), c("b_ik", 64),
              c("c_kr", 32), c("b_iw", 8), z(24), c("c_kr", 32, 32), z(32),
              c("c_q", 384), c("c_kv", 128), c("c_z", 512),
              c("d_q", 512), c("d_k", 512), c("d_v", 512), c("d_z", 512)]
    return jnp.concatenate(pieces, axis=1).astype(BF16)


def _rope_layout(v):
    z = jnp.zeros(v.shape[:-1] + (32,), v.dtype)
    return jnp.concatenate([v[..., :32], z, v[..., 32:], z], axis=-1)


def _layout_c(w_qb, w_kvb, q_gain, k_gain, qa_gain):
    wq = w_qb.reshape(Q_LORA, C_HEADS, C_QK)
    wq = jnp.concatenate([wq[..., :C_NOPE], _rope_layout(wq[..., C_NOPE:])], axis=-1)
    wq = wq.reshape(Q_LORA, C_HEADS * 256).astype(BF16)
    wkv = w_kvb.reshape(KV_LORA, C_HEADS, 2 * LANES)
    wkv = jnp.concatenate([wkv[..., :C_NOPE].reshape(KV_LORA, -1), wkv[..., C_NOPE:].reshape(KV_LORA, -1)],
                          axis=1).astype(BF16)
    lay = lambda g: jnp.concatenate([g[:C_NOPE], _rope_layout(g[C_NOPE:])])[None, :]
    return wq, wkv, lay(q_gain), lay(k_gain), qa_gain[None, :]


def _t5_bucket_static(rel):
    half = T5_BUCKETS // 2
    exact = half // 2
    n = abs(rel)
    if n < exact:
        val = n
    else:
        val = min(exact + (n * n // (exact * exact)).bit_length() - 1, half - 1)
    return (half if rel > 0 else 0) + val


def _t5_tables(t5_bias):
    m = np.arange(512)
    d0 = np.where(m < 256, m, m - 512)
    d1 = np.where(m <= 256, m - 256, m - 768)
    idx0 = np.array([_t5_bucket_static(int(d)) for d in d0], np.int32)
    idx1 = np.array([_t5_bucket_static(int(d)) for d in d1], np.int32)
    far = _t5_bucket_static(-512)
    return t5_bias[idx0].T, t5_bias[idx1].T, t5_bias[far]


def _band_table(rel_bias, tq):
    width = 2 * tq + D_LEFT
    m = np.arange(width)
    dist = np.where(m <= tq + D_LEFT, D_LEFT - m, D_LEFT + width - m)
    idx = np.clip(dist, -REL_CLIP, REL_CLIP) + REL_CLIP
    return rel_bias[idx.astype(np.int32)].T


def _rope_tables(seq):
    inv = ROPE_BASE ** (-jnp.arange(0, C_ROPE, 2, dtype=F32) / C_ROPE)
    ang = jnp.arange(seq, dtype=F32)[:, None] * inv[None, :]
    c, s = jnp.cos(ang), jnp.sin(ang)
    z = jnp.zeros_like(c)
    return jnp.concatenate([c, z, c, z], axis=1), jnp.concatenate([-s, z, s, z], axis=1)


def kernel(x, t5_bias, norm_g, w_in, a_v_gain, a_ws, a_bs, b_q_gain, b_k_gain, c_qa_gain, c_kva_gain,
           c_w_qb, c_w_kvb, c_q_gain, c_k_gain, d_q_gain, d_k_gain, d_rel_bias, w_out):
    bsz, seq, d_model = x.shape
    depth = w_in.shape[0]
    tq = 256
    x2 = x.reshape(bsz * seq, d_model)
    cos, sin = _rope_tables(seq)
    base0, base1, cfar = _t5_tables(t5_bias)
    for l in range(depth):
        h = _inproj(x2, norm_g[l][None, :], _layout_w_in(w_in[l]))
        y_a = _mixer_a(h, a_v_gain[l][None, :], a_ws[l], a_bs[l][:, :, None])
        mask = _b_select(h, bsz, seq, tq)
        y_b = _b_attn(h, mask, cfar, jnp.tile(b_q_gain[l], N_HEADS64)[None, :],
                      jnp.tile(b_k_gain[l], 2)[None, :], base0, base1, bsz, seq, tq)
        wq, wkv, qg, kg, qag = _layout_c(c_w_qb[l], c_w_kvb[l], c_q_gain[l], c_k_gain[l], c_qa_gain[l])
        qc, kc, vc = _c_prep(h, wq, wkv, qag, c_kva_gain[l][None, :], qg, kg, cos, sin, seq)
        y_c = _c_attn(qc, kc, vc, h, bsz, seq, tq)
        y_d = _mixer_d(h, jnp.tile(d_q_gain[l], N_HEADS64)[None, :], jnp.tile(d_k_gain[l], N_HEADS64)[None, :],
                       _band_table(d_rel_bias[l], tq), bsz, seq, tq)
        x2 = _outproj(x2, (y_a, y_b, y_c, y_d), w_out[l].astype(BF16))
    return x2.reshape(bsz, seq, d_model)
```

```python
import functools
import math

import numpy as np
import jax
import jax.numpy as jnp
from jax import lax
from jax.experimental import pallas as pl
from jax.experimental.pallas import tpu as pltpu

F32 = jnp.float32
BF16 = jnp.bfloat16

EPS = 1e-6
NEG = -1e30
LOG2E = math.log2(math.e)
INT_MIN = -(2 ** 31)
CHUNK = 64
LANES = 128
GROUP_W = 512
A_GROUPS = 4
GMLP_BLOCK = 128
N_HEADS64 = 8
IDX_SCALE = (8 ** -0.5) * 0.125
TOPK_MAX = 256
T5_BUCKETS = 32
C_HEADS = 4
C_NOPE = 128
C_ROPE = 64
C_QK = 192
Q_LORA = 384
KV_LORA = 128
ROPE_BASE = 10000.0
D_LEFT = 8 * CHUNK
REL_CLIP = 128
VMEM_LIMIT = 56 * 1024 * 1024

BLK_A_U, BLK_A_V, BLK_A_Z = 0, 1, 2
BLK_B_Q, BLK_B_IQ, BLK_B_Z, BLK_SMALL = 3, 4, 5, 6
BLK_C_QKV, BLK_C_Z = 7, 8
BLK_D_Q, BLK_D_K, BLK_D_V, BLK_D_Z = 9, 10, 11, 12
H_COLS = 13 * GROUP_W
UNIT_B_K, UNIT_B_V, UNIT_B_IK, UNIT_KR_IW = (BLK_SMALL * 4 + i for i in range(4))
IW_LANE = 32

_SRC = dict(a_u=0, a_v=512, a_z=1024, b_q=1536, b_k=2048, b_v=2112, b_iq=2176, b_ik=2688,
            b_iw=2752, b_z=2760, c_q=3272, c_kv=3656, c_kr=3784, c_z=3848,
            d_q=4360, d_k=4872, d_v=5384, d_z=5896)


def _params(n_axes):
    return pltpu.CompilerParams(dimension_semantics=("arbitrary",) * n_axes,
                                vmem_limit_bytes=VMEM_LIMIT)


def _gelu(x):
    c = math.sqrt(2.0 / math.pi)
    return x * (0.5 * (1.0 + jnp.tanh(c * (x + 0.044715 * (x * x * x)))))


def _silu(x):
    return x * (1.0 / (1.0 + jnp.exp(-x)))


def _dot_t(a, b):
    return lax.dot_general(a, b, (((1,), (1,)), ((), ())), preferred_element_type=F32)


def _lo_mask(rows):
    return lax.broadcasted_iota(jnp.int32, (rows, LANES), 1) < 64


def _rms_heads64(x, gain, ntiles):
    lo = _lo_mask(x.shape[0])
    tiles = []
    for t in range(ntiles):
        xt = x[:, t * LANES:(t + 1) * LANES]
        sq = xt * xt
        s_lo = jnp.sum(jnp.where(lo, sq, 0.0), axis=-1, keepdims=True)
        s_hi = jnp.sum(jnp.where(lo, 0.0, sq), axis=-1, keepdims=True)
        r = jnp.where(lo, lax.rsqrt(s_lo * (1.0 / 64) + EPS), lax.rsqrt(s_hi * (1.0 / 64) + EPS))
        tiles.append(xt * r * gain[:, t * LANES:(t + 1) * LANES])
    return tiles


def _toeplitz(base_row, rows, width):
    t = jnp.broadcast_to(base_row, (rows, base_row.shape[1]))
    t = pltpu.roll(t, 0, 1, stride=1, stride_axis=0)
    return t[:, :width]


def _inproj_kernel(x_ref, g_ref, w_ref, o_ref):
    x = x_ref[...]
    ms = jnp.mean(x * x, axis=-1, keepdims=True)
    xn = (x * lax.rsqrt(ms + EPS) * g_ref[...]).astype(BF16)
    for c in range(H_COLS // GROUP_W):
        cols = slice(c * GROUP_W, (c + 1) * GROUP_W)
        o_ref[:, cols] = jnp.dot(xn, w_ref[:, cols], preferred_element_type=F32).astype(BF16)


def _inproj(x2, g, w, tm=512):
    n, d = x2.shape
    return pl.pallas_call(
        _inproj_kernel,
        grid=(n // tm,),
        in_specs=[pl.BlockSpec((tm, d), lambda i: (i, 0)),
                  pl.BlockSpec((1, d), lambda i: (0, 0)),
                  pl.BlockSpec((d, H_COLS), lambda i: (0, 0), pipeline_mode=pl.Buffered(1))],
        out_specs=pl.BlockSpec((tm, H_COLS), lambda i: (i, 0)),
        out_shape=jax.ShapeDtypeStruct((n, H_COLS), BF16),
        compiler_params=_params(1),
    )(x2, g, w)


def _outproj_kernel(x_ref, ya_ref, yb_ref, yc_ref, yd_ref, w_ref, o_ref):
    acc = x_ref[...]
    for g, y_ref in enumerate((ya_ref, yb_ref, yc_ref, yd_ref)):
        acc = acc + jnp.dot(y_ref[...], w_ref[g * GROUP_W:(g + 1) * GROUP_W, :],
                            preferred_element_type=F32)
    o_ref[...] = acc


def _outproj(x2, ys, w, tm=512):
    n, d = x2.shape
    yspec = pl.BlockSpec((tm, GROUP_W), lambda i: (i, 0))
    return pl.pallas_call(
        _outproj_kernel,
        grid=(n // tm,),
        in_specs=[pl.BlockSpec((tm, d), lambda i: (i, 0)), yspec, yspec, yspec, yspec,
                  pl.BlockSpec((4 * GROUP_W, d), lambda i: (0, 0))],
        out_specs=pl.BlockSpec((tm, d), lambda i: (i, 0)),
        out_shape=jax.ShapeDtypeStruct((n, d), F32),
        compiler_params=_params(1),
    )(x2, *ys, w)


def _mixer_a_kernel(u_ref, v_ref, z_ref, vg_ref, w_ref, b_ref, o_ref):
    tm = u_ref.shape[0]
    u = _gelu(u_ref[...].astype(F32))
    v = _gelu(v_ref[...].astype(F32))
    ms = jnp.mean(v * v, axis=-1, keepdims=True)
    vb = (v * lax.rsqrt(ms + EPS) * vg_ref[...]).astype(BF16)
    gate = _silu(z_ref[...].astype(F32))
    i = lax.broadcasted_iota(jnp.int32, (GMLP_BLOCK, GMLP_BLOCK), 0)
    j = lax.broadcasted_iota(jnp.int32, (GMLP_BLOCK, GMLP_BLOCK), 1)
    keep = (j // CHUNK) <= (i // CHUNK)
    for g in range(A_GROUPS):
        wg = jnp.where(keep, w_ref[g], 0.0).astype(BF16)
        cols = slice(g * LANES, (g + 1) * LANES)
        for blk in range(tm // GMLP_BLOCK):
            rows = slice(blk * GMLP_BLOCK, (blk + 1) * GMLP_BLOCK)
            sg = jnp.dot(wg, vb[rows, cols], preferred_element_type=F32) + b_ref[g]
            o_ref[rows, cols] = (u[rows, cols] * sg * gate[rows, cols]).astype(BF16)


def _mixer_a(h, vg, ws, bs, tm=512):
    n = h.shape[0]
    hspec = lambda blk: pl.BlockSpec((tm, GROUP_W), lambda i, blk=blk: (i, blk))
    return pl.pallas_call(
        _mixer_a_kernel,
        grid=(n // tm,),
        in_specs=[hspec(BLK_A_U), hspec(BLK_A_V), hspec(BLK_A_Z),
                  pl.BlockSpec((1, GROUP_W), lambda i: (0, 0)),
                  pl.BlockSpec((A_GROUPS, GMLP_BLOCK, GMLP_BLOCK), lambda i: (0, 0, 0)),
                  pl.BlockSpec((A_GROUPS, GMLP_BLOCK, 1), lambda i: (0, 0, 0))],
        out_specs=pl.BlockSpec((tm, GROUP_W), lambda i: (i, 0)),
        out_shape=jax.ShapeDtypeStruct((n, GROUP_W), BF16),
        compiler_params=_params(1),
    )(h, h, h, vg, ws, bs)


def _order_key(x):
    return jnp.where(x < 0, x ^ 0x7FFFFFFF, x)


def _b_select_kernel(iq_ref, iw_ref, ik_ref, o_ref, lhs_ref, key_ref, *, seq, tq, topk):
    t_blk = pl.program_id(1)
    nkb = seq // 256
    nblk = ((t_blk + 1) * tq) // 256
    n_interp = 12
    hrows = [slice(h * tq, (h + 1) * tq) for h in range(N_HEADS64)]

    lo_half = _lo_mask(tq)
    w_t = (iw_ref[...].astype(F32) * IDX_SCALE).T
    for h in range(N_HEADS64):
        iqt = iq_ref[:, (h // 2) * LANES:(h // 2 + 1) * LANES]
        sel = lo_half if h % 2 == 0 else jnp.logical_not(lo_half)
        lhs_ref[hrows[h], :] = jnp.where(sel, iqt, jnp.zeros_like(iqt))

    qpos = t_blk * tq + lax.broadcasted_iota(jnp.int32, (256, tq), 1)
    krow = lax.broadcasted_iota(jnp.int32, (256, tq), 0)

    def score_block(kb, amax, masked):
        off = pl.multiple_of(kb * 256, 256)
        ikblk = ik_ref[pl.ds(off, 256), :]
        score = jnp.zeros((256, tq), F32)
        for h in range(N_HEADS64):
            w_h = w_t[IW_LANE + h:IW_LANE + h + 1, :]
            score = score + w_h * jnp.maximum(_dot_t(ikblk, lhs_ref[hrows[h], :]), 0.0)
        score = jnp.where(score == 0.0, 0.0, score)
        key = _order_key(lax.bitcast_convert_type(score, jnp.int32))
        if masked:
            key = jnp.where(((kb * 256 + krow) // CHUNK) <= (qpos // CHUNK), key, INT_MIN)
        key_ref[kb] = key
        return jnp.maximum(amax, jnp.max(jnp.abs(score).reshape(256 // 8, 8, tq), axis=0))

    nfull = (t_blk * tq) // 256
    amax = lax.fori_loop(0, nfull, lambda kb, a: score_block(kb, a, False), jnp.zeros((8, tq), F32))
    amax = lax.fori_loop(nfull, nblk, lambda kb, a: score_block(kb, a, True), amax)
    amax = jnp.max(amax, axis=0, keepdims=True)

    def count(pred):
        def body(kb, acc):
            ones = jnp.where(pred(key_ref[kb], kb), 1.0, 0.0)
            return acc + jnp.sum(ones.reshape(256 // 8, 8, tq), axis=0)
        return jnp.sum(lax.fori_loop(0, nblk, body, jnp.zeros((8, tq), F32)), axis=0, keepdims=True)

    kf = float(topk)
    qrow = t_blk * tq + lax.broadcasted_iota(jnp.int32, (1, tq), 1)
    n_adm = ((qrow // CHUNK + 1) * CHUNK).astype(F32)
    one = jnp.ones((1, tq), jnp.int32)
    f_pos = count(lambda k, kb: k >= one)
    f_nn = count(lambda k, kb: k >= one - 1)
    pos = f_pos > kf
    neg = f_nn < kf
    lo0 = jnp.where(pos, one, _order_key(lax.bitcast_convert_type(-amax, jnp.int32)))
    hi0 = jnp.where(neg, one - 1, _order_key(lax.bitcast_convert_type(amax, jnp.int32)) + 1)
    w_lo0 = jnp.where(pos, f_pos, n_adm) - kf
    w_hi0 = kf - jnp.where(neg, f_nn, 0.0)
    all_sel = n_adm <= kf
    at_zero = jnp.logical_not(pos | neg)
    done0 = jnp.where(all_sel | at_zero | (hi0 == lo0 + 1), 1.0, 0.0)
    thr0 = jnp.where(all_sel, INT_MIN + 1, jnp.where(at_zero, jnp.where(f_pos == kf, one, one - 1), lo0))

    def search_cond(st):
        return jnp.logical_and(st[0] < n_interp + 32, st[1] < 0.5)

    def search_step(st):
        it, _, lo, hi, w_lo, w_hi, side, done, thr = st
        lo_v = lax.bitcast_convert_type(_order_key(lo), F32)
        hi_v = lax.bitcast_convert_type(_order_key(hi), F32)
        c_v = lo_v + (hi_v - lo_v) * (w_lo / (w_lo + w_hi))
        c_interp = _order_key(lax.bitcast_convert_type(c_v, jnp.int32))
        c_mid = (lo >> 1) + (hi >> 1) + (lo & hi & 1)
        cand = jnp.where(it < n_interp, c_interp, c_mid)
        cand = jnp.minimum(jnp.maximum(cand, lo + 1), hi - 1)
        f = count(lambda k, kb: k >= cand)
        live = done < 0.5
        up = f > kf
        hit = f == kf
        new_lo = jnp.where(live & up, cand, lo)
        new_hi = jnp.where(live & jnp.logical_not(up), cand, hi)
        new_w_lo = jnp.where(up, f - kf, jnp.where(side < 0.0, 0.5 * w_lo, w_lo))
        new_w_hi = jnp.where(up, jnp.where(side > 0.0, 0.5 * w_hi, w_hi), kf - f)
        new_side = jnp.where(up, 1.0, -1.0)
        new_thr = jnp.where(live, jnp.where(hit, cand, new_lo), thr)
        new_done = jnp.where(live & (hit | (new_hi == new_lo + 1)), 1.0, done)
        return (it + 1, jnp.min(new_done), new_lo, new_hi, jnp.where(live, new_w_lo, w_lo),
                jnp.where(live, new_w_hi, w_hi), jnp.where(live, new_side, side), new_done, new_thr)

    state = (jnp.int32(0), jnp.min(done0), lo0, hi0, w_lo0, w_hi0, jnp.zeros((1, tq), F32), done0, thr0)
    thr = lax.while_loop(search_cond, search_step, state)[-1]

    need = kf - count(lambda k, kb: k > thr)
    ties = count(lambda k, kb: k == thr)
    any_excess = jnp.max(ties - need) > 0.0

    def write_unused(kb, carry):
        o_ref[0, kb] = jnp.zeros((tq, 256), F32)
        return carry

    lax.fori_loop(nblk, nkb, write_unused, 0)

    @pl.when(jnp.logical_not(any_excess))
    def _():
        def write(kb, carry):
            o_ref[0, kb] = jnp.where(key_ref[kb] >= thr, 1.0, 0.0).T
            return carry

        lax.fori_loop(0, nblk, write, 0)

    @pl.when(any_excess)
    def _():
        def idx_step(it, jmax):
            cand = jmax | lax.shift_left(jnp.int32(1), 10 - it)
            below = count(lambda k, kb: (k == thr) & ((kb * 256 + krow) < cand))
            return jnp.where(below < need, cand, jmax)

        jmax = lax.fori_loop(0, 11, idx_step, jnp.zeros((1, tq), jnp.int32))

        def write(kb, carry):
            k = key_ref[kb]
            keep_tie = (k == thr) & ((kb * 256 + krow) <= jmax)
            o_ref[0, kb] = jnp.where((k > thr) | keep_tie, 1.0, 0.0).T
            return carry

        lax.fori_loop(0, nblk, write, 0)


def _b_select(h, bsz, seq, tq=256):
    nt = seq // tq
    topk = min(TOPK_MAX, seq // 4)
    kern = functools.partial(_b_select_kernel, seq=seq, tq=tq, topk=topk)
    return pl.pallas_call(
        kern,
        grid=(bsz, nt),
        in_specs=[pl.BlockSpec((tq, GROUP_W), lambda b, t: (b * nt + t, BLK_B_IQ)),
                  pl.BlockSpec((tq, LANES), lambda b, t: (b * nt + t, UNIT_KR_IW)),
                  pl.BlockSpec((seq, LANES), lambda b, t: (b, UNIT_B_IK))],
        out_specs=pl.BlockSpec((1, seq // 256, tq, 256), lambda b, t: (b * nt + t, 0, 0, 0)),
        out_shape=jax.ShapeDtypeStruct((bsz * nt, seq // 256, tq, 256), F32),
        scratch_shapes=[pltpu.VMEM((N_HEADS64 * tq, LANES), BF16),
                        pltpu.VMEM((seq // 256, 256, tq), jnp.int32)],
        compiler_params=_params(2),
    )(h, h, h)


def _b_attn_kernel(cfar_ref, q_ref, z_ref, k_ref, v_ref, msk_ref, qg_ref, kg_ref, base0_ref, base1_ref,
                   o_ref, kn_ref, v1_ref, bias_ref, qall_ref, s_ref, mp_ref, acc_ref, *, seq, tq):
    b = pl.program_id(0)
    t_blk = pl.program_id(1)
    hrows = [slice(h * tq, (h + 1) * tq) for h in range(N_HEADS64)]

    @pl.when((b == 0) & (t_blk == 0))
    def _():
        for h in range(N_HEADS64):
            bias_ref[0, hrows[h], :] = jnp.full((tq, 256), cfar_ref[h] * LOG2E, F32)
            bias_ref[1, hrows[h], :] = _toeplitz(base1_ref[h:h + 1, :], tq, 256) * LOG2E
            bias_ref[2, hrows[h], :] = _toeplitz(base0_ref[h:h + 1, :], tq, 256) * LOG2E

    @pl.when(t_blk == 0)
    def _():
        lo256 = _lo_mask(256)
        for r in range(seq // 256):
            rows = slice(r * 256, (r + 1) * 256)
            k = k_ref[rows, :].astype(F32)
            ms = jnp.mean(k * k, axis=-1, keepdims=True)
            kn_ref[rows, :] = (k * lax.rsqrt(ms + EPS) * kg_ref[...]).astype(BF16)
            v = v_ref[rows, :]
            v1_ref[rows, :] = jnp.where(lo256, v, jnp.ones_like(v))

    lo = _lo_mask(tq)
    qtiles = _rms_heads64(q_ref[...].astype(F32), qg_ref[...], 4)
    for h in range(N_HEADS64):
        sel = lo if h % 2 == 0 else jnp.logical_not(lo)
        qall_ref[hrows[h], :] = jnp.where(sel, qtiles[h // 2] * (0.125 * LOG2E), 0.0).astype(BF16)
    mp_ref[...] = jnp.full(mp_ref.shape, NEG, F32)
    acc_ref[...] = jnp.zeros(acc_ref.shape, F32)
    nblk = t_blk + 1

    def logits_blocks(kb0, n):
        for i in range(n):
            kb = kb0 + i
            off = pl.multiple_of(kb * 256, 256)
            kblk = kn_ref[pl.ds(off, 256), :]
            which = jnp.clip(kb - (t_blk - 2), 0, 2)
            keep = msk_ref[0, kb] > 0.5
            for h in range(N_HEADS64):
                s = jnp.where(keep, _dot_t(qall_ref[hrows[h], :], kblk) + bias_ref[which, hrows[h], :], NEG)
                s_ref[kb, hrows[h], :] = s
                mp_ref[hrows[h], :] = jnp.maximum(mp_ref[hrows[h], :], jnp.maximum(s[:, :LANES], s[:, LANES:]))

    def value_blocks(kb0, n):
        off = pl.multiple_of(kb0 * 256, 256)
        v1 = v1_ref[pl.ds(off, n * 256), :]
        for h in range(N_HEADS64):
            m = mp_ref[hrows[h], :]
            mm = jnp.concatenate([m, m], axis=1)
            p = [jnp.exp2(s_ref[kb0 + i, hrows[h], :] - mm).astype(BF16) for i in range(n)]
            p = p[0] if n == 1 else jnp.concatenate(p, axis=1)
            acc_ref[hrows[h], :] += jnp.dot(p, v1, preferred_element_type=F32)

    def pairs_then_rest(blocks):
        def pair(i, carry):
            blocks(2 * i, 2)
            return carry

        lax.fori_loop(0, nblk // 2, pair, 0)

        @pl.when(nblk % 2 == 1)
        def _():
            blocks(nblk - 1, 1)

    pairs_then_rest(logits_blocks)
    for h in range(N_HEADS64):
        m = jnp.max(mp_ref[hrows[h], :], axis=-1, keepdims=True)
        mp_ref[hrows[h], :] = jnp.broadcast_to(m, (tq, LANES))
    pairs_then_rest(value_blocks)

    gate = _silu(z_ref[...].astype(F32))
    for t in range(4):
        a_even = acc_ref[hrows[2 * t], :]
        a_odd = acc_ref[hrows[2 * t + 1], :]
        o_even = a_even / pltpu.roll(a_even, 64, 1)
        o_odd = pltpu.roll(a_odd, 64, 1) / a_odd
        cols = slice(t * LANES, (t + 1) * LANES)
        o_ref[:, cols] = (jnp.where(lo, o_even, o_odd) * gate[:, cols]).astype(BF16)


def _b_attn(h, mask, cfar, qg, kg, base0, base1, bsz, seq, tq=256):
    nt = seq // tq
    rows = N_HEADS64 * tq
    kern = functools.partial(_b_attn_kernel, seq=seq, tq=tq)
    full = lambda shape: pl.BlockSpec(shape, lambda b, t: (0,) * len(shape))
    return pl.pallas_call(
        kern,
        grid=(bsz, nt),
        in_specs=[pl.BlockSpec(memory_space=pltpu.SMEM),
                  pl.BlockSpec((tq, GROUP_W), lambda b, t: (b * nt + t, BLK_B_Q)),
                  pl.BlockSpec((tq, GROUP_W), lambda b, t: (b * nt + t, BLK_B_Z)),
                  pl.BlockSpec((seq, LANES), lambda b, t: (b, UNIT_B_K)),
                  pl.BlockSpec((seq, LANES), lambda b, t: (b, UNIT_B_V)),
                  pl.BlockSpec((1, seq // 256, tq, 256), lambda b, t: (b * nt + t, 0, 0, 0)),
                  full((1, GROUP_W)), full((1, LANES)), full((N_HEADS64, 512)), full((N_HEADS64, 512))],
        out_specs=pl.BlockSpec((tq, GROUP_W), lambda b, t: (b * nt + t, 0)),
        out_shape=jax.ShapeDtypeStruct((bsz * seq, GROUP_W), BF16),
        scratch_shapes=[pltpu.VMEM((seq, LANES), BF16), pltpu.VMEM((seq, LANES), BF16),
                        pltpu.VMEM((3, rows, 256), F32), pltpu.VMEM((rows, LANES), BF16),
                        pltpu.VMEM((seq // 256, rows, 256), F32),
                        pltpu.VMEM((rows, LANES), F32), pltpu.VMEM((rows, LANES), F32)],
        compiler_params=_params(2),
    )(cfar, h, h, h, h, mask, qg, kg, base0, base1)


def _rope(tile, cos, sin):
    return tile * cos + pltpu.roll(tile, 64, 1) * sin


def _c_prep_kernel(lat_ref, kr_ref, wq_ref, wkv_ref, qag_ref, kvag_ref, qg_ref, kg_ref, cos_ref, sin_ref,
                   qo_ref, ko_ref, vo_ref):
    cq = lat_ref[:, :Q_LORA].astype(F32)
    ms = jnp.mean(cq * cq, axis=-1, keepdims=True)
    cqn = (cq * lax.rsqrt(ms + EPS) * qag_ref[...]).astype(BF16)
    qpre = jnp.dot(cqn, wq_ref[...], preferred_element_type=F32)
    ckv = lat_ref[:, Q_LORA:].astype(F32)
    ms = jnp.mean(ckv * ckv, axis=-1, keepdims=True)
    ckvn = (ckv * lax.rsqrt(ms + EPS) * kvag_ref[...]).astype(BF16)
    kvpre = jnp.dot(ckvn, wkv_ref[...], preferred_element_type=F32)
    lane = lax.broadcasted_iota(jnp.int32, kr_ref.shape, 1)
    kr = jnp.where((lane % 64) < 32, kr_ref[...].astype(F32), 0.0)
    kr_ss = jnp.sum(kr * kr, axis=-1, keepdims=True)
    cos = cos_ref[...]
    sin = sin_ref[...]
    qg = qg_ref[...]
    kg = kg_ref[...]
    for h in range(C_HEADS):
        qh = qpre[:, h * 256:(h + 1) * 256]
        r = lax.rsqrt(jnp.sum(qh * qh, axis=-1, keepdims=True) * (1.0 / C_QK) + EPS)
        qn = qh * r * qg
        qo_ref[:, h * 256:h * 256 + LANES] = qn[:, :LANES].astype(BF16)
        qo_ref[:, h * 256 + LANES:(h + 1) * 256] = _rope(qn[:, LANES:], cos, sin).astype(BF16)
        kn = kvpre[:, h * LANES:(h + 1) * LANES]
        r = lax.rsqrt((jnp.sum(kn * kn, axis=-1, keepdims=True) + kr_ss) * (1.0 / C_QK) + EPS)
        ko_ref[:, h * 256:h * 256 + LANES] = (kn * r * kg[:, :LANES]).astype(BF16)
        ko_ref[:, h * 256 + LANES:(h + 1) * 256] = _rope(kr * r * kg[:, LANES:], cos, sin).astype(BF16)
    vo_ref[...] = kvpre[:, C_HEADS * LANES:].astype(BF16)


def _c_prep(h, wq, wkv, qag, kvag, qg, kg, cos, sin, seq, tm=512):
    n = h.shape[0]
    ns = seq // tm
    full = lambda shape: pl.BlockSpec(shape, lambda i: (0,) * len(shape))
    return pl.pallas_call(
        _c_prep_kernel,
        grid=(n // tm,),
        in_specs=[pl.BlockSpec((tm, GROUP_W), lambda i: (i, BLK_C_QKV)),
                  pl.BlockSpec((tm, LANES), lambda i: (i, UNIT_KR_IW)),
                  full((Q_LORA, 4 * 256)), full((KV_LORA, 8 * LANES)),
                  full((1, Q_LORA)), full((1, LANES)), full((1, 256)), full((1, 256)),
                  pl.BlockSpec((tm, LANES), lambda i: (i % ns, 0)),
                  pl.BlockSpec((tm, LANES), lambda i: (i % ns, 0))],
        out_specs=[pl.BlockSpec((tm, 4 * 256), lambda i: (i, 0)),
                   pl.BlockSpec((tm, 4 * 256), lambda i: (i, 0)),
                   pl.BlockSpec((tm, GROUP_W), lambda i: (i, 0))],
        out_shape=[jax.ShapeDtypeStruct((n, 4 * 256), BF16), jax.ShapeDtypeStruct((n, 4 * 256), BF16),
                   jax.ShapeDtypeStruct((n, GROUP_W), BF16)],
        compiler_params=_params(1),
    )(h, h, wq, wkv, qag, kvag, qg, kg, cos, sin)


def _c_attn_kernel(q_ref, k_ref, v_ref, z_ref, o_ref, s_ref, mp_ref, lp_ref, acc_ref, *, tq):
    qt = pl.program_id(1)
    scale = C_QK ** -0.5 * LOG2E
    hrows = [slice(h * tq, (h + 1) * tq) for h in range(C_HEADS)]
    qchunk = (qt * tq + lax.broadcasted_iota(jnp.int32, (tq, 256), 0)) // CHUNK
    kcol = lax.broadcasted_iota(jnp.int32, (tq, 256), 1)
    mp_ref[...] = jnp.full(mp_ref.shape, NEG, F32)
    lp_ref[...] = jnp.zeros(lp_ref.shape, F32)
    acc_ref[...] = jnp.zeros(acc_ref.shape, F32)
    nfull = (qt * tq) // 256
    nblk = ((qt + 1) * tq) // 256

    def logits_blocks(kb0, n, masked):
        for i in range(n):
            kb = kb0 + i
            off = pl.multiple_of(kb * 256, 256)
            for h in range(C_HEADS):
                cols = slice(h * 256, (h + 1) * 256)
                s = _dot_t(q_ref[:, cols], k_ref[pl.ds(off, 256), cols]) * scale
                if masked:
                    s = jnp.where(((kb * 256 + kcol) // CHUNK) <= qchunk, s, NEG)
                s_ref[kb, hrows[h], :] = s
                mp_ref[hrows[h], :] = jnp.maximum(mp_ref[hrows[h], :], jnp.maximum(s[:, :LANES], s[:, LANES:]))

    def value_blocks(kb0, n):
        off = pl.multiple_of(kb0 * 256, 256)
        for h in range(C_HEADS):
            m = mp_ref[hrows[h], :]
            mm = jnp.concatenate([m, m], axis=1)
            p = [jnp.exp2(s_ref[kb0 + i, hrows[h], :] - mm) for i in range(n)]
            lsum = p[0][:, :LANES] + p[0][:, LANES:]
            for pi in p[1:]:
                lsum = lsum + pi[:, :LANES] + pi[:, LANES:]
            lp_ref[hrows[h], :] += lsum
            pb = p[0].astype(BF16) if n == 1 else jnp.concatenate([pi.astype(BF16) for pi in p], axis=1)
            acc_ref[hrows[h], :] += jnp.dot(pb, v_ref[pl.ds(off, n * 256), h * LANES:(h + 1) * LANES],
                                            preferred_element_type=F32)

    def pairs_then_rest(blocks, count):
        def pair(i, carry):
            blocks(2 * i, 2)
            return carry

        lax.fori_loop(0, count // 2, pair, 0)

        @pl.when(count % 2 == 1)
        def _():
            blocks(count - 1, 1)

    pairs_then_rest(lambda kb0, n: logits_blocks(kb0, n, False), nfull)
    lax.fori_loop(nfull, nblk, lambda kb, c: (logits_blocks(kb, 1, True), c)[1], 0)
    for h in range(C_HEADS):
        m = jnp.max(mp_ref[hrows[h], :], axis=-1, keepdims=True)
        mp_ref[hrows[h], :] = jnp.broadcast_to(m, (tq, LANES))
    pairs_then_rest(value_blocks, nblk)
    gate = _silu(z_ref[...].astype(F32))
    for h in range(C_HEADS):
        cols = slice(h * LANES, (h + 1) * LANES)
        l = jnp.sum(lp_ref[hrows[h], :], axis=-1, keepdims=True)
        o_ref[:, cols] = (acc_ref[hrows[h], :] / l * gate[:, cols]).astype(BF16)


def _c_attn(qc, kc, vc, h, bsz, seq, tq=256):
    nt = seq // tq
    rows = C_HEADS * tq
    kern = functools.partial(_c_attn_kernel, tq=tq)
    return pl.pallas_call(
        kern,
        grid=(bsz, nt),
        in_specs=[pl.BlockSpec((tq, C_HEADS * 256), lambda b, t: (b * nt + t, 0)),
                  pl.BlockSpec((seq, C_HEADS * 256), lambda b, t: (b, 0)),
                  pl.BlockSpec((seq, GROUP_W), lambda b, t: (b, 0)),
                  pl.BlockSpec((tq, GROUP_W), lambda b, t: (b * nt + t, BLK_C_Z))],
        out_specs=pl.BlockSpec((tq, GROUP_W), lambda b, t: (b * nt + t, 0)),
        out_shape=jax.ShapeDtypeStruct((bsz * seq, GROUP_W), BF16),
        scratch_shapes=[pltpu.VMEM((seq // 256, rows, 256), F32), pltpu.VMEM((rows, LANES), F32),
                        pltpu.VMEM((rows, LANES), F32), pltpu.VMEM((rows, LANES), F32)],
        compiler_params=_params(2),
    )(qc, kc, vc, h)


def _mixer_d_kernel(q_ref, k_ref, v_ref, z_ref, qg_ref, kg_ref, base_ref, o_ref,
                    kpad_ref, vpad_ref, bias_ref, s_ref, mp_ref, *, seq, tq):
    b = pl.program_id(0)
    qt = pl.program_id(1)
    win = tq + D_LEFT

    @pl.when((b == 0) & (qt == 0))
    def _():
        for h in range(N_HEADS64):
            bias_ref[h] = _toeplitz(base_ref[h:h + 1, :], tq, win) * LOG2E

    @pl.when(qt == 0)
    def _():
        kpad_ref[0:D_LEFT, :] = jnp.zeros((D_LEFT, GROUP_W), BF16)
        vpad_ref[0:D_LEFT, :] = jnp.zeros((D_LEFT, GROUP_W), BF16)
        for r in range(seq // 256):
            rows = slice(r * 256, (r + 1) * 256)
            dst = slice(D_LEFT + r * 256, D_LEFT + (r + 1) * 256)
            tiles = _rms_heads64(k_ref[rows, :].astype(F32), kg_ref[...], 4)
            for t in range(4):
                kpad_ref[dst, t * LANES:(t + 1) * LANES] = tiles[t].astype(BF16)
            vpad_ref[dst, :] = v_ref[rows, :]

    lo = _lo_mask(tq)
    qtiles = _rms_heads64(q_ref[...].astype(F32), qg_ref[...], 4)
    start = pl.multiple_of(qt * tq, tq)
    qc = lax.broadcasted_iota(jnp.int32, (tq, win), 0) // CHUNK
    j = lax.broadcasted_iota(jnp.int32, (tq, win), 1)
    kc = j // CHUNK
    valid = (kc >= qc) & (kc <= qc + D_LEFT // CHUNK) & (j + start >= D_LEFT)
    gate = _silu(z_ref[...].astype(F32))
    ntile = win // LANES

    def lane_tiles(x):
        return [x[:, i * LANES:(i + 1) * LANES] for i in range(ntile)]

    for t in range(4):
        kwin = kpad_ref[pl.ds(start, win), t * LANES:(t + 1) * LANES]
        for half in range(2):
            h = 2 * t + half
            sel = lo if half == 0 else jnp.logical_not(lo)
            qh = jnp.where(sel, qtiles[t] * (0.125 * LOG2E), 0.0).astype(BF16)
            s = jnp.where(valid, _dot_t(qh, kwin) + bias_ref[h], NEG)
            s_ref[h] = s
            mp_ref[h] = functools.reduce(jnp.maximum, lane_tiles(s))
    for h in range(N_HEADS64):
        mp_ref[h] = jnp.broadcast_to(jnp.max(mp_ref[h], axis=-1, keepdims=True), (tq, LANES))
    for t in range(4):
        cols = slice(t * LANES, (t + 1) * LANES)
        vwin = vpad_ref[pl.ds(start, win), cols]
        outs = []
        for half in range(2):
            h = 2 * t + half
            m = mp_ref[h]
            p = jnp.exp2(s_ref[h] - jnp.concatenate([m] * ntile, axis=1))
            l = jnp.sum(functools.reduce(jnp.add, lane_tiles(p)), axis=-1, keepdims=True)
            outs.append(jnp.dot(p.astype(BF16), vwin, preferred_element_type=F32) / l)
        o_ref[:, cols] = (jnp.where(lo, outs[0], outs[1]) * gate[:, cols]).astype(BF16)


def _mixer_d(h, qg, kg, base, bsz, seq, tq=256):
    nt = seq // tq
    kern = functools.partial(_mixer_d_kernel, seq=seq, tq=tq)
    full = lambda shape: pl.BlockSpec(shape, lambda b, t: (0,) * len(shape))
    return pl.pallas_call(
        kern,
        grid=(bsz, nt),
        in_specs=[pl.BlockSpec((tq, GROUP_W), lambda b, t: (b * nt + t, BLK_D_Q)),
                  pl.BlockSpec((seq, GROUP_W), lambda b, t: (b, BLK_D_K)),
                  pl.BlockSpec((seq, GROUP_W), lambda b, t: (b, BLK_D_V)),
                  pl.BlockSpec((tq, GROUP_W), lambda b, t: (b * nt + t, BLK_D_Z)),
                  full((1, GROUP_W)), full((1, GROUP_W)), full((N_HEADS64, 2 * tq + D_LEFT))],
        out_specs=pl.BlockSpec((tq, GROUP_W), lambda b, t: (b * nt + t, 0)),
        out_shape=jax.ShapeDtypeStruct((bsz * seq, GROUP_W), BF16),
        scratch_shapes=[pltpu.VMEM((seq + D_LEFT, GROUP_W), BF16), pltpu.VMEM((seq + D_LEFT, GROUP_W), BF16),
                        pltpu.VMEM((N_HEADS64, tq, tq + D_LEFT), F32),
                        pltpu.VMEM((N_HEADS64, tq, tq + D_LEFT), F32), pltpu.VMEM((N_HEADS64, tq, LANES), F32)],
        compiler_params=_params(2),
    )(h, h, h, h, qg, kg, base)


def _zeros_cols(rows, n, dtype):
    return jnp.zeros((rows, n), dtype)


def _layout_w_in(w):
    d = w.shape[0]
    c = lambda name, size, off=0: w[:, _SRC[name] + off:_SRC[name] + off + size]
    z = lambda n: _zeros_cols(d, n, w.dtype)
    pieces = [c("a_u", 512), c("a_v", 512), c("a_z", 512),
              c("b_q", 512), c("b_iq", 512), c("b_z", 512),
              c("b_k", 64), c("b_k", 64), c("b_v", 64), c("b_v", 64), c("b_ik", 64), c("b_ik", 64),
              c("c_kr", 32), c("b_iw", 8), z(24), c("c_kr", 32, 32), z(32),
              c("c_q", 384), c("c_kv", 128), c("c_z", 512),
              c("d_q", 512), c("d_k", 512), c("d_v", 512), c("d_z", 512)]
    return jnp.concatenate(pieces, axis=1).astype(BF16)


def _rope_layout(v):
    z = jnp.zeros(v.shape[:-1] + (32,), v.dtype)
    return jnp.concatenate([v[..., :32], z, v[..., 32:], z], axis=-1)


def _layout_c(w_qb, w_kvb, q_gain, k_gain, qa_gain):
    wq = w_qb.reshape(Q_LORA, C_HEADS, C_QK)
    wq = jnp.concatenate([wq[..., :C_NOPE], _rope_layout(wq[..., C_NOPE:])], axis=-1)
    wq = wq.reshape(Q_LORA, C_HEADS * 256).astype(BF16)
    wkv = w_kvb.reshape(KV_LORA, C_HEADS, 2 * LANES)
    wkv = jnp.concatenate([wkv[..., :C_NOPE].reshape(KV_LORA, -1), wkv[..., C_NOPE:].reshape(KV_LORA, -1)],
                          axis=1).astype(BF16)
    lay = lambda g: jnp.concatenate([g[:C_NOPE], _rope_layout(g[C_NOPE:])])[None, :]
    return wq, wkv, lay(q_gain), lay(k_gain), qa_gain[None, :]


def _t5_bucket_static(rel):
    half = T5_BUCKETS // 2
    exact = half // 2
    n = abs(rel)
    if n < exact:
        val = n
    else:
        val = min(exact + (n * n // (exact * exact)).bit_length() - 1, half - 1)
    return (half if rel > 0 else 0) + val


def _t5_tables(t5_bias):
    m = np.arange(512)
    d0 = np.where(m < 256, m, m - 512)
    d1 = np.where(m <= 256, m - 256, m - 768)
    idx0 = np.array([_t5_bucket_static(int(d)) for d in d0], np.int32)
    idx1 = np.array([_t5_bucket_static(int(d)) for d in d1], np.int32)
    far = _t5_bucket_static(-512)
    return t5_bias[idx0].T, t5_bias[idx1].T, t5_bias[far]


def _band_table(rel_bias, tq):
    width = 2 * tq + D_LEFT
    m = np.arange(width)
    dist = np.where(m <= tq + D_LEFT, D_LEFT - m, D_LEFT + width - m)
    idx = np.clip(dist, -REL_CLIP, REL_CLIP) + REL_CLIP
    return rel_bias[idx.astype(np.int32)].T


def _rope_tables(seq):
    inv = ROPE_BASE ** (-jnp.arange(0, C_ROPE, 2, dtype=F32) / C_ROPE)
    ang = jnp.arange(seq, dtype=F32)[:, None] * inv[None, :]
    c, s = jnp.cos(ang), jnp.sin(ang)
    z = jnp.zeros_like(c)
    return jnp.concatenate([c, z, c, z], axis=1), jnp.concatenate([-s, z, s, z], axis=1)


def kernel(x, t5_bias, norm_g, w_in, a_v_gain, a_ws, a_bs, b_q_gain, b_k_gain, c_qa_gain, c_kva_gain,
           c_w_qb, c_w_kvb, c_q_gain, c_k_gain, d_q_gain, d_k_gain, d_rel_bias, w_out):
    bsz, seq, d_model = x.shape
    depth = w_in.shape[0]
    tq = 256
    x2 = x.reshape(bsz * seq, d_model)
    cos, sin = _rope_tables(seq)
    base0, base1, cfar = _t5_tables(t5_bias)
    for l in range(depth):
        h = _inproj(x2, norm_g[l][None, :], _layout_w_in(w_in[l]))
        y_a = _mixer_a(h, a_v_gain[l][None, :], a_ws[l], a_bs[l][:, :, None])
        mask = _b_select(h, bsz, seq, tq)
        y_b = _b_attn(h, mask, cfar, jnp.tile(b_q_gain[l], N_HEADS64)[None, :],
                      jnp.tile(b_k_gain[l], 2)[None, :], base0, base1, bsz, seq, tq)
        wq, wkv, qg, kg, qag = _layout_c(c_w_qb[l], c_w_kvb[l], c_q_gain[l], c_k_gain[l], c_qa_gain[l])
        qc, kc, vc = _c_prep(h, wq, wkv, qag, c_kva_gain[l][None, :], qg, kg, cos, sin, seq)
        y_c = _c_attn(qc, kc, vc, h, bsz, seq, tq)
        y_d = _mixer_d(h, jnp.tile(d_q_gain[l], N_HEADS64)[None, :], jnp.tile(d_k_gain[l], N_HEADS64)[None, :],
                       _band_table(d_rel_bias[l], tq), bsz, seq, tq)
        x2 = _outproj(x2, (y_a, y_b, y_c, y_d), w_out[l].astype(BF16))
    return x2.reshape(bsz, seq, d_model)
```

```python
import functools
import math

import numpy as np
import jax
import jax.numpy as jnp
from jax import lax
from jax.experimental import pallas as pl
from jax.experimental.pallas import tpu as pltpu

F32 = jnp.float32
BF16 = jnp.bfloat16

EPS = 1e-6
NEG = -1e30
LOG2E = math.log2(math.e)
INT_MIN = -(2 ** 31)
CHUNK = 64
LANES = 128
GROUP_W = 512
A_GROUPS = 4
GMLP_BLOCK = 128
N_HEADS64 = 8
IDX_SCALE = (8 ** -0.5) * 0.125
TOPK_MAX = 256
T5_BUCKETS = 32
C_HEADS = 4
C_NOPE = 128
C_ROPE = 64
C_QK = 192
Q_LORA = 384
KV_LORA = 128
ROPE_BASE = 10000.0
D_LEFT = 8 * CHUNK
REL_CLIP = 128
VMEM_LIMIT = 56 * 1024 * 1024

BLK_A_U, BLK_A_V, BLK_A_Z = 0, 1, 2
BLK_B_Q, BLK_B_IQ, BLK_B_Z, BLK_SMALL = 3, 4, 5, 6
BLK_C_QKV, BLK_C_Z = 7, 8
BLK_D_Q, BLK_D_K, BLK_D_V, BLK_D_Z = 9, 10, 11, 12
H_COLS = 13 * GROUP_W
UNIT_B_K, UNIT_B_V, UNIT_B_IK, UNIT_KR_IW = (BLK_SMALL * 4 + i for i in range(4))
IW_LANE = 32

_SRC = dict(a_u=0, a_v=512, a_z=1024, b_q=1536, b_k=2048, b_v=2112, b_iq=2176, b_ik=2688,
            b_iw=2752, b_z=2760, c_q=3272, c_kv=3656, c_kr=3784, c_z=3848,
            d_q=4360, d_k=4872, d_v=5384, d_z=5896)


def _params(n_axes):
    return pltpu.CompilerParams(dimension_semantics=("arbitrary",) * n_axes,
                                vmem_limit_bytes=VMEM_LIMIT)


def _gelu(x):
    c = math.sqrt(2.0 / math.pi)
    return x * (0.5 * (1.0 + jnp.tanh(c * (x + 0.044715 * (x * x * x)))))


def _silu(x):
    return x * (1.0 / (1.0 + jnp.exp(-x)))


def _dot_t(a, b):
    return lax.dot_general(a, b, (((1,), (1,)), ((), ())), preferred_element_type=F32)


def _lo_mask(rows):
    return lax.broadcasted_iota(jnp.int32, (rows, LANES), 1) < 64


def _rms_heads64(x, gain, ntiles):
    lo = _lo_mask(x.shape[0])
    tiles = []
    for t in range(ntiles):
        xt = x[:, t * LANES:(t + 1) * LANES]
        sq = xt * xt
        s_lo = jnp.sum(jnp.where(lo, sq, 0.0), axis=-1, keepdims=True)
        s_hi = jnp.sum(jnp.where(lo, 0.0, sq), axis=-1, keepdims=True)
        r = jnp.where(lo, lax.rsqrt(s_lo * (1.0 / 64) + EPS), lax.rsqrt(s_hi * (1.0 / 64) + EPS))
        tiles.append(xt * r * gain[:, t * LANES:(t + 1) * LANES])
    return tiles


def _toeplitz(base_row, rows, width):
    t = jnp.broadcast_to(base_row, (rows, base_row.shape[1]))
    t = pltpu.roll(t, 0, 1, stride=1, stride_axis=0)
    return t[:, :width]


def _inproj_kernel(x_ref, g_ref, w_ref, o_ref):
    x = x_ref[...]
    ms = jnp.mean(x * x, axis=-1, keepdims=True)
    xn = (x * lax.rsqrt(ms + EPS) * g_ref[...]).astype(BF16)
    for c in range(H_COLS // GROUP_W):
        cols = slice(c * GROUP_W, (c + 1) * GROUP_W)
        o_ref[:, cols] = jnp.dot(xn, w_ref[:, cols], preferred_element_type=F32).astype(BF16)


def _inproj(x2, g, w_all, layer, tm=512):
    n, d = x2.shape
    return pl.pallas_call(
        _inproj_kernel,
        grid=(n // tm,),
        in_specs=[pl.BlockSpec((tm, d), lambda i: (i, 0)),
                  pl.BlockSpec((1, d), lambda i: (0, 0)),
                  pl.BlockSpec((None, d, H_COLS), lambda i: (layer, 0, 0), pipeline_mode=pl.Buffered(1))],
        out_specs=pl.BlockSpec((tm, H_COLS), lambda i: (i, 0)),
        out_shape=jax.ShapeDtypeStruct((n, H_COLS), BF16),
        compiler_params=_params(1),
    )(x2, g, w_all)


def _outproj_kernel(x_ref, ya_ref, yb_ref, yc_ref, yd_ref, w_ref, o_ref):
    acc = x_ref[...]
    for g, y_ref in enumerate((ya_ref, yb_ref, yc_ref, yd_ref)):
        acc = acc + jnp.dot(y_ref[...], w_ref[g * GROUP_W:(g + 1) * GROUP_W, :],
                            preferred_element_type=F32)
    o_ref[...] = acc


def _outproj(x2, ys, w, tm=512):
    n, d = x2.shape
    yspec = pl.BlockSpec((tm, GROUP_W), lambda i: (i, 0))
    return pl.pallas_call(
        _outproj_kernel,
        grid=(n // tm,),
        in_specs=[pl.BlockSpec((tm, d), lambda i: (i, 0)), yspec, yspec, yspec, yspec,
                  pl.BlockSpec((4 * GROUP_W, d), lambda i: (0, 0))],
        out_specs=pl.BlockSpec((tm, d), lambda i: (i, 0)),
        out_shape=jax.ShapeDtypeStruct((n, d), F32),
        compiler_params=_params(1),
    )(x2, *ys, w)


def _mixer_a_kernel(u_ref, v_ref, z_ref, vg_ref, w_ref, b_ref, o_ref):
    tm = u_ref.shape[0]
    u = _gelu(u_ref[...].astype(F32))
    v = _gelu(v_ref[...].astype(F32))
    ms = jnp.mean(v * v, axis=-1, keepdims=True)
    vb = (v * lax.rsqrt(ms + EPS) * vg_ref[...]).astype(BF16)
    gate = _silu(z_ref[...].astype(F32))
    i = lax.broadcasted_iota(jnp.int32, (GMLP_BLOCK, GMLP_BLOCK), 0)
    j = lax.broadcasted_iota(jnp.int32, (GMLP_BLOCK, GMLP_BLOCK), 1)
    keep = (j // CHUNK) <= (i // CHUNK)
    for g in range(A_GROUPS):
        wg = jnp.where(keep, w_ref[g], 0.0).astype(BF16)
        cols = slice(g * LANES, (g + 1) * LANES)
        for blk in range(tm // GMLP_BLOCK):
            rows = slice(blk * GMLP_BLOCK, (blk + 1) * GMLP_BLOCK)
            sg = jnp.dot(wg, vb[rows, cols], preferred_element_type=F32) + b_ref[g]
            o_ref[rows, cols] = (u[rows, cols] * sg * gate[rows, cols]).astype(BF16)


def _mixer_a(h, vg, ws, bs, tm=512):
    n = h.shape[0]
    hspec = lambda blk: pl.BlockSpec((tm, GROUP_W), lambda i, blk=blk: (i, blk))
    return pl.pallas_call(
        _mixer_a_kernel,
        grid=(n // tm,),
        in_specs=[hspec(BLK_A_U), hspec(BLK_A_V), hspec(BLK_A_Z),
                  pl.BlockSpec((1, GROUP_W), lambda i: (0, 0)),
                  pl.BlockSpec((A_GROUPS, GMLP_BLOCK, GMLP_BLOCK), lambda i: (0, 0, 0)),
                  pl.BlockSpec((A_GROUPS, GMLP_BLOCK, 1), lambda i: (0, 0, 0))],
        out_specs=pl.BlockSpec((tm, GROUP_W), lambda i: (i, 0)),
        out_shape=jax.ShapeDtypeStruct((n, GROUP_W), BF16),
        compiler_params=_params(1),
    )(h, h, h, vg, ws, bs)


def _order_key(x):
    return jnp.where(x < 0, x ^ 0x7FFFFFFF, x)


def _b_select_kernel(iq_ref, iw_ref, ik_ref, o_ref, lhs_ref, key_ref, *, seq, tq, topk):
    t_blk = pl.program_id(1)
    nkb = seq // 256
    nblk = ((t_blk + 1) * tq) // 256
    n_interp = 12
    hrows = [slice(h * tq, (h + 1) * tq) for h in range(N_HEADS64)]

    lo_half = _lo_mask(tq)
    w_t = (iw_ref[...].astype(F32) * IDX_SCALE).T
    for h in range(N_HEADS64):
        iqt = iq_ref[:, (h // 2) * LANES:(h // 2 + 1) * LANES]
        sel = lo_half if h % 2 == 0 else jnp.logical_not(lo_half)
        lhs_ref[hrows[h], :] = jnp.where(sel, iqt, jnp.zeros_like(iqt))

    qpos = t_blk * tq + lax.broadcasted_iota(jnp.int32, (256, tq), 1)
    krow = lax.broadcasted_iota(jnp.int32, (256, tq), 0)

    def score_block(kb, amax, masked):
        off = pl.multiple_of(kb * 256, 256)
        ikblk = ik_ref[pl.ds(off, 256), :]
        score = jnp.zeros((256, tq), F32)
        for h in range(N_HEADS64):
            w_h = w_t[IW_LANE + h:IW_LANE + h + 1, :]
            score = score + w_h * jnp.maximum(_dot_t(ikblk, lhs_ref[hrows[h], :]), 0.0)
        score = jnp.where(score == 0.0, 0.0, score)
        key = _order_key(lax.bitcast_convert_type(score, jnp.int32))
        if masked:
            key = jnp.where(((kb * 256 + krow) // CHUNK) <= (qpos // CHUNK), key, INT_MIN)
        key_ref[kb] = key
        return jnp.maximum(amax, jnp.max(jnp.abs(score).reshape(256 // 8, 8, tq), axis=0))

    nfull = (t_blk * tq) // 256
    amax = lax.fori_loop(0, nfull, lambda kb, a: score_block(kb, a, False), jnp.zeros((8, tq), F32))
    amax = lax.fori_loop(nfull, nblk, lambda kb, a: score_block(kb, a, True), amax)
    amax = jnp.max(amax, axis=0, keepdims=True)

    def count(pred):
        def body(kb, acc):
            ones = jnp.where(pred(key_ref[kb], kb), 1.0, 0.0)
            return acc + jnp.sum(ones.reshape(256 // 8, 8, tq), axis=0)
        return jnp.sum(lax.fori_loop(0, nblk, body, jnp.zeros((8, tq), F32)), axis=0, keepdims=True)

    kf = float(topk)
    qrow = t_blk * tq + lax.broadcasted_iota(jnp.int32, (1, tq), 1)
    n_adm = ((qrow // CHUNK + 1) * CHUNK).astype(F32)
    one = jnp.ones((1, tq), jnp.int32)
    f_pos = count(lambda k, kb: k >= one)
    f_nn = count(lambda k, kb: k >= one - 1)
    pos = f_pos > kf
    neg = f_nn < kf
    lo0 = jnp.where(pos, one, _order_key(lax.bitcast_convert_type(-amax, jnp.int32)))
    hi0 = jnp.where(neg, one - 1, _order_key(lax.bitcast_convert_type(amax, jnp.int32)) + 1)
    w_lo0 = jnp.where(pos, f_pos, n_adm) - kf
    w_hi0 = kf - jnp.where(neg, f_nn, 0.0)
    all_sel = n_adm <= kf
    at_zero = jnp.logical_not(pos | neg)
    done0 = jnp.where(all_sel | at_zero | (hi0 == lo0 + 1), 1.0, 0.0)
    thr0 = jnp.where(all_sel, INT_MIN + 1, jnp.where(at_zero, jnp.where(f_pos == kf, one, one - 1), lo0))

    def search_cond(st):
        return jnp.logical_and(st[0] < n_interp + 32, st[1] < 0.5)

    def search_step(st):
        it, _, lo, hi, w_lo, w_hi, side, done, thr = st
        lo_v = lax.bitcast_convert_type(_order_key(lo), F32)
        hi_v = lax.bitcast_convert_type(_order_key(hi), F32)
        c_v = lo_v + (hi_v - lo_v) * (w_lo / (w_lo + w_hi))
        c_interp = _order_key(lax.bitcast_convert_type(c_v, jnp.int32))
        c_mid = (lo >> 1) + (hi >> 1) + (lo & hi & 1)
        cand = jnp.where(it < n_interp, c_interp, c_mid)
        cand = jnp.minimum(jnp.maximum(cand, lo + 1), hi - 1)
        f = count(lambda k, kb: k >= cand)
        live = done < 0.5
        up = f > kf
        hit = f == kf
        new_lo = jnp.where(live & up, cand, lo)
        new_hi = jnp.where(live & jnp.logical_not(up), cand, hi)
        new_w_lo = jnp.where(up, f - kf, jnp.where(side < 0.0, 0.5 * w_lo, w_lo))
        new_w_hi = jnp.where(up, jnp.where(side > 0.0, 0.5 * w_hi, w_hi), kf - f)
        new_side = jnp.where(up, 1.0, -1.0)
        new_thr = jnp.where(live, jnp.where(hit, cand, new_lo), thr)
        new_done = jnp.where(live & (hit | (new_hi == new_lo + 1)), 1.0, done)
        return (it + 1, jnp.min(new_done), new_lo, new_hi, jnp.where(live, new_w_lo, w_lo),
                jnp.where(live, new_w_hi, w_hi), jnp.where(live, new_side, side), new_done, new_thr)

    state = (jnp.int32(0), jnp.min(done0), lo0, hi0, w_lo0, w_hi0, jnp.zeros((1, tq), F32), done0, thr0)
    thr = lax.while_loop(search_cond, search_step, state)[-1]

    need = kf - count(lambda k, kb: k > thr)
    ties = count(lambda k, kb: k == thr)
    any_excess = jnp.max(ties - need) > 0.0

    def write_unused(kb, carry):
        o_ref[0, kb] = jnp.zeros((tq, 256), F32)
        return carry

    lax.fori_loop(nblk, nkb, write_unused, 0)

    @pl.when(jnp.logical_not(any_excess))
    def _():
        def write(kb, carry):
            o_ref[0, kb] = jnp.where(key_ref[kb] >= thr, 1.0, 0.0).T
            return carry

        lax.fori_loop(0, nblk, write, 0)

    @pl.when(any_excess)
    def _():
        def idx_step(it, jmax):
            cand = jmax | lax.shift_left(jnp.int32(1), 10 - it)
            below = count(lambda k, kb: (k == thr) & ((kb * 256 + krow) < cand))
            return jnp.where(below < need, cand, jmax)

        jmax = lax.fori_loop(0, 11, idx_step, jnp.zeros((1, tq), jnp.int32))

        def write(kb, carry):
            k = key_ref[kb]
            keep_tie = (k == thr) & ((kb * 256 + krow) <= jmax)
            o_ref[0, kb] = jnp.where((k > thr) | keep_tie, 1.0, 0.0).T
            return carry

        lax.fori_loop(0, nblk, write, 0)


def _b_select(h, bsz, seq, tq=256):
    nt = seq // tq
    topk = min(TOPK_MAX, seq // 4)
    kern = functools.partial(_b_select_kernel, seq=seq, tq=tq, topk=topk)
    return pl.pallas_call(
        kern,
        grid=(bsz, nt),
        in_specs=[pl.BlockSpec((tq, GROUP_W), lambda b, t: (b * nt + t, BLK_B_IQ)),
                  pl.BlockSpec((tq, LANES), lambda b, t: (b * nt + t, UNIT_KR_IW)),
                  pl.BlockSpec((seq, LANES), lambda b, t: (b, UNIT_B_IK))],
        out_specs=pl.BlockSpec((1, seq // 256, tq, 256), lambda b, t: (b * nt + t, 0, 0, 0)),
        out_shape=jax.ShapeDtypeStruct((bsz * nt, seq // 256, tq, 256), F32),
        scratch_shapes=[pltpu.VMEM((N_HEADS64 * tq, LANES), BF16),
                        pltpu.VMEM((seq // 256, 256, tq), jnp.int32)],
        compiler_params=_params(2),
    )(h, h, h)


def _b_attn_kernel(cfar_ref, q_ref, z_ref, k_ref, v_ref, msk_ref, qg_ref, kg_ref, base0_ref, base1_ref,
                   o_ref, kn_ref, v1_ref, bias_ref, qall_ref, s_ref, mp_ref, acc_ref, *, seq, tq):
    b = pl.program_id(0)
    t_blk = pl.program_id(1)
    hrows = [slice(h * tq, (h + 1) * tq) for h in range(N_HEADS64)]

    @pl.when((b == 0) & (t_blk == 0))
    def _():
        for h in range(N_HEADS64):
            bias_ref[0, hrows[h], :] = jnp.full((tq, 256), cfar_ref[h] * LOG2E, F32)
            bias_ref[1, hrows[h], :] = _toeplitz(base1_ref[h:h + 1, :], tq, 256) * LOG2E
            bias_ref[2, hrows[h], :] = _toeplitz(base0_ref[h:h + 1, :], tq, 256) * LOG2E

    @pl.when(t_blk == 0)
    def _():
        lo256 = _lo_mask(256)
        for r in range(seq // 256):
            rows = slice(r * 256, (r + 1) * 256)
            k = k_ref[rows, :].astype(F32)
            ms = jnp.mean(k * k, axis=-1, keepdims=True)
            kn_ref[rows, :] = (k * lax.rsqrt(ms + EPS) * kg_ref[...]).astype(BF16)
            v = v_ref[rows, :]
            v1_ref[rows, :] = jnp.where(lo256, v, jnp.ones_like(v))

    lo = _lo_mask(tq)
    qtiles = _rms_heads64(q_ref[...].astype(F32), qg_ref[...], 4)
    for h in range(N_HEADS64):
        sel = lo if h % 2 == 0 else jnp.logical_not(lo)
        qall_ref[hrows[h], :] = jnp.where(sel, qtiles[h // 2] * (0.125 * LOG2E), 0.0).astype(BF16)
    mp_ref[...] = jnp.full(mp_ref.shape, NEG, F32)
    acc_ref[...] = jnp.zeros(acc_ref.shape, F32)
    nblk = t_blk + 1

    def logits_blocks(kb0, n):
        for i in range(n):
            kb = kb0 + i
            off = pl.multiple_of(kb * 256, 256)
            kblk = kn_ref[pl.ds(off, 256), :]
            which = jnp.clip(kb - (t_blk - 2), 0, 2)
            keep = msk_ref[0, kb] > 0.5
            for h in range(N_HEADS64):
                s = jnp.where(keep, _dot_t(qall_ref[hrows[h], :], kblk) + bias_ref[which, hrows[h], :], NEG)
                s_ref[kb, hrows[h], :] = s
                mp_ref[hrows[h], :] = jnp.maximum(mp_ref[hrows[h], :], jnp.maximum(s[:, :LANES], s[:, LANES:]))

    def value_blocks(kb0, n):
        off = pl.multiple_of(kb0 * 256, 256)
        v1 = v1_ref[pl.ds(off, n * 256), :]
        for h in range(N_HEADS64):
            m = mp_ref[hrows[h], :]
            mm = jnp.concatenate([m, m], axis=1)
            p = [jnp.exp2(s_ref[kb0 + i, hrows[h], :] - mm).astype(BF16) for i in range(n)]
            p = p[0] if n == 1 else jnp.concatenate(p, axis=1)
            acc_ref[hrows[h], :] += jnp.dot(p, v1, preferred_element_type=F32)

    def pairs_then_rest(blocks):
        def pair(i, carry):
            blocks(2 * i, 2)
            return carry

        lax.fori_loop(0, nblk // 2, pair, 0)

        @pl.when(nblk % 2 == 1)
        def _():
            blocks(nblk - 1, 1)

    pairs_then_rest(logits_blocks)
    for h in range(N_HEADS64):
        m = jnp.max(mp_ref[hrows[h], :], axis=-1, keepdims=True)
        mp_ref[hrows[h], :] = jnp.broadcast_to(m, (tq, LANES))
    pairs_then_rest(value_blocks)

    gate = _silu(z_ref[...].astype(F32))
    for t in range(4):
        a_even = acc_ref[hrows[2 * t], :]
        a_odd = acc_ref[hrows[2 * t + 1], :]
        o_even = a_even / pltpu.roll(a_even, 64, 1)
        o_odd = pltpu.roll(a_odd, 64, 1) / a_odd
        cols = slice(t * LANES, (t + 1) * LANES)
        o_ref[:, cols] = (jnp.where(lo, o_even, o_odd) * gate[:, cols]).astype(BF16)


def _b_attn(h, mask, cfar, qg, kg, base0, base1, bsz, seq, tq=256):
    nt = seq // tq
    rows = N_HEADS64 * tq
    kern = functools.partial(_b_attn_kernel, seq=seq, tq=tq)
    full = lambda shape: pl.BlockSpec(shape, lambda b, t: (0,) * len(shape))
    return pl.pallas_call(
        kern,
        grid=(bsz, nt),
        in_specs=[pl.BlockSpec(memory_space=pltpu.SMEM),
                  pl.BlockSpec((tq, GROUP_W), lambda b, t: (b * nt + t, BLK_B_Q)),
                  pl.BlockSpec((tq, GROUP_W), lambda b, t: (b * nt + t, BLK_B_Z)),
                  pl.BlockSpec((seq, LANES), lambda b, t: (b, UNIT_B_K)),
                  pl.BlockSpec((seq, LANES), lambda b, t: (b, UNIT_B_V)),
                  pl.BlockSpec((1, seq // 256, tq, 256), lambda b, t: (b * nt + t, 0, 0, 0)),
                  full((1, GROUP_W)), full((1, LANES)), full((N_HEADS64, 512)), full((N_HEADS64, 512))],
        out_specs=pl.BlockSpec((tq, GROUP_W), lambda b, t: (b * nt + t, 0)),
        out_shape=jax.ShapeDtypeStruct((bsz * seq, GROUP_W), BF16),
        scratch_shapes=[pltpu.VMEM((seq, LANES), BF16), pltpu.VMEM((seq, LANES), BF16),
                        pltpu.VMEM((3, rows, 256), F32), pltpu.VMEM((rows, LANES), BF16),
                        pltpu.VMEM((seq // 256, rows, 256), F32),
                        pltpu.VMEM((rows, LANES), F32), pltpu.VMEM((rows, LANES), F32)],
        compiler_params=_params(2),
    )(cfar, h, h, h, h, mask, qg, kg, base0, base1)


def _rope(tile, cos, sin):
    return tile * cos + pltpu.roll(tile, 64, 1) * sin


def _c_prep_kernel(lat_ref, kr_ref, wq_ref, wkv_ref, qag_ref, kvag_ref, qg_ref, kg_ref, cos_ref, sin_ref,
                   qo_ref, ko_ref, vo_ref):
    cq = lat_ref[:, :Q_LORA].astype(F32)
    ms = jnp.mean(cq * cq, axis=-1, keepdims=True)
    cqn = (cq * lax.rsqrt(ms + EPS) * qag_ref[...]).astype(BF16)
    qpre = jnp.dot(cqn, wq_ref[...], preferred_element_type=F32)
    ckv = lat_ref[:, Q_LORA:].astype(F32)
    ms = jnp.mean(ckv * ckv, axis=-1, keepdims=True)
    ckvn = (ckv * lax.rsqrt(ms + EPS) * kvag_ref[...]).astype(BF16)
    kvpre = jnp.dot(ckvn, wkv_ref[...], preferred_element_type=F32)
    lane = lax.broadcasted_iota(jnp.int32, kr_ref.shape, 1)
    kr = jnp.where((lane % 64) < 32, kr_ref[...].astype(F32), 0.0)
    kr_ss = jnp.sum(kr * kr, axis=-1, keepdims=True)
    cos = cos_ref[...]
    sin = sin_ref[...]
    qg = qg_ref[...]
    kg = kg_ref[...]
    for h in range(C_HEADS):
        qh = qpre[:, h * 256:(h + 1) * 256]
        r = lax.rsqrt(jnp.sum(qh * qh, axis=-1, keepdims=True) * (1.0 / C_QK) + EPS)
        qn = qh * r * qg
        qo_ref[:, h * 256:h * 256 + LANES] = qn[:, :LANES].astype(BF16)
        qo_ref[:, h * 256 + LANES:(h + 1) * 256] = _rope(qn[:, LANES:], cos, sin).astype(BF16)
        kn = kvpre[:, h * LANES:(h + 1) * LANES]
        r = lax.rsqrt((jnp.sum(kn * kn, axis=-1, keepdims=True) + kr_ss) * (1.0 / C_QK) + EPS)
        ko_ref[:, h * 256:h * 256 + LANES] = (kn * r * kg[:, :LANES]).astype(BF16)
        ko_ref[:, h * 256 + LANES:(h + 1) * 256] = _rope(kr * r * kg[:, LANES:], cos, sin).astype(BF16)
    vo_ref[...] = kvpre[:, C_HEADS * LANES:].astype(BF16)


def _c_prep(h, wq, wkv, qag, kvag, qg, kg, cos, sin, seq, tm=512):
    n = h.shape[0]
    ns = seq // tm
    full = lambda shape: pl.BlockSpec(shape, lambda i: (0,) * len(shape))
    return pl.pallas_call(
        _c_prep_kernel,
        grid=(n // tm,),
        in_specs=[pl.BlockSpec((tm, GROUP_W), lambda i: (i, BLK_C_QKV)),
                  pl.BlockSpec((tm, LANES), lambda i: (i, UNIT_KR_IW)),
                  full((Q_LORA, 4 * 256)), full((KV_LORA, 8 * LANES)),
                  full((1, Q_LORA)), full((1, LANES)), full((1, 256)), full((1, 256)),
                  pl.BlockSpec((tm, LANES), lambda i: (i % ns, 0)),
                  pl.BlockSpec((tm, LANES), lambda i: (i % ns, 0))],
        out_specs=[pl.BlockSpec((tm, 4 * 256), lambda i: (i, 0)),
                   pl.BlockSpec((tm, 4 * 256), lambda i: (i, 0)),
                   pl.BlockSpec((tm, GROUP_W), lambda i: (i, 0))],
        out_shape=[jax.ShapeDtypeStruct((n, 4 * 256), BF16), jax.ShapeDtypeStruct((n, 4 * 256), BF16),
                   jax.ShapeDtypeStruct((n, GROUP_W), BF16)],
        compiler_params=_params(1),
    )(h, h, wq, wkv, qag, kvag, qg, kg, cos, sin)


def _c_attn_kernel(q_ref, k_ref, v_ref, z_ref, o_ref, s_ref, mp_ref, lp_ref, acc_ref, *, tq):
    qt = pl.program_id(1)
    scale = C_QK ** -0.5 * LOG2E
    hrows = [slice(h * tq, (h + 1) * tq) for h in range(C_HEADS)]
    qchunk = (qt * tq + lax.broadcasted_iota(jnp.int32, (tq, 256), 0)) // CHUNK
    kcol = lax.broadcasted_iota(jnp.int32, (tq, 256), 1)
    mp_ref[...] = jnp.full(mp_ref.shape, NEG, F32)
    lp_ref[...] = jnp.zeros(lp_ref.shape, F32)
    acc_ref[...] = jnp.zeros(acc_ref.shape, F32)
    nfull = (qt * tq) // 256
    nblk = ((qt + 1) * tq) // 256

    def logits_blocks(kb0, n, masked):
        for i in range(n):
            kb = kb0 + i
            off = pl.multiple_of(kb * 256, 256)
            for h in range(C_HEADS):
                cols = slice(h * 256, (h + 1) * 256)
                s = _dot_t(q_ref[:, cols], k_ref[pl.ds(off, 256), cols]) * scale
                if masked:
                    s = jnp.where(((kb * 256 + kcol) // CHUNK) <= qchunk, s, NEG)
                s_ref[kb, hrows[h], :] = s
                mp_ref[hrows[h], :] = jnp.maximum(mp_ref[hrows[h], :], jnp.maximum(s[:, :LANES], s[:, LANES:]))

    def value_blocks(kb0, n):
        off = pl.multiple_of(kb0 * 256, 256)
        for h in range(C_HEADS):
            m = mp_ref[hrows[h], :]
            mm = jnp.concatenate([m, m], axis=1)
            p = [jnp.exp2(s_ref[kb0 + i, hrows[h], :] - mm) for i in range(n)]
            lsum = p[0][:, :LANES] + p[0][:, LANES:]
            for pi in p[1:]:
                lsum = lsum + pi[:, :LANES] + pi[:, LANES:]
            lp_ref[hrows[h], :] += lsum
            pb = p[0].astype(BF16) if n == 1 else jnp.concatenate([pi.astype(BF16) for pi in p], axis=1)
            acc_ref[hrows[h], :] += jnp.dot(pb, v_ref[pl.ds(off, n * 256), h * LANES:(h + 1) * LANES],
                                            preferred_element_type=F32)

    def pairs_then_rest(blocks, count):
        def pair(i, carry):
            blocks(2 * i, 2)
            return carry

        lax.fori_loop(0, count // 2, pair, 0)

        @pl.when(count % 2 == 1)
        def _():
            blocks(count - 1, 1)

    pairs_then_rest(lambda kb0, n: logits_blocks(kb0, n, False), nfull)
    lax.fori_loop(nfull, nblk, lambda kb, c: (logits_blocks(kb, 1, True), c)[1], 0)
    for h in range(C_HEADS):
        m = jnp.max(mp_ref[hrows[h], :], axis=-1, keepdims=True)
        mp_ref[hrows[h], :] = jnp.broadcast_to(m, (tq, LANES))
    pairs_then_rest(value_blocks, nblk)
    gate = _silu(z_ref[...].astype(F32))
    for h in range(C_HEADS):
        cols = slice(h * LANES, (h + 1) * LANES)
        l = jnp.sum(lp_ref[hrows[h], :], axis=-1, keepdims=True)
        o_ref[:, cols] = (acc_ref[hrows[h], :] / l * gate[:, cols]).astype(BF16)


def _c_attn(qc, kc, vc, h, bsz, seq, tq=256):
    nt = seq // tq
    rows = C_HEADS * tq
    kern = functools.partial(_c_attn_kernel, tq=tq)
    return pl.pallas_call(
        kern,
        grid=(bsz, nt),
        in_specs=[pl.BlockSpec((tq, C_HEADS * 256), lambda b, t: (b * nt + t, 0)),
                  pl.BlockSpec((seq, C_HEADS * 256), lambda b, t: (b, 0)),
                  pl.BlockSpec((seq, GROUP_W), lambda b, t: (b, 0)),
                  pl.BlockSpec((tq, GROUP_W), lambda b, t: (b * nt + t, BLK_C_Z))],
        out_specs=pl.BlockSpec((tq, GROUP_W), lambda b, t: (b * nt + t, 0)),
        out_shape=jax.ShapeDtypeStruct((bsz * seq, GROUP_W), BF16),
        scratch_shapes=[pltpu.VMEM((seq // 256, rows, 256), F32), pltpu.VMEM((rows, LANES), F32),
                        pltpu.VMEM((rows, LANES), F32), pltpu.VMEM((rows, LANES), F32)],
        compiler_params=_params(2),
    )(qc, kc, vc, h)


def _mixer_d_kernel(q_ref, k_ref, v_ref, z_ref, qg_ref, kg_ref, base_ref, o_ref,
                    kpad_ref, vpad_ref, bias_ref, s_ref, mp_ref, *, seq, tq):
    b = pl.program_id(0)
    qt = pl.program_id(1)
    win = tq + D_LEFT

    @pl.when((b == 0) & (qt == 0))
    def _():
        for h in range(N_HEADS64):
            bias_ref[h] = _toeplitz(base_ref[h:h + 1, :], tq, win) * LOG2E

    @pl.when(qt == 0)
    def _():
        kpad_ref[0:D_LEFT, :] = jnp.zeros((D_LEFT, GROUP_W), BF16)
        vpad_ref[0:D_LEFT, :] = jnp.zeros((D_LEFT, GROUP_W), BF16)
        for r in range(seq // 256):
            rows = slice(r * 256, (r + 1) * 256)
            dst = slice(D_LEFT + r * 256, D_LEFT + (r + 1) * 256)
            tiles = _rms_heads64(k_ref[rows, :].astype(F32), kg_ref[...], 4)
            for t in range(4):
                kpad_ref[dst, t * LANES:(t + 1) * LANES] = tiles[t].astype(BF16)
            vpad_ref[dst, :] = v_ref[rows, :]

    lo = _lo_mask(tq)
    qtiles = _rms_heads64(q_ref[...].astype(F32), qg_ref[...], 4)
    start = pl.multiple_of(qt * tq, tq)
    qc = lax.broadcasted_iota(jnp.int32, (tq, win), 0) // CHUNK
    j = lax.broadcasted_iota(jnp.int32, (tq, win), 1)
    kc = j // CHUNK
    valid = (kc >= qc) & (kc <= qc + D_LEFT // CHUNK) & (j + start >= D_LEFT)
    gate = _silu(z_ref[...].astype(F32))
    ntile = win // LANES

    def lane_tiles(x):
        return [x[:, i * LANES:(i + 1) * LANES] for i in range(ntile)]

    for t in range(4):
        kwin = kpad_ref[pl.ds(start, win), t * LANES:(t + 1) * LANES]
        for half in range(2):
            h = 2 * t + half
            sel = lo if half == 0 else jnp.logical_not(lo)
            qh = jnp.where(sel, qtiles[t] * (0.125 * LOG2E), 0.0).astype(BF16)
            s = jnp.where(valid, _dot_t(qh, kwin) + bias_ref[h], NEG)
            s_ref[h] = s
            mp_ref[h] = functools.reduce(jnp.maximum, lane_tiles(s))
    for h in range(N_HEADS64):
        mp_ref[h] = jnp.broadcast_to(jnp.max(mp_ref[h], axis=-1, keepdims=True), (tq, LANES))
    for t in range(4):
        cols = slice(t * LANES, (t + 1) * LANES)
        vwin = vpad_ref[pl.ds(start, win), cols]
        outs = []
        for half in range(2):
            h = 2 * t + half
            m = mp_ref[h]
            p = jnp.exp2(s_ref[h] - jnp.concatenate([m] * ntile, axis=1))
            l = jnp.sum(functools.reduce(jnp.add, lane_tiles(p)), axis=-1, keepdims=True)
            outs.append(jnp.dot(p.astype(BF16), vwin, preferred_element_type=F32) / l)
        o_ref[:, cols] = (jnp.where(lo, outs[0], outs[1]) * gate[:, cols]).astype(BF16)


def _mixer_d(h, qg, kg, base, bsz, seq, tq=256):
    nt = seq // tq
    kern = functools.partial(_mixer_d_kernel, seq=seq, tq=tq)
    full = lambda shape: pl.BlockSpec(shape, lambda b, t: (0,) * len(shape))
    return pl.pallas_call(
        kern,
        grid=(bsz, nt),
        in_specs=[pl.BlockSpec((tq, GROUP_W), lambda b, t: (b * nt + t, BLK_D_Q)),
                  pl.BlockSpec((seq, GROUP_W), lambda b, t: (b, BLK_D_K)),
                  pl.BlockSpec((seq, GROUP_W), lambda b, t: (b, BLK_D_V)),
                  pl.BlockSpec((tq, GROUP_W), lambda b, t: (b * nt + t, BLK_D_Z)),
                  full((1, GROUP_W)), full((1, GROUP_W)), full((N_HEADS64, 2 * tq + D_LEFT))],
        out_specs=pl.BlockSpec((tq, GROUP_W), lambda b, t: (b * nt + t, 0)),
        out_shape=jax.ShapeDtypeStruct((bsz * seq, GROUP_W), BF16),
        scratch_shapes=[pltpu.VMEM((seq + D_LEFT, GROUP_W), BF16), pltpu.VMEM((seq + D_LEFT, GROUP_W), BF16),
                        pltpu.VMEM((N_HEADS64, tq, tq + D_LEFT), F32),
                        pltpu.VMEM((N_HEADS64, tq, tq + D_LEFT), F32), pltpu.VMEM((N_HEADS64, tq, LANES), F32)],
        compiler_params=_params(2),
    )(h, h, h, h, qg, kg, base)


def _w_in_pieces(w, zeros):
    c = lambda name, size, off=0: w[..., _SRC[name] + off:_SRC[name] + off + size]
    return [c("a_u", 512), c("a_v", 512), c("a_z", 512),
            c("b_q", 512), c("b_iq", 512), c("b_z", 512),
            c("b_k", 64), c("b_k", 64), c("b_v", 64), c("b_v", 64), c("b_ik", 64), c("b_ik", 64),
            c("c_kr", 32), c("b_iw", 8), zeros(24), c("c_kr", 32, 32), zeros(32),
            c("c_q", 384), c("c_kv", 128), c("c_z", 512),
            c("d_q", 512), c("d_k", 512), c("d_v", 512), c("d_z", 512)]


def _layout_w_in_kernel(w_ref, o_ref):
    w = w_ref[0]
    pieces = _w_in_pieces(w, lambda n: jnp.zeros((w.shape[0], n), w.dtype))
    widths = np.cumsum([0] + [p.shape[1] for p in pieces])
    start = 0
    for i in range(1, len(pieces) + 1):
        if widths[i] % GROUP_W == 0:
            group = pieces[start:i]
            blk = group[0] if len(group) == 1 else jnp.concatenate(group, axis=1)
            o_ref[0, :, widths[start]:widths[i]] = blk.astype(BF16)
            start = i


def _layout_w_in(w_in, tr=256):
    depth, d, cols = w_in.shape
    return pl.pallas_call(
        _layout_w_in_kernel,
        grid=(depth, d // tr),
        in_specs=[pl.BlockSpec((1, tr, cols), lambda l, i: (l, i, 0))],
        out_specs=pl.BlockSpec((1, tr, H_COLS), lambda l, i: (l, i, 0)),
        out_shape=jax.ShapeDtypeStruct((depth, d, H_COLS), BF16),
        compiler_params=_params(2),
    )(w_in)


def _rope_layout(v):
    z = jnp.zeros(v.shape[:-1] + (32,), v.dtype)
    return jnp.concatenate([v[..., :32], z, v[..., 32:], z], axis=-1)


def _layout_c(w_qb, w_kvb, q_gain, k_gain, qa_gain):
    wq = w_qb.reshape(Q_LORA, C_HEADS, C_QK)
    wq = jnp.concatenate([wq[..., :C_NOPE], _rope_layout(wq[..., C_NOPE:])], axis=-1)
    wq = wq.reshape(Q_LORA, C_HEADS * 256).astype(BF16)
    wkv = w_kvb.reshape(KV_LORA, C_HEADS, 2 * LANES)
    wkv = jnp.concatenate([wkv[..., :C_NOPE].reshape(KV_LORA, -1), wkv[..., C_NOPE:].reshape(KV_LORA, -1)],
                          axis=1).astype(BF16)
    lay = lambda g: jnp.concatenate([g[:C_NOPE], _rope_layout(g[C_NOPE:])])[None, :]
    return wq, wkv, lay(q_gain), lay(k_gain), qa_gain[None, :]


def _t5_bucket_static(rel):
    half = T5_BUCKETS // 2
    exact = half // 2
    n = abs(rel)
    if n < exact:
        val = n
    else:
        val = min(exact + (n * n // (exact * exact)).bit_length() - 1, half - 1)
    return (half if rel > 0 else 0) + val


def _t5_tables(t5_bias):
    m = np.arange(512)
    d0 = np.where(m < 256, m, m - 512)
    d1 = np.where(m <= 256, m - 256, m - 768)
    idx0 = np.array([_t5_bucket_static(int(d)) for d in d0], np.int32)
    idx1 = np.array([_t5_bucket_static(int(d)) for d in d1], np.int32)
    far = _t5_bucket_static(-512)
    return t5_bias[idx0].T, t5_bias[idx1].T, t5_bias[far]


def _band_table(rel_bias, tq):
    width = 2 * tq + D_LEFT
    m = np.arange(width)
    dist = np.where(m <= tq + D_LEFT, D_LEFT - m, D_LEFT + width - m)
    idx = np.clip(dist, -REL_CLIP, REL_CLIP) + REL_CLIP
    return rel_bias[idx.astype(np.int32)].T


def _rope_tables(seq):
    inv = ROPE_BASE ** (-jnp.arange(0, C_ROPE, 2, dtype=F32) / C_ROPE)
    ang = jnp.arange(seq, dtype=F32)[:, None] * inv[None, :]
    c, s = jnp.cos(ang), jnp.sin(ang)
    z = jnp.zeros_like(c)
    return jnp.concatenate([c, z, c, z], axis=1), jnp.concatenate([-s, z, s, z], axis=1)


def kernel(x, t5_bias, norm_g, w_in, a_v_gain, a_ws, a_bs, b_q_gain, b_k_gain, c_qa_gain, c_kva_gain,
           c_w_qb, c_w_kvb, c_q_gain, c_k_gain, d_q_gain, d_k_gain, d_rel_bias, w_out):
    bsz, seq, d_model = x.shape
    depth = w_in.shape[0]
    tq = 256
    x2 = x.reshape(bsz * seq, d_model)
    cos, sin = _rope_tables(seq)
    base0, base1, cfar = _t5_tables(t5_bias)
    w_in_blocks = _layout_w_in(w_in)
    for l in range(depth):
        h = _inproj(x2, norm_g[l][None, :], w_in_blocks, l)
        y_a = _mixer_a(h, a_v_gain[l][None, :], a_ws[l], a_bs[l][:, :, None])
        mask = _b_select(h, bsz, seq, tq)
        y_b = _b_attn(h, mask, cfar, jnp.tile(b_q_gain[l], N_HEADS64)[None, :],
                      jnp.tile(b_k_gain[l], 2)[None, :], base0, base1, bsz, seq, tq)
        wq, wkv, qg, kg, qag = _layout_c(c_w_qb[l], c_w_kvb[l], c_q_gain[l], c_k_gain[l], c_qa_gain[l])
        qc, kc, vc = _c_prep(h, wq, wkv, qag, c_kva_gain[l][None, :], qg, kg, cos, sin, seq)
        y_c = _c_attn(qc, kc, vc, h, bsz, seq, tq)
        y_d = _mixer_d(h, jnp.tile(d_q_gain[l], N_HEADS64)[None, :], jnp.tile(d_k_gain[l], N_HEADS64)[None, :],
                       _band_table(d_rel_bias[l], tq), bsz, seq, tq)
        x2 = _outproj(x2, (y_a, y_b, y_c, y_d), w_out[l].astype(BF16))
    return x2.reshape(bsz, seq, d_model)
```

```python
import functools
import math

import numpy as np
import jax
import jax.numpy as jnp
from jax import lax
from jax.experimental import pallas as pl
from jax.experimental.pallas import tpu as pltpu

F32 = jnp.float32
BF16 = jnp.bfloat16

EPS = 1e-6
NEG = -1e30
LOG2E = math.log2(math.e)
INT_MIN = -(2 ** 31)
CHUNK = 64
LANES = 128
GROUP_W = 512
A_GROUPS = 4
GMLP_BLOCK = 128
N_HEADS64 = 8
IDX_SCALE = (8 ** -0.5) * 0.125
TOPK_MAX = 256
T5_BUCKETS = 32
C_HEADS = 4
C_NOPE = 128
C_ROPE = 64
C_QK = 192
Q_LORA = 384
KV_LORA = 128
ROPE_BASE = 10000.0
D_LEFT = 8 * CHUNK
REL_CLIP = 128
VMEM_LIMIT = 56 * 1024 * 1024

BLK_A_U, BLK_A_V, BLK_A_Z = 0, 1, 2
BLK_B_Q, BLK_B_IQ, BLK_B_Z, BLK_SMALL = 3, 4, 5, 6
BLK_C_QKV, BLK_C_Z = 7, 8
BLK_D_Q, BLK_D_K, BLK_D_V, BLK_D_Z = 9, 10, 11, 12
H_COLS = 13 * GROUP_W
UNIT_B_K, UNIT_B_V, UNIT_B_IK, UNIT_KR_IW = (BLK_SMALL * 4 + i for i in range(4))
IW_LANE = 32

_SRC = dict(a_u=0, a_v=512, a_z=1024, b_q=1536, b_k=2048, b_v=2112, b_iq=2176, b_ik=2688,
            b_iw=2752, b_z=2760, c_q=3272, c_kv=3656, c_kr=3784, c_z=3848,
            d_q=4360, d_k=4872, d_v=5384, d_z=5896)


def _params(n_axes):
    return pltpu.CompilerParams(dimension_semantics=("arbitrary",) * n_axes,
                                vmem_limit_bytes=VMEM_LIMIT)


def _gelu(x):
    c = math.sqrt(2.0 / math.pi)
    return x * (0.5 * (1.0 + jnp.tanh(c * (x + 0.044715 * (x * x * x)))))


def _silu(x):
    return x * (1.0 / (1.0 + jnp.exp(-x)))


def _dot_t(a, b):
    return lax.dot_general(a, b, (((1,), (1,)), ((), ())), preferred_element_type=F32)


def _lo_mask(rows):
    return lax.broadcasted_iota(jnp.int32, (rows, LANES), 1) < 64


def _rms_heads64(x, gain, ntiles):
    lo = _lo_mask(x.shape[0])
    tiles = []
    for t in range(ntiles):
        xt = x[:, t * LANES:(t + 1) * LANES]
        sq = xt * xt
        s_lo = jnp.sum(jnp.where(lo, sq, 0.0), axis=-1, keepdims=True)
        s_hi = jnp.sum(jnp.where(lo, 0.0, sq), axis=-1, keepdims=True)
        r = jnp.where(lo, lax.rsqrt(s_lo * (1.0 / 64) + EPS), lax.rsqrt(s_hi * (1.0 / 64) + EPS))
        tiles.append(xt * r * gain[:, t * LANES:(t + 1) * LANES])
    return tiles


def _toeplitz(base_row, rows, width):
    t = jnp.broadcast_to(base_row, (rows, base_row.shape[1]))
    t = pltpu.roll(t, 0, 1, stride=1, stride_axis=0)
    return t[:, :width]


def _inproj_kernel(x_ref, g_ref, w_ref, o_ref):
    x = x_ref[...]
    ms = jnp.mean(x * x, axis=-1, keepdims=True)
    xn = (x * lax.rsqrt(ms + EPS) * g_ref[...]).astype(BF16)
    for c in range(H_COLS // GROUP_W):
        cols = slice(c * GROUP_W, (c + 1) * GROUP_W)
        o_ref[:, cols] = _dot_t(xn, w_ref[cols, :]).astype(BF16)


def _inproj(x2, g, w_all, layer, tm=512):
    n, d = x2.shape
    return pl.pallas_call(
        _inproj_kernel,
        grid=(n // tm,),
        in_specs=[pl.BlockSpec((tm, d), lambda i: (i, 0)),
                  pl.BlockSpec((1, d), lambda i: (0, 0)),
                  pl.BlockSpec((None, H_COLS, d), lambda i: (layer, 0, 0), pipeline_mode=pl.Buffered(1))],
        out_specs=pl.BlockSpec((tm, H_COLS), lambda i: (i, 0)),
        out_shape=jax.ShapeDtypeStruct((n, H_COLS), BF16),
        compiler_params=_params(1),
    )(x2, g, w_all)


def _outproj_kernel(x_ref, ya_ref, yb_ref, yc_ref, yd_ref, w_ref, o_ref):
    acc = x_ref[...]
    for g, y_ref in enumerate((ya_ref, yb_ref, yc_ref, yd_ref)):
        acc = acc + jnp.dot(y_ref[...], w_ref[g * GROUP_W:(g + 1) * GROUP_W, :],
                            preferred_element_type=F32)
    o_ref[...] = acc


def _outproj(x2, ys, w, tm=512):
    n, d = x2.shape
    yspec = pl.BlockSpec((tm, GROUP_W), lambda i: (i, 0))
    return pl.pallas_call(
        _outproj_kernel,
        grid=(n // tm,),
        in_specs=[pl.BlockSpec((tm, d), lambda i: (i, 0)), yspec, yspec, yspec, yspec,
                  pl.BlockSpec((4 * GROUP_W, d), lambda i: (0, 0))],
        out_specs=pl.BlockSpec((tm, d), lambda i: (i, 0)),
        out_shape=jax.ShapeDtypeStruct((n, d), F32),
        compiler_params=_params(1),
    )(x2, *ys, w)


def _mixer_a_kernel(u_ref, v_ref, z_ref, vg_ref, w_ref, b_ref, o_ref):
    tm = u_ref.shape[0]
    u = _gelu(u_ref[...].astype(F32))
    v = _gelu(v_ref[...].astype(F32))
    ms = jnp.mean(v * v, axis=-1, keepdims=True)
    vb = (v * lax.rsqrt(ms + EPS) * vg_ref[...]).astype(BF16)
    gate = _silu(z_ref[...].astype(F32))
    i = lax.broadcasted_iota(jnp.int32, (GMLP_BLOCK, GMLP_BLOCK), 0)
    j = lax.broadcasted_iota(jnp.int32, (GMLP_BLOCK, GMLP_BLOCK), 1)
    keep = (j // CHUNK) <= (i // CHUNK)
    for g in range(A_GROUPS):
        wg = jnp.where(keep, w_ref[g], 0.0).astype(BF16)
        cols = slice(g * LANES, (g + 1) * LANES)
        for blk in range(tm // GMLP_BLOCK):
            rows = slice(blk * GMLP_BLOCK, (blk + 1) * GMLP_BLOCK)
            sg = jnp.dot(wg, vb[rows, cols], preferred_element_type=F32) + b_ref[g]
            o_ref[rows, cols] = (u[rows, cols] * sg * gate[rows, cols]).astype(BF16)


def _mixer_a(h, vg, ws, bs, tm=512):
    n = h.shape[0]
    hspec = lambda blk: pl.BlockSpec((tm, GROUP_W), lambda i, blk=blk: (i, blk))
    return pl.pallas_call(
        _mixer_a_kernel,
        grid=(n // tm,),
        in_specs=[hspec(BLK_A_U), hspec(BLK_A_V), hspec(BLK_A_Z),
                  pl.BlockSpec((1, GROUP_W), lambda i: (0, 0)),
                  pl.BlockSpec((A_GROUPS, GMLP_BLOCK, GMLP_BLOCK), lambda i: (0, 0, 0)),
                  pl.BlockSpec((A_GROUPS, GMLP_BLOCK, 1), lambda i: (0, 0, 0))],
        out_specs=pl.BlockSpec((tm, GROUP_W), lambda i: (i, 0)),
        out_shape=jax.ShapeDtypeStruct((n, GROUP_W), BF16),
        compiler_params=_params(1),
    )(h, h, h, vg, ws, bs)


def _order_key(x):
    return jnp.where(x < 0, x ^ 0x7FFFFFFF, x)


def _b_select_kernel(iq_ref, iw_ref, ik_ref, o_ref, lhs_ref, key_ref, *, seq, tq, topk):
    t_blk = pl.program_id(1)
    nkb = seq // 256
    nblk = ((t_blk + 1) * tq) // 256
    n_interp = 12
    hrows = [slice(h * tq, (h + 1) * tq) for h in range(N_HEADS64)]

    lo_half = _lo_mask(tq)
    w_t = (iw_ref[...].astype(F32) * IDX_SCALE).T
    for h in range(N_HEADS64):
        iqt = iq_ref[:, (h // 2) * LANES:(h // 2 + 1) * LANES]
        sel = lo_half if h % 2 == 0 else jnp.logical_not(lo_half)
        lhs_ref[hrows[h], :] = jnp.where(sel, iqt, jnp.zeros_like(iqt))

    qpos = t_blk * tq + lax.broadcasted_iota(jnp.int32, (256, tq), 1)
    krow = lax.broadcasted_iota(jnp.int32, (256, tq), 0)

    def score_block(kb, amax, masked):
        off = pl.multiple_of(kb * 256, 256)
        ikblk = ik_ref[pl.ds(off, 256), :]
        score = jnp.zeros((256, tq), F32)
        for h in range(N_HEADS64):
            w_h = w_t[IW_LANE + h:IW_LANE + h + 1, :]
            score = score + w_h * jnp.maximum(_dot_t(ikblk, lhs_ref[hrows[h], :]), 0.0)
        score = jnp.where(score == 0.0, 0.0, score)
        key = _order_key(lax.bitcast_convert_type(score, jnp.int32))
        if masked:
            key = jnp.where(((kb * 256 + krow) // CHUNK) <= (qpos // CHUNK), key, INT_MIN)
        key_ref[kb] = key
        return jnp.maximum(amax, jnp.max(jnp.abs(score).reshape(256 // 8, 8, tq), axis=0))

    nfull = (t_blk * tq) // 256
    amax = lax.fori_loop(0, nfull, lambda kb, a: score_block(kb, a, False), jnp.zeros((8, tq), F32))
    amax = lax.fori_loop(nfull, nblk, lambda kb, a: score_block(kb, a, True), amax)
    amax = jnp.max(amax, axis=0, keepdims=True)

    def count(pred):
        def body(kb, acc):
            ones = jnp.where(pred(key_ref[kb], kb), 1.0, 0.0)
            return acc + jnp.sum(ones.reshape(256 // 8, 8, tq), axis=0)
        return jnp.sum(lax.fori_loop(0, nblk, body, jnp.zeros((8, tq), F32)), axis=0, keepdims=True)

    kf = float(topk)
    qrow = t_blk * tq + lax.broadcasted_iota(jnp.int32, (1, tq), 1)
    n_adm = ((qrow // CHUNK + 1) * CHUNK).astype(F32)
    one = jnp.ones((1, tq), jnp.int32)
    f_pos = count(lambda k, kb: k >= one)
    f_nn = count(lambda k, kb: k >= one - 1)
    pos = f_pos > kf
    neg = f_nn < kf
    lo0 = jnp.where(pos, one, _order_key(lax.bitcast_convert_type(-amax, jnp.int32)))
    hi0 = jnp.where(neg, one - 1, _order_key(lax.bitcast_convert_type(amax, jnp.int32)) + 1)
    w_lo0 = jnp.where(pos, f_pos, n_adm) - kf
    w_hi0 = kf - jnp.where(neg, f_nn, 0.0)
    all_sel = n_adm <= kf
    at_zero = jnp.logical_not(pos | neg)
    done0 = jnp.where(all_sel | at_zero | (hi0 == lo0 + 1), 1.0, 0.0)
    thr0 = jnp.where(all_sel, INT_MIN + 1, jnp.where(at_zero, jnp.where(f_pos == kf, one, one - 1), lo0))

    def search_cond(st):
        return jnp.logical_and(st[0] < n_interp + 32, st[1] < 0.5)

    def search_step(st):
        it, _, lo, hi, w_lo, w_hi, side, done, thr = st
        lo_v = lax.bitcast_convert_type(_order_key(lo), F32)
        hi_v = lax.bitcast_convert_type(_order_key(hi), F32)
        c_v = lo_v + (hi_v - lo_v) * (w_lo / (w_lo + w_hi))
        c_interp = _order_key(lax.bitcast_convert_type(c_v, jnp.int32))
        c_mid = (lo >> 1) + (hi >> 1) + (lo & hi & 1)
        cand = jnp.where(it < n_interp, c_interp, c_mid)
        cand = jnp.minimum(jnp.maximum(cand, lo + 1), hi - 1)
        f = count(lambda k, kb: k >= cand)
        live = done < 0.5
        up = f > kf
        hit = f == kf
        new_lo = jnp.where(live & up, cand, lo)
        new_hi = jnp.where(live & jnp.logical_not(up), cand, hi)
        new_w_lo = jnp.where(up, f - kf, jnp.where(side < 0.0, 0.5 * w_lo, w_lo))
        new_w_hi = jnp.where(up, jnp.where(side > 0.0, 0.5 * w_hi, w_hi), kf - f)
        new_side = jnp.where(up, 1.0, -1.0)
        new_thr = jnp.where(live, jnp.where(hit, cand, new_lo), thr)
        new_done = jnp.where(live & (hit | (new_hi == new_lo + 1)), 1.0, done)
        return (it + 1, jnp.min(new_done), new_lo, new_hi, jnp.where(live, new_w_lo, w_lo),
                jnp.where(live, new_w_hi, w_hi), jnp.where(live, new_side, side), new_done, new_thr)

    state = (jnp.int32(0), jnp.min(done0), lo0, hi0, w_lo0, w_hi0, jnp.zeros((1, tq), F32), done0, thr0)
    thr = lax.while_loop(search_cond, search_step, state)[-1]

    need = kf - count(lambda k, kb: k > thr)
    ties = count(lambda k, kb: k == thr)
    any_excess = jnp.max(ties - need) > 0.0

    def write_unused(kb, carry):
        o_ref[0, kb] = jnp.zeros((tq, 256), F32)
        return carry

    lax.fori_loop(nblk, nkb, write_unused, 0)

    @pl.when(jnp.logical_not(any_excess))
    def _():
        def write(kb, carry):
            o_ref[0, kb] = jnp.where(key_ref[kb] >= thr, 1.0, 0.0).T
            return carry

        lax.fori_loop(0, nblk, write, 0)

    @pl.when(any_excess)
    def _():
        def idx_step(it, jmax):
            cand = jmax | lax.shift_left(jnp.int32(1), 10 - it)
            below = count(lambda k, kb: (k == thr) & ((kb * 256 + krow) < cand))
            return jnp.where(below < need, cand, jmax)

        jmax = lax.fori_loop(0, 11, idx_step, jnp.zeros((1, tq), jnp.int32))

        def write(kb, carry):
            k = key_ref[kb]
            keep_tie = (k == thr) & ((kb * 256 + krow) <= jmax)
            o_ref[0, kb] = jnp.where((k > thr) | keep_tie, 1.0, 0.0).T
            return carry

        lax.fori_loop(0, nblk, write, 0)


def _b_select(h, bsz, seq, tq=256):
    nt = seq // tq
    topk = min(TOPK_MAX, seq // 4)
    kern = functools.partial(_b_select_kernel, seq=seq, tq=tq, topk=topk)
    return pl.pallas_call(
        kern,
        grid=(bsz, nt),
        in_specs=[pl.BlockSpec((tq, GROUP_W), lambda b, t: (b * nt + t, BLK_B_IQ)),
                  pl.BlockSpec((tq, LANES), lambda b, t: (b * nt + t, UNIT_KR_IW)),
                  pl.BlockSpec((seq, LANES), lambda b, t: (b, UNIT_B_IK))],
        out_specs=pl.BlockSpec((1, seq // 256, tq, 256), lambda b, t: (b * nt + t, 0, 0, 0)),
        out_shape=jax.ShapeDtypeStruct((bsz * nt, seq // 256, tq, 256), F32),
        scratch_shapes=[pltpu.VMEM((N_HEADS64 * tq, LANES), BF16),
                        pltpu.VMEM((seq // 256, 256, tq), jnp.int32)],
        compiler_params=_params(2),
    )(h, h, h)


def _b_attn_kernel(cfar_ref, q_ref, z_ref, k_ref, v_ref, msk_ref, qg_ref, kg_ref, base0_ref, base1_ref,
                   o_ref, kn_ref, v1_ref, bias_ref, qall_ref, s_ref, mp_ref, acc_ref, *, seq, tq):
    b = pl.program_id(0)
    t_blk = pl.program_id(1)
    hrows = [slice(h * tq, (h + 1) * tq) for h in range(N_HEADS64)]

    @pl.when((b == 0) & (t_blk == 0))
    def _():
        for h in range(N_HEADS64):
            bias_ref[0, hrows[h], :] = jnp.full((tq, 256), cfar_ref[h] * LOG2E, F32)
            bias_ref[1, hrows[h], :] = _toeplitz(base1_ref[h:h + 1, :], tq, 256) * LOG2E
            bias_ref[2, hrows[h], :] = _toeplitz(base0_ref[h:h + 1, :], tq, 256) * LOG2E

    @pl.when(t_blk == 0)
    def _():
        lo256 = _lo_mask(256)
        for r in range(seq // 256):
            rows = slice(r * 256, (r + 1) * 256)
            k = k_ref[rows, :].astype(F32)
            ms = jnp.mean(k * k, axis=-1, keepdims=True)
            kn_ref[rows, :] = (k * lax.rsqrt(ms + EPS) * kg_ref[...]).astype(BF16)
            v = v_ref[rows, :]
            v1_ref[rows, :] = jnp.where(lo256, v, jnp.ones_like(v))

    lo = _lo_mask(tq)
    qtiles = _rms_heads64(q_ref[...].astype(F32), qg_ref[...], 4)
    for h in range(N_HEADS64):
        sel = lo if h % 2 == 0 else jnp.logical_not(lo)
        qall_ref[hrows[h], :] = jnp.where(sel, qtiles[h // 2] * (0.125 * LOG2E), 0.0).astype(BF16)
    mp_ref[...] = jnp.full(mp_ref.shape, NEG, F32)
    acc_ref[...] = jnp.zeros(acc_ref.shape, F32)
    nblk = t_blk + 1

    def logits_blocks(kb0, n):
        for i in range(n):
            kb = kb0 + i
            off = pl.multiple_of(kb * 256, 256)
            kblk = kn_ref[pl.ds(off, 256), :]
            which = jnp.clip(kb - (t_blk - 2), 0, 2)
            keep = msk_ref[0, kb] > 0.5
            for h in range(N_HEADS64):
                s = jnp.where(keep, _dot_t(qall_ref[hrows[h], :], kblk) + bias_ref[which, hrows[h], :], NEG)
                s_ref[kb, hrows[h], :] = s
                mp_ref[hrows[h], :] = jnp.maximum(mp_ref[hrows[h], :], jnp.maximum(s[:, :LANES], s[:, LANES:]))

    def value_blocks(kb0, n):
        off = pl.multiple_of(kb0 * 256, 256)
        v1 = v1_ref[pl.ds(off, n * 256), :]
        for h in range(N_HEADS64):
            m = mp_ref[hrows[h], :]
            mm = jnp.concatenate([m, m], axis=1)
            p = [jnp.exp2(s_ref[kb0 + i, hrows[h], :] - mm).astype(BF16) for i in range(n)]
            p = p[0] if n == 1 else jnp.concatenate(p, axis=1)
            acc_ref[hrows[h], :] += jnp.dot(p, v1, preferred_element_type=F32)

    def pairs_then_rest(blocks):
        def pair(i, carry):
            blocks(2 * i, 2)
            return carry

        lax.fori_loop(0, nblk // 2, pair, 0)

        @pl.when(nblk % 2 == 1)
        def _():
            blocks(nblk - 1, 1)

    pairs_then_rest(logits_blocks)
    for h in range(N_HEADS64):
        m = jnp.max(mp_ref[hrows[h], :], axis=-1, keepdims=True)
        mp_ref[hrows[h], :] = jnp.broadcast_to(m, (tq, LANES))
    pairs_then_rest(value_blocks)

    gate = _silu(z_ref[...].astype(F32))
    for t in range(4):
        a_even = acc_ref[hrows[2 * t], :]
        a_odd = acc_ref[hrows[2 * t + 1], :]
        o_even = a_even / pltpu.roll(a_even, 64, 1)
        o_odd = pltpu.roll(a_odd, 64, 1) / a_odd
        cols = slice(t * LANES, (t + 1) * LANES)
        o_ref[:, cols] = (jnp.where(lo, o_even, o_odd) * gate[:, cols]).astype(BF16)


def _b_attn(h, mask, cfar, qg, kg, base0, base1, bsz, seq, tq=256):
    nt = seq // tq
    rows = N_HEADS64 * tq
    kern = functools.partial(_b_attn_kernel, seq=seq, tq=tq)
    full = lambda shape: pl.BlockSpec(shape, lambda b, t: (0,) * len(shape))
    return pl.pallas_call(
        kern,
        grid=(bsz, nt),
        in_specs=[pl.BlockSpec(memory_space=pltpu.SMEM),
                  pl.BlockSpec((tq, GROUP_W), lambda b, t: (b * nt + t, BLK_B_Q)),
                  pl.BlockSpec((tq, GROUP_W), lambda b, t: (b * nt + t, BLK_B_Z)),
                  pl.BlockSpec((seq, LANES), lambda b, t: (b, UNIT_B_K)),
                  pl.BlockSpec((seq, LANES), lambda b, t: (b, UNIT_B_V)),
                  pl.BlockSpec((1, seq // 256, tq, 256), lambda b, t: (b * nt + t, 0, 0, 0)),
                  full((1, GROUP_W)), full((1, LANES)), full((N_HEADS64, 512)), full((N_HEADS64, 512))],
        out_specs=pl.BlockSpec((tq, GROUP_W), lambda b, t: (b * nt + t, 0)),
        out_shape=jax.ShapeDtypeStruct((bsz * seq, GROUP_W), BF16),
        scratch_shapes=[pltpu.VMEM((seq, LANES), BF16), pltpu.VMEM((seq, LANES), BF16),
                        pltpu.VMEM((3, rows, 256), F32), pltpu.VMEM((rows, LANES), BF16),
                        pltpu.VMEM((seq // 256, rows, 256), F32),
                        pltpu.VMEM((rows, LANES), F32), pltpu.VMEM((rows, LANES), F32)],
        compiler_params=_params(2),
    )(cfar, h, h, h, h, mask, qg, kg, base0, base1)


def _rope(tile, cos, sin):
    return tile * cos + pltpu.roll(tile, 64, 1) * sin


def _c_prep_kernel(lat_ref, kr_ref, wq_ref, wkv_ref, qag_ref, kvag_ref, qg_ref, kg_ref, cos_ref, sin_ref,
                   qo_ref, ko_ref, vo_ref):
    cq = lat_ref[:, :Q_LORA].astype(F32)
    ms = jnp.mean(cq * cq, axis=-1, keepdims=True)
    cqn = (cq * lax.rsqrt(ms + EPS) * qag_ref[...]).astype(BF16)
    qpre = jnp.dot(cqn, wq_ref[...], preferred_element_type=F32)
    ckv = lat_ref[:, Q_LORA:].astype(F32)
    ms = jnp.mean(ckv * ckv, axis=-1, keepdims=True)
    ckvn = (ckv * lax.rsqrt(ms + EPS) * kvag_ref[...]).astype(BF16)
    kvpre = jnp.dot(ckvn, wkv_ref[...], preferred_element_type=F32)
    lane = lax.broadcasted_iota(jnp.int32, kr_ref.shape, 1)
    kr = jnp.where((lane % 64) < 32, kr_ref[...].astype(F32), 0.0)
    kr_ss = jnp.sum(kr * kr, axis=-1, keepdims=True)
    cos = cos_ref[...]
    sin = sin_ref[...]
    qg = qg_ref[...]
    kg = kg_ref[...]
    for h in range(C_HEADS):
        qh = qpre[:, h * 256:(h + 1) * 256]
        r = lax.rsqrt(jnp.sum(qh * qh, axis=-1, keepdims=True) * (1.0 / C_QK) + EPS)
        qn = qh * r * qg
        qo_ref[:, h * 256:h * 256 + LANES] = qn[:, :LANES].astype(BF16)
        qo_ref[:, h * 256 + LANES:(h + 1) * 256] = _rope(qn[:, LANES:], cos, sin).astype(BF16)
        kn = kvpre[:, h * LANES:(h + 1) * LANES]
        r = lax.rsqrt((jnp.sum(kn * kn, axis=-1, keepdims=True) + kr_ss) * (1.0 / C_QK) + EPS)
        ko_ref[:, h * 256:h * 256 + LANES] = (kn * r * kg[:, :LANES]).astype(BF16)
        ko_ref[:, h * 256 + LANES:(h + 1) * 256] = _rope(kr * r * kg[:, LANES:], cos, sin).astype(BF16)
    vo_ref[...] = kvpre[:, C_HEADS * LANES:].astype(BF16)


def _c_prep(h, wq, wkv, qag, kvag, qg, kg, cos, sin, seq, tm=512):
    n = h.shape[0]
    ns = seq // tm
    full = lambda shape: pl.BlockSpec(shape, lambda i: (0,) * len(shape))
    return pl.pallas_call(
        _c_prep_kernel,
        grid=(n // tm,),
        in_specs=[pl.BlockSpec((tm, GROUP_W), lambda i: (i, BLK_C_QKV)),
                  pl.BlockSpec((tm, LANES), lambda i: (i, UNIT_KR_IW)),
                  full((Q_LORA, 4 * 256)), full((KV_LORA, 8 * LANES)),
                  full((1, Q_LORA)), full((1, LANES)), full((1, 256)), full((1, 256)),
                  pl.BlockSpec((tm, LANES), lambda i: (i % ns, 0)),
                  pl.BlockSpec((tm, LANES), lambda i: (i % ns, 0))],
        out_specs=[pl.BlockSpec((tm, 4 * 256), lambda i: (i, 0)),
                   pl.BlockSpec((tm, 4 * 256), lambda i: (i, 0)),
                   pl.BlockSpec((tm, GROUP_W), lambda i: (i, 0))],
        out_shape=[jax.ShapeDtypeStruct((n, 4 * 256), BF16), jax.ShapeDtypeStruct((n, 4 * 256), BF16),
                   jax.ShapeDtypeStruct((n, GROUP_W), BF16)],
        compiler_params=_params(1),
    )(h, h, wq, wkv, qag, kvag, qg, kg, cos, sin)


def _c_attn_kernel(q_ref, k_ref, v_ref, z_ref, o_ref, s_ref, mp_ref, lp_ref, acc_ref, *, tq):
    qt = pl.program_id(1)
    scale = C_QK ** -0.5 * LOG2E
    hrows = [slice(h * tq, (h + 1) * tq) for h in range(C_HEADS)]
    qchunk = (qt * tq + lax.broadcasted_iota(jnp.int32, (tq, 256), 0)) // CHUNK
    kcol = lax.broadcasted_iota(jnp.int32, (tq, 256), 1)
    mp_ref[...] = jnp.full(mp_ref.shape, NEG, F32)
    lp_ref[...] = jnp.zeros(lp_ref.shape, F32)
    acc_ref[...] = jnp.zeros(acc_ref.shape, F32)
    nfull = (qt * tq) // 256
    nblk = ((qt + 1) * tq) // 256

    def logits_blocks(kb0, n, masked):
        for i in range(n):
            kb = kb0 + i
            off = pl.multiple_of(kb * 256, 256)
            for h in range(C_HEADS):
                cols = slice(h * 256, (h + 1) * 256)
                s = _dot_t(q_ref[:, cols], k_ref[pl.ds(off, 256), cols]) * scale
                if masked:
                    s = jnp.where(((kb * 256 + kcol) // CHUNK) <= qchunk, s, NEG)
                s_ref[kb, hrows[h], :] = s
                mp_ref[hrows[h], :] = jnp.maximum(mp_ref[hrows[h], :], jnp.maximum(s[:, :LANES], s[:, LANES:]))

    def value_blocks(kb0, n):
        off = pl.multiple_of(kb0 * 256, 256)
        for h in range(C_HEADS):
            m = mp_ref[hrows[h], :]
            mm = jnp.concatenate([m, m], axis=1)
            p = [jnp.exp2(s_ref[kb0 + i, hrows[h], :] - mm) for i in range(n)]
            lsum = p[0][:, :LANES] + p[0][:, LANES:]
            for pi in p[1:]:
                lsum = lsum + pi[:, :LANES] + pi[:, LANES:]
            lp_ref[hrows[h], :] += lsum
            pb = p[0].astype(BF16) if n == 1 else jnp.concatenate([pi.astype(BF16) for pi in p], axis=1)
            acc_ref[hrows[h], :] += jnp.dot(pb, v_ref[pl.ds(off, n * 256), h * LANES:(h + 1) * LANES],
                                            preferred_element_type=F32)

    def pairs_then_rest(blocks, count):
        def pair(i, carry):
            blocks(2 * i, 2)
            return carry

        lax.fori_loop(0, count // 2, pair, 0)

        @pl.when(count % 2 == 1)
        def _():
            blocks(count - 1, 1)

    pairs_then_rest(lambda kb0, n: logits_blocks(kb0, n, False), nfull)
    lax.fori_loop(nfull, nblk, lambda kb, c: (logits_blocks(kb, 1, True), c)[1], 0)
    for h in range(C_HEADS):
        m = jnp.max(mp_ref[hrows[h], :], axis=-1, keepdims=True)
        mp_ref[hrows[h], :] = jnp.broadcast_to(m, (tq, LANES))
    pairs_then_rest(value_blocks, nblk)
    gate = _silu(z_ref[...].astype(F32))
    for h in range(C_HEADS):
        cols = slice(h * LANES, (h + 1) * LANES)
        l = jnp.sum(lp_ref[hrows[h], :], axis=-1, keepdims=True)
        o_ref[:, cols] = (acc_ref[hrows[h], :] / l * gate[:, cols]).astype(BF16)


def _c_attn(qc, kc, vc, h, bsz, seq, tq=256):
    nt = seq // tq
    rows = C_HEADS * tq
    kern = functools.partial(_c_attn_kernel, tq=tq)
    return pl.pallas_call(
        kern,
        grid=(bsz, nt),
        in_specs=[pl.BlockSpec((tq, C_HEADS * 256), lambda b, t: (b * nt + t, 0)),
                  pl.BlockSpec((seq, C_HEADS * 256), lambda b, t: (b, 0)),
                  pl.BlockSpec((seq, GROUP_W), lambda b, t: (b, 0)),
                  pl.BlockSpec((tq, GROUP_W), lambda b, t: (b * nt + t, BLK_C_Z))],
        out_specs=pl.BlockSpec((tq, GROUP_W), lambda b, t: (b * nt + t, 0)),
        out_shape=jax.ShapeDtypeStruct((bsz * seq, GROUP_W), BF16),
        scratch_shapes=[pltpu.VMEM((seq // 256, rows, 256), F32), pltpu.VMEM((rows, LANES), F32),
                        pltpu.VMEM((rows, LANES), F32), pltpu.VMEM((rows, LANES), F32)],
        compiler_params=_params(2),
    )(qc, kc, vc, h)


def _mixer_d_kernel(q_ref, k_ref, v_ref, z_ref, qg_ref, kg_ref, base_ref, o_ref,
                    kpad_ref, vpad_ref, bias_ref, s_ref, mp_ref, *, seq, tq):
    b = pl.program_id(0)
    qt = pl.program_id(1)
    win = tq + D_LEFT

    @pl.when((b == 0) & (qt == 0))
    def _():
        for h in range(N_HEADS64):
            bias_ref[h] = _toeplitz(base_ref[h:h + 1, :], tq, win) * LOG2E

    @pl.when(qt == 0)
    def _():
        kpad_ref[0:D_LEFT, :] = jnp.zeros((D_LEFT, GROUP_W), BF16)
        vpad_ref[0:D_LEFT, :] = jnp.zeros((D_LEFT, GROUP_W), BF16)
        for r in range(seq // 256):
            rows = slice(r * 256, (r + 1) * 256)
            dst = slice(D_LEFT + r * 256, D_LEFT + (r + 1) * 256)
            tiles = _rms_heads64(k_ref[rows, :].astype(F32), kg_ref[...], 4)
            for t in range(4):
                kpad_ref[dst, t * LANES:(t + 1) * LANES] = tiles[t].astype(BF16)
            vpad_ref[dst, :] = v_ref[rows, :]

    lo = _lo_mask(tq)
    qtiles = _rms_heads64(q_ref[...].astype(F32), qg_ref[...], 4)
    start = pl.multiple_of(qt * tq, tq)
    qc = lax.broadcasted_iota(jnp.int32, (tq, win), 0) // CHUNK
    j = lax.broadcasted_iota(jnp.int32, (tq, win), 1)
    kc = j // CHUNK
    valid = (kc >= qc) & (kc <= qc + D_LEFT // CHUNK) & (j + start >= D_LEFT)
    gate = _silu(z_ref[...].astype(F32))
    ntile = win // LANES

    def lane_tiles(x):
        return [x[:, i * LANES:(i + 1) * LANES] for i in range(ntile)]

    for t in range(4):
        kwin = kpad_ref[pl.ds(start, win), t * LANES:(t + 1) * LANES]
        for half in range(2):
            h = 2 * t + half
            sel = lo if half == 0 else jnp.logical_not(lo)
            qh = jnp.where(sel, qtiles[t] * (0.125 * LOG2E), 0.0).astype(BF16)
            s = jnp.where(valid, _dot_t(qh, kwin) + bias_ref[h], NEG)
            s_ref[h] = s
            mp_ref[h] = functools.reduce(jnp.maximum, lane_tiles(s))
    for h in range(N_HEADS64):
        mp_ref[h] = jnp.broadcast_to(jnp.max(mp_ref[h], axis=-1, keepdims=True), (tq, LANES))
    for t in range(4):
        cols = slice(t * LANES, (t + 1) * LANES)
        vwin = vpad_ref[pl.ds(start, win), cols]
        outs = []
        for half in range(2):
            h = 2 * t + half
            m = mp_ref[h]
            p = jnp.exp2(s_ref[h] - jnp.concatenate([m] * ntile, axis=1))
            l = jnp.sum(functools.reduce(jnp.add, lane_tiles(p)), axis=-1, keepdims=True)
            outs.append(jnp.dot(p.astype(BF16), vwin, preferred_element_type=F32) / l)
        o_ref[:, cols] = (jnp.where(lo, outs[0], outs[1]) * gate[:, cols]).astype(BF16)


def _mixer_d(h, qg, kg, base, bsz, seq, tq=256):
    nt = seq // tq
    kern = functools.partial(_mixer_d_kernel, seq=seq, tq=tq)
    full = lambda shape: pl.BlockSpec(shape, lambda b, t: (0,) * len(shape))
    return pl.pallas_call(
        kern,
        grid=(bsz, nt),
        in_specs=[pl.BlockSpec((tq, GROUP_W), lambda b, t: (b * nt + t, BLK_D_Q)),
                  pl.BlockSpec((seq, GROUP_W), lambda b, t: (b, BLK_D_K)),
                  pl.BlockSpec((seq, GROUP_W), lambda b, t: (b, BLK_D_V)),
                  pl.BlockSpec((tq, GROUP_W), lambda b, t: (b * nt + t, BLK_D_Z)),
                  full((1, GROUP_W)), full((1, GROUP_W)), full((N_HEADS64, 2 * tq + D_LEFT))],
        out_specs=pl.BlockSpec((tq, GROUP_W), lambda b, t: (b * nt + t, 0)),
        out_shape=jax.ShapeDtypeStruct((bsz * seq, GROUP_W), BF16),
        scratch_shapes=[pltpu.VMEM((seq + D_LEFT, GROUP_W), BF16), pltpu.VMEM((seq + D_LEFT, GROUP_W), BF16),
                        pltpu.VMEM((N_HEADS64, tq, tq + D_LEFT), F32),
                        pltpu.VMEM((N_HEADS64, tq, tq + D_LEFT), F32), pltpu.VMEM((N_HEADS64, tq, LANES), F32)],
        compiler_params=_params(2),
    )(h, h, h, h, qg, kg, base)


def _w_in_pieces(take, zeros):
    c = lambda name, size, off=0: take(_SRC[name] + off, size)
    return [c("a_u", 512), c("a_v", 512), c("a_z", 512),
            c("b_q", 512), c("b_iq", 512), c("b_z", 512),
            c("b_k", 64), c("b_k", 64), c("b_v", 64), c("b_v", 64), c("b_ik", 64), c("b_ik", 64),
            c("c_kr", 32), c("b_iw", 8), zeros(24), c("c_kr", 32, 32), zeros(32),
            c("c_q", 384), c("c_kv", 128), c("c_z", 512),
            c("d_q", 512), c("d_k", 512), c("d_v", 512), c("d_z", 512)]


def _layout_w_in_kernel(w_ref, o_ref):
    tk = w_ref.shape[2]
    pieces = _w_in_pieces(lambda s, n: w_ref[0, s:s + n, :], lambda n: jnp.zeros((n, tk), F32))
    ends = np.cumsum([0] + [p.shape[0] for p in pieces])
    start = 0
    for i in range(1, len(pieces) + 1):
        if ends[i] % GROUP_W == 0:
            group = pieces[start:i]
            blk = group[0] if len(group) == 1 else jnp.concatenate(group, axis=0)
            o_ref[0, ends[start]:ends[i], :] = blk.astype(BF16)
            start = i


def _layout_w_in(w_in, tk=256):
    w_t = jnp.swapaxes(w_in, 1, 2)
    depth, cols, d = w_t.shape
    return pl.pallas_call(
        _layout_w_in_kernel,
        grid=(depth, d // tk),
        in_specs=[pl.BlockSpec((1, cols, tk), lambda l, i: (l, 0, i))],
        out_specs=pl.BlockSpec((1, H_COLS, tk), lambda l, i: (l, 0, i)),
        out_shape=jax.ShapeDtypeStruct((depth, H_COLS, d), BF16),
        compiler_params=_params(2),
    )(w_t)


def _rope_layout(v):
    z = jnp.zeros(v.shape[:-1] + (32,), v.dtype)
    return jnp.concatenate([v[..., :32], z, v[..., 32:], z], axis=-1)


def _layout_c(w_qb, w_kvb, q_gain, k_gain, qa_gain):
    wq = w_qb.reshape(Q_LORA, C_HEADS, C_QK)
    wq = jnp.concatenate([wq[..., :C_NOPE], _rope_layout(wq[..., C_NOPE:])], axis=-1)
    wq = wq.reshape(Q_LORA, C_HEADS * 256).astype(BF16)
    wkv = w_kvb.reshape(KV_LORA, C_HEADS, 2 * LANES)
    wkv = jnp.concatenate([wkv[..., :C_NOPE].reshape(KV_LORA, -1), wkv[..., C_NOPE:].reshape(KV_LORA, -1)],
                          axis=1).astype(BF16)
    lay = lambda g: jnp.concatenate([g[:C_NOPE], _rope_layout(g[C_NOPE:])])[None, :]
    return wq, wkv, lay(q_gain), lay(k_gain), qa_gain[None, :]


def _t5_bucket_static(rel):
    half = T5_BUCKETS // 2
    exact = half // 2
    n = abs(rel)
    if n < exact:
        val = n
    else:
        val = min(exact + (n * n // (exact * exact)).bit_length() - 1, half - 1)
    return (half if rel > 0 else 0) + val


def _t5_tables(t5_bias):
    m = np.arange(512)
    d0 = np.where(m < 256, m, m - 512)
    d1 = np.where(m <= 256, m - 256, m - 768)
    idx0 = np.array([_t5_bucket_static(int(d)) for d in d0], np.int32)
    idx1 = np.array([_t5_bucket_static(int(d)) for d in d1], np.int32)
    far = _t5_bucket_static(-512)
    return t5_bias[idx0].T, t5_bias[idx1].T, t5_bias[far]


def _band_table(rel_bias, tq):
    width = 2 * tq + D_LEFT
    m = np.arange(width)
    dist = np.where(m <= tq + D_LEFT, D_LEFT - m, D_LEFT + width - m)
    idx = np.clip(dist, -REL_CLIP, REL_CLIP) + REL_CLIP
    return rel_bias[idx.astype(np.int32)].T


def _rope_tables(seq):
    inv = ROPE_BASE ** (-jnp.arange(0, C_ROPE, 2, dtype=F32) / C_ROPE)
    ang = jnp.arange(seq, dtype=F32)[:, None] * inv[None, :]
    c, s = jnp.cos(ang), jnp.sin(ang)
    z = jnp.zeros_like(c)
    return jnp.concatenate([c, z, c, z], axis=1), jnp.concatenate([-s, z, s, z], axis=1)


def kernel(x, t5_bias, norm_g, w_in, a_v_gain, a_ws, a_bs, b_q_gain, b_k_gain, c_qa_gain, c_kva_gain,
           c_w_qb, c_w_kvb, c_q_gain, c_k_gain, d_q_gain, d_k_gain, d_rel_bias, w_out):
    bsz, seq, d_model = x.shape
    depth = w_in.shape[0]
    tq = 256
    x2 = x.reshape(bsz * seq, d_model)
    cos, sin = _rope_tables(seq)
    base0, base1, cfar = _t5_tables(t5_bias)
    w_in_blocks = _layout_w_in(w_in)
    for l in range(depth):
        h = _inproj(x2, norm_g[l][None, :], w_in_blocks, l)
        y_a = _mixer_a(h, a_v_gain[l][None, :], a_ws[l], a_bs[l][:, :, None])
        mask = _b_select(h, bsz, seq, tq)
        y_b = _b_attn(h, mask, cfar, jnp.tile(b_q_gain[l], N_HEADS64)[None, :],
                      jnp.tile(b_k_gain[l], 2)[None, :], base0, base1, bsz, seq, tq)
        wq, wkv, qg, kg, qag = _layout_c(c_w_qb[l], c_w_kvb[l], c_q_gain[l], c_k_gain[l], c_qa_gain[l])
        qc, kc, vc = _c_prep(h, wq, wkv, qag, c_kva_gain[l][None, :], qg, kg, cos, sin, seq)
        y_c = _c_attn(qc, kc, vc, h, bsz, seq, tq)
        y_d = _mixer_d(h, jnp.tile(d_q_gain[l], N_HEADS64)[None, :], jnp.tile(d_k_gain[l], N_HEADS64)[None, :],
                       _band_table(d_rel_bias[l], tq), bsz, seq, tq)
        x2 = _outproj(x2, (y_a, y_b, y_c, y_d), w_out[l].astype(BF16))
    return x2.reshape(bsz, seq, d_model)
```

```python
import functools
import math

import numpy as np
import jax
import jax.numpy as jnp
from jax import lax
from jax.experimental import pallas as pl
from jax.experimental.pallas import tpu as pltpu

F32 = jnp.float32
BF16 = jnp.bfloat16

EPS = 1e-6
NEG = -1e30
LOG2E = math.log2(math.e)
INT_MIN = -(2 ** 31)
CHUNK = 64
LANES = 128
GROUP_W = 512
A_GROUPS = 4
GMLP_BLOCK = 128
N_HEADS64 = 8
IDX_SCALE = (8 ** -0.5) * 0.125
TOPK_MAX = 256
T5_BUCKETS = 32
C_HEADS = 4
C_NOPE = 128
C_ROPE = 64
C_QK = 192
Q_LORA = 384
KV_LORA = 128
ROPE_BASE = 10000.0
D_LEFT = 8 * CHUNK
REL_CLIP = 128
VMEM_LIMIT = 56 * 1024 * 1024

BLK_A_U, BLK_A_V, BLK_A_Z = 0, 1, 2
BLK_B_Q, BLK_B_IQ, BLK_B_Z, BLK_SMALL = 3, 4, 5, 6
BLK_C_QKV, BLK_C_Z = 7, 8
BLK_D_Q, BLK_D_K, BLK_D_V, BLK_D_Z = 9, 10, 11, 12
H_COLS = 13 * GROUP_W
UNIT_B_K, UNIT_B_V, UNIT_B_IK, UNIT_KR_IW = (BLK_SMALL * 4 + i for i in range(4))
IW_LANE = 32

_SRC = dict(a_u=0, a_v=512, a_z=1024, b_q=1536, b_k=2048, b_v=2112, b_iq=2176, b_ik=2688,
            b_iw=2752, b_z=2760, c_q=3272, c_kv=3656, c_kr=3784, c_z=3848,
            d_q=4360, d_k=4872, d_v=5384, d_z=5896)


def _params(n_axes):
    return pltpu.CompilerParams(dimension_semantics=("arbitrary",) * n_axes,
                                vmem_limit_bytes=VMEM_LIMIT)


def _gelu(x):
    c = math.sqrt(2.0 / math.pi)
    return x * (0.5 * (1.0 + jnp.tanh(c * (x + 0.044715 * (x * x * x)))))


def _silu(x):
    return x * (1.0 / (1.0 + jnp.exp(-x)))


def _dot_t(a, b):
    return lax.dot_general(a, b, (((1,), (1,)), ((), ())), preferred_element_type=F32)


def _lo_mask(rows):
    return lax.broadcasted_iota(jnp.int32, (rows, LANES), 1) < 64


def _rms_heads64(x, gain, ntiles):
    lo = _lo_mask(x.shape[0])
    tiles = []
    for t in range(ntiles):
        xt = x[:, t * LANES:(t + 1) * LANES]
        sq = xt * xt
        s_lo = jnp.sum(jnp.where(lo, sq, 0.0), axis=-1, keepdims=True)
        s_hi = jnp.sum(jnp.where(lo, 0.0, sq), axis=-1, keepdims=True)
        r = jnp.where(lo, lax.rsqrt(s_lo * (1.0 / 64) + EPS), lax.rsqrt(s_hi * (1.0 / 64) + EPS))
        tiles.append(xt * r * gain[:, t * LANES:(t + 1) * LANES])
    return tiles


def _toeplitz(base_row, rows, width):
    t = jnp.broadcast_to(base_row, (rows, base_row.shape[1]))
    t = pltpu.roll(t, 0, 1, stride=1, stride_axis=0)
    return t[:, :width]


def _inproj_kernel(x_ref, g_ref, w_ref, o_ref):
    x = x_ref[...]
    ms = jnp.mean(x * x, axis=-1, keepdims=True)
    xn = (x * lax.rsqrt(ms + EPS) * g_ref[...]).astype(BF16)
    for c in range(H_COLS // GROUP_W):
        cols = slice(c * GROUP_W, (c + 1) * GROUP_W)
        o_ref[:, cols] = _dot_t(xn, w_ref[cols, :]).astype(BF16)


def _inproj(x2, g, w_all, layer, tm=512):
    n, d = x2.shape
    return pl.pallas_call(
        _inproj_kernel,
        grid=(n // tm,),
        in_specs=[pl.BlockSpec((tm, d), lambda i: (i, 0)),
                  pl.BlockSpec((1, d), lambda i: (0, 0)),
                  pl.BlockSpec((None, H_COLS, d), lambda i: (layer, 0, 0), pipeline_mode=pl.Buffered(1))],
        out_specs=pl.BlockSpec((tm, H_COLS), lambda i: (i, 0)),
        out_shape=jax.ShapeDtypeStruct((n, H_COLS), BF16),
        compiler_params=_params(1),
    )(x2, g, w_all)


def _outproj_kernel(x_ref, ya_ref, yb_ref, yc_ref, yd_ref, w_ref, o_ref):
    acc = x_ref[...]
    for g, y_ref in enumerate((ya_ref, yb_ref, yc_ref, yd_ref)):
        acc = acc + jnp.dot(y_ref[...], w_ref[g * GROUP_W:(g + 1) * GROUP_W, :],
                            preferred_element_type=F32)
    o_ref[...] = acc


def _outproj(x2, ys, w, tm=512):
    n, d = x2.shape
    yspec = pl.BlockSpec((tm, GROUP_W), lambda i: (i, 0))
    return pl.pallas_call(
        _outproj_kernel,
        grid=(n // tm,),
        in_specs=[pl.BlockSpec((tm, d), lambda i: (i, 0)), yspec, yspec, yspec, yspec,
                  pl.BlockSpec((4 * GROUP_W, d), lambda i: (0, 0))],
        out_specs=pl.BlockSpec((tm, d), lambda i: (i, 0)),
        out_shape=jax.ShapeDtypeStruct((n, d), F32),
        compiler_params=_params(1),
    )(x2, *ys, w)


def _mixer_a_kernel(u_ref, v_ref, z_ref, vg_ref, w_ref, b_ref, o_ref):
    tm = u_ref.shape[0]
    u = _gelu(u_ref[...].astype(F32))
    v = _gelu(v_ref[...].astype(F32))
    ms = jnp.mean(v * v, axis=-1, keepdims=True)
    vb = (v * lax.rsqrt(ms + EPS) * vg_ref[...]).astype(BF16)
    gate = _silu(z_ref[...].astype(F32))
    i = lax.broadcasted_iota(jnp.int32, (GMLP_BLOCK, GMLP_BLOCK), 0)
    j = lax.broadcasted_iota(jnp.int32, (GMLP_BLOCK, GMLP_BLOCK), 1)
    keep = (j // CHUNK) <= (i // CHUNK)
    for g in range(A_GROUPS):
        wg = jnp.where(keep, w_ref[g], 0.0).astype(BF16)
        cols = slice(g * LANES, (g + 1) * LANES)
        for blk in range(tm // GMLP_BLOCK):
            rows = slice(blk * GMLP_BLOCK, (blk + 1) * GMLP_BLOCK)
            sg = jnp.dot(wg, vb[rows, cols], preferred_element_type=F32) + b_ref[g]
            o_ref[rows, cols] = (u[rows, cols] * sg * gate[rows, cols]).astype(BF16)


def _mixer_a(h, vg, ws, bs, tm=512):
    n = h.shape[0]
    hspec = lambda blk: pl.BlockSpec((tm, GROUP_W), lambda i, blk=blk: (i, blk))
    return pl.pallas_call(
        _mixer_a_kernel,
        grid=(n // tm,),
        in_specs=[hspec(BLK_A_U), hspec(BLK_A_V), hspec(BLK_A_Z),
                  pl.BlockSpec((1, GROUP_W), lambda i: (0, 0)),
                  pl.BlockSpec((A_GROUPS, GMLP_BLOCK, GMLP_BLOCK), lambda i: (0, 0, 0)),
                  pl.BlockSpec((A_GROUPS, GMLP_BLOCK, 1), lambda i: (0, 0, 0))],
        out_specs=pl.BlockSpec((tm, GROUP_W), lambda i: (i, 0)),
        out_shape=jax.ShapeDtypeStruct((n, GROUP_W), BF16),
        compiler_params=_params(1),
    )(h, h, h, vg, ws, bs)


def _order_key(x):
    return jnp.where(x < 0, x ^ 0x7FFFFFFF, x)


def _b_select_kernel(iq_ref, iw_ref, ik_ref, o_ref, lhs_ref, key_ref, *, seq, tq, topk):
    t_blk = pl.program_id(1)
    nkb = seq // 256
    nblk = ((t_blk + 1) * tq) // 256
    n_interp = 12
    hrows = [slice(h * tq, (h + 1) * tq) for h in range(N_HEADS64)]

    lo_half = _lo_mask(tq)
    w_t = (iw_ref[...].astype(F32) * IDX_SCALE).T
    for h in range(N_HEADS64):
        iqt = iq_ref[:, (h // 2) * LANES:(h // 2 + 1) * LANES]
        sel = lo_half if h % 2 == 0 else jnp.logical_not(lo_half)
        lhs_ref[hrows[h], :] = jnp.where(sel, iqt, jnp.zeros_like(iqt))

    qpos = t_blk * tq + lax.broadcasted_iota(jnp.int32, (256, tq), 1)
    krow = lax.broadcasted_iota(jnp.int32, (256, tq), 0)

    def score_block(kb, amax, masked):
        off = pl.multiple_of(kb * 256, 256)
        ikblk = ik_ref[pl.ds(off, 256), :]
        score = jnp.zeros((256, tq), F32)
        for h in range(N_HEADS64):
            w_h = w_t[IW_LANE + h:IW_LANE + h + 1, :]
            score = score + w_h * jnp.maximum(_dot_t(ikblk, lhs_ref[hrows[h], :]), 0.0)
        score = jnp.where(score == 0.0, 0.0, score)
        key = _order_key(lax.bitcast_convert_type(score, jnp.int32))
        if masked:
            key = jnp.where(((kb * 256 + krow) // CHUNK) <= (qpos // CHUNK), key, INT_MIN)
        key_ref[kb] = key
        return jnp.maximum(amax, jnp.max(jnp.abs(score).reshape(256 // 8, 8, tq), axis=0))

    nfull = (t_blk * tq) // 256
    amax = lax.fori_loop(0, nfull, lambda kb, a: score_block(kb, a, False), jnp.zeros((8, tq), F32))
    amax = lax.fori_loop(nfull, nblk, lambda kb, a: score_block(kb, a, True), amax)
    amax = jnp.max(amax, axis=0, keepdims=True)

    def count(pred):
        def body(kb, acc):
            ones = jnp.where(pred(key_ref[kb], kb), 1.0, 0.0)
            return acc + jnp.sum(ones.reshape(256 // 8, 8, tq), axis=0)
        return jnp.sum(lax.fori_loop(0, nblk, body, jnp.zeros((8, tq), F32)), axis=0, keepdims=True)

    kf = float(topk)
    qrow = t_blk * tq + lax.broadcasted_iota(jnp.int32, (1, tq), 1)
    n_adm = ((qrow // CHUNK + 1) * CHUNK).astype(F32)
    one = jnp.ones((1, tq), jnp.int32)
    f_pos = count(lambda k, kb: k >= one)
    f_nn = count(lambda k, kb: k >= one - 1)
    pos = f_pos > kf
    neg = f_nn < kf
    lo0 = jnp.where(pos, one, _order_key(lax.bitcast_convert_type(-amax, jnp.int32)))
    hi0 = jnp.where(neg, one - 1, _order_key(lax.bitcast_convert_type(amax, jnp.int32)) + 1)
    w_lo0 = jnp.where(pos, f_pos, n_adm) - kf
    w_hi0 = kf - jnp.where(neg, f_nn, 0.0)
    all_sel = n_adm <= kf
    at_zero = jnp.logical_not(pos | neg)
    done0 = jnp.where(all_sel | at_zero | (hi0 == lo0 + 1), 1.0, 0.0)
    thr0 = jnp.where(all_sel, INT_MIN + 1, jnp.where(at_zero, jnp.where(f_pos == kf, one, one - 1), lo0))

    def search_cond(st):
        return jnp.logical_and(st[0] < n_interp + 32, st[1] < 0.5)

    def search_step(st):
        it, _, lo, hi, w_lo, w_hi, side, done, thr = st
        lo_v = lax.bitcast_convert_type(_order_key(lo), F32)
        hi_v = lax.bitcast_convert_type(_order_key(hi), F32)
        c_v = lo_v + (hi_v - lo_v) * (w_lo / (w_lo + w_hi))
        c_interp = _order_key(lax.bitcast_convert_type(c_v, jnp.int32))
        c_mid = (lo >> 1) + (hi >> 1) + (lo & hi & 1)
        cand = jnp.where(it < n_interp, c_interp, c_mid)
        cand = jnp.minimum(jnp.maximum(cand, lo + 1), hi - 1)
        f = count(lambda k, kb: k >= cand)
        live = done < 0.5
        up = f > kf
        hit = f == kf
        new_lo = jnp.where(live & up, cand, lo)
        new_hi = jnp.where(live & jnp.logical_not(up), cand, hi)
        new_w_lo = jnp.where(up, f - kf, jnp.where(side < 0.0, 0.5 * w_lo, w_lo))
        new_w_hi = jnp.where(up, jnp.where(side > 0.0, 0.5 * w_hi, w_hi), kf - f)
        new_side = jnp.where(up, 1.0, -1.0)
        new_thr = jnp.where(live, jnp.where(hit, cand, new_lo), thr)
        new_done = jnp.where(live & (hit | (new_hi == new_lo + 1)), 1.0, done)
        return (it + 1, jnp.min(new_done), new_lo, new_hi, jnp.where(live, new_w_lo, w_lo),
                jnp.where(live, new_w_hi, w_hi), jnp.where(live, new_side, side), new_done, new_thr)

    state = (jnp.int32(0), jnp.min(done0), lo0, hi0, w_lo0, w_hi0, jnp.zeros((1, tq), F32), done0, thr0)
    thr = lax.while_loop(search_cond, search_step, state)[-1]

    need = kf - count(lambda k, kb: k > thr)
    ties = count(lambda k, kb: k == thr)
    any_excess = jnp.max(ties - need) > 0.0

    def store_mask(kb, keep_t):
        for g in range(tq // 256):
            o_ref[g, kb] = keep_t[:, g * 256:(g + 1) * 256].T

    def write_unused(kb, carry):
        store_mask(kb, jnp.zeros((256, tq), F32))
        return carry

    lax.fori_loop(nblk, nkb, write_unused, 0)

    @pl.when(jnp.logical_not(any_excess))
    def _():
        def write(kb, carry):
            store_mask(kb, jnp.where(key_ref[kb] >= thr, 1.0, 0.0))
            return carry

        lax.fori_loop(0, nblk, write, 0)

    @pl.when(any_excess)
    def _():
        def idx_step(it, jmax):
            cand = jmax | lax.shift_left(jnp.int32(1), 10 - it)
            below = count(lambda k, kb: (k == thr) & ((kb * 256 + krow) < cand))
            return jnp.where(below < need, cand, jmax)

        jmax = lax.fori_loop(0, 11, idx_step, jnp.zeros((1, tq), jnp.int32))

        def write(kb, carry):
            k = key_ref[kb]
            keep_tie = (k == thr) & ((kb * 256 + krow) <= jmax)
            store_mask(kb, jnp.where((k > thr) | keep_tie, 1.0, 0.0))
            return carry

        lax.fori_loop(0, nblk, write, 0)


def _b_select(h, bsz, seq, tq=512):
    tq = min(tq, seq)
    nt = seq // tq
    topk = min(TOPK_MAX, seq // 4)
    kern = functools.partial(_b_select_kernel, seq=seq, tq=tq, topk=topk)
    return pl.pallas_call(
        kern,
        grid=(bsz, nt),
        in_specs=[pl.BlockSpec((tq, GROUP_W), lambda b, t: (b * nt + t, BLK_B_IQ)),
                  pl.BlockSpec((tq, LANES), lambda b, t: (b * nt + t, UNIT_KR_IW)),
                  pl.BlockSpec((seq, LANES), lambda b, t: (b, UNIT_B_IK))],
        out_specs=pl.BlockSpec((tq // 256, seq // 256, 256, 256), lambda b, t: (b * nt + t, 0, 0, 0)),
        out_shape=jax.ShapeDtypeStruct((bsz * seq // 256, seq // 256, 256, 256), F32),
        scratch_shapes=[pltpu.VMEM((N_HEADS64 * tq, LANES), BF16),
                        pltpu.VMEM((seq // 256, 256, tq), jnp.int32)],
        compiler_params=_params(2),
    )(h, h, h)


def _b_attn_kernel(cfar_ref, q_ref, z_ref, k_ref, v_ref, msk_ref, qg_ref, kg_ref, base0_ref, base1_ref,
                   o_ref, kn_ref, v1_ref, bias_ref, qall_ref, s_ref, mp_ref, acc_ref, *, seq, tq):
    b = pl.program_id(0)
    t_blk = pl.program_id(1)
    hrows = [slice(h * tq, (h + 1) * tq) for h in range(N_HEADS64)]

    @pl.when((b == 0) & (t_blk == 0))
    def _():
        for h in range(N_HEADS64):
            bias_ref[0, hrows[h], :] = jnp.full((tq, 256), cfar_ref[h] * LOG2E, F32)
            bias_ref[1, hrows[h], :] = _toeplitz(base1_ref[h:h + 1, :], tq, 256) * LOG2E
            bias_ref[2, hrows[h], :] = _toeplitz(base0_ref[h:h + 1, :], tq, 256) * LOG2E

    @pl.when(t_blk == 0)
    def _():
        lo256 = _lo_mask(256)
        for r in range(seq // 256):
            rows = slice(r * 256, (r + 1) * 256)
            k = k_ref[rows, :].astype(F32)
            ms = jnp.mean(k * k, axis=-1, keepdims=True)
            kn_ref[rows, :] = (k * lax.rsqrt(ms + EPS) * kg_ref[...]).astype(BF16)
            v = v_ref[rows, :]
            v1_ref[rows, :] = jnp.where(lo256, v, jnp.ones_like(v))

    lo = _lo_mask(tq)
    qtiles = _rms_heads64(q_ref[...].astype(F32), qg_ref[...], 4)
    for h in range(N_HEADS64):
        sel = lo if h % 2 == 0 else jnp.logical_not(lo)
        qall_ref[hrows[h], :] = jnp.where(sel, qtiles[h // 2] * (0.125 * LOG2E), 0.0).astype(BF16)
    mp_ref[...] = jnp.full(mp_ref.shape, NEG, F32)
    acc_ref[...] = jnp.zeros(acc_ref.shape, F32)
    nblk = t_blk + 1

    def logits_blocks(kb0, n):
        for i in range(n):
            kb = kb0 + i
            off = pl.multiple_of(kb * 256, 256)
            kblk = kn_ref[pl.ds(off, 256), :]
            which = jnp.clip(kb - (t_blk - 2), 0, 2)
            keep = msk_ref[0, kb] > 0.5
            for h in range(N_HEADS64):
                s = jnp.where(keep, _dot_t(qall_ref[hrows[h], :], kblk) + bias_ref[which, hrows[h], :], NEG)
                s_ref[kb, hrows[h], :] = s
                mp_ref[hrows[h], :] = jnp.maximum(mp_ref[hrows[h], :], jnp.maximum(s[:, :LANES], s[:, LANES:]))

    def value_blocks(kb0, n):
        off = pl.multiple_of(kb0 * 256, 256)
        v1 = v1_ref[pl.ds(off, n * 256), :]
        for h in range(N_HEADS64):
            m = mp_ref[hrows[h], :]
            mm = jnp.concatenate([m, m], axis=1)
            p = [jnp.exp2(s_ref[kb0 + i, hrows[h], :] - mm).astype(BF16) for i in range(n)]
            p = p[0] if n == 1 else jnp.concatenate(p, axis=1)
            acc_ref[hrows[h], :] += jnp.dot(p, v1, preferred_element_type=F32)

    def pairs_then_rest(blocks):
        def pair(i, carry):
            blocks(2 * i, 2)
            return carry

        lax.fori_loop(0, nblk // 2, pair, 0)

        @pl.when(nblk % 2 == 1)
        def _():
            blocks(nblk - 1, 1)

    pairs_then_rest(logits_blocks)
    for h in range(N_HEADS64):
        m = jnp.max(mp_ref[hrows[h], :], axis=-1, keepdims=True)
        mp_ref[hrows[h], :] = jnp.broadcast_to(m, (tq, LANES))
    pairs_then_rest(value_blocks)

    gate = _silu(z_ref[...].astype(F32))
    for t in range(4):
        a_even = acc_ref[hrows[2 * t], :]
        a_odd = acc_ref[hrows[2 * t + 1], :]
        o_even = a_even / pltpu.roll(a_even, 64, 1)
        o_odd = pltpu.roll(a_odd, 64, 1) / a_odd
        cols = slice(t * LANES, (t + 1) * LANES)
        o_ref[:, cols] = (jnp.where(lo, o_even, o_odd) * gate[:, cols]).astype(BF16)


def _b_attn(h, mask, cfar, qg, kg, base0, base1, bsz, seq, tq=256):
    nt = seq // tq
    rows = N_HEADS64 * tq
    kern = functools.partial(_b_attn_kernel, seq=seq, tq=tq)
    full = lambda shape: pl.BlockSpec(shape, lambda b, t: (0,) * len(shape))
    return pl.pallas_call(
        kern,
        grid=(bsz, nt),
        in_specs=[pl.BlockSpec(memory_space=pltpu.SMEM),
                  pl.BlockSpec((tq, GROUP_W), lambda b, t: (b * nt + t, BLK_B_Q)),
                  pl.BlockSpec((tq, GROUP_W), lambda b, t: (b * nt + t, BLK_B_Z)),
                  pl.BlockSpec((seq, LANES), lambda b, t: (b, UNIT_B_K)),
                  pl.BlockSpec((seq, LANES), lambda b, t: (b, UNIT_B_V)),
                  pl.BlockSpec((1, seq // 256, tq, 256), lambda b, t: (b * nt + t, 0, 0, 0)),
                  full((1, GROUP_W)), full((1, LANES)), full((N_HEADS64, 512)), full((N_HEADS64, 512))],
        out_specs=pl.BlockSpec((tq, GROUP_W), lambda b, t: (b * nt + t, 0)),
        out_shape=jax.ShapeDtypeStruct((bsz * seq, GROUP_W), BF16),
        scratch_shapes=[pltpu.VMEM((seq, LANES), BF16), pltpu.VMEM((seq, LANES), BF16),
                        pltpu.VMEM((3, rows, 256), F32), pltpu.VMEM((rows, LANES), BF16),
                        pltpu.VMEM((seq // 256, rows, 256), F32),
                        pltpu.VMEM((rows, LANES), F32), pltpu.VMEM((rows, LANES), F32)],
        compiler_params=_params(2),
    )(cfar, h, h, h, h, mask, qg, kg, base0, base1)


def _rope(tile, cos, sin):
    return tile * cos + pltpu.roll(tile, 64, 1) * sin


def _c_prep_kernel(lat_ref, kr_ref, wq_ref, wkv_ref, qag_ref, kvag_ref, qg_ref, kg_ref, cos_ref, sin_ref,
                   qo_ref, ko_ref, vo_ref):
    cq = lat_ref[:, :Q_LORA].astype(F32)
    ms = jnp.mean(cq * cq, axis=-1, keepdims=True)
    cqn = (cq * lax.rsqrt(ms + EPS) * qag_ref[...]).astype(BF16)
    qpre = jnp.dot(cqn, wq_ref[...], preferred_element_type=F32)
    ckv = lat_ref[:, Q_LORA:].astype(F32)
    ms = jnp.mean(ckv * ckv, axis=-1, keepdims=True)
    ckvn = (ckv * lax.rsqrt(ms + EPS) * kvag_ref[...]).astype(BF16)
    kvpre = jnp.dot(ckvn, wkv_ref[...], preferred_element_type=F32)
    lane = lax.broadcasted_iota(jnp.int32, kr_ref.shape, 1)
    kr = jnp.where((lane % 64) < 32, kr_ref[...].astype(F32), 0.0)
    kr_ss = jnp.sum(kr * kr, axis=-1, keepdims=True)
    cos = cos_ref[...]
    sin = sin_ref[...]
    qg = qg_ref[...]
    kg = kg_ref[...]
    for h in range(C_HEADS):
        qh = qpre[:, h * 256:(h + 1) * 256]
        r = lax.rsqrt(jnp.sum(qh * qh, axis=-1, keepdims=True) * (1.0 / C_QK) + EPS)
        qn = qh * r * qg
        qo_ref[:, h * 256:h * 256 + LANES] = qn[:, :LANES].astype(BF16)
        qo_ref[:, h * 256 + LANES:(h + 1) * 256] = _rope(qn[:, LANES:], cos, sin).astype(BF16)
        kn = kvpre[:, h * LANES:(h + 1) * LANES]
        r = lax.rsqrt((jnp.sum(kn * kn, axis=-1, keepdims=True) + kr_ss) * (1.0 / C_QK) + EPS)
        ko_ref[:, h * 256:h * 256 + LANES] = (kn * r * kg[:, :LANES]).astype(BF16)
        ko_ref[:, h * 256 + LANES:(h + 1) * 256] = _rope(kr * r * kg[:, LANES:], cos, sin).astype(BF16)
    vo_ref[...] = kvpre[:, C_HEADS * LANES:].astype(BF16)


def _c_prep(h, wq, wkv, qag, kvag, qg, kg, cos, sin, seq, tm=512):
    n = h.shape[0]
    ns = seq // tm
    full = lambda shape: pl.BlockSpec(shape, lambda i: (0,) * len(shape))
    return pl.pallas_call(
        _c_prep_kernel,
        grid=(n // tm,),
        in_specs=[pl.BlockSpec((tm, GROUP_W), lambda i: (i, BLK_C_QKV)),
                  pl.BlockSpec((tm, LANES), lambda i: (i, UNIT_KR_IW)),
                  full((Q_LORA, 4 * 256)), full((KV_LORA, 8 * LANES)),
                  full((1, Q_LORA)), full((1, LANES)), full((1, 256)), full((1, 256)),
                  pl.BlockSpec((tm, LANES), lambda i: (i % ns, 0)),
                  pl.BlockSpec((tm, LANES), lambda i: (i % ns, 0))],
        out_specs=[pl.BlockSpec((tm, 4 * 256), lambda i: (i, 0)),
                   pl.BlockSpec((tm, 4 * 256), lambda i: (i, 0)),
                   pl.BlockSpec((tm, GROUP_W), lambda i: (i, 0))],
        out_shape=[jax.ShapeDtypeStruct((n, 4 * 256), BF16), jax.ShapeDtypeStruct((n, 4 * 256), BF16),
                   jax.ShapeDtypeStruct((n, GROUP_W), BF16)],
        compiler_params=_params(1),
    )(h, h, wq, wkv, qag, kvag, qg, kg, cos, sin)


def _c_attn_kernel(q_ref, k_ref, v_ref, z_ref, o_ref, s_ref, mp_ref, lp_ref, acc_ref, *, tq):
    qt = pl.program_id(1)
    scale = C_QK ** -0.5 * LOG2E
    hrows = [slice(h * tq, (h + 1) * tq) for h in range(C_HEADS)]
    qchunk = (qt * tq + lax.broadcasted_iota(jnp.int32, (tq, 256), 0)) // CHUNK
    kcol = lax.broadcasted_iota(jnp.int32, (tq, 256), 1)
    mp_ref[...] = jnp.full(mp_ref.shape, NEG, F32)
    lp_ref[...] = jnp.zeros(lp_ref.shape, F32)
    acc_ref[...] = jnp.zeros(acc_ref.shape, F32)
    nfull = (qt * tq) // 256
    nblk = ((qt + 1) * tq) // 256

    def logits_blocks(kb0, n, masked):
        for i in range(n):
            kb = kb0 + i
            off = pl.multiple_of(kb * 256, 256)
            for h in range(C_HEADS):
                cols = slice(h * 256, (h + 1) * 256)
                s = _dot_t(q_ref[:, cols], k_ref[pl.ds(off, 256), cols]) * scale
                if masked:
                    s = jnp.where(((kb * 256 + kcol) // CHUNK) <= qchunk, s, NEG)
                s_ref[kb, hrows[h], :] = s
                mp_ref[hrows[h], :] = jnp.maximum(mp_ref[hrows[h], :], jnp.maximum(s[:, :LANES], s[:, LANES:]))

    def value_blocks(kb0, n):
        off = pl.multiple_of(kb0 * 256, 256)
        for h in range(C_HEADS):
            m = mp_ref[hrows[h], :]
            mm = jnp.concatenate([m, m], axis=1)
            p = [jnp.exp2(s_ref[kb0 + i, hrows[h], :] - mm) for i in range(n)]
            lsum = p[0][:, :LANES] + p[0][:, LANES:]
            for pi in p[1:]:
                lsum = lsum + pi[:, :LANES] + pi[:, LANES:]
            lp_ref[hrows[h], :] += lsum
            pb = p[0].astype(BF16) if n == 1 else jnp.concatenate([pi.astype(BF16) for pi in p], axis=1)
            acc_ref[hrows[h], :] += jnp.dot(pb, v_ref[pl.ds(off, n * 256), h * LANES:(h + 1) * LANES],
                                            preferred_element_type=F32)

    def pairs_then_rest(blocks, count):
        def pair(i, carry):
            blocks(2 * i, 2)
            return carry

        lax.fori_loop(0, count // 2, pair, 0)

        @pl.when(count % 2 == 1)
        def _():
            blocks(count - 1, 1)

    pairs_then_rest(lambda kb0, n: logits_blocks(kb0, n, False), nfull)
    lax.fori_loop(nfull, nblk, lambda kb, c: (logits_blocks(kb, 1, True), c)[1], 0)
    for h in range(C_HEADS):
        m = jnp.max(mp_ref[hrows[h], :], axis=-1, keepdims=True)
        mp_ref[hrows[h], :] = jnp.broadcast_to(m, (tq, LANES))
    pairs_then_rest(value_blocks, nblk)
    gate = _silu(z_ref[...].astype(F32))
    for h in range(C_HEADS):
        cols = slice(h * LANES, (h + 1) * LANES)
        l = jnp.sum(lp_ref[hrows[h], :], axis=-1, keepdims=True)
        o_ref[:, cols] = (acc_ref[hrows[h], :] / l * gate[:, cols]).astype(BF16)


def _c_attn(qc, kc, vc, h, bsz, seq, tq=256):
    nt = seq // tq
    rows = C_HEADS * tq
    kern = functools.partial(_c_attn_kernel, tq=tq)
    return pl.pallas_call(
        kern,
        grid=(bsz, nt),
        in_specs=[pl.BlockSpec((tq, C_HEADS * 256), lambda b, t: (b * nt + t, 0)),
                  pl.BlockSpec((seq, C_HEADS * 256), lambda b, t: (b, 0)),
                  pl.BlockSpec((seq, GROUP_W), lambda b, t: (b, 0)),
                  pl.BlockSpec((tq, GROUP_W), lambda b, t: (b * nt + t, BLK_C_Z))],
        out_specs=pl.BlockSpec((tq, GROUP_W), lambda b, t: (b * nt + t, 0)),
        out_shape=jax.ShapeDtypeStruct((bsz * seq, GROUP_W), BF16),
        scratch_shapes=[pltpu.VMEM((seq // 256, rows, 256), F32), pltpu.VMEM((rows, LANES), F32),
                        pltpu.VMEM((rows, LANES), F32), pltpu.VMEM((rows, LANES), F32)],
        compiler_params=_params(2),
    )(qc, kc, vc, h)


def _mixer_d_kernel(q_ref, k_ref, v_ref, z_ref, qg_ref, kg_ref, base_ref, o_ref,
                    kpad_ref, vpad_ref, bias_ref, s_ref, mp_ref, *, seq, tq):
    b = pl.program_id(0)
    qt = pl.program_id(1)
    win = tq + D_LEFT

    @pl.when((b == 0) & (qt == 0))
    def _():
        for h in range(N_HEADS64):
            bias_ref[h] = _toeplitz(base_ref[h:h + 1, :], tq, win) * LOG2E

    @pl.when(qt == 0)
    def _():
        kpad_ref[0:D_LEFT, :] = jnp.zeros((D_LEFT, GROUP_W), BF16)
        vpad_ref[0:D_LEFT, :] = jnp.zeros((D_LEFT, GROUP_W), BF16)
        for r in range(seq // 256):
            rows = slice(r * 256, (r + 1) * 256)
            dst = slice(D_LEFT + r * 256, D_LEFT + (r + 1) * 256)
            tiles = _rms_heads64(k_ref[rows, :].astype(F32), kg_ref[...], 4)
            for t in range(4):
                kpad_ref[dst, t * LANES:(t + 1) * LANES] = tiles[t].astype(BF16)
            vpad_ref[dst, :] = v_ref[rows, :]

    lo = _lo_mask(tq)
    qtiles = _rms_heads64(q_ref[...].astype(F32), qg_ref[...], 4)
    start = pl.multiple_of(qt * tq, tq)
    qc = lax.broadcasted_iota(jnp.int32, (tq, win), 0) // CHUNK
    j = lax.broadcasted_iota(jnp.int32, (tq, win), 1)
    kc = j // CHUNK
    valid = (kc >= qc) & (kc <= qc + D_LEFT // CHUNK) & (j + start >= D_LEFT)
    gate = _silu(z_ref[...].astype(F32))
    ntile = win // LANES

    def lane_tiles(x):
        return [x[:, i * LANES:(i + 1) * LANES] for i in range(ntile)]

    for t in range(4):
        kwin = kpad_ref[pl.ds(start, win), t * LANES:(t + 1) * LANES]
        for half in range(2):
            h = 2 * t + half
            sel = lo if half == 0 else jnp.logical_not(lo)
            qh = jnp.where(sel, qtiles[t] * (0.125 * LOG2E), 0.0).astype(BF16)
            s = jnp.where(valid, _dot_t(qh, kwin) + bias_ref[h], NEG)
            s_ref[h] = s
            mp_ref[h] = functools.reduce(jnp.maximum, lane_tiles(s))
    for h in range(N_HEADS64):
        mp_ref[h] = jnp.broadcast_to(jnp.max(mp_ref[h], axis=-1, keepdims=True), (tq, LANES))
    for t in range(4):
        cols = slice(t * LANES, (t + 1) * LANES)
        vwin = vpad_ref[pl.ds(start, win), cols]
        outs = []
        for half in range(2):
            h = 2 * t + half
            m = mp_ref[h]
            p = jnp.exp2(s_ref[h] - jnp.concatenate([m] * ntile, axis=1))
            l = jnp.sum(functools.reduce(jnp.add, lane_tiles(p)), axis=-1, keepdims=True)
            outs.append(jnp.dot(p.astype(BF16), vwin, preferred_element_type=F32) / l)
        o_ref[:, cols] = (jnp.where(lo, outs[0], outs[1]) * gate[:, cols]).astype(BF16)


def _mixer_d(h, qg, kg, base, bsz, seq, tq=256):
    nt = seq // tq
    kern = functools.partial(_mixer_d_kernel, seq=seq, tq=tq)
    full = lambda shape: pl.BlockSpec(shape, lambda b, t: (0,) * len(shape))
    return pl.pallas_call(
        kern,
        grid=(bsz, nt),
        in_specs=[pl.BlockSpec((tq, GROUP_W), lambda b, t: (b * nt + t, BLK_D_Q)),
                  pl.BlockSpec((seq, GROUP_W), lambda b, t: (b, BLK_D_K)),
                  pl.BlockSpec((seq, GROUP_W), lambda b, t: (b, BLK_D_V)),
                  pl.BlockSpec((tq, GROUP_W), lambda b, t: (b * nt + t, BLK_D_Z)),
                  full((1, GROUP_W)), full((1, GROUP_W)), full((N_HEADS64, 2 * tq + D_LEFT))],
        out_specs=pl.BlockSpec((tq, GROUP_W), lambda b, t: (b * nt + t, 0)),
        out_shape=jax.ShapeDtypeStruct((bsz * seq, GROUP_W), BF16),
        scratch_shapes=[pltpu.VMEM((seq + D_LEFT, GROUP_W), BF16), pltpu.VMEM((seq + D_LEFT, GROUP_W), BF16),
                        pltpu.VMEM((N_HEADS64, tq, tq + D_LEFT), F32),
                        pltpu.VMEM((N_HEADS64, tq, tq + D_LEFT), F32), pltpu.VMEM((N_HEADS64, tq, LANES), F32)],
        compiler_params=_params(2),
    )(h, h, h, h, qg, kg, base)


def _w_in_pieces(take, zeros):
    c = lambda name, size, off=0: take(_SRC[name] + off, size)
    return [c("a_u", 512), c("a_v", 512), c("a_z", 512),
            c("b_q", 512), c("b_iq", 512), c("b_z", 512),
            c("b_k", 64), c("b_k", 64), c("b_v", 64), c("b_v", 64), c("b_ik", 64), c("b_ik", 64),
            c("c_kr", 32), c("b_iw", 8), zeros(24), c("c_kr", 32, 32), zeros(32),
            c("c_q", 384), c("c_kv", 128), c("c_z", 512),
            c("d_q", 512), c("d_k", 512), c("d_v", 512), c("d_z", 512)]


def _layout_w_in_kernel(w_ref, o_ref):
    tk = w_ref.shape[2]
    pieces = _w_in_pieces(lambda s, n: w_ref[0, s:s + n, :], lambda n: jnp.zeros((n, tk), F32))
    ends = np.cumsum([0] + [p.shape[0] for p in pieces])
    start = 0
    for i in range(1, len(pieces) + 1):
        if ends[i] % GROUP_W == 0:
            group = pieces[start:i]
            blk = group[0] if len(group) == 1 else jnp.concatenate(group, axis=0)
            o_ref[0, ends[start]:ends[i], :] = blk.astype(BF16)
            start = i


def _layout_w_in(w_in, tk=256):
    w_t = jnp.swapaxes(w_in, 1, 2)
    depth, cols, d = w_t.shape
    return pl.pallas_call(
        _layout_w_in_kernel,
        grid=(depth, d // tk),
        in_specs=[pl.BlockSpec((1, cols, tk), lambda l, i: (l, 0, i))],
        out_specs=pl.BlockSpec((1, H_COLS, tk), lambda l, i: (l, 0, i)),
        out_shape=jax.ShapeDtypeStruct((depth, H_COLS, d), BF16),
        compiler_params=_params(2),
    )(w_t)


def _rope_layout(v):
    z = jnp.zeros(v.shape[:-1] + (32,), v.dtype)
    return jnp.concatenate([v[..., :32], z, v[..., 32:], z], axis=-1)


def _layout_c(w_qb, w_kvb, q_gain, k_gain, qa_gain):
    wq = w_qb.reshape(Q_LORA, C_HEADS, C_QK)
    wq = jnp.concatenate([wq[..., :C_NOPE], _rope_layout(wq[..., C_NOPE:])], axis=-1)
    wq = wq.reshape(Q_LORA, C_HEADS * 256).astype(BF16)
    wkv = w_kvb.reshape(KV_LORA, C_HEADS, 2 * LANES)
    wkv = jnp.concatenate([wkv[..., :C_NOPE].reshape(KV_LORA, -1), wkv[..., C_NOPE:].reshape(KV_LORA, -1)],
                          axis=1).astype(BF16)
    lay = lambda g: jnp.concatenate([g[:C_NOPE], _rope_layout(g[C_NOPE:])])[None, :]
    return wq, wkv, lay(q_gain), lay(k_gain), qa_gain[None, :]


def _t5_bucket_static(rel):
    half = T5_BUCKETS // 2
    exact = half // 2
    n = abs(rel)
    if n < exact:
        val = n
    else:
        val = min(exact + (n * n // (exact * exact)).bit_length() - 1, half - 1)
    return (half if rel > 0 else 0) + val


def _t5_tables(t5_bias):
    m = np.arange(512)
    d0 = np.where(m < 256, m, m - 512)
    d1 = np.where(m <= 256, m - 256, m - 768)
    idx0 = np.array([_t5_bucket_static(int(d)) for d in d0], np.int32)
    idx1 = np.array([_t5_bucket_static(int(d)) for d in d1], np.int32)
    far = _t5_bucket_static(-512)
    return t5_bias[idx0].T, t5_bias[idx1].T, t5_bias[far]


def _band_table(rel_bias, tq):
    width = 2 * tq + D_LEFT
    m = np.arange(width)
    dist = np.where(m <= tq + D_LEFT, D_LEFT - m, D_LEFT + width - m)
    idx = np.clip(dist, -REL_CLIP, REL_CLIP) + REL_CLIP
    return rel_bias[idx.astype(np.int32)].T


def _rope_tables(seq):
    inv = ROPE_BASE ** (-jnp.arange(0, C_ROPE, 2, dtype=F32) / C_ROPE)
    ang = jnp.arange(seq, dtype=F32)[:, None] * inv[None, :]
    c, s = jnp.cos(ang), jnp.sin(ang)
    z = jnp.zeros_like(c)
    return jnp.concatenate([c, z, c, z], axis=1), jnp.concatenate([-s, z, s, z], axis=1)


def kernel(x, t5_bias, norm_g, w_in, a_v_gain, a_ws, a_bs, b_q_gain, b_k_gain, c_qa_gain, c_kva_gain,
           c_w_qb, c_w_kvb, c_q_gain, c_k_gain, d_q_gain, d_k_gain, d_rel_bias, w_out):
    bsz, seq, d_model = x.shape
    depth = w_in.shape[0]
    tq = 256
    x2 = x.reshape(bsz * seq, d_model)
    cos, sin = _rope_tables(seq)
    base0, base1, cfar = _t5_tables(t5_bias)
    w_in_blocks = _layout_w_in(w_in)
    for l in range(depth):
        h = _inproj(x2, norm_g[l][None, :], w_in_blocks, l)
        y_a = _mixer_a(h, a_v_gain[l][None, :], a_ws[l], a_bs[l][:, :, None])
        mask = _b_select(h, bsz, seq)
        y_b = _b_attn(h, mask, cfar, jnp.tile(b_q_gain[l], N_HEADS64)[None, :],
                      jnp.tile(b_k_gain[l], 2)[None, :], base0, base1, bsz, seq, tq)
        wq, wkv, qg, kg, qag = _layout_c(c_w_qb[l], c_w_kvb[l], c_q_gain[l], c_k_gain[l], c_qa_gain[l])
        qc, kc, vc = _c_prep(h, wq, wkv, qag, c_kva_gain[l][None, :], qg, kg, cos, sin, seq)
        y_c = _c_attn(qc, kc, vc, h, bsz, seq, tq)
        y_d = _mixer_d(h, jnp.tile(d_q_gain[l], N_HEADS64)[None, :], jnp.tile(d_k_gain[l], N_HEADS64)[None, :],
                       _band_table(d_rel_bias[l], tq), bsz, seq, tq)
        x2 = _outproj(x2, (y_a, y_b, y_c, y_d), w_out[l].astype(BF16))
    return x2.reshape(bsz, seq, d_model)
```

```python
import functools
import math

import numpy as np
import jax
import jax.numpy as jnp
from jax import lax
from jax.experimental import pallas as pl
from jax.experimental.pallas import tpu as pltpu

F32 = jnp.float32
BF16 = jnp.bfloat16

EPS = 1e-6
NEG = -1e30
LOG2E = math.log2(math.e)
INT_MIN = -(2 ** 31)
CHUNK = 64
LANES = 128
GROUP_W = 512
A_GROUPS = 4
GMLP_BLOCK = 128
N_HEADS64 = 8
IDX_SCALE = (8 ** -0.5) * 0.125
TOPK_MAX = 256
T5_BUCKETS = 32
C_HEADS = 4
C_NOPE = 128
C_ROPE = 64
C_QK = 192
Q_LORA = 384
KV_LORA = 128
ROPE_BASE = 10000.0
D_LEFT = 8 * CHUNK
REL_CLIP = 128
VMEM_LIMIT = 56 * 1024 * 1024

BLK_A_U, BLK_A_V, BLK_A_Z = 0, 1, 2
BLK_B_Q, BLK_B_IQ, BLK_B_Z, BLK_SMALL = 3, 4, 5, 6
BLK_C_QKV, BLK_C_Z = 7, 8
BLK_D_Q, BLK_D_K, BLK_D_V, BLK_D_Z = 9, 10, 11, 12
H_COLS = 13 * GROUP_W
UNIT_B_K, UNIT_B_V, UNIT_B_IK, UNIT_KR_IW = (BLK_SMALL * 4 + i for i in range(4))
IW_LANE = 32

_SRC = dict(a_u=0, a_v=512, a_z=1024, b_q=1536, b_k=2048, b_v=2112, b_iq=2176, b_ik=2688,
            b_iw=2752, b_z=2760, c_q=3272, c_kv=3656, c_kr=3784, c_z=3848,
            d_q=4360, d_k=4872, d_v=5384, d_z=5896)


def _params(n_axes):
    return pltpu.CompilerParams(dimension_semantics=("arbitrary",) * n_axes,
                                vmem_limit_bytes=VMEM_LIMIT)


def _gelu(x):
    c = math.sqrt(2.0 / math.pi)
    return x * (0.5 * (1.0 + jnp.tanh(c * (x + 0.044715 * (x * x * x)))))


def _silu(x):
    return x * (1.0 / (1.0 + jnp.exp(-x)))


def _dot_t(a, b):
    return lax.dot_general(a, b, (((1,), (1,)), ((), ())), preferred_element_type=F32)


def _lo_mask(rows):
    return lax.broadcasted_iota(jnp.int32, (rows, LANES), 1) < 64


def _rms_heads64(x, gain, ntiles):
    lo = _lo_mask(x.shape[0])
    tiles = []
    for t in range(ntiles):
        xt = x[:, t * LANES:(t + 1) * LANES]
        sq = xt * xt
        s_lo = jnp.sum(jnp.where(lo, sq, 0.0), axis=-1, keepdims=True)
        s_hi = jnp.sum(jnp.where(lo, 0.0, sq), axis=-1, keepdims=True)
        r = jnp.where(lo, lax.rsqrt(s_lo * (1.0 / 64) + EPS), lax.rsqrt(s_hi * (1.0 / 64) + EPS))
        tiles.append(xt * r * gain[:, t * LANES:(t + 1) * LANES])
    return tiles


def _toeplitz(base_row, rows, width):
    t = jnp.broadcast_to(base_row, (rows, base_row.shape[1]))
    t = pltpu.roll(t, 0, 1, stride=1, stride_axis=0)
    return t[:, :width]


def _inproj_kernel(x_ref, g_ref, w_ref, o_ref):
    x = x_ref[...]
    ms = jnp.mean(x * x, axis=-1, keepdims=True)
    xn = (x * lax.rsqrt(ms + EPS) * g_ref[...]).astype(BF16)
    for c in range(H_COLS // GROUP_W):
        cols = slice(c * GROUP_W, (c + 1) * GROUP_W)
        o_ref[:, cols] = _dot_t(xn, w_ref[cols, :]).astype(BF16)


def _inproj(x2, g, w_all, layer, tm=512):
    n, d = x2.shape
    return pl.pallas_call(
        _inproj_kernel,
        grid=(n // tm,),
        in_specs=[pl.BlockSpec((tm, d), lambda i: (i, 0)),
                  pl.BlockSpec((1, d), lambda i: (0, 0)),
                  pl.BlockSpec((None, H_COLS, d), lambda i: (layer, 0, 0), pipeline_mode=pl.Buffered(1))],
        out_specs=pl.BlockSpec((tm, H_COLS), lambda i: (i, 0)),
        out_shape=jax.ShapeDtypeStruct((n, H_COLS), BF16),
        compiler_params=_params(1),
    )(x2, g, w_all)


def _outproj_kernel(x_ref, ya_ref, yb_ref, yc_ref, yd_ref, w_ref, o_ref):
    acc = x_ref[...]
    for g, y_ref in enumerate((ya_ref, yb_ref, yc_ref, yd_ref)):
        acc = acc + jnp.dot(y_ref[...], w_ref[g * GROUP_W:(g + 1) * GROUP_W, :],
                            preferred_element_type=F32)
    o_ref[...] = acc


def _outproj(x2, ys, w, tm=512):
    n, d = x2.shape
    yspec = pl.BlockSpec((tm, GROUP_W), lambda i: (i, 0))
    return pl.pallas_call(
        _outproj_kernel,
        grid=(n // tm,),
        in_specs=[pl.BlockSpec((tm, d), lambda i: (i, 0)), yspec, yspec, yspec, yspec,
                  pl.BlockSpec((4 * GROUP_W, d), lambda i: (0, 0))],
        out_specs=pl.BlockSpec((tm, d), lambda i: (i, 0)),
        out_shape=jax.ShapeDtypeStruct((n, d), F32),
        compiler_params=_params(1),
    )(x2, *ys, w)


def _mixer_a_kernel(u_ref, v_ref, z_ref, vg_ref, w_ref, b_ref, o_ref):
    tm = u_ref.shape[0]
    u = _gelu(u_ref[...].astype(F32))
    v = _gelu(v_ref[...].astype(F32))
    ms = jnp.mean(v * v, axis=-1, keepdims=True)
    vb = (v * lax.rsqrt(ms + EPS) * vg_ref[...]).astype(BF16)
    gate = _silu(z_ref[...].astype(F32))
    i = lax.broadcasted_iota(jnp.int32, (GMLP_BLOCK, GMLP_BLOCK), 0)
    j = lax.broadcasted_iota(jnp.int32, (GMLP_BLOCK, GMLP_BLOCK), 1)
    keep = (j // CHUNK) <= (i // CHUNK)
    for g in range(A_GROUPS):
        wg = jnp.where(keep, w_ref[g], 0.0).astype(BF16)
        cols = slice(g * LANES, (g + 1) * LANES)
        for blk in range(tm // GMLP_BLOCK):
            rows = slice(blk * GMLP_BLOCK, (blk + 1) * GMLP_BLOCK)
            sg = jnp.dot(wg, vb[rows, cols], preferred_element_type=F32) + b_ref[g]
            o_ref[rows, cols] = (u[rows, cols] * sg * gate[rows, cols]).astype(BF16)


def _mixer_a(h, vg, ws, bs, tm=512):
    n = h.shape[0]
    hspec = lambda blk: pl.BlockSpec((tm, GROUP_W), lambda i, blk=blk: (i, blk))
    return pl.pallas_call(
        _mixer_a_kernel,
        grid=(n // tm,),
        in_specs=[hspec(BLK_A_U), hspec(BLK_A_V), hspec(BLK_A_Z),
                  pl.BlockSpec((1, GROUP_W), lambda i: (0, 0)),
                  pl.BlockSpec((A_GROUPS, GMLP_BLOCK, GMLP_BLOCK), lambda i: (0, 0, 0)),
                  pl.BlockSpec((A_GROUPS, GMLP_BLOCK, 1), lambda i: (0, 0, 0))],
        out_specs=pl.BlockSpec((tm, GROUP_W), lambda i: (i, 0)),
        out_shape=jax.ShapeDtypeStruct((n, GROUP_W), BF16),
        compiler_params=_params(1),
    )(h, h, h, vg, ws, bs)


def _order_key(x):
    return jnp.where(x < 0, x ^ 0x7FFFFFFF, x)


def _b_select_kernel(iq_ref, iw_ref, ik_ref, o_ref, lhs_ref, key_ref, *, seq, tq, topk):
    t_blk = pl.program_id(1)
    nkb = seq // 256
    nblk = ((t_blk + 1) * tq) // 256
    n_interp = 12
    hrows = [slice(h * tq, (h + 1) * tq) for h in range(N_HEADS64)]

    lo_half = _lo_mask(tq)
    w_t = (iw_ref[...].astype(F32) * IDX_SCALE).T
    for h in range(N_HEADS64):
        iqt = iq_ref[:, (h // 2) * LANES:(h // 2 + 1) * LANES]
        sel = lo_half if h % 2 == 0 else jnp.logical_not(lo_half)
        lhs_ref[hrows[h], :] = jnp.where(sel, iqt, jnp.zeros_like(iqt))

    qpos = t_blk * tq + lax.broadcasted_iota(jnp.int32, (256, tq), 1)
    krow = lax.broadcasted_iota(jnp.int32, (256, tq), 0)

    def fold8(x):
        return jnp.sum(x.reshape(256 // 8, 8, tq), axis=0)

    def score_block(kb, carry, masked):
        amax, n_pos, n_nn = carry
        off = pl.multiple_of(kb * 256, 256)
        ikblk = ik_ref[pl.ds(off, 256), :]
        score = jnp.zeros((256, tq), F32)
        for h in range(N_HEADS64):
            w_h = w_t[IW_LANE + h:IW_LANE + h + 1, :]
            score = score + w_h * jnp.maximum(_dot_t(ikblk, lhs_ref[hrows[h], :]), 0.0)
        score = jnp.where(score == 0.0, 0.0, score)
        key = _order_key(lax.bitcast_convert_type(score, jnp.int32))
        if masked:
            key = jnp.where(((kb * 256 + krow) // CHUNK) <= (qpos // CHUNK), key, INT_MIN)
        key_ref[kb] = key
        return (jnp.maximum(amax, jnp.max(jnp.abs(score).reshape(256 // 8, 8, tq), axis=0)),
                n_pos + fold8(jnp.where(key >= 1, 1.0, 0.0)), n_nn + fold8(jnp.where(key >= 0, 1.0, 0.0)))

    nfull = (t_blk * tq) // 256
    zeros8 = jnp.zeros((8, tq), F32)
    carry = lax.fori_loop(0, nfull, lambda kb, c: score_block(kb, c, False), (zeros8, zeros8, zeros8))
    carry = lax.fori_loop(nfull, nblk, lambda kb, c: score_block(kb, c, True), carry)
    amax = jnp.max(carry[0], axis=0, keepdims=True)
    f_pos = jnp.sum(carry[1], axis=0, keepdims=True)
    f_nn = jnp.sum(carry[2], axis=0, keepdims=True)

    def count(pred):
        def body(kb, acc):
            return acc + fold8(jnp.where(pred(key_ref[kb], kb), 1.0, 0.0))
        return jnp.sum(lax.fori_loop(0, nblk, body, zeros8), axis=0, keepdims=True)

    kf = float(topk)
    qrow = t_blk * tq + lax.broadcasted_iota(jnp.int32, (1, tq), 1)
    n_adm = ((qrow // CHUNK + 1) * CHUNK).astype(F32)
    one = jnp.ones((1, tq), jnp.int32)
    pos = f_pos > kf
    neg = f_nn < kf
    lo0 = jnp.where(pos, one, _order_key(lax.bitcast_convert_type(-amax, jnp.int32)))
    hi0 = jnp.where(neg, one - 1, _order_key(lax.bitcast_convert_type(amax, jnp.int32)) + 1)
    w_lo0 = jnp.where(pos, f_pos, n_adm) - kf
    w_hi0 = kf - jnp.where(neg, f_nn, 0.0)
    all_sel = n_adm <= kf
    at_zero = jnp.logical_not(pos | neg)
    done0 = jnp.where(all_sel | at_zero | (hi0 == lo0 + 1), 1.0, 0.0)
    thr0 = jnp.where(all_sel, INT_MIN + 1, jnp.where(at_zero, jnp.where(f_pos == kf, one, one - 1), lo0))

    def search_cond(st):
        return jnp.logical_and(st[0] < n_interp + 32, st[1] < 0.5)

    def search_step(st):
        it, _, lo, hi, w_lo, w_hi, side, done, thr = st
        lo_v = lax.bitcast_convert_type(_order_key(lo), F32)
        hi_v = lax.bitcast_convert_type(_order_key(hi), F32)
        c_v = lo_v + (hi_v - lo_v) * (w_lo / (w_lo + w_hi))
        c_interp = _order_key(lax.bitcast_convert_type(c_v, jnp.int32))
        c_mid = (lo >> 1) + (hi >> 1) + (lo & hi & 1)
        cand = jnp.where(it < n_interp, c_interp, c_mid)
        cand = jnp.minimum(jnp.maximum(cand, lo + 1), hi - 1)
        f = count(lambda k, kb: k >= cand)
        live = done < 0.5
        up = f > kf
        hit = f == kf
        new_lo = jnp.where(live & up, cand, lo)
        new_hi = jnp.where(live & jnp.logical_not(up), cand, hi)
        new_w_lo = jnp.where(up, f - kf, jnp.where(side < 0.0, 0.5 * w_lo, w_lo))
        new_w_hi = jnp.where(up, jnp.where(side > 0.0, 0.5 * w_hi, w_hi), kf - f)
        new_side = jnp.where(up, 1.0, -1.0)
        new_thr = jnp.where(live, jnp.where(hit, cand, new_lo), thr)
        new_done = jnp.where(live & (hit | (new_hi == new_lo + 1)), 1.0, done)
        return (it + 1, jnp.min(new_done), new_lo, new_hi, jnp.where(live, new_w_lo, w_lo),
                jnp.where(live, new_w_hi, w_hi), jnp.where(live, new_side, side), new_done, new_thr)

    state = (jnp.int32(0), jnp.min(done0), lo0, hi0, w_lo0, w_hi0, jnp.zeros((1, tq), F32), done0, thr0)
    thr = lax.while_loop(search_cond, search_step, state)[-1]

    any_excess = jnp.max(count(lambda k, kb: k >= thr)) > kf

    def store_mask(kb, keep_t):
        for g in range(tq // 256):
            o_ref[g, kb] = keep_t[:, g * 256:(g + 1) * 256].T

    def write_unused(kb, carry):
        store_mask(kb, jnp.zeros((256, tq), F32))
        return carry

    lax.fori_loop(nblk, nkb, write_unused, 0)

    @pl.when(jnp.logical_not(any_excess))
    def _():
        def write(kb, carry):
            store_mask(kb, jnp.where(key_ref[kb] >= thr, 1.0, 0.0))
            return carry

        lax.fori_loop(0, nblk, write, 0)

    @pl.when(any_excess)
    def _():
        need = kf - count(lambda k, kb: k > thr)

        def idx_step(it, jmax):
            cand = jmax | lax.shift_left(jnp.int32(1), 10 - it)
            below = count(lambda k, kb: (k == thr) & ((kb * 256 + krow) < cand))
            return jnp.where(below < need, cand, jmax)

        jmax = lax.fori_loop(0, 11, idx_step, jnp.zeros((1, tq), jnp.int32))

        def write(kb, carry):
            k = key_ref[kb]
            keep_tie = (k == thr) & ((kb * 256 + krow) <= jmax)
            store_mask(kb, jnp.where((k > thr) | keep_tie, 1.0, 0.0))
            return carry

        lax.fori_loop(0, nblk, write, 0)


def _b_select(h, bsz, seq, tq=512):
    tq = min(tq, seq)
    nt = seq // tq
    topk = min(TOPK_MAX, seq // 4)
    kern = functools.partial(_b_select_kernel, seq=seq, tq=tq, topk=topk)
    return pl.pallas_call(
        kern,
        grid=(bsz, nt),
        in_specs=[pl.BlockSpec((tq, GROUP_W), lambda b, t: (b * nt + t, BLK_B_IQ)),
                  pl.BlockSpec((tq, LANES), lambda b, t: (b * nt + t, UNIT_KR_IW)),
                  pl.BlockSpec((seq, LANES), lambda b, t: (b, UNIT_B_IK))],
        out_specs=pl.BlockSpec((tq // 256, seq // 256, 256, 256), lambda b, t: (b * nt + t, 0, 0, 0)),
        out_shape=jax.ShapeDtypeStruct((bsz * seq // 256, seq // 256, 256, 256), F32),
        scratch_shapes=[pltpu.VMEM((N_HEADS64 * tq, LANES), BF16),
                        pltpu.VMEM((seq // 256, 256, tq), jnp.int32)],
        compiler_params=_params(2),
    )(h, h, h)


def _b_attn_kernel(cfar_ref, q_ref, z_ref, k_ref, v_ref, msk_ref, qg_ref, kg_ref, base0_ref, base1_ref,
                   o_ref, kn_ref, v1_ref, bias_ref, qall_ref, s_ref, mp_ref, acc_ref, *, seq, tq):
    b = pl.program_id(0)
    t_blk = pl.program_id(1)
    hrows = [slice(h * tq, (h + 1) * tq) for h in range(N_HEADS64)]

    @pl.when((b == 0) & (t_blk == 0))
    def _():
        for h in range(N_HEADS64):
            bias_ref[0, hrows[h], :] = jnp.full((tq, 256), cfar_ref[h] * LOG2E, F32)
            bias_ref[1, hrows[h], :] = _toeplitz(base1_ref[h:h + 1, :], tq, 256) * LOG2E
            bias_ref[2, hrows[h], :] = _toeplitz(base0_ref[h:h + 1, :], tq, 256) * LOG2E

    @pl.when(t_blk == 0)
    def _():
        lo256 = _lo_mask(256)
        for r in range(seq // 256):
            rows = slice(r * 256, (r + 1) * 256)
            k = k_ref[rows, :].astype(F32)
            ms = jnp.mean(k * k, axis=-1, keepdims=True)
            kn_ref[rows, :] = (k * lax.rsqrt(ms + EPS) * kg_ref[...]).astype(BF16)
            v = v_ref[rows, :]
            v1_ref[rows, :] = jnp.where(lo256, v, jnp.ones_like(v))

    lo = _lo_mask(tq)
    qtiles = _rms_heads64(q_ref[...].astype(F32), qg_ref[...], 4)
    for h in range(N_HEADS64):
        sel = lo if h % 2 == 0 else jnp.logical_not(lo)
        qall_ref[hrows[h], :] = jnp.where(sel, qtiles[h // 2] * (0.125 * LOG2E), 0.0).astype(BF16)
    mp_ref[...] = jnp.full(mp_ref.shape, NEG, F32)
    acc_ref[...] = jnp.zeros(acc_ref.shape, F32)
    nblk = t_blk + 1

    def logits_blocks(kb0, n):
        for i in range(n):
            kb = kb0 + i
            off = pl.multiple_of(kb * 256, 256)
            kblk = kn_ref[pl.ds(off, 256), :]
            which = jnp.clip(kb - (t_blk - 2), 0, 2)
            keep = msk_ref[0, kb] > 0.5
            for h in range(N_HEADS64):
                s = jnp.where(keep, _dot_t(qall_ref[hrows[h], :], kblk) + bias_ref[which, hrows[h], :], NEG)
                s_ref[kb, hrows[h], :] = s
                mp_ref[hrows[h], :] = jnp.maximum(mp_ref[hrows[h], :], jnp.maximum(s[:, :LANES], s[:, LANES:]))

    def value_blocks(kb0, n):
        off = pl.multiple_of(kb0 * 256, 256)
        v1 = v1_ref[pl.ds(off, n * 256), :]
        for h in range(N_HEADS64):
            m = mp_ref[hrows[h], :]
            mm = jnp.concatenate([m, m], axis=1)
            p = [jnp.exp2(s_ref[kb0 + i, hrows[h], :] - mm).astype(BF16) for i in range(n)]
            p = p[0] if n == 1 else jnp.concatenate(p, axis=1)
            acc_ref[hrows[h], :] += jnp.dot(p, v1, preferred_element_type=F32)

    def pairs_then_rest(blocks):
        def pair(i, carry):
            blocks(2 * i, 2)
            return carry

        lax.fori_loop(0, nblk // 2, pair, 0)

        @pl.when(nblk % 2 == 1)
        def _():
            blocks(nblk - 1, 1)

    pairs_then_rest(logits_blocks)
    for h in range(N_HEADS64):
        m = jnp.max(mp_ref[hrows[h], :], axis=-1, keepdims=True)
        mp_ref[hrows[h], :] = jnp.broadcast_to(m, (tq, LANES))
    pairs_then_rest(value_blocks)

    gate = _silu(z_ref[...].astype(F32))
    for t in range(4):
        a_even = acc_ref[hrows[2 * t], :]
        a_odd = acc_ref[hrows[2 * t + 1], :]
        o_even = a_even / pltpu.roll(a_even, 64, 1)
        o_odd = pltpu.roll(a_odd, 64, 1) / a_odd
        cols = slice(t * LANES, (t + 1) * LANES)
        o_ref[:, cols] = (jnp.where(lo, o_even, o_odd) * gate[:, cols]).astype(BF16)


def _b_attn(h, mask, cfar, qg, kg, base0, base1, bsz, seq, tq=256):
    nt = seq // tq
    rows = N_HEADS64 * tq
    kern = functools.partial(_b_attn_kernel, seq=seq, tq=tq)
    full = lambda shape: pl.BlockSpec(shape, lambda b, t: (0,) * len(shape))
    return pl.pallas_call(
        kern,
        grid=(bsz, nt),
        in_specs=[pl.BlockSpec(memory_space=pltpu.SMEM),
                  pl.BlockSpec((tq, GROUP_W), lambda b, t: (b * nt + t, BLK_B_Q)),
                  pl.BlockSpec((tq, GROUP_W), lambda b, t: (b * nt + t, BLK_B_Z)),
                  pl.BlockSpec((seq, LANES), lambda b, t: (b, UNIT_B_K)),
                  pl.BlockSpec((seq, LANES), lambda b, t: (b, UNIT_B_V)),
                  pl.BlockSpec((1, seq // 256, tq, 256), lambda b, t: (b * nt + t, 0, 0, 0)),
                  full((1, GROUP_W)), full((1, LANES)), full((N_HEADS64, 512)), full((N_HEADS64, 512))],
        out_specs=pl.BlockSpec((tq, GROUP_W), lambda b, t: (b * nt + t, 0)),
        out_shape=jax.ShapeDtypeStruct((bsz * seq, GROUP_W), BF16),
        scratch_shapes=[pltpu.VMEM((seq, LANES), BF16), pltpu.VMEM((seq, LANES), BF16),
                        pltpu.VMEM((3, rows, 256), F32), pltpu.VMEM((rows, LANES), BF16),
                        pltpu.VMEM((seq // 256, rows, 256), F32),
                        pltpu.VMEM((rows, LANES), F32), pltpu.VMEM((rows, LANES), F32)],
        compiler_params=_params(2),
    )(cfar, h, h, h, h, mask, qg, kg, base0, base1)


def _rope(tile, cos, sin):
    return tile * cos + pltpu.roll(tile, 64, 1) * sin


def _c_prep_kernel(lat_ref, kr_ref, wq_ref, wkv_ref, qag_ref, kvag_ref, qg_ref, kg_ref, cos_ref, sin_ref,
                   qo_ref, ko_ref, vo_ref):
    cq = lat_ref[:, :Q_LORA].astype(F32)
    ms = jnp.mean(cq * cq, axis=-1, keepdims=True)
    cqn = (cq * lax.rsqrt(ms + EPS) * qag_ref[...]).astype(BF16)
    qpre = jnp.dot(cqn, wq_ref[...], preferred_element_type=F32)
    ckv = lat_ref[:, Q_LORA:].astype(F32)
    ms = jnp.mean(ckv * ckv, axis=-1, keepdims=True)
    ckvn = (ckv * lax.rsqrt(ms + EPS) * kvag_ref[...]).astype(BF16)
    kvpre = jnp.dot(ckvn, wkv_ref[...], preferred_element_type=F32)
    lane = lax.broadcasted_iota(jnp.int32, kr_ref.shape, 1)
    kr = jnp.where((lane % 64) < 32, kr_ref[...].astype(F32), 0.0)
    kr_ss = jnp.sum(kr * kr, axis=-1, keepdims=True)
    cos = cos_ref[...]
    sin = sin_ref[...]
    qg = qg_ref[...]
    kg = kg_ref[...]
    kr_rot = _rope(kr * kg[:, LANES:], cos, sin)
    for h in range(C_HEADS):
        qh = qpre[:, h * 256:(h + 1) * 256]
        r = lax.rsqrt(jnp.sum(qh * qh, axis=-1, keepdims=True) * (1.0 / C_QK) + EPS)
        qn = qh * r * qg
        qo_ref[:, h * 256:h * 256 + LANES] = qn[:, :LANES].astype(BF16)
        qo_ref[:, h * 256 + LANES:(h + 1) * 256] = _rope(qn[:, LANES:], cos, sin).astype(BF16)
        kn = kvpre[:, h * LANES:(h + 1) * LANES]
        r = lax.rsqrt((jnp.sum(kn * kn, axis=-1, keepdims=True) + kr_ss) * (1.0 / C_QK) + EPS)
        ko_ref[:, h * 256:h * 256 + LANES] = (kn * r * kg[:, :LANES]).astype(BF16)
        ko_ref[:, h * 256 + LANES:(h + 1) * 256] = (kr_rot * r).astype(BF16)
    vo_ref[...] = kvpre[:, C_HEADS * LANES:].astype(BF16)


def _c_prep(h, wq, wkv, qag, kvag, qg, kg, cos, sin, seq, tm=512):
    n = h.shape[0]
    ns = seq // tm
    full = lambda shape: pl.BlockSpec(shape, lambda i: (0,) * len(shape))
    return pl.pallas_call(
        _c_prep_kernel,
        grid=(n // tm,),
        in_specs=[pl.BlockSpec((tm, GROUP_W), lambda i: (i, BLK_C_QKV)),
                  pl.BlockSpec((tm, LANES), lambda i: (i, UNIT_KR_IW)),
                  full((Q_LORA, 4 * 256)), full((KV_LORA, 8 * LANES)),
                  full((1, Q_LORA)), full((1, LANES)), full((1, 256)), full((1, 256)),
                  pl.BlockSpec((tm, LANES), lambda i: (i % ns, 0)),
                  pl.BlockSpec((tm, LANES), lambda i: (i % ns, 0))],
        out_specs=[pl.BlockSpec((tm, 4 * 256), lambda i: (i, 0)),
                   pl.BlockSpec((tm, 4 * 256), lambda i: (i, 0)),
                   pl.BlockSpec((tm, GROUP_W), lambda i: (i, 0))],
        out_shape=[jax.ShapeDtypeStruct((n, 4 * 256), BF16), jax.ShapeDtypeStruct((n, 4 * 256), BF16),
                   jax.ShapeDtypeStruct((n, GROUP_W), BF16)],
        compiler_params=_params(1),
    )(h, h, wq, wkv, qag, kvag, qg, kg, cos, sin)


def _c_attn_kernel(q_ref, k_ref, v_ref, z_ref, o_ref, s_ref, mp_ref, lp_ref, acc_ref, *, tq):
    qt = pl.program_id(1)
    scale = C_QK ** -0.5 * LOG2E
    hrows = [slice(h * tq, (h + 1) * tq) for h in range(C_HEADS)]
    qchunk = (qt * tq + lax.broadcasted_iota(jnp.int32, (tq, 256), 0)) // CHUNK
    kcol = lax.broadcasted_iota(jnp.int32, (tq, 256), 1)
    mp_ref[...] = jnp.full(mp_ref.shape, NEG, F32)
    lp_ref[...] = jnp.zeros(lp_ref.shape, F32)
    acc_ref[...] = jnp.zeros(acc_ref.shape, F32)
    nfull = (qt * tq) // 256
    nblk = ((qt + 1) * tq) // 256

    def logits_blocks(kb0, n, masked):
        for i in range(n):
            kb = kb0 + i
            off = pl.multiple_of(kb * 256, 256)
            for h in range(C_HEADS):
                cols = slice(h * 256, (h + 1) * 256)
                s = _dot_t(q_ref[:, cols], k_ref[pl.ds(off, 256), cols]) * scale
                if masked:
                    s = jnp.where(((kb * 256 + kcol) // CHUNK) <= qchunk, s, NEG)
                s_ref[kb, hrows[h], :] = s
                mp_ref[hrows[h], :] = jnp.maximum(mp_ref[hrows[h], :], jnp.maximum(s[:, :LANES], s[:, LANES:]))

    def value_blocks(kb0, n):
        off = pl.multiple_of(kb0 * 256, 256)
        for h in range(C_HEADS):
            m = mp_ref[hrows[h], :]
            mm = jnp.concatenate([m, m], axis=1)
            p = [jnp.exp2(s_ref[kb0 + i, hrows[h], :] - mm) for i in range(n)]
            lsum = p[0][:, :LANES] + p[0][:, LANES:]
            for pi in p[1:]:
                lsum = lsum + pi[:, :LANES] + pi[:, LANES:]
            lp_ref[hrows[h], :] += lsum
            pb = p[0].astype(BF16) if n == 1 else jnp.concatenate([pi.astype(BF16) for pi in p], axis=1)
            acc_ref[hrows[h], :] += jnp.dot(pb, v_ref[pl.ds(off, n * 256), h * LANES:(h + 1) * LANES],
                                            preferred_element_type=F32)

    def pairs_then_rest(blocks, count):
        def pair(i, carry):
            blocks(2 * i, 2)
            return carry

        lax.fori_loop(0, count // 2, pair, 0)

        @pl.when(count % 2 == 1)
        def _():
            blocks(count - 1, 1)

    pairs_then_rest(lambda kb0, n: logits_blocks(kb0, n, False), nfull)
    lax.fori_loop(nfull, nblk, lambda kb, c: (logits_blocks(kb, 1, True), c)[1], 0)
    for h in range(C_HEADS):
        m = jnp.max(mp_ref[hrows[h], :], axis=-1, keepdims=True)
        mp_ref[hrows[h], :] = jnp.broadcast_to(m, (tq, LANES))
    pairs_then_rest(value_blocks, nblk)
    gate = _silu(z_ref[...].astype(F32))
    for h in range(C_HEADS):
        cols = slice(h * LANES, (h + 1) * LANES)
        l = jnp.sum(lp_ref[hrows[h], :], axis=-1, keepdims=True)
        o_ref[:, cols] = (acc_ref[hrows[h], :] / l * gate[:, cols]).astype(BF16)


def _c_attn(qc, kc, vc, h, bsz, seq, tq=256):
    nt = seq // tq
    rows = C_HEADS * tq
    kern = functools.partial(_c_attn_kernel, tq=tq)
    return pl.pallas_call(
        kern,
        grid=(bsz, nt),
        in_specs=[pl.BlockSpec((tq, C_HEADS * 256), lambda b, t: (b * nt + t, 0)),
                  pl.BlockSpec((seq, C_HEADS * 256), lambda b, t: (b, 0)),
                  pl.BlockSpec((seq, GROUP_W), lambda b, t: (b, 0)),
                  pl.BlockSpec((tq, GROUP_W), lambda b, t: (b * nt + t, BLK_C_Z))],
        out_specs=pl.BlockSpec((tq, GROUP_W), lambda b, t: (b * nt + t, 0)),
        out_shape=jax.ShapeDtypeStruct((bsz * seq, GROUP_W), BF16),
        scratch_shapes=[pltpu.VMEM((seq // 256, rows, 256), F32), pltpu.VMEM((rows, LANES), F32),
                        pltpu.VMEM((rows, LANES), F32), pltpu.VMEM((rows, LANES), F32)],
        compiler_params=_params(2),
    )(qc, kc, vc, h)


def _mixer_d_kernel(q_ref, k_ref, v_ref, z_ref, qg_ref, kg_ref, base_ref, o_ref,
                    kpad_ref, vpad_ref, bias_ref, s_ref, mp_ref, *, seq, tq):
    b = pl.program_id(0)
    qt = pl.program_id(1)
    win = tq + D_LEFT

    @pl.when((b == 0) & (qt == 0))
    def _():
        qc = lax.broadcasted_iota(jnp.int32, (tq, win), 0) // CHUNK
        kc = lax.broadcasted_iota(jnp.int32, (tq, win), 1) // CHUNK
        band = (kc >= qc) & (kc <= qc + D_LEFT // CHUNK)
        for h in range(N_HEADS64):
            bias_ref[h] = jnp.where(band, _toeplitz(base_ref[h:h + 1, :], tq, win) * LOG2E, NEG)

    @pl.when(qt == 0)
    def _():
        kpad_ref[0:D_LEFT, :] = jnp.zeros((D_LEFT, GROUP_W), BF16)
        vpad_ref[0:D_LEFT, :] = jnp.zeros((D_LEFT, GROUP_W), BF16)
        for r in range(seq // 256):
            rows = slice(r * 256, (r + 1) * 256)
            dst = slice(D_LEFT + r * 256, D_LEFT + (r + 1) * 256)
            tiles = _rms_heads64(k_ref[rows, :].astype(F32), kg_ref[...], 4)
            for t in range(4):
                kpad_ref[dst, t * LANES:(t + 1) * LANES] = tiles[t].astype(BF16)
            vpad_ref[dst, :] = v_ref[rows, :]

    lo = _lo_mask(tq)
    qtiles = _rms_heads64(q_ref[...].astype(F32), qg_ref[...], 4)
    start = pl.multiple_of(qt * tq, tq)
    gate = _silu(z_ref[...].astype(F32))
    ntile = win // LANES

    def lane_tiles(x):
        return [x[:, i * LANES:(i + 1) * LANES] for i in range(ntile)]

    def logits_pass(before_start):
        if before_start:
            in_seq = lax.broadcasted_iota(jnp.int32, (tq, win), 1) + start >= D_LEFT
        for t in range(4):
            kwin = kpad_ref[pl.ds(start, win), t * LANES:(t + 1) * LANES]
            for half in range(2):
                h = 2 * t + half
                sel = lo if half == 0 else jnp.logical_not(lo)
                qh = jnp.where(sel, qtiles[t] * (0.125 * LOG2E), 0.0).astype(BF16)
                s = _dot_t(qh, kwin) + bias_ref[h]
                if before_start:
                    s = jnp.where(in_seq, s, NEG)
                s_ref[h] = s
                mp_ref[h] = functools.reduce(jnp.maximum, lane_tiles(s))

    @pl.when(start < D_LEFT)
    def _():
        logits_pass(True)

    @pl.when(start >= D_LEFT)
    def _():
        logits_pass(False)

    for h in range(N_HEADS64):
        mp_ref[h] = jnp.broadcast_to(jnp.max(mp_ref[h], axis=-1, keepdims=True), (tq, LANES))
    for t in range(4):
        cols = slice(t * LANES, (t + 1) * LANES)
        vwin = vpad_ref[pl.ds(start, win), cols]
        outs = []
        for half in range(2):
            h = 2 * t + half
            m = mp_ref[h]
            p = jnp.exp2(s_ref[h] - jnp.concatenate([m] * ntile, axis=1))
            l = jnp.sum(functools.reduce(jnp.add, lane_tiles(p)), axis=-1, keepdims=True)
            outs.append(jnp.dot(p.astype(BF16), vwin, preferred_element_type=F32) / l)
        o_ref[:, cols] = (jnp.where(lo, outs[0], outs[1]) * gate[:, cols]).astype(BF16)


def _mixer_d(h, qg, kg, base, bsz, seq, tq=256):
    nt = seq // tq
    kern = functools.partial(_mixer_d_kernel, seq=seq, tq=tq)
    full = lambda shape: pl.BlockSpec(shape, lambda b, t: (0,) * len(shape))
    return pl.pallas_call(
        kern,
        grid=(bsz, nt),
        in_specs=[pl.BlockSpec((tq, GROUP_W), lambda b, t: (b * nt + t, BLK_D_Q)),
                  pl.BlockSpec((seq, GROUP_W), lambda b, t: (b, BLK_D_K)),
                  pl.BlockSpec((seq, GROUP_W), lambda b, t: (b, BLK_D_V)),
                  pl.BlockSpec((tq, GROUP_W), lambda b, t: (b * nt + t, BLK_D_Z)),
                  full((1, GROUP_W)), full((1, GROUP_W)), full((N_HEADS64, 2 * tq + D_LEFT))],
        out_specs=pl.BlockSpec((tq, GROUP_W), lambda b, t: (b * nt + t, 0)),
        out_shape=jax.ShapeDtypeStruct((bsz * seq, GROUP_W), BF16),
        scratch_shapes=[pltpu.VMEM((seq + D_LEFT, GROUP_W), BF16), pltpu.VMEM((seq + D_LEFT, GROUP_W), BF16),
                        pltpu.VMEM((N_HEADS64, tq, tq + D_LEFT), F32),
                        pltpu.VMEM((N_HEADS64, tq, tq + D_LEFT), F32), pltpu.VMEM((N_HEADS64, tq, LANES), F32)],
        compiler_params=_params(2),
    )(h, h, h, h, qg, kg, base)


def _w_in_pieces(take, zeros):
    c = lambda name, size, off=0: take(_SRC[name] + off, size)
    return [c("a_u", 512), c("a_v", 512), c("a_z", 512),
            c("b_q", 512), c("b_iq", 512), c("b_z", 512),
            c("b_k", 64), c("b_k", 64), c("b_v", 64), c("b_v", 64), c("b_ik", 64), c("b_ik", 64),
            c("c_kr", 32), c("b_iw", 8), zeros(24), c("c_kr", 32, 32), zeros(32),
            c("c_q", 384), c("c_kv", 128), c("c_z", 512),
            c("d_q", 512), c("d_k", 512), c("d_v", 512), c("d_z", 512)]


def _layout_w_in_kernel(w_ref, o_ref):
    tk = w_ref.shape[2]
    pieces = _w_in_pieces(lambda s, n: w_ref[0, s:s + n, :], lambda n: jnp.zeros((n, tk), F32))
    ends = np.cumsum([0] + [p.shape[0] for p in pieces])
    start = 0
    for i in range(1, len(pieces) + 1):
        if ends[i] % GROUP_W == 0:
            group = pieces[start:i]
            blk = group[0] if len(group) == 1 else jnp.concatenate(group, axis=0)
            o_ref[0, ends[start]:ends[i], :] = blk.astype(BF16)
            start = i


def _layout_w_in(w_in, tk=256):
    w_t = jnp.swapaxes(w_in, 1, 2)
    depth, cols, d = w_t.shape
    return pl.pallas_call(
        _layout_w_in_kernel,
        grid=(depth, d // tk),
        in_specs=[pl.BlockSpec((1, cols, tk), lambda l, i: (l, 0, i))],
        out_specs=pl.BlockSpec((1, H_COLS, tk), lambda l, i: (l, 0, i)),
        out_shape=jax.ShapeDtypeStruct((depth, H_COLS, d), BF16),
        compiler_params=_params(2),
    )(w_t)


def _rope_layout(v):
    z = jnp.zeros(v.shape[:-1] + (32,), v.dtype)
    return jnp.concatenate([v[..., :32], z, v[..., 32:], z], axis=-1)


def _layout_c(w_qb, w_kvb, q_gain, k_gain, qa_gain):
    wq = w_qb.reshape(Q_LORA, C_HEADS, C_QK)
    wq = jnp.concatenate([wq[..., :C_NOPE], _rope_layout(wq[..., C_NOPE:])], axis=-1)
    wq = wq.reshape(Q_LORA, C_HEADS * 256).astype(BF16)
    wkv = w_kvb.reshape(KV_LORA, C_HEADS, 2 * LANES)
    wkv = jnp.concatenate([wkv[..., :C_NOPE].reshape(KV_LORA, -1), wkv[..., C_NOPE:].reshape(KV_LORA, -1)],
                          axis=1).astype(BF16)
    lay = lambda g: jnp.concatenate([g[:C_NOPE], _rope_layout(g[C_NOPE:])])[None, :]
    return wq, wkv, lay(q_gain), lay(k_gain), qa_gain[None, :]


def _t5_bucket_static(rel):
    half = T5_BUCKETS // 2
    exact = half // 2
    n = abs(rel)
    if n < exact:
        val = n
    else:
        val = min(exact + (n * n // (exact * exact)).bit_length() - 1, half - 1)
    return (half if rel > 0 else 0) + val


def _t5_tables(t5_bias):
    m = np.arange(512)
    d0 = np.where(m < 256, m, m - 512)
    d1 = np.where(m <= 256, m - 256, m - 768)
    idx0 = np.array([_t5_bucket_static(int(d)) for d in d0], np.int32)
    idx1 = np.array([_t5_bucket_static(int(d)) for d in d1], np.int32)
    far = _t5_bucket_static(-512)
    return t5_bias[idx0].T, t5_bias[idx1].T, t5_bias[far]


def _band_table(rel_bias, tq):
    width = 2 * tq + D_LEFT
    m = np.arange(width)
    dist = np.where(m <= tq + D_LEFT, D_LEFT - m, D_LEFT + width - m)
    idx = np.clip(dist, -REL_CLIP, REL_CLIP) + REL_CLIP
    return rel_bias[idx.astype(np.int32)].T


def _rope_tables(seq):
    inv = ROPE_BASE ** (-jnp.arange(0, C_ROPE, 2, dtype=F32) / C_ROPE)
    ang = jnp.arange(seq, dtype=F32)[:, None] * inv[None, :]
    c, s = jnp.cos(ang), jnp.sin(ang)
    z = jnp.zeros_like(c)
    return jnp.concatenate([c, z, c, z], axis=1), jnp.concatenate([-s, z, s, z], axis=1)


def kernel(x, t5_bias, norm_g, w_in, a_v_gain, a_ws, a_bs, b_q_gain, b_k_gain, c_qa_gain, c_kva_gain,
           c_w_qb, c_w_kvb, c_q_gain, c_k_gain, d_q_gain, d_k_gain, d_rel_bias, w_out):
    bsz, seq, d_model = x.shape
    depth = w_in.shape[0]
    tq = 256
    x2 = x.reshape(bsz * seq, d_model)
    cos, sin = _rope_tables(seq)
    base0, base1, cfar = _t5_tables(t5_bias)
    w_in_blocks = _layout_w_in(w_in)
    for l in range(depth):
        h = _inproj(x2, norm_g[l][None, :], w_in_blocks, l)
        y_a = _mixer_a(h, a_v_gain[l][None, :], a_ws[l], a_bs[l][:, :, None])
        mask = _b_select(h, bsz, seq)
        y_b = _b_attn(h, mask, cfar, jnp.tile(b_q_gain[l], N_HEADS64)[None, :],
                      jnp.tile(b_k_gain[l], 2)[None, :], base0, base1, bsz, seq, tq)
        wq, wkv, qg, kg, qag = _layout_c(c_w_qb[l], c_w_kvb[l], c_q_gain[l], c_k_gain[l], c_qa_gain[l])
        qc, kc, vc = _c_prep(h, wq, wkv, qag, c_kva_gain[l][None, :], qg, kg, cos, sin, seq)
        y_c = _c_attn(qc, kc, vc, h, bsz, seq, tq)
        y_d = _mixer_d(h, jnp.tile(d_q_gain[l], N_HEADS64)[None, :], jnp.tile(d_k_gain[l], N_HEADS64)[None, :],
                       _band_table(d_rel_bias[l], tq), bsz, seq, tq)
        x2 = _outproj(x2, (y_a, y_b, y_c, y_d), w_out[l].astype(BF16))
    return x2.reshape(bsz, seq, d_model)
```

```python
import functools
import math

import numpy as np
import jax
import jax.numpy as jnp
from jax import lax
from jax.experimental import pallas as pl
from jax.experimental.pallas import tpu as pltpu

F32 = jnp.float32
BF16 = jnp.bfloat16

EPS = 1e-6
NEG = -1e30
LOG2E = math.log2(math.e)
INT_MIN = -(2 ** 31)
CHUNK = 64
LANES = 128
GROUP_W = 512
A_GROUPS = 4
GMLP_BLOCK = 128
N_HEADS64 = 8
IDX_SCALE = (8 ** -0.5) * 0.125
TOPK_MAX = 256
T5_BUCKETS = 32
C_HEADS = 4
C_NOPE = 128
C_ROPE = 64
C_QK = 192
Q_LORA = 384
KV_LORA = 128
ROPE_BASE = 10000.0
D_LEFT = 8 * CHUNK
REL_CLIP = 128
VMEM_LIMIT = 56 * 1024 * 1024

BLK_A_U, BLK_A_V, BLK_A_Z = 0, 1, 2
BLK_B_Q, BLK_B_IQ, BLK_B_Z, BLK_SMALL = 3, 4, 5, 6
BLK_C_QKV, BLK_C_Z = 7, 8
BLK_D_Q, BLK_D_K, BLK_D_V, BLK_D_Z = 9, 10, 11, 12
H_COLS = 13 * GROUP_W
UNIT_B_K, UNIT_B_V, UNIT_B_IK, UNIT_KR_IW = (BLK_SMALL * 4 + i for i in range(4))
IW_LANE = 32

_SRC = dict(a_u=0, a_v=512, a_z=1024, b_q=1536, b_k=2048, b_v=2112, b_iq=2176, b_ik=2688,
            b_iw=2752, b_z=2760, c_q=3272, c_kv=3656, c_kr=3784, c_z=3848,
            d_q=4360, d_k=4872, d_v=5384, d_z=5896)


def _params(n_axes):
    return pltpu.CompilerParams(dimension_semantics=("arbitrary",) * n_axes,
                                vmem_limit_bytes=VMEM_LIMIT)


def _gelu(x):
    c = math.sqrt(2.0 / math.pi)
    return x * (0.5 * (1.0 + jnp.tanh(c * (x + 0.044715 * (x * x * x)))))


def _silu(x):
    return x * (1.0 / (1.0 + jnp.exp(-x)))


def _dot_t(a, b):
    return lax.dot_general(a, b, (((1,), (1,)), ((), ())), preferred_element_type=F32)


def _lo_mask(rows):
    return lax.broadcasted_iota(jnp.int32, (rows, LANES), 1) < 64


def _rms_heads64(x, gain, ntiles):
    lo = _lo_mask(x.shape[0])
    tiles = []
    for t in range(ntiles):
        xt = x[:, t * LANES:(t + 1) * LANES]
        sq = xt * xt
        s_lo = jnp.sum(jnp.where(lo, sq, 0.0), axis=-1, keepdims=True)
        s_hi = jnp.sum(jnp.where(lo, 0.0, sq), axis=-1, keepdims=True)
        r = jnp.where(lo, lax.rsqrt(s_lo * (1.0 / 64) + EPS), lax.rsqrt(s_hi * (1.0 / 64) + EPS))
        tiles.append(xt * r * gain[:, t * LANES:(t + 1) * LANES])
    return tiles


def _toeplitz(base_row, rows, width):
    t = jnp.broadcast_to(base_row, (rows, base_row.shape[1]))
    t = pltpu.roll(t, 0, 1, stride=1, stride_axis=0)
    return t[:, :width]


def _inproj_kernel(x_ref, g_ref, w_ref, o_ref):
    x = x_ref[...]
    ms = jnp.mean(x * x, axis=-1, keepdims=True)
    xn = (x * lax.rsqrt(ms + EPS) * g_ref[...]).astype(BF16)
    for c in range(H_COLS // GROUP_W):
        cols = slice(c * GROUP_W, (c + 1) * GROUP_W)
        o_ref[:, cols] = _dot_t(xn, w_ref[cols, :]).astype(BF16)


def _inproj(x2, g, w_all, layer, tm=512):
    n, d = x2.shape
    return pl.pallas_call(
        _inproj_kernel,
        grid=(n // tm,),
        in_specs=[pl.BlockSpec((tm, d), lambda i: (i, 0)),
                  pl.BlockSpec((1, d), lambda i: (0, 0)),
                  pl.BlockSpec((None, H_COLS, d), lambda i: (layer, 0, 0), pipeline_mode=pl.Buffered(1))],
        out_specs=pl.BlockSpec((tm, H_COLS), lambda i: (i, 0)),
        out_shape=jax.ShapeDtypeStruct((n, H_COLS), BF16),
        compiler_params=_params(1),
    )(x2, g, w_all)


def _outproj_kernel(x_ref, ya_ref, yb_ref, yc_ref, yd_ref, w_ref, o_ref):
    acc = x_ref[...]
    for g, y_ref in enumerate((ya_ref, yb_ref, yc_ref, yd_ref)):
        acc = acc + jnp.dot(y_ref[...], w_ref[g * GROUP_W:(g + 1) * GROUP_W, :],
                            preferred_element_type=F32)
    o_ref[...] = acc


def _outproj(x2, ys, w, tm=512):
    n, d = x2.shape
    yspec = pl.BlockSpec((tm, GROUP_W), lambda i: (i, 0))
    return pl.pallas_call(
        _outproj_kernel,
        grid=(n // tm,),
        in_specs=[pl.BlockSpec((tm, d), lambda i: (i, 0)), yspec, yspec, yspec, yspec,
                  pl.BlockSpec((4 * GROUP_W, d), lambda i: (0, 0))],
        out_specs=pl.BlockSpec((tm, d), lambda i: (i, 0)),
        out_shape=jax.ShapeDtypeStruct((n, d), F32),
        compiler_params=_params(1),
    )(x2, *ys, w)


def _mixer_a_kernel(u_ref, v_ref, z_ref, vg_ref, w_ref, b_ref, o_ref):
    tm = u_ref.shape[0]
    u = _gelu(u_ref[...].astype(F32))
    v = _gelu(v_ref[...].astype(F32))
    ms = jnp.mean(v * v, axis=-1, keepdims=True)
    vb = (v * lax.rsqrt(ms + EPS) * vg_ref[...]).astype(BF16)
    gate = _silu(z_ref[...].astype(F32))
    i = lax.broadcasted_iota(jnp.int32, (GMLP_BLOCK, GMLP_BLOCK), 0)
    j = lax.broadcasted_iota(jnp.int32, (GMLP_BLOCK, GMLP_BLOCK), 1)
    keep = (j // CHUNK) <= (i // CHUNK)
    for g in range(A_GROUPS):
        wg = jnp.where(keep, w_ref[g], 0.0).astype(BF16)
        cols = slice(g * LANES, (g + 1) * LANES)
        for blk in range(tm // GMLP_BLOCK):
            rows = slice(blk * GMLP_BLOCK, (blk + 1) * GMLP_BLOCK)
            sg = jnp.dot(wg, vb[rows, cols], preferred_element_type=F32) + b_ref[g]
            o_ref[rows, cols] = (u[rows, cols] * sg * gate[rows, cols]).astype(BF16)


def _mixer_a(h, vg, ws, bs, tm=512):
    n = h.shape[0]
    hspec = lambda blk: pl.BlockSpec((tm, GROUP_W), lambda i, blk=blk: (i, blk))
    return pl.pallas_call(
        _mixer_a_kernel,
        grid=(n // tm,),
        in_specs=[hspec(BLK_A_U), hspec(BLK_A_V), hspec(BLK_A_Z),
                  pl.BlockSpec((1, GROUP_W), lambda i: (0, 0)),
                  pl.BlockSpec((A_GROUPS, GMLP_BLOCK, GMLP_BLOCK), lambda i: (0, 0, 0)),
                  pl.BlockSpec((A_GROUPS, GMLP_BLOCK, 1), lambda i: (0, 0, 0))],
        out_specs=pl.BlockSpec((tm, GROUP_W), lambda i: (i, 0)),
        out_shape=jax.ShapeDtypeStruct((n, GROUP_W), BF16),
        compiler_params=_params(1),
    )(h, h, h, vg, ws, bs)


def _order_key(x):
    return jnp.where(x < 0, x ^ 0x7FFFFFFF, x)


def _b_select_kernel(iq_ref, iw_ref, ik_ref, o_ref, lhs_ref, key_ref, *, seq, tq, topk):
    t_blk = pl.program_id(1)
    nkb = seq // 256
    nblk = ((t_blk + 1) * tq) // 256
    n_interp = 12
    n_unchecked = 12
    hrows = [slice(h * tq, (h + 1) * tq) for h in range(N_HEADS64)]

    lo_half = _lo_mask(tq)
    w_t = (iw_ref[...].astype(F32) * IDX_SCALE).T
    for h in range(N_HEADS64):
        iqt = iq_ref[:, (h // 2) * LANES:(h // 2 + 1) * LANES]
        sel = lo_half if h % 2 == 0 else jnp.logical_not(lo_half)
        lhs_ref[hrows[h], :] = jnp.where(sel, iqt, jnp.zeros_like(iqt))

    qpos = t_blk * tq + lax.broadcasted_iota(jnp.int32, (256, tq), 1)
    krow = lax.broadcasted_iota(jnp.int32, (256, tq), 0)

    def fold8(x):
        return jnp.sum(x.reshape(256 // 8, 8, tq), axis=0)

    def score_block(kb, carry, masked):
        amax, n_pos, n_nn = carry
        off = pl.multiple_of(kb * 256, 256)
        ikblk = ik_ref[pl.ds(off, 256), :]
        score = jnp.zeros((256, tq), F32)
        for h in range(N_HEADS64):
            w_h = w_t[IW_LANE + h:IW_LANE + h + 1, :]
            score = score + w_h * jnp.maximum(_dot_t(ikblk, lhs_ref[hrows[h], :]), 0.0)
        score = jnp.where(score == 0.0, 0.0, score)
        key = _order_key(lax.bitcast_convert_type(score, jnp.int32))
        if masked:
            key = jnp.where(((kb * 256 + krow) // CHUNK) <= (qpos // CHUNK), key, INT_MIN)
        key_ref[kb] = key
        return (jnp.maximum(amax, jnp.max(jnp.abs(score).reshape(256 // 8, 8, tq), axis=0)),
                n_pos + fold8(jnp.where(key >= 1, 1.0, 0.0)), n_nn + fold8(jnp.where(key >= 0, 1.0, 0.0)))

    nfull = (t_blk * tq) // 256
    zeros8 = jnp.zeros((8, tq), F32)
    carry = lax.fori_loop(0, nfull, lambda kb, c: score_block(kb, c, False), (zeros8, zeros8, zeros8))
    carry = lax.fori_loop(nfull, nblk, lambda kb, c: score_block(kb, c, True), carry)
    amax = jnp.max(carry[0], axis=0, keepdims=True)
    f_pos = jnp.sum(carry[1], axis=0, keepdims=True)
    f_nn = jnp.sum(carry[2], axis=0, keepdims=True)

    def count(pred):
        def body(kb, acc):
            return acc + fold8(jnp.where(pred(key_ref[kb], kb), 1.0, 0.0))
        return jnp.sum(lax.fori_loop(0, nblk, body, zeros8), axis=0, keepdims=True)

    kf = float(topk)
    qrow = t_blk * tq + lax.broadcasted_iota(jnp.int32, (1, tq), 1)
    n_adm = ((qrow // CHUNK + 1) * CHUNK).astype(F32)
    one = jnp.ones((1, tq), jnp.int32)
    pos = f_pos > kf
    neg = f_nn < kf
    lo0 = jnp.where(pos, one, _order_key(lax.bitcast_convert_type(-amax, jnp.int32)))
    hi0 = jnp.where(neg, one - 1, _order_key(lax.bitcast_convert_type(amax, jnp.int32)) + 1)
    w_lo0 = jnp.where(pos, f_pos, n_adm) - kf
    w_hi0 = kf - jnp.where(neg, f_nn, 0.0)
    all_sel = n_adm <= kf
    at_zero = jnp.logical_not(pos | neg)
    done0 = jnp.where(all_sel | at_zero | (hi0 == lo0 + 1), 1.0, 0.0)
    thr0 = jnp.where(all_sel, INT_MIN + 1, jnp.where(at_zero, jnp.where(f_pos == kf, one, one - 1), lo0))

    def search_cond(st):
        return jnp.logical_and(st[0][0] < n_interp + 32, st[1] < 0.5)

    def search_step(st):
        it, lo, hi, w_lo, w_hi, side, done, thr = st
        lo_v = lax.bitcast_convert_type(_order_key(lo), F32)
        hi_v = lax.bitcast_convert_type(_order_key(hi), F32)
        c_v = lo_v + (hi_v - lo_v) * (w_lo / (w_lo + w_hi))
        c_interp = _order_key(lax.bitcast_convert_type(c_v, jnp.int32))
        c_mid = (lo >> 1) + (hi >> 1) + (lo & hi & 1)
        cand = jnp.where(it < n_interp, c_interp, c_mid)
        cand = jnp.minimum(jnp.maximum(cand, lo + 1), hi - 1)
        f = count(lambda k, kb: k >= cand)
        live = done < 0.5
        up = f > kf
        hit = f == kf
        new_lo = jnp.where(live & up, cand, lo)
        new_hi = jnp.where(live & jnp.logical_not(up), cand, hi)
        new_w_lo = jnp.where(up, f - kf, jnp.where(side < 0.0, 0.5 * w_lo, w_lo))
        new_w_hi = jnp.where(up, jnp.where(side > 0.0, 0.5 * w_hi, w_hi), kf - f)
        new_side = jnp.where(up, 1.0, -1.0)
        new_thr = jnp.where(live, jnp.where(hit, cand, new_lo), thr)
        new_done = jnp.where(live & (hit | (new_hi == new_lo + 1)), 1.0, done)
        return (it + 1, new_lo, new_hi, jnp.where(live, new_w_lo, w_lo),
                jnp.where(live, new_w_hi, w_hi), jnp.where(live, new_side, side), new_done, new_thr)

    def checked_step(st):
        new = search_step(st[0])
        return new, jnp.min(new[-2])

    state = (jnp.int32(0), lo0, hi0, w_lo0, w_hi0, jnp.zeros((1, tq), F32), done0, thr0)
    state = lax.fori_loop(0, n_unchecked, lambda i, st: search_step(st), state)
    thr = lax.while_loop(search_cond, checked_step, (state, jnp.min(state[-2])))[0][-1]

    any_excess = jnp.max(count(lambda k, kb: k >= thr)) > kf

    def store_mask(kb, keep_t):
        for g in range(tq // 256):
            o_ref[g, kb] = keep_t[:, g * 256:(g + 1) * 256].T

    def write_unused(kb, carry):
        store_mask(kb, jnp.zeros((256, tq), F32))
        return carry

    lax.fori_loop(nblk, nkb, write_unused, 0)

    @pl.when(jnp.logical_not(any_excess))
    def _():
        def write(kb, carry):
            store_mask(kb, jnp.where(key_ref[kb] >= thr, 1.0, 0.0))
            return carry

        lax.fori_loop(0, nblk, write, 0)

    @pl.when(any_excess)
    def _():
        need = kf - count(lambda k, kb: k > thr)

        def idx_step(it, jmax):
            cand = jmax | lax.shift_left(jnp.int32(1), 10 - it)
            below = count(lambda k, kb: (k == thr) & ((kb * 256 + krow) < cand))
            return jnp.where(below < need, cand, jmax)

        jmax = lax.fori_loop(0, 11, idx_step, jnp.zeros((1, tq), jnp.int32))

        def write(kb, carry):
            k = key_ref[kb]
            keep_tie = (k == thr) & ((kb * 256 + krow) <= jmax)
            store_mask(kb, jnp.where((k > thr) | keep_tie, 1.0, 0.0))
            return carry

        lax.fori_loop(0, nblk, write, 0)


def _b_select(h, bsz, seq, tq=512):
    tq = min(tq, seq)
    nt = seq // tq
    topk = min(TOPK_MAX, seq // 4)
    kern = functools.partial(_b_select_kernel, seq=seq, tq=tq, topk=topk)
    return pl.pallas_call(
        kern,
        grid=(bsz, nt),
        in_specs=[pl.BlockSpec((tq, GROUP_W), lambda b, t: (b * nt + t, BLK_B_IQ)),
                  pl.BlockSpec((tq, LANES), lambda b, t: (b * nt + t, UNIT_KR_IW)),
                  pl.BlockSpec((seq, LANES), lambda b, t: (b, UNIT_B_IK))],
        out_specs=pl.BlockSpec((tq // 256, seq // 256, 256, 256), lambda b, t: (b * nt + t, 0, 0, 0)),
        out_shape=jax.ShapeDtypeStruct((bsz * seq // 256, seq // 256, 256, 256), F32),
        scratch_shapes=[pltpu.VMEM((N_HEADS64 * tq, LANES), BF16),
                        pltpu.VMEM((seq // 256, 256, tq), jnp.int32)],
        compiler_params=_params(2),
    )(h, h, h)


def _b_attn_kernel(cfar_ref, q_ref, z_ref, k_ref, v_ref, msk_ref, qg_ref, kg_ref, base0_ref, base1_ref,
                   o_ref, kn_ref, v1_ref, bias_ref, qall_ref, s_ref, mp_ref, acc_ref, *, seq, tq):
    b = pl.program_id(0)
    t_blk = pl.program_id(1)
    hrows = [slice(h * tq, (h + 1) * tq) for h in range(N_HEADS64)]

    @pl.when((b == 0) & (t_blk == 0))
    def _():
        for h in range(N_HEADS64):
            bias_ref[0, hrows[h], :] = jnp.full((tq, 256), cfar_ref[h] * LOG2E, F32)
            bias_ref[1, hrows[h], :] = _toeplitz(base1_ref[h:h + 1, :], tq, 256) * LOG2E
            bias_ref[2, hrows[h], :] = _toeplitz(base0_ref[h:h + 1, :], tq, 256) * LOG2E

    @pl.when(t_blk == 0)
    def _():
        lo256 = _lo_mask(256)
        for r in range(seq // 256):
            rows = slice(r * 256, (r + 1) * 256)
            k = k_ref[rows, :].astype(F32)
            ms = jnp.mean(k * k, axis=-1, keepdims=True)
            kn_ref[rows, :] = (k * lax.rsqrt(ms + EPS) * kg_ref[...]).astype(BF16)
            v = v_ref[rows, :]
            v1_ref[rows, :] = jnp.where(lo256, v, jnp.ones_like(v))

    lo = _lo_mask(tq)
    qtiles = _rms_heads64(q_ref[...].astype(F32), qg_ref[...], 4)
    for h in range(N_HEADS64):
        sel = lo if h % 2 == 0 else jnp.logical_not(lo)
        qall_ref[hrows[h], :] = jnp.where(sel, qtiles[h // 2] * (0.125 * LOG2E), 0.0).astype(BF16)
    mp_ref[...] = jnp.full(mp_ref.shape, NEG, F32)
    acc_ref[...] = jnp.zeros(acc_ref.shape, F32)
    nblk = t_blk + 1

    def logits_blocks(kb0, n):
        for i in range(n):
            kb = kb0 + i
            off = pl.multiple_of(kb * 256, 256)
            kblk = kn_ref[pl.ds(off, 256), :]
            which = jnp.clip(kb - (t_blk - 2), 0, 2)
            keep = msk_ref[0, kb] > 0.5
            for h in range(N_HEADS64):
                s = jnp.where(keep, _dot_t(qall_ref[hrows[h], :], kblk) + bias_ref[which, hrows[h], :], NEG)
                s_ref[kb, hrows[h], :] = s
                mp_ref[hrows[h], :] = jnp.maximum(mp_ref[hrows[h], :], jnp.maximum(s[:, :LANES], s[:, LANES:]))

    def value_blocks(kb0, n):
        off = pl.multiple_of(kb0 * 256, 256)
        v1 = v1_ref[pl.ds(off, n * 256), :]
        for h in range(N_HEADS64):
            m = mp_ref[hrows[h], :]
            mm = jnp.concatenate([m, m], axis=1)
            p = [jnp.exp2(s_ref[kb0 + i, hrows[h], :] - mm).astype(BF16) for i in range(n)]
            p = p[0] if n == 1 else jnp.concatenate(p, axis=1)
            acc_ref[hrows[h], :] += jnp.dot(p, v1, preferred_element_type=F32)

    def pairs_then_rest(blocks):
        def pair(i, carry):
            blocks(2 * i, 2)
            return carry

        lax.fori_loop(0, nblk // 2, pair, 0)

        @pl.when(nblk % 2 == 1)
        def _():
            blocks(nblk - 1, 1)

    pairs_then_rest(logits_blocks)
    for h in range(N_HEADS64):
        m = jnp.max(mp_ref[hrows[h], :], axis=-1, keepdims=True)
        mp_ref[hrows[h], :] = jnp.broadcast_to(m, (tq, LANES))
    pairs_then_rest(value_blocks)

    gate = _silu(z_ref[...].astype(F32))
    for t in range(4):
        a_even = acc_ref[hrows[2 * t], :]
        a_odd = acc_ref[hrows[2 * t + 1], :]
        o_even = a_even / pltpu.roll(a_even, 64, 1)
        o_odd = pltpu.roll(a_odd, 64, 1) / a_odd
        cols = slice(t * LANES, (t + 1) * LANES)
        o_ref[:, cols] = (jnp.where(lo, o_even, o_odd) * gate[:, cols]).astype(BF16)


def _b_attn(h, mask, cfar, qg, kg, base0, base1, bsz, seq, tq=256):
    nt = seq // tq
    rows = N_HEADS64 * tq
    kern = functools.partial(_b_attn_kernel, seq=seq, tq=tq)
    full = lambda shape: pl.BlockSpec(shape, lambda b, t: (0,) * len(shape))
    return pl.pallas_call(
        kern,
        grid=(bsz, nt),
        in_specs=[pl.BlockSpec(memory_space=pltpu.SMEM),
                  pl.BlockSpec((tq, GROUP_W), lambda b, t: (b * nt + t, BLK_B_Q)),
                  pl.BlockSpec((tq, GROUP_W), lambda b, t: (b * nt + t, BLK_B_Z)),
                  pl.BlockSpec((seq, LANES), lambda b, t: (b, UNIT_B_K)),
                  pl.BlockSpec((seq, LANES), lambda b, t: (b, UNIT_B_V)),
                  pl.BlockSpec((1, seq // 256, tq, 256), lambda b, t: (b * nt + t, 0, 0, 0)),
                  full((1, GROUP_W)), full((1, LANES)), full((N_HEADS64, 512)), full((N_HEADS64, 512))],
        out_specs=pl.BlockSpec((tq, GROUP_W), lambda b, t: (b * nt + t, 0)),
        out_shape=jax.ShapeDtypeStruct((bsz * seq, GROUP_W), BF16),
        scratch_shapes=[pltpu.VMEM((seq, LANES), BF16), pltpu.VMEM((seq, LANES), BF16),
                        pltpu.VMEM((3, rows, 256), F32), pltpu.VMEM((rows, LANES), BF16),
                        pltpu.VMEM((seq // 256, rows, 256), F32),
                        pltpu.VMEM((rows, LANES), F32), pltpu.VMEM((rows, LANES), F32)],
        compiler_params=_params(2),
    )(cfar, h, h, h, h, mask, qg, kg, base0, base1)


def _rope(tile, cos, sin):
    return tile * cos + pltpu.roll(tile, 64, 1) * sin


def _c_prep_kernel(lat_ref, kr_ref, wq_ref, wkv_ref, qag_ref, kvag_ref, qg_ref, kg_ref, cos_ref, sin_ref,
                   qo_ref, ko_ref, vo_ref):
    cq = lat_ref[:, :Q_LORA].astype(F32)
    ms = jnp.mean(cq * cq, axis=-1, keepdims=True)
    cqn = (cq * lax.rsqrt(ms + EPS) * qag_ref[...]).astype(BF16)
    qpre = jnp.dot(cqn, wq_ref[...], preferred_element_type=F32)
    ckv = lat_ref[:, Q_LORA:].astype(F32)
    ms = jnp.mean(ckv * ckv, axis=-1, keepdims=True)
    ckvn = (ckv * lax.rsqrt(ms + EPS) * kvag_ref[...]).astype(BF16)
    kvpre = jnp.dot(ckvn, wkv_ref[...], preferred_element_type=F32)
    lane = lax.broadcasted_iota(jnp.int32, kr_ref.shape, 1)
    kr = jnp.where((lane % 64) < 32, kr_ref[...].astype(F32), 0.0)
    kr_ss = jnp.sum(kr * kr, axis=-1, keepdims=True)
    cos = cos_ref[...]
    sin = sin_ref[...]
    qg = qg_ref[...]
    kg = kg_ref[...]
    kr_rot = _rope(kr * kg[:, LANES:], cos, sin)
    for h in range(C_HEADS):
        qh = qpre[:, h * 256:(h + 1) * 256]
        r = lax.rsqrt(jnp.sum(qh * qh, axis=-1, keepdims=True) * (1.0 / C_QK) + EPS)
        qn = qh * r * qg
        qo_ref[:, h * 256:h * 256 + LANES] = qn[:, :LANES].astype(BF16)
        qo_ref[:, h * 256 + LANES:(h + 1) * 256] = _rope(qn[:, LANES:], cos, sin).astype(BF16)
        kn = kvpre[:, h * LANES:(h + 1) * LANES]
        r = lax.rsqrt((jnp.sum(kn * kn, axis=-1, keepdims=True) + kr_ss) * (1.0 / C_QK) + EPS)
        ko_ref[:, h * 256:h * 256 + LANES] = (kn * r * kg[:, :LANES]).astype(BF16)
        ko_ref[:, h * 256 + LANES:(h + 1) * 256] = (kr_rot * r).astype(BF16)
    vo_ref[...] = kvpre[:, C_HEADS * LANES:].astype(BF16)


def _c_prep(h, wq, wkv, qag, kvag, qg, kg, cos, sin, seq, tm=512):
    n = h.shape[0]
    ns = seq // tm
    full = lambda shape: pl.BlockSpec(shape, lambda i: (0,) * len(shape))
    return pl.pallas_call(
        _c_prep_kernel,
        grid=(n // tm,),
        in_specs=[pl.BlockSpec((tm, GROUP_W), lambda i: (i, BLK_C_QKV)),
                  pl.BlockSpec((tm, LANES), lambda i: (i, UNIT_KR_IW)),
                  full((Q_LORA, 4 * 256)), full((KV_LORA, 8 * LANES)),
                  full((1, Q_LORA)), full((1, LANES)), full((1, 256)), full((1, 256)),
                  pl.BlockSpec((tm, LANES), lambda i: (i % ns, 0)),
                  pl.BlockSpec((tm, LANES), lambda i: (i % ns, 0))],
        out_specs=[pl.BlockSpec((tm, 4 * 256), lambda i: (i, 0)),
                   pl.BlockSpec((tm, 4 * 256), lambda i: (i, 0)),
                   pl.BlockSpec((tm, GROUP_W), lambda i: (i, 0))],
        out_shape=[jax.ShapeDtypeStruct((n, 4 * 256), BF16), jax.ShapeDtypeStruct((n, 4 * 256), BF16),
                   jax.ShapeDtypeStruct((n, GROUP_W), BF16)],
        compiler_params=_params(1),
    )(h, h, wq, wkv, qag, kvag, qg, kg, cos, sin)


def _c_attn_kernel(q_ref, k_ref, v_ref, z_ref, o_ref, s_ref, mp_ref, lp_ref, acc_ref, *, tq):
    qt = pl.program_id(1)
    scale = C_QK ** -0.5 * LOG2E
    hrows = [slice(h * tq, (h + 1) * tq) for h in range(C_HEADS)]
    qchunk = (qt * tq + lax.broadcasted_iota(jnp.int32, (tq, 256), 0)) // CHUNK
    kcol = lax.broadcasted_iota(jnp.int32, (tq, 256), 1)
    mp_ref[...] = jnp.full(mp_ref.shape, NEG, F32)
    lp_ref[...] = jnp.zeros(lp_ref.shape, F32)
    acc_ref[...] = jnp.zeros(acc_ref.shape, F32)
    nfull = (qt * tq) // 256
    nblk = ((qt + 1) * tq) // 256

    def logits_blocks(kb0, n, masked):
        for i in range(n):
            kb = kb0 + i
            off = pl.multiple_of(kb * 256, 256)
            for h in range(C_HEADS):
                cols = slice(h * 256, (h + 1) * 256)
                s = _dot_t(q_ref[:, cols], k_ref[pl.ds(off, 256), cols]) * scale
                if masked:
                    s = jnp.where(((kb * 256 + kcol) // CHUNK) <= qchunk, s, NEG)
                s_ref[kb, hrows[h], :] = s
                mp_ref[hrows[h], :] = jnp.maximum(mp_ref[hrows[h], :], jnp.maximum(s[:, :LANES], s[:, LANES:]))

    def value_blocks(kb0, n):
        off = pl.multiple_of(kb0 * 256, 256)
        for h in range(C_HEADS):
            m = mp_ref[hrows[h], :]
            mm = jnp.concatenate([m, m], axis=1)
            p = [jnp.exp2(s_ref[kb0 + i, hrows[h], :] - mm) for i in range(n)]
            lsum = p[0][:, :LANES] + p[0][:, LANES:]
            for pi in p[1:]:
                lsum = lsum + pi[:, :LANES] + pi[:, LANES:]
            lp_ref[hrows[h], :] += lsum
            pb = p[0].astype(BF16) if n == 1 else jnp.concatenate([pi.astype(BF16) for pi in p], axis=1)
            acc_ref[hrows[h], :] += jnp.dot(pb, v_ref[pl.ds(off, n * 256), h * LANES:(h + 1) * LANES],
                                            preferred_element_type=F32)

    def pairs_then_rest(blocks, count):
        def pair(i, carry):
            blocks(2 * i, 2)
            return carry

        lax.fori_loop(0, count // 2, pair, 0)

        @pl.when(count % 2 == 1)
        def _():
            blocks(count - 1, 1)

    pairs_then_rest(lambda kb0, n: logits_blocks(kb0, n, False), nfull)
    lax.fori_loop(nfull, nblk, lambda kb, c: (logits_blocks(kb, 1, True), c)[1], 0)
    for h in range(C_HEADS):
        m = jnp.max(mp_ref[hrows[h], :], axis=-1, keepdims=True)
        mp_ref[hrows[h], :] = jnp.broadcast_to(m, (tq, LANES))
    pairs_then_rest(value_blocks, nblk)
    gate = _silu(z_ref[...].astype(F32))
    for h in range(C_HEADS):
        cols = slice(h * LANES, (h + 1) * LANES)
        l = jnp.sum(lp_ref[hrows[h], :], axis=-1, keepdims=True)
        o_ref[:, cols] = (acc_ref[hrows[h], :] / l * gate[:, cols]).astype(BF16)


def _c_attn(qc, kc, vc, h, bsz, seq, tq=256):
    nt = seq // tq
    rows = C_HEADS * tq
    kern = functools.partial(_c_attn_kernel, tq=tq)
    return pl.pallas_call(
        kern,
        grid=(bsz, nt),
        in_specs=[pl.BlockSpec((tq, C_HEADS * 256), lambda b, t: (b * nt + t, 0)),
                  pl.BlockSpec((seq, C_HEADS * 256), lambda b, t: (b, 0)),
                  pl.BlockSpec((seq, GROUP_W), lambda b, t: (b, 0)),
                  pl.BlockSpec((tq, GROUP_W), lambda b, t: (b * nt + t, BLK_C_Z))],
        out_specs=pl.BlockSpec((tq, GROUP_W), lambda b, t: (b * nt + t, 0)),
        out_shape=jax.ShapeDtypeStruct((bsz * seq, GROUP_W), BF16),
        scratch_shapes=[pltpu.VMEM((seq // 256, rows, 256), F32), pltpu.VMEM((rows, LANES), F32),
                        pltpu.VMEM((rows, LANES), F32), pltpu.VMEM((rows, LANES), F32)],
        compiler_params=_params(2),
    )(qc, kc, vc, h)


def _mixer_d_kernel(q_ref, k_ref, v_ref, z_ref, qg_ref, kg_ref, base_ref, o_ref,
                    kpad_ref, vpad_ref, bias_ref, s_ref, mp_ref, *, seq, tq):
    b = pl.program_id(0)
    qt = pl.program_id(1)
    win = tq + D_LEFT

    @pl.when((b == 0) & (qt == 0))
    def _():
        qc = lax.broadcasted_iota(jnp.int32, (tq, win), 0) // CHUNK
        kc = lax.broadcasted_iota(jnp.int32, (tq, win), 1) // CHUNK
        band = (kc >= qc) & (kc <= qc + D_LEFT // CHUNK)
        for h in range(N_HEADS64):
            bias_ref[h] = jnp.where(band, _toeplitz(base_ref[h:h + 1, :], tq, win) * LOG2E, NEG)

    @pl.when(qt == 0)
    def _():
        kpad_ref[0:D_LEFT, :] = jnp.zeros((D_LEFT, GROUP_W), BF16)
        vpad_ref[0:D_LEFT, :] = jnp.zeros((D_LEFT, GROUP_W), BF16)
        for r in range(seq // 256):
            rows = slice(r * 256, (r + 1) * 256)
            dst = slice(D_LEFT + r * 256, D_LEFT + (r + 1) * 256)
            tiles = _rms_heads64(k_ref[rows, :].astype(F32), kg_ref[...], 4)
            for t in range(4):
                kpad_ref[dst, t * LANES:(t + 1) * LANES] = tiles[t].astype(BF16)
            vpad_ref[dst, :] = v_ref[rows, :]

    lo = _lo_mask(tq)
    qtiles = _rms_heads64(q_ref[...].astype(F32), qg_ref[...], 4)
    start = pl.multiple_of(qt * tq, tq)
    gate = _silu(z_ref[...].astype(F32))
    ntile = win // LANES

    def lane_tiles(x):
        return [x[:, i * LANES:(i + 1) * LANES] for i in range(ntile)]

    def logits_pass(before_start):
        if before_start:
            in_seq = lax.broadcasted_iota(jnp.int32, (tq, win), 1) + start >= D_LEFT
        for t in range(4):
            kwin = kpad_ref[pl.ds(start, win), t * LANES:(t + 1) * LANES]
            for half in range(2):
                h = 2 * t + half
                sel = lo if half == 0 else jnp.logical_not(lo)
                qh = jnp.where(sel, qtiles[t] * (0.125 * LOG2E), 0.0).astype(BF16)
                s = _dot_t(qh, kwin) + bias_ref[h]
                if before_start:
                    s = jnp.where(in_seq, s, NEG)
                s_ref[h] = s
                mp_ref[h] = functools.reduce(jnp.maximum, lane_tiles(s))

    @pl.when(start < D_LEFT)
    def _():
        logits_pass(True)

    @pl.when(start >= D_LEFT)
    def _():
        logits_pass(False)

    for h in range(N_HEADS64):
        mp_ref[h] = jnp.broadcast_to(jnp.max(mp_ref[h], axis=-1, keepdims=True), (tq, LANES))
    for t in range(4):
        cols = slice(t * LANES, (t + 1) * LANES)
        vwin = vpad_ref[pl.ds(start, win), cols]
        outs = []
        for half in range(2):
            h = 2 * t + half
            m = mp_ref[h]
            p = jnp.exp2(s_ref[h] - jnp.concatenate([m] * ntile, axis=1))
            l = jnp.sum(functools.reduce(jnp.add, lane_tiles(p)), axis=-1, keepdims=True)
            outs.append(jnp.dot(p.astype(BF16), vwin, preferred_element_type=F32) / l)
        o_ref[:, cols] = (jnp.where(lo, outs[0], outs[1]) * gate[:, cols]).astype(BF16)


def _mixer_d(h, qg, kg, base, bsz, seq, tq=256):
    nt = seq // tq
    kern = functools.partial(_mixer_d_kernel, seq=seq, tq=tq)
    full = lambda shape: pl.BlockSpec(shape, lambda b, t: (0,) * len(shape))
    return pl.pallas_call(
        kern,
        grid=(bsz, nt),
        in_specs=[pl.BlockSpec((tq, GROUP_W), lambda b, t: (b * nt + t, BLK_D_Q)),
                  pl.BlockSpec((seq, GROUP_W), lambda b, t: (b, BLK_D_K)),
                  pl.BlockSpec((seq, GROUP_W), lambda b, t: (b, BLK_D_V)),
                  pl.BlockSpec((tq, GROUP_W), lambda b, t: (b * nt + t, BLK_D_Z)),
                  full((1, GROUP_W)), full((1, GROUP_W)), full((N_HEADS64, 2 * tq + D_LEFT))],
        out_specs=pl.BlockSpec((tq, GROUP_W), lambda b, t: (b * nt + t, 0)),
        out_shape=jax.ShapeDtypeStruct((bsz * seq, GROUP_W), BF16),
        scratch_shapes=[pltpu.VMEM((seq + D_LEFT, GROUP_W), BF16), pltpu.VMEM((seq + D_LEFT, GROUP_W), BF16),
                        pltpu.VMEM((N_HEADS64, tq, tq + D_LEFT), F32),
                        pltpu.VMEM((N_HEADS64, tq, tq + D_LEFT), F32), pltpu.VMEM((N_HEADS64, tq, LANES), F32)],
        compiler_params=_params(2),
    )(h, h, h, h, qg, kg, base)


def _w_in_pieces(take, zeros):
    c = lambda name, size, off=0: take(_SRC[name] + off, size)
    return [c("a_u", 512), c("a_v", 512), c("a_z", 512),
            c("b_q", 512), c("b_iq", 512), c("b_z", 512),
            c("b_k", 64), c("b_k", 64), c("b_v", 64), c("b_v", 64), c("b_ik", 64), c("b_ik", 64),
            c("c_kr", 32), c("b_iw", 8), zeros(24), c("c_kr", 32, 32), zeros(32),
            c("c_q", 384), c("c_kv", 128), c("c_z", 512),
            c("d_q", 512), c("d_k", 512), c("d_v", 512), c("d_z", 512)]


def _layout_w_in_kernel(w_ref, o_ref):
    tk = w_ref.shape[2]
    pieces = _w_in_pieces(lambda s, n: w_ref[0, s:s + n, :], lambda n: jnp.zeros((n, tk), F32))
    ends = np.cumsum([0] + [p.shape[0] for p in pieces])
    start = 0
    for i in range(1, len(pieces) + 1):
        if ends[i] % GROUP_W == 0:
            group = pieces[start:i]
            blk = group[0] if len(group) == 1 else jnp.concatenate(group, axis=0)
            o_ref[0, ends[start]:ends[i], :] = blk.astype(BF16)
            start = i


def _layout_w_in(w_in, tk=256):
    w_t = jnp.swapaxes(w_in, 1, 2)
    depth, cols, d = w_t.shape
    return pl.pallas_call(
        _layout_w_in_kernel,
        grid=(depth, d // tk),
        in_specs=[pl.BlockSpec((1, cols, tk), lambda l, i: (l, 0, i))],
        out_specs=pl.BlockSpec((1, H_COLS, tk), lambda l, i: (l, 0, i)),
        out_shape=jax.ShapeDtypeStruct((depth, H_COLS, d), BF16),
        compiler_params=_params(2),
    )(w_t)


def _rope_layout(v):
    z = jnp.zeros(v.shape[:-1] + (32,), v.dtype)
    return jnp.concatenate([v[..., :32], z, v[..., 32:], z], axis=-1)


def _layout_c(w_qb, w_kvb, q_gain, k_gain, qa_gain):
    wq = w_qb.reshape(Q_LORA, C_HEADS, C_QK)
    wq = jnp.concatenate([wq[..., :C_NOPE], _rope_layout(wq[..., C_NOPE:])], axis=-1)
    wq = wq.reshape(Q_LORA, C_HEADS * 256).astype(BF16)
    wkv = w_kvb.reshape(KV_LORA, C_HEADS, 2 * LANES)
    wkv = jnp.concatenate([wkv[..., :C_NOPE].reshape(KV_LORA, -1), wkv[..., C_NOPE:].reshape(KV_LORA, -1)],
                          axis=1).astype(BF16)
    lay = lambda g: jnp.concatenate([g[:C_NOPE], _rope_layout(g[C_NOPE:])])[None, :]
    return wq, wkv, lay(q_gain), lay(k_gain), qa_gain[None, :]


def _t5_bucket_static(rel):
    half = T5_BUCKETS // 2
    exact = half // 2
    n = abs(rel)
    if n < exact:
        val = n
    else:
        val = min(exact + (n * n // (exact * exact)).bit_length() - 1, half - 1)
    return (half if rel > 0 else 0) + val


def _t5_tables(t5_bias):
    m = np.arange(512)
    d0 = np.where(m < 256, m, m - 512)
    d1 = np.where(m <= 256, m - 256, m - 768)
    idx0 = np.array([_t5_bucket_static(int(d)) for d in d0], np.int32)
    idx1 = np.array([_t5_bucket_static(int(d)) for d in d1], np.int32)
    far = _t5_bucket_static(-512)
    return t5_bias[idx0].T, t5_bias[idx1].T, t5_bias[far]


def _band_table(rel_bias, tq):
    width = 2 * tq + D_LEFT
    m = np.arange(width)
    dist = np.where(m <= tq + D_LEFT, D_LEFT - m, D_LEFT + width - m)
    idx = np.clip(dist, -REL_CLIP, REL_CLIP) + REL_CLIP
    return rel_bias[idx.astype(np.int32)].T


def _rope_tables(seq):
    inv = ROPE_BASE ** (-jnp.arange(0, C_ROPE, 2, dtype=F32) / C_ROPE)
    ang = jnp.arange(seq, dtype=F32)[:, None] * inv[None, :]
    c, s = jnp.cos(ang), jnp.sin(ang)
    z = jnp.zeros_like(c)
    return jnp.concatenate([c, z, c, z], axis=1), jnp.concatenate([-s, z, s, z], axis=1)


def kernel(x, t5_bias, norm_g, w_in, a_v_gain, a_ws, a_bs, b_q_gain, b_k_gain, c_qa_gain, c_kva_gain,
           c_w_qb, c_w_kvb, c_q_gain, c_k_gain, d_q_gain, d_k_gain, d_rel_bias, w_out):
    bsz, seq, d_model = x.shape
    depth = w_in.shape[0]
    tq = 256
    x2 = x.reshape(bsz * seq, d_model)
    cos, sin = _rope_tables(seq)
    base0, base1, cfar = _t5_tables(t5_bias)
    w_in_blocks = _layout_w_in(w_in)
    for l in range(depth):
        h = _inproj(x2, norm_g[l][None, :], w_in_blocks, l)
        y_a = _mixer_a(h, a_v_gain[l][None, :], a_ws[l], a_bs[l][:, :, None])
        mask = _b_select(h, bsz, seq)
        y_b = _b_attn(h, mask, cfar, jnp.tile(b_q_gain[l], N_HEADS64)[None, :],
                      jnp.tile(b_k_gain[l], 2)[None, :], base0, base1, bsz, seq, tq)
        wq, wkv, qg, kg, qag = _layout_c(c_w_qb[l], c_w_kvb[l], c_q_gain[l], c_k_gain[l], c_qa_gain[l])
        qc, kc, vc = _c_prep(h, wq, wkv, qag, c_kva_gain[l][None, :], qg, kg, cos, sin, seq)
        y_c = _c_attn(qc, kc, vc, h, bsz, seq, tq)
        y_d = _mixer_d(h, jnp.tile(d_q_gain[l], N_HEADS64)[None, :], jnp.tile(d_k_gain[l], N_HEADS64)[None, :],
                       _band_table(d_rel_bias[l], tq), bsz, seq, tq)
        x2 = _outproj(x2, (y_a, y_b, y_c, y_d), w_out[l].astype(BF16))
    return x2.reshape(bsz, seq, d_model)
```

```python
import functools
import math

import numpy as np
import jax
import jax.numpy as jnp
from jax import lax
from jax.experimental import pallas as pl
from jax.experimental.pallas import tpu as pltpu

F32 = jnp.float32
BF16 = jnp.bfloat16

EPS = 1e-6
NEG = -1e30
LOG2E = math.log2(math.e)
INT_MIN = -(2 ** 31)
CHUNK = 64
LANES = 128
GROUP_W = 512
A_GROUPS = 4
GMLP_BLOCK = 128
N_HEADS64 = 8
IDX_SCALE = (8 ** -0.5) * 0.125
TOPK_MAX = 256
T5_BUCKETS = 32
C_HEADS = 4
C_NOPE = 128
C_ROPE = 64
C_QK = 192
Q_LORA = 384
KV_LORA = 128
ROPE_BASE = 10000.0
D_LEFT = 8 * CHUNK
REL_CLIP = 128
VMEM_LIMIT = 56 * 1024 * 1024

BLK_A_U, BLK_A_V, BLK_A_Z = 0, 1, 2
BLK_B_Q, BLK_B_IQ, BLK_B_Z, BLK_SMALL = 3, 4, 5, 6
BLK_C_QKV, BLK_C_Z = 7, 8
BLK_D_Q, BLK_D_K, BLK_D_V, BLK_D_Z = 9, 10, 11, 12
H_COLS = 13 * GROUP_W
UNIT_B_K, UNIT_B_V, UNIT_B_IK, UNIT_KR_IW = (BLK_SMALL * 4 + i for i in range(4))
IW_LANE = 32

_SRC = dict(a_u=0, a_v=512, a_z=1024, b_q=1536, b_k=2048, b_v=2112, b_iq=2176, b_ik=2688,
            b_iw=2752, b_z=2760, c_q=3272, c_kv=3656, c_kr=3784, c_z=3848,
            d_q=4360, d_k=4872, d_v=5384, d_z=5896)


def _params(n_axes):
    return pltpu.CompilerParams(dimension_semantics=("arbitrary",) * n_axes,
                                vmem_limit_bytes=VMEM_LIMIT)


def _gelu(x):
    c = math.sqrt(2.0 / math.pi)
    return x * (0.5 * (1.0 + jnp.tanh(c * (x + 0.044715 * (x * x * x)))))


def _silu(x):
    return x * (1.0 / (1.0 + jnp.exp(-x)))


def _dot_t(a, b):
    return lax.dot_general(a, b, (((1,), (1,)), ((), ())), preferred_element_type=F32)


def _lo_mask(rows):
    return lax.broadcasted_iota(jnp.int32, (rows, LANES), 1) < 64


def _rms_heads64(x, gain, ntiles):
    lo = _lo_mask(x.shape[0])
    tiles = []
    for t in range(ntiles):
        xt = x[:, t * LANES:(t + 1) * LANES]
        sq = xt * xt
        s_lo = jnp.sum(jnp.where(lo, sq, 0.0), axis=-1, keepdims=True)
        s_hi = jnp.sum(jnp.where(lo, 0.0, sq), axis=-1, keepdims=True)
        r = jnp.where(lo, lax.rsqrt(s_lo * (1.0 / 64) + EPS), lax.rsqrt(s_hi * (1.0 / 64) + EPS))
        tiles.append(xt * r * gain[:, t * LANES:(t + 1) * LANES])
    return tiles


def _for_blocks(blocks, count):
    def quad(i, carry):
        blocks(4 * i, 4)
        return carry

    lax.fori_loop(0, count // 4, quad, 0)
    first = (count // 4) * 4

    @pl.when((count & 2) != 0)
    def _():
        blocks(first, 2)

    @pl.when((count & 1) != 0)
    def _():
        blocks(first + (count & 2), 1)


def _toeplitz(base_row, rows, width):
    t = jnp.broadcast_to(base_row, (rows, base_row.shape[1]))
    t = pltpu.roll(t, 0, 1, stride=1, stride_axis=0)
    return t[:, :width]


def _inproj_kernel(x_ref, g_ref, w_ref, o_ref):
    x = x_ref[...]
    ms = jnp.mean(x * x, axis=-1, keepdims=True)
    xn = (x * lax.rsqrt(ms + EPS) * g_ref[...]).astype(BF16)
    for c in range(H_COLS // GROUP_W):
        cols = slice(c * GROUP_W, (c + 1) * GROUP_W)
        o_ref[:, cols] = _dot_t(xn, w_ref[cols, :]).astype(BF16)


def _inproj(x2, g, w_all, layer, tm=512):
    n, d = x2.shape
    return pl.pallas_call(
        _inproj_kernel,
        grid=(n // tm,),
        in_specs=[pl.BlockSpec((tm, d), lambda i: (i, 0)),
                  pl.BlockSpec((1, d), lambda i: (0, 0)),
                  pl.BlockSpec((None, H_COLS, d), lambda i: (layer, 0, 0), pipeline_mode=pl.Buffered(1))],
        out_specs=pl.BlockSpec((tm, H_COLS), lambda i: (i, 0)),
        out_shape=jax.ShapeDtypeStruct((n, H_COLS), BF16),
        compiler_params=_params(1),
    )(x2, g, w_all)


def _outproj_kernel(x_ref, ya_ref, yb_ref, yc_ref, yd_ref, w_ref, o_ref):
    acc = x_ref[...]
    for g, y_ref in enumerate((ya_ref, yb_ref, yc_ref, yd_ref)):
        acc = acc + jnp.dot(y_ref[...], w_ref[g * GROUP_W:(g + 1) * GROUP_W, :],
                            preferred_element_type=F32)
    o_ref[...] = acc


def _outproj(x2, ys, w, tm=512):
    n, d = x2.shape
    yspec = pl.BlockSpec((tm, GROUP_W), lambda i: (i, 0))
    return pl.pallas_call(
        _outproj_kernel,
        grid=(n // tm,),
        in_specs=[pl.BlockSpec((tm, d), lambda i: (i, 0)), yspec, yspec, yspec, yspec,
                  pl.BlockSpec((4 * GROUP_W, d), lambda i: (0, 0))],
        out_specs=pl.BlockSpec((tm, d), lambda i: (i, 0)),
        out_shape=jax.ShapeDtypeStruct((n, d), F32),
        compiler_params=_params(1),
    )(x2, *ys, w)


def _mixer_a_kernel(u_ref, v_ref, z_ref, vg_ref, w_ref, b_ref, o_ref):
    tm = u_ref.shape[0]
    u = _gelu(u_ref[...].astype(F32))
    v = _gelu(v_ref[...].astype(F32))
    ms = jnp.mean(v * v, axis=-1, keepdims=True)
    vb = (v * lax.rsqrt(ms + EPS) * vg_ref[...]).astype(BF16)
    gate = _silu(z_ref[...].astype(F32))
    i = lax.broadcasted_iota(jnp.int32, (GMLP_BLOCK, GMLP_BLOCK), 0)
    j = lax.broadcasted_iota(jnp.int32, (GMLP_BLOCK, GMLP_BLOCK), 1)
    keep = (j // CHUNK) <= (i // CHUNK)
    for g in range(A_GROUPS):
        wg = jnp.where(keep, w_ref[g], 0.0).astype(BF16)
        cols = slice(g * LANES, (g + 1) * LANES)
        for blk in range(tm // GMLP_BLOCK):
            rows = slice(blk * GMLP_BLOCK, (blk + 1) * GMLP_BLOCK)
            sg = jnp.dot(wg, vb[rows, cols], preferred_element_type=F32) + b_ref[g]
            o_ref[rows, cols] = (u[rows, cols] * sg * gate[rows, cols]).astype(BF16)


def _mixer_a(h, vg, ws, bs, tm=512):
    n = h.shape[0]
    hspec = lambda blk: pl.BlockSpec((tm, GROUP_W), lambda i, blk=blk: (i, blk))
    return pl.pallas_call(
        _mixer_a_kernel,
        grid=(n // tm,),
        in_specs=[hspec(BLK_A_U), hspec(BLK_A_V), hspec(BLK_A_Z),
                  pl.BlockSpec((1, GROUP_W), lambda i: (0, 0)),
                  pl.BlockSpec((A_GROUPS, GMLP_BLOCK, GMLP_BLOCK), lambda i: (0, 0, 0)),
                  pl.BlockSpec((A_GROUPS, GMLP_BLOCK, 1), lambda i: (0, 0, 0))],
        out_specs=pl.BlockSpec((tm, GROUP_W), lambda i: (i, 0)),
        out_shape=jax.ShapeDtypeStruct((n, GROUP_W), BF16),
        compiler_params=_params(1),
    )(h, h, h, vg, ws, bs)


def _order_key(x):
    return jnp.where(x < 0, x ^ 0x7FFFFFFF, x)


def _b_select_kernel(iq_ref, iw_ref, ik_ref, o_ref, lhs_ref, key_ref, *, seq, tq, topk):
    t_blk = pl.program_id(1)
    nkb = seq // 256
    nblk = ((t_blk + 1) * tq) // 256
    n_interp = 12
    n_unchecked = 12
    hrows = [slice(h * tq, (h + 1) * tq) for h in range(N_HEADS64)]

    lo_half = _lo_mask(tq)
    w_t = (iw_ref[...].astype(F32) * IDX_SCALE).T
    for h in range(N_HEADS64):
        iqt = iq_ref[:, (h // 2) * LANES:(h // 2 + 1) * LANES]
        sel = lo_half if h % 2 == 0 else jnp.logical_not(lo_half)
        lhs_ref[hrows[h], :] = jnp.where(sel, iqt, jnp.zeros_like(iqt))

    qpos = t_blk * tq + lax.broadcasted_iota(jnp.int32, (256, tq), 1)
    krow = lax.broadcasted_iota(jnp.int32, (256, tq), 0)

    def fold8(x):
        return jnp.sum(x.reshape(256 // 8, 8, tq), axis=0)

    def score_block(kb, carry, masked):
        amax, n_pos, n_nn = carry
        off = pl.multiple_of(kb * 256, 256)
        ikblk = ik_ref[pl.ds(off, 256), :]
        score = jnp.zeros((256, tq), F32)
        for h in range(N_HEADS64):
            w_h = w_t[IW_LANE + h:IW_LANE + h + 1, :]
            score = score + w_h * jnp.maximum(_dot_t(ikblk, lhs_ref[hrows[h], :]), 0.0)
        score = jnp.where(score == 0.0, 0.0, score)
        key = _order_key(lax.bitcast_convert_type(score, jnp.int32))
        if masked:
            key = jnp.where(((kb * 256 + krow) // CHUNK) <= (qpos // CHUNK), key, INT_MIN)
        key_ref[kb] = key
        return (jnp.maximum(amax, jnp.max(jnp.abs(score).reshape(256 // 8, 8, tq), axis=0)),
                n_pos + fold8(jnp.where(key >= 1, 1.0, 0.0)), n_nn + fold8(jnp.where(key >= 0, 1.0, 0.0)))

    nfull = (t_blk * tq) // 256
    zeros8 = jnp.zeros((8, tq), F32)
    carry = lax.fori_loop(0, nfull, lambda kb, c: score_block(kb, c, False), (zeros8, zeros8, zeros8))
    carry = lax.fori_loop(nfull, nblk, lambda kb, c: score_block(kb, c, True), carry)
    amax = jnp.max(carry[0], axis=0, keepdims=True)
    f_pos = jnp.sum(carry[1], axis=0, keepdims=True)
    f_nn = jnp.sum(carry[2], axis=0, keepdims=True)

    def count(pred):
        def body(kb, acc):
            return acc + fold8(jnp.where(pred(key_ref[kb], kb), 1.0, 0.0))
        return jnp.sum(lax.fori_loop(0, nblk, body, zeros8), axis=0, keepdims=True)

    kf = float(topk)
    qrow = t_blk * tq + lax.broadcasted_iota(jnp.int32, (1, tq), 1)
    n_adm = ((qrow // CHUNK + 1) * CHUNK).astype(F32)
    one = jnp.ones((1, tq), jnp.int32)
    pos = f_pos > kf
    neg = f_nn < kf
    lo0 = jnp.where(pos, one, _order_key(lax.bitcast_convert_type(-amax, jnp.int32)))
    hi0 = jnp.where(neg, one - 1, _order_key(lax.bitcast_convert_type(amax, jnp.int32)) + 1)
    w_lo0 = jnp.where(pos, f_pos, n_adm) - kf
    w_hi0 = kf - jnp.where(neg, f_nn, 0.0)
    all_sel = n_adm <= kf
    at_zero = jnp.logical_not(pos | neg)
    done0 = jnp.where(all_sel | at_zero | (hi0 == lo0 + 1), 1.0, 0.0)
    thr0 = jnp.where(all_sel, INT_MIN + 1, jnp.where(at_zero, jnp.where(f_pos == kf, one, one - 1), lo0))

    def search_cond(st):
        return jnp.logical_and(st[0][0] < n_interp + 32, st[1] < 0.5)

    def search_step(st):
        it, lo, hi, w_lo, w_hi, side, done, thr = st
        lo_v = lax.bitcast_convert_type(_order_key(lo), F32)
        hi_v = lax.bitcast_convert_type(_order_key(hi), F32)
        c_v = lo_v + (hi_v - lo_v) * (w_lo / (w_lo + w_hi))
        c_interp = _order_key(lax.bitcast_convert_type(c_v, jnp.int32))
        c_mid = (lo >> 1) + (hi >> 1) + (lo & hi & 1)
        cand = jnp.where(it < n_interp, c_interp, c_mid)
        cand = jnp.minimum(jnp.maximum(cand, lo + 1), hi - 1)
        f = count(lambda k, kb: k >= cand)
        live = done < 0.5
        up = f > kf
        hit = f == kf
        new_lo = jnp.where(live & up, cand, lo)
        new_hi = jnp.where(live & jnp.logical_not(up), cand, hi)
        new_w_lo = jnp.where(up, f - kf, jnp.where(side < 0.0, 0.5 * w_lo, w_lo))
        new_w_hi = jnp.where(up, jnp.where(side > 0.0, 0.5 * w_hi, w_hi), kf - f)
        new_side = jnp.where(up, 1.0, -1.0)
        new_thr = jnp.where(live, jnp.where(hit, cand, new_lo), thr)
        new_done = jnp.where(live & (hit | (new_hi == new_lo + 1)), 1.0, done)
        return (it + 1, new_lo, new_hi, jnp.where(live, new_w_lo, w_lo),
                jnp.where(live, new_w_hi, w_hi), jnp.where(live, new_side, side), new_done, new_thr)

    def checked_step(st):
        new = search_step(st[0])
        return new, jnp.min(new[-2])

    state = (jnp.int32(0), lo0, hi0, w_lo0, w_hi0, jnp.zeros((1, tq), F32), done0, thr0)
    state = lax.fori_loop(0, n_unchecked, lambda i, st: search_step(st), state)
    thr = lax.while_loop(search_cond, checked_step, (state, jnp.min(state[-2])))[0][-1]

    any_excess = jnp.max(count(lambda k, kb: k >= thr)) > kf

    def store_mask(kb, keep_t):
        for g in range(tq // 256):
            o_ref[g, kb] = keep_t[:, g * 256:(g + 1) * 256].T

    def write_unused(kb, carry):
        store_mask(kb, jnp.zeros((256, tq), F32))
        return carry

    lax.fori_loop(nblk, nkb, write_unused, 0)

    @pl.when(jnp.logical_not(any_excess))
    def _():
        def write(kb, carry):
            store_mask(kb, jnp.where(key_ref[kb] >= thr, 1.0, 0.0))
            return carry

        lax.fori_loop(0, nblk, write, 0)

    @pl.when(any_excess)
    def _():
        need = kf - count(lambda k, kb: k > thr)

        def idx_step(it, jmax):
            cand = jmax | lax.shift_left(jnp.int32(1), 10 - it)
            below = count(lambda k, kb: (k == thr) & ((kb * 256 + krow) < cand))
            return jnp.where(below < need, cand, jmax)

        jmax = lax.fori_loop(0, 11, idx_step, jnp.zeros((1, tq), jnp.int32))

        def write(kb, carry):
            k = key_ref[kb]
            keep_tie = (k == thr) & ((kb * 256 + krow) <= jmax)
            store_mask(kb, jnp.where((k > thr) | keep_tie, 1.0, 0.0))
            return carry

        lax.fori_loop(0, nblk, write, 0)


def _b_select(h, bsz, seq, tq=512):
    tq = min(tq, seq)
    nt = seq // tq
    topk = min(TOPK_MAX, seq // 4)
    kern = functools.partial(_b_select_kernel, seq=seq, tq=tq, topk=topk)
    return pl.pallas_call(
        kern,
        grid=(bsz, nt),
        in_specs=[pl.BlockSpec((tq, GROUP_W), lambda b, t: (b * nt + t, BLK_B_IQ)),
                  pl.BlockSpec((tq, LANES), lambda b, t: (b * nt + t, UNIT_KR_IW)),
                  pl.BlockSpec((seq, LANES), lambda b, t: (b, UNIT_B_IK))],
        out_specs=pl.BlockSpec((tq // 256, seq // 256, 256, 256), lambda b, t: (b * nt + t, 0, 0, 0)),
        out_shape=jax.ShapeDtypeStruct((bsz * seq // 256, seq // 256, 256, 256), F32),
        scratch_shapes=[pltpu.VMEM((N_HEADS64 * tq, LANES), BF16),
                        pltpu.VMEM((seq // 256, 256, tq), jnp.int32)],
        compiler_params=_params(2),
    )(h, h, h)


def _b_attn_kernel(cfar_ref, q_ref, z_ref, k_ref, v_ref, msk_ref, qg_ref, kg_ref, base0_ref, base1_ref,
                   o_ref, kn_ref, v1_ref, bias_ref, qall_ref, s_ref, mp_ref, acc_ref, *, seq, tq):
    b = pl.program_id(0)
    t_blk = pl.program_id(1)
    hrows = [slice(h * tq, (h + 1) * tq) for h in range(N_HEADS64)]

    @pl.when((b == 0) & (t_blk == 0))
    def _():
        for h in range(N_HEADS64):
            bias_ref[0, hrows[h], :] = jnp.full((tq, 256), cfar_ref[h] * LOG2E, F32)
            bias_ref[1, hrows[h], :] = _toeplitz(base1_ref[h:h + 1, :], tq, 256) * LOG2E
            bias_ref[2, hrows[h], :] = _toeplitz(base0_ref[h:h + 1, :], tq, 256) * LOG2E

    @pl.when(t_blk == 0)
    def _():
        lo256 = _lo_mask(256)
        for r in range(seq // 256):
            rows = slice(r * 256, (r + 1) * 256)
            k = k_ref[rows, :].astype(F32)
            ms = jnp.mean(k * k, axis=-1, keepdims=True)
            kn_ref[rows, :] = (k * lax.rsqrt(ms + EPS) * kg_ref[...]).astype(BF16)
            v = v_ref[rows, :]
            v1_ref[rows, :] = jnp.where(lo256, v, jnp.ones_like(v))

    lo = _lo_mask(tq)
    qtiles = _rms_heads64(q_ref[...].astype(F32), qg_ref[...], 4)
    for h in range(N_HEADS64):
        sel = lo if h % 2 == 0 else jnp.logical_not(lo)
        qall_ref[hrows[h], :] = jnp.where(sel, qtiles[h // 2] * (0.125 * LOG2E), 0.0).astype(BF16)
    mp_ref[...] = jnp.full(mp_ref.shape, NEG, F32)
    acc_ref[...] = jnp.zeros(acc_ref.shape, F32)
    nblk = t_blk + 1

    def logits_blocks(kb0, n):
        for i in range(n):
            kb = kb0 + i
            off = pl.multiple_of(kb * 256, 256)
            kblk = kn_ref[pl.ds(off, 256), :]
            which = jnp.clip(kb - (t_blk - 2), 0, 2)
            keep = msk_ref[0, kb] > 0.5
            for h in range(N_HEADS64):
                s = jnp.where(keep, _dot_t(qall_ref[hrows[h], :], kblk) + bias_ref[which, hrows[h], :], NEG)
                s_ref[kb, hrows[h], :] = s
                mp_ref[hrows[h], :] = jnp.maximum(mp_ref[hrows[h], :], jnp.maximum(s[:, :LANES], s[:, LANES:]))

    def value_blocks(kb0, n):
        off = pl.multiple_of(kb0 * 256, 256)
        v1 = v1_ref[pl.ds(off, n * 256), :]
        for h in range(N_HEADS64):
            m = mp_ref[hrows[h], :]
            mm = jnp.concatenate([m, m], axis=1)
            p = [jnp.exp2(s_ref[kb0 + i, hrows[h], :] - mm).astype(BF16) for i in range(n)]
            p = p[0] if n == 1 else jnp.concatenate(p, axis=1)
            acc_ref[hrows[h], :] += jnp.dot(p, v1, preferred_element_type=F32)

    _for_blocks(logits_blocks, nblk)
    for h in range(N_HEADS64):
        m = jnp.max(mp_ref[hrows[h], :], axis=-1, keepdims=True)
        mp_ref[hrows[h], :] = jnp.broadcast_to(m, (tq, LANES))
    _for_blocks(value_blocks, nblk)

    gate = _silu(z_ref[...].astype(F32))
    for t in range(4):
        a_even = acc_ref[hrows[2 * t], :]
        a_odd = acc_ref[hrows[2 * t + 1], :]
        o_even = a_even / pltpu.roll(a_even, 64, 1)
        o_odd = pltpu.roll(a_odd, 64, 1) / a_odd
        cols = slice(t * LANES, (t + 1) * LANES)
        o_ref[:, cols] = (jnp.where(lo, o_even, o_odd) * gate[:, cols]).astype(BF16)


def _b_attn(h, mask, cfar, qg, kg, base0, base1, bsz, seq, tq=256):
    nt = seq // tq
    rows = N_HEADS64 * tq
    kern = functools.partial(_b_attn_kernel, seq=seq, tq=tq)
    full = lambda shape: pl.BlockSpec(shape, lambda b, t: (0,) * len(shape))
    return pl.pallas_call(
        kern,
        grid=(bsz, nt),
        in_specs=[pl.BlockSpec(memory_space=pltpu.SMEM),
                  pl.BlockSpec((tq, GROUP_W), lambda b, t: (b * nt + t, BLK_B_Q)),
                  pl.BlockSpec((tq, GROUP_W), lambda b, t: (b * nt + t, BLK_B_Z)),
                  pl.BlockSpec((seq, LANES), lambda b, t: (b, UNIT_B_K)),
                  pl.BlockSpec((seq, LANES), lambda b, t: (b, UNIT_B_V)),
                  pl.BlockSpec((1, seq // 256, tq, 256), lambda b, t: (b * nt + t, 0, 0, 0)),
                  full((1, GROUP_W)), full((1, LANES)), full((N_HEADS64, 512)), full((N_HEADS64, 512))],
        out_specs=pl.BlockSpec((tq, GROUP_W), lambda b, t: (b * nt + t, 0)),
        out_shape=jax.ShapeDtypeStruct((bsz * seq, GROUP_W), BF16),
        scratch_shapes=[pltpu.VMEM((seq, LANES), BF16), pltpu.VMEM((seq, LANES), BF16),
                        pltpu.VMEM((3, rows, 256), F32), pltpu.VMEM((rows, LANES), BF16),
                        pltpu.VMEM((seq // 256, rows, 256), F32),
                        pltpu.VMEM((rows, LANES), F32), pltpu.VMEM((rows, LANES), F32)],
        compiler_params=_params(2),
    )(cfar, h, h, h, h, mask, qg, kg, base0, base1)


def _rope(tile, cos, sin):
    return tile * cos + pltpu.roll(tile, 64, 1) * sin


def _c_prep_kernel(lat_ref, kr_ref, wq_ref, wkv_ref, qag_ref, kvag_ref, qg_ref, kg_ref, cos_ref, sin_ref,
                   qo_ref, ko_ref, vo_ref):
    cq = lat_ref[:, :Q_LORA].astype(F32)
    ms = jnp.mean(cq * cq, axis=-1, keepdims=True)
    cqn = (cq * lax.rsqrt(ms + EPS) * qag_ref[...]).astype(BF16)
    qpre = jnp.dot(cqn, wq_ref[...], preferred_element_type=F32)
    ckv = lat_ref[:, Q_LORA:].astype(F32)
    ms = jnp.mean(ckv * ckv, axis=-1, keepdims=True)
    ckvn = (ckv * lax.rsqrt(ms + EPS) * kvag_ref[...]).astype(BF16)
    kvpre = jnp.dot(ckvn, wkv_ref[...], preferred_element_type=F32)
    lane = lax.broadcasted_iota(jnp.int32, kr_ref.shape, 1)
    kr = jnp.where((lane % 64) < 32, kr_ref[...].astype(F32), 0.0)
    kr_ss = jnp.sum(kr * kr, axis=-1, keepdims=True)
    cos = cos_ref[...]
    sin = sin_ref[...]
    qg = qg_ref[...]
    kg = kg_ref[...]
    kr_rot = _rope(kr * kg[:, LANES:], cos, sin)
    for h in range(C_HEADS):
        qh = qpre[:, h * 256:(h + 1) * 256]
        r = lax.rsqrt(jnp.sum(qh * qh, axis=-1, keepdims=True) * (1.0 / C_QK) + EPS)
        qn = qh * r * qg
        qo_ref[:, h * 256:h * 256 + LANES] = qn[:, :LANES].astype(BF16)
        qo_ref[:, h * 256 + LANES:(h + 1) * 256] = _rope(qn[:, LANES:], cos, sin).astype(BF16)
        kn = kvpre[:, h * LANES:(h + 1) * LANES]
        r = lax.rsqrt((jnp.sum(kn * kn, axis=-1, keepdims=True) + kr_ss) * (1.0 / C_QK) + EPS)
        ko_ref[:, h * 256:h * 256 + LANES] = (kn * r * kg[:, :LANES]).astype(BF16)
        ko_ref[:, h * 256 + LANES:(h + 1) * 256] = (kr_rot * r).astype(BF16)
    vo_ref[...] = kvpre[:, C_HEADS * LANES:].astype(BF16)


def _c_prep(h, wq, wkv, qag, kvag, qg, kg, cos, sin, seq, tm=512):
    n = h.shape[0]
    ns = seq // tm
    full = lambda shape: pl.BlockSpec(shape, lambda i: (0,) * len(shape))
    return pl.pallas_call(
        _c_prep_kernel,
        grid=(n // tm,),
        in_specs=[pl.BlockSpec((tm, GROUP_W), lambda i: (i, BLK_C_QKV)),
                  pl.BlockSpec((tm, LANES), lambda i: (i, UNIT_KR_IW)),
                  full((Q_LORA, 4 * 256)), full((KV_LORA, 8 * LANES)),
                  full((1, Q_LORA)), full((1, LANES)), full((1, 256)), full((1, 256)),
                  pl.BlockSpec((tm, LANES), lambda i: (i % ns, 0)),
                  pl.BlockSpec((tm, LANES), lambda i: (i % ns, 0))],
        out_specs=[pl.BlockSpec((tm, 4 * 256), lambda i: (i, 0)),
                   pl.BlockSpec((tm, 4 * 256), lambda i: (i, 0)),
                   pl.BlockSpec((tm, GROUP_W), lambda i: (i, 0))],
        out_shape=[jax.ShapeDtypeStruct((n, 4 * 256), BF16), jax.ShapeDtypeStruct((n, 4 * 256), BF16),
                   jax.ShapeDtypeStruct((n, GROUP_W), BF16)],
        compiler_params=_params(1),
    )(h, h, wq, wkv, qag, kvag, qg, kg, cos, sin)


def _c_attn_kernel(q_ref, k_ref, v_ref, z_ref, o_ref, s_ref, mp_ref, lp_ref, acc_ref, *, tq):
    qt = pl.program_id(1)
    scale = C_QK ** -0.5 * LOG2E
    hrows = [slice(h * tq, (h + 1) * tq) for h in range(C_HEADS)]
    qchunk = (qt * tq + lax.broadcasted_iota(jnp.int32, (tq, 256), 0)) // CHUNK
    kcol = lax.broadcasted_iota(jnp.int32, (tq, 256), 1)
    mp_ref[...] = jnp.full(mp_ref.shape, NEG, F32)
    lp_ref[...] = jnp.zeros(lp_ref.shape, F32)
    acc_ref[...] = jnp.zeros(acc_ref.shape, F32)
    nfull = (qt * tq) // 256
    nblk = ((qt + 1) * tq) // 256

    def logits_blocks(kb0, n, masked):
        for i in range(n):
            kb = kb0 + i
            off = pl.multiple_of(kb * 256, 256)
            for h in range(C_HEADS):
                cols = slice(h * 256, (h + 1) * 256)
                s = _dot_t(q_ref[:, cols], k_ref[pl.ds(off, 256), cols]) * scale
                if masked:
                    s = jnp.where(((kb * 256 + kcol) // CHUNK) <= qchunk, s, NEG)
                s_ref[kb, hrows[h], :] = s
                mp_ref[hrows[h], :] = jnp.maximum(mp_ref[hrows[h], :], jnp.maximum(s[:, :LANES], s[:, LANES:]))

    def value_blocks(kb0, n):
        off = pl.multiple_of(kb0 * 256, 256)
        for h in range(C_HEADS):
            m = mp_ref[hrows[h], :]
            mm = jnp.concatenate([m, m], axis=1)
            p = [jnp.exp2(s_ref[kb0 + i, hrows[h], :] - mm) for i in range(n)]
            lsum = p[0][:, :LANES] + p[0][:, LANES:]
            for pi in p[1:]:
                lsum = lsum + pi[:, :LANES] + pi[:, LANES:]
            lp_ref[hrows[h], :] += lsum
            pb = p[0].astype(BF16) if n == 1 else jnp.concatenate([pi.astype(BF16) for pi in p], axis=1)
            acc_ref[hrows[h], :] += jnp.dot(pb, v_ref[pl.ds(off, n * 256), h * LANES:(h + 1) * LANES],
                                            preferred_element_type=F32)

    _for_blocks(lambda kb0, n: logits_blocks(kb0, n, False), nfull)
    lax.fori_loop(nfull, nblk, lambda kb, c: (logits_blocks(kb, 1, True), c)[1], 0)
    for h in range(C_HEADS):
        m = jnp.max(mp_ref[hrows[h], :], axis=-1, keepdims=True)
        mp_ref[hrows[h], :] = jnp.broadcast_to(m, (tq, LANES))
    _for_blocks(value_blocks, nblk)
    gate = _silu(z_ref[...].astype(F32))
    for h in range(C_HEADS):
        cols = slice(h * LANES, (h + 1) * LANES)
        l = jnp.sum(lp_ref[hrows[h], :], axis=-1, keepdims=True)
        o_ref[:, cols] = (acc_ref[hrows[h], :] / l * gate[:, cols]).astype(BF16)


def _c_attn(qc, kc, vc, h, bsz, seq, tq=256):
    nt = seq // tq
    rows = C_HEADS * tq
    kern = functools.partial(_c_attn_kernel, tq=tq)
    return pl.pallas_call(
        kern,
        grid=(bsz, nt),
        in_specs=[pl.BlockSpec((tq, C_HEADS * 256), lambda b, t: (b * nt + t, 0)),
                  pl.BlockSpec((seq, C_HEADS * 256), lambda b, t: (b, 0)),
                  pl.BlockSpec((seq, GROUP_W), lambda b, t: (b, 0)),
                  pl.BlockSpec((tq, GROUP_W), lambda b, t: (b * nt + t, BLK_C_Z))],
        out_specs=pl.BlockSpec((tq, GROUP_W), lambda b, t: (b * nt + t, 0)),
        out_shape=jax.ShapeDtypeStruct((bsz * seq, GROUP_W), BF16),
        scratch_shapes=[pltpu.VMEM((seq // 256, rows, 256), F32), pltpu.VMEM((rows, LANES), F32),
                        pltpu.VMEM((rows, LANES), F32), pltpu.VMEM((rows, LANES), F32)],
        compiler_params=_params(2),
    )(qc, kc, vc, h)


def _mixer_d_kernel(q_ref, k_ref, v_ref, z_ref, qg_ref, kg_ref, base_ref, o_ref,
                    kpad_ref, vpad_ref, bias_ref, s_ref, mp_ref, *, seq, tq):
    b = pl.program_id(0)
    qt = pl.program_id(1)
    win = tq + D_LEFT

    @pl.when((b == 0) & (qt == 0))
    def _():
        qc = lax.broadcasted_iota(jnp.int32, (tq, win), 0) // CHUNK
        kc = lax.broadcasted_iota(jnp.int32, (tq, win), 1) // CHUNK
        band = (kc >= qc) & (kc <= qc + D_LEFT // CHUNK)
        for h in range(N_HEADS64):
            bias_ref[h] = jnp.where(band, _toeplitz(base_ref[h:h + 1, :], tq, win) * LOG2E, NEG)

    @pl.when(qt == 0)
    def _():
        kpad_ref[0:D_LEFT, :] = jnp.zeros((D_LEFT, GROUP_W), BF16)
        vpad_ref[0:D_LEFT, :] = jnp.zeros((D_LEFT, GROUP_W), BF16)
        for r in range(seq // 256):
            rows = slice(r * 256, (r + 1) * 256)
            dst = slice(D_LEFT + r * 256, D_LEFT + (r + 1) * 256)
            tiles = _rms_heads64(k_ref[rows, :].astype(F32), kg_ref[...], 4)
            for t in range(4):
                kpad_ref[dst, t * LANES:(t + 1) * LANES] = tiles[t].astype(BF16)
            vpad_ref[dst, :] = v_ref[rows, :]

    lo = _lo_mask(tq)
    qtiles = _rms_heads64(q_ref[...].astype(F32), qg_ref[...], 4)
    start = pl.multiple_of(qt * tq, tq)
    gate = _silu(z_ref[...].astype(F32))
    ntile = win // LANES

    def lane_tiles(x):
        return [x[:, i * LANES:(i + 1) * LANES] for i in range(ntile)]

    def logits_pass(before_start):
        if before_start:
            in_seq = lax.broadcasted_iota(jnp.int32, (tq, win), 1) + start >= D_LEFT
        for t in range(4):
            kwin = kpad_ref[pl.ds(start, win), t * LANES:(t + 1) * LANES]
            for half in range(2):
                h = 2 * t + half
                sel = lo if half == 0 else jnp.logical_not(lo)
                qh = jnp.where(sel, qtiles[t] * (0.125 * LOG2E), 0.0).astype(BF16)
                s = _dot_t(qh, kwin) + bias_ref[h]
                if before_start:
                    s = jnp.where(in_seq, s, NEG)
                s_ref[h] = s
                mp_ref[h] = functools.reduce(jnp.maximum, lane_tiles(s))

    @pl.when(start < D_LEFT)
    def _():
        logits_pass(True)

    @pl.when(start >= D_LEFT)
    def _():
        logits_pass(False)

    for h in range(N_HEADS64):
        mp_ref[h] = jnp.broadcast_to(jnp.max(mp_ref[h], axis=-1, keepdims=True), (tq, LANES))
    for t in range(4):
        cols = slice(t * LANES, (t + 1) * LANES)
        vwin = vpad_ref[pl.ds(start, win), cols]
        outs = []
        for half in range(2):
            h = 2 * t + half
            m = mp_ref[h]
            p = jnp.exp2(s_ref[h] - jnp.concatenate([m] * ntile, axis=1))
            l = jnp.sum(functools.reduce(jnp.add, lane_tiles(p)), axis=-1, keepdims=True)
            outs.append(jnp.dot(p.astype(BF16), vwin, preferred_element_type=F32) / l)
        o_ref[:, cols] = (jnp.where(lo, outs[0], outs[1]) * gate[:, cols]).astype(BF16)


def _mixer_d(h, qg, kg, base, bsz, seq, tq=256):
    nt = seq // tq
    kern = functools.partial(_mixer_d_kernel, seq=seq, tq=tq)
    full = lambda shape: pl.BlockSpec(shape, lambda b, t: (0,) * len(shape))
    return pl.pallas_call(
        kern,
        grid=(bsz, nt),
        in_specs=[pl.BlockSpec((tq, GROUP_W), lambda b, t: (b * nt + t, BLK_D_Q)),
                  pl.BlockSpec((seq, GROUP_W), lambda b, t: (b, BLK_D_K)),
                  pl.BlockSpec((seq, GROUP_W), lambda b, t: (b, BLK_D_V)),
                  pl.BlockSpec((tq, GROUP_W), lambda b, t: (b * nt + t, BLK_D_Z)),
                  full((1, GROUP_W)), full((1, GROUP_W)), full((N_HEADS64, 2 * tq + D_LEFT))],
        out_specs=pl.BlockSpec((tq, GROUP_W), lambda b, t: (b * nt + t, 0)),
        out_shape=jax.ShapeDtypeStruct((bsz * seq, GROUP_W), BF16),
        scratch_shapes=[pltpu.VMEM((seq + D_LEFT, GROUP_W), BF16), pltpu.VMEM((seq + D_LEFT, GROUP_W), BF16),
                        pltpu.VMEM((N_HEADS64, tq, tq + D_LEFT), F32),
                        pltpu.VMEM((N_HEADS64, tq, tq + D_LEFT), F32), pltpu.VMEM((N_HEADS64, tq, LANES), F32)],
        compiler_params=_params(2),
    )(h, h, h, h, qg, kg, base)


def _w_in_pieces(take, zeros):
    c = lambda name, size, off=0: take(_SRC[name] + off, size)
    return [c("a_u", 512), c("a_v", 512), c("a_z", 512),
            c("b_q", 512), c("b_iq", 512), c("b_z", 512),
            c("b_k", 64), c("b_k", 64), c("b_v", 64), c("b_v", 64), c("b_ik", 64), c("b_ik", 64),
            c("c_kr", 32), c("b_iw", 8), zeros(24), c("c_kr", 32, 32), zeros(32),
            c("c_q", 384), c("c_kv", 128), c("c_z", 512),
            c("d_q", 512), c("d_k", 512), c("d_v", 512), c("d_z", 512)]


def _layout_w_in_kernel(w_ref, o_ref):
    tk = w_ref.shape[2]
    pieces = _w_in_pieces(lambda s, n: w_ref[0, s:s + n, :], lambda n: jnp.zeros((n, tk), F32))
    ends = np.cumsum([0] + [p.shape[0] for p in pieces])
    start = 0
    for i in range(1, len(pieces) + 1):
        if ends[i] % GROUP_W == 0:
            group = pieces[start:i]
            blk = group[0] if len(group) == 1 else jnp.concatenate(group, axis=0)
            o_ref[0, ends[start]:ends[i], :] = blk.astype(BF16)
            start = i


def _layout_w_in(w_in, tk=256):
    w_t = jnp.swapaxes(w_in, 1, 2)
    depth, cols, d = w_t.shape
    return pl.pallas_call(
        _layout_w_in_kernel,
        grid=(depth, d // tk),
        in_specs=[pl.BlockSpec((1, cols, tk), lambda l, i: (l, 0, i))],
        out_specs=pl.BlockSpec((1, H_COLS, tk), lambda l, i: (l, 0, i)),
        out_shape=jax.ShapeDtypeStruct((depth, H_COLS, d), BF16),
        compiler_params=_params(2),
    )(w_t)


def _rope_layout(v):
    z = jnp.zeros(v.shape[:-1] + (32,), v.dtype)
    return jnp.concatenate([v[..., :32], z, v[..., 32:], z], axis=-1)


def _layout_c(w_qb, w_kvb, q_gain, k_gain, qa_gain):
    wq = w_qb.reshape(Q_LORA, C_HEADS, C_QK)
    wq = jnp.concatenate([wq[..., :C_NOPE], _rope_layout(wq[..., C_NOPE:])], axis=-1)
    wq = wq.reshape(Q_LORA, C_HEADS * 256).astype(BF16)
    wkv = w_kvb.reshape(KV_LORA, C_HEADS, 2 * LANES)
    wkv = jnp.concatenate([wkv[..., :C_NOPE].reshape(KV_LORA, -1), wkv[..., C_NOPE:].reshape(KV_LORA, -1)],
                          axis=1).astype(BF16)
    lay = lambda g: jnp.concatenate([g[:C_NOPE], _rope_layout(g[C_NOPE:])])[None, :]
    return wq, wkv, lay(q_gain), lay(k_gain), qa_gain[None, :]


def _t5_bucket_static(rel):
    half = T5_BUCKETS // 2
    exact = half // 2
    n = abs(rel)
    if n < exact:
        val = n
    else:
        val = min(exact + (n * n // (exact * exact)).bit_length() - 1, half - 1)
    return (half if rel > 0 else 0) + val


def _t5_tables(t5_bias):
    m = np.arange(512)
    d0 = np.where(m < 256, m, m - 512)
    d1 = np.where(m <= 256, m - 256, m - 768)
    idx0 = np.array([_t5_bucket_static(int(d)) for d in d0], np.int32)
    idx1 = np.array([_t5_bucket_static(int(d)) for d in d1], np.int32)
    far = _t5_bucket_static(-512)
    return t5_bias[idx0].T, t5_bias[idx1].T, t5_bias[far]


def _band_table(rel_bias, tq):
    width = 2 * tq + D_LEFT
    m = np.arange(width)
    dist = np.where(m <= tq + D_LEFT, D_LEFT - m, D_LEFT + width - m)
    idx = np.clip(dist, -REL_CLIP, REL_CLIP) + REL_CLIP
    return rel_bias[idx.astype(np.int32)].T


def _rope_tables(seq):
    inv = ROPE_BASE ** (-jnp.arange(0, C_ROPE, 2, dtype=F32) / C_ROPE)
    ang = jnp.arange(seq, dtype=F32)[:, None] * inv[None, :]
    c, s = jnp.cos(ang), jnp.sin(ang)
    z = jnp.zeros_like(c)
    return jnp.concatenate([c, z, c, z], axis=1), jnp.concatenate([-s, z, s, z], axis=1)


def kernel(x, t5_bias, norm_g, w_in, a_v_gain, a_ws, a_bs, b_q_gain, b_k_gain, c_qa_gain, c_kva_gain,
           c_w_qb, c_w_kvb, c_q_gain, c_k_gain, d_q_gain, d_k_gain, d_rel_bias, w_out):
    bsz, seq, d_model = x.shape
    depth = w_in.shape[0]
    tq = 256
    x2 = x.reshape(bsz * seq, d_model)
    cos, sin = _rope_tables(seq)
    base0, base1, cfar = _t5_tables(t5_bias)
    w_in_blocks = _layout_w_in(w_in)
    for l in range(depth):
        h = _inproj(x2, norm_g[l][None, :], w_in_blocks, l)
        y_a = _mixer_a(h, a_v_gain[l][None, :], a_ws[l], a_bs[l][:, :, None])
        mask = _b_select(h, bsz, seq)
        y_b = _b_attn(h, mask, cfar, jnp.tile(b_q_gain[l], N_HEADS64)[None, :],
                      jnp.tile(b_k_gain[l], 2)[None, :], base0, base1, bsz, seq, tq)
        wq, wkv, qg, kg, qag = _layout_c(c_w_qb[l], c_w_kvb[l], c_q_gain[l], c_k_gain[l], c_qa_gain[l])
        qc, kc, vc = _c_prep(h, wq, wkv, qag, c_kva_gain[l][None, :], qg, kg, cos, sin, seq)
        y_c = _c_attn(qc, kc, vc, h, bsz, seq, tq)
        y_d = _mixer_d(h, jnp.tile(d_q_gain[l], N_HEADS64)[None, :], jnp.tile(d_k_gain[l], N_HEADS64)[None, :],
                       _band_table(d_rel_bias[l], tq), bsz, seq, tq)
        x2 = _outproj(x2, (y_a, y_b, y_c, y_d), w_out[l].astype(BF16))
    return x2.reshape(bsz, seq, d_model)
```

```python
import functools
import math

import numpy as np
import jax
import jax.numpy as jnp
from jax import lax
from jax.experimental import pallas as pl
from jax.experimental.pallas import tpu as pltpu

F32 = jnp.float32
BF16 = jnp.bfloat16

EPS = 1e-6
NEG = -1e30
LOG2E = math.log2(math.e)
INT_MIN = -(2 ** 31)
CHUNK = 64
LANES = 128
GROUP_W = 512
A_GROUPS = 4
GMLP_BLOCK = 128
N_HEADS64 = 8
IDX_SCALE = (8 ** -0.5) * 0.125
TOPK_MAX = 256
T5_BUCKETS = 32
C_HEADS = 4
C_NOPE = 128
C_ROPE = 64
C_QK = 192
Q_LORA = 384
KV_LORA = 128
ROPE_BASE = 10000.0
D_LEFT = 8 * CHUNK
REL_CLIP = 128
VMEM_LIMIT = 56 * 1024 * 1024

BLK_A_U, BLK_A_V, BLK_A_Z = 0, 1, 2
BLK_B_Q, BLK_B_IQ, BLK_B_Z, BLK_SMALL = 3, 4, 5, 6
BLK_C_QKV, BLK_C_Z = 7, 8
BLK_D_Q, BLK_D_K, BLK_D_V, BLK_D_Z = 9, 10, 11, 12
H_COLS = 13 * GROUP_W
UNIT_B_K, UNIT_B_V, UNIT_B_IK, UNIT_KR_IW = (BLK_SMALL * 4 + i for i in range(4))
IW_LANE = 32

_SRC = dict(a_u=0, a_v=512, a_z=1024, b_q=1536, b_k=2048, b_v=2112, b_iq=2176, b_ik=2688,
            b_iw=2752, b_z=2760, c_q=3272, c_kv=3656, c_kr=3784, c_z=3848,
            d_q=4360, d_k=4872, d_v=5384, d_z=5896)


def _params(n_axes):
    return pltpu.CompilerParams(dimension_semantics=("arbitrary",) * n_axes,
                                vmem_limit_bytes=VMEM_LIMIT)


def _gelu(x):
    c = math.sqrt(2.0 / math.pi)
    return x * (0.5 * (1.0 + jnp.tanh(c * (x + 0.044715 * (x * x * x)))))


def _silu(x):
    return x * (1.0 / (1.0 + jnp.exp(-x)))


def _dot_t(a, b):
    return lax.dot_general(a, b, (((1,), (1,)), ((), ())), preferred_element_type=F32)


def _lo_mask(rows):
    return lax.broadcasted_iota(jnp.int32, (rows, LANES), 1) < 64


def _rms_heads64(x, gain, ntiles):
    lo = _lo_mask(x.shape[0])
    tiles = []
    for t in range(ntiles):
        xt = x[:, t * LANES:(t + 1) * LANES]
        sq = xt * xt
        s_lo = jnp.sum(jnp.where(lo, sq, 0.0), axis=-1, keepdims=True)
        s_hi = jnp.sum(jnp.where(lo, 0.0, sq), axis=-1, keepdims=True)
        r = jnp.where(lo, lax.rsqrt(s_lo * (1.0 / 64) + EPS), lax.rsqrt(s_hi * (1.0 / 64) + EPS))
        tiles.append(xt * r * gain[:, t * LANES:(t + 1) * LANES])
    return tiles


def _for_blocks(blocks, count):
    def quad(i, carry):
        blocks(4 * i, 4)
        return carry

    lax.fori_loop(0, count // 4, quad, 0)
    first = (count // 4) * 4

    @pl.when((count & 2) != 0)
    def _():
        blocks(first, 2)

    @pl.when((count & 1) != 0)
    def _():
        blocks(first + (count & 2), 1)


def _toeplitz(base_row, rows, width):
    t = jnp.broadcast_to(base_row, (rows, base_row.shape[1]))
    t = pltpu.roll(t, 0, 1, stride=1, stride_axis=0)
    return t[:, :width]


def _inproj_kernel(x_ref, g_ref, w_ref, o_ref):
    x = x_ref[...]
    ms = jnp.mean(x * x, axis=-1, keepdims=True)
    xn = (x * lax.rsqrt(ms + EPS) * g_ref[...]).astype(BF16)
    for c in range(H_COLS // GROUP_W):
        cols = slice(c * GROUP_W, (c + 1) * GROUP_W)
        o_ref[:, cols] = _dot_t(xn, w_ref[cols, :]).astype(BF16)


def _inproj(x2, g, w_all, layer, tm=512):
    n, d = x2.shape
    return pl.pallas_call(
        _inproj_kernel,
        grid=(n // tm,),
        in_specs=[pl.BlockSpec((tm, d), lambda i: (i, 0)),
                  pl.BlockSpec((1, d), lambda i: (0, 0)),
                  pl.BlockSpec((None, H_COLS, d), lambda i: (layer, 0, 0), pipeline_mode=pl.Buffered(1))],
        out_specs=pl.BlockSpec((tm, H_COLS), lambda i: (i, 0)),
        out_shape=jax.ShapeDtypeStruct((n, H_COLS), BF16),
        compiler_params=_params(1),
    )(x2, g, w_all)


def _outproj_kernel(x_ref, u_ref, v_ref, z_ref, vg_ref, ws_ref, bs_ref, yb_ref, yc_ref, yd_ref, w_ref, o_ref,
                    ya_ref):
    tm, d = x_ref.shape
    wgs = _mixer_a_weights(ws_ref)
    nblk = tm // GMLP_BLOCK
    ncol = d // nblk
    accs = []
    for c in range(nblk):
        cols = slice(c * ncol, (c + 1) * ncol)
        acc = x_ref[:, cols]
        for g, y_ref in ((1, yb_ref), (2, yc_ref), (3, yd_ref)):
            acc = acc + jnp.dot(y_ref[...], w_ref[g * GROUP_W:(g + 1) * GROUP_W, cols],
                                preferred_element_type=F32)
        accs.append(acc)
        _mixer_a_rows(u_ref, v_ref, z_ref, vg_ref, wgs, bs_ref, ya_ref, c)
    for c in range(nblk):
        cols = slice(c * ncol, (c + 1) * ncol)
        o_ref[:, cols] = accs[c] + jnp.dot(ya_ref[...], w_ref[0:GROUP_W, cols], preferred_element_type=F32)


def _outproj(x2, h, vg, ws, bs, ys, w, tm=512):
    n, d = x2.shape
    yspec = pl.BlockSpec((tm, GROUP_W), lambda i: (i, 0))
    hspec = lambda blk: pl.BlockSpec((tm, GROUP_W), lambda i, blk=blk: (i, blk))
    return pl.pallas_call(
        _outproj_kernel,
        grid=(n // tm,),
        in_specs=[pl.BlockSpec((tm, d), lambda i: (i, 0)),
                  hspec(BLK_A_U), hspec(BLK_A_V), hspec(BLK_A_Z),
                  pl.BlockSpec((1, GROUP_W), lambda i: (0, 0)),
                  pl.BlockSpec((A_GROUPS, GMLP_BLOCK, GMLP_BLOCK), lambda i: (0, 0, 0)),
                  pl.BlockSpec((A_GROUPS, GMLP_BLOCK, 1), lambda i: (0, 0, 0)),
                  yspec, yspec, yspec,
                  pl.BlockSpec((4 * GROUP_W, d), lambda i: (0, 0))],
        out_specs=pl.BlockSpec((tm, d), lambda i: (i, 0)),
        out_shape=jax.ShapeDtypeStruct((n, d), F32),
        scratch_shapes=[pltpu.VMEM((tm, GROUP_W), BF16)],
        compiler_params=_params(1),
    )(x2, h, h, h, vg, ws, bs, *ys, w)


def _mixer_a_weights(w_ref):
    i = lax.broadcasted_iota(jnp.int32, (GMLP_BLOCK, GMLP_BLOCK), 0)
    j = lax.broadcasted_iota(jnp.int32, (GMLP_BLOCK, GMLP_BLOCK), 1)
    keep = (j // CHUNK) <= (i // CHUNK)
    return [jnp.where(keep, w_ref[g], 0.0).astype(BF16) for g in range(A_GROUPS)]


def _mixer_a_rows(u_ref, v_ref, z_ref, vg_ref, wgs, b_ref, o_ref, blk):
    rows = slice(blk * GMLP_BLOCK, (blk + 1) * GMLP_BLOCK)
    u = _gelu(u_ref[rows, :].astype(F32))
    v = _gelu(v_ref[rows, :].astype(F32))
    ms = jnp.mean(v * v, axis=-1, keepdims=True)
    vb = (v * lax.rsqrt(ms + EPS) * vg_ref[...]).astype(BF16)
    gate = _silu(z_ref[rows, :].astype(F32))
    for g in range(A_GROUPS):
        cols = slice(g * LANES, (g + 1) * LANES)
        sg = jnp.dot(wgs[g], vb[:, cols], preferred_element_type=F32) + b_ref[g]
        o_ref[rows, cols] = (u[:, cols] * sg * gate[:, cols]).astype(BF16)


def _order_key(x):
    return jnp.where(x < 0, x ^ 0x7FFFFFFF, x)


def _b_select_kernel(iq_ref, iw_ref, ik_ref, o_ref, lhs_ref, key_ref, *, seq, tq, topk):
    t_blk = pl.program_id(1)
    nkb = seq // 256
    nblk = ((t_blk + 1) * tq) // 256
    n_interp = 12
    n_unchecked = 12
    hrows = [slice(h * tq, (h + 1) * tq) for h in range(N_HEADS64)]

    lo_half = _lo_mask(tq)
    w_t = (iw_ref[...].astype(F32) * IDX_SCALE).T
    for h in range(N_HEADS64):
        iqt = iq_ref[:, (h // 2) * LANES:(h // 2 + 1) * LANES]
        sel = lo_half if h % 2 == 0 else jnp.logical_not(lo_half)
        lhs_ref[hrows[h], :] = jnp.where(sel, iqt, jnp.zeros_like(iqt))

    qpos = t_blk * tq + lax.broadcasted_iota(jnp.int32, (256, tq), 1)
    krow = lax.broadcasted_iota(jnp.int32, (256, tq), 0)

    def fold8(x):
        return jnp.sum(x.reshape(256 // 8, 8, tq), axis=0)

    def score_block(kb, carry, masked):
        amax, n_pos, n_nn = carry
        off = pl.multiple_of(kb * 256, 256)
        ikblk = ik_ref[pl.ds(off, 256), :]
        score = jnp.zeros((256, tq), F32)
        for h in range(N_HEADS64):
            w_h = w_t[IW_LANE + h:IW_LANE + h + 1, :]
            score = score + w_h * jnp.maximum(_dot_t(ikblk, lhs_ref[hrows[h], :]), 0.0)
        score = jnp.where(score == 0.0, 0.0, score)
        key = _order_key(lax.bitcast_convert_type(score, jnp.int32))
        if masked:
            key = jnp.where(((kb * 256 + krow) // CHUNK) <= (qpos // CHUNK), key, INT_MIN)
        key_ref[kb] = key
        return (jnp.maximum(amax, jnp.max(jnp.abs(score).reshape(256 // 8, 8, tq), axis=0)),
                n_pos + fold8(jnp.where(key >= 1, 1.0, 0.0)), n_nn + fold8(jnp.where(key >= 0, 1.0, 0.0)))

    nfull = (t_blk * tq) // 256
    zeros8 = jnp.zeros((8, tq), F32)
    carry = lax.fori_loop(0, nfull, lambda kb, c: score_block(kb, c, False), (zeros8, zeros8, zeros8))
    carry = lax.fori_loop(nfull, nblk, lambda kb, c: score_block(kb, c, True), carry)
    amax = jnp.max(carry[0], axis=0, keepdims=True)
    f_pos = jnp.sum(carry[1], axis=0, keepdims=True)
    f_nn = jnp.sum(carry[2], axis=0, keepdims=True)

    def count(pred):
        def body(kb, acc):
            return acc + fold8(jnp.where(pred(key_ref[kb], kb), 1.0, 0.0))
        return jnp.sum(lax.fori_loop(0, nblk, body, zeros8), axis=0, keepdims=True)

    kf = float(topk)
    qrow = t_blk * tq + lax.broadcasted_iota(jnp.int32, (1, tq), 1)
    n_adm = ((qrow // CHUNK + 1) * CHUNK).astype(F32)
    one = jnp.ones((1, tq), jnp.int32)
    pos = f_pos > kf
    neg = f_nn < kf
    lo0 = jnp.where(pos, one, _order_key(lax.bitcast_convert_type(-amax, jnp.int32)))
    hi0 = jnp.where(neg, one - 1, _order_key(lax.bitcast_convert_type(amax, jnp.int32)) + 1)
    w_lo0 = jnp.where(pos, f_pos, n_adm) - kf
    w_hi0 = kf - jnp.where(neg, f_nn, 0.0)
    all_sel = n_adm <= kf
    at_zero = jnp.logical_not(pos | neg)
    done0 = jnp.where(all_sel | at_zero | (hi0 == lo0 + 1), 1.0, 0.0)
    thr0 = jnp.where(all_sel, INT_MIN + 1, jnp.where(at_zero, jnp.where(f_pos == kf, one, one - 1), lo0))

    def search_cond(st):
        return jnp.logical_and(st[0][0] < n_interp + 32, st[1] < 0.5)

    def search_step(st):
        it, lo, hi, w_lo, w_hi, side, done, thr = st
        lo_v = lax.bitcast_convert_type(_order_key(lo), F32)
        hi_v = lax.bitcast_convert_type(_order_key(hi), F32)
        c_v = lo_v + (hi_v - lo_v) * (w_lo / (w_lo + w_hi))
        c_interp = _order_key(lax.bitcast_convert_type(c_v, jnp.int32))
        c_mid = (lo >> 1) + (hi >> 1) + (lo & hi & 1)
        cand = jnp.where(it < n_interp, c_interp, c_mid)
        cand = jnp.minimum(jnp.maximum(cand, lo + 1), hi - 1)
        f = count(lambda k, kb: k >= cand)
        live = done < 0.5
        up = f > kf
        hit = f == kf
        new_lo = jnp.where(live & up, cand, lo)
        new_hi = jnp.where(live & jnp.logical_not(up), cand, hi)
        new_w_lo = jnp.where(up, f - kf, jnp.where(side < 0.0, 0.5 * w_lo, w_lo))
        new_w_hi = jnp.where(up, jnp.where(side > 0.0, 0.5 * w_hi, w_hi), kf - f)
        new_side = jnp.where(up, 1.0, -1.0)
        new_thr = jnp.where(live, jnp.where(hit, cand, new_lo), thr)
        new_done = jnp.where(live & (hit | (new_hi == new_lo + 1)), 1.0, done)
        return (it + 1, new_lo, new_hi, jnp.where(live, new_w_lo, w_lo),
                jnp.where(live, new_w_hi, w_hi), jnp.where(live, new_side, side), new_done, new_thr)

    def checked_step(st):
        new = search_step(st[0])
        return new, jnp.min(new[-2])

    state = (jnp.int32(0), lo0, hi0, w_lo0, w_hi0, jnp.zeros((1, tq), F32), done0, thr0)
    state = lax.fori_loop(0, n_unchecked, lambda i, st: search_step(st), state)
    thr = lax.while_loop(search_cond, checked_step, (state, jnp.min(state[-2])))[0][-1]

    any_excess = jnp.max(count(lambda k, kb: k >= thr)) > kf

    def store_mask(kb, keep_t):
        for g in range(tq // 256):
            o_ref[g, kb] = keep_t[:, g * 256:(g + 1) * 256].T

    def write_unused(kb, carry):
        store_mask(kb, jnp.zeros((256, tq), F32))
        return carry

    lax.fori_loop(nblk, nkb, write_unused, 0)

    @pl.when(jnp.logical_not(any_excess))
    def _():
        def write(kb, carry):
            store_mask(kb, jnp.where(key_ref[kb] >= thr, 1.0, 0.0))
            return carry

        lax.fori_loop(0, nblk, write, 0)

    @pl.when(any_excess)
    def _():
        need = kf - count(lambda k, kb: k > thr)

        def idx_step(it, jmax):
            cand = jmax | lax.shift_left(jnp.int32(1), 10 - it)
            below = count(lambda k, kb: (k == thr) & ((kb * 256 + krow) < cand))
            return jnp.where(below < need, cand, jmax)

        jmax = lax.fori_loop(0, 11, idx_step, jnp.zeros((1, tq), jnp.int32))

        def write(kb, carry):
            k = key_ref[kb]
            keep_tie = (k == thr) & ((kb * 256 + krow) <= jmax)
            store_mask(kb, jnp.where((k > thr) | keep_tie, 1.0, 0.0))
            return carry

        lax.fori_loop(0, nblk, write, 0)


def _b_select(h, bsz, seq, tq=512):
    tq = min(tq, seq)
    nt = seq // tq
    topk = min(TOPK_MAX, seq // 4)
    kern = functools.partial(_b_select_kernel, seq=seq, tq=tq, topk=topk)
    return pl.pallas_call(
        kern,
        grid=(bsz, nt),
        in_specs=[pl.BlockSpec((tq, GROUP_W), lambda b, t: (b * nt + t, BLK_B_IQ)),
                  pl.BlockSpec((tq, LANES), lambda b, t: (b * nt + t, UNIT_KR_IW)),
                  pl.BlockSpec((seq, LANES), lambda b, t: (b, UNIT_B_IK))],
        out_specs=pl.BlockSpec((tq // 256, seq // 256, 256, 256), lambda b, t: (b * nt + t, 0, 0, 0)),
        out_shape=jax.ShapeDtypeStruct((bsz * seq // 256, seq // 256, 256, 256), F32),
        scratch_shapes=[pltpu.VMEM((N_HEADS64 * tq, LANES), BF16),
                        pltpu.VMEM((seq // 256, 256, tq), jnp.int32)],
        compiler_params=_params(2),
    )(h, h, h)


def _b_attn_kernel(cfar_ref, q_ref, z_ref, k_ref, v_ref, msk_ref, qg_ref, kg_ref, base0_ref, base1_ref,
                   o_ref, kn_ref, v1_ref, bias_ref, qall_ref, s_ref, mp_ref, acc_ref, *, seq, tq):
    b = pl.program_id(0)
    t_blk = pl.program_id(1)
    hrows = [slice(h * tq, (h + 1) * tq) for h in range(N_HEADS64)]

    @pl.when((b == 0) & (t_blk == 0))
    def _():
        for h in range(N_HEADS64):
            bias_ref[0, hrows[h], :] = jnp.full((tq, 256), cfar_ref[h] * LOG2E, F32)
            bias_ref[1, hrows[h], :] = _toeplitz(base1_ref[h:h + 1, :], tq, 256) * LOG2E
            bias_ref[2, hrows[h], :] = _toeplitz(base0_ref[h:h + 1, :], tq, 256) * LOG2E

    @pl.when(t_blk == 0)
    def _():
        lo256 = _lo_mask(256)
        for r in range(seq // 256):
            rows = slice(r * 256, (r + 1) * 256)
            k = k_ref[rows, :].astype(F32)
            ms = jnp.mean(k * k, axis=-1, keepdims=True)
            kn_ref[rows, :] = (k * lax.rsqrt(ms + EPS) * kg_ref[...]).astype(BF16)
            v = v_ref[rows, :]
            v1_ref[rows, :] = jnp.where(lo256, v, jnp.ones_like(v))

    lo = _lo_mask(tq)
    qtiles = _rms_heads64(q_ref[...].astype(F32), qg_ref[...], 4)
    for h in range(N_HEADS64):
        sel = lo if h % 2 == 0 else jnp.logical_not(lo)
        qall_ref[hrows[h], :] = jnp.where(sel, qtiles[h // 2] * (0.125 * LOG2E), 0.0).astype(BF16)
    mp_ref[...] = jnp.full(mp_ref.shape, NEG, F32)
    acc_ref[...] = jnp.zeros(acc_ref.shape, F32)
    nblk = t_blk + 1

    def logits_blocks(kb0, n):
        for i in range(n):
            kb = kb0 + i
            off = pl.multiple_of(kb * 256, 256)
            kblk = kn_ref[pl.ds(off, 256), :]
            which = jnp.clip(kb - (t_blk - 2), 0, 2)
            keep = msk_ref[0, kb] > 0.5
            for h in range(N_HEADS64):
                s = jnp.where(keep, _dot_t(qall_ref[hrows[h], :], kblk) + bias_ref[which, hrows[h], :], NEG)
                s_ref[kb, hrows[h], :] = s
                mp_ref[hrows[h], :] = jnp.maximum(mp_ref[hrows[h], :], jnp.maximum(s[:, :LANES], s[:, LANES:]))

    def value_blocks(kb0, n):
        off = pl.multiple_of(kb0 * 256, 256)
        v1 = v1_ref[pl.ds(off, n * 256), :]
        for h in range(N_HEADS64):
            m = mp_ref[hrows[h], :]
            mm = jnp.concatenate([m, m], axis=1)
            p = [jnp.exp2(s_ref[kb0 + i, hrows[h], :] - mm).astype(BF16) for i in range(n)]
            p = p[0] if n == 1 else jnp.concatenate(p, axis=1)
            acc_ref[hrows[h], :] += jnp.dot(p, v1, preferred_element_type=F32)

    _for_blocks(logits_blocks, nblk)
    for h in range(N_HEADS64):
        m = jnp.max(mp_ref[hrows[h], :], axis=-1, keepdims=True)
        mp_ref[hrows[h], :] = jnp.broadcast_to(m, (tq, LANES))
    _for_blocks(value_blocks, nblk)

    gate = _silu(z_ref[...].astype(F32))
    for t in range(4):
        a_even = acc_ref[hrows[2 * t], :]
        a_odd = acc_ref[hrows[2 * t + 1], :]
        o_even = a_even / pltpu.roll(a_even, 64, 1)
        o_odd = pltpu.roll(a_odd, 64, 1) / a_odd
        cols = slice(t * LANES, (t + 1) * LANES)
        o_ref[:, cols] = (jnp.where(lo, o_even, o_odd) * gate[:, cols]).astype(BF16)


def _b_attn(h, mask, cfar, qg, kg, base0, base1, bsz, seq, tq=256):
    nt = seq // tq
    rows = N_HEADS64 * tq
    kern = functools.partial(_b_attn_kernel, seq=seq, tq=tq)
    full = lambda shape: pl.BlockSpec(shape, lambda b, t: (0,) * len(shape))
    return pl.pallas_call(
        kern,
        grid=(bsz, nt),
        in_specs=[pl.BlockSpec(memory_space=pltpu.SMEM),
                  pl.BlockSpec((tq, GROUP_W), lambda b, t: (b * nt + t, BLK_B_Q)),
                  pl.BlockSpec((tq, GROUP_W), lambda b, t: (b * nt + t, BLK_B_Z)),
                  pl.BlockSpec((seq, LANES), lambda b, t: (b, UNIT_B_K)),
                  pl.BlockSpec((seq, LANES), lambda b, t: (b, UNIT_B_V)),
                  pl.BlockSpec((1, seq // 256, tq, 256), lambda b, t: (b * nt + t, 0, 0, 0)),
                  full((1, GROUP_W)), full((1, LANES)), full((N_HEADS64, 512)), full((N_HEADS64, 512))],
        out_specs=pl.BlockSpec((tq, GROUP_W), lambda b, t: (b * nt + t, 0)),
        out_shape=jax.ShapeDtypeStruct((bsz * seq, GROUP_W), BF16),
        scratch_shapes=[pltpu.VMEM((seq, LANES), BF16), pltpu.VMEM((seq, LANES), BF16),
                        pltpu.VMEM((3, rows, 256), F32), pltpu.VMEM((rows, LANES), BF16),
                        pltpu.VMEM((seq // 256, rows, 256), F32),
                        pltpu.VMEM((rows, LANES), F32), pltpu.VMEM((rows, LANES), F32)],
        compiler_params=_params(2),
    )(cfar, h, h, h, h, mask, qg, kg, base0, base1)


def _rope(tile, cos, sin):
    return tile * cos + pltpu.roll(tile, 64, 1) * sin


def _c_prep_kernel(lat_ref, kr_ref, wq_ref, wkv_ref, qag_ref, kvag_ref, qg_ref, kg_ref, cos_ref, sin_ref,
                   qo_ref, ko_ref, vo_ref):
    cq = lat_ref[:, :Q_LORA].astype(F32)
    ms = jnp.mean(cq * cq, axis=-1, keepdims=True)
    cqn = (cq * lax.rsqrt(ms + EPS) * qag_ref[...]).astype(BF16)
    qpre = jnp.dot(cqn, wq_ref[...], preferred_element_type=F32)
    ckv = lat_ref[:, Q_LORA:].astype(F32)
    ms = jnp.mean(ckv * ckv, axis=-1, keepdims=True)
    ckvn = (ckv * lax.rsqrt(ms + EPS) * kvag_ref[...]).astype(BF16)
    kvpre = jnp.dot(ckvn, wkv_ref[...], preferred_element_type=F32)
    lane = lax.broadcasted_iota(jnp.int32, kr_ref.shape, 1)
    kr = jnp.where((lane % 64) < 32, kr_ref[...].astype(F32), 0.0)
    kr_ss = jnp.sum(kr * kr, axis=-1, keepdims=True)
    cos = cos_ref[...]
    sin = sin_ref[...]
    qg = qg_ref[...]
    kg = kg_ref[...]
    kr_rot = _rope(kr * kg[:, LANES:], cos, sin)
    for h in range(C_HEADS):
        qh = qpre[:, h * 256:(h + 1) * 256]
        r = lax.rsqrt(jnp.sum(qh * qh, axis=-1, keepdims=True) * (1.0 / C_QK) + EPS)
        qn = qh * r * qg
        qo_ref[:, h * 256:h * 256 + LANES] = qn[:, :LANES].astype(BF16)
        qo_ref[:, h * 256 + LANES:(h + 1) * 256] = _rope(qn[:, LANES:], cos, sin).astype(BF16)
        kn = kvpre[:, h * LANES:(h + 1) * LANES]
        r = lax.rsqrt((jnp.sum(kn * kn, axis=-1, keepdims=True) + kr_ss) * (1.0 / C_QK) + EPS)
        ko_ref[:, h * 256:h * 256 + LANES] = (kn * r * kg[:, :LANES]).astype(BF16)
        ko_ref[:, h * 256 + LANES:(h + 1) * 256] = (kr_rot * r).astype(BF16)
    vo_ref[...] = kvpre[:, C_HEADS * LANES:].astype(BF16)


def _c_prep(h, wq, wkv, qag, kvag, qg, kg, cos, sin, seq, tm=512):
    n = h.shape[0]
    ns = seq // tm
    full = lambda shape: pl.BlockSpec(shape, lambda i: (0,) * len(shape))
    return pl.pallas_call(
        _c_prep_kernel,
        grid=(n // tm,),
        in_specs=[pl.BlockSpec((tm, GROUP_W), lambda i: (i, BLK_C_QKV)),
                  pl.BlockSpec((tm, LANES), lambda i: (i, UNIT_KR_IW)),
                  full((Q_LORA, 4 * 256)), full((KV_LORA, 8 * LANES)),
                  full((1, Q_LORA)), full((1, LANES)), full((1, 256)), full((1, 256)),
                  pl.BlockSpec((tm, LANES), lambda i: (i % ns, 0)),
                  pl.BlockSpec((tm, LANES), lambda i: (i % ns, 0))],
        out_specs=[pl.BlockSpec((tm, 4 * 256), lambda i: (i, 0)),
                   pl.BlockSpec((tm, 4 * 256), lambda i: (i, 0)),
                   pl.BlockSpec((tm, GROUP_W), lambda i: (i, 0))],
        out_shape=[jax.ShapeDtypeStruct((n, 4 * 256), BF16), jax.ShapeDtypeStruct((n, 4 * 256), BF16),
                   jax.ShapeDtypeStruct((n, GROUP_W), BF16)],
        compiler_params=_params(1),
    )(h, h, wq, wkv, qag, kvag, qg, kg, cos, sin)


def _c_attn_kernel(q_ref, k_ref, v_ref, z_ref, o_ref, s_ref, mp_ref, lp_ref, acc_ref, *, tq):
    qt = pl.program_id(1)
    scale = C_QK ** -0.5 * LOG2E
    hrows = [slice(h * tq, (h + 1) * tq) for h in range(C_HEADS)]
    qchunk = (qt * tq + lax.broadcasted_iota(jnp.int32, (tq, 256), 0)) // CHUNK
    kcol = lax.broadcasted_iota(jnp.int32, (tq, 256), 1)
    mp_ref[...] = jnp.full(mp_ref.shape, NEG, F32)
    lp_ref[...] = jnp.zeros(lp_ref.shape, F32)
    acc_ref[...] = jnp.zeros(acc_ref.shape, F32)
    nfull = (qt * tq) // 256
    nblk = ((qt + 1) * tq) // 256

    def logits_blocks(kb0, n, masked):
        for i in range(n):
            kb = kb0 + i
            off = pl.multiple_of(kb * 256, 256)
            for h in range(C_HEADS):
                cols = slice(h * 256, (h + 1) * 256)
                s = _dot_t(q_ref[:, cols], k_ref[pl.ds(off, 256), cols]) * scale
                if masked:
                    s = jnp.where(((kb * 256 + kcol) // CHUNK) <= qchunk, s, NEG)
                s_ref[kb, hrows[h], :] = s
                mp_ref[hrows[h], :] = jnp.maximum(mp_ref[hrows[h], :], jnp.maximum(s[:, :LANES], s[:, LANES:]))

    def value_blocks(kb0, n):
        off = pl.multiple_of(kb0 * 256, 256)
        for h in range(C_HEADS):
            m = mp_ref[hrows[h], :]
            mm = jnp.concatenate([m, m], axis=1)
            p = [jnp.exp2(s_ref[kb0 + i, hrows[h], :] - mm) for i in range(n)]
            lsum = p[0][:, :LANES] + p[0][:, LANES:]
            for pi in p[1:]:
                lsum = lsum + pi[:, :LANES] + pi[:, LANES:]
            lp_ref[hrows[h], :] += lsum
            pb = p[0].astype(BF16) if n == 1 else jnp.concatenate([pi.astype(BF16) for pi in p], axis=1)
            acc_ref[hrows[h], :] += jnp.dot(pb, v_ref[pl.ds(off, n * 256), h * LANES:(h + 1) * LANES],
                                            preferred_element_type=F32)

    _for_blocks(lambda kb0, n: logits_blocks(kb0, n, False), nfull)
    lax.fori_loop(nfull, nblk, lambda kb, c: (logits_blocks(kb, 1, True), c)[1], 0)
    for h in range(C_HEADS):
        m = jnp.max(mp_ref[hrows[h], :], axis=-1, keepdims=True)
        mp_ref[hrows[h], :] = jnp.broadcast_to(m, (tq, LANES))
    _for_blocks(value_blocks, nblk)
    gate = _silu(z_ref[...].astype(F32))
    for h in range(C_HEADS):
        cols = slice(h * LANES, (h + 1) * LANES)
        l = jnp.sum(lp_ref[hrows[h], :], axis=-1, keepdims=True)
        o_ref[:, cols] = (acc_ref[hrows[h], :] / l * gate[:, cols]).astype(BF16)


def _c_attn(qc, kc, vc, h, bsz, seq, tq=256):
    nt = seq // tq
    rows = C_HEADS * tq
    kern = functools.partial(_c_attn_kernel, tq=tq)
    return pl.pallas_call(
        kern,
        grid=(bsz, nt),
        in_specs=[pl.BlockSpec((tq, C_HEADS * 256), lambda b, t: (b * nt + t, 0)),
                  pl.BlockSpec((seq, C_HEADS * 256), lambda b, t: (b, 0)),
                  pl.BlockSpec((seq, GROUP_W), lambda b, t: (b, 0)),
                  pl.BlockSpec((tq, GROUP_W), lambda b, t: (b * nt + t, BLK_C_Z))],
        out_specs=pl.BlockSpec((tq, GROUP_W), lambda b, t: (b * nt + t, 0)),
        out_shape=jax.ShapeDtypeStruct((bsz * seq, GROUP_W), BF16),
        scratch_shapes=[pltpu.VMEM((seq // 256, rows, 256), F32), pltpu.VMEM((rows, LANES), F32),
                        pltpu.VMEM((rows, LANES), F32), pltpu.VMEM((rows, LANES), F32)],
        compiler_params=_params(2),
    )(qc, kc, vc, h)


def _mixer_d_kernel(q_ref, k_ref, v_ref, z_ref, qg_ref, kg_ref, base_ref, o_ref,
                    kpad_ref, vpad_ref, bias_ref, s_ref, mp_ref, *, seq, tq):
    b = pl.program_id(0)
    qt = pl.program_id(1)
    win = tq + D_LEFT

    @pl.when((b == 0) & (qt == 0))
    def _():
        qc = lax.broadcasted_iota(jnp.int32, (tq, win), 0) // CHUNK
        kc = lax.broadcasted_iota(jnp.int32, (tq, win), 1) // CHUNK
        band = (kc >= qc) & (kc <= qc + D_LEFT // CHUNK)
        for h in range(N_HEADS64):
            bias_ref[h] = jnp.where(band, _toeplitz(base_ref[h:h + 1, :], tq, win) * LOG2E, NEG)

    @pl.when(qt == 0)
    def _():
        kpad_ref[0:D_LEFT, :] = jnp.zeros((D_LEFT, GROUP_W), BF16)
        vpad_ref[0:D_LEFT, :] = jnp.zeros((D_LEFT, GROUP_W), BF16)
        for r in range(seq // 256):
            rows = slice(r * 256, (r + 1) * 256)
            dst = slice(D_LEFT + r * 256, D_LEFT + (r + 1) * 256)
            tiles = _rms_heads64(k_ref[rows, :].astype(F32), kg_ref[...], 4)
            for t in range(4):
                kpad_ref[dst, t * LANES:(t + 1) * LANES] = tiles[t].astype(BF16)
            vpad_ref[dst, :] = v_ref[rows, :]

    lo = _lo_mask(tq)
    qtiles = _rms_heads64(q_ref[...].astype(F32), qg_ref[...], 4)
    start = pl.multiple_of(qt * tq, tq)
    gate = _silu(z_ref[...].astype(F32))
    ntile = win // LANES

    def lane_tiles(x):
        return [x[:, i * LANES:(i + 1) * LANES] for i in range(ntile)]

    def logits_pass(before_start):
        if before_start:
            in_seq = lax.broadcasted_iota(jnp.int32, (tq, win), 1) + start >= D_LEFT
        for t in range(4):
            kwin = kpad_ref[pl.ds(start, win), t * LANES:(t + 1) * LANES]
            for half in range(2):
                h = 2 * t + half
                sel = lo if half == 0 else jnp.logical_not(lo)
                qh = jnp.where(sel, qtiles[t] * (0.125 * LOG2E), 0.0).astype(BF16)
                s = _dot_t(qh, kwin) + bias_ref[h]
                if before_start:
                    s = jnp.where(in_seq, s, NEG)
                s_ref[h] = s
                mp_ref[h] = functools.reduce(jnp.maximum, lane_tiles(s))

    @pl.when(start < D_LEFT)
    def _():
        logits_pass(True)

    @pl.when(start >= D_LEFT)
    def _():
        logits_pass(False)

    for h in range(N_HEADS64):
        mp_ref[h] = jnp.broadcast_to(jnp.max(mp_ref[h], axis=-1, keepdims=True), (tq, LANES))
    for t in range(4):
        cols = slice(t * LANES, (t + 1) * LANES)
        vwin = vpad_ref[pl.ds(start, win), cols]
        outs = []
        for half in range(2):
            h = 2 * t + half
            m = mp_ref[h]
            p = jnp.exp2(s_ref[h] - jnp.concatenate([m] * ntile, axis=1))
            l = jnp.sum(functools.reduce(jnp.add, lane_tiles(p)), axis=-1, keepdims=True)
            outs.append(jnp.dot(p.astype(BF16), vwin, preferred_element_type=F32) / l)
        o_ref[:, cols] = (jnp.where(lo, outs[0], outs[1]) * gate[:, cols]).astype(BF16)


def _mixer_d(h, qg, kg, base, bsz, seq, tq=256):
    nt = seq // tq
    kern = functools.partial(_mixer_d_kernel, seq=seq, tq=tq)
    full = lambda shape: pl.BlockSpec(shape, lambda b, t: (0,) * len(shape))
    return pl.pallas_call(
        kern,
        grid=(bsz, nt),
        in_specs=[pl.BlockSpec((tq, GROUP_W), lambda b, t: (b * nt + t, BLK_D_Q)),
                  pl.BlockSpec((seq, GROUP_W), lambda b, t: (b, BLK_D_K)),
                  pl.BlockSpec((seq, GROUP_W), lambda b, t: (b, BLK_D_V)),
                  pl.BlockSpec((tq, GROUP_W), lambda b, t: (b * nt + t, BLK_D_Z)),
                  full((1, GROUP_W)), full((1, GROUP_W)), full((N_HEADS64, 2 * tq + D_LEFT))],
        out_specs=pl.BlockSpec((tq, GROUP_W), lambda b, t: (b * nt + t, 0)),
        out_shape=jax.ShapeDtypeStruct((bsz * seq, GROUP_W), BF16),
        scratch_shapes=[pltpu.VMEM((seq + D_LEFT, GROUP_W), BF16), pltpu.VMEM((seq + D_LEFT, GROUP_W), BF16),
                        pltpu.VMEM((N_HEADS64, tq, tq + D_LEFT), F32),
                        pltpu.VMEM((N_HEADS64, tq, tq + D_LEFT), F32), pltpu.VMEM((N_HEADS64, tq, LANES), F32)],
        compiler_params=_params(2),
    )(h, h, h, h, qg, kg, base)


def _w_in_pieces(take, zeros):
    c = lambda name, size, off=0: take(_SRC[name] + off, size)
    return [c("a_u", 512), c("a_v", 512), c("a_z", 512),
            c("b_q", 512), c("b_iq", 512), c("b_z", 512),
            c("b_k", 64), c("b_k", 64), c("b_v", 64), c("b_v", 64), c("b_ik", 64), c("b_ik", 64),
            c("c_kr", 32), c("b_iw", 8), zeros(24), c("c_kr", 32, 32), zeros(32),
            c("c_q", 384), c("c_kv", 128), c("c_z", 512),
            c("d_q", 512), c("d_k", 512), c("d_v", 512), c("d_z", 512)]


def _layout_w_in_kernel(w_ref, o_ref):
    tk = w_ref.shape[2]
    pieces = _w_in_pieces(lambda s, n: w_ref[0, s:s + n, :], lambda n: jnp.zeros((n, tk), F32))
    ends = np.cumsum([0] + [p.shape[0] for p in pieces])
    start = 0
    for i in range(1, len(pieces) + 1):
        if ends[i] % GROUP_W == 0:
            group = pieces[start:i]
            blk = group[0] if len(group) == 1 else jnp.concatenate(group, axis=0)
            o_ref[0, ends[start]:ends[i], :] = blk.astype(BF16)
            start = i


def _layout_w_in(w_in, tk=256):
    w_t = jnp.swapaxes(w_in, 1, 2)
    depth, cols, d = w_t.shape
    return pl.pallas_call(
        _layout_w_in_kernel,
        grid=(depth, d // tk),
        in_specs=[pl.BlockSpec((1, cols, tk), lambda l, i: (l, 0, i))],
        out_specs=pl.BlockSpec((1, H_COLS, tk), lambda l, i: (l, 0, i)),
        out_shape=jax.ShapeDtypeStruct((depth, H_COLS, d), BF16),
        compiler_params=_params(2),
    )(w_t)


def _rope_layout(v):
    z = jnp.zeros(v.shape[:-1] + (32,), v.dtype)
    return jnp.concatenate([v[..., :32], z, v[..., 32:], z], axis=-1)


def _layout_c(w_qb, w_kvb, q_gain, k_gain, qa_gain):
    wq = w_qb.reshape(Q_LORA, C_HEADS, C_QK)
    wq = jnp.concatenate([wq[..., :C_NOPE], _rope_layout(wq[..., C_NOPE:])], axis=-1)
    wq = wq.reshape(Q_LORA, C_HEADS * 256).astype(BF16)
    wkv = w_kvb.reshape(KV_LORA, C_HEADS, 2 * LANES)
    wkv = jnp.concatenate([wkv[..., :C_NOPE].reshape(KV_LORA, -1), wkv[..., C_NOPE:].reshape(KV_LORA, -1)],
                          axis=1).astype(BF16)
    lay = lambda g: jnp.concatenate([g[:C_NOPE], _rope_layout(g[C_NOPE:])])[None, :]
    return wq, wkv, lay(q_gain), lay(k_gain), qa_gain[None, :]


def _t5_bucket_static(rel):
    half = T5_BUCKETS // 2
    exact = half // 2
    n = abs(rel)
    if n < exact:
        val = n
    else:
        val = min(exact + (n * n // (exact * exact)).bit_length() - 1, half - 1)
    return (half if rel > 0 else 0) + val


def _t5_tables(t5_bias):
    m = np.arange(512)
    d0 = np.where(m < 256, m, m - 512)
    d1 = np.where(m <= 256, m - 256, m - 768)
    idx0 = np.array([_t5_bucket_static(int(d)) for d in d0], np.int32)
    idx1 = np.array([_t5_bucket_static(int(d)) for d in d1], np.int32)
    far = _t5_bucket_static(-512)
    return t5_bias[idx0].T, t5_bias[idx1].T, t5_bias[far]


def _band_table(rel_bias, tq):
    width = 2 * tq + D_LEFT
    m = np.arange(width)
    dist = np.where(m <= tq + D_LEFT, D_LEFT - m, D_LEFT + width - m)
    idx = np.clip(dist, -REL_CLIP, REL_CLIP) + REL_CLIP
    return rel_bias[idx.astype(np.int32)].T


def _rope_tables(seq):
    inv = ROPE_BASE ** (-jnp.arange(0, C_ROPE, 2, dtype=F32) / C_ROPE)
    ang = jnp.arange(seq, dtype=F32)[:, None] * inv[None, :]
    c, s = jnp.cos(ang), jnp.sin(ang)
    z = jnp.zeros_like(c)
    return jnp.concatenate([c, z, c, z], axis=1), jnp.concatenate([-s, z, s, z], axis=1)


def kernel(x, t5_bias, norm_g, w_in, a_v_gain, a_ws, a_bs, b_q_gain, b_k_gain, c_qa_gain, c_kva_gain,
           c_w_qb, c_w_kvb, c_q_gain, c_k_gain, d_q_gain, d_k_gain, d_rel_bias, w_out):
    bsz, seq, d_model = x.shape
    depth = w_in.shape[0]
    tq = 256
    x2 = x.reshape(bsz * seq, d_model)
    cos, sin = _rope_tables(seq)
    base0, base1, cfar = _t5_tables(t5_bias)
    w_in_blocks = _layout_w_in(w_in)
    for l in range(depth):
        h = _inproj(x2, norm_g[l][None, :], w_in_blocks, l)
        mask = _b_select(h, bsz, seq)
        y_b = _b_attn(h, mask, cfar, jnp.tile(b_q_gain[l], N_HEADS64)[None, :],
                      jnp.tile(b_k_gain[l], 2)[None, :], base0, base1, bsz, seq, tq)
        wq, wkv, qg, kg, qag = _layout_c(c_w_qb[l], c_w_kvb[l], c_q_gain[l], c_k_gain[l], c_qa_gain[l])
        qc, kc, vc = _c_prep(h, wq, wkv, qag, c_kva_gain[l][None, :], qg, kg, cos, sin, seq)
        y_c = _c_attn(qc, kc, vc, h, bsz, seq, tq)
        y_d = _mixer_d(h, jnp.tile(d_q_gain[l], N_HEADS64)[None, :], jnp.tile(d_k_gain[l], N_HEADS64)[None, :],
                       _band_table(d_rel_bias[l], tq), bsz, seq, tq)
        x2 = _outproj(x2, h, a_v_gain[l][None, :], a_ws[l], a_bs[l][:, :, None], (y_b, y_c, y_d),
                      w_out[l].astype(BF16))
    return x2.reshape(bsz, seq, d_model)
```

```python
import functools
import math

import numpy as np
import jax
import jax.numpy as jnp
from jax import lax
from jax.experimental import pallas as pl
from jax.experimental.pallas import tpu as pltpu

F32 = jnp.float32
BF16 = jnp.bfloat16

EPS = 1e-6
NEG = -1e30
LOG2E = math.log2(math.e)
KEY_LOWEST = int(np.float32(-np.finfo(np.float32).max).view(np.int32)) ^ 0x7FFFFFFF
CHUNK = 64
LANES = 128
GROUP_W = 512
A_GROUPS = 4
GMLP_BLOCK = 128
N_HEADS64 = 8
IDX_SCALE = (8 ** -0.5) * 0.125
TOPK_MAX = 256
T5_BUCKETS = 32
C_HEADS = 4
C_NOPE = 128
C_ROPE = 64
C_QK = 192
Q_LORA = 384
KV_LORA = 128
ROPE_BASE = 10000.0
D_LEFT = 8 * CHUNK
REL_CLIP = 128
VMEM_LIMIT = 56 * 1024 * 1024

BLK_A_U, BLK_A_V, BLK_A_Z = 0, 1, 2
BLK_B_Q, BLK_B_IQ, BLK_B_Z, BLK_SMALL = 3, 4, 5, 6
BLK_C_QKV, BLK_C_Z = 7, 8
BLK_D_Q, BLK_D_K, BLK_D_V, BLK_D_Z = 9, 10, 11, 12
H_COLS = 13 * GROUP_W
UNIT_B_K, UNIT_B_V, UNIT_B_IK, UNIT_KR_IW = (BLK_SMALL * 4 + i for i in range(4))
IW_LANE = 32

_SRC = dict(a_u=0, a_v=512, a_z=1024, b_q=1536, b_k=2048, b_v=2112, b_iq=2176, b_ik=2688,
            b_iw=2752, b_z=2760, c_q=3272, c_kv=3656, c_kr=3784, c_z=3848,
            d_q=4360, d_k=4872, d_v=5384, d_z=5896)


def _params(n_axes):
    return pltpu.CompilerParams(dimension_semantics=("arbitrary",) * n_axes,
                                vmem_limit_bytes=VMEM_LIMIT)


def _gelu(x):
    c = math.sqrt(2.0 / math.pi)
    return x * (0.5 * (1.0 + jnp.tanh(c * (x + 0.044715 * (x * x * x)))))


def _silu(x):
    return x * (1.0 / (1.0 + jnp.exp(-x)))


def _dot_t(a, b):
    return lax.dot_general(a, b, (((1,), (1,)), ((), ())), preferred_element_type=F32)


def _lo_mask(rows):
    return lax.broadcasted_iota(jnp.int32, (rows, LANES), 1) < 64


def _rms_heads64(x, gain, ntiles):
    lo = _lo_mask(x.shape[0])
    tiles = []
    for t in range(ntiles):
        xt = x[:, t * LANES:(t + 1) * LANES]
        sq = xt * xt
        s_lo = jnp.sum(jnp.where(lo, sq, 0.0), axis=-1, keepdims=True)
        s_hi = jnp.sum(jnp.where(lo, 0.0, sq), axis=-1, keepdims=True)
        r = jnp.where(lo, lax.rsqrt(s_lo * (1.0 / 64) + EPS), lax.rsqrt(s_hi * (1.0 / 64) + EPS))
        tiles.append(xt * r * gain[:, t * LANES:(t + 1) * LANES])
    return tiles


def _for_blocks(blocks, count):
    def quad(i, carry):
        blocks(4 * i, 4)
        return carry

    lax.fori_loop(0, count // 4, quad, 0)
    first = (count // 4) * 4

    @pl.when((count & 2) != 0)
    def _():
        blocks(first, 2)

    @pl.when((count & 1) != 0)
    def _():
        blocks(first + (count & 2), 1)


def _toeplitz(base_row, rows, width):
    t = jnp.broadcast_to(base_row, (rows, base_row.shape[1]))
    t = pltpu.roll(t, 0, 1, stride=1, stride_axis=0)
    return t[:, :width]


def _inproj_kernel(x_ref, g_ref, w_ref, o_ref):
    x = x_ref[...]
    ms = jnp.mean(x * x, axis=-1, keepdims=True)
    xn = (x * lax.rsqrt(ms + EPS) * g_ref[...]).astype(BF16)
    for c in range(H_COLS // GROUP_W):
        cols = slice(c * GROUP_W, (c + 1) * GROUP_W)
        o_ref[:, cols] = _dot_t(xn, w_ref[cols, :]).astype(BF16)


def _inproj(x2, g, w_all, layer, tm=512):
    n, d = x2.shape
    return pl.pallas_call(
        _inproj_kernel,
        grid=(n // tm,),
        in_specs=[pl.BlockSpec((tm, d), lambda i: (i, 0)),
                  pl.BlockSpec((1, d), lambda i: (0, 0)),
                  pl.BlockSpec((None, H_COLS, d), lambda i: (layer, 0, 0), pipeline_mode=pl.Buffered(1))],
        out_specs=pl.BlockSpec((tm, H_COLS), lambda i: (i, 0)),
        out_shape=jax.ShapeDtypeStruct((n, H_COLS), BF16),
        compiler_params=_params(1),
    )(x2, g, w_all)


def _outproj_kernel(x_ref, u_ref, v_ref, z_ref, vg_ref, ws_ref, bs_ref, yb_ref, yc_ref, yd_ref, w_ref, o_ref,
                    ya_ref):
    tm, d = x_ref.shape
    wgs = _mixer_a_weights(ws_ref)
    nblk = tm // GMLP_BLOCK
    ncol = d // nblk
    accs = []
    for c in range(nblk):
        cols = slice(c * ncol, (c + 1) * ncol)
        acc = x_ref[:, cols]
        for g, y_ref in ((1, yb_ref), (2, yc_ref), (3, yd_ref)):
            acc = acc + jnp.dot(y_ref[...], w_ref[g * GROUP_W:(g + 1) * GROUP_W, cols],
                                preferred_element_type=F32)
        accs.append(acc)
        _mixer_a_rows(u_ref, v_ref, z_ref, vg_ref, wgs, bs_ref, ya_ref, c)
    for c in range(nblk):
        cols = slice(c * ncol, (c + 1) * ncol)
        o_ref[:, cols] = accs[c] + jnp.dot(ya_ref[...], w_ref[0:GROUP_W, cols], preferred_element_type=F32)


def _outproj(x2, h, vg, ws, bs, ys, w, tm=512):
    n, d = x2.shape
    yspec = pl.BlockSpec((tm, GROUP_W), lambda i: (i, 0))
    hspec = lambda blk: pl.BlockSpec((tm, GROUP_W), lambda i, blk=blk: (i, blk))
    return pl.pallas_call(
        _outproj_kernel,
        grid=(n // tm,),
        in_specs=[pl.BlockSpec((tm, d), lambda i: (i, 0)),
                  hspec(BLK_A_U), hspec(BLK_A_V), hspec(BLK_A_Z),
                  pl.BlockSpec((1, GROUP_W), lambda i: (0, 0)),
                  pl.BlockSpec((A_GROUPS, GMLP_BLOCK, GMLP_BLOCK), lambda i: (0, 0, 0)),
                  pl.BlockSpec((A_GROUPS, GMLP_BLOCK, 1), lambda i: (0, 0, 0)),
                  yspec, yspec, yspec,
                  pl.BlockSpec((4 * GROUP_W, d), lambda i: (0, 0))],
        out_specs=pl.BlockSpec((tm, d), lambda i: (i, 0)),
        out_shape=jax.ShapeDtypeStruct((n, d), F32),
        scratch_shapes=[pltpu.VMEM((tm, GROUP_W), BF16)],
        compiler_params=_params(1),
    )(x2, h, h, h, vg, ws, bs, *ys, w)


def _mixer_a_weights(w_ref):
    i = lax.broadcasted_iota(jnp.int32, (GMLP_BLOCK, GMLP_BLOCK), 0)
    j = lax.broadcasted_iota(jnp.int32, (GMLP_BLOCK, GMLP_BLOCK), 1)
    keep = (j // CHUNK) <= (i // CHUNK)
    return [jnp.where(keep, w_ref[g], 0.0).astype(BF16) for g in range(A_GROUPS)]


def _mixer_a_rows(u_ref, v_ref, z_ref, vg_ref, wgs, b_ref, o_ref, blk):
    rows = slice(blk * GMLP_BLOCK, (blk + 1) * GMLP_BLOCK)
    u = _gelu(u_ref[rows, :].astype(F32))
    v = _gelu(v_ref[rows, :].astype(F32))
    ms = jnp.mean(v * v, axis=-1, keepdims=True)
    vb = (v * lax.rsqrt(ms + EPS) * vg_ref[...]).astype(BF16)
    gate = _silu(z_ref[rows, :].astype(F32))
    for g in range(A_GROUPS):
        cols = slice(g * LANES, (g + 1) * LANES)
        sg = jnp.dot(wgs[g], vb[:, cols], preferred_element_type=F32) + b_ref[g]
        o_ref[rows, cols] = (u[:, cols] * sg * gate[:, cols]).astype(BF16)


def _order_key(x):
    return jnp.where(x < 0, x ^ 0x7FFFFFFF, x)


def _b_select_kernel(iq_ref, iw_ref, ik_ref, o_ref, lhs_ref, sc_ref, *, seq, tq, topk):
    t_blk = pl.program_id(1)
    nkb = seq // 256
    nblk = ((t_blk + 1) * tq) // 256
    n_interp = 12
    n_unchecked = 12
    hrows = [slice(h * tq, (h + 1) * tq) for h in range(N_HEADS64)]

    lo_half = _lo_mask(tq)
    w_t = (iw_ref[...].astype(F32) * IDX_SCALE).T
    for h in range(N_HEADS64):
        iqt = iq_ref[:, (h // 2) * LANES:(h // 2 + 1) * LANES]
        sel = lo_half if h % 2 == 0 else jnp.logical_not(lo_half)
        lhs_ref[hrows[h], :] = jnp.where(sel, iqt, jnp.zeros_like(iqt))

    qpos = t_blk * tq + lax.broadcasted_iota(jnp.int32, (256, tq), 1)
    krow = lax.broadcasted_iota(jnp.int32, (256, tq), 0)

    def fold8(x):
        return jnp.sum(x.reshape(256 // 8, 8, tq), axis=0)

    def score_block(kb, carry, masked):
        amax, n_pos, n_nn = carry
        off = pl.multiple_of(kb * 256, 256)
        ikblk = ik_ref[pl.ds(off, 256), :]
        score = jnp.zeros((256, tq), F32)
        for h in range(N_HEADS64):
            w_h = w_t[IW_LANE + h:IW_LANE + h + 1, :]
            score = score + w_h * jnp.maximum(_dot_t(ikblk, lhs_ref[hrows[h], :]), 0.0)
        score = jnp.where(score == 0.0, 0.0, score)
        mag = jnp.abs(score)
        if masked:
            adm = ((kb * 256 + krow) // CHUNK) <= (qpos // CHUNK)
            score = jnp.where(adm, score, -jnp.inf)
            mag = jnp.where(adm, mag, 0.0)
        sc_ref[kb] = score
        return (jnp.maximum(amax, jnp.max(mag.reshape(256 // 8, 8, tq), axis=0)),
                n_pos + fold8(jnp.where(score > 0.0, 1.0, 0.0)), n_nn + fold8(jnp.where(score >= 0.0, 1.0, 0.0)))

    nfull = (t_blk * tq) // 256
    zeros8 = jnp.zeros((8, tq), F32)
    carry = lax.fori_loop(0, nfull, lambda kb, c: score_block(kb, c, False), (zeros8, zeros8, zeros8))
    carry = lax.fori_loop(nfull, nblk, lambda kb, c: score_block(kb, c, True), carry)
    amax = jnp.max(carry[0], axis=0, keepdims=True)
    f_pos = jnp.sum(carry[1], axis=0, keepdims=True)
    f_nn = jnp.sum(carry[2], axis=0, keepdims=True)

    def count(pred):
        def body(kb, acc):
            return acc + fold8(jnp.where(pred(sc_ref[kb], kb), 1.0, 0.0))
        return jnp.sum(lax.fori_loop(0, nblk, body, zeros8), axis=0, keepdims=True)

    kf = float(topk)
    qrow = t_blk * tq + lax.broadcasted_iota(jnp.int32, (1, tq), 1)
    n_adm = ((qrow // CHUNK + 1) * CHUNK).astype(F32)
    one = jnp.ones((1, tq), jnp.int32)
    pos = f_pos > kf
    neg = f_nn < kf
    lo0 = jnp.where(pos, one, _order_key(lax.bitcast_convert_type(-amax, jnp.int32)))
    hi0 = jnp.where(neg, one - 1, _order_key(lax.bitcast_convert_type(amax, jnp.int32)) + 1)
    w_lo0 = jnp.where(pos, f_pos, n_adm) - kf
    w_hi0 = kf - jnp.where(neg, f_nn, 0.0)
    all_sel = n_adm <= kf
    at_zero = jnp.logical_not(pos | neg)
    done0 = jnp.where(all_sel | at_zero | (hi0 == lo0 + 1), 1.0, 0.0)
    thr0 = jnp.where(all_sel, KEY_LOWEST, jnp.where(at_zero, jnp.where(f_pos == kf, one, one - 1), lo0))

    def as_score(key):
        return lax.bitcast_convert_type(_order_key(key), F32)

    def search_cond(st):
        return jnp.logical_and(st[0][0] < n_interp + 32, st[1] < 0.5)

    def search_step(st):
        it, lo, hi, w_lo, w_hi, side, done, thr = st
        lo_v = as_score(lo)
        hi_v = as_score(hi)
        c_v = lo_v + (hi_v - lo_v) * (w_lo / (w_lo + w_hi))
        c_interp = _order_key(lax.bitcast_convert_type(c_v, jnp.int32))
        c_mid = (lo >> 1) + (hi >> 1) + (lo & hi & 1)
        cand = jnp.where(it < n_interp, c_interp, c_mid)
        cand = jnp.minimum(jnp.maximum(cand, lo + 1), hi - 1)
        cand_v = as_score(cand)
        f = count(lambda s, kb: s >= cand_v)
        live = done < 0.5
        up = f > kf
        hit = f == kf
        new_lo = jnp.where(live & up, cand, lo)
        new_hi = jnp.where(live & jnp.logical_not(up), cand, hi)
        new_w_lo = jnp.where(up, f - kf, jnp.where(side < 0.0, 0.5 * w_lo, w_lo))
        new_w_hi = jnp.where(up, jnp.where(side > 0.0, 0.5 * w_hi, w_hi), kf - f)
        new_side = jnp.where(up, 1.0, -1.0)
        new_thr = jnp.where(live, jnp.where(hit, cand, new_lo), thr)
        new_done = jnp.where(live & (hit | (new_hi == new_lo + 1)), 1.0, done)
        return (it + 1, new_lo, new_hi, jnp.where(live, new_w_lo, w_lo),
                jnp.where(live, new_w_hi, w_hi), jnp.where(live, new_side, side), new_done, new_thr)

    def checked_step(st):
        new = search_step(st[0])
        return new, jnp.min(new[-2])

    state = (jnp.int32(0), lo0, hi0, w_lo0, w_hi0, jnp.zeros((1, tq), F32), done0, thr0)
    state = lax.fori_loop(0, n_unchecked, lambda i, st: search_step(st), state)
    thr = as_score(lax.while_loop(search_cond, checked_step, (state, jnp.min(state[-2])))[0][-1])

    any_excess = jnp.max(count(lambda s, kb: s >= thr)) > kf

    def store_mask(kb, keep_t):
        for g in range(tq // 256):
            o_ref[g, kb] = keep_t[:, g * 256:(g + 1) * 256].T

    def write_unused(kb, carry):
        store_mask(kb, jnp.zeros((256, tq), F32))
        return carry

    lax.fori_loop(nblk, nkb, write_unused, 0)

    @pl.when(jnp.logical_not(any_excess))
    def _():
        def write(kb, carry):
            store_mask(kb, jnp.where(sc_ref[kb] >= thr, 1.0, 0.0))
            return carry

        lax.fori_loop(0, nblk, write, 0)

    @pl.when(any_excess)
    def _():
        need = kf - count(lambda s, kb: s > thr)

        def idx_step(it, jmax):
            cand = jmax | lax.shift_left(jnp.int32(1), 10 - it)
            below = count(lambda s, kb: (s == thr) & ((kb * 256 + krow) < cand))
            return jnp.where(below < need, cand, jmax)

        jmax = lax.fori_loop(0, 11, idx_step, jnp.zeros((1, tq), jnp.int32))

        def write(kb, carry):
            s = sc_ref[kb]
            keep_tie = (s == thr) & ((kb * 256 + krow) <= jmax)
            store_mask(kb, jnp.where((s > thr) | keep_tie, 1.0, 0.0))
            return carry

        lax.fori_loop(0, nblk, write, 0)


def _b_select(h, bsz, seq, tq=512):
    tq = min(tq, seq)
    nt = seq // tq
    topk = min(TOPK_MAX, seq // 4)
    kern = functools.partial(_b_select_kernel, seq=seq, tq=tq, topk=topk)
    return pl.pallas_call(
        kern,
        grid=(bsz, nt),
        in_specs=[pl.BlockSpec((tq, GROUP_W), lambda b, t: (b * nt + t, BLK_B_IQ)),
                  pl.BlockSpec((tq, LANES), lambda b, t: (b * nt + t, UNIT_KR_IW)),
                  pl.BlockSpec((seq, LANES), lambda b, t: (b, UNIT_B_IK))],
        out_specs=pl.BlockSpec((tq // 256, seq // 256, 256, 256), lambda b, t: (b * nt + t, 0, 0, 0)),
        out_shape=jax.ShapeDtypeStruct((bsz * seq // 256, seq // 256, 256, 256), F32),
        scratch_shapes=[pltpu.VMEM((N_HEADS64 * tq, LANES), BF16),
                        pltpu.VMEM((seq // 256, 256, tq), F32)],
        compiler_params=_params(2),
    )(h, h, h)


def _b_attn_kernel(cfar_ref, q_ref, z_ref, k_ref, v_ref, msk_ref, qg_ref, kg_ref, base0_ref, base1_ref,
                   o_ref, kn_ref, v1_ref, bias_ref, qall_ref, s_ref, mp_ref, acc_ref, *, seq, tq):
    b = pl.program_id(0)
    t_blk = pl.program_id(1)
    hrows = [slice(h * tq, (h + 1) * tq) for h in range(N_HEADS64)]

    @pl.when((b == 0) & (t_blk == 0))
    def _():
        for h in range(N_HEADS64):
            bias_ref[0, hrows[h], :] = jnp.full((tq, 256), cfar_ref[h] * LOG2E, F32)
            bias_ref[1, hrows[h], :] = _toeplitz(base1_ref[h:h + 1, :], tq, 256) * LOG2E
            bias_ref[2, hrows[h], :] = _toeplitz(base0_ref[h:h + 1, :], tq, 256) * LOG2E

    @pl.when(t_blk == 0)
    def _():
        lo256 = _lo_mask(256)
        for r in range(seq // 256):
            rows = slice(r * 256, (r + 1) * 256)
            k = k_ref[rows, :].astype(F32)
            ms = jnp.mean(k * k, axis=-1, keepdims=True)
            kn_ref[rows, :] = (k * lax.rsqrt(ms + EPS) * kg_ref[...]).astype(BF16)
            v = v_ref[rows, :]
            v1_ref[rows, :] = jnp.where(lo256, v, jnp.ones_like(v))

    lo = _lo_mask(tq)
    qtiles = _rms_heads64(q_ref[...].astype(F32), qg_ref[...], 4)
    for h in range(N_HEADS64):
        sel = lo if h % 2 == 0 else jnp.logical_not(lo)
        qall_ref[hrows[h], :] = jnp.where(sel, qtiles[h // 2] * (0.125 * LOG2E), 0.0).astype(BF16)
    mp_ref[...] = jnp.full(mp_ref.shape, NEG, F32)
    acc_ref[...] = jnp.zeros(acc_ref.shape, F32)
    nblk = t_blk + 1

    def logits_blocks(kb0, n):
        for i in range(n):
            kb = kb0 + i
            off = pl.multiple_of(kb * 256, 256)
            kblk = kn_ref[pl.ds(off, 256), :]
            which = jnp.clip(kb - (t_blk - 2), 0, 2)
            keep = msk_ref[0, kb] > 0.5
            for h in range(N_HEADS64):
                s = jnp.where(keep, _dot_t(qall_ref[hrows[h], :], kblk) + bias_ref[which, hrows[h], :], NEG)
                s_ref[kb, hrows[h], :] = s
                mp_ref[hrows[h], :] = jnp.maximum(mp_ref[hrows[h], :], jnp.maximum(s[:, :LANES], s[:, LANES:]))

    def value_blocks(kb0, n):
        off = pl.multiple_of(kb0 * 256, 256)
        v1 = v1_ref[pl.ds(off, n * 256), :]
        for h in range(N_HEADS64):
            m = mp_ref[hrows[h], :]
            mm = jnp.concatenate([m, m], axis=1)
            p = [jnp.exp2(s_ref[kb0 + i, hrows[h], :] - mm).astype(BF16) for i in range(n)]
            p = p[0] if n == 1 else jnp.concatenate(p, axis=1)
            acc_ref[hrows[h], :] += jnp.dot(p, v1, preferred_element_type=F32)

    _for_blocks(logits_blocks, nblk)
    for h in range(N_HEADS64):
        m = jnp.max(mp_ref[hrows[h], :], axis=-1, keepdims=True)
        mp_ref[hrows[h], :] = jnp.broadcast_to(m, (tq, LANES))
    _for_blocks(value_blocks, nblk)

    gate = _silu(z_ref[...].astype(F32))
    for t in range(4):
        a_even = acc_ref[hrows[2 * t], :]
        a_odd = acc_ref[hrows[2 * t + 1], :]
        o_even = a_even / pltpu.roll(a_even, 64, 1)
        o_odd = pltpu.roll(a_odd, 64, 1) / a_odd
        cols = slice(t * LANES, (t + 1) * LANES)
        o_ref[:, cols] = (jnp.where(lo, o_even, o_odd) * gate[:, cols]).astype(BF16)


def _b_attn(h, mask, cfar, qg, kg, base0, base1, bsz, seq, tq=256):
    nt = seq // tq
    rows = N_HEADS64 * tq
    kern = functools.partial(_b_attn_kernel, seq=seq, tq=tq)
    full = lambda shape: pl.BlockSpec(shape, lambda b, t: (0,) * len(shape))
    return pl.pallas_call(
        kern,
        grid=(bsz, nt),
        in_specs=[pl.BlockSpec(memory_space=pltpu.SMEM),
                  pl.BlockSpec((tq, GROUP_W), lambda b, t: (b * nt + t, BLK_B_Q)),
                  pl.BlockSpec((tq, GROUP_W), lambda b, t: (b * nt + t, BLK_B_Z)),
                  pl.BlockSpec((seq, LANES), lambda b, t: (b, UNIT_B_K)),
                  pl.BlockSpec((seq, LANES), lambda b, t: (b, UNIT_B_V)),
                  pl.BlockSpec((1, seq // 256, tq, 256), lambda b, t: (b * nt + t, 0, 0, 0)),
                  full((1, GROUP_W)), full((1, LANES)), full((N_HEADS64, 512)), full((N_HEADS64, 512))],
        out_specs=pl.BlockSpec((tq, GROUP_W), lambda b, t: (b * nt + t, 0)),
        out_shape=jax.ShapeDtypeStruct((bsz * seq, GROUP_W), BF16),
        scratch_shapes=[pltpu.VMEM((seq, LANES), BF16), pltpu.VMEM((seq, LANES), BF16),
                        pltpu.VMEM((3, rows, 256), F32), pltpu.VMEM((rows, LANES), BF16),
                        pltpu.VMEM((seq // 256, rows, 256), F32),
                        pltpu.VMEM((rows, LANES), F32), pltpu.VMEM((rows, LANES), F32)],
        compiler_params=_params(2),
    )(cfar, h, h, h, h, mask, qg, kg, base0, base1)


def _rope(tile, cos, sin):
    return tile * cos + pltpu.roll(tile, 64, 1) * sin


def _c_prep_kernel(lat_ref, kr_ref, wq_ref, wkv_ref, qag_ref, kvag_ref, qg_ref, kg_ref, cos_ref, sin_ref,
                   qo_ref, ko_ref, vo_ref):
    cq = lat_ref[:, :Q_LORA].astype(F32)
    ms = jnp.mean(cq * cq, axis=-1, keepdims=True)
    cqn = (cq * lax.rsqrt(ms + EPS) * qag_ref[...]).astype(BF16)
    qpre = jnp.dot(cqn, wq_ref[...], preferred_element_type=F32)
    ckv = lat_ref[:, Q_LORA:].astype(F32)
    ms = jnp.mean(ckv * ckv, axis=-1, keepdims=True)
    ckvn = (ckv * lax.rsqrt(ms + EPS) * kvag_ref[...]).astype(BF16)
    kvpre = jnp.dot(ckvn, wkv_ref[...], preferred_element_type=F32)
    lane = lax.broadcasted_iota(jnp.int32, kr_ref.shape, 1)
    kr = jnp.where((lane % 64) < 32, kr_ref[...].astype(F32), 0.0)
    kr_ss = jnp.sum(kr * kr, axis=-1, keepdims=True)
    cos = cos_ref[...]
    sin = sin_ref[...]
    qg = qg_ref[...]
    kg = kg_ref[...]
    kr_rot = _rope(kr * kg[:, LANES:], cos, sin)
    for h in range(C_HEADS):
        qh = qpre[:, h * 256:(h + 1) * 256]
        r = lax.rsqrt(jnp.sum(qh * qh, axis=-1, keepdims=True) * (1.0 / C_QK) + EPS)
        qn = qh * r * qg
        qo_ref[:, h * 256:h * 256 + LANES] = qn[:, :LANES].astype(BF16)
        qo_ref[:, h * 256 + LANES:(h + 1) * 256] = _rope(qn[:, LANES:], cos, sin).astype(BF16)
        kn = kvpre[:, h * LANES:(h + 1) * LANES]
        r = lax.rsqrt((jnp.sum(kn * kn, axis=-1, keepdims=True) + kr_ss) * (1.0 / C_QK) + EPS)
        ko_ref[:, h * 256:h * 256 + LANES] = (kn * r * kg[:, :LANES]).astype(BF16)
        ko_ref[:, h * 256 + LANES:(h + 1) * 256] = (kr_rot * r).astype(BF16)
    vo_ref[...] = kvpre[:, C_HEADS * LANES:].astype(BF16)


def _c_prep(h, wq, wkv, qag, kvag, qg, kg, cos, sin, seq, tm=512):
    n = h.shape[0]
    ns = seq // tm
    full = lambda shape: pl.BlockSpec(shape, lambda i: (0,) * len(shape))
    return pl.pallas_call(
        _c_prep_kernel,
        grid=(n // tm,),
        in_specs=[pl.BlockSpec((tm, GROUP_W), lambda i: (i, BLK_C_QKV)),
                  pl.BlockSpec((tm, LANES), lambda i: (i, UNIT_KR_IW)),
                  full((Q_LORA, 4 * 256)), full((KV_LORA, 8 * LANES)),
                  full((1, Q_LORA)), full((1, LANES)), full((1, 256)), full((1, 256)),
                  pl.BlockSpec((tm, LANES), lambda i: (i % ns, 0)),
                  pl.BlockSpec((tm, LANES), lambda i: (i % ns, 0))],
        out_specs=[pl.BlockSpec((tm, 4 * 256), lambda i: (i, 0)),
                   pl.BlockSpec((tm, 4 * 256), lambda i: (i, 0)),
                   pl.BlockSpec((tm, GROUP_W), lambda i: (i, 0))],
        out_shape=[jax.ShapeDtypeStruct((n, 4 * 256), BF16), jax.ShapeDtypeStruct((n, 4 * 256), BF16),
                   jax.ShapeDtypeStruct((n, GROUP_W), BF16)],
        compiler_params=_params(1),
    )(h, h, wq, wkv, qag, kvag, qg, kg, cos, sin)


def _c_attn_kernel(q_ref, k_ref, v_ref, z_ref, o_ref, s_ref, mp_ref, lp_ref, acc_ref, *, tq):
    qt = pl.program_id(1)
    scale = C_QK ** -0.5 * LOG2E
    hrows = [slice(h * tq, (h + 1) * tq) for h in range(C_HEADS)]
    qchunk = (qt * tq + lax.broadcasted_iota(jnp.int32, (tq, 256), 0)) // CHUNK
    kcol = lax.broadcasted_iota(jnp.int32, (tq, 256), 1)
    mp_ref[...] = jnp.full(mp_ref.shape, NEG, F32)
    lp_ref[...] = jnp.zeros(lp_ref.shape, F32)
    acc_ref[...] = jnp.zeros(acc_ref.shape, F32)
    nfull = (qt * tq) // 256
    nblk = ((qt + 1) * tq) // 256

    def logits_blocks(kb0, n, masked):
        for i in range(n):
            kb = kb0 + i
            off = pl.multiple_of(kb * 256, 256)
            for h in range(C_HEADS):
                cols = slice(h * 256, (h + 1) * 256)
                s = _dot_t(q_ref[:, cols], k_ref[pl.ds(off, 256), cols]) * scale
                if masked:
                    s = jnp.where(((kb * 256 + kcol) // CHUNK) <= qchunk, s, NEG)
                s_ref[kb, hrows[h], :] = s
                mp_ref[hrows[h], :] = jnp.maximum(mp_ref[hrows[h], :], jnp.maximum(s[:, :LANES], s[:, LANES:]))

    def value_blocks(kb0, n):
        off = pl.multiple_of(kb0 * 256, 256)
        for h in range(C_HEADS):
            m = mp_ref[hrows[h], :]
            mm = jnp.concatenate([m, m], axis=1)
            p = [jnp.exp2(s_ref[kb0 + i, hrows[h], :] - mm) for i in range(n)]
            lsum = p[0][:, :LANES] + p[0][:, LANES:]
            for pi in p[1:]:
                lsum = lsum + pi[:, :LANES] + pi[:, LANES:]
            lp_ref[hrows[h], :] += lsum
            pb = p[0].astype(BF16) if n == 1 else jnp.concatenate([pi.astype(BF16) for pi in p], axis=1)
            acc_ref[hrows[h], :] += jnp.dot(pb, v_ref[pl.ds(off, n * 256), h * LANES:(h + 1) * LANES],
                                            preferred_element_type=F32)

    _for_blocks(lambda kb0, n: logits_blocks(kb0, n, False), nfull)
    lax.fori_loop(nfull, nblk, lambda kb, c: (logits_blocks(kb, 1, True), c)[1], 0)
    for h in range(C_HEADS):
        m = jnp.max(mp_ref[hrows[h], :], axis=-1, keepdims=True)
        mp_ref[hrows[h], :] = jnp.broadcast_to(m, (tq, LANES))
    _for_blocks(value_blocks, nblk)
    gate = _silu(z_ref[...].astype(F32))
    for h in range(C_HEADS):
        cols = slice(h * LANES, (h + 1) * LANES)
        l = jnp.sum(lp_ref[hrows[h], :], axis=-1, keepdims=True)
        o_ref[:, cols] = (acc_ref[hrows[h], :] / l * gate[:, cols]).astype(BF16)


def _c_attn(qc, kc, vc, h, bsz, seq, tq=256):
    nt = seq // tq
    rows = C_HEADS * tq
    kern = functools.partial(_c_attn_kernel, tq=tq)
    return pl.pallas_call(
        kern,
        grid=(bsz, nt),
        in_specs=[pl.BlockSpec((tq, C_HEADS * 256), lambda b, t: (b * nt + t, 0)),
                  pl.BlockSpec((seq, C_HEADS * 256), lambda b, t: (b, 0)),
                  pl.BlockSpec((seq, GROUP_W), lambda b, t: (b, 0)),
                  pl.BlockSpec((tq, GROUP_W), lambda b, t: (b * nt + t, BLK_C_Z))],
        out_specs=pl.BlockSpec((tq, GROUP_W), lambda b, t: (b * nt + t, 0)),
        out_shape=jax.ShapeDtypeStruct((bsz * seq, GROUP_W), BF16),
        scratch_shapes=[pltpu.VMEM((seq // 256, rows, 256), F32), pltpu.VMEM((rows, LANES), F32),
                        pltpu.VMEM((rows, LANES), F32), pltpu.VMEM((rows, LANES), F32)],
        compiler_params=_params(2),
    )(qc, kc, vc, h)


def _mixer_d_kernel(q_ref, k_ref, v_ref, z_ref, qg_ref, kg_ref, base_ref, o_ref,
                    kpad_ref, vpad_ref, bias_ref, s_ref, mp_ref, *, seq, tq):
    b = pl.program_id(0)
    qt = pl.program_id(1)
    win = tq + D_LEFT

    @pl.when((b == 0) & (qt == 0))
    def _():
        qc = lax.broadcasted_iota(jnp.int32, (tq, win), 0) // CHUNK
        kc = lax.broadcasted_iota(jnp.int32, (tq, win), 1) // CHUNK
        band = (kc >= qc) & (kc <= qc + D_LEFT // CHUNK)
        for h in range(N_HEADS64):
            bias_ref[h] = jnp.where(band, _toeplitz(base_ref[h:h + 1, :], tq, win) * LOG2E, NEG)

    @pl.when(qt == 0)
    def _():
        kpad_ref[0:D_LEFT, :] = jnp.zeros((D_LEFT, GROUP_W), BF16)
        vpad_ref[0:D_LEFT, :] = jnp.zeros((D_LEFT, GROUP_W), BF16)
        for r in range(seq // 256):
            rows = slice(r * 256, (r + 1) * 256)
            dst = slice(D_LEFT + r * 256, D_LEFT + (r + 1) * 256)
            tiles = _rms_heads64(k_ref[rows, :].astype(F32), kg_ref[...], 4)
            for t in range(4):
                kpad_ref[dst, t * LANES:(t + 1) * LANES] = tiles[t].astype(BF16)
            vpad_ref[dst, :] = v_ref[rows, :]

    lo = _lo_mask(tq)
    qtiles = _rms_heads64(q_ref[...].astype(F32), qg_ref[...], 4)
    start = pl.multiple_of(qt * tq, tq)
    gate = _silu(z_ref[...].astype(F32))
    ntile = win // LANES

    def lane_tiles(x):
        return [x[:, i * LANES:(i + 1) * LANES] for i in range(ntile)]

    in_seq = lax.broadcasted_iota(jnp.int32, (tq, win), 1) + start >= D_LEFT

    def logits(h):
        t, half = divmod(h, 2)
        kwin = kpad_ref[pl.ds(start, win), t * LANES:(t + 1) * LANES]
        sel = lo if half == 0 else jnp.logical_not(lo)
        qh = jnp.where(sel, qtiles[t] * (0.125 * LOG2E), 0.0).astype(BF16)
        s = jnp.where(in_seq, _dot_t(qh, kwin) + bias_ref[h], NEG)
        s_ref[h] = s
        m = jnp.max(functools.reduce(jnp.maximum, lane_tiles(s)), axis=-1, keepdims=True)
        mp_ref[h] = jnp.broadcast_to(m, (tq, LANES))

    def values(h):
        vwin = vpad_ref[pl.ds(start, win), (h // 2) * LANES:(h // 2 + 1) * LANES]
        m = mp_ref[h]
        p = jnp.exp2(s_ref[h] - jnp.concatenate([m] * ntile, axis=1))
        l = jnp.sum(functools.reduce(jnp.add, lane_tiles(p)), axis=-1, keepdims=True)
        return jnp.dot(p.astype(BF16), vwin, preferred_element_type=F32) / l

    outs = []
    logits(0)
    for h in range(1, N_HEADS64 + 1):
        if h < N_HEADS64:
            logits(h)
        outs.append(values(h - 1))
        if h % 2 == 0:
            cols = slice((h // 2 - 1) * LANES, (h // 2) * LANES)
            o_ref[:, cols] = (jnp.where(lo, outs[h - 2], outs[h - 1]) * gate[:, cols]).astype(BF16)


def _mixer_d(h, qg, kg, base, bsz, seq, tq=256):
    nt = seq // tq
    kern = functools.partial(_mixer_d_kernel, seq=seq, tq=tq)
    full = lambda shape: pl.BlockSpec(shape, lambda b, t: (0,) * len(shape))
    return pl.pallas_call(
        kern,
        grid=(bsz, nt),
        in_specs=[pl.BlockSpec((tq, GROUP_W), lambda b, t: (b * nt + t, BLK_D_Q)),
                  pl.BlockSpec((seq, GROUP_W), lambda b, t: (b, BLK_D_K)),
                  pl.BlockSpec((seq, GROUP_W), lambda b, t: (b, BLK_D_V)),
                  pl.BlockSpec((tq, GROUP_W), lambda b, t: (b * nt + t, BLK_D_Z)),
                  full((1, GROUP_W)), full((1, GROUP_W)), full((N_HEADS64, 2 * tq + D_LEFT))],
        out_specs=pl.BlockSpec((tq, GROUP_W), lambda b, t: (b * nt + t, 0)),
        out_shape=jax.ShapeDtypeStruct((bsz * seq, GROUP_W), BF16),
        scratch_shapes=[pltpu.VMEM((seq + D_LEFT, GROUP_W), BF16), pltpu.VMEM((seq + D_LEFT, GROUP_W), BF16),
                        pltpu.VMEM((N_HEADS64, tq, tq + D_LEFT), F32),
                        pltpu.VMEM((N_HEADS64, tq, tq + D_LEFT), F32), pltpu.VMEM((N_HEADS64, tq, LANES), F32)],
        compiler_params=_params(2),
    )(h, h, h, h, qg, kg, base)


def _w_in_pieces(take, zeros):
    c = lambda name, size, off=0: take(_SRC[name] + off, size)
    return [c("a_u", 512), c("a_v", 512), c("a_z", 512),
            c("b_q", 512), c("b_iq", 512), c("b_z", 512),
            c("b_k", 64), c("b_k", 64), c("b_v", 64), c("b_v", 64), c("b_ik", 64), c("b_ik", 64),
            c("c_kr", 32), c("b_iw", 8), zeros(24), c("c_kr", 32, 32), zeros(32),
            c("c_q", 384), c("c_kv", 128), c("c_z", 512),
            c("d_q", 512), c("d_k", 512), c("d_v", 512), c("d_z", 512)]


def _layout_w_in_kernel(w_ref, o_ref):
    tk = w_ref.shape[2]
    pieces = _w_in_pieces(lambda s, n: w_ref[0, s:s + n, :], lambda n: jnp.zeros((n, tk), F32))
    ends = np.cumsum([0] + [p.shape[0] for p in pieces])
    start = 0
    for i in range(1, len(pieces) + 1):
        if ends[i] % GROUP_W == 0:
            group = pieces[start:i]
            blk = group[0] if len(group) == 1 else jnp.concatenate(group, axis=0)
            o_ref[0, ends[start]:ends[i], :] = blk.astype(BF16)
            start = i


def _layout_w_in(w_in, tk=256):
    w_t = jnp.swapaxes(w_in, 1, 2)
    depth, cols, d = w_t.shape
    return pl.pallas_call(
        _layout_w_in_kernel,
        grid=(depth, d // tk),
        in_specs=[pl.BlockSpec((1, cols, tk), lambda l, i: (l, 0, i))],
        out_specs=pl.BlockSpec((1, H_COLS, tk), lambda l, i: (l, 0, i)),
        out_shape=jax.ShapeDtypeStruct((depth, H_COLS, d), BF16),
        compiler_params=_params(2),
    )(w_t)


def _rope_layout(v):
    z = jnp.zeros(v.shape[:-1] + (32,), v.dtype)
    return jnp.concatenate([v[..., :32], z, v[..., 32:], z], axis=-1)


def _layout_c(w_qb, w_kvb, q_gain, k_gain, qa_gain):
    wq = w_qb.reshape(Q_LORA, C_HEADS, C_QK)
    wq = jnp.concatenate([wq[..., :C_NOPE], _rope_layout(wq[..., C_NOPE:])], axis=-1)
    wq = wq.reshape(Q_LORA, C_HEADS * 256).astype(BF16)
    wkv = w_kvb.reshape(KV_LORA, C_HEADS, 2 * LANES)
    wkv = jnp.concatenate([wkv[..., :C_NOPE].reshape(KV_LORA, -1), wkv[..., C_NOPE:].reshape(KV_LORA, -1)],
                          axis=1).astype(BF16)
    lay = lambda g: jnp.concatenate([g[:C_NOPE], _rope_layout(g[C_NOPE:])])[None, :]
    return wq, wkv, lay(q_gain), lay(k_gain), qa_gain[None, :]


def _t5_bucket_static(rel):
    half = T5_BUCKETS // 2
    exact = half // 2
    n = abs(rel)
    if n < exact:
        val = n
    else:
        val = min(exact + (n * n // (exact * exact)).bit_length() - 1, half - 1)
    return (half if rel > 0 else 0) + val


def _t5_tables(t5_bias):
    m = np.arange(512)
    d0 = np.where(m < 256, m, m - 512)
    d1 = np.where(m <= 256, m - 256, m - 768)
    idx0 = np.array([_t5_bucket_static(int(d)) for d in d0], np.int32)
    idx1 = np.array([_t5_bucket_static(int(d)) for d in d1], np.int32)
    far = _t5_bucket_static(-512)
    return t5_bias[idx0].T, t5_bias[idx1].T, t5_bias[far]


def _band_table(rel_bias, tq):
    width = 2 * tq + D_LEFT
    m = np.arange(width)
    dist = np.where(m <= tq + D_LEFT, D_LEFT - m, D_LEFT + width - m)
    idx = np.clip(dist, -REL_CLIP, REL_CLIP) + REL_CLIP
    return rel_bias[idx.astype(np.int32)].T


def _rope_tables(seq):
    inv = ROPE_BASE ** (-jnp.arange(0, C_ROPE, 2, dtype=F32) / C_ROPE)
    ang = jnp.arange(seq, dtype=F32)[:, None] * inv[None, :]
    c, s = jnp.cos(ang), jnp.sin(ang)
    z = jnp.zeros_like(c)
    return jnp.concatenate([c, z, c, z], axis=1), jnp.concatenate([-s, z, s, z], axis=1)


def kernel(x, t5_bias, norm_g, w_in, a_v_gain, a_ws, a_bs, b_q_gain, b_k_gain, c_qa_gain, c_kva_gain,
           c_w_qb, c_w_kvb, c_q_gain, c_k_gain, d_q_gain, d_k_gain, d_rel_bias, w_out):
    bsz, seq, d_model = x.shape
    depth = w_in.shape[0]
    tq = 256
    x2 = x.reshape(bsz * seq, d_model)
    cos, sin = _rope_tables(seq)
    base0, base1, cfar = _t5_tables(t5_bias)
    w_in_blocks = _layout_w_in(w_in)
    for l in range(depth):
        h = _inproj(x2, norm_g[l][None, :], w_in_blocks, l)
        mask = _b_select(h, bsz, seq)
        y_b = _b_attn(h, mask, cfar, jnp.tile(b_q_gain[l], N_HEADS64)[None, :],
                      jnp.tile(b_k_gain[l], 2)[None, :], base0, base1, bsz, seq, tq)
        wq, wkv, qg, kg, qag = _layout_c(c_w_qb[l], c_w_kvb[l], c_q_gain[l], c_k_gain[l], c_qa_gain[l])
        qc, kc, vc = _c_prep(h, wq, wkv, qag, c_kva_gain[l][None, :], qg, kg, cos, sin, seq)
        y_c = _c_attn(qc, kc, vc, h, bsz, seq, tq)
        y_d = _mixer_d(h, jnp.tile(d_q_gain[l], N_HEADS64)[None, :], jnp.tile(d_k_gain[l], N_HEADS64)[None, :],
                       _band_table(d_rel_bias[l], tq), bsz, seq, tq)
        x2 = _outproj(x2, h, a_v_gain[l][None, :], a_ws[l], a_bs[l][:, :, None], (y_b, y_c, y_d),
                      w_out[l].astype(BF16))
    return x2.reshape(bsz, seq, d_model)
```

```python
import functools
import math

import numpy as np
import jax
import jax.numpy as jnp
from jax import lax
from jax.experimental import pallas as pl
from jax.experimental.pallas import tpu as pltpu

F32 = jnp.float32
BF16 = jnp.bfloat16

EPS = 1e-6
NEG = -1e30
LOG2E = math.log2(math.e)
KEY_LOWEST = int(np.float32(-np.finfo(np.float32).max).view(np.int32)) ^ 0x7FFFFFFF
CHUNK = 64
LANES = 128
GROUP_W = 512
A_GROUPS = 4
GMLP_BLOCK = 128
N_HEADS64 = 8
IDX_SCALE = (8 ** -0.5) * 0.125
TOPK_MAX = 256
T5_BUCKETS = 32
C_HEADS = 4
C_NOPE = 128
C_ROPE = 64
C_QK = 192
Q_LORA = 384
KV_LORA = 128
ROPE_BASE = 10000.0
D_LEFT = 8 * CHUNK
REL_CLIP = 128
VMEM_LIMIT = 56 * 1024 * 1024

BLK_A_U, BLK_A_V, BLK_A_Z = 0, 1, 2
BLK_B_Q, BLK_B_IQ, BLK_B_Z, BLK_SMALL = 3, 4, 5, 6
BLK_C_QKV, BLK_C_Z = 7, 8
BLK_D_Q, BLK_D_K, BLK_D_V, BLK_D_Z = 9, 10, 11, 12
H_COLS = 13 * GROUP_W
UNIT_B_K, UNIT_B_V, UNIT_B_IK, UNIT_KR_IW = (BLK_SMALL * 4 + i for i in range(4))
IW_LANE = 32

_SRC = dict(a_u=0, a_v=512, a_z=1024, b_q=1536, b_k=2048, b_v=2112, b_iq=2176, b_ik=2688,
            b_iw=2752, b_z=2760, c_q=3272, c_kv=3656, c_kr=3784, c_z=3848,
            d_q=4360, d_k=4872, d_v=5384, d_z=5896)


def _params(n_axes):
    return pltpu.CompilerParams(dimension_semantics=("arbitrary",) * n_axes,
                                vmem_limit_bytes=VMEM_LIMIT)


def _gelu(x):
    c = math.sqrt(2.0 / math.pi)
    return x * (0.5 * (1.0 + jnp.tanh(c * (x + 0.044715 * (x * x * x)))))


def _silu(x):
    return x * (1.0 / (1.0 + jnp.exp(-x)))


def _dot_t(a, b):
    return lax.dot_general(a, b, (((1,), (1,)), ((), ())), preferred_element_type=F32)


def _lo_mask(rows):
    return lax.broadcasted_iota(jnp.int32, (rows, LANES), 1) < 64


def _rms_heads64(x, gain, ntiles):
    lo = _lo_mask(x.shape[0])
    tiles = []
    for t in range(ntiles):
        xt = x[:, t * LANES:(t + 1) * LANES]
        sq = xt * xt
        s_lo = jnp.sum(jnp.where(lo, sq, 0.0), axis=-1, keepdims=True)
        s_hi = jnp.sum(jnp.where(lo, 0.0, sq), axis=-1, keepdims=True)
        r = jnp.where(lo, lax.rsqrt(s_lo * (1.0 / 64) + EPS), lax.rsqrt(s_hi * (1.0 / 64) + EPS))
        tiles.append(xt * r * gain[:, t * LANES:(t + 1) * LANES])
    return tiles


def _for_blocks(blocks, count):
    def quad(i, carry):
        blocks(4 * i, 4)
        return carry

    lax.fori_loop(0, count // 4, quad, 0)
    first = (count // 4) * 4

    @pl.when((count & 2) != 0)
    def _():
        blocks(first, 2)

    @pl.when((count & 1) != 0)
    def _():
        blocks(first + (count & 2), 1)


def _toeplitz(base_row, rows, width):
    t = jnp.broadcast_to(base_row, (rows, base_row.shape[1]))
    t = pltpu.roll(t, 0, 1, stride=1, stride_axis=0)
    return t[:, :width]


def _inproj_kernel(x_ref, g_ref, w_ref, o_ref):
    x = x_ref[...]
    ms = jnp.mean(x * x, axis=-1, keepdims=True)
    xn = (x * lax.rsqrt(ms + EPS) * g_ref[...]).astype(BF16)
    for c in range(H_COLS // GROUP_W):
        cols = slice(c * GROUP_W, (c + 1) * GROUP_W)
        o_ref[:, cols] = _dot_t(xn, w_ref[cols, :]).astype(BF16)


def _inproj(x2, g, w_all, layer, tm=512):
    n, d = x2.shape
    return pl.pallas_call(
        _inproj_kernel,
        grid=(n // tm,),
        in_specs=[pl.BlockSpec((tm, d), lambda i: (i, 0)),
                  pl.BlockSpec((1, d), lambda i: (0, 0)),
                  pl.BlockSpec((None, H_COLS, d), lambda i: (layer, 0, 0), pipeline_mode=pl.Buffered(1))],
        out_specs=pl.BlockSpec((tm, H_COLS), lambda i: (i, 0)),
        out_shape=jax.ShapeDtypeStruct((n, H_COLS), BF16),
        compiler_params=_params(1),
    )(x2, g, w_all)


def _outproj_kernel(x_ref, u_ref, v_ref, z_ref, vg_ref, ws_ref, bs_ref, yb_ref, yc_ref, yd_ref, w_ref, o_ref,
                    ya_ref):
    tm, d = x_ref.shape
    wgs = _mixer_a_weights(ws_ref)
    nblk = tm // GMLP_BLOCK
    ncol = d // nblk
    accs = []
    for c in range(nblk):
        cols = slice(c * ncol, (c + 1) * ncol)
        acc = x_ref[:, cols]
        for g, y_ref in ((1, yb_ref), (2, yc_ref), (3, yd_ref)):
            acc = acc + jnp.dot(y_ref[...], w_ref[g * GROUP_W:(g + 1) * GROUP_W, cols],
                                preferred_element_type=F32)
        accs.append(acc)
        _mixer_a_rows(u_ref, v_ref, z_ref, vg_ref, wgs, bs_ref, ya_ref, c)
    for c in range(nblk):
        cols = slice(c * ncol, (c + 1) * ncol)
        o_ref[:, cols] = accs[c] + jnp.dot(ya_ref[...], w_ref[0:GROUP_W, cols], preferred_element_type=F32)


def _outproj(x2, h, vg, ws, bs, ys, w, tm=512):
    n, d = x2.shape
    yspec = pl.BlockSpec((tm, GROUP_W), lambda i: (i, 0))
    hspec = lambda blk: pl.BlockSpec((tm, GROUP_W), lambda i, blk=blk: (i, blk))
    return pl.pallas_call(
        _outproj_kernel,
        grid=(n // tm,),
        in_specs=[pl.BlockSpec((tm, d), lambda i: (i, 0)),
                  hspec(BLK_A_U), hspec(BLK_A_V), hspec(BLK_A_Z),
                  pl.BlockSpec((1, GROUP_W), lambda i: (0, 0)),
                  pl.BlockSpec((A_GROUPS, GMLP_BLOCK, GMLP_BLOCK), lambda i: (0, 0, 0)),
                  pl.BlockSpec((A_GROUPS, GMLP_BLOCK, 1), lambda i: (0, 0, 0)),
                  yspec, yspec, yspec,
                  pl.BlockSpec((4 * GROUP_W, d), lambda i: (0, 0))],
        out_specs=pl.BlockSpec((tm, d), lambda i: (i, 0)),
        out_shape=jax.ShapeDtypeStruct((n, d), F32),
        scratch_shapes=[pltpu.VMEM((tm, GROUP_W), BF16)],
        compiler_params=_params(1),
    )(x2, h, h, h, vg, ws, bs, *ys, w)


def _mixer_a_weights(w_ref):
    i = lax.broadcasted_iota(jnp.int32, (GMLP_BLOCK, GMLP_BLOCK), 0)
    j = lax.broadcasted_iota(jnp.int32, (GMLP_BLOCK, GMLP_BLOCK), 1)
    keep = (j // CHUNK) <= (i // CHUNK)
    return [jnp.where(keep, w_ref[g], 0.0).astype(BF16) for g in range(A_GROUPS)]


def _mixer_a_rows(u_ref, v_ref, z_ref, vg_ref, wgs, b_ref, o_ref, blk):
    rows = slice(blk * GMLP_BLOCK, (blk + 1) * GMLP_BLOCK)
    u = _gelu(u_ref[rows, :].astype(F32))
    v = _gelu(v_ref[rows, :].astype(F32))
    ms = jnp.mean(v * v, axis=-1, keepdims=True)
    vb = (v * lax.rsqrt(ms + EPS) * vg_ref[...]).astype(BF16)
    gate = _silu(z_ref[rows, :].astype(F32))
    for g in range(A_GROUPS):
        cols = slice(g * LANES, (g + 1) * LANES)
        sg = jnp.dot(wgs[g], vb[:, cols], preferred_element_type=F32) + b_ref[g]
        o_ref[rows, cols] = (u[:, cols] * sg * gate[:, cols]).astype(BF16)


def _order_key(x):
    return jnp.where(x < 0, x ^ 0x7FFFFFFF, x)


def _b_select_kernel(iq_ref, iw_ref, ik_ref, o_ref, lhs_ref, sc_ref, *, seq, tq, topk):
    t_blk = pl.program_id(1)
    nkb = seq // 256
    nblk = ((t_blk + 1) * tq) // 256
    n_interp = 12
    n_unchecked = 12 + (3 * t_blk) // 2
    hrows = [slice(h * tq, (h + 1) * tq) for h in range(N_HEADS64)]

    lo_half = _lo_mask(tq)
    w_t = (iw_ref[...].astype(F32) * IDX_SCALE).T
    for h in range(N_HEADS64):
        iqt = iq_ref[:, (h // 2) * LANES:(h // 2 + 1) * LANES]
        sel = lo_half if h % 2 == 0 else jnp.logical_not(lo_half)
        lhs_ref[hrows[h], :] = jnp.where(sel, iqt, jnp.zeros_like(iqt))

    qpos = t_blk * tq + lax.broadcasted_iota(jnp.int32, (256, tq), 1)
    krow = lax.broadcasted_iota(jnp.int32, (256, tq), 0)

    def fold8(x):
        return jnp.sum(x.reshape(256 // 8, 8, tq), axis=0)

    def score_block(kb, carry, masked):
        amax, n_pos, n_nn = carry
        off = pl.multiple_of(kb * 256, 256)
        ikblk = ik_ref[pl.ds(off, 256), :]
        score = jnp.zeros((256, tq), F32)
        for h in range(N_HEADS64):
            w_h = w_t[IW_LANE + h:IW_LANE + h + 1, :]
            score = score + w_h * jnp.maximum(_dot_t(ikblk, lhs_ref[hrows[h], :]), 0.0)
        mag = jnp.abs(score)
        if masked:
            adm = ((kb * 256 + krow) // CHUNK) <= (qpos // CHUNK)
            score = jnp.where(adm, score, -jnp.inf)
            mag = jnp.where(adm, mag, 0.0)
        sc_ref[kb] = score
        return (jnp.maximum(amax, jnp.max(mag.reshape(256 // 8, 8, tq), axis=0)),
                n_pos + fold8(jnp.where(score > 0.0, 1.0, 0.0)), n_nn + fold8(jnp.where(score >= 0.0, 1.0, 0.0)))

    nfull = (t_blk * tq) // 256
    zeros8 = jnp.zeros((8, tq), F32)
    carry = lax.fori_loop(0, nfull, lambda kb, c: score_block(kb, c, False), (zeros8, zeros8, zeros8))
    carry = lax.fori_loop(nfull, nblk, lambda kb, c: score_block(kb, c, True), carry)
    amax = jnp.max(carry[0], axis=0, keepdims=True)
    f_pos = jnp.sum(carry[1], axis=0, keepdims=True)
    f_nn = jnp.sum(carry[2], axis=0, keepdims=True)

    def count(pred):
        def body(kb, acc):
            return acc + fold8(jnp.where(pred(sc_ref[kb], kb), 1.0, 0.0))
        return jnp.sum(lax.fori_loop(0, nblk, body, zeros8), axis=0, keepdims=True)

    kf = float(topk)
    qrow = t_blk * tq + lax.broadcasted_iota(jnp.int32, (1, tq), 1)
    n_adm = ((qrow // CHUNK + 1) * CHUNK).astype(F32)
    one = jnp.ones((1, tq), jnp.int32)
    pos = f_pos > kf
    neg = f_nn < kf
    lo0 = jnp.where(pos, one, _order_key(lax.bitcast_convert_type(-amax, jnp.int32)))
    hi0 = jnp.where(neg, one - 1, _order_key(lax.bitcast_convert_type(amax, jnp.int32)) + 1)
    w_lo0 = jnp.where(pos, f_pos, n_adm) - kf
    w_hi0 = kf - jnp.where(neg, f_nn, 0.0)
    all_sel = n_adm <= kf
    at_zero = jnp.logical_not(pos | neg)
    done0 = jnp.where(all_sel | at_zero | (hi0 == lo0 + 1), 1.0, 0.0)
    thr0 = jnp.where(all_sel, KEY_LOWEST, jnp.where(at_zero, jnp.where(f_pos == kf, one, one - 1), lo0))

    def as_score(key):
        return lax.bitcast_convert_type(_order_key(key), F32)

    def search_cond(st):
        return jnp.logical_and(st[0][0] < n_interp + 32, st[1] < 0.5)

    def search_step(st):
        it, lo, hi, w_lo, w_hi, side, done, thr = st
        lo_v = as_score(lo)
        hi_v = as_score(hi)
        c_v = lo_v + (hi_v - lo_v) * (w_lo / (w_lo + w_hi))
        c_interp = _order_key(lax.bitcast_convert_type(c_v, jnp.int32))
        c_mid = (lo >> 1) + (hi >> 1) + (lo & hi & 1)
        cand = jnp.where(it < n_interp, c_interp, c_mid)
        cand = jnp.minimum(jnp.maximum(cand, lo + 1), hi - 1)
        cand_v = as_score(cand)
        f = count(lambda s, kb: s >= cand_v)
        live = done < 0.5
        up = f > kf
        hit = f == kf
        new_lo = jnp.where(live & up, cand, lo)
        new_hi = jnp.where(live & jnp.logical_not(up), cand, hi)
        new_w_lo = jnp.where(up, f - kf, jnp.where(side < 0.0, 0.5 * w_lo, w_lo))
        new_w_hi = jnp.where(up, jnp.where(side > 0.0, 0.5 * w_hi, w_hi), kf - f)
        new_side = jnp.where(up, 1.0, -1.0)
        new_thr = jnp.where(live, jnp.where(hit, cand, new_lo), thr)
        new_done = jnp.where(live & (hit | (new_hi == new_lo + 1)), 1.0, done)
        return (it + 1, new_lo, new_hi, jnp.where(live, new_w_lo, w_lo),
                jnp.where(live, new_w_hi, w_hi), jnp.where(live, new_side, side), new_done, new_thr)

    def checked_step(st):
        new = search_step(st[0])
        return new, jnp.min(new[-2])

    state = (jnp.int32(0), lo0, hi0, w_lo0, w_hi0, jnp.zeros((1, tq), F32), done0, thr0)
    state = lax.fori_loop(0, n_unchecked, lambda i, st: search_step(st), state)
    thr = as_score(lax.while_loop(search_cond, checked_step, (state, jnp.min(state[-2])))[0][-1])

    any_excess = jnp.max(count(lambda s, kb: s >= thr)) > kf

    def store_mask(kb, keep_t):
        for g in range(tq // 256):
            o_ref[g, kb] = keep_t[:, g * 256:(g + 1) * 256].T

    def write_unused(kb, carry):
        store_mask(kb, jnp.zeros((256, tq), F32))
        return carry

    lax.fori_loop(nblk, nkb, write_unused, 0)

    @pl.when(jnp.logical_not(any_excess))
    def _():
        def write(kb, carry):
            store_mask(kb, jnp.where(sc_ref[kb] >= thr, 1.0, 0.0))
            return carry

        lax.fori_loop(0, nblk, write, 0)

    @pl.when(any_excess)
    def _():
        need = kf - count(lambda s, kb: s > thr)

        def idx_step(it, jmax):
            cand = jmax | lax.shift_left(jnp.int32(1), 10 - it)
            below = count(lambda s, kb: (s == thr) & ((kb * 256 + krow) < cand))
            return jnp.where(below < need, cand, jmax)

        jmax = lax.fori_loop(0, 11, idx_step, jnp.zeros((1, tq), jnp.int32))

        def write(kb, carry):
            s = sc_ref[kb]
            keep_tie = (s == thr) & ((kb * 256 + krow) <= jmax)
            store_mask(kb, jnp.where((s > thr) | keep_tie, 1.0, 0.0))
            return carry

        lax.fori_loop(0, nblk, write, 0)


def _b_select(h, bsz, seq, tq=512):
    tq = min(tq, seq)
    nt = seq // tq
    topk = min(TOPK_MAX, seq // 4)
    kern = functools.partial(_b_select_kernel, seq=seq, tq=tq, topk=topk)
    return pl.pallas_call(
        kern,
        grid=(bsz, nt),
        in_specs=[pl.BlockSpec((tq, GROUP_W), lambda b, t: (b * nt + t, BLK_B_IQ)),
                  pl.BlockSpec((tq, LANES), lambda b, t: (b * nt + t, UNIT_KR_IW)),
                  pl.BlockSpec((seq, LANES), lambda b, t: (b, UNIT_B_IK))],
        out_specs=pl.BlockSpec((tq // 256, seq // 256, 256, 256), lambda b, t: (b * nt + t, 0, 0, 0)),
        out_shape=jax.ShapeDtypeStruct((bsz * seq // 256, seq // 256, 256, 256), F32),
        scratch_shapes=[pltpu.VMEM((N_HEADS64 * tq, LANES), BF16),
                        pltpu.VMEM((seq // 256, 256, tq), F32)],
        compiler_params=_params(2),
    )(h, h, h)


def _b_attn_kernel(cfar_ref, q_ref, z_ref, k_ref, v_ref, msk_ref, qg_ref, kg_ref, base0_ref, base1_ref,
                   o_ref, kn_ref, v1_ref, bias_ref, qall_ref, s_ref, mp_ref, acc_ref, *, seq, tq):
    b = pl.program_id(0)
    t_blk = pl.program_id(1)
    hrows = [slice(h * tq, (h + 1) * tq) for h in range(N_HEADS64)]

    @pl.when((b == 0) & (t_blk == 0))
    def _():
        for h in range(N_HEADS64):
            bias_ref[0, hrows[h], :] = jnp.full((tq, 256), cfar_ref[h] * LOG2E, F32)
            bias_ref[1, hrows[h], :] = _toeplitz(base1_ref[h:h + 1, :], tq, 256) * LOG2E
            bias_ref[2, hrows[h], :] = _toeplitz(base0_ref[h:h + 1, :], tq, 256) * LOG2E

    @pl.when(t_blk == 0)
    def _():
        lo256 = _lo_mask(256)
        for r in range(seq // 256):
            rows = slice(r * 256, (r + 1) * 256)
            k = k_ref[rows, :].astype(F32)
            ms = jnp.mean(k * k, axis=-1, keepdims=True)
            kn_ref[rows, :] = (k * lax.rsqrt(ms + EPS) * kg_ref[...]).astype(BF16)
            v = v_ref[rows, :]
            v1_ref[rows, :] = jnp.where(lo256, v, jnp.ones_like(v))

    lo = _lo_mask(tq)
    qtiles = _rms_heads64(q_ref[...].astype(F32), qg_ref[...], 4)
    for h in range(N_HEADS64):
        sel = lo if h % 2 == 0 else jnp.logical_not(lo)
        qall_ref[hrows[h], :] = jnp.where(sel, qtiles[h // 2] * (0.125 * LOG2E), 0.0).astype(BF16)
    mp_ref[...] = jnp.full(mp_ref.shape, NEG, F32)
    acc_ref[...] = jnp.zeros(acc_ref.shape, F32)
    nblk = t_blk + 1

    def logits_blocks(kb0, n):
        for i in range(n):
            kb = kb0 + i
            off = pl.multiple_of(kb * 256, 256)
            kblk = kn_ref[pl.ds(off, 256), :]
            which = jnp.clip(kb - (t_blk - 2), 0, 2)
            keep = msk_ref[0, kb] > 0.5
            for h in range(N_HEADS64):
                s = jnp.where(keep, _dot_t(qall_ref[hrows[h], :], kblk) + bias_ref[which, hrows[h], :], NEG)
                s_ref[kb, hrows[h], :] = s
                mp_ref[hrows[h], :] = jnp.maximum(mp_ref[hrows[h], :], jnp.maximum(s[:, :LANES], s[:, LANES:]))

    def value_blocks(kb0, n):
        off = pl.multiple_of(kb0 * 256, 256)
        v1 = v1_ref[pl.ds(off, n * 256), :]
        for h in range(N_HEADS64):
            m = mp_ref[hrows[h], :]
            mm = jnp.concatenate([m, m], axis=1)
            p = [jnp.exp2(s_ref[kb0 + i, hrows[h], :] - mm).astype(BF16) for i in range(n)]
            p = p[0] if n == 1 else jnp.concatenate(p, axis=1)
            acc_ref[hrows[h], :] += jnp.dot(p, v1, preferred_element_type=F32)

    _for_blocks(logits_blocks, nblk)
    for h in range(N_HEADS64):
        m = jnp.max(mp_ref[hrows[h], :], axis=-1, keepdims=True)
        mp_ref[hrows[h], :] = jnp.broadcast_to(m, (tq, LANES))
    _for_blocks(value_blocks, nblk)

    gate = _silu(z_ref[...].astype(F32))
    for t in range(4):
        a_even = acc_ref[hrows[2 * t], :]
        a_odd = acc_ref[hrows[2 * t + 1], :]
        o_even = a_even / pltpu.roll(a_even, 64, 1)
        o_odd = pltpu.roll(a_odd, 64, 1) / a_odd
        cols = slice(t * LANES, (t + 1) * LANES)
        o_ref[:, cols] = (jnp.where(lo, o_even, o_odd) * gate[:, cols]).astype(BF16)


def _b_attn(h, mask, cfar, qg, kg, base0, base1, bsz, seq, tq=256):
    nt = seq // tq
    rows = N_HEADS64 * tq
    kern = functools.partial(_b_attn_kernel, seq=seq, tq=tq)
    full = lambda shape: pl.BlockSpec(shape, lambda b, t: (0,) * len(shape))
    return pl.pallas_call(
        kern,
        grid=(bsz, nt),
        in_specs=[pl.BlockSpec(memory_space=pltpu.SMEM),
                  pl.BlockSpec((tq, GROUP_W), lambda b, t: (b * nt + t, BLK_B_Q)),
                  pl.BlockSpec((tq, GROUP_W), lambda b, t: (b * nt + t, BLK_B_Z)),
                  pl.BlockSpec((seq, LANES), lambda b, t: (b, UNIT_B_K)),
                  pl.BlockSpec((seq, LANES), lambda b, t: (b, UNIT_B_V)),
                  pl.BlockSpec((1, seq // 256, tq, 256), lambda b, t: (b * nt + t, 0, 0, 0)),
                  full((1, GROUP_W)), full((1, LANES)), full((N_HEADS64, 512)), full((N_HEADS64, 512))],
        out_specs=pl.BlockSpec((tq, GROUP_W), lambda b, t: (b * nt + t, 0)),
        out_shape=jax.ShapeDtypeStruct((bsz * seq, GROUP_W), BF16),
        scratch_shapes=[pltpu.VMEM((seq, LANES), BF16), pltpu.VMEM((seq, LANES), BF16),
                        pltpu.VMEM((3, rows, 256), F32), pltpu.VMEM((rows, LANES), BF16),
                        pltpu.VMEM((seq // 256, rows, 256), F32),
                        pltpu.VMEM((rows, LANES), F32), pltpu.VMEM((rows, LANES), F32)],
        compiler_params=_params(2),
    )(cfar, h, h, h, h, mask, qg, kg, base0, base1)


def _rope(tile, cos, sin):
    return tile * cos + pltpu.roll(tile, 64, 1) * sin


def _c_prep_kernel(lat_ref, kr_ref, wq_ref, wkv_ref, qag_ref, kvag_ref, qg_ref, kg_ref, cos_ref, sin_ref,
                   qo_ref, ko_ref, vo_ref):
    cq = lat_ref[:, :Q_LORA].astype(F32)
    ms = jnp.mean(cq * cq, axis=-1, keepdims=True)
    cqn = (cq * lax.rsqrt(ms + EPS) * qag_ref[...]).astype(BF16)
    qpre = jnp.dot(cqn, wq_ref[...], preferred_element_type=F32)
    ckv = lat_ref[:, Q_LORA:].astype(F32)
    ms = jnp.mean(ckv * ckv, axis=-1, keepdims=True)
    ckvn = (ckv * lax.rsqrt(ms + EPS) * kvag_ref[...]).astype(BF16)
    kvpre = jnp.dot(ckvn, wkv_ref[...], preferred_element_type=F32)
    lane = lax.broadcasted_iota(jnp.int32, kr_ref.shape, 1)
    kr = jnp.where((lane % 64) < 32, kr_ref[...].astype(F32), 0.0)
    kr_ss = jnp.sum(kr * kr, axis=-1, keepdims=True)
    cos = cos_ref[...]
    sin = sin_ref[...]
    qg = qg_ref[...]
    kg = kg_ref[...]
    kr_rot = _rope(kr * kg[:, LANES:], cos, sin)
    for h in range(C_HEADS):
        qh = qpre[:, h * 256:(h + 1) * 256]
        r = lax.rsqrt(jnp.sum(qh * qh, axis=-1, keepdims=True) * (1.0 / C_QK) + EPS)
        qn = qh * r * qg
        qo_ref[:, h * 256:h * 256 + LANES] = qn[:, :LANES].astype(BF16)
        qo_ref[:, h * 256 + LANES:(h + 1) * 256] = _rope(qn[:, LANES:], cos, sin).astype(BF16)
        kn = kvpre[:, h * LANES:(h + 1) * LANES]
        r = lax.rsqrt((jnp.sum(kn * kn, axis=-1, keepdims=True) + kr_ss) * (1.0 / C_QK) + EPS)
        ko_ref[:, h * 256:h * 256 + LANES] = (kn * r * kg[:, :LANES]).astype(BF16)
        ko_ref[:, h * 256 + LANES:(h + 1) * 256] = (kr_rot * r).astype(BF16)
    vo_ref[...] = kvpre[:, C_HEADS * LANES:].astype(BF16)


def _c_prep(h, wq, wkv, qag, kvag, qg, kg, cos, sin, seq, tm=512):
    n = h.shape[0]
    ns = seq // tm
    full = lambda shape: pl.BlockSpec(shape, lambda i: (0,) * len(shape))
    return pl.pallas_call(
        _c_prep_kernel,
        grid=(n // tm,),
        in_specs=[pl.BlockSpec((tm, GROUP_W), lambda i: (i, BLK_C_QKV)),
                  pl.BlockSpec((tm, LANES), lambda i: (i, UNIT_KR_IW)),
                  full((Q_LORA, 4 * 256)), full((KV_LORA, 8 * LANES)),
                  full((1, Q_LORA)), full((1, LANES)), full((1, 256)), full((1, 256)),
                  pl.BlockSpec((tm, LANES), lambda i: (i % ns, 0)),
                  pl.BlockSpec((tm, LANES), lambda i: (i % ns, 0))],
        out_specs=[pl.BlockSpec((tm, 4 * 256), lambda i: (i, 0)),
                   pl.BlockSpec((tm, 4 * 256), lambda i: (i, 0)),
                   pl.BlockSpec((tm, GROUP_W), lambda i: (i, 0))],
        out_shape=[jax.ShapeDtypeStruct((n, 4 * 256), BF16), jax.ShapeDtypeStruct((n, 4 * 256), BF16),
                   jax.ShapeDtypeStruct((n, GROUP_W), BF16)],
        compiler_params=_params(1),
    )(h, h, wq, wkv, qag, kvag, qg, kg, cos, sin)


def _c_attn_kernel(q_ref, k_ref, v_ref, z_ref, o_ref, s_ref, mp_ref, lp_ref, acc_ref, *, tq):
    qt = pl.program_id(1)
    scale = C_QK ** -0.5 * LOG2E
    hrows = [slice(h * tq, (h + 1) * tq) for h in range(C_HEADS)]
    qchunk = (qt * tq + lax.broadcasted_iota(jnp.int32, (tq, 256), 0)) // CHUNK
    kcol = lax.broadcasted_iota(jnp.int32, (tq, 256), 1)
    mp_ref[...] = jnp.full(mp_ref.shape, NEG, F32)
    lp_ref[...] = jnp.zeros(lp_ref.shape, F32)
    acc_ref[...] = jnp.zeros(acc_ref.shape, F32)
    nfull = (qt * tq) // 256
    nblk = ((qt + 1) * tq) // 256

    def logits_blocks(kb0, n, masked):
        for i in range(n):
            kb = kb0 + i
            off = pl.multiple_of(kb * 256, 256)
            for h in range(C_HEADS):
                cols = slice(h * 256, (h + 1) * 256)
                s = _dot_t(q_ref[:, cols], k_ref[pl.ds(off, 256), cols]) * scale
                if masked:
                    s = jnp.where(((kb * 256 + kcol) // CHUNK) <= qchunk, s, NEG)
                s_ref[kb, hrows[h], :] = s
                mp_ref[hrows[h], :] = jnp.maximum(mp_ref[hrows[h], :], jnp.maximum(s[:, :LANES], s[:, LANES:]))

    def value_blocks(kb0, n):
        off = pl.multiple_of(kb0 * 256, 256)
        for h in range(C_HEADS):
            m = mp_ref[hrows[h], :]
            mm = jnp.concatenate([m, m], axis=1)
            p = [jnp.exp2(s_ref[kb0 + i, hrows[h], :] - mm) for i in range(n)]
            lsum = p[0][:, :LANES] + p[0][:, LANES:]
            for pi in p[1:]:
                lsum = lsum + pi[:, :LANES] + pi[:, LANES:]
            lp_ref[hrows[h], :] += lsum
            pb = p[0].astype(BF16) if n == 1 else jnp.concatenate([pi.astype(BF16) for pi in p], axis=1)
            acc_ref[hrows[h], :] += jnp.dot(pb, v_ref[pl.ds(off, n * 256), h * LANES:(h + 1) * LANES],
                                            preferred_element_type=F32)

    _for_blocks(lambda kb0, n: logits_blocks(kb0, n, False), nfull)
    lax.fori_loop(nfull, nblk, lambda kb, c: (logits_blocks(kb, 1, True), c)[1], 0)
    for h in range(C_HEADS):
        m = jnp.max(mp_ref[hrows[h], :], axis=-1, keepdims=True)
        mp_ref[hrows[h], :] = jnp.broadcast_to(m, (tq, LANES))
    _for_blocks(value_blocks, nblk)
    gate = _silu(z_ref[...].astype(F32))
    for h in range(C_HEADS):
        cols = slice(h * LANES, (h + 1) * LANES)
        l = jnp.sum(lp_ref[hrows[h], :], axis=-1, keepdims=True)
        o_ref[:, cols] = (acc_ref[hrows[h], :] / l * gate[:, cols]).astype(BF16)


def _c_attn(qc, kc, vc, h, bsz, seq, tq=256):
    nt = seq // tq
    rows = C_HEADS * tq
    kern = functools.partial(_c_attn_kernel, tq=tq)
    return pl.pallas_call(
        kern,
        grid=(bsz, nt),
        in_specs=[pl.BlockSpec((tq, C_HEADS * 256), lambda b, t: (b * nt + t, 0)),
                  pl.BlockSpec((seq, C_HEADS * 256), lambda b, t: (b, 0)),
                  pl.BlockSpec((seq, GROUP_W), lambda b, t: (b, 0)),
                  pl.BlockSpec((tq, GROUP_W), lambda b, t: (b * nt + t, BLK_C_Z))],
        out_specs=pl.BlockSpec((tq, GROUP_W), lambda b, t: (b * nt + t, 0)),
        out_shape=jax.ShapeDtypeStruct((bsz * seq, GROUP_W), BF16),
        scratch_shapes=[pltpu.VMEM((seq // 256, rows, 256), F32), pltpu.VMEM((rows, LANES), F32),
                        pltpu.VMEM((rows, LANES), F32), pltpu.VMEM((rows, LANES), F32)],
        compiler_params=_params(2),
    )(qc, kc, vc, h)


def _mixer_d_kernel(q_ref, k_ref, v_ref, z_ref, qg_ref, kg_ref, base_ref, o_ref,
                    kpad_ref, vpad_ref, bias_ref, s_ref, mp_ref, *, seq, tq):
    b = pl.program_id(0)
    qt = pl.program_id(1)
    win = tq + D_LEFT

    @pl.when((b == 0) & (qt == 0))
    def _():
        qc = lax.broadcasted_iota(jnp.int32, (tq, win), 0) // CHUNK
        kc = lax.broadcasted_iota(jnp.int32, (tq, win), 1) // CHUNK
        band = (kc >= qc) & (kc <= qc + D_LEFT // CHUNK)
        for h in range(N_HEADS64):
            bias_ref[h] = jnp.where(band, _toeplitz(base_ref[h:h + 1, :], tq, win) * LOG2E, NEG)

    @pl.when(qt == 0)
    def _():
        kpad_ref[0:D_LEFT, :] = jnp.zeros((D_LEFT, GROUP_W), BF16)
        vpad_ref[0:D_LEFT, :] = jnp.zeros((D_LEFT, GROUP_W), BF16)
        for r in range(seq // 256):
            rows = slice(r * 256, (r + 1) * 256)
            dst = slice(D_LEFT + r * 256, D_LEFT + (r + 1) * 256)
            tiles = _rms_heads64(k_ref[rows, :].astype(F32), kg_ref[...], 4)
            for t in range(4):
                kpad_ref[dst, t * LANES:(t + 1) * LANES] = tiles[t].astype(BF16)
            vpad_ref[dst, :] = v_ref[rows, :]

    lo = _lo_mask(tq)
    qtiles = _rms_heads64(q_ref[...].astype(F32), qg_ref[...], 4)
    start = pl.multiple_of(qt * tq, tq)
    gate = _silu(z_ref[...].astype(F32))
    ntile = win // LANES

    def lane_tiles(x):
        return [x[:, i * LANES:(i + 1) * LANES] for i in range(ntile)]

    in_seq = lax.broadcasted_iota(jnp.int32, (tq, win), 1) + start >= D_LEFT

    def logits(h):
        t, half = divmod(h, 2)
        kwin = kpad_ref[pl.ds(start, win), t * LANES:(t + 1) * LANES]
        sel = lo if half == 0 else jnp.logical_not(lo)
        qh = jnp.where(sel, qtiles[t] * (0.125 * LOG2E), 0.0).astype(BF16)
        s = jnp.where(in_seq, _dot_t(qh, kwin) + bias_ref[h], NEG)
        s_ref[h] = s
        m = jnp.max(functools.reduce(jnp.maximum, lane_tiles(s)), axis=-1, keepdims=True)
        mp_ref[h] = jnp.broadcast_to(m, (tq, LANES))

    def values(h):
        vwin = vpad_ref[pl.ds(start, win), (h // 2) * LANES:(h // 2 + 1) * LANES]
        m = mp_ref[h]
        p = jnp.exp2(s_ref[h] - jnp.concatenate([m] * ntile, axis=1))
        l = jnp.sum(functools.reduce(jnp.add, lane_tiles(p)), axis=-1, keepdims=True)
        return jnp.dot(p.astype(BF16), vwin, preferred_element_type=F32) / l

    outs = []
    logits(0)
    for h in range(1, N_HEADS64 + 1):
        if h < N_HEADS64:
            logits(h)
        outs.append(values(h - 1))
        if h % 2 == 0:
            cols = slice((h // 2 - 1) * LANES, (h // 2) * LANES)
            o_ref[:, cols] = (jnp.where(lo, outs[h - 2], outs[h - 1]) * gate[:, cols]).astype(BF16)


def _mixer_d(h, qg, kg, base, bsz, seq, tq=256):
    nt = seq // tq
    kern = functools.partial(_mixer_d_kernel, seq=seq, tq=tq)
    full = lambda shape: pl.BlockSpec(shape, lambda b, t: (0,) * len(shape))
    return pl.pallas_call(
        kern,
        grid=(bsz, nt),
        in_specs=[pl.BlockSpec((tq, GROUP_W), lambda b, t: (b * nt + t, BLK_D_Q)),
                  pl.BlockSpec((seq, GROUP_W), lambda b, t: (b, BLK_D_K)),
                  pl.BlockSpec((seq, GROUP_W), lambda b, t: (b, BLK_D_V)),
                  pl.BlockSpec((tq, GROUP_W), lambda b, t: (b * nt + t, BLK_D_Z)),
                  full((1, GROUP_W)), full((1, GROUP_W)), full((N_HEADS64, 2 * tq + D_LEFT))],
        out_specs=pl.BlockSpec((tq, GROUP_W), lambda b, t: (b * nt + t, 0)),
        out_shape=jax.ShapeDtypeStruct((bsz * seq, GROUP_W), BF16),
        scratch_shapes=[pltpu.VMEM((seq + D_LEFT, GROUP_W), BF16), pltpu.VMEM((seq + D_LEFT, GROUP_W), BF16),
                        pltpu.VMEM((N_HEADS64, tq, tq + D_LEFT), F32),
                        pltpu.VMEM((N_HEADS64, tq, tq + D_LEFT), F32), pltpu.VMEM((N_HEADS64, tq, LANES), F32)],
        compiler_params=_params(2),
    )(h, h, h, h, qg, kg, base)


def _w_in_pieces(take, zeros):
    c = lambda name, size, off=0: take(_SRC[name] + off, size)
    return [c("a_u", 512), c("a_v", 512), c("a_z", 512),
            c("b_q", 512), c("b_iq", 512), c("b_z", 512),
            c("b_k", 64), c("b_k", 64), c("b_v", 64), c("b_v", 64), c("b_ik", 64), c("b_ik", 64),
            c("c_kr", 32), c("b_iw", 8), zeros(24), c("c_kr", 32, 32), zeros(32),
            c("c_q", 384), c("c_kv", 128), c("c_z", 512),
            c("d_q", 512), c("d_k", 512), c("d_v", 512), c("d_z", 512)]


def _layout_w_in_kernel(w_ref, o_ref):
    tk = w_ref.shape[2]
    pieces = _w_in_pieces(lambda s, n: w_ref[0, s:s + n, :], lambda n: jnp.zeros((n, tk), F32))
    ends = np.cumsum([0] + [p.shape[0] for p in pieces])
    start = 0
    for i in range(1, len(pieces) + 1):
        if ends[i] % GROUP_W == 0:
            group = pieces[start:i]
            blk = group[0] if len(group) == 1 else jnp.concatenate(group, axis=0)
            o_ref[0, ends[start]:ends[i], :] = blk.astype(BF16)
            start = i


def _layout_w_in(w_in, tk=256):
    w_t = jnp.swapaxes(w_in, 1, 2)
    depth, cols, d = w_t.shape
    return pl.pallas_call(
        _layout_w_in_kernel,
        grid=(depth, d // tk),
        in_specs=[pl.BlockSpec((1, cols, tk), lambda l, i: (l, 0, i))],
        out_specs=pl.BlockSpec((1, H_COLS, tk), lambda l, i: (l, 0, i)),
        out_shape=jax.ShapeDtypeStruct((depth, H_COLS, d), BF16),
        compiler_params=_params(2),
    )(w_t)


def _rope_layout(v):
    z = jnp.zeros(v.shape[:-1] + (32,), v.dtype)
    return jnp.concatenate([v[..., :32], z, v[..., 32:], z], axis=-1)


def _layout_c(w_qb, w_kvb, q_gain, k_gain, qa_gain):
    wq = w_qb.reshape(Q_LORA, C_HEADS, C_QK)
    wq = jnp.concatenate([wq[..., :C_NOPE], _rope_layout(wq[..., C_NOPE:])], axis=-1)
    wq = wq.reshape(Q_LORA, C_HEADS * 256).astype(BF16)
    wkv = w_kvb.reshape(KV_LORA, C_HEADS, 2 * LANES)
    wkv = jnp.concatenate([wkv[..., :C_NOPE].reshape(KV_LORA, -1), wkv[..., C_NOPE:].reshape(KV_LORA, -1)],
                          axis=1).astype(BF16)
    lay = lambda g: jnp.concatenate([g[:C_NOPE], _rope_layout(g[C_NOPE:])])[None, :]
    return wq, wkv, lay(q_gain), lay(k_gain), qa_gain[None, :]


def _t5_bucket_static(rel):
    half = T5_BUCKETS // 2
    exact = half // 2
    n = abs(rel)
    if n < exact:
        val = n
    else:
        val = min(exact + (n * n // (exact * exact)).bit_length() - 1, half - 1)
    return (half if rel > 0 else 0) + val


def _t5_tables(t5_bias):
    m = np.arange(512)
    d0 = np.where(m < 256, m, m - 512)
    d1 = np.where(m <= 256, m - 256, m - 768)
    idx0 = np.array([_t5_bucket_static(int(d)) for d in d0], np.int32)
    idx1 = np.array([_t5_bucket_static(int(d)) for d in d1], np.int32)
    far = _t5_bucket_static(-512)
    return t5_bias[idx0].T, t5_bias[idx1].T, t5_bias[far]


def _band_table(rel_bias, tq):
    width = 2 * tq + D_LEFT
    m = np.arange(width)
    dist = np.where(m <= tq + D_LEFT, D_LEFT - m, D_LEFT + width - m)
    idx = np.clip(dist, -REL_CLIP, REL_CLIP) + REL_CLIP
    return rel_bias[idx.astype(np.int32)].T


def _rope_tables(seq):
    inv = ROPE_BASE ** (-jnp.arange(0, C_ROPE, 2, dtype=F32) / C_ROPE)
    ang = jnp.arange(seq, dtype=F32)[:, None] * inv[None, :]
    c, s = jnp.cos(ang), jnp.sin(ang)
    z = jnp.zeros_like(c)
    return jnp.concatenate([c, z, c, z], axis=1), jnp.concatenate([-s, z, s, z], axis=1)


def kernel(x, t5_bias, norm_g, w_in, a_v_gain, a_ws, a_bs, b_q_gain, b_k_gain, c_qa_gain, c_kva_gain,
           c_w_qb, c_w_kvb, c_q_gain, c_k_gain, d_q_gain, d_k_gain, d_rel_bias, w_out):
    bsz, seq, d_model = x.shape
    depth = w_in.shape[0]
    tq = 256
    assert seq % 512 == 0 and seq <= 2048 and d_model % 512 == 0, (seq, d_model)
    assert w_in.shape[2] == _SRC["d_z"] + GROUP_W and w_out.shape[1] == 4 * GROUP_W, (w_in.shape, w_out.shape)
    x2 = x.reshape(bsz * seq, d_model)
    cos, sin = _rope_tables(seq)
    base0, base1, cfar = _t5_tables(t5_bias)
    w_in_blocks = _layout_w_in(w_in)
    for l in range(depth):
        h = _inproj(x2, norm_g[l][None, :], w_in_blocks, l)
        mask = _b_select(h, bsz, seq)
        y_b = _b_attn(h, mask, cfar, jnp.tile(b_q_gain[l], N_HEADS64)[None, :],
                      jnp.tile(b_k_gain[l], 2)[None, :], base0, base1, bsz, seq, tq)
        wq, wkv, qg, kg, qag = _layout_c(c_w_qb[l], c_w_kvb[l], c_q_gain[l], c_k_gain[l], c_qa_gain[l])
        qc, kc, vc = _c_prep(h, wq, wkv, qag, c_kva_gain[l][None, :], qg, kg, cos, sin, seq)
        y_c = _c_attn(qc, kc, vc, h, bsz, seq, tq)
        y_d = _mixer_d(h, jnp.tile(d_q_gain[l], N_HEADS64)[None, :], jnp.tile(d_k_gain[l], N_HEADS64)[None, :],
                       _band_table(d_rel_bias[l], tq), bsz, seq, tq)
        x2 = _outproj(x2, h, a_v_gain[l][None, :], a_ws[l], a_bs[l][:, :, None], (y_b, y_c, y_d),
                      w_out[l].astype(BF16))
    return x2.reshape(bsz, seq, d_model)
```

```python
import functools
import math

import numpy as np
import jax
import jax.numpy as jnp
from jax import lax
from jax.experimental import pallas as pl
from jax.experimental.pallas import tpu as pltpu

F32 = jnp.float32
BF16 = jnp.bfloat16

EPS = 1e-6
NEG = -1e30
LOG2E = math.log2(math.e)
KEY_LOWEST = int(np.float32(-np.finfo(np.float32).max).view(np.int32)) ^ 0x7FFFFFFF
CHUNK = 64
LANES = 128
GROUP_W = 512
A_GROUPS = 4
GMLP_BLOCK = 128
N_HEADS64 = 8
IDX_SCALE = (8 ** -0.5) * 0.125
TOPK_MAX = 256
T5_BUCKETS = 32
C_HEADS = 4
C_NOPE = 128
C_ROPE = 64
C_QK = 192
Q_LORA = 384
KV_LORA = 128
ROPE_BASE = 10000.0
D_LEFT = 8 * CHUNK
REL_CLIP = 128
VMEM_LIMIT = 56 * 1024 * 1024

BLK_A_U, BLK_A_V, BLK_A_Z = 0, 1, 2
BLK_B_Q, BLK_B_IQ, BLK_B_Z, BLK_SMALL = 3, 4, 5, 6
BLK_C_QKV, BLK_C_Z = 7, 8
BLK_D_Q, BLK_D_K, BLK_D_V, BLK_D_Z = 9, 10, 11, 12
H_COLS = 13 * GROUP_W
UNIT_B_K, UNIT_B_V, UNIT_B_IK, UNIT_KR_IW = (BLK_SMALL * 4 + i for i in range(4))
IW_LANE = 32

_SRC = dict(a_u=0, a_v=512, a_z=1024, b_q=1536, b_k=2048, b_v=2112, b_iq=2176, b_ik=2688,
            b_iw=2752, b_z=2760, c_q=3272, c_kv=3656, c_kr=3784, c_z=3848,
            d_q=4360, d_k=4872, d_v=5384, d_z=5896)


def _params(n_axes):
    return pltpu.CompilerParams(dimension_semantics=("arbitrary",) * n_axes,
                                vmem_limit_bytes=VMEM_LIMIT)


def _gelu(x):
    c = math.sqrt(2.0 / math.pi)
    return x * (0.5 * (1.0 + jnp.tanh(c * (x + 0.044715 * (x * x * x)))))


def _silu(x):
    return x * (1.0 / (1.0 + jnp.exp(-x)))


def _dot_t(a, b):
    return lax.dot_general(a, b, (((1,), (1,)), ((), ())), preferred_element_type=F32)


def _lo_mask(rows):
    return lax.broadcasted_iota(jnp.int32, (rows, LANES), 1) < 64


def _rms_heads64(x, gain, ntiles):
    lo = _lo_mask(x.shape[0])
    tiles = []
    for t in range(ntiles):
        xt = x[:, t * LANES:(t + 1) * LANES]
        sq = xt * xt
        s_lo = jnp.sum(jnp.where(lo, sq, 0.0), axis=-1, keepdims=True)
        s_hi = jnp.sum(jnp.where(lo, 0.0, sq), axis=-1, keepdims=True)
        r = jnp.where(lo, lax.rsqrt(s_lo * (1.0 / 64) + EPS), lax.rsqrt(s_hi * (1.0 / 64) + EPS))
        tiles.append(xt * r * gain[:, t * LANES:(t + 1) * LANES])
    return tiles


def _for_blocks(blocks, count):
    def quad(i, carry):
        blocks(4 * i, 4)
        return carry

    lax.fori_loop(0, count // 4, quad, 0)
    first = (count // 4) * 4

    @pl.when((count & 2) != 0)
    def _():
        blocks(first, 2)

    @pl.when((count & 1) != 0)
    def _():
        blocks(first + (count & 2), 1)


def _toeplitz(base_row, rows, width):
    t = jnp.broadcast_to(base_row, (rows, base_row.shape[1]))
    t = pltpu.roll(t, 0, 1, stride=1, stride_axis=0)
    return t[:, :width]


def _inproj_kernel(x_ref, g_ref, w_ref, o_ref):
    x = x_ref[...]
    ms = jnp.mean(x * x, axis=-1, keepdims=True)
    xn = (x * lax.rsqrt(ms + EPS) * g_ref[...]).astype(BF16)
    for c in range(H_COLS // GROUP_W):
        cols = slice(c * GROUP_W, (c + 1) * GROUP_W)
        o_ref[:, cols] = _dot_t(xn, w_ref[cols, :]).astype(BF16)


def _inproj(x2, g, w_all, layer, tm=512):
    n, d = x2.shape
    return pl.pallas_call(
        _inproj_kernel,
        grid=(n // tm,),
        in_specs=[pl.BlockSpec((tm, d), lambda i: (i, 0)),
                  pl.BlockSpec((1, d), lambda i: (0, 0)),
                  pl.BlockSpec((None, H_COLS, d), lambda i: (layer, 0, 0), pipeline_mode=pl.Buffered(1))],
        out_specs=pl.BlockSpec((tm, H_COLS), lambda i: (i, 0)),
        out_shape=jax.ShapeDtypeStruct((n, H_COLS), BF16),
        compiler_params=_params(1),
    )(x2, g, w_all)


def _outproj_kernel(x_ref, u_ref, v_ref, z_ref, vg_ref, ws_ref, bs_ref, yb_ref, yc_ref, yd_ref, w_ref, o_ref,
                    ya_ref):
    tm, d = x_ref.shape
    wgs = _mixer_a_weights(ws_ref)
    nblk = tm // GMLP_BLOCK
    ncol = d // nblk
    accs = []
    for c in range(nblk):
        cols = slice(c * ncol, (c + 1) * ncol)
        acc = x_ref[:, cols]
        for g, y_ref in ((1, yb_ref), (2, yc_ref), (3, yd_ref)):
            acc = acc + jnp.dot(y_ref[...], w_ref[g * GROUP_W:(g + 1) * GROUP_W, cols],
                                preferred_element_type=F32)
        accs.append(acc)
        _mixer_a_rows(u_ref, v_ref, z_ref, vg_ref, wgs, bs_ref, ya_ref, c)
    for c in range(nblk):
        cols = slice(c * ncol, (c + 1) * ncol)
        o_ref[:, cols] = accs[c] + jnp.dot(ya_ref[...], w_ref[0:GROUP_W, cols], preferred_element_type=F32)


def _outproj(x2, h, vg, ws, bs, ys, w, tm=512):
    n, d = x2.shape
    yspec = pl.BlockSpec((tm, GROUP_W), lambda i: (i, 0))
    hspec = lambda blk: pl.BlockSpec((tm, GROUP_W), lambda i, blk=blk: (i, blk))
    return pl.pallas_call(
        _outproj_kernel,
        grid=(n // tm,),
        in_specs=[pl.BlockSpec((tm, d), lambda i: (i, 0)),
                  hspec(BLK_A_U), hspec(BLK_A_V), hspec(BLK_A_Z),
                  pl.BlockSpec((1, GROUP_W), lambda i: (0, 0)),
                  pl.BlockSpec((A_GROUPS, GMLP_BLOCK, GMLP_BLOCK), lambda i: (0, 0, 0)),
                  pl.BlockSpec((A_GROUPS, GMLP_BLOCK, 1), lambda i: (0, 0, 0)),
                  yspec, yspec, yspec,
                  pl.BlockSpec((4 * GROUP_W, d), lambda i: (0, 0))],
        out_specs=pl.BlockSpec((tm, d), lambda i: (i, 0)),
        out_shape=jax.ShapeDtypeStruct((n, d), F32),
        scratch_shapes=[pltpu.VMEM((tm, GROUP_W), BF16)],
        compiler_params=_params(1),
    )(x2, h, h, h, vg, ws, bs, *ys, w)


def _mixer_a_weights(w_ref):
    i = lax.broadcasted_iota(jnp.int32, (GMLP_BLOCK, GMLP_BLOCK), 0)
    j = lax.broadcasted_iota(jnp.int32, (GMLP_BLOCK, GMLP_BLOCK), 1)
    keep = (j // CHUNK) <= (i // CHUNK)
    return [jnp.where(keep, w_ref[g], 0.0).astype(BF16) for g in range(A_GROUPS)]


def _mixer_a_rows(u_ref, v_ref, z_ref, vg_ref, wgs, b_ref, o_ref, blk):
    rows = slice(blk * GMLP_BLOCK, (blk + 1) * GMLP_BLOCK)
    u = _gelu(u_ref[rows, :].astype(F32))
    v = _gelu(v_ref[rows, :].astype(F32))
    ms = jnp.mean(v * v, axis=-1, keepdims=True)
    vb = (v * lax.rsqrt(ms + EPS) * vg_ref[...]).astype(BF16)
    gate = _silu(z_ref[rows, :].astype(F32))
    for g in range(A_GROUPS):
        cols = slice(g * LANES, (g + 1) * LANES)
        sg = jnp.dot(wgs[g], vb[:, cols], preferred_element_type=F32) + b_ref[g]
        o_ref[rows, cols] = (u[:, cols] * sg * gate[:, cols]).astype(BF16)


def _order_key(x):
    return jnp.where(x < 0, x ^ 0x7FFFFFFF, x)


def _b_select_kernel(iq_ref, iw_ref, ik_ref, o_ref, lhs_ref, sc_ref, *, seq, tq, topk):
    t_blk = pl.program_id(1)
    nkb = seq // 256
    nblk = ((t_blk + 1) * tq) // 256
    n_interp = 12
    n_unchecked = 12 + (3 * t_blk) // 2
    hrows = [slice(h * tq, (h + 1) * tq) for h in range(N_HEADS64)]

    lo_half = _lo_mask(tq)
    w_t = (iw_ref[...].astype(F32) * IDX_SCALE).T
    for h in range(N_HEADS64):
        iqt = iq_ref[:, (h // 2) * LANES:(h // 2 + 1) * LANES]
        sel = lo_half if h % 2 == 0 else jnp.logical_not(lo_half)
        lhs_ref[hrows[h], :] = jnp.where(sel, iqt, jnp.zeros_like(iqt))

    qpos = t_blk * tq + lax.broadcasted_iota(jnp.int32, (256, tq), 1)
    krow = lax.broadcasted_iota(jnp.int32, (256, tq), 0)

    def fold8(x):
        return jnp.sum(x.reshape(256 // 8, 8, tq), axis=0)

    def score_block(kb, carry, masked):
        amax, n_pos, n_nn = carry
        off = pl.multiple_of(kb * 256, 256)
        ikblk = ik_ref[pl.ds(off, 256), :]
        score = jnp.zeros((256, tq), F32)
        for h in range(N_HEADS64):
            w_h = w_t[IW_LANE + h:IW_LANE + h + 1, :]
            score = score + w_h * jnp.maximum(_dot_t(ikblk, lhs_ref[hrows[h], :]), 0.0)
        mag = jnp.abs(score)
        if masked:
            adm = ((kb * 256 + krow) // CHUNK) <= (qpos // CHUNK)
            score = jnp.where(adm, score, -jnp.inf)
            mag = jnp.where(adm, mag, 0.0)
        sc_ref[kb] = score
        return (jnp.maximum(amax, jnp.max(mag.reshape(256 // 8, 8, tq), axis=0)),
                n_pos + fold8(jnp.where(score > 0.0, 1.0, 0.0)), n_nn + fold8(jnp.where(score >= 0.0, 1.0, 0.0)))

    nfull = (t_blk * tq) // 256
    zeros8 = jnp.zeros((8, tq), F32)
    carry = lax.fori_loop(0, nfull, lambda kb, c: score_block(kb, c, False), (zeros8, zeros8, zeros8))
    carry = lax.fori_loop(nfull, nblk, lambda kb, c: score_block(kb, c, True), carry)
    amax = jnp.max(carry[0], axis=0, keepdims=True)
    f_pos = jnp.sum(carry[1], axis=0, keepdims=True)
    f_nn = jnp.sum(carry[2], axis=0, keepdims=True)

    def count(pred):
        def body(kb, acc):
            return acc + fold8(jnp.where(pred(sc_ref[kb], kb), 1.0, 0.0))
        return jnp.sum(lax.fori_loop(0, nblk, body, zeros8), axis=0, keepdims=True)

    kf = float(topk)
    qrow = t_blk * tq + lax.broadcasted_iota(jnp.int32, (1, tq), 1)
    n_adm = ((qrow // CHUNK + 1) * CHUNK).astype(F32)
    one = jnp.ones((1, tq), jnp.int32)
    pos = f_pos > kf
    neg = f_nn < kf
    lo0 = jnp.where(pos, one, _order_key(lax.bitcast_convert_type(-amax, jnp.int32)))
    hi0 = jnp.where(neg, one - 1, _order_key(lax.bitcast_convert_type(amax, jnp.int32)) + 1)
    w_lo0 = jnp.where(pos, f_pos, n_adm) - kf
    w_hi0 = kf - jnp.where(neg, f_nn, 0.0)
    all_sel = n_adm <= kf
    at_zero = jnp.logical_not(pos | neg)
    done0 = jnp.where(all_sel | at_zero | (hi0 == lo0 + 1), 1.0, 0.0)
    thr0 = jnp.where(all_sel, KEY_LOWEST, jnp.where(at_zero, jnp.where(f_pos == kf, one, one - 1), lo0))

    def as_score(key):
        return lax.bitcast_convert_type(_order_key(key), F32)

    def search_cond(st):
        return jnp.logical_and(st[0][0] < n_interp + 32, st[1] < 0.5)

    def search_step(st):
        it, lo, hi, w_lo, w_hi, side, done, thr = st
        lo_v = as_score(lo)
        hi_v = as_score(hi)
        c_v = lo_v + (hi_v - lo_v) * (w_lo / (w_lo + w_hi))
        c_interp = _order_key(lax.bitcast_convert_type(c_v, jnp.int32))
        c_mid = (lo >> 1) + (hi >> 1) + (lo & hi & 1)
        cand = jnp.where(it < n_interp, c_interp, c_mid)
        cand = jnp.minimum(jnp.maximum(cand, lo + 1), hi - 1)
        cand_v = as_score(cand)
        f = count(lambda s, kb: s >= cand_v)
        live = done < 0.5
        up = f > kf
        hit = f == kf
        new_lo = jnp.where(live & up, cand, lo)
        new_hi = jnp.where(live & jnp.logical_not(up), cand, hi)
        new_w_lo = jnp.where(up, f - kf, jnp.where(side < 0.0, 0.5 * w_lo, w_lo))
        new_w_hi = jnp.where(up, jnp.where(side > 0.0, 0.5 * w_hi, w_hi), kf - f)
        new_side = jnp.where(up, 1.0, -1.0)
        new_thr = jnp.where(live, jnp.where(hit, cand, new_lo), thr)
        new_done = jnp.where(live & (hit | (new_hi == new_lo + 1)), 1.0, done)
        return (it + 1, new_lo, new_hi, jnp.where(live, new_w_lo, w_lo),
                jnp.where(live, new_w_hi, w_hi), jnp.where(live, new_side, side), new_done, new_thr)

    def checked_step(st):
        new = search_step(st[0])
        return new, jnp.min(new[-2])

    state = (jnp.int32(0), lo0, hi0, w_lo0, w_hi0, jnp.zeros((1, tq), F32), done0, thr0)
    state = lax.fori_loop(0, n_unchecked, lambda i, st: search_step(st), state)
    thr = as_score(lax.while_loop(search_cond, checked_step, (state, jnp.min(state[-2])))[0][-1])

    any_excess = jnp.max(count(lambda s, kb: s >= thr)) > kf

    def store_mask(kb, keep_t):
        for g in range(tq // 256):
            o_ref[g, kb] = keep_t[:, g * 256:(g + 1) * 256]

    def write_unused(kb, carry):
        store_mask(kb, jnp.zeros((256, tq), F32))
        return carry

    lax.fori_loop(nblk, nkb, write_unused, 0)

    @pl.when(jnp.logical_not(any_excess))
    def _():
        def write(kb, carry):
            store_mask(kb, jnp.where(sc_ref[kb] >= thr, 1.0, 0.0))
            return carry

        lax.fori_loop(0, nblk, write, 0)

    @pl.when(any_excess)
    def _():
        need = kf - count(lambda s, kb: s > thr)

        def idx_step(it, jmax):
            cand = jmax | lax.shift_left(jnp.int32(1), 10 - it)
            below = count(lambda s, kb: (s == thr) & ((kb * 256 + krow) < cand))
            return jnp.where(below < need, cand, jmax)

        jmax = lax.fori_loop(0, 11, idx_step, jnp.zeros((1, tq), jnp.int32))

        def write(kb, carry):
            s = sc_ref[kb]
            keep_tie = (s == thr) & ((kb * 256 + krow) <= jmax)
            store_mask(kb, jnp.where((s > thr) | keep_tie, 1.0, 0.0))
            return carry

        lax.fori_loop(0, nblk, write, 0)


def _b_select(h, bsz, seq, tq=512):
    tq = min(tq, seq)
    nt = seq // tq
    topk = min(TOPK_MAX, seq // 4)
    kern = functools.partial(_b_select_kernel, seq=seq, tq=tq, topk=topk)
    return pl.pallas_call(
        kern,
        grid=(bsz, nt),
        in_specs=[pl.BlockSpec((tq, GROUP_W), lambda b, t: (b * nt + t, BLK_B_IQ)),
                  pl.BlockSpec((tq, LANES), lambda b, t: (b * nt + t, UNIT_KR_IW)),
                  pl.BlockSpec((seq, LANES), lambda b, t: (b, UNIT_B_IK))],
        out_specs=pl.BlockSpec((tq // 256, seq // 256, 256, 256), lambda b, t: (b * nt + t, 0, 0, 0)),
        out_shape=jax.ShapeDtypeStruct((bsz * seq // 256, seq // 256, 256, 256), F32),
        scratch_shapes=[pltpu.VMEM((N_HEADS64 * tq, LANES), BF16),
                        pltpu.VMEM((seq // 256, 256, tq), F32)],
        compiler_params=_params(2),
    )(h, h, h)


def _b_attn_kernel(cfar_ref, q_ref, z_ref, k_ref, v_ref, msk_ref, qg_ref, kg_ref, base0_ref, base1_ref,
                   o_ref, kn_ref, v1_ref, bias_ref, qall_ref, s_ref, mp_ref, acc_ref, *, seq, tq):
    b = pl.program_id(0)
    t_blk = pl.program_id(1)
    hrows = [slice(h * tq, (h + 1) * tq) for h in range(N_HEADS64)]

    @pl.when((b == 0) & (t_blk == 0))
    def _():
        for h in range(N_HEADS64):
            bias_ref[0, hrows[h], :] = jnp.full((tq, 256), cfar_ref[h] * LOG2E, F32)
            bias_ref[1, hrows[h], :] = _toeplitz(base1_ref[h:h + 1, :], tq, 256) * LOG2E
            bias_ref[2, hrows[h], :] = _toeplitz(base0_ref[h:h + 1, :], tq, 256) * LOG2E

    @pl.when(t_blk == 0)
    def _():
        lo256 = _lo_mask(256)
        for r in range(seq // 256):
            rows = slice(r * 256, (r + 1) * 256)
            k = k_ref[rows, :].astype(F32)
            ms = jnp.mean(k * k, axis=-1, keepdims=True)
            kn_ref[rows, :] = (k * lax.rsqrt(ms + EPS) * kg_ref[...]).astype(BF16)
            v = v_ref[rows, :]
            v1_ref[rows, :] = jnp.where(lo256, v, jnp.ones_like(v))

    lo = _lo_mask(tq)
    qtiles = _rms_heads64(q_ref[...].astype(F32), qg_ref[...], 4)
    for h in range(N_HEADS64):
        sel = lo if h % 2 == 0 else jnp.logical_not(lo)
        qall_ref[hrows[h], :] = jnp.where(sel, qtiles[h // 2] * (0.125 * LOG2E), 0.0).astype(BF16)
    mp_ref[...] = jnp.full(mp_ref.shape, NEG, F32)
    acc_ref[...] = jnp.zeros(acc_ref.shape, F32)
    nblk = t_blk + 1

    def logits_blocks(kb0, n):
        for i in range(n):
            kb = kb0 + i
            off = pl.multiple_of(kb * 256, 256)
            kblk = kn_ref[pl.ds(off, 256), :]
            which = jnp.clip(kb - (t_blk - 2), 0, 2)
            keep = msk_ref[0, kb].T > 0.5
            for h in range(N_HEADS64):
                s = jnp.where(keep, _dot_t(qall_ref[hrows[h], :], kblk) + bias_ref[which, hrows[h], :], NEG)
                s_ref[kb, hrows[h], :] = s
                mp_ref[hrows[h], :] = jnp.maximum(mp_ref[hrows[h], :], jnp.maximum(s[:, :LANES], s[:, LANES:]))

    def value_blocks(kb0, n):
        off = pl.multiple_of(kb0 * 256, 256)
        v1 = v1_ref[pl.ds(off, n * 256), :]
        for h in range(N_HEADS64):
            m = mp_ref[hrows[h], :]
            mm = jnp.concatenate([m, m], axis=1)
            p = [jnp.exp2(s_ref[kb0 + i, hrows[h], :] - mm).astype(BF16) for i in range(n)]
            p = p[0] if n == 1 else jnp.concatenate(p, axis=1)
            acc_ref[hrows[h], :] += jnp.dot(p, v1, preferred_element_type=F32)

    _for_blocks(logits_blocks, nblk)
    for h in range(N_HEADS64):
        m = jnp.max(mp_ref[hrows[h], :], axis=-1, keepdims=True)
        mp_ref[hrows[h], :] = jnp.broadcast_to(m, (tq, LANES))
    _for_blocks(value_blocks, nblk)

    gate = _silu(z_ref[...].astype(F32))
    for t in range(4):
        a_even = acc_ref[hrows[2 * t], :]
        a_odd = acc_ref[hrows[2 * t + 1], :]
        o_even = a_even / pltpu.roll(a_even, 64, 1)
        o_odd = pltpu.roll(a_odd, 64, 1) / a_odd
        cols = slice(t * LANES, (t + 1) * LANES)
        o_ref[:, cols] = (jnp.where(lo, o_even, o_odd) * gate[:, cols]).astype(BF16)


def _b_attn(h, mask, cfar, qg, kg, base0, base1, bsz, seq, tq=256):
    nt = seq // tq
    rows = N_HEADS64 * tq
    kern = functools.partial(_b_attn_kernel, seq=seq, tq=tq)
    full = lambda shape: pl.BlockSpec(shape, lambda b, t: (0,) * len(shape))
    return pl.pallas_call(
        kern,
        grid=(bsz, nt),
        in_specs=[pl.BlockSpec(memory_space=pltpu.SMEM),
                  pl.BlockSpec((tq, GROUP_W), lambda b, t: (b * nt + t, BLK_B_Q)),
                  pl.BlockSpec((tq, GROUP_W), lambda b, t: (b * nt + t, BLK_B_Z)),
                  pl.BlockSpec((seq, LANES), lambda b, t: (b, UNIT_B_K)),
                  pl.BlockSpec((seq, LANES), lambda b, t: (b, UNIT_B_V)),
                  pl.BlockSpec((1, seq // 256, tq, 256), lambda b, t: (b * nt + t, 0, 0, 0)),
                  full((1, GROUP_W)), full((1, LANES)), full((N_HEADS64, 512)), full((N_HEADS64, 512))],
        out_specs=pl.BlockSpec((tq, GROUP_W), lambda b, t: (b * nt + t, 0)),
        out_shape=jax.ShapeDtypeStruct((bsz * seq, GROUP_W), BF16),
        scratch_shapes=[pltpu.VMEM((seq, LANES), BF16), pltpu.VMEM((seq, LANES), BF16),
                        pltpu.VMEM((3, rows, 256), F32), pltpu.VMEM((rows, LANES), BF16),
                        pltpu.VMEM((seq // 256, rows, 256), F32),
                        pltpu.VMEM((rows, LANES), F32), pltpu.VMEM((rows, LANES), F32)],
        compiler_params=_params(2),
    )(cfar, h, h, h, h, mask, qg, kg, base0, base1)


def _rope(tile, cos, sin):
    return tile * cos + pltpu.roll(tile, 64, 1) * sin


def _c_prep_kernel(lat_ref, kr_ref, wq_ref, wkv_ref, qag_ref, kvag_ref, qg_ref, kg_ref, cos_ref, sin_ref,
                   qo_ref, ko_ref, vo_ref):
    cq = lat_ref[:, :Q_LORA].astype(F32)
    ms = jnp.mean(cq * cq, axis=-1, keepdims=True)
    cqn = (cq * lax.rsqrt(ms + EPS) * qag_ref[...]).astype(BF16)
    qpre = jnp.dot(cqn, wq_ref[...], preferred_element_type=F32)
    ckv = lat_ref[:, Q_LORA:].astype(F32)
    ms = jnp.mean(ckv * ckv, axis=-1, keepdims=True)
    ckvn = (ckv * lax.rsqrt(ms + EPS) * kvag_ref[...]).astype(BF16)
    kvpre = jnp.dot(ckvn, wkv_ref[...], preferred_element_type=F32)
    lane = lax.broadcasted_iota(jnp.int32, kr_ref.shape, 1)
    kr = jnp.where((lane % 64) < 32, kr_ref[...].astype(F32), 0.0)
    kr_ss = jnp.sum(kr * kr, axis=-1, keepdims=True)
    cos = cos_ref[...]
    sin = sin_ref[...]
    qg = qg_ref[...]
    kg = kg_ref[...]
    kr_rot = _rope(kr * kg[:, LANES:], cos, sin)
    for h in range(C_HEADS):
        qh = qpre[:, h * 256:(h + 1) * 256]
        r = lax.rsqrt(jnp.sum(qh * qh, axis=-1, keepdims=True) * (1.0 / C_QK) + EPS)
        qn = qh * r * qg
        qo_ref[:, h * 256:h * 256 + LANES] = qn[:, :LANES].astype(BF16)
        qo_ref[:, h * 256 + LANES:(h + 1) * 256] = _rope(qn[:, LANES:], cos, sin).astype(BF16)
        kn = kvpre[:, h * LANES:(h + 1) * LANES]
        r = lax.rsqrt((jnp.sum(kn * kn, axis=-1, keepdims=True) + kr_ss) * (1.0 / C_QK) + EPS)
        ko_ref[:, h * 256:h * 256 + LANES] = (kn * r * kg[:, :LANES]).astype(BF16)
        ko_ref[:, h * 256 + LANES:(h + 1) * 256] = (kr_rot * r).astype(BF16)
    vo_ref[...] = kvpre[:, C_HEADS * LANES:].astype(BF16)


def _c_prep(h, wq, wkv, qag, kvag, qg, kg, cos, sin, seq, tm=512):
    n = h.shape[0]
    ns = seq // tm
    full = lambda shape: pl.BlockSpec(shape, lambda i: (0,) * len(shape))
    return pl.pallas_call(
        _c_prep_kernel,
        grid=(n // tm,),
        in_specs=[pl.BlockSpec((tm, GROUP_W), lambda i: (i, BLK_C_QKV)),
                  pl.BlockSpec((tm, LANES), lambda i: (i, UNIT_KR_IW)),
                  full((Q_LORA, 4 * 256)), full((KV_LORA, 8 * LANES)),
                  full((1, Q_LORA)), full((1, LANES)), full((1, 256)), full((1, 256)),
                  pl.BlockSpec((tm, LANES), lambda i: (i % ns, 0)),
                  pl.BlockSpec((tm, LANES), lambda i: (i % ns, 0))],
        out_specs=[pl.BlockSpec((tm, 4 * 256), lambda i: (i, 0)),
                   pl.BlockSpec((tm, 4 * 256), lambda i: (i, 0)),
                   pl.BlockSpec((tm, GROUP_W), lambda i: (i, 0))],
        out_shape=[jax.ShapeDtypeStruct((n, 4 * 256), BF16), jax.ShapeDtypeStruct((n, 4 * 256), BF16),
                   jax.ShapeDtypeStruct((n, GROUP_W), BF16)],
        compiler_params=_params(1),
    )(h, h, wq, wkv, qag, kvag, qg, kg, cos, sin)


def _c_attn_kernel(q_ref, k_ref, v_ref, z_ref, o_ref, s_ref, mp_ref, lp_ref, acc_ref, *, tq):
    qt = pl.program_id(1)
    scale = C_QK ** -0.5 * LOG2E
    hrows = [slice(h * tq, (h + 1) * tq) for h in range(C_HEADS)]
    qchunk = (qt * tq + lax.broadcasted_iota(jnp.int32, (tq, 256), 0)) // CHUNK
    kcol = lax.broadcasted_iota(jnp.int32, (tq, 256), 1)
    mp_ref[...] = jnp.full(mp_ref.shape, NEG, F32)
    lp_ref[...] = jnp.zeros(lp_ref.shape, F32)
    acc_ref[...] = jnp.zeros(acc_ref.shape, F32)
    nfull = (qt * tq) // 256
    nblk = ((qt + 1) * tq) // 256

    def logits_blocks(kb0, n, masked):
        for i in range(n):
            kb = kb0 + i
            off = pl.multiple_of(kb * 256, 256)
            for h in range(C_HEADS):
                cols = slice(h * 256, (h + 1) * 256)
                s = _dot_t(q_ref[:, cols], k_ref[pl.ds(off, 256), cols]) * scale
                if masked:
                    s = jnp.where(((kb * 256 + kcol) // CHUNK) <= qchunk, s, NEG)
                s_ref[kb, hrows[h], :] = s
                mp_ref[hrows[h], :] = jnp.maximum(mp_ref[hrows[h], :], jnp.maximum(s[:, :LANES], s[:, LANES:]))

    def value_blocks(kb0, n):
        off = pl.multiple_of(kb0 * 256, 256)
        for h in range(C_HEADS):
            m = mp_ref[hrows[h], :]
            mm = jnp.concatenate([m, m], axis=1)
            p = [jnp.exp2(s_ref[kb0 + i, hrows[h], :] - mm) for i in range(n)]
            lsum = p[0][:, :LANES] + p[0][:, LANES:]
            for pi in p[1:]:
                lsum = lsum + pi[:, :LANES] + pi[:, LANES:]
            lp_ref[hrows[h], :] += lsum
            pb = p[0].astype(BF16) if n == 1 else jnp.concatenate([pi.astype(BF16) for pi in p], axis=1)
            acc_ref[hrows[h], :] += jnp.dot(pb, v_ref[pl.ds(off, n * 256), h * LANES:(h + 1) * LANES],
                                            preferred_element_type=F32)

    _for_blocks(lambda kb0, n: logits_blocks(kb0, n, False), nfull)
    lax.fori_loop(nfull, nblk, lambda kb, c: (logits_blocks(kb, 1, True), c)[1], 0)
    for h in range(C_HEADS):
        m = jnp.max(mp_ref[hrows[h], :], axis=-1, keepdims=True)
        mp_ref[hrows[h], :] = jnp.broadcast_to(m, (tq, LANES))
    _for_blocks(value_blocks, nblk)
    gate = _silu(z_ref[...].astype(F32))
    for h in range(C_HEADS):
        cols = slice(h * LANES, (h + 1) * LANES)
        l = jnp.sum(lp_ref[hrows[h], :], axis=-1, keepdims=True)
        o_ref[:, cols] = (acc_ref[hrows[h], :] / l * gate[:, cols]).astype(BF16)


def _c_attn(qc, kc, vc, h, bsz, seq, tq=256):
    nt = seq // tq
    rows = C_HEADS * tq
    kern = functools.partial(_c_attn_kernel, tq=tq)
    return pl.pallas_call(
        kern,
        grid=(bsz, nt),
        in_specs=[pl.BlockSpec((tq, C_HEADS * 256), lambda b, t: (b * nt + t, 0)),
                  pl.BlockSpec((seq, C_HEADS * 256), lambda b, t: (b, 0)),
                  pl.BlockSpec((seq, GROUP_W), lambda b, t: (b, 0)),
                  pl.BlockSpec((tq, GROUP_W), lambda b, t: (b * nt + t, BLK_C_Z))],
        out_specs=pl.BlockSpec((tq, GROUP_W), lambda b, t: (b * nt + t, 0)),
        out_shape=jax.ShapeDtypeStruct((bsz * seq, GROUP_W), BF16),
        scratch_shapes=[pltpu.VMEM((seq // 256, rows, 256), F32), pltpu.VMEM((rows, LANES), F32),
                        pltpu.VMEM((rows, LANES), F32), pltpu.VMEM((rows, LANES), F32)],
        compiler_params=_params(2),
    )(qc, kc, vc, h)


def _mixer_d_kernel(q_ref, k_ref, v_ref, z_ref, qg_ref, kg_ref, base_ref, o_ref,
                    kpad_ref, vpad_ref, bias_ref, s_ref, mp_ref, *, seq, tq):
    b = pl.program_id(0)
    qt = pl.program_id(1)
    win = tq + D_LEFT

    @pl.when((b == 0) & (qt == 0))
    def _():
        qc = lax.broadcasted_iota(jnp.int32, (tq, win), 0) // CHUNK
        kc = lax.broadcasted_iota(jnp.int32, (tq, win), 1) // CHUNK
        band = (kc >= qc) & (kc <= qc + D_LEFT // CHUNK)
        for h in range(N_HEADS64):
            bias_ref[h] = jnp.where(band, _toeplitz(base_ref[h:h + 1, :], tq, win) * LOG2E, NEG)

    @pl.when(qt == 0)
    def _():
        kpad_ref[0:D_LEFT, :] = jnp.zeros((D_LEFT, GROUP_W), BF16)
        vpad_ref[0:D_LEFT, :] = jnp.zeros((D_LEFT, GROUP_W), BF16)
        for r in range(seq // 256):
            rows = slice(r * 256, (r + 1) * 256)
            dst = slice(D_LEFT + r * 256, D_LEFT + (r + 1) * 256)
            tiles = _rms_heads64(k_ref[rows, :].astype(F32), kg_ref[...], 4)
            for t in range(4):
                kpad_ref[dst, t * LANES:(t + 1) * LANES] = tiles[t].astype(BF16)
            vpad_ref[dst, :] = v_ref[rows, :]

    lo = _lo_mask(tq)
    qtiles = _rms_heads64(q_ref[...].astype(F32), qg_ref[...], 4)
    start = pl.multiple_of(qt * tq, tq)
    gate = _silu(z_ref[...].astype(F32))
    ntile = win // LANES

    def lane_tiles(x):
        return [x[:, i * LANES:(i + 1) * LANES] for i in range(ntile)]

    in_seq = lax.broadcasted_iota(jnp.int32, (tq, win), 1) + start >= D_LEFT

    def logits(h):
        t, half = divmod(h, 2)
        kwin = kpad_ref[pl.ds(start, win), t * LANES:(t + 1) * LANES]
        sel = lo if half == 0 else jnp.logical_not(lo)
        qh = jnp.where(sel, qtiles[t] * (0.125 * LOG2E), 0.0).astype(BF16)
        s = jnp.where(in_seq, _dot_t(qh, kwin) + bias_ref[h], NEG)
        s_ref[h] = s
        m = jnp.max(functools.reduce(jnp.maximum, lane_tiles(s)), axis=-1, keepdims=True)
        mp_ref[h] = jnp.broadcast_to(m, (tq, LANES))

    def in_band(r, c):
        return c * LANES < r * CHUNK + D_LEFT + CHUNK and (c + 1) * LANES > r * CHUNK

    def values(h):
        vwin = vpad_ref[pl.ds(start, win), (h // 2) * LANES:(h // 2 + 1) * LANES]
        p_rows, l_rows = [], []
        for r in range(tq // CHUNK):
            rows = slice(r * CHUNK, (r + 1) * CHUNK)
            m = mp_ref[h, rows, :]
            tiles = [jnp.exp2(s_ref[h, rows, c * LANES:(c + 1) * LANES] - m) if in_band(r, c) else None
                     for c in range(ntile)]
            l_rows.append(functools.reduce(jnp.add, [t for t in tiles if t is not None]))
            p_rows.append(jnp.concatenate([jnp.zeros((CHUNK, LANES), BF16) if t is None else t.astype(BF16)
                                           for t in tiles], axis=1))
        l = jnp.sum(jnp.concatenate(l_rows, axis=0), axis=-1, keepdims=True)
        return jnp.dot(jnp.concatenate(p_rows, axis=0), vwin, preferred_element_type=F32) / l

    outs = []
    logits(0)
    for h in range(1, N_HEADS64 + 1):
        if h < N_HEADS64:
            logits(h)
        outs.append(values(h - 1))
        if h % 2 == 0:
            cols = slice((h // 2 - 1) * LANES, (h // 2) * LANES)
            o_ref[:, cols] = (jnp.where(lo, outs[h - 2], outs[h - 1]) * gate[:, cols]).astype(BF16)


def _mixer_d(h, qg, kg, base, bsz, seq, tq=256):
    nt = seq // tq
    kern = functools.partial(_mixer_d_kernel, seq=seq, tq=tq)
    full = lambda shape: pl.BlockSpec(shape, lambda b, t: (0,) * len(shape))
    return pl.pallas_call(
        kern,
        grid=(bsz, nt),
        in_specs=[pl.BlockSpec((tq, GROUP_W), lambda b, t: (b * nt + t, BLK_D_Q)),
                  pl.BlockSpec((seq, GROUP_W), lambda b, t: (b, BLK_D_K)),
                  pl.BlockSpec((seq, GROUP_W), lambda b, t: (b, BLK_D_V)),
                  pl.BlockSpec((tq, GROUP_W), lambda b, t: (b * nt + t, BLK_D_Z)),
                  full((1, GROUP_W)), full((1, GROUP_W)), full((N_HEADS64, 2 * tq + D_LEFT))],
        out_specs=pl.BlockSpec((tq, GROUP_W), lambda b, t: (b * nt + t, 0)),
        out_shape=jax.ShapeDtypeStruct((bsz * seq, GROUP_W), BF16),
        scratch_shapes=[pltpu.VMEM((seq + D_LEFT, GROUP_W), BF16), pltpu.VMEM((seq + D_LEFT, GROUP_W), BF16),
                        pltpu.VMEM((N_HEADS64, tq, tq + D_LEFT), F32),
                        pltpu.VMEM((N_HEADS64, tq, tq + D_LEFT), F32), pltpu.VMEM((N_HEADS64, tq, LANES), F32)],
        compiler_params=_params(2),
    )(h, h, h, h, qg, kg, base)


def _w_in_pieces(take, zeros):
    c = lambda name, size, off=0: take(_SRC[name] + off, size)
    return [c("a_u", 512), c("a_v", 512), c("a_z", 512),
            c("b_q", 512), c("b_iq", 512), c("b_z", 512),
            c("b_k", 64), c("b_k", 64), c("b_v", 64), c("b_v", 64), c("b_ik", 64), c("b_ik", 64),
            c("c_kr", 32), c("b_iw", 8), zeros(24), c("c_kr", 32, 32), zeros(32),
            c("c_q", 384), c("c_kv", 128), c("c_z", 512),
            c("d_q", 512), c("d_k", 512), c("d_v", 512), c("d_z", 512)]


def _layout_w_in_kernel(w_ref, o_ref):
    tk = w_ref.shape[2]
    pieces = _w_in_pieces(lambda s, n: w_ref[0, s:s + n, :], lambda n: jnp.zeros((n, tk), F32))
    ends = np.cumsum([0] + [p.shape[0] for p in pieces])
    start = 0
    for i in range(1, len(pieces) + 1):
        if ends[i] % GROUP_W == 0:
            group = pieces[start:i]
            blk = group[0] if len(group) == 1 else jnp.concatenate(group, axis=0)
            o_ref[0, ends[start]:ends[i], :] = blk.astype(BF16)
            start = i


def _layout_w_in(w_in, tk=256):
    w_t = jnp.swapaxes(w_in, 1, 2)
    depth, cols, d = w_t.shape
    return pl.pallas_call(
        _layout_w_in_kernel,
        grid=(depth, d // tk),
        in_specs=[pl.BlockSpec((1, cols, tk), lambda l, i: (l, 0, i))],
        out_specs=pl.BlockSpec((1, H_COLS, tk), lambda l, i: (l, 0, i)),
        out_shape=jax.ShapeDtypeStruct((depth, H_COLS, d), BF16),
        compiler_params=_params(2),
    )(w_t)


def _rope_layout(v):
    z = jnp.zeros(v.shape[:-1] + (32,), v.dtype)
    return jnp.concatenate([v[..., :32], z, v[..., 32:], z], axis=-1)


def _layout_c(w_qb, w_kvb, q_gain, k_gain, qa_gain):
    wq = w_qb.reshape(Q_LORA, C_HEADS, C_QK)
    wq = jnp.concatenate([wq[..., :C_NOPE], _rope_layout(wq[..., C_NOPE:])], axis=-1)
    wq = wq.reshape(Q_LORA, C_HEADS * 256).astype(BF16)
    wkv = w_kvb.reshape(KV_LORA, C_HEADS, 2 * LANES)
    wkv = jnp.concatenate([wkv[..., :C_NOPE].reshape(KV_LORA, -1), wkv[..., C_NOPE:].reshape(KV_LORA, -1)],
                          axis=1).astype(BF16)
    lay = lambda g: jnp.concatenate([g[:C_NOPE], _rope_layout(g[C_NOPE:])])[None, :]
    return wq, wkv, lay(q_gain), lay(k_gain), qa_gain[None, :]


def _t5_bucket_static(rel):
    half = T5_BUCKETS // 2
    exact = half // 2
    n = abs(rel)
    if n < exact:
        val = n
    else:
        val = min(exact + (n * n // (exact * exact)).bit_length() - 1, half - 1)
    return (half if rel > 0 else 0) + val


def _t5_tables(t5_bias):
    m = np.arange(512)
    d0 = np.where(m < 256, m, m - 512)
    d1 = np.where(m <= 256, m - 256, m - 768)
    idx0 = np.array([_t5_bucket_static(int(d)) for d in d0], np.int32)
    idx1 = np.array([_t5_bucket_static(int(d)) for d in d1], np.int32)
    far = _t5_bucket_static(-512)
    return t5_bias[idx0].T, t5_bias[idx1].T, t5_bias[far]


def _band_table(rel_bias, tq):
    width = 2 * tq + D_LEFT
    m = np.arange(width)
    dist = np.where(m <= tq + D_LEFT, D_LEFT - m, D_LEFT + width - m)
    idx = np.clip(dist, -REL_CLIP, REL_CLIP) + REL_CLIP
    return rel_bias[idx.astype(np.int32)].T


def _rope_tables(seq):
    inv = ROPE_BASE ** (-jnp.arange(0, C_ROPE, 2, dtype=F32) / C_ROPE)
    ang = jnp.arange(seq, dtype=F32)[:, None] * inv[None, :]
    c, s = jnp.cos(ang), jnp.sin(ang)
    z = jnp.zeros_like(c)
    return jnp.concatenate([c, z, c, z], axis=1), jnp.concatenate([-s, z, s, z], axis=1)


def kernel(x, t5_bias, norm_g, w_in, a_v_gain, a_ws, a_bs, b_q_gain, b_k_gain, c_qa_gain, c_kva_gain,
           c_w_qb, c_w_kvb, c_q_gain, c_k_gain, d_q_gain, d_k_gain, d_rel_bias, w_out):
    bsz, seq, d_model = x.shape
    depth = w_in.shape[0]
    tq = 256
    assert seq % 512 == 0 and seq <= 2048 and d_model % 512 == 0, (seq, d_model)
    assert w_in.shape[2] == _SRC["d_z"] + GROUP_W and w_out.shape[1] == 4 * GROUP_W, (w_in.shape, w_out.shape)
    x2 = x.reshape(bsz * seq, d_model)
    cos, sin = _rope_tables(seq)
    base0, base1, cfar = _t5_tables(t5_bias)
    w_in_blocks = _layout_w_in(w_in)
    for l in range(depth):
        h = _inproj(x2, norm_g[l][None, :], w_in_blocks, l)
        mask = _b_select(h, bsz, seq)
        y_b = _b_attn(h, mask, cfar, jnp.tile(b_q_gain[l], N_HEADS64)[None, :],
                      jnp.tile(b_k_gain[l], 2)[None, :], base0, base1, bsz, seq, tq)
        wq, wkv, qg, kg, qag = _layout_c(c_w_qb[l], c_w_kvb[l], c_q_gain[l], c_k_gain[l], c_qa_gain[l])
        qc, kc, vc = _c_prep(h, wq, wkv, qag, c_kva_gain[l][None, :], qg, kg, cos, sin, seq)
        y_c = _c_attn(qc, kc, vc, h, bsz, seq, tq)
        y_d = _mixer_d(h, jnp.tile(d_q_gain[l], N_HEADS64)[None, :], jnp.tile(d_k_gain[l], N_HEADS64)[None, :],
                       _band_table(d_rel_bias[l], tq), bsz, seq, tq)
        x2 = _outproj(x2, h, a_v_gain[l][None, :], a_ws[l], a_bs[l][:, :, None], (y_b, y_c, y_d),
                      w_out[l].astype(BF16))
    return x2.reshape(bsz, seq, d_model)
```

```python
import functools
import math

import numpy as np
import jax
import jax.numpy as jnp
from jax import lax
from jax.experimental import pallas as pl
from jax.experimental.pallas import tpu as pltpu

F32 = jnp.float32
BF16 = jnp.bfloat16

EPS = 1e-6
NEG = -1e30
LOG2E = math.log2(math.e)
KEY_LOWEST = int(np.float32(-np.finfo(np.float32).max).view(np.int32)) ^ 0x7FFFFFFF
CHUNK = 64
LANES = 128
GROUP_W = 512
A_GROUPS = 4
GMLP_BLOCK = 128
N_HEADS64 = 8
IDX_SCALE = (8 ** -0.5) * 0.125
TOPK_MAX = 256
T5_BUCKETS = 32
C_HEADS = 4
C_NOPE = 128
C_ROPE = 64
C_QK = 192
Q_LORA = 384
KV_LORA = 128
ROPE_BASE = 10000.0
D_LEFT = 8 * CHUNK
REL_CLIP = 128
VMEM_LIMIT = 56 * 1024 * 1024

BLK_A_U, BLK_A_V, BLK_A_Z = 0, 1, 2
BLK_B_Q, BLK_B_IQ, BLK_B_Z, BLK_SMALL = 3, 4, 5, 6
BLK_C_QKV, BLK_C_Z = 7, 8
BLK_D_Q, BLK_D_K, BLK_D_V, BLK_D_Z = 9, 10, 11, 12
H_COLS = 13 * GROUP_W
UNIT_B_K, UNIT_B_V, UNIT_B_IK, UNIT_KR_IW = (BLK_SMALL * 4 + i for i in range(4))
IW_LANE = 32

_SRC = dict(a_u=0, a_v=512, a_z=1024, b_q=1536, b_k=2048, b_v=2112, b_iq=2176, b_ik=2688,
            b_iw=2752, b_z=2760, c_q=3272, c_kv=3656, c_kr=3784, c_z=3848,
            d_q=4360, d_k=4872, d_v=5384, d_z=5896)


def _params(n_axes):
    return pltpu.CompilerParams(dimension_semantics=("arbitrary",) * n_axes,
                                vmem_limit_bytes=VMEM_LIMIT)


def _gelu(x):
    c = math.sqrt(2.0 / math.pi)
    return x * (0.5 * (1.0 + jnp.tanh(c * (x + 0.044715 * (x * x * x)))))


def _silu(x):
    return x * (1.0 / (1.0 + jnp.exp(-x)))


def _dot_t(a, b):
    return lax.dot_general(a, b, (((1,), (1,)), ((), ())), preferred_element_type=F32)


def _lo_mask(rows):
    return lax.broadcasted_iota(jnp.int32, (rows, LANES), 1) < 64


def _rms_heads64(x, gain, head_sum):
    ss = jnp.dot((x * x).astype(BF16), head_sum, preferred_element_type=F32)
    y = x * lax.rsqrt(ss * (1.0 / 64) + EPS) * gain
    return [y[:, t * LANES:(t + 1) * LANES] for t in range(x.shape[1] // LANES)]


def _for_blocks(blocks, count):
    def quad(i, carry):
        blocks(4 * i, 4)
        return carry

    lax.fori_loop(0, count // 4, quad, 0)
    first = (count // 4) * 4

    @pl.when((count & 2) != 0)
    def _():
        blocks(first, 2)

    @pl.when((count & 1) != 0)
    def _():
        blocks(first + (count & 2), 1)


def _toeplitz(base_row, rows, width):
    t = jnp.broadcast_to(base_row, (rows, base_row.shape[1]))
    t = pltpu.roll(t, 0, 1, stride=1, stride_axis=0)
    return t[:, :width]


def _inproj_kernel(x_ref, g_ref, w_ref, o_ref):
    x = x_ref[...]
    ms = jnp.mean(x * x, axis=-1, keepdims=True)
    xn = (x * lax.rsqrt(ms + EPS) * g_ref[...]).astype(BF16)
    for c in range(H_COLS // GROUP_W):
        cols = slice(c * GROUP_W, (c + 1) * GROUP_W)
        o_ref[:, cols] = _dot_t(xn, w_ref[cols, :]).astype(BF16)


def _inproj(x2, g, w_all, layer, tm=512):
    n, d = x2.shape
    return pl.pallas_call(
        _inproj_kernel,
        grid=(n // tm,),
        in_specs=[pl.BlockSpec((tm, d), lambda i: (i, 0)),
                  pl.BlockSpec((1, d), lambda i: (0, 0)),
                  pl.BlockSpec((None, H_COLS, d), lambda i: (layer, 0, 0), pipeline_mode=pl.Buffered(1))],
        out_specs=pl.BlockSpec((tm, H_COLS), lambda i: (i, 0)),
        out_shape=jax.ShapeDtypeStruct((n, H_COLS), BF16),
        compiler_params=_params(1),
    )(x2, g, w_all)


def _outproj_kernel(x_ref, u_ref, v_ref, z_ref, vg_ref, ws_ref, bs_ref, yb_ref, yc_ref, yd_ref, w_ref, o_ref,
                    ya_ref):
    tm, d = x_ref.shape
    wgs = _mixer_a_weights(ws_ref)
    nblk = tm // GMLP_BLOCK
    ncol = d // nblk
    accs = []
    for c in range(nblk):
        cols = slice(c * ncol, (c + 1) * ncol)
        acc = x_ref[:, cols]
        for g, y_ref in ((1, yb_ref), (2, yc_ref), (3, yd_ref)):
            acc = acc + jnp.dot(y_ref[...], w_ref[g * GROUP_W:(g + 1) * GROUP_W, cols],
                                preferred_element_type=F32)
        accs.append(acc)
        _mixer_a_rows(u_ref, v_ref, z_ref, vg_ref, wgs, bs_ref, ya_ref, c)
    for c in range(nblk):
        cols = slice(c * ncol, (c + 1) * ncol)
        o_ref[:, cols] = accs[c] + jnp.dot(ya_ref[...], w_ref[0:GROUP_W, cols], preferred_element_type=F32)


def _outproj(x2, h, vg, ws, bs, ys, w, tm=512):
    n, d = x2.shape
    yspec = pl.BlockSpec((tm, GROUP_W), lambda i: (i, 0))
    hspec = lambda blk: pl.BlockSpec((tm, GROUP_W), lambda i, blk=blk: (i, blk))
    return pl.pallas_call(
        _outproj_kernel,
        grid=(n // tm,),
        in_specs=[pl.BlockSpec((tm, d), lambda i: (i, 0)),
                  hspec(BLK_A_U), hspec(BLK_A_V), hspec(BLK_A_Z),
                  pl.BlockSpec((1, GROUP_W), lambda i: (0, 0)),
                  pl.BlockSpec((A_GROUPS, GMLP_BLOCK, GMLP_BLOCK), lambda i: (0, 0, 0)),
                  pl.BlockSpec((A_GROUPS, GMLP_BLOCK, 1), lambda i: (0, 0, 0)),
                  yspec, yspec, yspec,
                  pl.BlockSpec((4 * GROUP_W, d), lambda i: (0, 0))],
        out_specs=pl.BlockSpec((tm, d), lambda i: (i, 0)),
        out_shape=jax.ShapeDtypeStruct((n, d), F32),
        scratch_shapes=[pltpu.VMEM((tm, GROUP_W), BF16)],
        compiler_params=_params(1),
    )(x2, h, h, h, vg, ws, bs, *ys, w)


def _mixer_a_weights(w_ref):
    i = lax.broadcasted_iota(jnp.int32, (GMLP_BLOCK, GMLP_BLOCK), 0)
    j = lax.broadcasted_iota(jnp.int32, (GMLP_BLOCK, GMLP_BLOCK), 1)
    keep = (j // CHUNK) <= (i // CHUNK)
    return [jnp.where(keep, w_ref[g], 0.0).astype(BF16) for g in range(A_GROUPS)]


def _mixer_a_rows(u_ref, v_ref, z_ref, vg_ref, wgs, b_ref, o_ref, blk):
    rows = slice(blk * GMLP_BLOCK, (blk + 1) * GMLP_BLOCK)
    u = _gelu(u_ref[rows, :].astype(F32))
    v = _gelu(v_ref[rows, :].astype(F32))
    ms = jnp.mean(v * v, axis=-1, keepdims=True)
    vb = (v * lax.rsqrt(ms + EPS) * vg_ref[...]).astype(BF16)
    gate = _silu(z_ref[rows, :].astype(F32))
    for g in range(A_GROUPS):
        cols = slice(g * LANES, (g + 1) * LANES)
        sg = jnp.dot(wgs[g], vb[:, cols], preferred_element_type=F32) + b_ref[g]
        o_ref[rows, cols] = (u[:, cols] * sg * gate[:, cols]).astype(BF16)


def _order_key(x):
    return jnp.where(x < 0, x ^ 0x7FFFFFFF, x)


def _b_select_kernel(iq_ref, iw_ref, ik_ref, o_ref, lhs_ref, sc_ref, *, seq, tq, topk):
    t_blk = pl.program_id(1)
    nkb = seq // 256
    nblk = ((t_blk + 1) * tq) // 256
    n_interp = 12
    n_unchecked = 12 + (3 * t_blk) // 2
    hrows = [slice(h * tq, (h + 1) * tq) for h in range(N_HEADS64)]

    lo_half = _lo_mask(tq)
    w_t = (iw_ref[...].astype(F32) * IDX_SCALE).T
    for h in range(N_HEADS64):
        iqt = iq_ref[:, (h // 2) * LANES:(h // 2 + 1) * LANES]
        sel = lo_half if h % 2 == 0 else jnp.logical_not(lo_half)
        lhs_ref[hrows[h], :] = jnp.where(sel, iqt, jnp.zeros_like(iqt))

    qpos = t_blk * tq + lax.broadcasted_iota(jnp.int32, (256, tq), 1)
    krow = lax.broadcasted_iota(jnp.int32, (256, tq), 0)

    def fold8(x):
        return jnp.sum(x.reshape(256 // 8, 8, tq), axis=0)

    def score_block(kb, carry, masked):
        amax, n_pos, n_nn = carry
        off = pl.multiple_of(kb * 256, 256)
        ikblk = ik_ref[pl.ds(off, 256), :]
        score = jnp.zeros((256, tq), F32)
        for h in range(N_HEADS64):
            w_h = w_t[IW_LANE + h:IW_LANE + h + 1, :]
            score = score + w_h * jnp.maximum(_dot_t(ikblk, lhs_ref[hrows[h], :]), 0.0)
        mag = jnp.abs(score)
        if masked:
            adm = ((kb * 256 + krow) // CHUNK) <= (qpos // CHUNK)
            score = jnp.where(adm, score, -jnp.inf)
            mag = jnp.where(adm, mag, 0.0)
        sc_ref[kb] = score
        return (jnp.maximum(amax, jnp.max(mag.reshape(256 // 8, 8, tq), axis=0)),
                n_pos + fold8(jnp.where(score > 0.0, 1.0, 0.0)), n_nn + fold8(jnp.where(score >= 0.0, 1.0, 0.0)))

    nfull = (t_blk * tq) // 256
    zeros8 = jnp.zeros((8, tq), F32)
    carry = lax.fori_loop(0, nfull, lambda kb, c: score_block(kb, c, False), (zeros8, zeros8, zeros8))
    carry = lax.fori_loop(nfull, nblk, lambda kb, c: score_block(kb, c, True), carry)
    amax = jnp.max(carry[0], axis=0, keepdims=True)
    f_pos = jnp.sum(carry[1], axis=0, keepdims=True)
    f_nn = jnp.sum(carry[2], axis=0, keepdims=True)

    def count(pred):
        def body(kb, acc):
            return acc + fold8(jnp.where(pred(sc_ref[kb], kb), 1.0, 0.0))
        return jnp.sum(lax.fori_loop(0, nblk, body, zeros8), axis=0, keepdims=True)

    kf = float(topk)
    qrow = t_blk * tq + lax.broadcasted_iota(jnp.int32, (1, tq), 1)
    n_adm = ((qrow // CHUNK + 1) * CHUNK).astype(F32)
    one = jnp.ones((1, tq), jnp.int32)
    pos = f_pos > kf
    neg = f_nn < kf
    lo0 = jnp.where(pos, one, _order_key(lax.bitcast_convert_type(-amax, jnp.int32)))
    hi0 = jnp.where(neg, one - 1, _order_key(lax.bitcast_convert_type(amax, jnp.int32)) + 1)
    w_lo0 = jnp.where(pos, f_pos, n_adm) - kf
    w_hi0 = kf - jnp.where(neg, f_nn, 0.0)
    all_sel = n_adm <= kf
    at_zero = jnp.logical_not(pos | neg)
    done0 = jnp.where(all_sel | at_zero | (hi0 == lo0 + 1), 1.0, 0.0)
    thr0 = jnp.where(all_sel, KEY_LOWEST, jnp.where(at_zero, jnp.where(f_pos == kf, one, one - 1), lo0))

    def as_score(key):
        return lax.bitcast_convert_type(_order_key(key), F32)

    def search_cond(st):
        return jnp.logical_and(st[0][0] < n_interp + 32, st[1] < 0.5)

    def search_step(st):
        it, lo, hi, w_lo, w_hi, side, done, thr = st
        lo_v = as_score(lo)
        hi_v = as_score(hi)
        c_v = lo_v + (hi_v - lo_v) * (w_lo / (w_lo + w_hi))
        c_interp = _order_key(lax.bitcast_convert_type(c_v, jnp.int32))
        c_mid = (lo >> 1) + (hi >> 1) + (lo & hi & 1)
        cand = jnp.where(it < n_interp, c_interp, c_mid)
        cand = jnp.minimum(jnp.maximum(cand, lo + 1), hi - 1)
        cand_v = as_score(cand)
        f = count(lambda s, kb: s >= cand_v)
        live = done < 0.5
        up = f > kf
        hit = f == kf
        new_lo = jnp.where(live & up, cand, lo)
        new_hi = jnp.where(live & jnp.logical_not(up), cand, hi)
        new_w_lo = jnp.where(up, f - kf, jnp.where(side < 0.0, 0.5 * w_lo, w_lo))
        new_w_hi = jnp.where(up, jnp.where(side > 0.0, 0.5 * w_hi, w_hi), kf - f)
        new_side = jnp.where(up, 1.0, -1.0)
        new_thr = jnp.where(live, jnp.where(hit, cand, new_lo), thr)
        new_done = jnp.where(live & (hit | (new_hi == new_lo + 1)), 1.0, done)
        return (it + 1, new_lo, new_hi, jnp.where(live, new_w_lo, w_lo),
                jnp.where(live, new_w_hi, w_hi), jnp.where(live, new_side, side), new_done, new_thr)

    def checked_step(st):
        new = search_step(st[0])
        return new, jnp.min(new[-2])

    state = (jnp.int32(0), lo0, hi0, w_lo0, w_hi0, jnp.zeros((1, tq), F32), done0, thr0)
    state = lax.fori_loop(0, n_unchecked, lambda i, st: search_step(st), state)
    thr = as_score(lax.while_loop(search_cond, checked_step, (state, jnp.min(state[-2])))[0][-1])

    any_excess = jnp.max(count(lambda s, kb: s >= thr)) > kf

    def store_mask(kb, keep_t):
        for g in range(tq // 256):
            o_ref[g, kb] = keep_t[:, g * 256:(g + 1) * 256]

    def write_unused(kb, carry):
        store_mask(kb, jnp.zeros((256, tq), F32))
        return carry

    lax.fori_loop(nblk, nkb, write_unused, 0)

    @pl.when(jnp.logical_not(any_excess))
    def _():
        def write(kb, carry):
            store_mask(kb, jnp.where(sc_ref[kb] >= thr, 1.0, 0.0))
            return carry

        lax.fori_loop(0, nblk, write, 0)

    @pl.when(any_excess)
    def _():
        need = kf - count(lambda s, kb: s > thr)

        def idx_step(it, jmax):
            cand = jmax | lax.shift_left(jnp.int32(1), 10 - it)
            below = count(lambda s, kb: (s == thr) & ((kb * 256 + krow) < cand))
            return jnp.where(below < need, cand, jmax)

        jmax = lax.fori_loop(0, 11, idx_step, jnp.zeros((1, tq), jnp.int32))

        def write(kb, carry):
            s = sc_ref[kb]
            keep_tie = (s == thr) & ((kb * 256 + krow) <= jmax)
            store_mask(kb, jnp.where((s > thr) | keep_tie, 1.0, 0.0))
            return carry

        lax.fori_loop(0, nblk, write, 0)


def _b_select(h, bsz, seq, tq=512):
    tq = min(tq, seq)
    nt = seq // tq
    topk = min(TOPK_MAX, seq // 4)
    kern = functools.partial(_b_select_kernel, seq=seq, tq=tq, topk=topk)
    return pl.pallas_call(
        kern,
        grid=(bsz, nt),
        in_specs=[pl.BlockSpec((tq, GROUP_W), lambda b, t: (b * nt + t, BLK_B_IQ)),
                  pl.BlockSpec((tq, LANES), lambda b, t: (b * nt + t, UNIT_KR_IW)),
                  pl.BlockSpec((seq, LANES), lambda b, t: (b, UNIT_B_IK))],
        out_specs=pl.BlockSpec((tq // 256, seq // 256, 256, 256), lambda b, t: (b * nt + t, 0, 0, 0)),
        out_shape=jax.ShapeDtypeStruct((bsz * seq // 256, seq // 256, 256, 256), F32),
        scratch_shapes=[pltpu.VMEM((N_HEADS64 * tq, LANES), BF16),
                        pltpu.VMEM((seq // 256, 256, tq), F32)],
        compiler_params=_params(2),
    )(h, h, h)


def _b_attn_kernel(cfar_ref, q_ref, z_ref, k_ref, v_ref, msk_ref, qg_ref, kg_ref, base0_ref, base1_ref, hs_ref,
                   o_ref, kn_ref, v1_ref, bias_ref, qall_ref, s_ref, mp_ref, acc_ref, *, seq, tq):
    b = pl.program_id(0)
    t_blk = pl.program_id(1)
    hrows = [slice(h * tq, (h + 1) * tq) for h in range(N_HEADS64)]

    @pl.when((b == 0) & (t_blk == 0))
    def _():
        for h in range(N_HEADS64):
            bias_ref[0, hrows[h], :] = jnp.full((tq, 256), cfar_ref[h] * LOG2E, F32)
            bias_ref[1, hrows[h], :] = _toeplitz(base1_ref[h:h + 1, :], tq, 256) * LOG2E
            bias_ref[2, hrows[h], :] = _toeplitz(base0_ref[h:h + 1, :], tq, 256) * LOG2E

    @pl.when(t_blk == 0)
    def _():
        lo256 = _lo_mask(256)
        for r in range(seq // 256):
            rows = slice(r * 256, (r + 1) * 256)
            k = k_ref[rows, :].astype(F32)
            ms = jnp.mean(k * k, axis=-1, keepdims=True)
            kn_ref[rows, :] = (k * lax.rsqrt(ms + EPS) * kg_ref[...]).astype(BF16)
            v = v_ref[rows, :]
            v1_ref[rows, :] = jnp.where(lo256, v, jnp.ones_like(v))

    lo = _lo_mask(tq)
    qtiles = _rms_heads64(q_ref[...].astype(F32), qg_ref[...], hs_ref[...])
    for h in range(N_HEADS64):
        sel = lo if h % 2 == 0 else jnp.logical_not(lo)
        qall_ref[hrows[h], :] = jnp.where(sel, qtiles[h // 2] * (0.125 * LOG2E), 0.0).astype(BF16)
    mp_ref[...] = jnp.full(mp_ref.shape, NEG, F32)
    acc_ref[...] = jnp.zeros(acc_ref.shape, F32)
    nblk = t_blk + 1

    def logits_blocks(kb0, n):
        for i in range(n):
            kb = kb0 + i
            off = pl.multiple_of(kb * 256, 256)
            kblk = kn_ref[pl.ds(off, 256), :]
            which = jnp.clip(kb - (t_blk - 2), 0, 2)
            keep = msk_ref[0, kb].T > 0.5
            for h in range(N_HEADS64):
                s = jnp.where(keep, _dot_t(qall_ref[hrows[h], :], kblk) + bias_ref[which, hrows[h], :], NEG)
                s_ref[kb, hrows[h], :] = s
                mp_ref[hrows[h], :] = jnp.maximum(mp_ref[hrows[h], :], jnp.maximum(s[:, :LANES], s[:, LANES:]))

    def value_blocks(kb0, n):
        off = pl.multiple_of(kb0 * 256, 256)
        v1 = v1_ref[pl.ds(off, n * 256), :]
        for h in range(N_HEADS64):
            m = mp_ref[hrows[h], :]
            mm = jnp.concatenate([m, m], axis=1)
            p = [jnp.exp2(s_ref[kb0 + i, hrows[h], :] - mm).astype(BF16) for i in range(n)]
            p = p[0] if n == 1 else jnp.concatenate(p, axis=1)
            acc_ref[hrows[h], :] += jnp.dot(p, v1, preferred_element_type=F32)

    _for_blocks(logits_blocks, nblk)
    for h in range(N_HEADS64):
        m = jnp.max(mp_ref[hrows[h], :], axis=-1, keepdims=True)
        mp_ref[hrows[h], :] = jnp.broadcast_to(m, (tq, LANES))
    _for_blocks(value_blocks, nblk)

    gate = _silu(z_ref[...].astype(F32))
    for t in range(4):
        a_even = acc_ref[hrows[2 * t], :]
        a_odd = acc_ref[hrows[2 * t + 1], :]
        o_even = a_even / pltpu.roll(a_even, 64, 1)
        o_odd = pltpu.roll(a_odd, 64, 1) / a_odd
        cols = slice(t * LANES, (t + 1) * LANES)
        o_ref[:, cols] = (jnp.where(lo, o_even, o_odd) * gate[:, cols]).astype(BF16)


def _b_attn(h, mask, cfar, qg, kg, base0, base1, head_sum, bsz, seq, tq=256):
    nt = seq // tq
    rows = N_HEADS64 * tq
    kern = functools.partial(_b_attn_kernel, seq=seq, tq=tq)
    full = lambda shape: pl.BlockSpec(shape, lambda b, t: (0,) * len(shape))
    return pl.pallas_call(
        kern,
        grid=(bsz, nt),
        in_specs=[pl.BlockSpec(memory_space=pltpu.SMEM),
                  pl.BlockSpec((tq, GROUP_W), lambda b, t: (b * nt + t, BLK_B_Q)),
                  pl.BlockSpec((tq, GROUP_W), lambda b, t: (b * nt + t, BLK_B_Z)),
                  pl.BlockSpec((seq, LANES), lambda b, t: (b, UNIT_B_K)),
                  pl.BlockSpec((seq, LANES), lambda b, t: (b, UNIT_B_V)),
                  pl.BlockSpec((1, seq // 256, tq, 256), lambda b, t: (b * nt + t, 0, 0, 0)),
                  full((1, GROUP_W)), full((1, LANES)), full((N_HEADS64, 512)), full((N_HEADS64, 512)),
                  full((GROUP_W, GROUP_W))],
        out_specs=pl.BlockSpec((tq, GROUP_W), lambda b, t: (b * nt + t, 0)),
        out_shape=jax.ShapeDtypeStruct((bsz * seq, GROUP_W), BF16),
        scratch_shapes=[pltpu.VMEM((seq, LANES), BF16), pltpu.VMEM((seq, LANES), BF16),
                        pltpu.VMEM((3, rows, 256), F32), pltpu.VMEM((rows, LANES), BF16),
                        pltpu.VMEM((seq // 256, rows, 256), F32),
                        pltpu.VMEM((rows, LANES), F32), pltpu.VMEM((rows, LANES), F32)],
        compiler_params=_params(2),
    )(cfar, h, h, h, h, mask, qg, kg, base0, base1, head_sum)


def _rope(tile, cos, sin):
    return tile * cos + pltpu.roll(tile, 64, 1) * sin


def _c_prep_kernel(lat_ref, kr_ref, wq_ref, wkv_ref, qag_ref, kvag_ref, qg_ref, kg_ref, cos_ref, sin_ref,
                   qo_ref, ko_ref, vo_ref):
    cq = lat_ref[:, :Q_LORA].astype(F32)
    ms = jnp.mean(cq * cq, axis=-1, keepdims=True)
    cqn = (cq * lax.rsqrt(ms + EPS) * qag_ref[...]).astype(BF16)
    qpre = jnp.dot(cqn, wq_ref[...], preferred_element_type=F32)
    ckv = lat_ref[:, Q_LORA:].astype(F32)
    ms = jnp.mean(ckv * ckv, axis=-1, keepdims=True)
    ckvn = (ckv * lax.rsqrt(ms + EPS) * kvag_ref[...]).astype(BF16)
    kvpre = jnp.dot(ckvn, wkv_ref[...], preferred_element_type=F32)
    lane = lax.broadcasted_iota(jnp.int32, kr_ref.shape, 1)
    kr = jnp.where((lane % 64) < 32, kr_ref[...].astype(F32), 0.0)
    kr_ss = jnp.sum(kr * kr, axis=-1, keepdims=True)
    cos = cos_ref[...]
    sin = sin_ref[...]
    qg = qg_ref[...]
    kg = kg_ref[...]
    kr_rot = _rope(kr * kg[:, LANES:], cos, sin)
    for h in range(C_HEADS):
        qh = qpre[:, h * 256:(h + 1) * 256]
        r = lax.rsqrt(jnp.sum(qh * qh, axis=-1, keepdims=True) * (1.0 / C_QK) + EPS)
        qn = qh * r * qg
        qo_ref[:, h * 256:h * 256 + LANES] = qn[:, :LANES].astype(BF16)
        qo_ref[:, h * 256 + LANES:(h + 1) * 256] = _rope(qn[:, LANES:], cos, sin).astype(BF16)
        kn = kvpre[:, h * LANES:(h + 1) * LANES]
        r = lax.rsqrt((jnp.sum(kn * kn, axis=-1, keepdims=True) + kr_ss) * (1.0 / C_QK) + EPS)
        ko_ref[:, h * 256:h * 256 + LANES] = (kn * r * kg[:, :LANES]).astype(BF16)
        ko_ref[:, h * 256 + LANES:(h + 1) * 256] = (kr_rot * r).astype(BF16)
    vo_ref[...] = kvpre[:, C_HEADS * LANES:].astype(BF16)


def _c_prep(h, wq, wkv, qag, kvag, qg, kg, cos, sin, seq, tm=512):
    n = h.shape[0]
    ns = seq // tm
    full = lambda shape: pl.BlockSpec(shape, lambda i: (0,) * len(shape))
    return pl.pallas_call(
        _c_prep_kernel,
        grid=(n // tm,),
        in_specs=[pl.BlockSpec((tm, GROUP_W), lambda i: (i, BLK_C_QKV)),
                  pl.BlockSpec((tm, LANES), lambda i: (i, UNIT_KR_IW)),
                  full((Q_LORA, 4 * 256)), full((KV_LORA, 8 * LANES)),
                  full((1, Q_LORA)), full((1, LANES)), full((1, 256)), full((1, 256)),
                  pl.BlockSpec((tm, LANES), lambda i: (i % ns, 0)),
                  pl.BlockSpec((tm, LANES), lambda i: (i % ns, 0))],
        out_specs=[pl.BlockSpec((tm, 4 * 256), lambda i: (i, 0)),
                   pl.BlockSpec((tm, 4 * 256), lambda i: (i, 0)),
                   pl.BlockSpec((tm, GROUP_W), lambda i: (i, 0))],
        out_shape=[jax.ShapeDtypeStruct((n, 4 * 256), BF16), jax.ShapeDtypeStruct((n, 4 * 256), BF16),
                   jax.ShapeDtypeStruct((n, GROUP_W), BF16)],
        compiler_params=_params(1),
    )(h, h, wq, wkv, qag, kvag, qg, kg, cos, sin)


def _c_attn_kernel(q_ref, k_ref, v_ref, z_ref, o_ref, s_ref, mp_ref, lp_ref, acc_ref, *, tq):
    qt = pl.program_id(1)
    scale = C_QK ** -0.5 * LOG2E
    hrows = [slice(h * tq, (h + 1) * tq) for h in range(C_HEADS)]
    qchunk = (qt * tq + lax.broadcasted_iota(jnp.int32, (tq, 256), 0)) // CHUNK
    kcol = lax.broadcasted_iota(jnp.int32, (tq, 256), 1)
    mp_ref[...] = jnp.full(mp_ref.shape, NEG, F32)
    lp_ref[...] = jnp.zeros(lp_ref.shape, F32)
    acc_ref[...] = jnp.zeros(acc_ref.shape, F32)
    nfull = (qt * tq) // 256
    nblk = ((qt + 1) * tq) // 256

    def logits_blocks(kb0, n, masked):
        for i in range(n):
            kb = kb0 + i
            off = pl.multiple_of(kb * 256, 256)
            for h in range(C_HEADS):
                cols = slice(h * 256, (h + 1) * 256)
                s = _dot_t(q_ref[:, cols], k_ref[pl.ds(off, 256), cols]) * scale
                if masked:
                    s = jnp.where(((kb * 256 + kcol) // CHUNK) <= qchunk, s, NEG)
                s_ref[kb, hrows[h], :] = s
                mp_ref[hrows[h], :] = jnp.maximum(mp_ref[hrows[h], :], jnp.maximum(s[:, :LANES], s[:, LANES:]))

    def value_blocks(kb0, n):
        off = pl.multiple_of(kb0 * 256, 256)
        for h in range(C_HEADS):
            m = mp_ref[hrows[h], :]
            mm = jnp.concatenate([m, m], axis=1)
            p = [jnp.exp2(s_ref[kb0 + i, hrows[h], :] - mm) for i in range(n)]
            lsum = p[0][:, :LANES] + p[0][:, LANES:]
            for pi in p[1:]:
                lsum = lsum + pi[:, :LANES] + pi[:, LANES:]
            lp_ref[hrows[h], :] += lsum
            pb = p[0].astype(BF16) if n == 1 else jnp.concatenate([pi.astype(BF16) for pi in p], axis=1)
            acc_ref[hrows[h], :] += jnp.dot(pb, v_ref[pl.ds(off, n * 256), h * LANES:(h + 1) * LANES],
                                            preferred_element_type=F32)

    _for_blocks(lambda kb0, n: logits_blocks(kb0, n, False), nfull)
    lax.fori_loop(nfull, nblk, lambda kb, c: (logits_blocks(kb, 1, True), c)[1], 0)
    for h in range(C_HEADS):
        m = jnp.max(mp_ref[hrows[h], :], axis=-1, keepdims=True)
        mp_ref[hrows[h], :] = jnp.broadcast_to(m, (tq, LANES))
    _for_blocks(value_blocks, nblk)
    gate = _silu(z_ref[...].astype(F32))
    for h in range(C_HEADS):
        cols = slice(h * LANES, (h + 1) * LANES)
        l = jnp.sum(lp_ref[hrows[h], :], axis=-1, keepdims=True)
        o_ref[:, cols] = (acc_ref[hrows[h], :] / l * gate[:, cols]).astype(BF16)


def _c_attn(qc, kc, vc, h, bsz, seq, tq=256):
    nt = seq // tq
    rows = C_HEADS * tq
    kern = functools.partial(_c_attn_kernel, tq=tq)
    return pl.pallas_call(
        kern,
        grid=(bsz, nt),
        in_specs=[pl.BlockSpec((tq, C_HEADS * 256), lambda b, t: (b * nt + t, 0)),
                  pl.BlockSpec((seq, C_HEADS * 256), lambda b, t: (b, 0)),
                  pl.BlockSpec((seq, GROUP_W), lambda b, t: (b, 0)),
                  pl.BlockSpec((tq, GROUP_W), lambda b, t: (b * nt + t, BLK_C_Z))],
        out_specs=pl.BlockSpec((tq, GROUP_W), lambda b, t: (b * nt + t, 0)),
        out_shape=jax.ShapeDtypeStruct((bsz * seq, GROUP_W), BF16),
        scratch_shapes=[pltpu.VMEM((seq // 256, rows, 256), F32), pltpu.VMEM((rows, LANES), F32),
                        pltpu.VMEM((rows, LANES), F32), pltpu.VMEM((rows, LANES), F32)],
        compiler_params=_params(2),
    )(qc, kc, vc, h)


def _mixer_d_kernel(q_ref, k_ref, v_ref, z_ref, qg_ref, kg_ref, base_ref, hs_ref, o_ref,
                    kpad_ref, vpad_ref, bias_ref, s_ref, mp_ref, *, seq, tq):
    b = pl.program_id(0)
    qt = pl.program_id(1)
    win = tq + D_LEFT

    @pl.when((b == 0) & (qt == 0))
    def _():
        qc = lax.broadcasted_iota(jnp.int32, (tq, win), 0) // CHUNK
        kc = lax.broadcasted_iota(jnp.int32, (tq, win), 1) // CHUNK
        band = (kc >= qc) & (kc <= qc + D_LEFT // CHUNK)
        for h in range(N_HEADS64):
            bias_ref[h] = jnp.where(band, _toeplitz(base_ref[h:h + 1, :], tq, win) * LOG2E, NEG)

    @pl.when(qt == 0)
    def _():
        kpad_ref[0:D_LEFT, :] = jnp.zeros((D_LEFT, GROUP_W), BF16)
        vpad_ref[0:D_LEFT, :] = jnp.zeros((D_LEFT, GROUP_W), BF16)
        for r in range(seq // 256):
            rows = slice(r * 256, (r + 1) * 256)
            dst = slice(D_LEFT + r * 256, D_LEFT + (r + 1) * 256)
            tiles = _rms_heads64(k_ref[rows, :].astype(F32), kg_ref[...], hs_ref[...])
            for t in range(4):
                kpad_ref[dst, t * LANES:(t + 1) * LANES] = tiles[t].astype(BF16)
            vpad_ref[dst, :] = v_ref[rows, :]

    lo = _lo_mask(tq)
    qtiles = _rms_heads64(q_ref[...].astype(F32), qg_ref[...], hs_ref[...])
    start = pl.multiple_of(qt * tq, tq)
    gate = _silu(z_ref[...].astype(F32))
    ntile = win // LANES

    def lane_tiles(x):
        return [x[:, i * LANES:(i + 1) * LANES] for i in range(ntile)]

    in_seq = lax.broadcasted_iota(jnp.int32, (tq, win), 1) + start >= D_LEFT

    def logits(h):
        t, half = divmod(h, 2)
        kwin = kpad_ref[pl.ds(start, win), t * LANES:(t + 1) * LANES]
        sel = lo if half == 0 else jnp.logical_not(lo)
        qh = jnp.where(sel, qtiles[t] * (0.125 * LOG2E), 0.0).astype(BF16)
        s = jnp.where(in_seq, _dot_t(qh, kwin) + bias_ref[h], NEG)
        s_ref[h] = s
        m = jnp.max(functools.reduce(jnp.maximum, lane_tiles(s)), axis=-1, keepdims=True)
        mp_ref[h] = jnp.broadcast_to(m, (tq, LANES))

    def in_band(r, c):
        return c * LANES < r * CHUNK + D_LEFT + CHUNK and (c + 1) * LANES > r * CHUNK

    def values(h):
        vwin = vpad_ref[pl.ds(start, win), (h // 2) * LANES:(h // 2 + 1) * LANES]
        p_rows, l_rows = [], []
        for r in range(tq // CHUNK):
            rows = slice(r * CHUNK, (r + 1) * CHUNK)
            m = mp_ref[h, rows, :]
            tiles = [jnp.exp2(s_ref[h, rows, c * LANES:(c + 1) * LANES] - m) if in_band(r, c) else None
                     for c in range(ntile)]
            l_rows.append(functools.reduce(jnp.add, [t for t in tiles if t is not None]))
            p_rows.append(jnp.concatenate([jnp.zeros((CHUNK, LANES), BF16) if t is None else t.astype(BF16)
                                           for t in tiles], axis=1))
        l = jnp.sum(jnp.concatenate(l_rows, axis=0), axis=-1, keepdims=True)
        return jnp.dot(jnp.concatenate(p_rows, axis=0), vwin, preferred_element_type=F32) / l

    outs = []
    logits(0)
    for h in range(1, N_HEADS64 + 1):
        if h < N_HEADS64:
            logits(h)
        outs.append(values(h - 1))
        if h % 2 == 0:
            cols = slice((h // 2 - 1) * LANES, (h // 2) * LANES)
            o_ref[:, cols] = (jnp.where(lo, outs[h - 2], outs[h - 1]) * gate[:, cols]).astype(BF16)


def _mixer_d(h, qg, kg, base, head_sum, bsz, seq, tq=256):
    nt = seq // tq
    kern = functools.partial(_mixer_d_kernel, seq=seq, tq=tq)
    full = lambda shape: pl.BlockSpec(shape, lambda b, t: (0,) * len(shape))
    return pl.pallas_call(
        kern,
        grid=(bsz, nt),
        in_specs=[pl.BlockSpec((tq, GROUP_W), lambda b, t: (b * nt + t, BLK_D_Q)),
                  pl.BlockSpec((seq, GROUP_W), lambda b, t: (b, BLK_D_K)),
                  pl.BlockSpec((seq, GROUP_W), lambda b, t: (b, BLK_D_V)),
                  pl.BlockSpec((tq, GROUP_W), lambda b, t: (b * nt + t, BLK_D_Z)),
                  full((1, GROUP_W)), full((1, GROUP_W)), full((N_HEADS64, 2 * tq + D_LEFT)),
                  full((GROUP_W, GROUP_W))],
        out_specs=pl.BlockSpec((tq, GROUP_W), lambda b, t: (b * nt + t, 0)),
        out_shape=jax.ShapeDtypeStruct((bsz * seq, GROUP_W), BF16),
        scratch_shapes=[pltpu.VMEM((seq + D_LEFT, GROUP_W), BF16), pltpu.VMEM((seq + D_LEFT, GROUP_W), BF16),
                        pltpu.VMEM((N_HEADS64, tq, tq + D_LEFT), F32),
                        pltpu.VMEM((N_HEADS64, tq, tq + D_LEFT), F32), pltpu.VMEM((N_HEADS64, tq, LANES), F32)],
        compiler_params=_params(2),
    )(h, h, h, h, qg, kg, base, head_sum)


def _w_in_pieces(take, zeros):
    c = lambda name, size, off=0: take(_SRC[name] + off, size)
    return [c("a_u", 512), c("a_v", 512), c("a_z", 512),
            c("b_q", 512), c("b_iq", 512), c("b_z", 512),
            c("b_k", 64), c("b_k", 64), c("b_v", 64), c("b_v", 64), c("b_ik", 64), c("b_ik", 64),
            c("c_kr", 32), c("b_iw", 8), zeros(24), c("c_kr", 32, 32), zeros(32),
            c("c_q", 384), c("c_kv", 128), c("c_z", 512),
            c("d_q", 512), c("d_k", 512), c("d_v", 512), c("d_z", 512)]


def _layout_w_in_kernel(w_ref, o_ref):
    tk = w_ref.shape[2]
    pieces = _w_in_pieces(lambda s, n: w_ref[0, s:s + n, :], lambda n: jnp.zeros((n, tk), F32))
    ends = np.cumsum([0] + [p.shape[0] for p in pieces])
    start = 0
    for i in range(1, len(pieces) + 1):
        if ends[i] % GROUP_W == 0:
            group = pieces[start:i]
            blk = group[0] if len(group) == 1 else jnp.concatenate(group, axis=0)
            o_ref[0, ends[start]:ends[i], :] = blk.astype(BF16)
            start = i


def _layout_w_in(w_in, tk=256):
    w_t = jnp.swapaxes(w_in, 1, 2)
    depth, cols, d = w_t.shape
    return pl.pallas_call(
        _layout_w_in_kernel,
        grid=(depth, d // tk),
        in_specs=[pl.BlockSpec((1, cols, tk), lambda l, i: (l, 0, i))],
        out_specs=pl.BlockSpec((1, H_COLS, tk), lambda l, i: (l, 0, i)),
        out_shape=jax.ShapeDtypeStruct((depth, H_COLS, d), BF16),
        compiler_params=_params(2),
    )(w_t)


def _rope_layout(v):
    z = jnp.zeros(v.shape[:-1] + (32,), v.dtype)
    return jnp.concatenate([v[..., :32], z, v[..., 32:], z], axis=-1)


def _layout_c(w_qb, w_kvb, q_gain, k_gain, qa_gain):
    wq = w_qb.reshape(Q_LORA, C_HEADS, C_QK)
    wq = jnp.concatenate([wq[..., :C_NOPE], _rope_layout(wq[..., C_NOPE:])], axis=-1)
    wq = wq.reshape(Q_LORA, C_HEADS * 256).astype(BF16)
    wkv = w_kvb.reshape(KV_LORA, C_HEADS, 2 * LANES)
    wkv = jnp.concatenate([wkv[..., :C_NOPE].reshape(KV_LORA, -1), wkv[..., C_NOPE:].reshape(KV_LORA, -1)],
                          axis=1).astype(BF16)
    lay = lambda g: jnp.concatenate([g[:C_NOPE], _rope_layout(g[C_NOPE:])])[None, :]
    return wq, wkv, lay(q_gain), lay(k_gain), qa_gain[None, :]


def _t5_bucket_static(rel):
    half = T5_BUCKETS // 2
    exact = half // 2
    n = abs(rel)
    if n < exact:
        val = n
    else:
        val = min(exact + (n * n // (exact * exact)).bit_length() - 1, half - 1)
    return (half if rel > 0 else 0) + val


def _t5_tables(t5_bias):
    m = np.arange(512)
    d0 = np.where(m < 256, m, m - 512)
    d1 = np.where(m <= 256, m - 256, m - 768)
    idx0 = np.array([_t5_bucket_static(int(d)) for d in d0], np.int32)
    idx1 = np.array([_t5_bucket_static(int(d)) for d in d1], np.int32)
    far = _t5_bucket_static(-512)
    return t5_bias[idx0].T, t5_bias[idx1].T, t5_bias[far]


def _band_table(rel_bias, tq):
    width = 2 * tq + D_LEFT
    m = np.arange(width)
    dist = np.where(m <= tq + D_LEFT, D_LEFT - m, D_LEFT + width - m)
    idx = np.clip(dist, -REL_CLIP, REL_CLIP) + REL_CLIP
    return rel_bias[idx.astype(np.int32)].T


def _rope_tables(seq):
    inv = ROPE_BASE ** (-jnp.arange(0, C_ROPE, 2, dtype=F32) / C_ROPE)
    ang = jnp.arange(seq, dtype=F32)[:, None] * inv[None, :]
    c, s = jnp.cos(ang), jnp.sin(ang)
    z = jnp.zeros_like(c)
    return jnp.concatenate([c, z, c, z], axis=1), jnp.concatenate([-s, z, s, z], axis=1)


def kernel(x, t5_bias, norm_g, w_in, a_v_gain, a_ws, a_bs, b_q_gain, b_k_gain, c_qa_gain, c_kva_gain,
           c_w_qb, c_w_kvb, c_q_gain, c_k_gain, d_q_gain, d_k_gain, d_rel_bias, w_out):
    bsz, seq, d_model = x.shape
    depth = w_in.shape[0]
    tq = 256
    assert seq % 512 == 0 and seq <= 2048 and d_model % 512 == 0, (seq, d_model)
    assert w_in.shape[2] == _SRC["d_z"] + GROUP_W and w_out.shape[1] == 4 * GROUP_W, (w_in.shape, w_out.shape)
    x2 = x.reshape(bsz * seq, d_model)
    cos, sin = _rope_tables(seq)
    base0, base1, cfar = _t5_tables(t5_bias)
    head_sum = jnp.asarray(np.kron(np.eye(N_HEADS64), np.ones((64, 64))), BF16)
    w_in_blocks = _layout_w_in(w_in)
    for l in range(depth):
        h = _inproj(x2, norm_g[l][None, :], w_in_blocks, l)
        mask = _b_select(h, bsz, seq)
        y_b = _b_attn(h, mask, cfar, jnp.tile(b_q_gain[l], N_HEADS64)[None, :],
                      jnp.tile(b_k_gain[l], 2)[None, :], base0, base1, head_sum, bsz, seq, tq)
        wq, wkv, qg, kg, qag = _layout_c(c_w_qb[l], c_w_kvb[l], c_q_gain[l], c_k_gain[l], c_qa_gain[l])
        qc, kc, vc = _c_prep(h, wq, wkv, qag, c_kva_gain[l][None, :], qg, kg, cos, sin, seq)
        y_c = _c_attn(qc, kc, vc, h, bsz, seq, tq)
        y_d = _mixer_d(h, jnp.tile(d_q_gain[l], N_HEADS64)[None, :], jnp.tile(d_k_gain[l], N_HEADS64)[None, :],
                       _band_table(d_rel_bias[l], tq), head_sum, bsz, seq, tq)
        x2 = _outproj(x2, h, a_v_gain[l][None, :], a_ws[l], a_bs[l][:, :, None], (y_b, y_c, y_d),
                      w_out[l].astype(BF16))
    return x2.reshape(bsz, seq, d_model)
```

```python
import functools
import math

import numpy as np
import jax
import jax.numpy as jnp
from jax import lax
from jax.experimental import pallas as pl
from jax.experimental.pallas import tpu as pltpu

F32 = jnp.float32
BF16 = jnp.bfloat16

EPS = 1e-6
NEG = -1e30
LOG2E = math.log2(math.e)
KEY_LOWEST = int(np.float32(-np.finfo(np.float32).max).view(np.int32)) ^ 0x7FFFFFFF
CHUNK = 64
LANES = 128
GROUP_W = 512
A_GROUPS = 4
GMLP_BLOCK = 128
N_HEADS64 = 8
IDX_SCALE = (8 ** -0.5) * 0.125
TOPK_MAX = 256
T5_BUCKETS = 32
C_HEADS = 4
C_NOPE = 128
C_ROPE = 64
C_QK = 192
Q_LORA = 384
KV_LORA = 128
ROPE_BASE = 10000.0
D_LEFT = 8 * CHUNK
REL_CLIP = 128
VMEM_LIMIT = 56 * 1024 * 1024

BLK_A_U, BLK_A_V, BLK_A_Z = 0, 1, 2
BLK_B_Q, BLK_B_IQ, BLK_B_Z, BLK_SMALL = 3, 4, 5, 6
BLK_C_QKV, BLK_C_Z = 7, 8
BLK_D_Q, BLK_D_K, BLK_D_V, BLK_D_Z = 9, 10, 11, 12
H_COLS = 13 * GROUP_W
UNIT_B_K, UNIT_B_V, UNIT_B_IK, UNIT_KR_IW = (BLK_SMALL * 4 + i for i in range(4))
IW_LANE = 32

_SRC = dict(a_u=0, a_v=512, a_z=1024, b_q=1536, b_k=2048, b_v=2112, b_iq=2176, b_ik=2688,
            b_iw=2752, b_z=2760, c_q=3272, c_kv=3656, c_kr=3784, c_z=3848,
            d_q=4360, d_k=4872, d_v=5384, d_z=5896)


def _params(n_axes):
    return pltpu.CompilerParams(dimension_semantics=("arbitrary",) * n_axes,
                                vmem_limit_bytes=VMEM_LIMIT)


def _gelu(x):
    c = math.sqrt(2.0 / math.pi)
    return x * (0.5 * (1.0 + jnp.tanh(c * (x + 0.044715 * (x * x * x)))))


def _silu(x):
    return x * (1.0 / (1.0 + jnp.exp(-x)))


def _dot_t(a, b):
    return lax.dot_general(a, b, (((1,), (1,)), ((), ())), preferred_element_type=F32)


def _lo_mask(rows):
    return lax.broadcasted_iota(jnp.int32, (rows, LANES), 1) < 64


def _rms_heads64(x, gain, ntiles):
    lo = _lo_mask(x.shape[0])
    tiles = []
    for t in range(ntiles):
        xt = x[:, t * LANES:(t + 1) * LANES]
        sq = xt * xt
        s_lo = jnp.sum(jnp.where(lo, sq, 0.0), axis=-1, keepdims=True)
        s_hi = jnp.sum(jnp.where(lo, 0.0, sq), axis=-1, keepdims=True)
        r = jnp.where(lo, lax.rsqrt(s_lo * (1.0 / 64) + EPS), lax.rsqrt(s_hi * (1.0 / 64) + EPS))
        tiles.append(xt * r * gain[:, t * LANES:(t + 1) * LANES])
    return tiles


def _for_blocks(blocks, count):
    def quad(i, carry):
        blocks(4 * i, 4)
        return carry

    lax.fori_loop(0, count // 4, quad, 0)
    first = (count // 4) * 4

    @pl.when((count & 2) != 0)
    def _():
        blocks(first, 2)

    @pl.when((count & 1) != 0)
    def _():
        blocks(first + (count & 2), 1)


def _toeplitz(base_row, rows, width):
    t = jnp.broadcast_to(base_row, (rows, base_row.shape[1]))
    t = pltpu.roll(t, 0, 1, stride=1, stride_axis=0)
    return t[:, :width]


def _inproj_kernel(x_ref, g_ref, w_ref, o_ref):
    x = x_ref[...]
    ms = jnp.mean(x * x, axis=-1, keepdims=True)
    xn = (x * lax.rsqrt(ms + EPS) * g_ref[...]).astype(BF16)
    for c in range(H_COLS // GROUP_W):
        cols = slice(c * GROUP_W, (c + 1) * GROUP_W)
        o_ref[:, cols] = _dot_t(xn, w_ref[cols, :]).astype(BF16)


def _inproj(x2, g, w_all, layer, tm=512):
    n, d = x2.shape
    return pl.pallas_call(
        _inproj_kernel,
        grid=(n // tm,),
        in_specs=[pl.BlockSpec((tm, d), lambda i: (i, 0)),
                  pl.BlockSpec((1, d), lambda i: (0, 0)),
                  pl.BlockSpec((None, H_COLS, d), lambda i: (layer, 0, 0), pipeline_mode=pl.Buffered(1))],
        out_specs=pl.BlockSpec((tm, H_COLS), lambda i: (i, 0)),
        out_shape=jax.ShapeDtypeStruct((n, H_COLS), BF16),
        compiler_params=_params(1),
    )(x2, g, w_all)


def _outproj_kernel(x_ref, u_ref, v_ref, z_ref, vg_ref, ws_ref, bs_ref, yb_ref, yc_ref, yd_ref, w_ref, o_ref,
                    ya_ref):
    tm, d = x_ref.shape
    wgs = _mixer_a_weights(ws_ref)
    nblk = tm // GMLP_BLOCK
    ncol = d // nblk
    for c in range(nblk):
        cols = slice(c * ncol, (c + 1) * ncol)
        acc = x_ref[:, cols]
        for g, y_ref in ((1, yb_ref), (2, yc_ref), (3, yd_ref)):
            acc = acc + jnp.dot(y_ref[...], w_ref[g * GROUP_W:(g + 1) * GROUP_W, cols],
                                preferred_element_type=F32)
        o_ref[:, cols] = acc
        _mixer_a_rows(u_ref, v_ref, z_ref, vg_ref, wgs, bs_ref, ya_ref, c)
    for c in range(nblk):
        cols = slice(c * ncol, (c + 1) * ncol)
        o_ref[:, cols] += jnp.dot(ya_ref[...], w_ref[0:GROUP_W, cols], preferred_element_type=F32)


def _outproj(x2, h, vg, ws, bs, ys, w, tm=512):
    n, d = x2.shape
    yspec = pl.BlockSpec((tm, GROUP_W), lambda i: (i, 0))
    hspec = lambda blk: pl.BlockSpec((tm, GROUP_W), lambda i, blk=blk: (i, blk))
    return pl.pallas_call(
        _outproj_kernel,
        grid=(n // tm,),
        in_specs=[pl.BlockSpec((tm, d), lambda i: (i, 0)),
                  hspec(BLK_A_U), hspec(BLK_A_V), hspec(BLK_A_Z),
                  pl.BlockSpec((1, GROUP_W), lambda i: (0, 0)),
                  pl.BlockSpec((A_GROUPS, GMLP_BLOCK, GMLP_BLOCK), lambda i: (0, 0, 0)),
                  pl.BlockSpec((A_GROUPS, GMLP_BLOCK, 1), lambda i: (0, 0, 0)),
                  yspec, yspec, yspec,
                  pl.BlockSpec((4 * GROUP_W, d), lambda i: (0, 0))],
        out_specs=pl.BlockSpec((tm, d), lambda i: (i, 0)),
        out_shape=jax.ShapeDtypeStruct((n, d), F32),
        scratch_shapes=[pltpu.VMEM((tm, GROUP_W), BF16)],
        compiler_params=_params(1),
    )(x2, h, h, h, vg, ws, bs, *ys, w)


def _mixer_a_weights(w_ref):
    i = lax.broadcasted_iota(jnp.int32, (GMLP_BLOCK, GMLP_BLOCK), 0)
    j = lax.broadcasted_iota(jnp.int32, (GMLP_BLOCK, GMLP_BLOCK), 1)
    keep = (j // CHUNK) <= (i // CHUNK)
    return [jnp.where(keep, w_ref[g], 0.0).astype(BF16) for g in range(A_GROUPS)]


def _mixer_a_rows(u_ref, v_ref, z_ref, vg_ref, wgs, b_ref, o_ref, blk):
    rows = slice(blk * GMLP_BLOCK, (blk + 1) * GMLP_BLOCK)
    u = _gelu(u_ref[rows, :].astype(F32))
    v = _gelu(v_ref[rows, :].astype(F32))
    ms = jnp.mean(v * v, axis=-1, keepdims=True)
    vb = (v * lax.rsqrt(ms + EPS) * vg_ref[...]).astype(BF16)
    gate = _silu(z_ref[rows, :].astype(F32))
    for g in range(A_GROUPS):
        cols = slice(g * LANES, (g + 1) * LANES)
        sg = jnp.dot(wgs[g], vb[:, cols], preferred_element_type=F32) + b_ref[g]
        o_ref[rows, cols] = (u[:, cols] * sg * gate[:, cols]).astype(BF16)


def _order_key(x):
    return jnp.where(x < 0, x ^ 0x7FFFFFFF, x)


def _b_select_kernel(iq_ref, iw_ref, ik_ref, o_ref, lhs_ref, sc_ref, *, seq, tq, topk):
    t_blk = pl.program_id(1)
    nkb = seq // 256
    nblk = ((t_blk + 1) * tq) // 256
    n_interp = 12
    n_unchecked = 12 + (3 * t_blk) // 2
    hrows = [slice(h * tq, (h + 1) * tq) for h in range(N_HEADS64)]

    lo_half = _lo_mask(tq)
    w_t = (iw_ref[...].astype(F32) * IDX_SCALE).T
    for h in range(N_HEADS64):
        iqt = iq_ref[:, (h // 2) * LANES:(h // 2 + 1) * LANES]
        sel = lo_half if h % 2 == 0 else jnp.logical_not(lo_half)
        lhs_ref[hrows[h], :] = jnp.where(sel, iqt, jnp.zeros_like(iqt))

    qpos = t_blk * tq + lax.broadcasted_iota(jnp.int32, (256, tq), 1)
    krow = lax.broadcasted_iota(jnp.int32, (256, tq), 0)

    def fold8(x):
        return jnp.sum(x.reshape(256 // 8, 8, tq), axis=0)

    def score_block(kb, carry, masked):
        amax, n_pos, n_nn = carry
        off = pl.multiple_of(kb * 256, 256)
        ikblk = ik_ref[pl.ds(off, 256), :]
        score = jnp.zeros((256, tq), F32)
        for h in range(N_HEADS64):
            w_h = w_t[IW_LANE + h:IW_LANE + h + 1, :]
            score = score + w_h * jnp.maximum(_dot_t(ikblk, lhs_ref[hrows[h], :]), 0.0)
        mag = jnp.abs(score)
        if masked:
            adm = ((kb * 256 + krow) // CHUNK) <= (qpos // CHUNK)
            score = jnp.where(adm, score, -jnp.inf)
            mag = jnp.where(adm, mag, 0.0)
        sc_ref[kb] = score
        return (jnp.maximum(amax, jnp.max(mag.reshape(256 // 8, 8, tq), axis=0)),
                n_pos + fold8(jnp.where(score > 0.0, 1.0, 0.0)), n_nn + fold8(jnp.where(score >= 0.0, 1.0, 0.0)))

    nfull = (t_blk * tq) // 256
    zeros8 = jnp.zeros((8, tq), F32)
    carry = lax.fori_loop(0, nfull, lambda kb, c: score_block(kb, c, False), (zeros8, zeros8, zeros8))
    carry = lax.fori_loop(nfull, nblk, lambda kb, c: score_block(kb, c, True), carry)
    amax = jnp.max(carry[0], axis=0, keepdims=True)
    f_pos = jnp.sum(carry[1], axis=0, keepdims=True)
    f_nn = jnp.sum(carry[2], axis=0, keepdims=True)

    def count(pred):
        def body(kb, acc):
            return acc + fold8(jnp.where(pred(sc_ref[kb], kb), 1.0, 0.0))
        return jnp.sum(lax.fori_loop(0, nblk, body, zeros8), axis=0, keepdims=True)

    kf = float(topk)
    qrow = t_blk * tq + lax.broadcasted_iota(jnp.int32, (1, tq), 1)
    n_adm = ((qrow // CHUNK + 1) * CHUNK).astype(F32)
    one = jnp.ones((1, tq), jnp.int32)
    pos = f_pos > kf
    neg = f_nn < kf
    lo0 = jnp.where(pos, one, _order_key(lax.bitcast_convert_type(-amax, jnp.int32)))
    hi0 = jnp.where(neg, one - 1, _order_key(lax.bitcast_convert_type(amax, jnp.int32)) + 1)
    w_lo0 = jnp.where(pos, f_pos, n_adm) - kf
    w_hi0 = kf - jnp.where(neg, f_nn, 0.0)
    all_sel = n_adm <= kf
    at_zero = jnp.logical_not(pos | neg)
    done0 = jnp.where(all_sel | at_zero | (hi0 == lo0 + 1), 1.0, 0.0)
    thr0 = jnp.where(all_sel, KEY_LOWEST, jnp.where(at_zero, jnp.where(f_pos == kf, one, one - 1), lo0))

    def as_score(key):
        return lax.bitcast_convert_type(_order_key(key), F32)

    def search_cond(st):
        return jnp.logical_and(st[0][0] < n_interp + 32, st[1] < 0.5)

    def search_step(st):
        it, lo, hi, w_lo, w_hi, side, done, thr = st
        lo_v = as_score(lo)
        hi_v = as_score(hi)
        c_v = lo_v + (hi_v - lo_v) * (w_lo / (w_lo + w_hi))
        c_interp = _order_key(lax.bitcast_convert_type(c_v, jnp.int32))
        c_mid = (lo >> 1) + (hi >> 1) + (lo & hi & 1)
        cand = jnp.where(it < n_interp, c_interp, c_mid)
        cand = jnp.minimum(jnp.maximum(cand, lo + 1), hi - 1)
        cand_v = as_score(cand)
        f = count(lambda s, kb: s >= cand_v)
        live = done < 0.5
        up = f > kf
        hit = f == kf
        new_lo = jnp.where(live & up, cand, lo)
        new_hi = jnp.where(live & jnp.logical_not(up), cand, hi)
        new_w_lo = jnp.where(up, f - kf, jnp.where(side < 0.0, 0.5 * w_lo, w_lo))
        new_w_hi = jnp.where(up, jnp.where(side > 0.0, 0.5 * w_hi, w_hi), kf - f)
        new_side = jnp.where(up, 1.0, -1.0)
        new_thr = jnp.where(live, jnp.where(hit, cand, new_lo), thr)
        new_done = jnp.where(live & (hit | (new_hi == new_lo + 1)), 1.0, done)
        return (it + 1, new_lo, new_hi, jnp.where(live, new_w_lo, w_lo),
                jnp.where(live, new_w_hi, w_hi), jnp.where(live, new_side, side), new_done, new_thr)

    def checked_step(st):
        new = search_step(st[0])
        return new, jnp.min(new[-2])

    state = (jnp.int32(0), lo0, hi0, w_lo0, w_hi0, jnp.zeros((1, tq), F32), done0, thr0)
    state = lax.fori_loop(0, n_unchecked, lambda i, st: search_step(st), state)
    thr = as_score(lax.while_loop(search_cond, checked_step, (state, jnp.min(state[-2])))[0][-1])

    any_excess = jnp.max(count(lambda s, kb: s >= thr)) > kf

    def store_mask(kb, keep_t):
        for g in range(tq // 256):
            o_ref[g, kb] = keep_t[:, g * 256:(g + 1) * 256]

    def write_unused(kb, carry):
        store_mask(kb, jnp.zeros((256, tq), F32))
        return carry

    lax.fori_loop(nblk, nkb, write_unused, 0)

    @pl.when(jnp.logical_not(any_excess))
    def _():
        def write(kb, carry):
            store_mask(kb, jnp.where(sc_ref[kb] >= thr, 1.0, 0.0))
            return carry

        lax.fori_loop(0, nblk, write, 0)

    @pl.when(any_excess)
    def _():
        need = kf - count(lambda s, kb: s > thr)

        def idx_step(it, jmax):
            cand = jmax | lax.shift_left(jnp.int32(1), 10 - it)
            below = count(lambda s, kb: (s == thr) & ((kb * 256 + krow) < cand))
            return jnp.where(below < need, cand, jmax)

        jmax = lax.fori_loop(0, 11, idx_step, jnp.zeros((1, tq), jnp.int32))

        def write(kb, carry):
            s = sc_ref[kb]
            keep_tie = (s == thr) & ((kb * 256 + krow) <= jmax)
            store_mask(kb, jnp.where((s > thr) | keep_tie, 1.0, 0.0))
            return carry

        lax.fori_loop(0, nblk, write, 0)


def _b_select(h, bsz, seq, tq=512):
    tq = min(tq, seq)
    nt = seq // tq
    topk = min(TOPK_MAX, seq // 4)
    kern = functools.partial(_b_select_kernel, seq=seq, tq=tq, topk=topk)
    return pl.pallas_call(
        kern,
        grid=(bsz, nt),
        in_specs=[pl.BlockSpec((tq, GROUP_W), lambda b, t: (b * nt + t, BLK_B_IQ)),
                  pl.BlockSpec((tq, LANES), lambda b, t: (b * nt + t, UNIT_KR_IW)),
                  pl.BlockSpec((seq, LANES), lambda b, t: (b, UNIT_B_IK))],
        out_specs=pl.BlockSpec((tq // 256, seq // 256, 256, 256), lambda b, t: (b * nt + t, 0, 0, 0)),
        out_shape=jax.ShapeDtypeStruct((bsz * seq // 256, seq // 256, 256, 256), F32),
        scratch_shapes=[pltpu.VMEM((N_HEADS64 * tq, LANES), BF16),
                        pltpu.VMEM((seq // 256, 256, tq), F32)],
        compiler_params=_params(2),
    )(h, h, h)


def _b_attn_kernel(cfar_ref, q_ref, z_ref, k_ref, v_ref, msk_ref, qg_ref, kg_ref, base0_ref, base1_ref,
                   o_ref, kn_ref, v1_ref, bias_ref, qall_ref, s_ref, mp_ref, acc_ref, *, seq, tq):
    b = pl.program_id(0)
    t_blk = pl.program_id(1)
    hrows = [slice(h * tq, (h + 1) * tq) for h in range(N_HEADS64)]

    @pl.when((b == 0) & (t_blk == 0))
    def _():
        for h in range(N_HEADS64):
            bias_ref[0, hrows[h], :] = jnp.full((tq, 256), cfar_ref[h] * LOG2E, F32)
            bias_ref[1, hrows[h], :] = _toeplitz(base1_ref[h:h + 1, :], tq, 256) * LOG2E
            bias_ref[2, hrows[h], :] = _toeplitz(base0_ref[h:h + 1, :], tq, 256) * LOG2E

    @pl.when(t_blk == 0)
    def _():
        lo256 = _lo_mask(256)
        for r in range(seq // 256):
            rows = slice(r * 256, (r + 1) * 256)
            k = k_ref[rows, :].astype(F32)
            ms = jnp.mean(k * k, axis=-1, keepdims=True)
            kn_ref[rows, :] = (k * lax.rsqrt(ms + EPS) * kg_ref[...]).astype(BF16)
            v = v_ref[rows, :]
            v1_ref[rows, :] = jnp.where(lo256, v, jnp.ones_like(v))

    lo = _lo_mask(tq)
    qtiles = _rms_heads64(q_ref[...].astype(F32), qg_ref[...], 4)
    for h in range(N_HEADS64):
        sel = lo if h % 2 == 0 else jnp.logical_not(lo)
        qall_ref[hrows[h], :] = jnp.where(sel, qtiles[h // 2] * (0.125 * LOG2E), 0.0).astype(BF16)
    mp_ref[...] = jnp.full(mp_ref.shape, NEG, F32)
    acc_ref[...] = jnp.zeros(acc_ref.shape, F32)
    nblk = t_blk + 1

    def logits_blocks(kb0, n):
        for i in range(n):
            kb = kb0 + i
            off = pl.multiple_of(kb * 256, 256)
            kblk = kn_ref[pl.ds(off, 256), :]
            which = jnp.clip(kb - (t_blk - 2), 0, 2)
            keep = msk_ref[0, kb].T > 0.5
            for h in range(N_HEADS64):
                s = jnp.where(keep, _dot_t(qall_ref[hrows[h], :], kblk) + bias_ref[which, hrows[h], :], NEG)
                s_ref[kb, hrows[h], :] = s
                mp_ref[hrows[h], :] = jnp.maximum(mp_ref[hrows[h], :], jnp.maximum(s[:, :LANES], s[:, LANES:]))

    def value_blocks(kb0, n):
        off = pl.multiple_of(kb0 * 256, 256)
        v1 = v1_ref[pl.ds(off, n * 256), :]
        for h in range(N_HEADS64):
            m = mp_ref[hrows[h], :]
            mm = jnp.concatenate([m, m], axis=1)
            p = [jnp.exp2(s_ref[kb0 + i, hrows[h], :] - mm).astype(BF16) for i in range(n)]
            p = p[0] if n == 1 else jnp.concatenate(p, axis=1)
            acc_ref[hrows[h], :] += jnp.dot(p, v1, preferred_element_type=F32)

    _for_blocks(logits_blocks, nblk)
    for h in range(N_HEADS64):
        m = jnp.max(mp_ref[hrows[h], :], axis=-1, keepdims=True)
        mp_ref[hrows[h], :] = jnp.broadcast_to(m, (tq, LANES))
    _for_blocks(value_blocks, nblk)

    gate = _silu(z_ref[...].astype(F32))
    for t in range(4):
        a_even = acc_ref[hrows[2 * t], :]
        a_odd = acc_ref[hrows[2 * t + 1], :]
        o_even = a_even / pltpu.roll(a_even, 64, 1)
        o_odd = pltpu.roll(a_odd, 64, 1) / a_odd
        cols = slice(t * LANES, (t + 1) * LANES)
        o_ref[:, cols] = (jnp.where(lo, o_even, o_odd) * gate[:, cols]).astype(BF16)


def _b_attn(h, mask, cfar, qg, kg, base0, base1, bsz, seq, tq=256):
    nt = seq // tq
    rows = N_HEADS64 * tq
    kern = functools.partial(_b_attn_kernel, seq=seq, tq=tq)
    full = lambda shape: pl.BlockSpec(shape, lambda b, t: (0,) * len(shape))
    return pl.pallas_call(
        kern,
        grid=(bsz, nt),
        in_specs=[pl.BlockSpec(memory_space=pltpu.SMEM),
                  pl.BlockSpec((tq, GROUP_W), lambda b, t: (b * nt + t, BLK_B_Q)),
                  pl.BlockSpec((tq, GROUP_W), lambda b, t: (b * nt + t, BLK_B_Z)),
                  pl.BlockSpec((seq, LANES), lambda b, t: (b, UNIT_B_K)),
                  pl.BlockSpec((seq, LANES), lambda b, t: (b, UNIT_B_V)),
                  pl.BlockSpec((1, seq // 256, tq, 256), lambda b, t: (b * nt + t, 0, 0, 0)),
                  full((1, GROUP_W)), full((1, LANES)), full((N_HEADS64, 512)), full((N_HEADS64, 512))],
        out_specs=pl.BlockSpec((tq, GROUP_W), lambda b, t: (b * nt + t, 0)),
        out_shape=jax.ShapeDtypeStruct((bsz * seq, GROUP_W), BF16),
        scratch_shapes=[pltpu.VMEM((seq, LANES), BF16), pltpu.VMEM((seq, LANES), BF16),
                        pltpu.VMEM((3, rows, 256), F32), pltpu.VMEM((rows, LANES), BF16),
                        pltpu.VMEM((seq // 256, rows, 256), F32),
                        pltpu.VMEM((rows, LANES), F32), pltpu.VMEM((rows, LANES), F32)],
        compiler_params=_params(2),
    )(cfar, h, h, h, h, mask, qg, kg, base0, base1)


def _rope(tile, cos, sin):
    return tile * cos + pltpu.roll(tile, 64, 1) * sin


def _c_prep_kernel(lat_ref, kr_ref, wq_ref, wkv_ref, qag_ref, kvag_ref, qg_ref, kg_ref, cos_ref, sin_ref,
                   qo_ref, ko_ref, vo_ref):
    cq = lat_ref[:, :Q_LORA].astype(F32)
    ms = jnp.mean(cq * cq, axis=-1, keepdims=True)
    cqn = (cq * lax.rsqrt(ms + EPS) * qag_ref[...]).astype(BF16)
    qpre = jnp.dot(cqn, wq_ref[...], preferred_element_type=F32)
    ckv = lat_ref[:, Q_LORA:].astype(F32)
    ms = jnp.mean(ckv * ckv, axis=-1, keepdims=True)
    ckvn = (ckv * lax.rsqrt(ms + EPS) * kvag_ref[...]).astype(BF16)
    kvpre = jnp.dot(ckvn, wkv_ref[...], preferred_element_type=F32)
    lane = lax.broadcasted_iota(jnp.int32, kr_ref.shape, 1)
    kr = jnp.where((lane % 64) < 32, kr_ref[...].astype(F32), 0.0)
    kr_ss = jnp.sum(kr * kr, axis=-1, keepdims=True)
    cos = cos_ref[...]
    sin = sin_ref[...]
    qg = qg_ref[...]
    kg = kg_ref[...]
    kr_rot = _rope(kr * kg[:, LANES:], cos, sin)
    for h in range(C_HEADS):
        qh = qpre[:, h * 256:(h + 1) * 256]
        r = lax.rsqrt(jnp.sum(qh * qh, axis=-1, keepdims=True) * (1.0 / C_QK) + EPS)
        qn = qh * r * qg
        qo_ref[:, h * 256:h * 256 + LANES] = qn[:, :LANES].astype(BF16)
        qo_ref[:, h * 256 + LANES:(h + 1) * 256] = _rope(qn[:, LANES:], cos, sin).astype(BF16)
        kn = kvpre[:, h * LANES:(h + 1) * LANES]
        r = lax.rsqrt((jnp.sum(kn * kn, axis=-1, keepdims=True) + kr_ss) * (1.0 / C_QK) + EPS)
        ko_ref[:, h * 256:h * 256 + LANES] = (kn * r * kg[:, :LANES]).astype(BF16)
        ko_ref[:, h * 256 + LANES:(h + 1) * 256] = (kr_rot * r).astype(BF16)
    vo_ref[...] = kvpre[:, C_HEADS * LANES:].astype(BF16)


def _c_prep(h, wq, wkv, qag, kvag, qg, kg, cos, sin, seq, tm=512):
    n = h.shape[0]
    ns = seq // tm
    full = lambda shape: pl.BlockSpec(shape, lambda i: (0,) * len(shape))
    return pl.pallas_call(
        _c_prep_kernel,
        grid=(n // tm,),
        in_specs=[pl.BlockSpec((tm, GROUP_W), lambda i: (i, BLK_C_QKV)),
                  pl.BlockSpec((tm, LANES), lambda i: (i, UNIT_KR_IW)),
                  full((Q_LORA, 4 * 256)), full((KV_LORA, 8 * LANES)),
                  full((1, Q_LORA)), full((1, LANES)), full((1, 256)), full((1, 256)),
                  pl.BlockSpec((tm, LANES), lambda i: (i % ns, 0)),
                  pl.BlockSpec((tm, LANES), lambda i: (i % ns, 0))],
        out_specs=[pl.BlockSpec((tm, 4 * 256), lambda i: (i, 0)),
                   pl.BlockSpec((tm, 4 * 256), lambda i: (i, 0)),
                   pl.BlockSpec((tm, GROUP_W), lambda i: (i, 0))],
        out_shape=[jax.ShapeDtypeStruct((n, 4 * 256), BF16), jax.ShapeDtypeStruct((n, 4 * 256), BF16),
                   jax.ShapeDtypeStruct((n, GROUP_W), BF16)],
        compiler_params=_params(1),
    )(h, h, wq, wkv, qag, kvag, qg, kg, cos, sin)


def _c_attn_kernel(q_ref, k_ref, v_ref, z_ref, o_ref, s_ref, mp_ref, lp_ref, acc_ref, *, tq):
    qt = pl.program_id(1)
    scale = C_QK ** -0.5 * LOG2E
    hrows = [slice(h * tq, (h + 1) * tq) for h in range(C_HEADS)]
    qchunk = (qt * tq + lax.broadcasted_iota(jnp.int32, (tq, 256), 0)) // CHUNK
    kcol = lax.broadcasted_iota(jnp.int32, (tq, 256), 1)
    mp_ref[...] = jnp.full(mp_ref.shape, NEG, F32)
    lp_ref[...] = jnp.zeros(lp_ref.shape, F32)
    acc_ref[...] = jnp.zeros(acc_ref.shape, F32)
    nfull = (qt * tq) // 256
    nblk = ((qt + 1) * tq) // 256

    def logits_blocks(kb0, n, masked):
        for i in range(n):
            kb = kb0 + i
            off = pl.multiple_of(kb * 256, 256)
            for h in range(C_HEADS):
                cols = slice(h * 256, (h + 1) * 256)
                s = _dot_t(q_ref[:, cols], k_ref[pl.ds(off, 256), cols]) * scale
                if masked:
                    s = jnp.where(((kb * 256 + kcol) // CHUNK) <= qchunk, s, NEG)
                s_ref[kb, hrows[h], :] = s
                mp_ref[hrows[h], :] = jnp.maximum(mp_ref[hrows[h], :], jnp.maximum(s[:, :LANES], s[:, LANES:]))

    def value_blocks(kb0, n):
        off = pl.multiple_of(kb0 * 256, 256)
        for h in range(C_HEADS):
            m = mp_ref[hrows[h], :]
            mm = jnp.concatenate([m, m], axis=1)
            p = [jnp.exp2(s_ref[kb0 + i, hrows[h], :] - mm) for i in range(n)]
            lsum = p[0][:, :LANES] + p[0][:, LANES:]
            for pi in p[1:]:
                lsum = lsum + pi[:, :LANES] + pi[:, LANES:]
            lp_ref[hrows[h], :] += lsum
            pb = p[0].astype(BF16) if n == 1 else jnp.concatenate([pi.astype(BF16) for pi in p], axis=1)
            acc_ref[hrows[h], :] += jnp.dot(pb, v_ref[pl.ds(off, n * 256), h * LANES:(h + 1) * LANES],
                                            preferred_element_type=F32)

    _for_blocks(lambda kb0, n: logits_blocks(kb0, n, False), nfull)
    lax.fori_loop(nfull, nblk, lambda kb, c: (logits_blocks(kb, 1, True), c)[1], 0)
    for h in range(C_HEADS):
        m = jnp.max(mp_ref[hrows[h], :], axis=-1, keepdims=True)
        mp_ref[hrows[h], :] = jnp.broadcast_to(m, (tq, LANES))
    _for_blocks(value_blocks, nblk)
    gate = _silu(z_ref[...].astype(F32))
    for h in range(C_HEADS):
        cols = slice(h * LANES, (h + 1) * LANES)
        l = jnp.sum(lp_ref[hrows[h], :], axis=-1, keepdims=True)
        o_ref[:, cols] = (acc_ref[hrows[h], :] / l * gate[:, cols]).astype(BF16)


def _c_attn(qc, kc, vc, h, bsz, seq, tq=256):
    nt = seq // tq
    rows = C_HEADS * tq
    kern = functools.partial(_c_attn_kernel, tq=tq)
    return pl.pallas_call(
        kern,
        grid=(bsz, nt),
        in_specs=[pl.BlockSpec((tq, C_HEADS * 256), lambda b, t: (b * nt + t, 0)),
                  pl.BlockSpec((seq, C_HEADS * 256), lambda b, t: (b, 0)),
                  pl.BlockSpec((seq, GROUP_W), lambda b, t: (b, 0)),
                  pl.BlockSpec((tq, GROUP_W), lambda b, t: (b * nt + t, BLK_C_Z))],
        out_specs=pl.BlockSpec((tq, GROUP_W), lambda b, t: (b * nt + t, 0)),
        out_shape=jax.ShapeDtypeStruct((bsz * seq, GROUP_W), BF16),
        scratch_shapes=[pltpu.VMEM((seq // 256, rows, 256), F32), pltpu.VMEM((rows, LANES), F32),
                        pltpu.VMEM((rows, LANES), F32), pltpu.VMEM((rows, LANES), F32)],
        compiler_params=_params(2),
    )(qc, kc, vc, h)


def _mixer_d_kernel(q_ref, k_ref, v_ref, z_ref, qg_ref, kg_ref, base_ref, o_ref,
                    kpad_ref, vpad_ref, bias_ref, s_ref, mp_ref, *, seq, tq):
    b = pl.program_id(0)
    qt = pl.program_id(1)
    win = tq + D_LEFT

    @pl.when((b == 0) & (qt == 0))
    def _():
        qc = lax.broadcasted_iota(jnp.int32, (tq, win), 0) // CHUNK
        kc = lax.broadcasted_iota(jnp.int32, (tq, win), 1) // CHUNK
        band = (kc >= qc) & (kc <= qc + D_LEFT // CHUNK)
        for h in range(N_HEADS64):
            bias_ref[h] = jnp.where(band, _toeplitz(base_ref[h:h + 1, :], tq, win) * LOG2E, NEG)

    @pl.when(qt == 0)
    def _():
        kpad_ref[0:D_LEFT, :] = jnp.zeros((D_LEFT, GROUP_W), BF16)
        vpad_ref[0:D_LEFT, :] = jnp.zeros((D_LEFT, GROUP_W), BF16)
        for r in range(seq // 256):
            rows = slice(r * 256, (r + 1) * 256)
            dst = slice(D_LEFT + r * 256, D_LEFT + (r + 1) * 256)
            tiles = _rms_heads64(k_ref[rows, :].astype(F32), kg_ref[...], 4)
            for t in range(4):
                kpad_ref[dst, t * LANES:(t + 1) * LANES] = tiles[t].astype(BF16)
            vpad_ref[dst, :] = v_ref[rows, :]

    lo = _lo_mask(tq)
    qtiles = _rms_heads64(q_ref[...].astype(F32), qg_ref[...], 4)
    start = pl.multiple_of(qt * tq, tq)
    gate = _silu(z_ref[...].astype(F32))
    ntile = win // LANES

    def lane_tiles(x):
        return [x[:, i * LANES:(i + 1) * LANES] for i in range(ntile)]

    in_seq = lax.broadcasted_iota(jnp.int32, (tq, win), 1) + start >= D_LEFT

    def logits(h):
        t, half = divmod(h, 2)
        kwin = kpad_ref[pl.ds(start, win), t * LANES:(t + 1) * LANES]
        sel = lo if half == 0 else jnp.logical_not(lo)
        qh = jnp.where(sel, qtiles[t] * (0.125 * LOG2E), 0.0).astype(BF16)
        s = jnp.where(in_seq, _dot_t(qh, kwin) + bias_ref[h], NEG)
        s_ref[h] = s
        m = jnp.max(functools.reduce(jnp.maximum, lane_tiles(s)), axis=-1, keepdims=True)
        mp_ref[h] = jnp.broadcast_to(m, (tq, LANES))

    def in_band(r, c):
        return c * LANES < r * CHUNK + D_LEFT + CHUNK and (c + 1) * LANES > r * CHUNK

    def values(h):
        vwin = vpad_ref[pl.ds(start, win), (h // 2) * LANES:(h // 2 + 1) * LANES]
        p_rows, l_rows = [], []
        for r in range(tq // CHUNK):
            rows = slice(r * CHUNK, (r + 1) * CHUNK)
            m = mp_ref[h, rows, :]
            tiles = [jnp.exp2(s_ref[h, rows, c * LANES:(c + 1) * LANES] - m) if in_band(r, c) else None
                     for c in range(ntile)]
            l_rows.append(functools.reduce(jnp.add, [t for t in tiles if t is not None]))
            p_rows.append(jnp.concatenate([jnp.zeros((CHUNK, LANES), BF16) if t is None else t.astype(BF16)
                                           for t in tiles], axis=1))
        l = jnp.sum(jnp.concatenate(l_rows, axis=0), axis=-1, keepdims=True)
        return jnp.dot(jnp.concatenate(p_rows, axis=0), vwin, preferred_element_type=F32) / l

    outs = []
    logits(0)
    for h in range(1, N_HEADS64 + 1):
        if h < N_HEADS64:
            logits(h)
        outs.append(values(h - 1))
        if h % 2 == 0:
            cols = slice((h // 2 - 1) * LANES, (h // 2) * LANES)
            o_ref[:, cols] = (jnp.where(lo, outs[h - 2], outs[h - 1]) * gate[:, cols]).astype(BF16)


def _mixer_d(h, qg, kg, base, bsz, seq, tq=256):
    nt = seq // tq
    kern = functools.partial(_mixer_d_kernel, seq=seq, tq=tq)
    full = lambda shape: pl.BlockSpec(shape, lambda b, t: (0,) * len(shape))
    return pl.pallas_call(
        kern,
        grid=(bsz, nt),
        in_specs=[pl.BlockSpec((tq, GROUP_W), lambda b, t: (b * nt + t, BLK_D_Q)),
                  pl.BlockSpec((seq, GROUP_W), lambda b, t: (b, BLK_D_K)),
                  pl.BlockSpec((seq, GROUP_W), lambda b, t: (b, BLK_D_V)),
                  pl.BlockSpec((tq, GROUP_W), lambda b, t: (b * nt + t, BLK_D_Z)),
                  full((1, GROUP_W)), full((1, GROUP_W)), full((N_HEADS64, 2 * tq + D_LEFT))],
        out_specs=pl.BlockSpec((tq, GROUP_W), lambda b, t: (b * nt + t, 0)),
        out_shape=jax.ShapeDtypeStruct((bsz * seq, GROUP_W), BF16),
        scratch_shapes=[pltpu.VMEM((seq + D_LEFT, GROUP_W), BF16), pltpu.VMEM((seq + D_LEFT, GROUP_W), BF16),
                        pltpu.VMEM((N_HEADS64, tq, tq + D_LEFT), F32),
                        pltpu.VMEM((N_HEADS64, tq, tq + D_LEFT), F32), pltpu.VMEM((N_HEADS64, tq, LANES), F32)],
        compiler_params=_params(2),
    )(h, h, h, h, qg, kg, base)


def _w_in_pieces(take, zeros):
    c = lambda name, size, off=0: take(_SRC[name] + off, size)
    return [c("a_u", 512), c("a_v", 512), c("a_z", 512),
            c("b_q", 512), c("b_iq", 512), c("b_z", 512),
            c("b_k", 64), c("b_k", 64), c("b_v", 64), c("b_v", 64), c("b_ik", 64), c("b_ik", 64),
            c("c_kr", 32), c("b_iw", 8), zeros(24), c("c_kr", 32, 32), zeros(32),
            c("c_q", 384), c("c_kv", 128), c("c_z", 512),
            c("d_q", 512), c("d_k", 512), c("d_v", 512), c("d_z", 512)]


def _layout_w_in_kernel(w_ref, o_ref):
    tk = w_ref.shape[2]
    pieces = _w_in_pieces(lambda s, n: w_ref[0, s:s + n, :], lambda n: jnp.zeros((n, tk), F32))
    ends = np.cumsum([0] + [p.shape[0] for p in pieces])
    start = 0
    for i in range(1, len(pieces) + 1):
        if ends[i] % GROUP_W == 0:
            group = pieces[start:i]
            blk = group[0] if len(group) == 1 else jnp.concatenate(group, axis=0)
            o_ref[0, ends[start]:ends[i], :] = blk.astype(BF16)
            start = i


def _layout_w_in(w_in, tk=256):
    w_t = jnp.swapaxes(w_in, 1, 2)
    depth, cols, d = w_t.shape
    return pl.pallas_call(
        _layout_w_in_kernel,
        grid=(depth, d // tk),
        in_specs=[pl.BlockSpec((1, cols, tk), lambda l, i: (l, 0, i))],
        out_specs=pl.BlockSpec((1, H_COLS, tk), lambda l, i: (l, 0, i)),
        out_shape=jax.ShapeDtypeStruct((depth, H_COLS, d), BF16),
        compiler_params=_params(2),
    )(w_t)


def _rope_layout(v):
    z = jnp.zeros(v.shape[:-1] + (32,), v.dtype)
    return jnp.concatenate([v[..., :32], z, v[..., 32:], z], axis=-1)


def _layout_c(w_qb, w_kvb, q_gain, k_gain, qa_gain):
    wq = w_qb.reshape(Q_LORA, C_HEADS, C_QK)
    wq = jnp.concatenate([wq[..., :C_NOPE], _rope_layout(wq[..., C_NOPE:])], axis=-1)
    wq = wq.reshape(Q_LORA, C_HEADS * 256).astype(BF16)
    wkv = w_kvb.reshape(KV_LORA, C_HEADS, 2 * LANES)
    wkv = jnp.concatenate([wkv[..., :C_NOPE].reshape(KV_LORA, -1), wkv[..., C_NOPE:].reshape(KV_LORA, -1)],
                          axis=1).astype(BF16)
    lay = lambda g: jnp.concatenate([g[:C_NOPE], _rope_layout(g[C_NOPE:])])[None, :]
    return wq, wkv, lay(q_gain), lay(k_gain), qa_gain[None, :]


def _t5_bucket_static(rel):
    half = T5_BUCKETS // 2
    exact = half // 2
    n = abs(rel)
    if n < exact:
        val = n
    else:
        val = min(exact + (n * n // (exact * exact)).bit_length() - 1, half - 1)
    return (half if rel > 0 else 0) + val


def _t5_tables(t5_bias):
    m = np.arange(512)
    d0 = np.where(m < 256, m, m - 512)
    d1 = np.where(m <= 256, m - 256, m - 768)
    idx0 = np.array([_t5_bucket_static(int(d)) for d in d0], np.int32)
    idx1 = np.array([_t5_bucket_static(int(d)) for d in d1], np.int32)
    far = _t5_bucket_static(-512)
    return t5_bias[idx0].T, t5_bias[idx1].T, t5_bias[far]


def _band_table(rel_bias, tq):
    width = 2 * tq + D_LEFT
    m = np.arange(width)
    dist = np.where(m <= tq + D_LEFT, D_LEFT - m, D_LEFT + width - m)
    idx = np.clip(dist, -REL_CLIP, REL_CLIP) + REL_CLIP
    return rel_bias[idx.astype(np.int32)].T


def _rope_tables(seq):
    inv = ROPE_BASE ** (-jnp.arange(0, C_ROPE, 2, dtype=F32) / C_ROPE)
    ang = jnp.arange(seq, dtype=F32)[:, None] * inv[None, :]
    c, s = jnp.cos(ang), jnp.sin(ang)
    z = jnp.zeros_like(c)
    return jnp.concatenate([c, z, c, z], axis=1), jnp.concatenate([-s, z, s, z], axis=1)


def kernel(x, t5_bias, norm_g, w_in, a_v_gain, a_ws, a_bs, b_q_gain, b_k_gain, c_qa_gain, c_kva_gain,
           c_w_qb, c_w_kvb, c_q_gain, c_k_gain, d_q_gain, d_k_gain, d_rel_bias, w_out):
    bsz, seq, d_model = x.shape
    depth = w_in.shape[0]
    tq = 256
    assert seq % 512 == 0 and seq <= 2048 and d_model % 512 == 0, (seq, d_model)
    assert w_in.shape[2] == _SRC["d_z"] + GROUP_W and w_out.shape[1] == 4 * GROUP_W, (w_in.shape, w_out.shape)
    x2 = x.reshape(bsz * seq, d_model)
    cos, sin = _rope_tables(seq)
    base0, base1, cfar = _t5_tables(t5_bias)
    w_in_blocks = _layout_w_in(w_in)
    for l in range(depth):
        h = _inproj(x2, norm_g[l][None, :], w_in_blocks, l)
        mask = _b_select(h, bsz, seq)
        y_b = _b_attn(h, mask, cfar, jnp.tile(b_q_gain[l], N_HEADS64)[None, :],
                      jnp.tile(b_k_gain[l], 2)[None, :], base0, base1, bsz, seq, tq)
        wq, wkv, qg, kg, qag = _layout_c(c_w_qb[l], c_w_kvb[l], c_q_gain[l], c_k_gain[l], c_qa_gain[l])
        qc, kc, vc = _c_prep(h, wq, wkv, qag, c_kva_gain[l][None, :], qg, kg, cos, sin, seq)
        y_c = _c_attn(qc, kc, vc, h, bsz, seq, tq)
        y_d = _mixer_d(h, jnp.tile(d_q_gain[l], N_HEADS64)[None, :], jnp.tile(d_k_gain[l], N_HEADS64)[None, :],
                       _band_table(d_rel_bias[l], tq), bsz, seq, tq)
        x2 = _outproj(x2, h, a_v_gain[l][None, :], a_ws[l], a_bs[l][:, :, None], (y_b, y_c, y_d),
                      w_out[l].astype(BF16))
    return x2.reshape(bsz, seq, d_model)
```

```python
import functools
import math

import numpy as np
import jax
import jax.numpy as jnp
from jax import lax
from jax.experimental import pallas as pl
from jax.experimental.pallas import tpu as pltpu

F32 = jnp.float32
BF16 = jnp.bfloat16

EPS = 1e-6
NEG = -1e30
LOG2E = math.log2(math.e)
KEY_LOWEST = int(np.float32(-np.finfo(np.float32).max).view(np.int32)) ^ 0x7FFFFFFF
CHUNK = 64
LANES = 128
GROUP_W = 512
A_GROUPS = 4
GMLP_BLOCK = 128
N_HEADS64 = 8
IDX_SCALE = (8 ** -0.5) * 0.125
TOPK_MAX = 256
T5_BUCKETS = 32
C_HEADS = 4
C_NOPE = 128
C_ROPE = 64
C_QK = 192
Q_LORA = 384
KV_LORA = 128
ROPE_BASE = 10000.0
D_LEFT = 8 * CHUNK
REL_CLIP = 128
VMEM_LIMIT = 56 * 1024 * 1024

BLK_A_U, BLK_A_V, BLK_A_Z = 0, 1, 2
BLK_B_Q, BLK_B_IQ, BLK_B_Z, BLK_SMALL = 3, 4, 5, 6
BLK_C_QKV, BLK_C_Z = 7, 8
BLK_D_Q, BLK_D_K, BLK_D_V, BLK_D_Z = 9, 10, 11, 12
H_COLS = 13 * GROUP_W
UNIT_B_K, UNIT_B_V, UNIT_B_IK, UNIT_KR_IW = (BLK_SMALL * 4 + i for i in range(4))
IW_LANE = 32

_SRC = dict(a_u=0, a_v=512, a_z=1024, b_q=1536, b_k=2048, b_v=2112, b_iq=2176, b_ik=2688,
            b_iw=2752, b_z=2760, c_q=3272, c_kv=3656, c_kr=3784, c_z=3848,
            d_q=4360, d_k=4872, d_v=5384, d_z=5896)


def _params(n_axes):
    return pltpu.CompilerParams(dimension_semantics=("arbitrary",) * n_axes,
                                vmem_limit_bytes=VMEM_LIMIT)


def _gelu(x):
    c = math.sqrt(2.0 / math.pi)
    return x * (0.5 * (1.0 + jnp.tanh(c * (x + 0.044715 * (x * x * x)))))


def _silu(x):
    return x * (1.0 / (1.0 + jnp.exp(-x)))


def _dot_t(a, b):
    return lax.dot_general(a, b, (((1,), (1,)), ((), ())), preferred_element_type=F32)


def _lo_mask(rows):
    return lax.broadcasted_iota(jnp.int32, (rows, LANES), 1) < 64


def _rms_heads64(x, gain, ntiles):
    lo = _lo_mask(x.shape[0])
    tiles = []
    for t in range(ntiles):
        xt = x[:, t * LANES:(t + 1) * LANES]
        sq = xt * xt
        s_lo = jnp.sum(jnp.where(lo, sq, 0.0), axis=-1, keepdims=True)
        s_hi = jnp.sum(jnp.where(lo, 0.0, sq), axis=-1, keepdims=True)
        r = jnp.where(lo, lax.rsqrt(s_lo * (1.0 / 64) + EPS), lax.rsqrt(s_hi * (1.0 / 64) + EPS))
        tiles.append(xt * r * gain[:, t * LANES:(t + 1) * LANES])
    return tiles


def _for_blocks(blocks, count):
    def quad(i, carry):
        blocks(4 * i, 4)
        return carry

    lax.fori_loop(0, count // 4, quad, 0)
    first = (count // 4) * 4

    @pl.when((count & 2) != 0)
    def _():
        blocks(first, 2)

    @pl.when((count & 1) != 0)
    def _():
        blocks(first + (count & 2), 1)


def _toeplitz(base_row, rows, width):
    t = jnp.broadcast_to(base_row, (rows, base_row.shape[1]))
    t = pltpu.roll(t, 0, 1, stride=1, stride_axis=0)
    return t[:, :width]


def _inproj_kernel(x_ref, g_ref, w_ref, o_ref):
    x = x_ref[...]
    ms = jnp.mean(x * x, axis=-1, keepdims=True)
    xn = (x * lax.rsqrt(ms + EPS) * g_ref[...]).astype(BF16)
    for c in range(H_COLS // GROUP_W):
        cols = slice(c * GROUP_W, (c + 1) * GROUP_W)
        o_ref[:, cols] = _dot_t(xn, w_ref[cols, :]).astype(BF16)


def _inproj(x2, g, w_all, layer, tm=512):
    n, d = x2.shape
    return pl.pallas_call(
        _inproj_kernel,
        grid=(n // tm,),
        in_specs=[pl.BlockSpec((tm, d), lambda i: (i, 0)),
                  pl.BlockSpec((1, d), lambda i: (0, 0)),
                  pl.BlockSpec((None, H_COLS, d), lambda i: (layer, 0, 0), pipeline_mode=pl.Buffered(1))],
        out_specs=pl.BlockSpec((tm, H_COLS), lambda i: (i, 0)),
        out_shape=jax.ShapeDtypeStruct((n, H_COLS), BF16),
        compiler_params=_params(1),
    )(x2, g, w_all)


def _outproj_kernel(x_ref, uvz_ref, vg_ref, ws_ref, bs_ref, yb_ref, yc_ref, yd_ref, w_ref, o_ref, ya_ref):
    u_ref, v_ref, z_ref = (uvz_ref.at[:, blk * GROUP_W:(blk + 1) * GROUP_W] for blk in (BLK_A_U, BLK_A_V, BLK_A_Z))
    tm, d = x_ref.shape
    wgs = _mixer_a_weights(ws_ref)
    nblk = tm // GMLP_BLOCK
    ncol = d // nblk
    for c in range(nblk):
        cols = slice(c * ncol, (c + 1) * ncol)
        acc = x_ref[:, cols]
        for g, y_ref in ((1, yb_ref), (2, yc_ref), (3, yd_ref)):
            acc = acc + jnp.dot(y_ref[...], w_ref[g * GROUP_W:(g + 1) * GROUP_W, cols],
                                preferred_element_type=F32)
        o_ref[:, cols] = acc
        _mixer_a_rows(u_ref, v_ref, z_ref, vg_ref, wgs, bs_ref, ya_ref, c)
    for c in range(nblk):
        cols = slice(c * ncol, (c + 1) * ncol)
        o_ref[:, cols] += jnp.dot(ya_ref[...], w_ref[0:GROUP_W, cols], preferred_element_type=F32)


def _outproj(x2, h, vg, ws, bs, ys, w, tm=512):
    n, d = x2.shape
    yspec = pl.BlockSpec((tm, GROUP_W), lambda i: (i, 0))
    return pl.pallas_call(
        _outproj_kernel,
        grid=(n // tm,),
        in_specs=[pl.BlockSpec((tm, d), lambda i: (i, 0)),
                  pl.BlockSpec((tm, 3 * GROUP_W), lambda i: (i, 0)),
                  pl.BlockSpec((1, GROUP_W), lambda i: (0, 0)),
                  pl.BlockSpec((A_GROUPS, GMLP_BLOCK, GMLP_BLOCK), lambda i: (0, 0, 0)),
                  pl.BlockSpec((A_GROUPS, GMLP_BLOCK, 1), lambda i: (0, 0, 0)),
                  yspec, yspec, yspec,
                  pl.BlockSpec((4 * GROUP_W, d), lambda i: (0, 0))],
        out_specs=pl.BlockSpec((tm, d), lambda i: (i, 0)),
        out_shape=jax.ShapeDtypeStruct((n, d), F32),
        scratch_shapes=[pltpu.VMEM((tm, GROUP_W), BF16)],
        compiler_params=_params(1),
    )(x2, h, vg, ws, bs, *ys, w)


def _mixer_a_weights(w_ref):
    i = lax.broadcasted_iota(jnp.int32, (GMLP_BLOCK, GMLP_BLOCK), 0)
    j = lax.broadcasted_iota(jnp.int32, (GMLP_BLOCK, GMLP_BLOCK), 1)
    keep = (j // CHUNK) <= (i // CHUNK)
    return [jnp.where(keep, w_ref[g], 0.0).astype(BF16) for g in range(A_GROUPS)]


def _mixer_a_rows(u_ref, v_ref, z_ref, vg_ref, wgs, b_ref, o_ref, blk):
    rows = slice(blk * GMLP_BLOCK, (blk + 1) * GMLP_BLOCK)
    u = _gelu(u_ref[rows, :].astype(F32))
    v = _gelu(v_ref[rows, :].astype(F32))
    ms = jnp.mean(v * v, axis=-1, keepdims=True)
    vb = (v * lax.rsqrt(ms + EPS) * vg_ref[...]).astype(BF16)
    gate = _silu(z_ref[rows, :].astype(F32))
    for g in range(A_GROUPS):
        cols = slice(g * LANES, (g + 1) * LANES)
        sg = jnp.dot(wgs[g], vb[:, cols], preferred_element_type=F32) + b_ref[g]
        o_ref[rows, cols] = (u[:, cols] * sg * gate[:, cols]).astype(BF16)


def _order_key(x):
    return jnp.where(x < 0, x ^ 0x7FFFFFFF, x)


def _b_select_kernel(iq_ref, iw_ref, ik_ref, o_ref, lhs_ref, sc_ref, *, seq, tq, topk):
    t_blk = pl.program_id(1)
    nkb = seq // 256
    nblk = ((t_blk + 1) * tq) // 256
    n_interp = 12
    n_unchecked = 12 + (3 * t_blk) // 2
    hrows = [slice(h * tq, (h + 1) * tq) for h in range(N_HEADS64)]

    lo_half = _lo_mask(tq)
    w_t = (iw_ref[...].astype(F32) * IDX_SCALE).T
    for h in range(N_HEADS64):
        iqt = iq_ref[:, (h // 2) * LANES:(h // 2 + 1) * LANES]
        sel = lo_half if h % 2 == 0 else jnp.logical_not(lo_half)
        lhs_ref[hrows[h], :] = jnp.where(sel, iqt, jnp.zeros_like(iqt))

    qpos = t_blk * tq + lax.broadcasted_iota(jnp.int32, (256, tq), 1)
    krow = lax.broadcasted_iota(jnp.int32, (256, tq), 0)

    def fold8(x):
        return jnp.sum(x.reshape(256 // 8, 8, tq), axis=0)

    def score_block(kb, carry, masked):
        amax, n_pos, n_nn = carry
        off = pl.multiple_of(kb * 256, 256)
        ikblk = ik_ref[pl.ds(off, 256), :]
        score = jnp.zeros((256, tq), F32)
        for h in range(N_HEADS64):
            w_h = w_t[IW_LANE + h:IW_LANE + h + 1, :]
            score = score + w_h * jnp.maximum(_dot_t(ikblk, lhs_ref[hrows[h], :]), 0.0)
        mag = jnp.abs(score)
        if masked:
            adm = ((kb * 256 + krow) // CHUNK) <= (qpos // CHUNK)
            score = jnp.where(adm, score, -jnp.inf)
            mag = jnp.where(adm, mag, 0.0)
        sc_ref[kb] = score
        return (jnp.maximum(amax, jnp.max(mag.reshape(256 // 8, 8, tq), axis=0)),
                n_pos + fold8(jnp.where(score > 0.0, 1.0, 0.0)), n_nn + fold8(jnp.where(score >= 0.0, 1.0, 0.0)))

    nfull = (t_blk * tq) // 256
    zeros8 = jnp.zeros((8, tq), F32)
    carry = lax.fori_loop(0, nfull, lambda kb, c: score_block(kb, c, False), (zeros8, zeros8, zeros8))
    carry = lax.fori_loop(nfull, nblk, lambda kb, c: score_block(kb, c, True), carry)
    amax = jnp.max(carry[0], axis=0, keepdims=True)
    f_pos = jnp.sum(carry[1], axis=0, keepdims=True)
    f_nn = jnp.sum(carry[2], axis=0, keepdims=True)

    def count(pred):
        def body(kb, acc):
            return acc + fold8(jnp.where(pred(sc_ref[kb], kb), 1.0, 0.0))
        return jnp.sum(lax.fori_loop(0, nblk, body, zeros8), axis=0, keepdims=True)

    kf = float(topk)
    qrow = t_blk * tq + lax.broadcasted_iota(jnp.int32, (1, tq), 1)
    n_adm = ((qrow // CHUNK + 1) * CHUNK).astype(F32)
    one = jnp.ones((1, tq), jnp.int32)
    pos = f_pos > kf
    neg = f_nn < kf
    lo0 = jnp.where(pos, one, _order_key(lax.bitcast_convert_type(-amax, jnp.int32)))
    hi0 = jnp.where(neg, one - 1, _order_key(lax.bitcast_convert_type(amax, jnp.int32)) + 1)
    w_lo0 = jnp.where(pos, f_pos, n_adm) - kf
    w_hi0 = kf - jnp.where(neg, f_nn, 0.0)
    all_sel = n_adm <= kf
    at_zero = jnp.logical_not(pos | neg)
    done0 = jnp.where(all_sel | at_zero | (hi0 == lo0 + 1), 1.0, 0.0)
    thr0 = jnp.where(all_sel, KEY_LOWEST, jnp.where(at_zero, jnp.where(f_pos == kf, one, one - 1), lo0))

    def as_score(key):
        return lax.bitcast_convert_type(_order_key(key), F32)

    def search_cond(st):
        return jnp.logical_and(st[0][0] < n_interp + 32, st[1] < 0.5)

    def search_step(st):
        it, lo, hi, w_lo, w_hi, side, done, thr = st
        lo_v = as_score(lo)
        hi_v = as_score(hi)
        c_v = lo_v + (hi_v - lo_v) * (w_lo / (w_lo + w_hi))
        c_interp = _order_key(lax.bitcast_convert_type(c_v, jnp.int32))
        c_mid = (lo >> 1) + (hi >> 1) + (lo & hi & 1)
        cand = jnp.where(it < n_interp, c_interp, c_mid)
        cand = jnp.minimum(jnp.maximum(cand, lo + 1), hi - 1)
        cand_v = as_score(cand)
        f = count(lambda s, kb: s >= cand_v)
        live = done < 0.5
        up = f > kf
        hit = f == kf
        new_lo = jnp.where(live & up, cand, lo)
        new_hi = jnp.where(live & jnp.logical_not(up), cand, hi)
        new_w_lo = jnp.where(up, f - kf, jnp.where(side < 0.0, 0.5 * w_lo, w_lo))
        new_w_hi = jnp.where(up, jnp.where(side > 0.0, 0.5 * w_hi, w_hi), kf - f)
        new_side = jnp.where(up, 1.0, -1.0)
        new_thr = jnp.where(live, jnp.where(hit, cand, new_lo), thr)
        new_done = jnp.where(live & (hit | (new_hi == new_lo + 1)), 1.0, done)
        return (it + 1, new_lo, new_hi, jnp.where(live, new_w_lo, w_lo),
                jnp.where(live, new_w_hi, w_hi), jnp.where(live, new_side, side), new_done, new_thr)

    def checked_step(st):
        new = search_step(st[0])
        return new, jnp.min(new[-2])

    state = (jnp.int32(0), lo0, hi0, w_lo0, w_hi0, jnp.zeros((1, tq), F32), done0, thr0)
    state = lax.fori_loop(0, n_unchecked, lambda i, st: search_step(st), state)
    thr = as_score(lax.while_loop(search_cond, checked_step, (state, jnp.min(state[-2])))[0][-1])

    any_excess = jnp.max(count(lambda s, kb: s >= thr)) > kf

    def store_mask(kb, keep_t):
        for g in range(tq // 256):
            o_ref[g, kb] = keep_t[:, g * 256:(g + 1) * 256]

    def write_unused(kb, carry):
        store_mask(kb, jnp.zeros((256, tq), F32))
        return carry

    lax.fori_loop(nblk, nkb, write_unused, 0)

    @pl.when(jnp.logical_not(any_excess))
    def _():
        def write(kb, carry):
            store_mask(kb, jnp.where(sc_ref[kb] >= thr, 1.0, 0.0))
            return carry

        lax.fori_loop(0, nblk, write, 0)

    @pl.when(any_excess)
    def _():
        need = kf - count(lambda s, kb: s > thr)

        def idx_step(it, jmax):
            cand = jmax | lax.shift_left(jnp.int32(1), 10 - it)
            below = count(lambda s, kb: (s == thr) & ((kb * 256 + krow) < cand))
            return jnp.where(below < need, cand, jmax)

        jmax = lax.fori_loop(0, 11, idx_step, jnp.zeros((1, tq), jnp.int32))

        def write(kb, carry):
            s = sc_ref[kb]
            keep_tie = (s == thr) & ((kb * 256 + krow) <= jmax)
            store_mask(kb, jnp.where((s > thr) | keep_tie, 1.0, 0.0))
            return carry

        lax.fori_loop(0, nblk, write, 0)


def _b_select(h, bsz, seq, tq=512):
    tq = min(tq, seq)
    nt = seq // tq
    topk = min(TOPK_MAX, seq // 4)
    kern = functools.partial(_b_select_kernel, seq=seq, tq=tq, topk=topk)
    return pl.pallas_call(
        kern,
        grid=(bsz, nt),
        in_specs=[pl.BlockSpec((tq, GROUP_W), lambda b, t: (b * nt + t, BLK_B_IQ)),
                  pl.BlockSpec((tq, LANES), lambda b, t: (b * nt + t, UNIT_KR_IW)),
                  pl.BlockSpec((seq, LANES), lambda b, t: (b, UNIT_B_IK))],
        out_specs=pl.BlockSpec((tq // 256, seq // 256, 256, 256), lambda b, t: (b * nt + t, 0, 0, 0)),
        out_shape=jax.ShapeDtypeStruct((bsz * seq // 256, seq // 256, 256, 256), F32),
        scratch_shapes=[pltpu.VMEM((N_HEADS64 * tq, LANES), BF16),
                        pltpu.VMEM((seq // 256, 256, tq), F32)],
        compiler_params=_params(2),
    )(h, h, h)


def _b_attn_kernel(cfar_ref, q_ref, z_ref, k_ref, v_ref, msk_ref, qg_ref, kg_ref, base0_ref, base1_ref,
                   o_ref, kn_ref, v1_ref, bias_ref, qall_ref, s_ref, mp_ref, acc_ref, *, seq, tq):
    b = pl.program_id(0)
    t_blk = pl.program_id(1)
    hrows = [slice(h * tq, (h + 1) * tq) for h in range(N_HEADS64)]

    @pl.when((b == 0) & (t_blk == 0))
    def _():
        for h in range(N_HEADS64):
            bias_ref[0, hrows[h], :] = jnp.full((tq, 256), cfar_ref[h] * LOG2E, F32)
            bias_ref[1, hrows[h], :] = _toeplitz(base1_ref[h:h + 1, :], tq, 256) * LOG2E
            bias_ref[2, hrows[h], :] = _toeplitz(base0_ref[h:h + 1, :], tq, 256) * LOG2E

    @pl.when(t_blk == 0)
    def _():
        lo256 = _lo_mask(256)
        for r in range(seq // 256):
            rows = slice(r * 256, (r + 1) * 256)
            k = k_ref[rows, :].astype(F32)
            ms = jnp.mean(k * k, axis=-1, keepdims=True)
            kn_ref[rows, :] = (k * lax.rsqrt(ms + EPS) * kg_ref[...]).astype(BF16)
            v = v_ref[rows, :]
            v1_ref[rows, :] = jnp.where(lo256, v, jnp.ones_like(v))

    lo = _lo_mask(tq)
    qtiles = _rms_heads64(q_ref[...].astype(F32), qg_ref[...], 4)
    for h in range(N_HEADS64):
        sel = lo if h % 2 == 0 else jnp.logical_not(lo)
        qall_ref[hrows[h], :] = jnp.where(sel, qtiles[h // 2] * (0.125 * LOG2E), 0.0).astype(BF16)
    mp_ref[...] = jnp.full(mp_ref.shape, NEG, F32)
    acc_ref[...] = jnp.zeros(acc_ref.shape, F32)
    nblk = t_blk + 1

    def logits_blocks(kb0, n):
        for i in range(n):
            kb = kb0 + i
            off = pl.multiple_of(kb * 256, 256)
            kblk = kn_ref[pl.ds(off, 256), :]
            which = jnp.clip(kb - (t_blk - 2), 0, 2)
            keep = msk_ref[0, kb].T > 0.5
            for h in range(N_HEADS64):
                s = jnp.where(keep, _dot_t(qall_ref[hrows[h], :], kblk) + bias_ref[which, hrows[h], :], NEG)
                s_ref[kb, hrows[h], :] = s
                mp_ref[hrows[h], :] = jnp.maximum(mp_ref[hrows[h], :], jnp.maximum(s[:, :LANES], s[:, LANES:]))

    def value_blocks(kb0, n):
        off = pl.multiple_of(kb0 * 256, 256)
        v1 = v1_ref[pl.ds(off, n * 256), :]
        for h in range(N_HEADS64):
            m = mp_ref[hrows[h], :]
            mm = jnp.concatenate([m, m], axis=1)
            p = [jnp.exp2(s_ref[kb0 + i, hrows[h], :] - mm).astype(BF16) for i in range(n)]
            p = p[0] if n == 1 else jnp.concatenate(p, axis=1)
            acc_ref[hrows[h], :] += jnp.dot(p, v1, preferred_element_type=F32)

    _for_blocks(logits_blocks, nblk)
    for h in range(N_HEADS64):
        m = jnp.max(mp_ref[hrows[h], :], axis=-1, keepdims=True)
        mp_ref[hrows[h], :] = jnp.broadcast_to(m, (tq, LANES))
    _for_blocks(value_blocks, nblk)

    gate = _silu(z_ref[...].astype(F32))
    for t in range(4):
        a_even = acc_ref[hrows[2 * t], :]
        a_odd = acc_ref[hrows[2 * t + 1], :]
        o_even = a_even / pltpu.roll(a_even, 64, 1)
        o_odd = pltpu.roll(a_odd, 64, 1) / a_odd
        cols = slice(t * LANES, (t + 1) * LANES)
        o_ref[:, cols] = (jnp.where(lo, o_even, o_odd) * gate[:, cols]).astype(BF16)


def _b_attn(h, mask, cfar, qg, kg, base0, base1, bsz, seq, tq=256):
    nt = seq // tq
    rows = N_HEADS64 * tq
    kern = functools.partial(_b_attn_kernel, seq=seq, tq=tq)
    full = lambda shape: pl.BlockSpec(shape, lambda b, t: (0,) * len(shape))
    return pl.pallas_call(
        kern,
        grid=(bsz, nt),
        in_specs=[pl.BlockSpec(memory_space=pltpu.SMEM),
                  pl.BlockSpec((tq, GROUP_W), lambda b, t: (b * nt + t, BLK_B_Q)),
                  pl.BlockSpec((tq, GROUP_W), lambda b, t: (b * nt + t, BLK_B_Z)),
                  pl.BlockSpec((seq, LANES), lambda b, t: (b, UNIT_B_K)),
                  pl.BlockSpec((seq, LANES), lambda b, t: (b, UNIT_B_V)),
                  pl.BlockSpec((1, seq // 256, tq, 256), lambda b, t: (b * nt + t, 0, 0, 0)),
                  full((1, GROUP_W)), full((1, LANES)), full((N_HEADS64, 512)), full((N_HEADS64, 512))],
        out_specs=pl.BlockSpec((tq, GROUP_W), lambda b, t: (b * nt + t, 0)),
        out_shape=jax.ShapeDtypeStruct((bsz * seq, GROUP_W), BF16),
        scratch_shapes=[pltpu.VMEM((seq, LANES), BF16), pltpu.VMEM((seq, LANES), BF16),
                        pltpu.VMEM((3, rows, 256), F32), pltpu.VMEM((rows, LANES), BF16),
                        pltpu.VMEM((seq // 256, rows, 256), F32),
                        pltpu.VMEM((rows, LANES), F32), pltpu.VMEM((rows, LANES), F32)],
        compiler_params=_params(2),
    )(cfar, h, h, h, h, mask, qg, kg, base0, base1)


def _rope(tile, cos, sin):
    return tile * cos + pltpu.roll(tile, 64, 1) * sin


def _c_prep_kernel(lat_ref, kr_ref, wq_ref, wkv_ref, qag_ref, kvag_ref, qg_ref, kg_ref, cos_ref, sin_ref,
                   qo_ref, ko_ref, vo_ref):
    cq = lat_ref[:, :Q_LORA].astype(F32)
    ms = jnp.mean(cq * cq, axis=-1, keepdims=True)
    cqn = (cq * lax.rsqrt(ms + EPS) * qag_ref[...]).astype(BF16)
    qpre = jnp.dot(cqn, wq_ref[...], preferred_element_type=F32)
    ckv = lat_ref[:, Q_LORA:].astype(F32)
    ms = jnp.mean(ckv * ckv, axis=-1, keepdims=True)
    ckvn = (ckv * lax.rsqrt(ms + EPS) * kvag_ref[...]).astype(BF16)
    kvpre = jnp.dot(ckvn, wkv_ref[...], preferred_element_type=F32)
    lane = lax.broadcasted_iota(jnp.int32, kr_ref.shape, 1)
    kr = jnp.where((lane % 64) < 32, kr_ref[...].astype(F32), 0.0)
    kr_ss = jnp.sum(kr * kr, axis=-1, keepdims=True)
    cos = cos_ref[...]
    sin = sin_ref[...]
    qg = qg_ref[...]
    kg = kg_ref[...]
    kr_rot = _rope(kr * kg[:, LANES:], cos, sin)
    for h in range(C_HEADS):
        qh = qpre[:, h * 256:(h + 1) * 256]
        r = lax.rsqrt(jnp.sum(qh * qh, axis=-1, keepdims=True) * (1.0 / C_QK) + EPS)
        qn = qh * r * qg
        qo_ref[:, h * 256:h * 256 + LANES] = qn[:, :LANES].astype(BF16)
        qo_ref[:, h * 256 + LANES:(h + 1) * 256] = _rope(qn[:, LANES:], cos, sin).astype(BF16)
        kn = kvpre[:, h * LANES:(h + 1) * LANES]
        r = lax.rsqrt((jnp.sum(kn * kn, axis=-1, keepdims=True) + kr_ss) * (1.0 / C_QK) + EPS)
        ko_ref[:, h * 256:h * 256 + LANES] = (kn * r * kg[:, :LANES]).astype(BF16)
        ko_ref[:, h * 256 + LANES:(h + 1) * 256] = (kr_rot * r).astype(BF16)
    vo_ref[...] = kvpre[:, C_HEADS * LANES:].astype(BF16)


def _c_prep(h, wq, wkv, qag, kvag, qg, kg, cos, sin, seq, tm=512):
    n = h.shape[0]
    ns = seq // tm
    full = lambda shape: pl.BlockSpec(shape, lambda i: (0,) * len(shape))
    return pl.pallas_call(
        _c_prep_kernel,
        grid=(n // tm,),
        in_specs=[pl.BlockSpec((tm, GROUP_W), lambda i: (i, BLK_C_QKV)),
                  pl.BlockSpec((tm, LANES), lambda i: (i, UNIT_KR_IW)),
                  full((Q_LORA, 4 * 256)), full((KV_LORA, 8 * LANES)),
                  full((1, Q_LORA)), full((1, LANES)), full((1, 256)), full((1, 256)),
                  pl.BlockSpec((tm, LANES), lambda i: (i % ns, 0)),
                  pl.BlockSpec((tm, LANES), lambda i: (i % ns, 0))],
        out_specs=[pl.BlockSpec((tm, 4 * 256), lambda i: (i, 0)),
                   pl.BlockSpec((tm, 4 * 256), lambda i: (i, 0)),
                   pl.BlockSpec((tm, GROUP_W), lambda i: (i, 0))],
        out_shape=[jax.ShapeDtypeStruct((n, 4 * 256), BF16), jax.ShapeDtypeStruct((n, 4 * 256), BF16),
                   jax.ShapeDtypeStruct((n, GROUP_W), BF16)],
        compiler_params=_params(1),
    )(h, h, wq, wkv, qag, kvag, qg, kg, cos, sin)


def _c_attn_kernel(q_ref, k_ref, v_ref, z_ref, o_ref, s_ref, mp_ref, lp_ref, acc_ref, *, tq):
    qt = pl.program_id(1)
    scale = C_QK ** -0.5 * LOG2E
    hrows = [slice(h * tq, (h + 1) * tq) for h in range(C_HEADS)]
    qchunk = (qt * tq + lax.broadcasted_iota(jnp.int32, (tq, 256), 0)) // CHUNK
    kcol = lax.broadcasted_iota(jnp.int32, (tq, 256), 1)
    mp_ref[...] = jnp.full(mp_ref.shape, NEG, F32)
    lp_ref[...] = jnp.zeros(lp_ref.shape, F32)
    acc_ref[...] = jnp.zeros(acc_ref.shape, F32)
    nfull = (qt * tq) // 256
    nblk = ((qt + 1) * tq) // 256

    def logits_blocks(kb0, n, masked):
        for i in range(n):
            kb = kb0 + i
            off = pl.multiple_of(kb * 256, 256)
            for h in range(C_HEADS):
                cols = slice(h * 256, (h + 1) * 256)
                s = _dot_t(q_ref[:, cols], k_ref[pl.ds(off, 256), cols]) * scale
                if masked:
                    s = jnp.where(((kb * 256 + kcol) // CHUNK) <= qchunk, s, NEG)
                s_ref[kb, hrows[h], :] = s
                mp_ref[hrows[h], :] = jnp.maximum(mp_ref[hrows[h], :], jnp.maximum(s[:, :LANES], s[:, LANES:]))

    def value_blocks(kb0, n):
        off = pl.multiple_of(kb0 * 256, 256)
        for h in range(C_HEADS):
            m = mp_ref[hrows[h], :]
            mm = jnp.concatenate([m, m], axis=1)
            p = [jnp.exp2(s_ref[kb0 + i, hrows[h], :] - mm) for i in range(n)]
            lsum = p[0][:, :LANES] + p[0][:, LANES:]
            for pi in p[1:]:
                lsum = lsum + pi[:, :LANES] + pi[:, LANES:]
            lp_ref[hrows[h], :] += lsum
            pb = p[0].astype(BF16) if n == 1 else jnp.concatenate([pi.astype(BF16) for pi in p], axis=1)
            acc_ref[hrows[h], :] += jnp.dot(pb, v_ref[pl.ds(off, n * 256), h * LANES:(h + 1) * LANES],
                                            preferred_element_type=F32)

    _for_blocks(lambda kb0, n: logits_blocks(kb0, n, False), nfull)
    lax.fori_loop(nfull, nblk, lambda kb, c: (logits_blocks(kb, 1, True), c)[1], 0)
    for h in range(C_HEADS):
        m = jnp.max(mp_ref[hrows[h], :], axis=-1, keepdims=True)
        mp_ref[hrows[h], :] = jnp.broadcast_to(m, (tq, LANES))
    _for_blocks(value_blocks, nblk)
    gate = _silu(z_ref[...].astype(F32))
    for h in range(C_HEADS):
        cols = slice(h * LANES, (h + 1) * LANES)
        l = jnp.sum(lp_ref[hrows[h], :], axis=-1, keepdims=True)
        o_ref[:, cols] = (acc_ref[hrows[h], :] / l * gate[:, cols]).astype(BF16)


def _c_attn(qc, kc, vc, h, bsz, seq, tq=256):
    nt = seq // tq
    rows = C_HEADS * tq
    kern = functools.partial(_c_attn_kernel, tq=tq)
    return pl.pallas_call(
        kern,
        grid=(bsz, nt),
        in_specs=[pl.BlockSpec((tq, C_HEADS * 256), lambda b, t: (b * nt + t, 0)),
                  pl.BlockSpec((seq, C_HEADS * 256), lambda b, t: (b, 0)),
                  pl.BlockSpec((seq, GROUP_W), lambda b, t: (b, 0)),
                  pl.BlockSpec((tq, GROUP_W), lambda b, t: (b * nt + t, BLK_C_Z))],
        out_specs=pl.BlockSpec((tq, GROUP_W), lambda b, t: (b * nt + t, 0)),
        out_shape=jax.ShapeDtypeStruct((bsz * seq, GROUP_W), BF16),
        scratch_shapes=[pltpu.VMEM((seq // 256, rows, 256), F32), pltpu.VMEM((rows, LANES), F32),
                        pltpu.VMEM((rows, LANES), F32), pltpu.VMEM((rows, LANES), F32)],
        compiler_params=_params(2),
    )(qc, kc, vc, h)


def _mixer_d_kernel(q_ref, k_ref, v_ref, z_ref, qg_ref, kg_ref, base_ref, o_ref,
                    kpad_ref, vpad_ref, bias_ref, s_ref, mp_ref, *, seq, tq):
    b = pl.program_id(0)
    qt = pl.program_id(1)
    win = tq + D_LEFT

    @pl.when((b == 0) & (qt == 0))
    def _():
        qc = lax.broadcasted_iota(jnp.int32, (tq, win), 0) // CHUNK
        kc = lax.broadcasted_iota(jnp.int32, (tq, win), 1) // CHUNK
        band = (kc >= qc) & (kc <= qc + D_LEFT // CHUNK)
        for h in range(N_HEADS64):
            bias_ref[h] = jnp.where(band, _toeplitz(base_ref[h:h + 1, :], tq, win) * LOG2E, NEG)

    @pl.when(qt == 0)
    def _():
        kpad_ref[0:D_LEFT, :] = jnp.zeros((D_LEFT, GROUP_W), BF16)
        vpad_ref[0:D_LEFT, :] = jnp.zeros((D_LEFT, GROUP_W), BF16)
        for r in range(seq // 256):
            rows = slice(r * 256, (r + 1) * 256)
            dst = slice(D_LEFT + r * 256, D_LEFT + (r + 1) * 256)
            tiles = _rms_heads64(k_ref[rows, :].astype(F32), kg_ref[...], 4)
            for t in range(4):
                kpad_ref[dst, t * LANES:(t + 1) * LANES] = tiles[t].astype(BF16)
            vpad_ref[dst, :] = v_ref[rows, :]

    lo = _lo_mask(tq)
    qtiles = _rms_heads64(q_ref[...].astype(F32), qg_ref[...], 4)
    start = pl.multiple_of(qt * tq, tq)
    gate = _silu(z_ref[...].astype(F32))
    ntile = win // LANES

    def lane_tiles(x):
        return [x[:, i * LANES:(i + 1) * LANES] for i in range(ntile)]

    in_seq = lax.broadcasted_iota(jnp.int32, (tq, win), 1) + start >= D_LEFT

    def logits(h):
        t, half = divmod(h, 2)
        kwin = kpad_ref[pl.ds(start, win), t * LANES:(t + 1) * LANES]
        sel = lo if half == 0 else jnp.logical_not(lo)
        qh = jnp.where(sel, qtiles[t] * (0.125 * LOG2E), 0.0).astype(BF16)
        s = jnp.where(in_seq, _dot_t(qh, kwin) + bias_ref[h], NEG)
        s_ref[h] = s
        m = jnp.max(functools.reduce(jnp.maximum, lane_tiles(s)), axis=-1, keepdims=True)
        mp_ref[h] = jnp.broadcast_to(m, (tq, LANES))

    def in_band(r, c):
        return c * LANES < r * CHUNK + D_LEFT + CHUNK and (c + 1) * LANES > r * CHUNK

    def values(h):
        vwin = vpad_ref[pl.ds(start, win), (h // 2) * LANES:(h // 2 + 1) * LANES]
        p_rows, l_rows = [], []
        for r in range(tq // CHUNK):
            rows = slice(r * CHUNK, (r + 1) * CHUNK)
            m = mp_ref[h, rows, :]
            tiles = [jnp.exp2(s_ref[h, rows, c * LANES:(c + 1) * LANES] - m) if in_band(r, c) else None
                     for c in range(ntile)]
            l_rows.append(functools.reduce(jnp.add, [t for t in tiles if t is not None]))
            p_rows.append(jnp.concatenate([jnp.zeros((CHUNK, LANES), BF16) if t is None else t.astype(BF16)
                                           for t in tiles], axis=1))
        l = jnp.sum(jnp.concatenate(l_rows, axis=0), axis=-1, keepdims=True)
        return jnp.dot(jnp.concatenate(p_rows, axis=0), vwin, preferred_element_type=F32) / l

    outs = []
    logits(0)
    for h in range(1, N_HEADS64 + 1):
        if h < N_HEADS64:
            logits(h)
        outs.append(values(h - 1))
        if h % 2 == 0:
            cols = slice((h // 2 - 1) * LANES, (h // 2) * LANES)
            o_ref[:, cols] = (jnp.where(lo, outs[h - 2], outs[h - 1]) * gate[:, cols]).astype(BF16)


def _mixer_d(h, qg, kg, base, bsz, seq, tq=256):
    nt = seq // tq
    kern = functools.partial(_mixer_d_kernel, seq=seq, tq=tq)
    full = lambda shape: pl.BlockSpec(shape, lambda b, t: (0,) * len(shape))
    return pl.pallas_call(
        kern,
        grid=(bsz, nt),
        in_specs=[pl.BlockSpec((tq, GROUP_W), lambda b, t: (b * nt + t, BLK_D_Q)),
                  pl.BlockSpec((seq, GROUP_W), lambda b, t: (b, BLK_D_K)),
                  pl.BlockSpec((seq, GROUP_W), lambda b, t: (b, BLK_D_V)),
                  pl.BlockSpec((tq, GROUP_W), lambda b, t: (b * nt + t, BLK_D_Z)),
                  full((1, GROUP_W)), full((1, GROUP_W)), full((N_HEADS64, 2 * tq + D_LEFT))],
        out_specs=pl.BlockSpec((tq, GROUP_W), lambda b, t: (b * nt + t, 0)),
        out_shape=jax.ShapeDtypeStruct((bsz * seq, GROUP_W), BF16),
        scratch_shapes=[pltpu.VMEM((seq + D_LEFT, GROUP_W), BF16), pltpu.VMEM((seq + D_LEFT, GROUP_W), BF16),
                        pltpu.VMEM((N_HEADS64, tq, tq + D_LEFT), F32),
                        pltpu.VMEM((N_HEADS64, tq, tq + D_LEFT), F32), pltpu.VMEM((N_HEADS64, tq, LANES), F32)],
        compiler_params=_params(2),
    )(h, h, h, h, qg, kg, base)


def _w_in_pieces(take, zeros):
    c = lambda name, size, off=0: take(_SRC[name] + off, size)
    return [c("a_u", 512), c("a_v", 512), c("a_z", 512),
            c("b_q", 512), c("b_iq", 512), c("b_z", 512),
            c("b_k", 64), c("b_k", 64), c("b_v", 64), c("b_v", 64), c("b_ik", 64), c("b_ik", 64),
            c("c_kr", 32), c("b_iw", 8), zeros(24), c("c_kr", 32, 32), zeros(32),
            c("c_q", 384), c("c_kv", 128), c("c_z", 512),
            c("d_q", 512), c("d_k", 512), c("d_v", 512), c("d_z", 512)]


def _layout_w_in_kernel(w_ref, o_ref):
    tk = w_ref.shape[2]
    pieces = _w_in_pieces(lambda s, n: w_ref[0, s:s + n, :], lambda n: jnp.zeros((n, tk), F32))
    ends = np.cumsum([0] + [p.shape[0] for p in pieces])
    start = 0
    for i in range(1, len(pieces) + 1):
        if ends[i] % GROUP_W == 0:
            group = pieces[start:i]
            blk = group[0] if len(group) == 1 else jnp.concatenate(group, axis=0)
            o_ref[0, ends[start]:ends[i], :] = blk.astype(BF16)
            start = i


def _layout_w_in(w_in, tk=256):
    w_t = jnp.swapaxes(w_in, 1, 2)
    depth, cols, d = w_t.shape
    return pl.pallas_call(
        _layout_w_in_kernel,
        grid=(depth, d // tk),
        in_specs=[pl.BlockSpec((1, cols, tk), lambda l, i: (l, 0, i))],
        out_specs=pl.BlockSpec((1, H_COLS, tk), lambda l, i: (l, 0, i)),
        out_shape=jax.ShapeDtypeStruct((depth, H_COLS, d), BF16),
        compiler_params=_params(2),
    )(w_t)


def _rope_layout(v):
    z = jnp.zeros(v.shape[:-1] + (32,), v.dtype)
    return jnp.concatenate([v[..., :32], z, v[..., 32:], z], axis=-1)


def _layout_c(w_qb, w_kvb, q_gain, k_gain, qa_gain):
    wq = w_qb.reshape(Q_LORA, C_HEADS, C_QK)
    wq = jnp.concatenate([wq[..., :C_NOPE], _rope_layout(wq[..., C_NOPE:])], axis=-1)
    wq = wq.reshape(Q_LORA, C_HEADS * 256).astype(BF16)
    wkv = w_kvb.reshape(KV_LORA, C_HEADS, 2 * LANES)
    wkv = jnp.concatenate([wkv[..., :C_NOPE].reshape(KV_LORA, -1), wkv[..., C_NOPE:].reshape(KV_LORA, -1)],
                          axis=1).astype(BF16)
    lay = lambda g: jnp.concatenate([g[:C_NOPE], _rope_layout(g[C_NOPE:])])[None, :]
    return wq, wkv, lay(q_gain), lay(k_gain), qa_gain[None, :]


def _t5_bucket_static(rel):
    half = T5_BUCKETS // 2
    exact = half // 2
    n = abs(rel)
    if n < exact:
        val = n
    else:
        val = min(exact + (n * n // (exact * exact)).bit_length() - 1, half - 1)
    return (half if rel > 0 else 0) + val


def _t5_tables(t5_bias):
    m = np.arange(512)
    d0 = np.where(m < 256, m, m - 512)
    d1 = np.where(m <= 256, m - 256, m - 768)
    idx0 = np.array([_t5_bucket_static(int(d)) for d in d0], np.int32)
    idx1 = np.array([_t5_bucket_static(int(d)) for d in d1], np.int32)
    far = _t5_bucket_static(-512)
    return t5_bias[idx0].T, t5_bias[idx1].T, t5_bias[far]


def _band_table(rel_bias, tq):
    width = 2 * tq + D_LEFT
    m = np.arange(width)
    dist = np.where(m <= tq + D_LEFT, D_LEFT - m, D_LEFT + width - m)
    idx = np.clip(dist, -REL_CLIP, REL_CLIP) + REL_CLIP
    return rel_bias[idx.astype(np.int32)].T


def _rope_tables(seq):
    inv = ROPE_BASE ** (-jnp.arange(0, C_ROPE, 2, dtype=F32) / C_ROPE)
    ang = jnp.arange(seq, dtype=F32)[:, None] * inv[None, :]
    c, s = jnp.cos(ang), jnp.sin(ang)
    z = jnp.zeros_like(c)
    return jnp.concatenate([c, z, c, z], axis=1), jnp.concatenate([-s, z, s, z], axis=1)


def kernel(x, t5_bias, norm_g, w_in, a_v_gain, a_ws, a_bs, b_q_gain, b_k_gain, c_qa_gain, c_kva_gain,
           c_w_qb, c_w_kvb, c_q_gain, c_k_gain, d_q_gain, d_k_gain, d_rel_bias, w_out):
    bsz, seq, d_model = x.shape
    depth = w_in.shape[0]
    tq = 256
    assert seq % 512 == 0 and seq <= 2048 and d_model % 512 == 0, (seq, d_model)
    assert w_in.shape[2] == _SRC["d_z"] + GROUP_W and w_out.shape[1] == 4 * GROUP_W, (w_in.shape, w_out.shape)
    x2 = x.reshape(bsz * seq, d_model)
    cos, sin = _rope_tables(seq)
    base0, base1, cfar = _t5_tables(t5_bias)
    w_in_blocks = _layout_w_in(w_in)
    for l in range(depth):
        h = _inproj(x2, norm_g[l][None, :], w_in_blocks, l)
        mask = _b_select(h, bsz, seq)
        y_b = _b_attn(h, mask, cfar, jnp.tile(b_q_gain[l], N_HEADS64)[None, :],
                      jnp.tile(b_k_gain[l], 2)[None, :], base0, base1, bsz, seq, tq)
        wq, wkv, qg, kg, qag = _layout_c(c_w_qb[l], c_w_kvb[l], c_q_gain[l], c_k_gain[l], c_qa_gain[l])
        qc, kc, vc = _c_prep(h, wq, wkv, qag, c_kva_gain[l][None, :], qg, kg, cos, sin, seq)
        y_c = _c_attn(qc, kc, vc, h, bsz, seq, tq)
        y_d = _mixer_d(h, jnp.tile(d_q_gain[l], N_HEADS64)[None, :], jnp.tile(d_k_gain[l], N_HEADS64)[None, :],
                       _band_table(d_rel_bias[l], tq), bsz, seq, tq)
        x2 = _outproj(x2, h, a_v_gain[l][None, :], a_ws[l], a_bs[l][:, :, None], (y_b, y_c, y_d),
                      w_out[l].astype(BF16))
    return x2.reshape(bsz, seq, d_model)
```

```python
import functools
import math

import numpy as np
import jax
import jax.numpy as jnp
from jax import lax
from jax.experimental import pallas as pl
from jax.experimental.pallas import tpu as pltpu

F32 = jnp.float32
BF16 = jnp.bfloat16

EPS = 1e-6
NEG = -1e30
LOG2E = math.log2(math.e)
KEY_LOWEST = int(np.float32(-np.finfo(np.float32).max).view(np.int32)) ^ 0x7FFFFFFF
CHUNK = 64
LANES = 128
GROUP_W = 512
A_GROUPS = 4
GMLP_BLOCK = 128
N_HEADS64 = 8
IDX_SCALE = (8 ** -0.5) * 0.125
TOPK_MAX = 256
T5_BUCKETS = 32
C_HEADS = 4
C_NOPE = 128
C_ROPE = 64
C_QK = 192
Q_LORA = 384
KV_LORA = 128
ROPE_BASE = 10000.0
D_LEFT = 8 * CHUNK
REL_CLIP = 128
VMEM_LIMIT = 56 * 1024 * 1024

BLK_A_U, BLK_A_V, BLK_A_Z = 0, 1, 2
BLK_B_Q, BLK_B_IQ, BLK_B_Z, BLK_SMALL = 3, 4, 5, 6
BLK_C_QKV, BLK_C_Z = 7, 8
BLK_D_Q, BLK_D_K, BLK_D_V, BLK_D_Z = 9, 10, 11, 12
H_COLS = 13 * GROUP_W
UNIT_B_K, UNIT_B_V, UNIT_B_IK, UNIT_KR_IW = (BLK_SMALL * 4 + i for i in range(4))
IW_LANE = 32

_SRC = dict(a_u=0, a_v=512, a_z=1024, b_q=1536, b_k=2048, b_v=2112, b_iq=2176, b_ik=2688,
            b_iw=2752, b_z=2760, c_q=3272, c_kv=3656, c_kr=3784, c_z=3848,
            d_q=4360, d_k=4872, d_v=5384, d_z=5896)


def _params(n_axes):
    return pltpu.CompilerParams(dimension_semantics=("arbitrary",) * n_axes,
                                vmem_limit_bytes=VMEM_LIMIT)


def _gelu(x):
    c = math.sqrt(2.0 / math.pi)
    return x * (0.5 * (1.0 + jnp.tanh(c * (x + 0.044715 * (x * x * x)))))


def _silu(x):
    return x * (1.0 / (1.0 + jnp.exp(-x)))


def _dot_t(a, b):
    return lax.dot_general(a, b, (((1,), (1,)), ((), ())), preferred_element_type=F32)


def _lo_mask(rows):
    return lax.broadcasted_iota(jnp.int32, (rows, LANES), 1) < 64


def _rms_heads64(x, gain, ntiles):
    lo = _lo_mask(x.shape[0])
    tiles = []
    for t in range(ntiles):
        xt = x[:, t * LANES:(t + 1) * LANES]
        sq = xt * xt
        s_lo = jnp.sum(jnp.where(lo, sq, 0.0), axis=-1, keepdims=True)
        s_hi = jnp.sum(jnp.where(lo, 0.0, sq), axis=-1, keepdims=True)
        r = jnp.where(lo, lax.rsqrt(s_lo * (1.0 / 64) + EPS), lax.rsqrt(s_hi * (1.0 / 64) + EPS))
        tiles.append(xt * r * gain[:, t * LANES:(t + 1) * LANES])
    return tiles


def _for_blocks(blocks, count):
    def quad(i, carry):
        blocks(4 * i, 4)
        return carry

    lax.fori_loop(0, count // 4, quad, 0)
    first = (count // 4) * 4

    @pl.when((count & 2) != 0)
    def _():
        blocks(first, 2)

    @pl.when((count & 1) != 0)
    def _():
        blocks(first + (count & 2), 1)


def _toeplitz(base_row, rows, width):
    t = jnp.broadcast_to(base_row, (rows, base_row.shape[1]))
    t = pltpu.roll(t, 0, 1, stride=1, stride_axis=0)
    return t[:, :width]


def _inproj_kernel(x_ref, g_ref, w_ref, o_ref):
    x = x_ref[...]
    ms = jnp.mean(x * x, axis=-1, keepdims=True)
    xn = (x * lax.rsqrt(ms + EPS) * g_ref[...]).astype(BF16)
    for c in range(H_COLS // GROUP_W):
        cols = slice(c * GROUP_W, (c + 1) * GROUP_W)
        o_ref[:, cols] = _dot_t(xn, w_ref[cols, :]).astype(BF16)


def _inproj(x2, g, w_all, layer, tm=512):
    n, d = x2.shape
    return pl.pallas_call(
        _inproj_kernel,
        grid=(n // tm,),
        in_specs=[pl.BlockSpec((tm, d), lambda i: (i, 0)),
                  pl.BlockSpec((1, d), lambda i: (0, 0)),
                  pl.BlockSpec((None, H_COLS, d), lambda i: (layer, 0, 0), pipeline_mode=pl.Buffered(1))],
        out_specs=pl.BlockSpec((tm, H_COLS), lambda i: (i, 0)),
        out_shape=jax.ShapeDtypeStruct((n, H_COLS), BF16),
        compiler_params=_params(1),
    )(x2, g, w_all)


def _outproj_kernel(x_ref, uvz_ref, vg_ref, ws_ref, bs_ref, yb_ref, yc_ref, yd_ref, w_ref, o_ref, ya_ref):
    u_ref, v_ref, z_ref = (uvz_ref.at[:, blk * GROUP_W:(blk + 1) * GROUP_W] for blk in (BLK_A_U, BLK_A_V, BLK_A_Z))
    tm, d = x_ref.shape
    wgs = _mixer_a_weights(ws_ref)
    nblk = tm // GMLP_BLOCK
    ncol = d // nblk
    for c in range(nblk):
        cols = slice(c * ncol, (c + 1) * ncol)
        acc = x_ref[:, cols]
        for g, y_ref in ((1, yb_ref), (2, yc_ref), (3, yd_ref)):
            acc = acc + jnp.dot(y_ref[...], w_ref[g * GROUP_W:(g + 1) * GROUP_W, cols],
                                preferred_element_type=F32)
        o_ref[:, cols] = acc
        _mixer_a_rows(u_ref, v_ref, z_ref, vg_ref, wgs, bs_ref, ya_ref, c)
    for c in range(nblk):
        cols = slice(c * ncol, (c + 1) * ncol)
        o_ref[:, cols] += jnp.dot(ya_ref[...], w_ref[0:GROUP_W, cols], preferred_element_type=F32)


def _outproj(x2, h, vg, ws, bs, ys, w, tm=512):
    n, d = x2.shape
    yspec = pl.BlockSpec((tm, GROUP_W), lambda i: (i, 0))
    return pl.pallas_call(
        _outproj_kernel,
        grid=(n // tm,),
        in_specs=[pl.BlockSpec((tm, d), lambda i: (i, 0)),
                  pl.BlockSpec((tm, 3 * GROUP_W), lambda i: (i, 0)),
                  pl.BlockSpec((1, GROUP_W), lambda i: (0, 0)),
                  pl.BlockSpec((A_GROUPS, GMLP_BLOCK, GMLP_BLOCK), lambda i: (0, 0, 0)),
                  pl.BlockSpec((A_GROUPS, GMLP_BLOCK, 1), lambda i: (0, 0, 0)),
                  yspec, yspec, yspec,
                  pl.BlockSpec((4 * GROUP_W, d), lambda i: (0, 0))],
        out_specs=pl.BlockSpec((tm, d), lambda i: (i, 0)),
        out_shape=jax.ShapeDtypeStruct((n, d), F32),
        scratch_shapes=[pltpu.VMEM((tm, GROUP_W), BF16)],
        compiler_params=_params(1),
    )(x2, h, vg, ws, bs, *ys, w)


def _mixer_a_weights(w_ref):
    i = lax.broadcasted_iota(jnp.int32, (GMLP_BLOCK, GMLP_BLOCK), 0)
    j = lax.broadcasted_iota(jnp.int32, (GMLP_BLOCK, GMLP_BLOCK), 1)
    keep = (j // CHUNK) <= (i // CHUNK)
    return [jnp.where(keep, w_ref[g], 0.0).astype(BF16) for g in range(A_GROUPS)]


def _mixer_a_rows(u_ref, v_ref, z_ref, vg_ref, wgs, b_ref, o_ref, blk):
    rows = slice(blk * GMLP_BLOCK, (blk + 1) * GMLP_BLOCK)
    u = _gelu(u_ref[rows, :].astype(F32))
    v = _gelu(v_ref[rows, :].astype(F32))
    ms = jnp.mean(v * v, axis=-1, keepdims=True)
    vb = (v * lax.rsqrt(ms + EPS) * vg_ref[...]).astype(BF16)
    gate = _silu(z_ref[rows, :].astype(F32))
    for g in range(A_GROUPS):
        cols = slice(g * LANES, (g + 1) * LANES)
        sg = jnp.dot(wgs[g], vb[:, cols], preferred_element_type=F32) + b_ref[g]
        o_ref[rows, cols] = (u[:, cols] * sg * gate[:, cols]).astype(BF16)


def _order_key(x):
    return jnp.where(x < 0, x ^ 0x7FFFFFFF, x)


def _b_select_kernel(iq_ref, iw_ref, ik_ref, o_ref, lhs_ref, sc_ref, *, seq, tq, topk):
    t_blk = pl.program_id(1)
    nkb = seq // 256
    nblk = ((t_blk + 1) * tq) // 256
    n_interp = 12
    n_unchecked = 12 + (3 * t_blk) // 2
    hrows = [slice(h * tq, (h + 1) * tq) for h in range(N_HEADS64)]

    lo_half = _lo_mask(tq)
    w_t = (iw_ref[...].astype(F32) * IDX_SCALE).T
    for h in range(N_HEADS64):
        iqt = iq_ref[:, (h // 2) * LANES:(h // 2 + 1) * LANES]
        sel = lo_half if h % 2 == 0 else jnp.logical_not(lo_half)
        lhs_ref[hrows[h], :] = jnp.where(sel, iqt, jnp.zeros_like(iqt))

    def qpos(ln):
        return t_blk * tq + ln.start + lax.broadcasted_iota(jnp.int32, (256, ln.stop - ln.start), 1)

    def krow(ln):
        return lax.broadcasted_iota(jnp.int32, (256, ln.stop - ln.start), 0)

    assert tq in (256, 512)
    every = slice(0, tq)
    tail = slice(256, tq) if tq == 512 else None
    nwide = nblk - 1 if tail else nblk

    def fold8(x):
        return jnp.sum(x.reshape(256 // 8, 8, x.shape[1]), axis=0)

    def widen(part, ln):
        return part if ln == every else jnp.concatenate([jnp.zeros((8, ln.start), F32), part], axis=1)

    def score_block(kb, carry, masked, ln):
        amax, n_pos, n_nn = carry
        off = pl.multiple_of(kb * 256, 256)
        ikblk = ik_ref[pl.ds(off, 256), :]
        score = jnp.zeros((256, ln.stop - ln.start), F32)
        for h in range(N_HEADS64):
            w_h = w_t[IW_LANE + h:IW_LANE + h + 1, ln]
            q_h = lhs_ref[h * tq + ln.start:h * tq + ln.stop, :]
            score = score + w_h * jnp.maximum(_dot_t(ikblk, q_h), 0.0)
        mag = jnp.abs(score)
        if masked:
            adm = ((kb * 256 + krow(ln)) // CHUNK) <= (qpos(ln) // CHUNK)
            score = jnp.where(adm, score, -jnp.inf)
            mag = jnp.where(adm, mag, 0.0)
        sc_ref[kb, :, ln] = score
        return (jnp.maximum(amax, widen(jnp.max(mag.reshape(256 // 8, 8, mag.shape[1]), axis=0), ln)),
                n_pos + widen(fold8(jnp.where(score > 0.0, 1.0, 0.0)), ln),
                n_nn + widen(fold8(jnp.where(score >= 0.0, 1.0, 0.0)), ln))

    nfull = (t_blk * tq) // 256
    zeros8 = jnp.zeros((8, tq), F32)
    carry = lax.fori_loop(0, nfull, lambda kb, c: score_block(kb, c, False, every), (zeros8, zeros8, zeros8))
    carry = lax.fori_loop(nfull, nwide, lambda kb, c: score_block(kb, c, True, every), carry)
    if tail:
        carry = score_block(nblk - 1, carry, True, tail)
    amax = jnp.max(carry[0], axis=0, keepdims=True)
    f_pos = jnp.sum(carry[1], axis=0, keepdims=True)
    f_nn = jnp.sum(carry[2], axis=0, keepdims=True)

    def count(pred):
        def body(kb, acc):
            return acc + fold8(jnp.where(pred(sc_ref[kb], kb, every), 1.0, 0.0))
        acc = lax.fori_loop(0, nwide, body, zeros8)
        if tail:
            acc = acc + widen(fold8(jnp.where(pred(sc_ref[nblk - 1, :, tail], nblk - 1, tail), 1.0, 0.0)), tail)
        return jnp.sum(acc, axis=0, keepdims=True)

    kf = float(topk)
    qrow = t_blk * tq + lax.broadcasted_iota(jnp.int32, (1, tq), 1)
    n_adm = ((qrow // CHUNK + 1) * CHUNK).astype(F32)
    one = jnp.ones((1, tq), jnp.int32)
    pos = f_pos > kf
    neg = f_nn < kf
    lo0 = jnp.where(pos, one, _order_key(lax.bitcast_convert_type(-amax, jnp.int32)))
    hi0 = jnp.where(neg, one - 1, _order_key(lax.bitcast_convert_type(amax, jnp.int32)) + 1)
    w_lo0 = jnp.where(pos, f_pos, n_adm) - kf
    w_hi0 = kf - jnp.where(neg, f_nn, 0.0)
    all_sel = n_adm <= kf
    at_zero = jnp.logical_not(pos | neg)
    done0 = jnp.where(all_sel | at_zero | (hi0 == lo0 + 1), 1.0, 0.0)
    thr0 = jnp.where(all_sel, KEY_LOWEST, jnp.where(at_zero, jnp.where(f_pos == kf, one, one - 1), lo0))

    def as_score(key):
        return lax.bitcast_convert_type(_order_key(key), F32)

    def search_cond(st):
        return jnp.logical_and(st[0][0] < n_interp + 32, st[1] < 0.5)

    def search_step(st):
        it, lo, hi, w_lo, w_hi, side, done, thr = st
        lo_v = as_score(lo)
        hi_v = as_score(hi)
        c_v = lo_v + (hi_v - lo_v) * (w_lo / (w_lo + w_hi))
        c_interp = _order_key(lax.bitcast_convert_type(c_v, jnp.int32))
        c_mid = (lo >> 1) + (hi >> 1) + (lo & hi & 1)
        cand = jnp.where(it < n_interp, c_interp, c_mid)
        cand = jnp.minimum(jnp.maximum(cand, lo + 1), hi - 1)
        cand_v = as_score(cand)
        f = count(lambda s, kb, ln: s >= cand_v[:, ln])
        live = done < 0.5
        up = f > kf
        hit = f == kf
        new_lo = jnp.where(live & up, cand, lo)
        new_hi = jnp.where(live & jnp.logical_not(up), cand, hi)
        new_w_lo = jnp.where(up, f - kf, jnp.where(side < 0.0, 0.5 * w_lo, w_lo))
        new_w_hi = jnp.where(up, jnp.where(side > 0.0, 0.5 * w_hi, w_hi), kf - f)
        new_side = jnp.where(up, 1.0, -1.0)
        new_thr = jnp.where(live, jnp.where(hit, cand, new_lo), thr)
        new_done = jnp.where(live & (hit | (new_hi == new_lo + 1)), 1.0, done)
        return (it + 1, new_lo, new_hi, jnp.where(live, new_w_lo, w_lo),
                jnp.where(live, new_w_hi, w_hi), jnp.where(live, new_side, side), new_done, new_thr)

    def checked_step(st):
        new = search_step(st[0])
        return new, jnp.min(new[-2])

    state = (jnp.int32(0), lo0, hi0, w_lo0, w_hi0, jnp.zeros((1, tq), F32), done0, thr0)
    state = lax.fori_loop(0, n_unchecked, lambda i, st: search_step(st), state)
    thr = as_score(lax.while_loop(search_cond, checked_step, (state, jnp.min(state[-2])))[0][-1])

    any_excess = jnp.max(count(lambda s, kb, ln: s >= thr[:, ln])) > kf

    def store_mask(kb, keep_t, ln):
        for g in range(tq // 256):
            if g * 256 >= ln.start:
                o_ref[g, kb] = keep_t[:, g * 256 - ln.start:(g + 1) * 256 - ln.start]
            else:
                o_ref[g, kb] = jnp.zeros((256, 256), F32)

    def write_mask(keep):
        def body(kb, carry):
            store_mask(kb, jnp.where(keep(sc_ref[kb], kb, every), 1.0, 0.0), every)
            return carry

        lax.fori_loop(0, nwide, body, 0)
        if tail:
            store_mask(nblk - 1, jnp.where(keep(sc_ref[nblk - 1, :, tail], nblk - 1, tail), 1.0, 0.0), tail)

    def write_unused(kb, carry):
        store_mask(kb, jnp.zeros((256, tq), F32), every)
        return carry

    lax.fori_loop(nblk, nkb, write_unused, 0)

    @pl.when(jnp.logical_not(any_excess))
    def _():
        write_mask(lambda s, kb, ln: s >= thr[:, ln])

    @pl.when(any_excess)
    def _():
        need = kf - count(lambda s, kb, ln: s > thr[:, ln])

        def idx_step(it, jmax):
            cand = jmax | lax.shift_left(jnp.int32(1), 10 - it)
            below = count(lambda s, kb, ln: (s == thr[:, ln]) & ((kb * 256 + krow(ln)) < cand[:, ln]))
            return jnp.where(below < need, cand, jmax)

        jmax = lax.fori_loop(0, 11, idx_step, jnp.zeros((1, tq), jnp.int32))
        write_mask(lambda s, kb, ln: (s > thr[:, ln]) | ((s == thr[:, ln]) & ((kb * 256 + krow(ln)) <= jmax[:, ln])))


def _b_select(h, bsz, seq, tq=512):
    tq = min(tq, seq)
    nt = seq // tq
    topk = min(TOPK_MAX, seq // 4)
    kern = functools.partial(_b_select_kernel, seq=seq, tq=tq, topk=topk)
    return pl.pallas_call(
        kern,
        grid=(bsz, nt),
        in_specs=[pl.BlockSpec((tq, GROUP_W), lambda b, t: (b * nt + t, BLK_B_IQ)),
                  pl.BlockSpec((tq, LANES), lambda b, t: (b * nt + t, UNIT_KR_IW)),
                  pl.BlockSpec((seq, LANES), lambda b, t: (b, UNIT_B_IK))],
        out_specs=pl.BlockSpec((tq // 256, seq // 256, 256, 256), lambda b, t: (b * nt + t, 0, 0, 0)),
        out_shape=jax.ShapeDtypeStruct((bsz * seq // 256, seq // 256, 256, 256), F32),
        scratch_shapes=[pltpu.VMEM((N_HEADS64 * tq, LANES), BF16),
                        pltpu.VMEM((seq // 256, 256, tq), F32)],
        compiler_params=_params(2),
    )(h, h, h)


def _b_attn_kernel(cfar_ref, q_ref, z_ref, k_ref, v_ref, msk_ref, qg_ref, kg_ref, base0_ref, base1_ref,
                   o_ref, kn_ref, v1_ref, bias_ref, qall_ref, s_ref, mp_ref, acc_ref, *, seq, tq):
    b = pl.program_id(0)
    t_blk = pl.program_id(1)
    hrows = [slice(h * tq, (h + 1) * tq) for h in range(N_HEADS64)]

    @pl.when((b == 0) & (t_blk == 0))
    def _():
        for h in range(N_HEADS64):
            bias_ref[0, hrows[h], :] = jnp.full((tq, 256), cfar_ref[h] * LOG2E, F32)
            bias_ref[1, hrows[h], :] = _toeplitz(base1_ref[h:h + 1, :], tq, 256) * LOG2E
            bias_ref[2, hrows[h], :] = _toeplitz(base0_ref[h:h + 1, :], tq, 256) * LOG2E

    @pl.when(t_blk == 0)
    def _():
        lo256 = _lo_mask(256)
        for r in range(seq // 256):
            rows = slice(r * 256, (r + 1) * 256)
            k = k_ref[rows, :].astype(F32)
            ms = jnp.mean(k * k, axis=-1, keepdims=True)
            kn_ref[rows, :] = (k * lax.rsqrt(ms + EPS) * kg_ref[...]).astype(BF16)
            v = v_ref[rows, :]
            v1_ref[rows, :] = jnp.where(lo256, v, jnp.ones_like(v))

    lo = _lo_mask(tq)
    qtiles = _rms_heads64(q_ref[...].astype(F32), qg_ref[...], 4)
    for h in range(N_HEADS64):
        sel = lo if h % 2 == 0 else jnp.logical_not(lo)
        qall_ref[hrows[h], :] = jnp.where(sel, qtiles[h // 2] * (0.125 * LOG2E), 0.0).astype(BF16)
    mp_ref[...] = jnp.full(mp_ref.shape, NEG, F32)
    acc_ref[...] = jnp.zeros(acc_ref.shape, F32)
    nblk = t_blk + 1

    def logits_blocks(kb0, n):
        for i in range(n):
            kb = kb0 + i
            off = pl.multiple_of(kb * 256, 256)
            kblk = kn_ref[pl.ds(off, 256), :]
            which = jnp.clip(kb - (t_blk - 2), 0, 2)
            keep = msk_ref[0, kb].T > 0.5
            for h in range(N_HEADS64):
                s = jnp.where(keep, _dot_t(qall_ref[hrows[h], :], kblk) + bias_ref[which, hrows[h], :], NEG)
                s_ref[kb, hrows[h], :] = s
                mp_ref[hrows[h], :] = jnp.maximum(mp_ref[hrows[h], :], jnp.maximum(s[:, :LANES], s[:, LANES:]))

    def value_blocks(kb0, n):
        off = pl.multiple_of(kb0 * 256, 256)
        v1 = v1_ref[pl.ds(off, n * 256), :]
        for h in range(N_HEADS64):
            m = mp_ref[hrows[h], :]
            mm = jnp.concatenate([m, m], axis=1)
            p = [jnp.exp2(s_ref[kb0 + i, hrows[h], :] - mm).astype(BF16) for i in range(n)]
            p = p[0] if n == 1 else jnp.concatenate(p, axis=1)
            acc_ref[hrows[h], :] += jnp.dot(p, v1, preferred_element_type=F32)

    _for_blocks(logits_blocks, nblk)
    for h in range(N_HEADS64):
        m = jnp.max(mp_ref[hrows[h], :], axis=-1, keepdims=True)
        mp_ref[hrows[h], :] = jnp.broadcast_to(m, (tq, LANES))
    _for_blocks(value_blocks, nblk)

    gate = _silu(z_ref[...].astype(F32))
    for t in range(4):
        a_even = acc_ref[hrows[2 * t], :]
        a_odd = acc_ref[hrows[2 * t + 1], :]
        o_even = a_even / pltpu.roll(a_even, 64, 1)
        o_odd = pltpu.roll(a_odd, 64, 1) / a_odd
        cols = slice(t * LANES, (t + 1) * LANES)
        o_ref[:, cols] = (jnp.where(lo, o_even, o_odd) * gate[:, cols]).astype(BF16)


def _b_attn(h, mask, cfar, qg, kg, base0, base1, bsz, seq, tq=256):
    nt = seq // tq
    rows = N_HEADS64 * tq
    kern = functools.partial(_b_attn_kernel, seq=seq, tq=tq)
    full = lambda shape: pl.BlockSpec(shape, lambda b, t: (0,) * len(shape))
    return pl.pallas_call(
        kern,
        grid=(bsz, nt),
        in_specs=[pl.BlockSpec(memory_space=pltpu.SMEM),
                  pl.BlockSpec((tq, GROUP_W), lambda b, t: (b * nt + t, BLK_B_Q)),
                  pl.BlockSpec((tq, GROUP_W), lambda b, t: (b * nt + t, BLK_B_Z)),
                  pl.BlockSpec((seq, LANES), lambda b, t: (b, UNIT_B_K)),
                  pl.BlockSpec((seq, LANES), lambda b, t: (b, UNIT_B_V)),
                  pl.BlockSpec((1, seq // 256, tq, 256), lambda b, t: (b * nt + t, 0, 0, 0)),
                  full((1, GROUP_W)), full((1, LANES)), full((N_HEADS64, 512)), full((N_HEADS64, 512))],
        out_specs=pl.BlockSpec((tq, GROUP_W), lambda b, t: (b * nt + t, 0)),
        out_shape=jax.ShapeDtypeStruct((bsz * seq, GROUP_W), BF16),
        scratch_shapes=[pltpu.VMEM((seq, LANES), BF16), pltpu.VMEM((seq, LANES), BF16),
                        pltpu.VMEM((3, rows, 256), F32), pltpu.VMEM((rows, LANES), BF16),
                        pltpu.VMEM((seq // 256, rows, 256), F32),
                        pltpu.VMEM((rows, LANES), F32), pltpu.VMEM((rows, LANES), F32)],
        compiler_params=_params(2),
    )(cfar, h, h, h, h, mask, qg, kg, base0, base1)


def _rope(tile, cos, sin):
    return tile * cos + pltpu.roll(tile, 64, 1) * sin


def _c_prep_kernel(lat_ref, kr_ref, wq_ref, wkv_ref, qag_ref, kvag_ref, qg_ref, kg_ref, cos_ref, sin_ref,
                   qo_ref, ko_ref, vo_ref):
    cq = lat_ref[:, :Q_LORA].astype(F32)
    ms = jnp.mean(cq * cq, axis=-1, keepdims=True)
    cqn = (cq * lax.rsqrt(ms + EPS) * qag_ref[...]).astype(BF16)
    qpre = jnp.dot(cqn, wq_ref[...], preferred_element_type=F32)
    ckv = lat_ref[:, Q_LORA:].astype(F32)
    ms = jnp.mean(ckv * ckv, axis=-1, keepdims=True)
    ckvn = (ckv * lax.rsqrt(ms + EPS) * kvag_ref[...]).astype(BF16)
    kvpre = jnp.dot(ckvn, wkv_ref[...], preferred_element_type=F32)
    lane = lax.broadcasted_iota(jnp.int32, kr_ref.shape, 1)
    kr = jnp.where((lane % 64) < 32, kr_ref[...].astype(F32), 0.0)
    kr_ss = jnp.sum(kr * kr, axis=-1, keepdims=True)
    cos = cos_ref[...]
    sin = sin_ref[...]
    qg = qg_ref[...]
    kg = kg_ref[...]
    kr_rot = _rope(kr * kg[:, LANES:], cos, sin)
    for h in range(C_HEADS):
        qh = qpre[:, h * 256:(h + 1) * 256]
        r = lax.rsqrt(jnp.sum(qh * qh, axis=-1, keepdims=True) * (1.0 / C_QK) + EPS)
        qn = qh * r * qg
        qo_ref[:, h * 256:h * 256 + LANES] = qn[:, :LANES].astype(BF16)
        qo_ref[:, h * 256 + LANES:(h + 1) * 256] = _rope(qn[:, LANES:], cos, sin).astype(BF16)
        kn = kvpre[:, h * LANES:(h + 1) * LANES]
        r = lax.rsqrt((jnp.sum(kn * kn, axis=-1, keepdims=True) + kr_ss) * (1.0 / C_QK) + EPS)
        ko_ref[:, h * 256:h * 256 + LANES] = (kn * r * kg[:, :LANES]).astype(BF16)
        ko_ref[:, h * 256 + LANES:(h + 1) * 256] = (kr_rot * r).astype(BF16)
    vo_ref[...] = kvpre[:, C_HEADS * LANES:].astype(BF16)


def _c_prep(h, wq, wkv, qag, kvag, qg, kg, cos, sin, seq, tm=512):
    n = h.shape[0]
    ns = seq // tm
    full = lambda shape: pl.BlockSpec(shape, lambda i: (0,) * len(shape))
    return pl.pallas_call(
        _c_prep_kernel,
        grid=(n // tm,),
        in_specs=[pl.BlockSpec((tm, GROUP_W), lambda i: (i, BLK_C_QKV)),
                  pl.BlockSpec((tm, LANES), lambda i: (i, UNIT_KR_IW)),
                  full((Q_LORA, 4 * 256)), full((KV_LORA, 8 * LANES)),
                  full((1, Q_LORA)), full((1, LANES)), full((1, 256)), full((1, 256)),
                  pl.BlockSpec((tm, LANES), lambda i: (i % ns, 0)),
                  pl.BlockSpec((tm, LANES), lambda i: (i % ns, 0))],
        out_specs=[pl.BlockSpec((tm, 4 * 256), lambda i: (i, 0)),
                   pl.BlockSpec((tm, 4 * 256), lambda i: (i, 0)),
                   pl.BlockSpec((tm, GROUP_W), lambda i: (i, 0))],
        out_shape=[jax.ShapeDtypeStruct((n, 4 * 256), BF16), jax.ShapeDtypeStruct((n, 4 * 256), BF16),
                   jax.ShapeDtypeStruct((n, GROUP_W), BF16)],
        compiler_params=_params(1),
    )(h, h, wq, wkv, qag, kvag, qg, kg, cos, sin)


def _c_attn_kernel(q_ref, k_ref, v_ref, z_ref, o_ref, s_ref, mp_ref, lp_ref, acc_ref, *, tq):
    qt = pl.program_id(1)
    scale = C_QK ** -0.5 * LOG2E
    hrows = [slice(h * tq, (h + 1) * tq) for h in range(C_HEADS)]
    qchunk = (qt * tq + lax.broadcasted_iota(jnp.int32, (tq, 256), 0)) // CHUNK
    kcol = lax.broadcasted_iota(jnp.int32, (tq, 256), 1)
    mp_ref[...] = jnp.full(mp_ref.shape, NEG, F32)
    lp_ref[...] = jnp.zeros(lp_ref.shape, F32)
    acc_ref[...] = jnp.zeros(acc_ref.shape, F32)
    nfull = (qt * tq) // 256
    nblk = ((qt + 1) * tq) // 256

    def logits_blocks(kb0, n, masked):
        for i in range(n):
            kb = kb0 + i
            off = pl.multiple_of(kb * 256, 256)
            for h in range(C_HEADS):
                cols = slice(h * 256, (h + 1) * 256)
                s = _dot_t(q_ref[:, cols], k_ref[pl.ds(off, 256), cols]) * scale
                if masked:
                    s = jnp.where(((kb * 256 + kcol) // CHUNK) <= qchunk, s, NEG)
                s_ref[kb, hrows[h], :] = s
                mp_ref[hrows[h], :] = jnp.maximum(mp_ref[hrows[h], :], jnp.maximum(s[:, :LANES], s[:, LANES:]))

    def value_blocks(kb0, n):
        off = pl.multiple_of(kb0 * 256, 256)
        for h in range(C_HEADS):
            m = mp_ref[hrows[h], :]
            mm = jnp.concatenate([m, m], axis=1)
            p = [jnp.exp2(s_ref[kb0 + i, hrows[h], :] - mm) for i in range(n)]
            lsum = p[0][:, :LANES] + p[0][:, LANES:]
            for pi in p[1:]:
                lsum = lsum + pi[:, :LANES] + pi[:, LANES:]
            lp_ref[hrows[h], :] += lsum
            pb = p[0].astype(BF16) if n == 1 else jnp.concatenate([pi.astype(BF16) for pi in p], axis=1)
            acc_ref[hrows[h], :] += jnp.dot(pb, v_ref[pl.ds(off, n * 256), h * LANES:(h + 1) * LANES],
                                            preferred_element_type=F32)

    _for_blocks(lambda kb0, n: logits_blocks(kb0, n, False), nfull)
    lax.fori_loop(nfull, nblk, lambda kb, c: (logits_blocks(kb, 1, True), c)[1], 0)
    for h in range(C_HEADS):
        m = jnp.max(mp_ref[hrows[h], :], axis=-1, keepdims=True)
        mp_ref[hrows[h], :] = jnp.broadcast_to(m, (tq, LANES))
    _for_blocks(value_blocks, nblk)
    gate = _silu(z_ref[...].astype(F32))
    for h in range(C_HEADS):
        cols = slice(h * LANES, (h + 1) * LANES)
        l = jnp.sum(lp_ref[hrows[h], :], axis=-1, keepdims=True)
        o_ref[:, cols] = (acc_ref[hrows[h], :] / l * gate[:, cols]).astype(BF16)


def _c_attn(qc, kc, vc, h, bsz, seq, tq=256):
    nt = seq // tq
    rows = C_HEADS * tq
    kern = functools.partial(_c_attn_kernel, tq=tq)
    return pl.pallas_call(
        kern,
        grid=(bsz, nt),
        in_specs=[pl.BlockSpec((tq, C_HEADS * 256), lambda b, t: (b * nt + t, 0)),
                  pl.BlockSpec((seq, C_HEADS * 256), lambda b, t: (b, 0)),
                  pl.BlockSpec((seq, GROUP_W), lambda b, t: (b, 0)),
                  pl.BlockSpec((tq, GROUP_W), lambda b, t: (b * nt + t, BLK_C_Z))],
        out_specs=pl.BlockSpec((tq, GROUP_W), lambda b, t: (b * nt + t, 0)),
        out_shape=jax.ShapeDtypeStruct((bsz * seq, GROUP_W), BF16),
        scratch_shapes=[pltpu.VMEM((seq // 256, rows, 256), F32), pltpu.VMEM((rows, LANES), F32),
                        pltpu.VMEM((rows, LANES), F32), pltpu.VMEM((rows, LANES), F32)],
        compiler_params=_params(2),
    )(qc, kc, vc, h)


def _mixer_d_kernel(q_ref, k_ref, v_ref, z_ref, qg_ref, kg_ref, base_ref, o_ref,
                    kpad_ref, vpad_ref, bias_ref, s_ref, mp_ref, *, seq, tq):
    b = pl.program_id(0)
    qt = pl.program_id(1)
    win = tq + D_LEFT

    @pl.when((b == 0) & (qt == 0))
    def _():
        qc = lax.broadcasted_iota(jnp.int32, (tq, win), 0) // CHUNK
        kc = lax.broadcasted_iota(jnp.int32, (tq, win), 1) // CHUNK
        band = (kc >= qc) & (kc <= qc + D_LEFT // CHUNK)
        for h in range(N_HEADS64):
            bias_ref[h] = jnp.where(band, _toeplitz(base_ref[h:h + 1, :], tq, win) * LOG2E, NEG)

    @pl.when(qt == 0)
    def _():
        kpad_ref[0:D_LEFT, :] = jnp.zeros((D_LEFT, GROUP_W), BF16)
        vpad_ref[0:D_LEFT, :] = jnp.zeros((D_LEFT, GROUP_W), BF16)
        for r in range(seq // 256):
            rows = slice(r * 256, (r + 1) * 256)
            dst = slice(D_LEFT + r * 256, D_LEFT + (r + 1) * 256)
            tiles = _rms_heads64(k_ref[rows, :].astype(F32), kg_ref[...], 4)
            for t in range(4):
                kpad_ref[dst, t * LANES:(t + 1) * LANES] = tiles[t].astype(BF16)
            vpad_ref[dst, :] = v_ref[rows, :]

    lo = _lo_mask(tq)
    qtiles = _rms_heads64(q_ref[...].astype(F32), qg_ref[...], 4)
    start = pl.multiple_of(qt * tq, tq)
    gate = _silu(z_ref[...].astype(F32))
    ntile = win // LANES

    def lane_tiles(x):
        return [x[:, i * LANES:(i + 1) * LANES] for i in range(ntile)]

    in_seq = lax.broadcasted_iota(jnp.int32, (tq, win), 1) + start >= D_LEFT

    def logits(h):
        t, half = divmod(h, 2)
        kwin = kpad_ref[pl.ds(start, win), t * LANES:(t + 1) * LANES]
        sel = lo if half == 0 else jnp.logical_not(lo)
        qh = jnp.where(sel, qtiles[t] * (0.125 * LOG2E), 0.0).astype(BF16)
        s = jnp.where(in_seq, _dot_t(qh, kwin) + bias_ref[h], NEG)
        s_ref[h] = s
        m = jnp.max(functools.reduce(jnp.maximum, lane_tiles(s)), axis=-1, keepdims=True)
        mp_ref[h] = jnp.broadcast_to(m, (tq, LANES))

    def in_band(r, c):
        return c * LANES < r * CHUNK + D_LEFT + CHUNK and (c + 1) * LANES > r * CHUNK

    def values(h):
        vwin = vpad_ref[pl.ds(start, win), (h // 2) * LANES:(h // 2 + 1) * LANES]
        p_rows, l_rows = [], []
        for r in range(tq // CHUNK):
            rows = slice(r * CHUNK, (r + 1) * CHUNK)
            m = mp_ref[h, rows, :]
            tiles = [jnp.exp2(s_ref[h, rows, c * LANES:(c + 1) * LANES] - m) if in_band(r, c) else None
                     for c in range(ntile)]
            l_rows.append(functools.reduce(jnp.add, [t for t in tiles if t is not None]))
            p_rows.append(jnp.concatenate([jnp.zeros((CHUNK, LANES), BF16) if t is None else t.astype(BF16)
                                           for t in tiles], axis=1))
        l = jnp.sum(jnp.concatenate(l_rows, axis=0), axis=-1, keepdims=True)
        return jnp.dot(jnp.concatenate(p_rows, axis=0), vwin, preferred_element_type=F32) / l

    outs = []
    logits(0)
    for h in range(1, N_HEADS64 + 1):
        if h < N_HEADS64:
            logits(h)
        outs.append(values(h - 1))
        if h % 2 == 0:
            cols = slice((h // 2 - 1) * LANES, (h // 2) * LANES)
            o_ref[:, cols] = (jnp.where(lo, outs[h - 2], outs[h - 1]) * gate[:, cols]).astype(BF16)


def _mixer_d(h, qg, kg, base, bsz, seq, tq=256):
    nt = seq // tq
    kern = functools.partial(_mixer_d_kernel, seq=seq, tq=tq)
    full = lambda shape: pl.BlockSpec(shape, lambda b, t: (0,) * len(shape))
    return pl.pallas_call(
        kern,
        grid=(bsz, nt),
        in_specs=[pl.BlockSpec((tq, GROUP_W), lambda b, t: (b * nt + t, BLK_D_Q)),
                  pl.BlockSpec((seq, GROUP_W), lambda b, t: (b, BLK_D_K)),
                  pl.BlockSpec((seq, GROUP_W), lambda b, t: (b, BLK_D_V)),
                  pl.BlockSpec((tq, GROUP_W), lambda b, t: (b * nt + t, BLK_D_Z)),
                  full((1, GROUP_W)), full((1, GROUP_W)), full((N_HEADS64, 2 * tq + D_LEFT))],
        out_specs=pl.BlockSpec((tq, GROUP_W), lambda b, t: (b * nt + t, 0)),
        out_shape=jax.ShapeDtypeStruct((bsz * seq, GROUP_W), BF16),
        scratch_shapes=[pltpu.VMEM((seq + D_LEFT, GROUP_W), BF16), pltpu.VMEM((seq + D_LEFT, GROUP_W), BF16),
                        pltpu.VMEM((N_HEADS64, tq, tq + D_LEFT), F32),
                        pltpu.VMEM((N_HEADS64, tq, tq + D_LEFT), F32), pltpu.VMEM((N_HEADS64, tq, LANES), F32)],
        compiler_params=_params(2),
    )(h, h, h, h, qg, kg, base)


def _w_in_pieces(take, zeros):
    c = lambda name, size, off=0: take(_SRC[name] + off, size)
    return [c("a_u", 512), c("a_v", 512), c("a_z", 512),
            c("b_q", 512), c("b_iq", 512), c("b_z", 512),
            c("b_k", 64), c("b_k", 64), c("b_v", 64), c("b_v", 64), c("b_ik", 64), c("b_ik", 64),
            c("c_kr", 32), c("b_iw", 8), zeros(24), c("c_kr", 32, 32), zeros(32),
            c("c_q", 384), c("c_kv", 128), c("c_z", 512),
            c("d_q", 512), c("d_k", 512), c("d_v", 512), c("d_z", 512)]


def _layout_w_in_kernel(w_ref, o_ref):
    tk = w_ref.shape[2]
    pieces = _w_in_pieces(lambda s, n: w_ref[0, s:s + n, :], lambda n: jnp.zeros((n, tk), F32))
    ends = np.cumsum([0] + [p.shape[0] for p in pieces])
    start = 0
    for i in range(1, len(pieces) + 1):
        if ends[i] % GROUP_W == 0:
            group = pieces[start:i]
            blk = group[0] if len(group) == 1 else jnp.concatenate(group, axis=0)
            o_ref[0, ends[start]:ends[i], :] = blk.astype(BF16)
            start = i


def _layout_w_in(w_in, tk=256):
    w_t = jnp.swapaxes(w_in, 1, 2)
    depth, cols, d = w_t.shape
    return pl.pallas_call(
        _layout_w_in_kernel,
        grid=(depth, d // tk),
        in_specs=[pl.BlockSpec((1, cols, tk), lambda l, i: (l, 0, i))],
        out_specs=pl.BlockSpec((1, H_COLS, tk), lambda l, i: (l, 0, i)),
        out_shape=jax.ShapeDtypeStruct((depth, H_COLS, d), BF16),
        compiler_params=_params(2),
    )(w_t)


def _rope_layout(v):
    z = jnp.zeros(v.shape[:-1] + (32,), v.dtype)
    return jnp.concatenate([v[..., :32], z, v[..., 32:], z], axis=-1)


def _layout_c(w_qb, w_kvb, q_gain, k_gain, qa_gain):
    wq = w_qb.reshape(Q_LORA, C_HEADS, C_QK)
    wq = jnp.concatenate([wq[..., :C_NOPE], _rope_layout(wq[..., C_NOPE:])], axis=-1)
    wq = wq.reshape(Q_LORA, C_HEADS * 256).astype(BF16)
    wkv = w_kvb.reshape(KV_LORA, C_HEADS, 2 * LANES)
    wkv = jnp.concatenate([wkv[..., :C_NOPE].reshape(KV_LORA, -1), wkv[..., C_NOPE:].reshape(KV_LORA, -1)],
                          axis=1).astype(BF16)
    lay = lambda g: jnp.concatenate([g[:C_NOPE], _rope_layout(g[C_NOPE:])])[None, :]
    return wq, wkv, lay(q_gain), lay(k_gain), qa_gain[None, :]


def _t5_bucket_static(rel):
    half = T5_BUCKETS // 2
    exact = half // 2
    n = abs(rel)
    if n < exact:
        val = n
    else:
        val = min(exact + (n * n // (exact * exact)).bit_length() - 1, half - 1)
    return (half if rel > 0 else 0) + val


def _t5_tables(t5_bias):
    m = np.arange(512)
    d0 = np.where(m < 256, m, m - 512)
    d1 = np.where(m <= 256, m - 256, m - 768)
    idx0 = np.array([_t5_bucket_static(int(d)) for d in d0], np.int32)
    idx1 = np.array([_t5_bucket_static(int(d)) for d in d1], np.int32)
    far = _t5_bucket_static(-512)
    return t5_bias[idx0].T, t5_bias[idx1].T, t5_bias[far]


def _band_table(rel_bias, tq):
    width = 2 * tq + D_LEFT
    m = np.arange(width)
    dist = np.where(m <= tq + D_LEFT, D_LEFT - m, D_LEFT + width - m)
    idx = np.clip(dist, -REL_CLIP, REL_CLIP) + REL_CLIP
    return rel_bias[idx.astype(np.int32)].T


def _rope_tables(seq):
    inv = ROPE_BASE ** (-jnp.arange(0, C_ROPE, 2, dtype=F32) / C_ROPE)
    ang = jnp.arange(seq, dtype=F32)[:, None] * inv[None, :]
    c, s = jnp.cos(ang), jnp.sin(ang)
    z = jnp.zeros_like(c)
    return jnp.concatenate([c, z, c, z], axis=1), jnp.concatenate([-s, z, s, z], axis=1)


def kernel(x, t5_bias, norm_g, w_in, a_v_gain, a_ws, a_bs, b_q_gain, b_k_gain, c_qa_gain, c_kva_gain,
           c_w_qb, c_w_kvb, c_q_gain, c_k_gain, d_q_gain, d_k_gain, d_rel_bias, w_out):
    bsz, seq, d_model = x.shape
    depth = w_in.shape[0]
    tq = 256
    assert seq % 512 == 0 and seq <= 2048 and d_model % 512 == 0, (seq, d_model)
    assert w_in.shape[2] == _SRC["d_z"] + GROUP_W and w_out.shape[1] == 4 * GROUP_W, (w_in.shape, w_out.shape)
    x2 = x.reshape(bsz * seq, d_model)
    cos, sin = _rope_tables(seq)
    base0, base1, cfar = _t5_tables(t5_bias)
    w_in_blocks = _layout_w_in(w_in)
    for l in range(depth):
        h = _inproj(x2, norm_g[l][None, :], w_in_blocks, l)
        mask = _b_select(h, bsz, seq)
        y_b = _b_attn(h, mask, cfar, jnp.tile(b_q_gain[l], N_HEADS64)[None, :],
                      jnp.tile(b_k_gain[l], 2)[None, :], base0, base1, bsz, seq, tq)
        wq, wkv, qg, kg, qag = _layout_c(c_w_qb[l], c_w_kvb[l], c_q_gain[l], c_k_gain[l], c_qa_gain[l])
        qc, kc, vc = _c_prep(h, wq, wkv, qag, c_kva_gain[l][None, :], qg, kg, cos, sin, seq)
        y_c = _c_attn(qc, kc, vc, h, bsz, seq, tq)
        y_d = _mixer_d(h, jnp.tile(d_q_gain[l], N_HEADS64)[None, :], jnp.tile(d_k_gain[l], N_HEADS64)[None, :],
                       _band_table(d_rel_bias[l], tq), bsz, seq, tq)
        x2 = _outproj(x2, h, a_v_gain[l][None, :], a_ws[l], a_bs[l][:, :, None], (y_b, y_c, y_d),
                      w_out[l].astype(BF16))
    return x2.reshape(bsz, seq, d_model)
```

```python
import functools
import math

import numpy as np
import jax
import jax.numpy as jnp
from jax import lax
from jax.experimental import pallas as pl
from jax.experimental.pallas import tpu as pltpu

F32 = jnp.float32
BF16 = jnp.bfloat16

EPS = 1e-6
NEG = -1e30
LOG2E = math.log2(math.e)
KEY_LOWEST = int(np.float32(-np.finfo(np.float32).max).view(np.int32)) ^ 0x7FFFFFFF
CHUNK = 64
LANES = 128
GROUP_W = 512
A_GROUPS = 4
GMLP_BLOCK = 128
N_HEADS64 = 8
IDX_SCALE = (8 ** -0.5) * 0.125
TOPK_MAX = 256
T5_BUCKETS = 32
C_HEADS = 4
C_NOPE = 128
C_ROPE = 64
C_QK = 192
Q_LORA = 384
KV_LORA = 128
ROPE_BASE = 10000.0
D_LEFT = 8 * CHUNK
REL_CLIP = 128
VMEM_LIMIT = 56 * 1024 * 1024

BLK_A_U, BLK_A_V, BLK_A_Z = 0, 1, 2
BLK_B_Q, BLK_B_IQ, BLK_B_Z, BLK_SMALL = 3, 4, 5, 6
BLK_C_QKV, BLK_C_Z = 7, 8
BLK_D_Q, BLK_D_K, BLK_D_V, BLK_D_Z = 9, 10, 11, 12
H_COLS = 13 * GROUP_W
UNIT_B_K, UNIT_B_V, UNIT_B_IK, UNIT_KR_IW = (BLK_SMALL * 4 + i for i in range(4))
IW_LANE = 32

_SRC = dict(a_u=0, a_v=512, a_z=1024, b_q=1536, b_k=2048, b_v=2112, b_iq=2176, b_ik=2688,
            b_iw=2752, b_z=2760, c_q=3272, c_kv=3656, c_kr=3784, c_z=3848,
            d_q=4360, d_k=4872, d_v=5384, d_z=5896)


def _params(n_axes):
    return pltpu.CompilerParams(dimension_semantics=("arbitrary",) * n_axes,
                                vmem_limit_bytes=VMEM_LIMIT)


def _gelu(x):
    c = math.sqrt(2.0 / math.pi)
    return x * (0.5 * (1.0 + jnp.tanh(c * (x + 0.044715 * (x * x * x)))))


def _silu(x):
    return x * (1.0 / (1.0 + jnp.exp(-x)))


def _dot_t(a, b):
    return lax.dot_general(a, b, (((1,), (1,)), ((), ())), preferred_element_type=F32)


def _lo_mask(rows):
    return lax.broadcasted_iota(jnp.int32, (rows, LANES), 1) < 64


def _rms_heads64(x, gain, ntiles):
    lo = _lo_mask(x.shape[0])
    tiles = []
    for t in range(ntiles):
        xt = x[:, t * LANES:(t + 1) * LANES]
        sq = xt * xt
        s_lo = jnp.sum(jnp.where(lo, sq, 0.0), axis=-1, keepdims=True)
        s_hi = jnp.sum(jnp.where(lo, 0.0, sq), axis=-1, keepdims=True)
        r = jnp.where(lo, lax.rsqrt(s_lo * (1.0 / 64) + EPS), lax.rsqrt(s_hi * (1.0 / 64) + EPS))
        tiles.append(xt * r * gain[:, t * LANES:(t + 1) * LANES])
    return tiles


def _for_blocks(blocks, count):
    def quad(i, carry):
        blocks(4 * i, 4)
        return carry

    lax.fori_loop(0, count // 4, quad, 0)
    first = (count // 4) * 4

    @pl.when((count & 2) != 0)
    def _():
        blocks(first, 2)

    @pl.when((count & 1) != 0)
    def _():
        blocks(first + (count & 2), 1)


def _toeplitz(base_row, rows, width):
    t = jnp.broadcast_to(base_row, (rows, base_row.shape[1]))
    t = pltpu.roll(t, 0, 1, stride=1, stride_axis=0)
    return t[:, :width]


def _inproj_kernel(x_ref, g_ref, w_ref, o_ref):
    x = x_ref[...]
    ms = jnp.mean(x * x, axis=-1, keepdims=True)
    xn = (x * lax.rsqrt(ms + EPS) * g_ref[...]).astype(BF16)
    for c in range(H_COLS // GROUP_W):
        cols = slice(c * GROUP_W, (c + 1) * GROUP_W)
        o_ref[:, cols] = _dot_t(xn, w_ref[cols, :]).astype(BF16)


def _inproj(x2, g, w_all, layer, tm=512):
    n, d = x2.shape
    return pl.pallas_call(
        _inproj_kernel,
        grid=(n // tm,),
        in_specs=[pl.BlockSpec((tm, d), lambda i: (i, 0)),
                  pl.BlockSpec((1, d), lambda i: (0, 0)),
                  pl.BlockSpec((None, H_COLS, d), lambda i: (layer, 0, 0), pipeline_mode=pl.Buffered(1))],
        out_specs=pl.BlockSpec((tm, H_COLS), lambda i: (i, 0)),
        out_shape=jax.ShapeDtypeStruct((n, H_COLS), BF16),
        compiler_params=_params(1),
    )(x2, g, w_all)


def _outproj_kernel(x_ref, uvz_ref, vg_ref, ws_ref, bs_ref, yb_ref, yc_ref, yd_ref, w_ref, o_ref, ya_ref):
    u_ref, v_ref, z_ref = (uvz_ref.at[:, blk * GROUP_W:(blk + 1) * GROUP_W] for blk in (BLK_A_U, BLK_A_V, BLK_A_Z))
    tm, d = x_ref.shape
    wgs = _mixer_a_weights(ws_ref)
    nblk = tm // GMLP_BLOCK
    ncol = d // nblk
    for c in range(nblk):
        cols = slice(c * ncol, (c + 1) * ncol)
        acc = x_ref[:, cols]
        for g, y_ref in ((1, yb_ref), (2, yc_ref), (3, yd_ref)):
            acc = acc + jnp.dot(y_ref[...], w_ref[g * GROUP_W:(g + 1) * GROUP_W, cols],
                                preferred_element_type=F32)
        o_ref[:, cols] = acc
        _mixer_a_rows(u_ref, v_ref, z_ref, vg_ref, wgs, bs_ref, ya_ref, c)
    for c in range(nblk):
        cols = slice(c * ncol, (c + 1) * ncol)
        o_ref[:, cols] += jnp.dot(ya_ref[...], w_ref[0:GROUP_W, cols], preferred_element_type=F32)


def _outproj(x2, h, vg, ws, bs, ys, w, tm=512):
    n, d = x2.shape
    yspec = pl.BlockSpec((tm, GROUP_W), lambda i: (i, 0))
    return pl.pallas_call(
        _outproj_kernel,
        grid=(n // tm,),
        in_specs=[pl.BlockSpec((tm, d), lambda i: (i, 0)),
                  pl.BlockSpec((tm, 3 * GROUP_W), lambda i: (i, 0)),
                  pl.BlockSpec((1, GROUP_W), lambda i: (0, 0)),
                  pl.BlockSpec((A_GROUPS, GMLP_BLOCK, GMLP_BLOCK), lambda i: (0, 0, 0)),
                  pl.BlockSpec((A_GROUPS, GMLP_BLOCK, 1), lambda i: (0, 0, 0)),
                  yspec, yspec, yspec,
                  pl.BlockSpec((4 * GROUP_W, d), lambda i: (0, 0))],
        out_specs=pl.BlockSpec((tm, d), lambda i: (i, 0)),
        out_shape=jax.ShapeDtypeStruct((n, d), F32),
        scratch_shapes=[pltpu.VMEM((tm, GROUP_W), BF16)],
        compiler_params=_params(1),
    )(x2, h, vg, ws, bs, *ys, w)


def _mixer_a_weights(w_ref):
    i = lax.broadcasted_iota(jnp.int32, (GMLP_BLOCK, GMLP_BLOCK), 0)
    j = lax.broadcasted_iota(jnp.int32, (GMLP_BLOCK, GMLP_BLOCK), 1)
    keep = (j // CHUNK) <= (i // CHUNK)
    return [jnp.where(keep, w_ref[g], 0.0).astype(BF16) for g in range(A_GROUPS)]


def _mixer_a_rows(u_ref, v_ref, z_ref, vg_ref, wgs, b_ref, o_ref, blk):
    rows = slice(blk * GMLP_BLOCK, (blk + 1) * GMLP_BLOCK)
    u = _gelu(u_ref[rows, :].astype(F32))
    v = _gelu(v_ref[rows, :].astype(F32))
    ms = jnp.mean(v * v, axis=-1, keepdims=True)
    vb = (v * lax.rsqrt(ms + EPS) * vg_ref[...]).astype(BF16)
    gate = _silu(z_ref[rows, :].astype(F32))
    for g in range(A_GROUPS):
        cols = slice(g * LANES, (g + 1) * LANES)
        sg = jnp.dot(wgs[g], vb[:, cols], preferred_element_type=F32) + b_ref[g]
        o_ref[rows, cols] = (u[:, cols] * sg * gate[:, cols]).astype(BF16)


def _order_key(x):
    return jnp.where(x < 0, x ^ 0x7FFFFFFF, x)


def _b_select_kernel(iq_ref, iw_ref, ik_ref, o_ref, lhs_ref, sc_ref, *, seq, tq, topk):
    t_blk = pl.program_id(1)
    nkb = seq // 256
    nblk = ((t_blk + 1) * tq) // 256
    n_interp = 12
    n_unchecked = 12 + (3 * t_blk) // 2
    hrows = [slice(h * tq, (h + 1) * tq) for h in range(N_HEADS64)]

    lo_half = _lo_mask(tq)
    w_t = (iw_ref[...].astype(F32) * IDX_SCALE).T
    for h in range(N_HEADS64):
        iqt = iq_ref[:, (h // 2) * LANES:(h // 2 + 1) * LANES]
        sel = lo_half if h % 2 == 0 else jnp.logical_not(lo_half)
        lhs_ref[hrows[h], :] = jnp.where(sel, iqt, jnp.zeros_like(iqt))

    def qpos(ln):
        return t_blk * tq + ln.start + lax.broadcasted_iota(jnp.int32, (256, ln.stop - ln.start), 1)

    def krow(ln):
        return lax.broadcasted_iota(jnp.int32, (256, ln.stop - ln.start), 0)

    assert tq in (256, 512)
    every = slice(0, tq)
    tail = slice(256, tq) if tq == 512 else None
    nwide = nblk - 1 if tail else nblk

    def fold8(x):
        return jnp.sum(x.reshape(256 // 8, 8, x.shape[1]), axis=0)

    def widen(part, ln):
        return part if ln == every else jnp.concatenate([jnp.zeros((8, ln.start), F32), part], axis=1)

    def score_block(kb, carry, masked, ln):
        amax, n_pos, n_nn = carry
        off = pl.multiple_of(kb * 256, 256)
        ikblk = ik_ref[pl.ds(off, 256), :]
        score = jnp.zeros((256, ln.stop - ln.start), F32)
        for h in range(N_HEADS64):
            w_h = w_t[IW_LANE + h:IW_LANE + h + 1, ln]
            q_h = lhs_ref[h * tq + ln.start:h * tq + ln.stop, :]
            score = score + w_h * jnp.maximum(_dot_t(ikblk, q_h), 0.0)
        mag = jnp.abs(score)
        if masked:
            adm = ((kb * 256 + krow(ln)) // CHUNK) <= (qpos(ln) // CHUNK)
            score = jnp.where(adm, score, -jnp.inf)
            mag = jnp.where(adm, mag, 0.0)
        sc_ref[kb, :, ln] = score
        return (jnp.maximum(amax, widen(jnp.max(mag.reshape(256 // 8, 8, mag.shape[1]), axis=0), ln)),
                n_pos + widen(fold8(jnp.where(score > 0.0, 1.0, 0.0)), ln),
                n_nn + widen(fold8(jnp.where(score >= 0.0, 1.0, 0.0)), ln))

    nfull = (t_blk * tq) // 256
    zeros8 = jnp.zeros((8, tq), F32)
    carry = lax.fori_loop(0, nfull, lambda kb, c: score_block(kb, c, False, every), (zeros8, zeros8, zeros8))
    carry = lax.fori_loop(nfull, nwide, lambda kb, c: score_block(kb, c, True, every), carry)
    if tail:
        carry = score_block(nblk - 1, carry, True, tail)
    amax = jnp.max(carry[0], axis=0, keepdims=True)
    f_pos = jnp.sum(carry[1], axis=0, keepdims=True)
    f_nn = jnp.sum(carry[2], axis=0, keepdims=True)

    def count(pred):
        def body(kb, acc):
            return acc + fold8(jnp.where(pred(sc_ref[kb], kb, every), 1.0, 0.0))
        acc = lax.fori_loop(0, nwide, body, zeros8)
        if tail:
            acc = acc + widen(fold8(jnp.where(pred(sc_ref[nblk - 1, :, tail], nblk - 1, tail), 1.0, 0.0)), tail)
        return jnp.sum(acc, axis=0, keepdims=True)

    kf = float(topk)
    qrow = t_blk * tq + lax.broadcasted_iota(jnp.int32, (1, tq), 1)
    n_adm = ((qrow // CHUNK + 1) * CHUNK).astype(F32)
    one = jnp.ones((1, tq), jnp.int32)
    pos = f_pos > kf
    neg = f_nn < kf
    lo0 = jnp.where(pos, one, _order_key(lax.bitcast_convert_type(-amax, jnp.int32)))
    hi0 = jnp.where(neg, one - 1, _order_key(lax.bitcast_convert_type(amax, jnp.int32)) + 1)
    w_lo0 = jnp.where(pos, f_pos, n_adm) - kf
    w_hi0 = kf - jnp.where(neg, f_nn, 0.0)
    all_sel = n_adm <= kf
    at_zero = jnp.logical_not(pos | neg)
    done0 = jnp.where(all_sel | at_zero | (hi0 == lo0 + 1), 1.0, 0.0)
    thr0 = jnp.where(all_sel, KEY_LOWEST, jnp.where(at_zero, jnp.where(f_pos == kf, one, one - 1), lo0))

    def as_score(key):
        return lax.bitcast_convert_type(_order_key(key), F32)

    def search_cond(st):
        return jnp.logical_and(st[0][0] < n_interp + 32, st[1] < 0.5)

    def search_step(st):
        it, lo, hi, w_lo, w_hi, side, done, thr = st
        lo_v = as_score(lo)
        hi_v = as_score(hi)
        c_v = lo_v + (hi_v - lo_v) * (w_lo / (w_lo + w_hi))
        c_interp = _order_key(lax.bitcast_convert_type(c_v, jnp.int32))
        c_mid = (lo >> 1) + (hi >> 1) + (lo & hi & 1)
        cand = jnp.where(it < n_interp, c_interp, c_mid)
        cand = jnp.minimum(jnp.maximum(cand, lo + 1), hi - 1)
        cand_v = as_score(cand)
        f = count(lambda s, kb, ln: s >= cand_v[:, ln])
        live = done < 0.5
        up = f > kf
        hit = f == kf
        new_lo = jnp.where(live & up, cand, lo)
        new_hi = jnp.where(live & jnp.logical_not(up), cand, hi)
        new_w_lo = jnp.where(up, f - kf, jnp.where(side < 0.0, 0.5 * w_lo, w_lo))
        new_w_hi = jnp.where(up, jnp.where(side > 0.0, 0.5 * w_hi, w_hi), kf - f)
        new_side = jnp.where(up, 1.0, -1.0)
        new_thr = jnp.where(live, jnp.where(hit, cand, new_lo), thr)
        new_done = jnp.where(live & (hit | (new_hi == new_lo + 1)), 1.0, done)
        return (it + 1, new_lo, new_hi, jnp.where(live, new_w_lo, w_lo),
                jnp.where(live, new_w_hi, w_hi), jnp.where(live, new_side, side), new_done, new_thr)

    def checked_step(st):
        new = search_step(st[0])
        return new, jnp.min(new[-2])

    state = (jnp.int32(0), lo0, hi0, w_lo0, w_hi0, jnp.zeros((1, tq), F32), done0, thr0)
    state = lax.fori_loop(0, n_unchecked, lambda i, st: search_step(st), state)
    thr = as_score(lax.while_loop(search_cond, checked_step, (state, jnp.min(state[-2])))[0][-1])

    any_excess = jnp.max(count(lambda s, kb, ln: s >= thr[:, ln])) > kf

    def store_mask(kb, keep_t, ln):
        for g in range(tq // 256):
            if g * 256 >= ln.start:
                o_ref[g, kb] = keep_t[:, g * 256 - ln.start:(g + 1) * 256 - ln.start]
            else:
                o_ref[g, kb] = jnp.zeros((256, 256), F32)

    def write_mask(keep):
        def body(kb, carry):
            store_mask(kb, jnp.where(keep(sc_ref[kb], kb, every), 1.0, 0.0), every)
            return carry

        lax.fori_loop(0, nwide, body, 0)
        if tail:
            store_mask(nblk - 1, jnp.where(keep(sc_ref[nblk - 1, :, tail], nblk - 1, tail), 1.0, 0.0), tail)

    def write_unused(kb, carry):
        store_mask(kb, jnp.zeros((256, tq), F32), every)
        return carry

    lax.fori_loop(nblk, nkb, write_unused, 0)

    @pl.when(jnp.logical_not(any_excess))
    def _():
        write_mask(lambda s, kb, ln: s >= thr[:, ln])

    @pl.when(any_excess)
    def _():
        need = kf - count(lambda s, kb, ln: s > thr[:, ln])

        def idx_step(it, jmax):
            cand = jmax | lax.shift_left(jnp.int32(1), 10 - it)
            below = count(lambda s, kb, ln: (s == thr[:, ln]) & ((kb * 256 + krow(ln)) < cand[:, ln]))
            return jnp.where(below < need, cand, jmax)

        jmax = lax.fori_loop(0, 11, idx_step, jnp.zeros((1, tq), jnp.int32))
        write_mask(lambda s, kb, ln: (s > thr[:, ln]) | ((s == thr[:, ln]) & ((kb * 256 + krow(ln)) <= jmax[:, ln])))


def _b_select(h, bsz, seq, tq=512):
    tq = min(tq, seq)
    nt = seq // tq
    topk = min(TOPK_MAX, seq // 4)
    kern = functools.partial(_b_select_kernel, seq=seq, tq=tq, topk=topk)
    return pl.pallas_call(
        kern,
        grid=(bsz, nt),
        in_specs=[pl.BlockSpec((tq, GROUP_W), lambda b, t: (b * nt + t, BLK_B_IQ)),
                  pl.BlockSpec((tq, LANES), lambda b, t: (b * nt + t, UNIT_KR_IW)),
                  pl.BlockSpec((seq, LANES), lambda b, t: (b, UNIT_B_IK))],
        out_specs=pl.BlockSpec((tq // 256, seq // 256, 256, 256), lambda b, t: (b * nt + t, 0, 0, 0)),
        out_shape=jax.ShapeDtypeStruct((bsz * seq // 256, seq // 256, 256, 256), F32),
        scratch_shapes=[pltpu.VMEM((N_HEADS64 * tq, LANES), BF16),
                        pltpu.VMEM((seq // 256, 256, tq), F32)],
        compiler_params=_params(2),
    )(h, h, h)


def _b_attn_kernel(cfar_ref, q_ref, z_ref, k_ref, v_ref, msk_ref, qg_ref, kg_ref, base0_ref, base1_ref,
                   o_ref, kn_ref, v1_ref, bias_ref, qall_ref, s_ref, mp_ref, acc_ref, *, seq, tq):
    b = pl.program_id(0)
    t_blk = pl.program_id(1)
    hrows = [slice(h * tq, (h + 1) * tq) for h in range(N_HEADS64)]

    @pl.when((b == 0) & (t_blk == 0))
    def _():
        for h in range(N_HEADS64):
            bias_ref[0, hrows[h], :] = jnp.full((tq, 256), cfar_ref[h] * LOG2E, F32)
            bias_ref[1, hrows[h], :] = _toeplitz(base1_ref[h:h + 1, :], tq, 256) * LOG2E
            bias_ref[2, hrows[h], :] = _toeplitz(base0_ref[h:h + 1, :], tq, 256) * LOG2E

    @pl.when(t_blk == 0)
    def _():
        lo256 = _lo_mask(256)
        for r in range(seq // 256):
            rows = slice(r * 256, (r + 1) * 256)
            k = k_ref[rows, :].astype(F32)
            ms = jnp.mean(k * k, axis=-1, keepdims=True)
            kn_ref[rows, :] = (k * lax.rsqrt(ms + EPS) * kg_ref[...]).astype(BF16)
            v = v_ref[rows, :]
            v1_ref[rows, :] = jnp.where(lo256, v, jnp.ones_like(v))

    lo = _lo_mask(tq)
    qtiles = _rms_heads64(q_ref[...].astype(F32), qg_ref[...], 4)
    for h in range(N_HEADS64):
        sel = lo if h % 2 == 0 else jnp.logical_not(lo)
        qall_ref[hrows[h], :] = jnp.where(sel, qtiles[h // 2] * (0.125 * LOG2E), 0.0).astype(BF16)
    mp_ref[...] = jnp.full(mp_ref.shape, NEG, F32)
    acc_ref[...] = jnp.zeros(acc_ref.shape, F32)
    nblk = t_blk + 1

    def logits_blocks(kb0, n):
        for i in range(n):
            kb = kb0 + i
            off = pl.multiple_of(kb * 256, 256)
            kblk = kn_ref[pl.ds(off, 256), :]
            which = jnp.clip(kb - (t_blk - 2), 0, 2)
            keep = msk_ref[0, kb].T > 0.5
            for h in range(N_HEADS64):
                s = jnp.where(keep, _dot_t(qall_ref[hrows[h], :], kblk) + bias_ref[which, hrows[h], :], NEG)
                s_ref[kb, hrows[h], :] = s
                mp_ref[hrows[h], :] = jnp.maximum(mp_ref[hrows[h], :], jnp.maximum(s[:, :LANES], s[:, LANES:]))

    def value_blocks(kb0, n):
        off = pl.multiple_of(kb0 * 256, 256)
        v1 = v1_ref[pl.ds(off, n * 256), :]
        for h in range(N_HEADS64):
            m = mp_ref[hrows[h], :]
            mm = jnp.concatenate([m, m], axis=1)
            p = [jnp.exp2(s_ref[kb0 + i, hrows[h], :] - mm).astype(BF16) for i in range(n)]
            p = p[0] if n == 1 else jnp.concatenate(p, axis=1)
            acc_ref[hrows[h], :] += jnp.dot(p, v1, preferred_element_type=F32)

    _for_blocks(logits_blocks, nblk)
    for h in range(N_HEADS64):
        m = jnp.max(mp_ref[hrows[h], :], axis=-1, keepdims=True)
        mp_ref[hrows[h], :] = jnp.broadcast_to(m, (tq, LANES))
    _for_blocks(value_blocks, nblk)

    gate = _silu(z_ref[...].astype(F32))
    for t in range(4):
        a_even = acc_ref[hrows[2 * t], :]
        a_odd = acc_ref[hrows[2 * t + 1], :]
        o_even = a_even / pltpu.roll(a_even, 64, 1)
        o_odd = pltpu.roll(a_odd, 64, 1) / a_odd
        cols = slice(t * LANES, (t + 1) * LANES)
        o_ref[:, cols] = (jnp.where(lo, o_even, o_odd) * gate[:, cols]).astype(BF16)


def _b_attn(h, mask, cfar, qg, kg, base0, base1, bsz, seq, tq=256):
    nt = seq // tq
    rows = N_HEADS64 * tq
    kern = functools.partial(_b_attn_kernel, seq=seq, tq=tq)
    full = lambda shape: pl.BlockSpec(shape, lambda b, t: (0,) * len(shape))
    return pl.pallas_call(
        kern,
        grid=(bsz, nt),
        in_specs=[pl.BlockSpec(memory_space=pltpu.SMEM),
                  pl.BlockSpec((tq, GROUP_W), lambda b, t: (b * nt + t, BLK_B_Q)),
                  pl.BlockSpec((tq, GROUP_W), lambda b, t: (b * nt + t, BLK_B_Z)),
                  pl.BlockSpec((seq, LANES), lambda b, t: (b, UNIT_B_K)),
                  pl.BlockSpec((seq, LANES), lambda b, t: (b, UNIT_B_V)),
                  pl.BlockSpec((1, seq // 256, tq, 256), lambda b, t: (b * nt + t, 0, 0, 0)),
                  full((1, GROUP_W)), full((1, LANES)), full((N_HEADS64, 512)), full((N_HEADS64, 512))],
        out_specs=pl.BlockSpec((tq, GROUP_W), lambda b, t: (b * nt + t, 0)),
        out_shape=jax.ShapeDtypeStruct((bsz * seq, GROUP_W), BF16),
        scratch_shapes=[pltpu.VMEM((seq, LANES), BF16), pltpu.VMEM((seq, LANES), BF16),
                        pltpu.VMEM((3, rows, 256), F32), pltpu.VMEM((rows, LANES), BF16),
                        pltpu.VMEM((seq // 256, rows, 256), F32),
                        pltpu.VMEM((rows, LANES), F32), pltpu.VMEM((rows, LANES), F32)],
        compiler_params=_params(2),
    )(cfar, h, h, h, h, mask, qg, kg, base0, base1)


def _rope(tile, cos, sin):
    return tile * cos + pltpu.roll(tile, 64, 1) * sin


def _c_prep_kernel(lat_ref, kr_ref, wq_ref, wkv_ref, qag_ref, kvag_ref, qg_ref, kg_ref, cos_ref, sin_ref,
                   qo_ref, ko_ref, vo_ref):
    cq = lat_ref[:, :Q_LORA].astype(F32)
    ms = jnp.mean(cq * cq, axis=-1, keepdims=True)
    cqn = (cq * lax.rsqrt(ms + EPS) * qag_ref[...]).astype(BF16)
    qpre = jnp.dot(cqn, wq_ref[...], preferred_element_type=F32)
    ckv = lat_ref[:, Q_LORA:].astype(F32)
    ms = jnp.mean(ckv * ckv, axis=-1, keepdims=True)
    ckvn = (ckv * lax.rsqrt(ms + EPS) * kvag_ref[...]).astype(BF16)
    kvpre = jnp.dot(ckvn, wkv_ref[...], preferred_element_type=F32)
    lane = lax.broadcasted_iota(jnp.int32, kr_ref.shape, 1)
    kr = jnp.where((lane % 64) < 32, kr_ref[...].astype(F32), 0.0)
    kr_ss = jnp.sum(kr * kr, axis=-1, keepdims=True)
    cos = cos_ref[...]
    sin = sin_ref[...]
    qg = qg_ref[...]
    kg = kg_ref[...]
    kr_rot = _rope(kr * kg[:, LANES:], cos, sin)
    for h in range(C_HEADS):
        qh = qpre[:, h * 256:(h + 1) * 256]
        r = lax.rsqrt(jnp.sum(qh * qh, axis=-1, keepdims=True) * (1.0 / C_QK) + EPS)
        qn = qh * r * qg
        qo_ref[:, h * 256:h * 256 + LANES] = qn[:, :LANES].astype(BF16)
        qo_ref[:, h * 256 + LANES:(h + 1) * 256] = _rope(qn[:, LANES:], cos, sin).astype(BF16)
        kn = kvpre[:, h * LANES:(h + 1) * LANES]
        r = lax.rsqrt((jnp.sum(kn * kn, axis=-1, keepdims=True) + kr_ss) * (1.0 / C_QK) + EPS)
        ko_ref[:, h * 256:h * 256 + LANES] = (kn * r * kg[:, :LANES]).astype(BF16)
        ko_ref[:, h * 256 + LANES:(h + 1) * 256] = (kr_rot * r).astype(BF16)
    vo_ref[...] = kvpre[:, C_HEADS * LANES:].astype(BF16)


def _c_prep(h, wq, wkv, qag, kvag, qg, kg, cos, sin, seq, tm=512):
    n = h.shape[0]
    ns = seq // tm
    full = lambda shape: pl.BlockSpec(shape, lambda i: (0,) * len(shape))
    return pl.pallas_call(
        _c_prep_kernel,
        grid=(n // tm,),
        in_specs=[pl.BlockSpec((tm, GROUP_W), lambda i: (i, BLK_C_QKV)),
                  pl.BlockSpec((tm, LANES), lambda i: (i, UNIT_KR_IW)),
                  full((Q_LORA, 4 * 256)), full((KV_LORA, 8 * LANES)),
                  full((1, Q_LORA)), full((1, LANES)), full((1, 256)), full((1, 256)),
                  pl.BlockSpec((tm, LANES), lambda i: (i % ns, 0)),
                  pl.BlockSpec((tm, LANES), lambda i: (i % ns, 0))],
        out_specs=[pl.BlockSpec((tm, 4 * 256), lambda i: (i, 0)),
                   pl.BlockSpec((tm, 4 * 256), lambda i: (i, 0)),
                   pl.BlockSpec((tm, GROUP_W), lambda i: (i, 0))],
        out_shape=[jax.ShapeDtypeStruct((n, 4 * 256), BF16), jax.ShapeDtypeStruct((n, 4 * 256), BF16),
                   jax.ShapeDtypeStruct((n, GROUP_W), BF16)],
        compiler_params=_params(1),
    )(h, h, wq, wkv, qag, kvag, qg, kg, cos, sin)


def _c_attn_kernel(q_ref, k_ref, v_ref, z_ref, o_ref, s_ref, mp_ref, lp_ref, acc_ref, *, tq):
    qt = pl.program_id(1)
    scale = C_QK ** -0.5 * LOG2E
    hrows = [slice(h * tq, (h + 1) * tq) for h in range(C_HEADS)]
    qchunk = (qt * tq + lax.broadcasted_iota(jnp.int32, (tq, 256), 0)) // CHUNK
    kcol = lax.broadcasted_iota(jnp.int32, (tq, 256), 1)
    mp_ref[...] = jnp.full(mp_ref.shape, NEG, F32)
    lp_ref[...] = jnp.zeros(lp_ref.shape, F32)
    acc_ref[...] = jnp.zeros(acc_ref.shape, F32)
    nfull = (qt * tq) // 256
    nblk = ((qt + 1) * tq) // 256

    def logits_blocks(kb0, n, masked):
        for i in range(n):
            kb = kb0 + i
            off = pl.multiple_of(kb * 256, 256)
            for h in range(C_HEADS):
                cols = slice(h * 256, (h + 1) * 256)
                s = _dot_t(q_ref[:, cols], k_ref[pl.ds(off, 256), cols]) * scale
                if masked:
                    s = jnp.where(((kb * 256 + kcol) // CHUNK) <= qchunk, s, NEG)
                s_ref[kb, hrows[h], :] = s
                mp_ref[hrows[h], :] = jnp.maximum(mp_ref[hrows[h], :], jnp.maximum(s[:, :LANES], s[:, LANES:]))

    def value_blocks(kb0, n):
        off = pl.multiple_of(kb0 * 256, 256)
        for h in range(C_HEADS):
            m = mp_ref[hrows[h], :]
            mm = jnp.concatenate([m, m], axis=1)
            p = [jnp.exp2(s_ref[kb0 + i, hrows[h], :] - mm) for i in range(n)]
            lsum = p[0][:, :LANES] + p[0][:, LANES:]
            for pi in p[1:]:
                lsum = lsum + pi[:, :LANES] + pi[:, LANES:]
            lp_ref[hrows[h], :] += lsum
            pb = p[0].astype(BF16) if n == 1 else jnp.concatenate([pi.astype(BF16) for pi in p], axis=1)
            acc_ref[hrows[h], :] += jnp.dot(pb, v_ref[pl.ds(off, n * 256), h * LANES:(h + 1) * LANES],
                                            preferred_element_type=F32)

    _for_blocks(lambda kb0, n: logits_blocks(kb0, n, False), nfull)
    lax.fori_loop(nfull, nblk, lambda kb, c: (logits_blocks(kb, 1, True), c)[1], 0)
    for h in range(C_HEADS):
        m = jnp.max(mp_ref[hrows[h], :], axis=-1, keepdims=True)
        mp_ref[hrows[h], :] = jnp.broadcast_to(m, (tq, LANES))
    _for_blocks(value_blocks, nblk)
    gate = _silu(z_ref[...].astype(F32))
    for h in range(C_HEADS):
        cols = slice(h * LANES, (h + 1) * LANES)
        l = jnp.sum(lp_ref[hrows[h], :], axis=-1, keepdims=True)
        o_ref[:, cols] = (acc_ref[hrows[h], :] / l * gate[:, cols]).astype(BF16)


def _c_attn(qc, kc, vc, h, bsz, seq, tq=256):
    nt = seq // tq
    rows = C_HEADS * tq
    kern = functools.partial(_c_attn_kernel, tq=tq)
    return pl.pallas_call(
        kern,
        grid=(bsz, nt),
        in_specs=[pl.BlockSpec((tq, C_HEADS * 256), lambda b, t: (b * nt + t, 0)),
                  pl.BlockSpec((seq, C_HEADS * 256), lambda b, t: (b, 0)),
                  pl.BlockSpec((seq, GROUP_W), lambda b, t: (b, 0)),
                  pl.BlockSpec((tq, GROUP_W), lambda b, t: (b * nt + t, BLK_C_Z))],
        out_specs=pl.BlockSpec((tq, GROUP_W), lambda b, t: (b * nt + t, 0)),
        out_shape=jax.ShapeDtypeStruct((bsz * seq, GROUP_W), BF16),
        scratch_shapes=[pltpu.VMEM((seq // 256, rows, 256), F32), pltpu.VMEM((rows, LANES), F32),
                        pltpu.VMEM((rows, LANES), F32), pltpu.VMEM((rows, LANES), F32)],
        compiler_params=_params(2),
    )(qc, kc, vc, h)


def _mixer_d_kernel(q_ref, k_ref, v_ref, z_ref, qg_ref, kg_ref, base_ref, o_ref,
                    kpad_ref, vpad_ref, bias_ref, s_ref, mp_ref, *, seq, tq):
    b = pl.program_id(0)
    qt = pl.program_id(1)
    win = tq + D_LEFT

    @pl.when((b == 0) & (qt == 0))
    def _():
        qc = lax.broadcasted_iota(jnp.int32, (tq, win), 0) // CHUNK
        kc = lax.broadcasted_iota(jnp.int32, (tq, win), 1) // CHUNK
        band = (kc >= qc) & (kc <= qc + D_LEFT // CHUNK)
        for h in range(N_HEADS64):
            bias_ref[h] = jnp.where(band, _toeplitz(base_ref[h:h + 1, :], tq, win) * LOG2E, NEG)

    @pl.when(qt == 0)
    def _():
        kpad_ref[0:D_LEFT, :] = jnp.zeros((D_LEFT, GROUP_W), BF16)
        vpad_ref[0:D_LEFT, :] = jnp.zeros((D_LEFT, GROUP_W), BF16)
        for r in range(seq // 256):
            rows = slice(r * 256, (r + 1) * 256)
            dst = slice(D_LEFT + r * 256, D_LEFT + (r + 1) * 256)
            tiles = _rms_heads64(k_ref[rows, :].astype(F32), kg_ref[...], 4)
            for t in range(4):
                kpad_ref[dst, t * LANES:(t + 1) * LANES] = tiles[t].astype(BF16)
            vpad_ref[dst, :] = v_ref[rows, :]

    lo = _lo_mask(tq)
    qtiles = _rms_heads64(q_ref[...].astype(F32), qg_ref[...], 4)
    start = pl.multiple_of(qt * tq, tq)
    gate = _silu(z_ref[...].astype(F32))
    ntile = win // LANES

    def in_band(r, c):
        return c * LANES < r * CHUNK + D_LEFT + CHUNK and (c + 1) * LANES > r * CHUNK

    def pipeline(first):
        live = slice(first * LANES, win)
        wstart = pl.multiple_of(start + first * LANES, LANES)

        def logits(h):
            t, half = divmod(h, 2)
            kwin = kpad_ref[pl.ds(wstart, win - first * LANES), t * LANES:(t + 1) * LANES]
            sel = lo if half == 0 else jnp.logical_not(lo)
            qh = jnp.where(sel, qtiles[t] * (0.125 * LOG2E), 0.0).astype(BF16)
            s = _dot_t(qh, kwin) + bias_ref[h, :, live]
            s_ref[h, :, live] = s
            tiles = [s[:, i * LANES:(i + 1) * LANES] for i in range(ntile - first)]
            m = jnp.max(functools.reduce(jnp.maximum, tiles), axis=-1, keepdims=True)
            mp_ref[h] = jnp.broadcast_to(m, (tq, LANES))

        def values(h):
            vwin = vpad_ref[pl.ds(wstart, win - first * LANES), (h // 2) * LANES:(h // 2 + 1) * LANES]
            p_rows, l_rows = [], []
            for r in range(tq // CHUNK):
                rows = slice(r * CHUNK, (r + 1) * CHUNK)
                m = mp_ref[h, rows, :]
                tiles = [jnp.exp2(s_ref[h, rows, c * LANES:(c + 1) * LANES] - m) if in_band(r, c) else None
                         for c in range(first, ntile)]
                l_rows.append(functools.reduce(jnp.add, [t for t in tiles if t is not None]))
                p_rows.append(jnp.concatenate([jnp.zeros((CHUNK, LANES), BF16) if t is None else t.astype(BF16)
                                               for t in tiles], axis=1))
            l = jnp.sum(jnp.concatenate(l_rows, axis=0), axis=-1, keepdims=True)
            return jnp.dot(jnp.concatenate(p_rows, axis=0), vwin, preferred_element_type=F32) / l

        outs = []
        logits(0)
        for h in range(1, N_HEADS64 + 1):
            if h < N_HEADS64:
                logits(h)
            outs.append(values(h - 1))
            if h % 2 == 0:
                cols = slice((h // 2 - 1) * LANES, (h // 2) * LANES)
                o_ref[:, cols] = (jnp.where(lo, outs[h - 2], outs[h - 1]) * gate[:, cols]).astype(BF16)

    n_lead = D_LEFT // tq
    for lead in range(n_lead):
        pl.when(qt == lead)(functools.partial(pipeline, (D_LEFT - lead * tq) // LANES))
    pl.when(qt >= n_lead)(functools.partial(pipeline, 0))


def _mixer_d(h, qg, kg, base, bsz, seq, tq=256):
    nt = seq // tq
    kern = functools.partial(_mixer_d_kernel, seq=seq, tq=tq)
    full = lambda shape: pl.BlockSpec(shape, lambda b, t: (0,) * len(shape))
    return pl.pallas_call(
        kern,
        grid=(bsz, nt),
        in_specs=[pl.BlockSpec((tq, GROUP_W), lambda b, t: (b * nt + t, BLK_D_Q)),
                  pl.BlockSpec((seq, GROUP_W), lambda b, t: (b, BLK_D_K)),
                  pl.BlockSpec((seq, GROUP_W), lambda b, t: (b, BLK_D_V)),
                  pl.BlockSpec((tq, GROUP_W), lambda b, t: (b * nt + t, BLK_D_Z)),
                  full((1, GROUP_W)), full((1, GROUP_W)), full((N_HEADS64, 2 * tq + D_LEFT))],
        out_specs=pl.BlockSpec((tq, GROUP_W), lambda b, t: (b * nt + t, 0)),
        out_shape=jax.ShapeDtypeStruct((bsz * seq, GROUP_W), BF16),
        scratch_shapes=[pltpu.VMEM((seq + D_LEFT, GROUP_W), BF16), pltpu.VMEM((seq + D_LEFT, GROUP_W), BF16),
                        pltpu.VMEM((N_HEADS64, tq, tq + D_LEFT), F32),
                        pltpu.VMEM((N_HEADS64, tq, tq + D_LEFT), F32), pltpu.VMEM((N_HEADS64, tq, LANES), F32)],
        compiler_params=_params(2),
    )(h, h, h, h, qg, kg, base)


def _w_in_pieces(take, zeros):
    c = lambda name, size, off=0: take(_SRC[name] + off, size)
    return [c("a_u", 512), c("a_v", 512), c("a_z", 512),
            c("b_q", 512), c("b_iq", 512), c("b_z", 512),
            c("b_k", 64), c("b_k", 64), c("b_v", 64), c("b_v", 64), c("b_ik", 64), c("b_ik", 64),
            c("c_kr", 32), c("b_iw", 8), zeros(24), c("c_kr", 32, 32), zeros(32),
            c("c_q", 384), c("c_kv", 128), c("c_z", 512),
            c("d_q", 512), c("d_k", 512), c("d_v", 512), c("d_z", 512)]


def _layout_w_in_kernel(w_ref, o_ref):
    tk = w_ref.shape[2]
    pieces = _w_in_pieces(lambda s, n: w_ref[0, s:s + n, :], lambda n: jnp.zeros((n, tk), F32))
    ends = np.cumsum([0] + [p.shape[0] for p in pieces])
    start = 0
    for i in range(1, len(pieces) + 1):
        if ends[i] % GROUP_W == 0:
            group = pieces[start:i]
            blk = group[0] if len(group) == 1 else jnp.concatenate(group, axis=0)
            o_ref[0, ends[start]:ends[i], :] = blk.astype(BF16)
            start = i


def _layout_w_in(w_in, tk=256):
    w_t = jnp.swapaxes(w_in, 1, 2)
    depth, cols, d = w_t.shape
    return pl.pallas_call(
        _layout_w_in_kernel,
        grid=(depth, d // tk),
        in_specs=[pl.BlockSpec((1, cols, tk), lambda l, i: (l, 0, i))],
        out_specs=pl.BlockSpec((1, H_COLS, tk), lambda l, i: (l, 0, i)),
        out_shape=jax.ShapeDtypeStruct((depth, H_COLS, d), BF16),
        compiler_params=_params(2),
    )(w_t)


def _rope_layout(v):
    z = jnp.zeros(v.shape[:-1] + (32,), v.dtype)
    return jnp.concatenate([v[..., :32], z, v[..., 32:], z], axis=-1)


def _layout_c(w_qb, w_kvb, q_gain, k_gain, qa_gain):
    wq = w_qb.reshape(Q_LORA, C_HEADS, C_QK)
    wq = jnp.concatenate([wq[..., :C_NOPE], _rope_layout(wq[..., C_NOPE:])], axis=-1)
    wq = wq.reshape(Q_LORA, C_HEADS * 256).astype(BF16)
    wkv = w_kvb.reshape(KV_LORA, C_HEADS, 2 * LANES)
    wkv = jnp.concatenate([wkv[..., :C_NOPE].reshape(KV_LORA, -1), wkv[..., C_NOPE:].reshape(KV_LORA, -1)],
                          axis=1).astype(BF16)
    lay = lambda g: jnp.concatenate([g[:C_NOPE], _rope_layout(g[C_NOPE:])])[None, :]
    return wq, wkv, lay(q_gain), lay(k_gain), qa_gain[None, :]


def _t5_bucket_static(rel):
    half = T5_BUCKETS // 2
    exact = half // 2
    n = abs(rel)
    if n < exact:
        val = n
    else:
        val = min(exact + (n * n // (exact * exact)).bit_length() - 1, half - 1)
    return (half if rel > 0 else 0) + val


def _t5_tables(t5_bias):
    m = np.arange(512)
    d0 = np.where(m < 256, m, m - 512)
    d1 = np.where(m <= 256, m - 256, m - 768)
    idx0 = np.array([_t5_bucket_static(int(d)) for d in d0], np.int32)
    idx1 = np.array([_t5_bucket_static(int(d)) for d in d1], np.int32)
    far = _t5_bucket_static(-512)
    return t5_bias[idx0].T, t5_bias[idx1].T, t5_bias[far]


def _band_table(rel_bias, tq):
    width = 2 * tq + D_LEFT
    m = np.arange(width)
    dist = np.where(m <= tq + D_LEFT, D_LEFT - m, D_LEFT + width - m)
    idx = np.clip(dist, -REL_CLIP, REL_CLIP) + REL_CLIP
    return rel_bias[idx.astype(np.int32)].T


def _rope_tables(seq):
    inv = ROPE_BASE ** (-jnp.arange(0, C_ROPE, 2, dtype=F32) / C_ROPE)
    ang = jnp.arange(seq, dtype=F32)[:, None] * inv[None, :]
    c, s = jnp.cos(ang), jnp.sin(ang)
    z = jnp.zeros_like(c)
    return jnp.concatenate([c, z, c, z], axis=1), jnp.concatenate([-s, z, s, z], axis=1)


def kernel(x, t5_bias, norm_g, w_in, a_v_gain, a_ws, a_bs, b_q_gain, b_k_gain, c_qa_gain, c_kva_gain,
           c_w_qb, c_w_kvb, c_q_gain, c_k_gain, d_q_gain, d_k_gain, d_rel_bias, w_out):
    bsz, seq, d_model = x.shape
    depth = w_in.shape[0]
    tq = 256
    assert seq % 512 == 0 and seq <= 2048 and d_model % 512 == 0, (seq, d_model)
    assert w_in.shape[2] == _SRC["d_z"] + GROUP_W and w_out.shape[1] == 4 * GROUP_W, (w_in.shape, w_out.shape)
    x2 = x.reshape(bsz * seq, d_model)
    cos, sin = _rope_tables(seq)
    base0, base1, cfar = _t5_tables(t5_bias)
    w_in_blocks = _layout_w_in(w_in)
    for l in range(depth):
        h = _inproj(x2, norm_g[l][None, :], w_in_blocks, l)
        mask = _b_select(h, bsz, seq)
        y_b = _b_attn(h, mask, cfar, jnp.tile(b_q_gain[l], N_HEADS64)[None, :],
                      jnp.tile(b_k_gain[l], 2)[None, :], base0, base1, bsz, seq, tq)
        wq, wkv, qg, kg, qag = _layout_c(c_w_qb[l], c_w_kvb[l], c_q_gain[l], c_k_gain[l], c_qa_gain[l])
        qc, kc, vc = _c_prep(h, wq, wkv, qag, c_kva_gain[l][None, :], qg, kg, cos, sin, seq)
        y_c = _c_attn(qc, kc, vc, h, bsz, seq, tq)
        y_d = _mixer_d(h, jnp.tile(d_q_gain[l], N_HEADS64)[None, :], jnp.tile(d_k_gain[l], N_HEADS64)[None, :],
                       _band_table(d_rel_bias[l], tq), bsz, seq, tq)
        x2 = _outproj(x2, h, a_v_gain[l][None, :], a_ws[l], a_bs[l][:, :, None], (y_b, y_c, y_d),
                      w_out[l].astype(BF16))
    return x2.reshape(bsz, seq, d_model)
```

```python
import functools
import math

import numpy as np
import jax
import jax.numpy as jnp
from jax import lax
from jax.experimental import pallas as pl
from jax.experimental.pallas import tpu as pltpu

F32 = jnp.float32
BF16 = jnp.bfloat16

EPS = 1e-6
NEG = -1e30
LOG2E = math.log2(math.e)
KEY_LOWEST = int(np.float32(-np.finfo(np.float32).max).view(np.int32)) ^ 0x7FFFFFFF
CHUNK = 64
LANES = 128
GROUP_W = 512
A_GROUPS = 4
GMLP_BLOCK = 128
N_HEADS64 = 8
IDX_SCALE = (8 ** -0.5) * 0.125
TOPK_MAX = 256
T5_BUCKETS = 32
C_HEADS = 4
C_NOPE = 128
C_ROPE = 64
C_QK = 192
Q_LORA = 384
KV_LORA = 128
ROPE_BASE = 10000.0
D_LEFT = 8 * CHUNK
REL_CLIP = 128
VMEM_LIMIT = 56 * 1024 * 1024

BLK_A_U, BLK_A_V, BLK_A_Z = 0, 1, 2
BLK_B_Q, BLK_B_IQ, BLK_B_Z, BLK_SMALL = 3, 4, 5, 6
BLK_C_QKV, BLK_C_Z = 7, 8
BLK_D_Q, BLK_D_K, BLK_D_V, BLK_D_Z = 9, 10, 11, 12
H_COLS = 13 * GROUP_W
UNIT_B_K, UNIT_B_V, UNIT_B_IK, UNIT_KR_IW = (BLK_SMALL * 4 + i for i in range(4))
IW_LANE = 32

_SRC = dict(a_u=0, a_v=512, a_z=1024, b_q=1536, b_k=2048, b_v=2112, b_iq=2176, b_ik=2688,
            b_iw=2752, b_z=2760, c_q=3272, c_kv=3656, c_kr=3784, c_z=3848,
            d_q=4360, d_k=4872, d_v=5384, d_z=5896)


def _params(n_axes):
    return pltpu.CompilerParams(dimension_semantics=("arbitrary",) * n_axes,
                                vmem_limit_bytes=VMEM_LIMIT)


def _gelu(x):
    c = math.sqrt(2.0 / math.pi)
    return x * (0.5 * (1.0 + jnp.tanh(c * (x + 0.044715 * (x * x * x)))))


def _silu(x):
    return x * (1.0 / (1.0 + jnp.exp(-x)))


def _dot_t(a, b):
    return lax.dot_general(a, b, (((1,), (1,)), ((), ())), preferred_element_type=F32)


def _lo_mask(rows):
    return lax.broadcasted_iota(jnp.int32, (rows, LANES), 1) < 64


def _rms_heads64(x, gain, ntiles):
    lo = _lo_mask(x.shape[0])
    tiles = []
    for t in range(ntiles):
        xt = x[:, t * LANES:(t + 1) * LANES]
        sq = xt * xt
        s_lo = jnp.sum(jnp.where(lo, sq, 0.0), axis=-1, keepdims=True)
        s_hi = jnp.sum(jnp.where(lo, 0.0, sq), axis=-1, keepdims=True)
        r = jnp.where(lo, lax.rsqrt(s_lo * (1.0 / 64) + EPS), lax.rsqrt(s_hi * (1.0 / 64) + EPS))
        tiles.append(xt * r * gain[:, t * LANES:(t + 1) * LANES])
    return tiles


def _for_blocks(blocks, count):
    def quad(i, carry):
        blocks(4 * i, 4)
        return carry

    lax.fori_loop(0, count // 4, quad, 0)
    first = (count // 4) * 4

    @pl.when((count & 2) != 0)
    def _():
        blocks(first, 2)

    @pl.when((count & 1) != 0)
    def _():
        blocks(first + (count & 2), 1)


def _toeplitz(base_row, rows, width):
    t = jnp.broadcast_to(base_row, (rows, base_row.shape[1]))
    t = pltpu.roll(t, 0, 1, stride=1, stride_axis=0)
    return t[:, :width]


def _inproj_kernel(x_ref, g_ref, w_ref, o_ref):
    x = x_ref[...]
    ms = jnp.mean(x * x, axis=-1, keepdims=True)
    xn = (x * lax.rsqrt(ms + EPS) * g_ref[...]).astype(BF16)
    for c in range(H_COLS // GROUP_W):
        cols = slice(c * GROUP_W, (c + 1) * GROUP_W)
        o_ref[:, cols] = _dot_t(xn, w_ref[cols, :]).astype(BF16)


def _inproj(x2, g, w_all, layer, tm=512):
    n, d = x2.shape
    return pl.pallas_call(
        _inproj_kernel,
        grid=(n // tm,),
        in_specs=[pl.BlockSpec((tm, d), lambda i: (i, 0)),
                  pl.BlockSpec((1, d), lambda i: (0, 0)),
                  pl.BlockSpec((None, H_COLS, d), lambda i: (layer, 0, 0), pipeline_mode=pl.Buffered(1))],
        out_specs=pl.BlockSpec((tm, H_COLS), lambda i: (i, 0)),
        out_shape=jax.ShapeDtypeStruct((n, H_COLS), BF16),
        compiler_params=_params(1),
    )(x2, g, w_all)


def _outproj_kernel(x_ref, uvz_ref, vg_ref, ws_ref, bs_ref, yb_ref, yc_ref, yd_ref, w_ref, o_ref, ya_ref):
    u_ref, v_ref, z_ref = (uvz_ref.at[:, blk * GROUP_W:(blk + 1) * GROUP_W] for blk in (BLK_A_U, BLK_A_V, BLK_A_Z))
    tm, d = x_ref.shape
    wgs = _mixer_a_weights(ws_ref)
    nblk = tm // GMLP_BLOCK
    ncol = d // nblk
    for c in range(nblk):
        cols = slice(c * ncol, (c + 1) * ncol)
        acc = x_ref[:, cols]
        for g, y_ref in ((1, yb_ref), (2, yc_ref), (3, yd_ref)):
            acc = acc + jnp.dot(y_ref[...], w_ref[g * GROUP_W:(g + 1) * GROUP_W, cols],
                                preferred_element_type=F32)
        o_ref[:, cols] = acc
        _mixer_a_rows(u_ref, v_ref, z_ref, vg_ref, wgs, bs_ref, ya_ref, c)
    for c in range(nblk):
        cols = slice(c * ncol, (c + 1) * ncol)
        o_ref[:, cols] += jnp.dot(ya_ref[...], w_ref[0:GROUP_W, cols], preferred_element_type=F32)


def _outproj(x2, h, vg, ws, bs, ys, w, tm=512):
    n, d = x2.shape
    yspec = pl.BlockSpec((tm, GROUP_W), lambda i: (i, 0))
    return pl.pallas_call(
        _outproj_kernel,
        grid=(n // tm,),
        in_specs=[pl.BlockSpec((tm, d), lambda i: (i, 0)),
                  pl.BlockSpec((tm, 3 * GROUP_W), lambda i: (i, 0)),
                  pl.BlockSpec((1, GROUP_W), lambda i: (0, 0)),
                  pl.BlockSpec((A_GROUPS, GMLP_BLOCK, GMLP_BLOCK), lambda i: (0, 0, 0)),
                  pl.BlockSpec((A_GROUPS, GMLP_BLOCK, 1), lambda i: (0, 0, 0)),
                  yspec, yspec, yspec,
                  pl.BlockSpec((4 * GROUP_W, d), lambda i: (0, 0))],
        out_specs=pl.BlockSpec((tm, d), lambda i: (i, 0)),
        out_shape=jax.ShapeDtypeStruct((n, d), F32),
        scratch_shapes=[pltpu.VMEM((tm, GROUP_W), BF16)],
        compiler_params=_params(1),
    )(x2, h, vg, ws, bs, *ys, w)


def _mixer_a_weights(w_ref):
    i = lax.broadcasted_iota(jnp.int32, (GMLP_BLOCK, GMLP_BLOCK), 0)
    j = lax.broadcasted_iota(jnp.int32, (GMLP_BLOCK, GMLP_BLOCK), 1)
    keep = (j // CHUNK) <= (i // CHUNK)
    return [jnp.where(keep, w_ref[g], 0.0).astype(BF16) for g in range(A_GROUPS)]


def _mixer_a_rows(u_ref, v_ref, z_ref, vg_ref, wgs, b_ref, o_ref, blk):
    rows = slice(blk * GMLP_BLOCK, (blk + 1) * GMLP_BLOCK)
    u = _gelu(u_ref[rows, :].astype(F32))
    v = _gelu(v_ref[rows, :].astype(F32))
    ms = jnp.mean(v * v, axis=-1, keepdims=True)
    vb = (v * lax.rsqrt(ms + EPS) * vg_ref[...]).astype(BF16)
    gate = _silu(z_ref[rows, :].astype(F32))
    for g in range(A_GROUPS):
        cols = slice(g * LANES, (g + 1) * LANES)
        sg = jnp.dot(wgs[g], vb[:, cols], preferred_element_type=F32) + b_ref[g]
        o_ref[rows, cols] = (u[:, cols] * sg * gate[:, cols]).astype(BF16)


def _order_key(x):
    return jnp.where(x < 0, x ^ 0x7FFFFFFF, x)


def _b_select_kernel(iq_ref, iw_ref, ik_ref, o_ref, lhs_ref, sc_ref, *, seq, tq, topk):
    t_blk = pl.program_id(1)
    nkb = seq // 256
    nblk = ((t_blk + 1) * tq) // 256
    n_interp = 12
    n_unchecked = 13 + 2 * t_blk
    hrows = [slice(h * tq, (h + 1) * tq) for h in range(N_HEADS64)]

    lo_half = _lo_mask(tq)
    w_t = (iw_ref[...].astype(F32) * IDX_SCALE).T
    for h in range(N_HEADS64):
        iqt = iq_ref[:, (h // 2) * LANES:(h // 2 + 1) * LANES]
        sel = lo_half if h % 2 == 0 else jnp.logical_not(lo_half)
        lhs_ref[hrows[h], :] = jnp.where(sel, iqt, jnp.zeros_like(iqt))

    def qpos(ln):
        return t_blk * tq + ln.start + lax.broadcasted_iota(jnp.int32, (256, ln.stop - ln.start), 1)

    def krow(ln):
        return lax.broadcasted_iota(jnp.int32, (256, ln.stop - ln.start), 0)

    assert tq in (256, 512)
    every = slice(0, tq)
    tail = slice(256, tq) if tq == 512 else None
    nwide = nblk - 1 if tail else nblk

    def fold8(x):
        return jnp.sum(x.reshape(256 // 8, 8, x.shape[1]), axis=0)

    def widen(part, ln):
        return part if ln == every else jnp.concatenate([jnp.zeros((8, ln.start), F32), part], axis=1)

    def score_block(kb, carry, masked, ln):
        amax, n_pos, n_nn = carry
        off = pl.multiple_of(kb * 256, 256)
        ikblk = ik_ref[pl.ds(off, 256), :]
        score = jnp.zeros((256, ln.stop - ln.start), F32)
        for h in range(N_HEADS64):
            w_h = w_t[IW_LANE + h:IW_LANE + h + 1, ln]
            q_h = lhs_ref[h * tq + ln.start:h * tq + ln.stop, :]
            score = score + w_h * jnp.maximum(_dot_t(ikblk, q_h), 0.0)
        mag = jnp.abs(score)
        if masked:
            adm = ((kb * 256 + krow(ln)) // CHUNK) <= (qpos(ln) // CHUNK)
            score = jnp.where(adm, score, -jnp.inf)
            mag = jnp.where(adm, mag, 0.0)
        sc_ref[kb, :, ln] = score
        return (jnp.maximum(amax, widen(jnp.max(mag.reshape(256 // 8, 8, mag.shape[1]), axis=0), ln)),
                n_pos + widen(fold8(jnp.where(score > 0.0, 1.0, 0.0)), ln),
                n_nn + widen(fold8(jnp.where(score >= 0.0, 1.0, 0.0)), ln))

    nfull = (t_blk * tq) // 256
    zeros8 = jnp.zeros((8, tq), F32)
    carry = lax.fori_loop(0, nfull, lambda kb, c: score_block(kb, c, False, every), (zeros8, zeros8, zeros8))
    carry = lax.fori_loop(nfull, nwide, lambda kb, c: score_block(kb, c, True, every), carry)
    if tail:
        carry = score_block(nblk - 1, carry, True, tail)
    amax = jnp.max(carry[0], axis=0, keepdims=True)
    f_pos = jnp.sum(carry[1], axis=0, keepdims=True)
    f_nn = jnp.sum(carry[2], axis=0, keepdims=True)

    def count(pred):
        def body(kb, acc):
            return acc + fold8(jnp.where(pred(sc_ref[kb], kb, every), 1.0, 0.0))
        acc = lax.fori_loop(0, nwide, body, zeros8)
        if tail:
            acc = acc + widen(fold8(jnp.where(pred(sc_ref[nblk - 1, :, tail], nblk - 1, tail), 1.0, 0.0)), tail)
        return jnp.sum(acc, axis=0, keepdims=True)

    kf = float(topk)
    qrow = t_blk * tq + lax.broadcasted_iota(jnp.int32, (1, tq), 1)
    n_adm = ((qrow // CHUNK + 1) * CHUNK).astype(F32)
    one = jnp.ones((1, tq), jnp.int32)
    pos = f_pos > kf
    neg = f_nn < kf
    lo0 = jnp.where(pos, one, _order_key(lax.bitcast_convert_type(-amax, jnp.int32)))
    hi0 = jnp.where(neg, one - 1, _order_key(lax.bitcast_convert_type(amax, jnp.int32)) + 1)
    w_lo0 = jnp.where(pos, f_pos, n_adm) - kf
    w_hi0 = kf - jnp.where(neg, f_nn, 0.0)
    all_sel = n_adm <= kf
    at_zero = jnp.logical_not(pos | neg)
    done0 = jnp.where(all_sel | at_zero | (hi0 == lo0 + 1), 1.0, 0.0)
    thr0 = jnp.where(all_sel, KEY_LOWEST, jnp.where(at_zero, jnp.where(f_pos == kf, one, one - 1), lo0))

    def as_score(key):
        return lax.bitcast_convert_type(_order_key(key), F32)

    def search_cond(st):
        return jnp.logical_and(st[0][0] < n_interp + 32, st[1] < 0.5)

    def search_step(st):
        it, lo, hi, w_lo, w_hi, side, done, thr = st
        lo_v = as_score(lo)
        hi_v = as_score(hi)
        c_v = lo_v + (hi_v - lo_v) * (w_lo / (w_lo + w_hi))
        c_interp = _order_key(lax.bitcast_convert_type(c_v, jnp.int32))
        c_mid = (lo >> 1) + (hi >> 1) + (lo & hi & 1)
        cand = jnp.where(it < n_interp, c_interp, c_mid)
        cand = jnp.minimum(jnp.maximum(cand, lo + 1), hi - 1)
        cand_v = as_score(cand)
        f = count(lambda s, kb, ln: s >= cand_v[:, ln])
        live = done < 0.5
        up = f > kf
        hit = f == kf
        new_lo = jnp.where(live & up, cand, lo)
        new_hi = jnp.where(live & jnp.logical_not(up), cand, hi)
        new_w_lo = jnp.where(up, f - kf, jnp.where(side < 0.0, 0.5 * w_lo, w_lo))
        new_w_hi = jnp.where(up, jnp.where(side > 0.0, 0.5 * w_hi, w_hi), kf - f)
        new_side = jnp.where(up, 1.0, -1.0)
        new_thr = jnp.where(live, jnp.where(hit, cand, new_lo), thr)
        new_done = jnp.where(live & (hit | (new_hi == new_lo + 1)), 1.0, done)
        return (it + 1, new_lo, new_hi, jnp.where(live, new_w_lo, w_lo),
                jnp.where(live, new_w_hi, w_hi), jnp.where(live, new_side, side), new_done, new_thr)

    def checked_step(st):
        new = search_step(st[0])
        return new, jnp.min(new[-2])

    state = (jnp.int32(0), lo0, hi0, w_lo0, w_hi0, jnp.zeros((1, tq), F32), done0, thr0)
    state = lax.fori_loop(0, n_unchecked, lambda i, st: search_step(st), state)
    thr = as_score(lax.while_loop(search_cond, checked_step, (state, jnp.min(state[-2])))[0][-1])

    any_excess = jnp.max(count(lambda s, kb, ln: s >= thr[:, ln])) > kf

    def store_mask(kb, keep_t, ln):
        for g in range(tq // 256):
            if g * 256 >= ln.start:
                o_ref[g, kb] = keep_t[:, g * 256 - ln.start:(g + 1) * 256 - ln.start]
            else:
                o_ref[g, kb] = jnp.zeros((256, 256), F32)

    def write_mask(keep):
        def body(kb, carry):
            store_mask(kb, jnp.where(keep(sc_ref[kb], kb, every), 1.0, 0.0), every)
            return carry

        lax.fori_loop(0, nwide, body, 0)
        if tail:
            store_mask(nblk - 1, jnp.where(keep(sc_ref[nblk - 1, :, tail], nblk - 1, tail), 1.0, 0.0), tail)

    def write_unused(kb, carry):
        store_mask(kb, jnp.zeros((256, tq), F32), every)
        return carry

    lax.fori_loop(nblk, nkb, write_unused, 0)

    @pl.when(jnp.logical_not(any_excess))
    def _():
        write_mask(lambda s, kb, ln: s >= thr[:, ln])

    @pl.when(any_excess)
    def _():
        need = kf - count(lambda s, kb, ln: s > thr[:, ln])

        def idx_step(it, jmax):
            cand = jmax | lax.shift_left(jnp.int32(1), 10 - it)
            below = count(lambda s, kb, ln: (s == thr[:, ln]) & ((kb * 256 + krow(ln)) < cand[:, ln]))
            return jnp.where(below < need, cand, jmax)

        jmax = lax.fori_loop(0, 11, idx_step, jnp.zeros((1, tq), jnp.int32))
        write_mask(lambda s, kb, ln: (s > thr[:, ln]) | ((s == thr[:, ln]) & ((kb * 256 + krow(ln)) <= jmax[:, ln])))


def _b_select(h, bsz, seq, tq=512):
    tq = min(tq, seq)
    nt = seq // tq
    topk = min(TOPK_MAX, seq // 4)
    kern = functools.partial(_b_select_kernel, seq=seq, tq=tq, topk=topk)
    return pl.pallas_call(
        kern,
        grid=(bsz, nt),
        in_specs=[pl.BlockSpec((tq, GROUP_W), lambda b, t: (b * nt + t, BLK_B_IQ)),
                  pl.BlockSpec((tq, LANES), lambda b, t: (b * nt + t, UNIT_KR_IW)),
                  pl.BlockSpec((seq, LANES), lambda b, t: (b, UNIT_B_IK))],
        out_specs=pl.BlockSpec((tq // 256, seq // 256, 256, 256), lambda b, t: (b * nt + t, 0, 0, 0)),
        out_shape=jax.ShapeDtypeStruct((bsz * seq // 256, seq // 256, 256, 256), F32),
        scratch_shapes=[pltpu.VMEM((N_HEADS64 * tq, LANES), BF16),
                        pltpu.VMEM((seq // 256, 256, tq), F32)],
        compiler_params=_params(2),
    )(h, h, h)


def _b_attn_kernel(cfar_ref, q_ref, z_ref, k_ref, v_ref, msk_ref, qg_ref, kg_ref, base0_ref, base1_ref,
                   o_ref, kn_ref, v1_ref, bias_ref, qall_ref, s_ref, mp_ref, acc_ref, *, seq, tq):
    b = pl.program_id(0)
    t_blk = pl.program_id(1)
    hrows = [slice(h * tq, (h + 1) * tq) for h in range(N_HEADS64)]

    @pl.when((b == 0) & (t_blk == 0))
    def _():
        for h in range(N_HEADS64):
            bias_ref[0, hrows[h], :] = jnp.full((tq, 256), cfar_ref[h] * LOG2E, F32)
            bias_ref[1, hrows[h], :] = _toeplitz(base1_ref[h:h + 1, :], tq, 256) * LOG2E
            bias_ref[2, hrows[h], :] = _toeplitz(base0_ref[h:h + 1, :], tq, 256) * LOG2E

    @pl.when(t_blk == 0)
    def _():
        lo256 = _lo_mask(256)
        for r in range(seq // 256):
            rows = slice(r * 256, (r + 1) * 256)
            k = k_ref[rows, :].astype(F32)
            ms = jnp.mean(k * k, axis=-1, keepdims=True)
            kn_ref[rows, :] = (k * lax.rsqrt(ms + EPS) * kg_ref[...]).astype(BF16)
            v = v_ref[rows, :]
            v1_ref[rows, :] = jnp.where(lo256, v, jnp.ones_like(v))

    lo = _lo_mask(tq)
    qtiles = _rms_heads64(q_ref[...].astype(F32), qg_ref[...], 4)
    for h in range(N_HEADS64):
        sel = lo if h % 2 == 0 else jnp.logical_not(lo)
        qall_ref[hrows[h], :] = jnp.where(sel, qtiles[h // 2] * (0.125 * LOG2E), 0.0).astype(BF16)
    mp_ref[...] = jnp.full(mp_ref.shape, NEG, F32)
    acc_ref[...] = jnp.zeros(acc_ref.shape, F32)
    nblk = t_blk + 1

    def logits_blocks(kb0, n):
        for i in range(n):
            kb = kb0 + i
            off = pl.multiple_of(kb * 256, 256)
            kblk = kn_ref[pl.ds(off, 256), :]
            which = jnp.clip(kb - (t_blk - 2), 0, 2)
            keep = msk_ref[0, kb].T > 0.5
            for h in range(N_HEADS64):
                s = jnp.where(keep, _dot_t(qall_ref[hrows[h], :], kblk) + bias_ref[which, hrows[h], :], NEG)
                s_ref[kb, hrows[h], :] = s
                mp_ref[hrows[h], :] = jnp.maximum(mp_ref[hrows[h], :], jnp.maximum(s[:, :LANES], s[:, LANES:]))

    def value_blocks(kb0, n):
        off = pl.multiple_of(kb0 * 256, 256)
        v1 = v1_ref[pl.ds(off, n * 256), :]
        for h in range(N_HEADS64):
            m = mp_ref[hrows[h], :]
            mm = jnp.concatenate([m, m], axis=1)
            p = [jnp.exp2(s_ref[kb0 + i, hrows[h], :] - mm).astype(BF16) for i in range(n)]
            p = p[0] if n == 1 else jnp.concatenate(p, axis=1)
            acc_ref[hrows[h], :] += jnp.dot(p, v1, preferred_element_type=F32)

    _for_blocks(logits_blocks, nblk)
    for h in range(N_HEADS64):
        m = jnp.max(mp_ref[hrows[h], :], axis=-1, keepdims=True)
        mp_ref[hrows[h], :] = jnp.broadcast_to(m, (tq, LANES))
    _for_blocks(value_blocks, nblk)

    gate = _silu(z_ref[...].astype(F32))
    for t in range(4):
        a_even = acc_ref[hrows[2 * t], :]
        a_odd = acc_ref[hrows[2 * t + 1], :]
        o_even = a_even / pltpu.roll(a_even, 64, 1)
        o_odd = pltpu.roll(a_odd, 64, 1) / a_odd
        cols = slice(t * LANES, (t + 1) * LANES)
        o_ref[:, cols] = (jnp.where(lo, o_even, o_odd) * gate[:, cols]).astype(BF16)


def _b_attn(h, mask, cfar, qg, kg, base0, base1, bsz, seq, tq=256):
    nt = seq // tq
    rows = N_HEADS64 * tq
    kern = functools.partial(_b_attn_kernel, seq=seq, tq=tq)
    full = lambda shape: pl.BlockSpec(shape, lambda b, t: (0,) * len(shape))
    return pl.pallas_call(
        kern,
        grid=(bsz, nt),
        in_specs=[pl.BlockSpec(memory_space=pltpu.SMEM),
                  pl.BlockSpec((tq, GROUP_W), lambda b, t: (b * nt + t, BLK_B_Q)),
                  pl.BlockSpec((tq, GROUP_W), lambda b, t: (b * nt + t, BLK_B_Z)),
                  pl.BlockSpec((seq, LANES), lambda b, t: (b, UNIT_B_K)),
                  pl.BlockSpec((seq, LANES), lambda b, t: (b, UNIT_B_V)),
                  pl.BlockSpec((1, seq // 256, tq, 256), lambda b, t: (b * nt + t, 0, 0, 0)),
                  full((1, GROUP_W)), full((1, LANES)), full((N_HEADS64, 512)), full((N_HEADS64, 512))],
        out_specs=pl.BlockSpec((tq, GROUP_W), lambda b, t: (b * nt + t, 0)),
        out_shape=jax.ShapeDtypeStruct((bsz * seq, GROUP_W), BF16),
        scratch_shapes=[pltpu.VMEM((seq, LANES), BF16), pltpu.VMEM((seq, LANES), BF16),
                        pltpu.VMEM((3, rows, 256), F32), pltpu.VMEM((rows, LANES), BF16),
                        pltpu.VMEM((seq // 256, rows, 256), F32),
                        pltpu.VMEM((rows, LANES), F32), pltpu.VMEM((rows, LANES), F32)],
        compiler_params=_params(2),
    )(cfar, h, h, h, h, mask, qg, kg, base0, base1)


def _rope(tile, cos, sin):
    return tile * cos + pltpu.roll(tile, 64, 1) * sin


def _c_prep_kernel(lat_ref, kr_ref, wq_ref, wkv_ref, qag_ref, kvag_ref, qg_ref, kg_ref, cos_ref, sin_ref,
                   qo_ref, ko_ref, vo_ref):
    cq = lat_ref[:, :Q_LORA].astype(F32)
    ms = jnp.mean(cq * cq, axis=-1, keepdims=True)
    cqn = (cq * lax.rsqrt(ms + EPS) * qag_ref[...]).astype(BF16)
    qpre = jnp.dot(cqn, wq_ref[...], preferred_element_type=F32)
    ckv = lat_ref[:, Q_LORA:].astype(F32)
    ms = jnp.mean(ckv * ckv, axis=-1, keepdims=True)
    ckvn = (ckv * lax.rsqrt(ms + EPS) * kvag_ref[...]).astype(BF16)
    kvpre = jnp.dot(ckvn, wkv_ref[...], preferred_element_type=F32)
    lane = lax.broadcasted_iota(jnp.int32, kr_ref.shape, 1)
    kr = jnp.where((lane % 64) < 32, kr_ref[...].astype(F32), 0.0)
    kr_ss = jnp.sum(kr * kr, axis=-1, keepdims=True)
    cos = cos_ref[...]
    sin = sin_ref[...]
    qg = qg_ref[...]
    kg = kg_ref[...]
    kr_rot = _rope(kr * kg[:, LANES:], cos, sin)
    for h in range(C_HEADS):
        qh = qpre[:, h * 256:(h + 1) * 256]
        r = lax.rsqrt(jnp.sum(qh * qh, axis=-1, keepdims=True) * (1.0 / C_QK) + EPS)
        qn = qh * r * qg
        qo_ref[:, h * 256:h * 256 + LANES] = qn[:, :LANES].astype(BF16)
        qo_ref[:, h * 256 + LANES:(h + 1) * 256] = _rope(qn[:, LANES:], cos, sin).astype(BF16)
        kn = kvpre[:, h * LANES:(h + 1) * LANES]
        r = lax.rsqrt((jnp.sum(kn * kn, axis=-1, keepdims=True) + kr_ss) * (1.0 / C_QK) + EPS)
        ko_ref[:, h * 256:h * 256 + LANES] = (kn * r * kg[:, :LANES]).astype(BF16)
        ko_ref[:, h * 256 + LANES:(h + 1) * 256] = (kr_rot * r).astype(BF16)
    vo_ref[...] = kvpre[:, C_HEADS * LANES:].astype(BF16)


def _c_prep(h, wq, wkv, qag, kvag, qg, kg, cos, sin, seq, tm=1024):
    n = h.shape[0]
    tm = min(tm, seq)
    ns = seq // tm
    full = lambda shape: pl.BlockSpec(shape, lambda i: (0,) * len(shape))
    return pl.pallas_call(
        _c_prep_kernel,
        grid=(n // tm,),
        in_specs=[pl.BlockSpec((tm, GROUP_W), lambda i: (i, BLK_C_QKV)),
                  pl.BlockSpec((tm, LANES), lambda i: (i, UNIT_KR_IW)),
                  full((Q_LORA, 4 * 256)), full((KV_LORA, 8 * LANES)),
                  full((1, Q_LORA)), full((1, LANES)), full((1, 256)), full((1, 256)),
                  pl.BlockSpec((tm, LANES), lambda i: (i % ns, 0)),
                  pl.BlockSpec((tm, LANES), lambda i: (i % ns, 0))],
        out_specs=[pl.BlockSpec((tm, 4 * 256), lambda i: (i, 0)),
                   pl.BlockSpec((tm, 4 * 256), lambda i: (i, 0)),
                   pl.BlockSpec((tm, GROUP_W), lambda i: (i, 0))],
        out_shape=[jax.ShapeDtypeStruct((n, 4 * 256), BF16), jax.ShapeDtypeStruct((n, 4 * 256), BF16),
                   jax.ShapeDtypeStruct((n, GROUP_W), BF16)],
        compiler_params=_params(1),
    )(h, h, wq, wkv, qag, kvag, qg, kg, cos, sin)


def _c_attn_kernel(q_ref, k_ref, v_ref, z_ref, o_ref, s_ref, mp_ref, lp_ref, acc_ref, *, tq):
    qt = pl.program_id(1)
    scale = C_QK ** -0.5 * LOG2E
    hrows = [slice(h * tq, (h + 1) * tq) for h in range(C_HEADS)]
    qchunk = (qt * tq + lax.broadcasted_iota(jnp.int32, (tq, 256), 0)) // CHUNK
    kcol = lax.broadcasted_iota(jnp.int32, (tq, 256), 1)
    mp_ref[...] = jnp.full(mp_ref.shape, NEG, F32)
    lp_ref[...] = jnp.zeros(lp_ref.shape, F32)
    acc_ref[...] = jnp.zeros(acc_ref.shape, F32)
    nfull = (qt * tq) // 256
    nblk = ((qt + 1) * tq) // 256

    def logits_blocks(kb0, n, masked):
        for i in range(n):
            kb = kb0 + i
            off = pl.multiple_of(kb * 256, 256)
            for h in range(C_HEADS):
                cols = slice(h * 256, (h + 1) * 256)
                s = _dot_t(q_ref[:, cols], k_ref[pl.ds(off, 256), cols]) * scale
                if masked:
                    s = jnp.where(((kb * 256 + kcol) // CHUNK) <= qchunk, s, NEG)
                s_ref[kb, hrows[h], :] = s
                mp_ref[hrows[h], :] = jnp.maximum(mp_ref[hrows[h], :], jnp.maximum(s[:, :LANES], s[:, LANES:]))

    def value_blocks(kb0, n):
        off = pl.multiple_of(kb0 * 256, 256)
        for h in range(C_HEADS):
            m = mp_ref[hrows[h], :]
            mm = jnp.concatenate([m, m], axis=1)
            p = [jnp.exp2(s_ref[kb0 + i, hrows[h], :] - mm) for i in range(n)]
            lsum = p[0][:, :LANES] + p[0][:, LANES:]
            for pi in p[1:]:
                lsum = lsum + pi[:, :LANES] + pi[:, LANES:]
            lp_ref[hrows[h], :] += lsum
            pb = p[0].astype(BF16) if n == 1 else jnp.concatenate([pi.astype(BF16) for pi in p], axis=1)
            acc_ref[hrows[h], :] += jnp.dot(pb, v_ref[pl.ds(off, n * 256), h * LANES:(h + 1) * LANES],
                                            preferred_element_type=F32)

    _for_blocks(lambda kb0, n: logits_blocks(kb0, n, False), nfull)
    lax.fori_loop(nfull, nblk, lambda kb, c: (logits_blocks(kb, 1, True), c)[1], 0)
    for h in range(C_HEADS):
        m = jnp.max(mp_ref[hrows[h], :], axis=-1, keepdims=True)
        mp_ref[hrows[h], :] = jnp.broadcast_to(m, (tq, LANES))
    _for_blocks(value_blocks, nblk)
    gate = _silu(z_ref[...].astype(F32))
    for h in range(C_HEADS):
        cols = slice(h * LANES, (h + 1) * LANES)
        l = jnp.sum(lp_ref[hrows[h], :], axis=-1, keepdims=True)
        o_ref[:, cols] = (acc_ref[hrows[h], :] / l * gate[:, cols]).astype(BF16)


def _c_attn(qc, kc, vc, h, bsz, seq, tq=256):
    nt = seq // tq
    rows = C_HEADS * tq
    kern = functools.partial(_c_attn_kernel, tq=tq)
    return pl.pallas_call(
        kern,
        grid=(bsz, nt),
        in_specs=[pl.BlockSpec((tq, C_HEADS * 256), lambda b, t: (b * nt + t, 0)),
                  pl.BlockSpec((seq, C_HEADS * 256), lambda b, t: (b, 0)),
                  pl.BlockSpec((seq, GROUP_W), lambda b, t: (b, 0)),
                  pl.BlockSpec((tq, GROUP_W), lambda b, t: (b * nt + t, BLK_C_Z))],
        out_specs=pl.BlockSpec((tq, GROUP_W), lambda b, t: (b * nt + t, 0)),
        out_shape=jax.ShapeDtypeStruct((bsz * seq, GROUP_W), BF16),
        scratch_shapes=[pltpu.VMEM((seq // 256, rows, 256), F32), pltpu.VMEM((rows, LANES), F32),
                        pltpu.VMEM((rows, LANES), F32), pltpu.VMEM((rows, LANES), F32)],
        compiler_params=_params(2),
    )(qc, kc, vc, h)


def _mixer_d_kernel(q_ref, k_ref, v_ref, z_ref, qg_ref, kg_ref, base_ref, o_ref,
                    kpad_ref, vpad_ref, bias_ref, s_ref, mp_ref, *, seq, tq):
    b = pl.program_id(0)
    qt = pl.program_id(1)
    win = tq + D_LEFT

    @pl.when((b == 0) & (qt == 0))
    def _():
        qc = lax.broadcasted_iota(jnp.int32, (tq, win), 0) // CHUNK
        kc = lax.broadcasted_iota(jnp.int32, (tq, win), 1) // CHUNK
        band = (kc >= qc) & (kc <= qc + D_LEFT // CHUNK)
        for h in range(N_HEADS64):
            bias_ref[h] = jnp.where(band, _toeplitz(base_ref[h:h + 1, :], tq, win) * LOG2E, NEG)

    @pl.when(qt == 0)
    def _():
        kpad_ref[0:D_LEFT, :] = jnp.zeros((D_LEFT, GROUP_W), BF16)
        vpad_ref[0:D_LEFT, :] = jnp.zeros((D_LEFT, GROUP_W), BF16)
        for r in range(seq // 256):
            rows = slice(r * 256, (r + 1) * 256)
            dst = slice(D_LEFT + r * 256, D_LEFT + (r + 1) * 256)
            tiles = _rms_heads64(k_ref[rows, :].astype(F32), kg_ref[...], 4)
            for t in range(4):
                kpad_ref[dst, t * LANES:(t + 1) * LANES] = tiles[t].astype(BF16)
            vpad_ref[dst, :] = v_ref[rows, :]

    lo = _lo_mask(tq)
    qtiles = _rms_heads64(q_ref[...].astype(F32), qg_ref[...], 4)
    start = pl.multiple_of(qt * tq, tq)
    gate = _silu(z_ref[...].astype(F32))
    ntile = win // LANES

    def in_band(r, c):
        return c * LANES < r * CHUNK + D_LEFT + CHUNK and (c + 1) * LANES > r * CHUNK

    def pipeline(first):
        live = slice(first * LANES, win)
        wstart = pl.multiple_of(start + first * LANES, LANES)

        def logits(h):
            t, half = divmod(h, 2)
            kwin = kpad_ref[pl.ds(wstart, win - first * LANES), t * LANES:(t + 1) * LANES]
            sel = lo if half == 0 else jnp.logical_not(lo)
            qh = jnp.where(sel, qtiles[t] * (0.125 * LOG2E), 0.0).astype(BF16)
            s = _dot_t(qh, kwin) + bias_ref[h, :, live]
            s_ref[h, :, live] = s
            tiles = [s[:, i * LANES:(i + 1) * LANES] for i in range(ntile - first)]
            m = jnp.max(functools.reduce(jnp.maximum, tiles), axis=-1, keepdims=True)
            mp_ref[h] = jnp.broadcast_to(m, (tq, LANES))

        def values(h):
            vwin = vpad_ref[pl.ds(wstart, win - first * LANES), (h // 2) * LANES:(h // 2 + 1) * LANES]
            p_rows, l_rows = [], []
            for r in range(tq // CHUNK):
                rows = slice(r * CHUNK, (r + 1) * CHUNK)
                m = mp_ref[h, rows, :]
                tiles = [jnp.exp2(s_ref[h, rows, c * LANES:(c + 1) * LANES] - m) if in_band(r, c) else None
                         for c in range(first, ntile)]
                l_rows.append(functools.reduce(jnp.add, [t for t in tiles if t is not None]))
                p_rows.append(jnp.concatenate([jnp.zeros((CHUNK, LANES), BF16) if t is None else t.astype(BF16)
                                               for t in tiles], axis=1))
            l = jnp.sum(jnp.concatenate(l_rows, axis=0), axis=-1, keepdims=True)
            return jnp.dot(jnp.concatenate(p_rows, axis=0), vwin, preferred_element_type=F32) / l

        outs = []
        logits(0)
        for h in range(1, N_HEADS64 + 1):
            if h < N_HEADS64:
                logits(h)
            outs.append(values(h - 1))
            if h % 2 == 0:
                cols = slice((h // 2 - 1) * LANES, (h // 2) * LANES)
                o_ref[:, cols] = (jnp.where(lo, outs[h - 2], outs[h - 1]) * gate[:, cols]).astype(BF16)

    n_lead = D_LEFT // tq
    for lead in range(n_lead):
        pl.when(qt == lead)(functools.partial(pipeline, (D_LEFT - lead * tq) // LANES))
    pl.when(qt >= n_lead)(functools.partial(pipeline, 0))


def _mixer_d(h, qg, kg, base, bsz, seq, tq=256):
    nt = seq // tq
    kern = functools.partial(_mixer_d_kernel, seq=seq, tq=tq)
    full = lambda shape: pl.BlockSpec(shape, lambda b, t: (0,) * len(shape))
    return pl.pallas_call(
        kern,
        grid=(bsz, nt),
        in_specs=[pl.BlockSpec((tq, GROUP_W), lambda b, t: (b * nt + t, BLK_D_Q)),
                  pl.BlockSpec((seq, GROUP_W), lambda b, t: (b, BLK_D_K)),
                  pl.BlockSpec((seq, GROUP_W), lambda b, t: (b, BLK_D_V)),
                  pl.BlockSpec((tq, GROUP_W), lambda b, t: (b * nt + t, BLK_D_Z)),
                  full((1, GROUP_W)), full((1, GROUP_W)), full((N_HEADS64, 2 * tq + D_LEFT))],
        out_specs=pl.BlockSpec((tq, GROUP_W), lambda b, t: (b * nt + t, 0)),
        out_shape=jax.ShapeDtypeStruct((bsz * seq, GROUP_W), BF16),
        scratch_shapes=[pltpu.VMEM((seq + D_LEFT, GROUP_W), BF16), pltpu.VMEM((seq + D_LEFT, GROUP_W), BF16),
                        pltpu.VMEM((N_HEADS64, tq, tq + D_LEFT), F32),
                        pltpu.VMEM((N_HEADS64, tq, tq + D_LEFT), F32), pltpu.VMEM((N_HEADS64, tq, LANES), F32)],
        compiler_params=_params(2),
    )(h, h, h, h, qg, kg, base)


def _w_in_pieces(take, zeros):
    c = lambda name, size, off=0: take(_SRC[name] + off, size)
    return [c("a_u", 512), c("a_v", 512), c("a_z", 512),
            c("b_q", 512), c("b_iq", 512), c("b_z", 512),
            c("b_k", 64), c("b_k", 64), c("b_v", 64), c("b_v", 64), c("b_ik", 64), c("b_ik", 64),
            c("c_kr", 32), c("b_iw", 8), zeros(24), c("c_kr", 32, 32), zeros(32),
            c("c_q", 384), c("c_kv", 128), c("c_z", 512),
            c("d_q", 512), c("d_k", 512), c("d_v", 512), c("d_z", 512)]


def _layout_w_in_kernel(w_ref, o_ref):
    tk = w_ref.shape[2]
    pieces = _w_in_pieces(lambda s, n: w_ref[0, s:s + n, :], lambda n: jnp.zeros((n, tk), F32))
    ends = np.cumsum([0] + [p.shape[0] for p in pieces])
    start = 0
    for i in range(1, len(pieces) + 1):
        if ends[i] % GROUP_W == 0:
            group = pieces[start:i]
            blk = group[0] if len(group) == 1 else jnp.concatenate(group, axis=0)
            o_ref[0, ends[start]:ends[i], :] = blk.astype(BF16)
            start = i


def _layout_w_in(w_in, tk=256):
    w_t = jnp.swapaxes(w_in, 1, 2)
    depth, cols, d = w_t.shape
    return pl.pallas_call(
        _layout_w_in_kernel,
        grid=(depth, d // tk),
        in_specs=[pl.BlockSpec((1, cols, tk), lambda l, i: (l, 0, i))],
        out_specs=pl.BlockSpec((1, H_COLS, tk), lambda l, i: (l, 0, i)),
        out_shape=jax.ShapeDtypeStruct((depth, H_COLS, d), BF16),
        compiler_params=_params(2),
    )(w_t)


def _rope_layout(v):
    z = jnp.zeros(v.shape[:-1] + (32,), v.dtype)
    return jnp.concatenate([v[..., :32], z, v[..., 32:], z], axis=-1)


def _layout_c(w_qb, w_kvb, q_gain, k_gain, qa_gain):
    wq = w_qb.reshape(Q_LORA, C_HEADS, C_QK)
    wq = jnp.concatenate([wq[..., :C_NOPE], _rope_layout(wq[..., C_NOPE:])], axis=-1)
    wq = wq.reshape(Q_LORA, C_HEADS * 256).astype(BF16)
    wkv = w_kvb.reshape(KV_LORA, C_HEADS, 2 * LANES)
    wkv = jnp.concatenate([wkv[..., :C_NOPE].reshape(KV_LORA, -1), wkv[..., C_NOPE:].reshape(KV_LORA, -1)],
                          axis=1).astype(BF16)
    lay = lambda g: jnp.concatenate([g[:C_NOPE], _rope_layout(g[C_NOPE:])])[None, :]
    return wq, wkv, lay(q_gain), lay(k_gain), qa_gain[None, :]


def _t5_bucket_static(rel):
    half = T5_BUCKETS // 2
    exact = half // 2
    n = abs(rel)
    if n < exact:
        val = n
    else:
        val = min(exact + (n * n // (exact * exact)).bit_length() - 1, half - 1)
    return (half if rel > 0 else 0) + val


def _t5_tables(t5_bias):
    m = np.arange(512)
    d0 = np.where(m < 256, m, m - 512)
    d1 = np.where(m <= 256, m - 256, m - 768)
    idx0 = np.array([_t5_bucket_static(int(d)) for d in d0], np.int32)
    idx1 = np.array([_t5_bucket_static(int(d)) for d in d1], np.int32)
    far = _t5_bucket_static(-512)
    return t5_bias[idx0].T, t5_bias[idx1].T, t5_bias[far]


def _band_table(rel_bias, tq):
    width = 2 * tq + D_LEFT
    m = np.arange(width)
    dist = np.where(m <= tq + D_LEFT, D_LEFT - m, D_LEFT + width - m)
    idx = np.clip(dist, -REL_CLIP, REL_CLIP) + REL_CLIP
    return rel_bias[idx.astype(np.int32)].T


def _rope_tables(seq):
    inv = ROPE_BASE ** (-jnp.arange(0, C_ROPE, 2, dtype=F32) / C_ROPE)
    ang = jnp.arange(seq, dtype=F32)[:, None] * inv[None, :]
    c, s = jnp.cos(ang), jnp.sin(ang)
    z = jnp.zeros_like(c)
    return jnp.concatenate([c, z, c, z], axis=1), jnp.concatenate([-s, z, s, z], axis=1)


def kernel(x, t5_bias, norm_g, w_in, a_v_gain, a_ws, a_bs, b_q_gain, b_k_gain, c_qa_gain, c_kva_gain,
           c_w_qb, c_w_kvb, c_q_gain, c_k_gain, d_q_gain, d_k_gain, d_rel_bias, w_out):
    bsz, seq, d_model = x.shape
    depth = w_in.shape[0]
    tq = 256
    assert seq % 512 == 0 and seq <= 2048 and d_model % 512 == 0, (seq, d_model)
    assert w_in.shape[2] == _SRC["d_z"] + GROUP_W and w_out.shape[1] == 4 * GROUP_W, (w_in.shape, w_out.shape)
    x2 = x.reshape(bsz * seq, d_model)
    cos, sin = _rope_tables(seq)
    base0, base1, cfar = _t5_tables(t5_bias)
    w_in_blocks = _layout_w_in(w_in)
    for l in range(depth):
        h = _inproj(x2, norm_g[l][None, :], w_in_blocks, l)
        mask = _b_select(h, bsz, seq)
        y_b = _b_attn(h, mask, cfar, jnp.tile(b_q_gain[l], N_HEADS64)[None, :],
                      jnp.tile(b_k_gain[l], 2)[None, :], base0, base1, bsz, seq, tq)
        wq, wkv, qg, kg, qag = _layout_c(c_w_qb[l], c_w_kvb[l], c_q_gain[l], c_k_gain[l], c_qa_gain[l])
        qc, kc, vc = _c_prep(h, wq, wkv, qag, c_kva_gain[l][None, :], qg, kg, cos, sin, seq)
        y_c = _c_attn(qc, kc, vc, h, bsz, seq, tq)
        y_d = _mixer_d(h, jnp.tile(d_q_gain[l], N_HEADS64)[None, :], jnp.tile(d_k_gain[l], N_HEADS64)[None, :],
                       _band_table(d_rel_bias[l], tq), bsz, seq, tq)
        x2 = _outproj(x2, h, a_v_gain[l][None, :], a_ws[l], a_bs[l][:, :, None], (y_b, y_c, y_d),
                      w_out[l].astype(BF16))
    return x2.reshape(bsz, seq, d_model)
```

```python
import functools
import math

import numpy as np
import jax
import jax.numpy as jnp
from jax import lax
from jax.experimental import pallas as pl
from jax.experimental.pallas import tpu as pltpu

F32 = jnp.float32
BF16 = jnp.bfloat16

EPS = 1e-6
NEG = -1e30
LOG2E = math.log2(math.e)
KEY_LOWEST = int(np.float32(-np.finfo(np.float32).max).view(np.int32)) ^ 0x7FFFFFFF
CHUNK = 64
LANES = 128
GROUP_W = 512
A_GROUPS = 4
GMLP_BLOCK = 128
N_HEADS64 = 8
IDX_SCALE = (8 ** -0.5) * 0.125
TOPK_MAX = 256
T5_BUCKETS = 32
C_HEADS = 4
C_NOPE = 128
C_ROPE = 64
C_QK = 192
Q_LORA = 384
KV_LORA = 128
ROPE_BASE = 10000.0
D_LEFT = 8 * CHUNK
REL_CLIP = 128
VMEM_LIMIT = 56 * 1024 * 1024
OUTPROJ_VMEM_LIMIT = 58 * 1024 * 1024

BLK_A_U, BLK_A_V, BLK_A_Z = 0, 1, 2
BLK_B_Q, BLK_B_IQ, BLK_B_Z, BLK_SMALL = 3, 4, 5, 6
BLK_C_QKV, BLK_C_Z = 7, 8
BLK_D_Q, BLK_D_K, BLK_D_V, BLK_D_Z = 9, 10, 11, 12
H_COLS = 13 * GROUP_W
UNIT_B_K, UNIT_B_V, UNIT_B_IK, UNIT_KR_IW = (BLK_SMALL * 4 + i for i in range(4))
IW_LANE = 32

_SRC = dict(a_u=0, a_v=512, a_z=1024, b_q=1536, b_k=2048, b_v=2112, b_iq=2176, b_ik=2688,
            b_iw=2752, b_z=2760, c_q=3272, c_kv=3656, c_kr=3784, c_z=3848,
            d_q=4360, d_k=4872, d_v=5384, d_z=5896)


def _params(n_axes, vmem_limit=VMEM_LIMIT):
    return pltpu.CompilerParams(dimension_semantics=("arbitrary",) * n_axes,
                                vmem_limit_bytes=vmem_limit)


def _gelu(x):
    c = math.sqrt(2.0 / math.pi)
    return x * (0.5 * (1.0 + jnp.tanh(c * (x + 0.044715 * (x * x * x)))))


def _silu(x):
    return x * (1.0 / (1.0 + jnp.exp(-x)))


def _dot_t(a, b):
    return lax.dot_general(a, b, (((1,), (1,)), ((), ())), preferred_element_type=F32)


def _lo_mask(rows):
    return lax.broadcasted_iota(jnp.int32, (rows, LANES), 1) < 64


def _rms_heads64(x, gain, ntiles):
    lo = _lo_mask(x.shape[0])
    tiles = []
    for t in range(ntiles):
        xt = x[:, t * LANES:(t + 1) * LANES]
        sq = xt * xt
        s_lo = jnp.sum(jnp.where(lo, sq, 0.0), axis=-1, keepdims=True)
        s_hi = jnp.sum(jnp.where(lo, 0.0, sq), axis=-1, keepdims=True)
        r = jnp.where(lo, lax.rsqrt(s_lo * (1.0 / 64) + EPS), lax.rsqrt(s_hi * (1.0 / 64) + EPS))
        tiles.append(xt * r * gain[:, t * LANES:(t + 1) * LANES])
    return tiles


def _for_blocks(blocks, count):
    def quad(i, carry):
        blocks(4 * i, 4)
        return carry

    lax.fori_loop(0, count // 4, quad, 0)
    first = (count // 4) * 4

    @pl.when((count & 2) != 0)
    def _():
        blocks(first, 2)

    @pl.when((count & 1) != 0)
    def _():
        blocks(first + (count & 2), 1)


def _toeplitz(base_row, rows, width):
    t = jnp.broadcast_to(base_row, (rows, base_row.shape[1]))
    t = pltpu.roll(t, 0, 1, stride=1, stride_axis=0)
    return t[:, :width]


def _inproj_kernel(x_ref, g_ref, w_ref, o_ref):
    x = x_ref[...]
    ms = jnp.mean(x * x, axis=-1, keepdims=True)
    xn = (x * lax.rsqrt(ms + EPS) * g_ref[...]).astype(BF16)
    for c in range(H_COLS // GROUP_W):
        cols = slice(c * GROUP_W, (c + 1) * GROUP_W)
        o_ref[:, cols] = _dot_t(xn, w_ref[cols, :]).astype(BF16)


def _inproj(x2, g, w_all, layer, tm=512):
    n, d = x2.shape
    return pl.pallas_call(
        _inproj_kernel,
        grid=(n // tm,),
        in_specs=[pl.BlockSpec((tm, d), lambda i: (i, 0)),
                  pl.BlockSpec((1, d), lambda i: (0, 0)),
                  pl.BlockSpec((None, H_COLS, d), lambda i: (layer, 0, 0), pipeline_mode=pl.Buffered(1))],
        out_specs=pl.BlockSpec((tm, H_COLS), lambda i: (i, 0)),
        out_shape=jax.ShapeDtypeStruct((n, H_COLS), BF16),
        compiler_params=_params(1),
    )(x2, g, w_all)


def _outproj_kernel(x_ref, uvz_ref, vg_ref, ws_ref, bs_ref, yb_ref, yc_ref, yd_ref, w_ref, o_ref, ya_ref):
    u_ref, v_ref, z_ref = (uvz_ref.at[:, blk * GROUP_W:(blk + 1) * GROUP_W] for blk in (BLK_A_U, BLK_A_V, BLK_A_Z))
    tm, d = x_ref.shape
    wgs = _mixer_a_weights(ws_ref)
    nblk = tm // GMLP_BLOCK
    ncol = d // nblk
    for c in range(nblk):
        cols = slice(c * ncol, (c + 1) * ncol)
        acc = x_ref[:, cols]
        for g, y_ref in ((1, yb_ref), (2, yc_ref), (3, yd_ref)):
            acc = acc + jnp.dot(y_ref[...], w_ref[g * GROUP_W:(g + 1) * GROUP_W, cols],
                                preferred_element_type=F32)
        o_ref[:, cols] = acc
        _mixer_a_rows(u_ref, v_ref, z_ref, vg_ref, wgs, bs_ref, ya_ref, c)
    for c in range(nblk):
        cols = slice(c * ncol, (c + 1) * ncol)
        o_ref[:, cols] += jnp.dot(ya_ref[...], w_ref[0:GROUP_W, cols], preferred_element_type=F32)


def _outproj(x2, h, vg, ws, bs, ys, w, tm=1024):
    n, d = x2.shape
    yspec = pl.BlockSpec((tm, GROUP_W), lambda i: (i, 0))
    return pl.pallas_call(
        _outproj_kernel,
        grid=(n // tm,),
        in_specs=[pl.BlockSpec((tm, d), lambda i: (i, 0)),
                  pl.BlockSpec((tm, 3 * GROUP_W), lambda i: (i, 0)),
                  pl.BlockSpec((1, GROUP_W), lambda i: (0, 0)),
                  pl.BlockSpec((A_GROUPS, GMLP_BLOCK, GMLP_BLOCK), lambda i: (0, 0, 0)),
                  pl.BlockSpec((A_GROUPS, GMLP_BLOCK, 1), lambda i: (0, 0, 0)),
                  yspec, yspec, yspec,
                  pl.BlockSpec((4 * GROUP_W, d), lambda i: (0, 0), pipeline_mode=pl.Buffered(1))],
        out_specs=pl.BlockSpec((tm, d), lambda i: (i, 0)),
        out_shape=jax.ShapeDtypeStruct((n, d), F32),
        scratch_shapes=[pltpu.VMEM((tm, GROUP_W), BF16)],
        compiler_params=_params(1, OUTPROJ_VMEM_LIMIT),
    )(x2, h, vg, ws, bs, *ys, w)


def _mixer_a_weights(w_ref):
    i = lax.broadcasted_iota(jnp.int32, (GMLP_BLOCK, GMLP_BLOCK), 0)
    j = lax.broadcasted_iota(jnp.int32, (GMLP_BLOCK, GMLP_BLOCK), 1)
    keep = (j // CHUNK) <= (i // CHUNK)
    return [jnp.where(keep, w_ref[g], 0.0).astype(BF16) for g in range(A_GROUPS)]


def _mixer_a_rows(u_ref, v_ref, z_ref, vg_ref, wgs, b_ref, o_ref, blk):
    rows = slice(blk * GMLP_BLOCK, (blk + 1) * GMLP_BLOCK)
    u = _gelu(u_ref[rows, :].astype(F32))
    v = _gelu(v_ref[rows, :].astype(F32))
    ms = jnp.mean(v * v, axis=-1, keepdims=True)
    vb = (v * lax.rsqrt(ms + EPS) * vg_ref[...]).astype(BF16)
    gate = _silu(z_ref[rows, :].astype(F32))
    for g in range(A_GROUPS):
        cols = slice(g * LANES, (g + 1) * LANES)
        sg = jnp.dot(wgs[g], vb[:, cols], preferred_element_type=F32) + b_ref[g]
        o_ref[rows, cols] = (u[:, cols] * sg * gate[:, cols]).astype(BF16)


def _order_key(x):
    return jnp.where(x < 0, x ^ 0x7FFFFFFF, x)


def _b_select_kernel(iq_ref, iw_ref, ik_ref, o_ref, lhs_ref, sc_ref, *, seq, tq, topk):
    t_blk = pl.program_id(1)
    nkb = seq // 256
    nblk = ((t_blk + 1) * tq) // 256
    n_interp = 12
    n_unchecked = 13 + 2 * t_blk
    hrows = [slice(h * tq, (h + 1) * tq) for h in range(N_HEADS64)]

    lo_half = _lo_mask(tq)
    w_t = (iw_ref[...].astype(F32) * IDX_SCALE).T
    for h in range(N_HEADS64):
        iqt = iq_ref[:, (h // 2) * LANES:(h // 2 + 1) * LANES]
        sel = lo_half if h % 2 == 0 else jnp.logical_not(lo_half)
        lhs_ref[hrows[h], :] = jnp.where(sel, iqt, jnp.zeros_like(iqt))

    def qpos(ln):
        return t_blk * tq + ln.start + lax.broadcasted_iota(jnp.int32, (256, ln.stop - ln.start), 1)

    def krow(ln):
        return lax.broadcasted_iota(jnp.int32, (256, ln.stop - ln.start), 0)

    assert tq in (256, 512)
    every = slice(0, tq)
    tail = slice(256, tq) if tq == 512 else None
    nwide = nblk - 1 if tail else nblk

    def fold8(x):
        return jnp.sum(x.reshape(256 // 8, 8, x.shape[1]), axis=0)

    def widen(part, ln):
        return part if ln == every else jnp.concatenate([jnp.zeros((8, ln.start), F32), part], axis=1)

    def score_block(kb, carry, masked, ln):
        amax, n_pos, n_nn = carry
        off = pl.multiple_of(kb * 256, 256)
        ikblk = ik_ref[pl.ds(off, 256), :]
        score = jnp.zeros((256, ln.stop - ln.start), F32)
        for h in range(N_HEADS64):
            w_h = w_t[IW_LANE + h:IW_LANE + h + 1, ln]
            q_h = lhs_ref[h * tq + ln.start:h * tq + ln.stop, :]
            score = score + w_h * jnp.maximum(_dot_t(ikblk, q_h), 0.0)
        mag = jnp.abs(score)
        if masked:
            adm = ((kb * 256 + krow(ln)) // CHUNK) <= (qpos(ln) // CHUNK)
            score = jnp.where(adm, score, -jnp.inf)
            mag = jnp.where(adm, mag, 0.0)
        sc_ref[kb, :, ln] = score
        return (jnp.maximum(amax, widen(jnp.max(mag.reshape(256 // 8, 8, mag.shape[1]), axis=0), ln)),
                n_pos + widen(fold8(jnp.where(score > 0.0, 1.0, 0.0)), ln),
                n_nn + widen(fold8(jnp.where(score >= 0.0, 1.0, 0.0)), ln))

    nfull = (t_blk * tq) // 256
    zeros8 = jnp.zeros((8, tq), F32)
    carry = lax.fori_loop(0, nfull, lambda kb, c: score_block(kb, c, False, every), (zeros8, zeros8, zeros8))
    carry = lax.fori_loop(nfull, nwide, lambda kb, c: score_block(kb, c, True, every), carry)
    if tail:
        carry = score_block(nblk - 1, carry, True, tail)
    amax = jnp.max(carry[0], axis=0, keepdims=True)
    f_pos = jnp.sum(carry[1], axis=0, keepdims=True)
    f_nn = jnp.sum(carry[2], axis=0, keepdims=True)

    def count(pred):
        def body(kb, acc):
            return acc + fold8(jnp.where(pred(sc_ref[kb], kb, every), 1.0, 0.0))
        acc = lax.fori_loop(0, nwide, body, zeros8)
        if tail:
            acc = acc + widen(fold8(jnp.where(pred(sc_ref[nblk - 1, :, tail], nblk - 1, tail), 1.0, 0.0)), tail)
        return jnp.sum(acc, axis=0, keepdims=True)

    kf = float(topk)
    qrow = t_blk * tq + lax.broadcasted_iota(jnp.int32, (1, tq), 1)
    n_adm = ((qrow // CHUNK + 1) * CHUNK).astype(F32)
    one = jnp.ones((1, tq), jnp.int32)
    pos = f_pos > kf
    neg = f_nn < kf
    lo0 = jnp.where(pos, one, _order_key(lax.bitcast_convert_type(-amax, jnp.int32)))
    hi0 = jnp.where(neg, one - 1, _order_key(lax.bitcast_convert_type(amax, jnp.int32)) + 1)
    w_lo0 = jnp.where(pos, f_pos, n_adm) - kf
    w_hi0 = kf - jnp.where(neg, f_nn, 0.0)
    all_sel = n_adm <= kf
    at_zero = jnp.logical_not(pos | neg)
    done0 = jnp.where(all_sel | at_zero | (hi0 == lo0 + 1), 1.0, 0.0)
    thr0 = jnp.where(all_sel, KEY_LOWEST, jnp.where(at_zero, jnp.where(f_pos == kf, one, one - 1), lo0))

    def as_score(key):
        return lax.bitcast_convert_type(_order_key(key), F32)

    def search_cond(st):
        return jnp.logical_and(st[0][0] < n_interp + 32, st[1] < 0.5)

    def search_step(st):
        it, lo, hi, w_lo, w_hi, side, done, thr = st
        lo_v = as_score(lo)
        hi_v = as_score(hi)
        c_v = lo_v + (hi_v - lo_v) * (w_lo / (w_lo + w_hi))
        c_interp = _order_key(lax.bitcast_convert_type(c_v, jnp.int32))
        c_mid = (lo >> 1) + (hi >> 1) + (lo & hi & 1)
        cand = jnp.where(it < n_interp, c_interp, c_mid)
        cand = jnp.minimum(jnp.maximum(cand, lo + 1), hi - 1)
        cand_v = as_score(cand)
        f = count(lambda s, kb, ln: s >= cand_v[:, ln])
        live = done < 0.5
        up = f > kf
        hit = f == kf
        new_lo = jnp.where(live & up, cand, lo)
        new_hi = jnp.where(live & jnp.logical_not(up), cand, hi)
        new_w_lo = jnp.where(up, f - kf, jnp.where(side < 0.0, 0.5 * w_lo, w_lo))
        new_w_hi = jnp.where(up, jnp.where(side > 0.0, 0.5 * w_hi, w_hi), kf - f)
        new_side = jnp.where(up, 1.0, -1.0)
        new_thr = jnp.where(live, jnp.where(hit, cand, new_lo), thr)
        new_done = jnp.where(live & (hit | (new_hi == new_lo + 1)), 1.0, done)
        return (it + 1, new_lo, new_hi, jnp.where(live, new_w_lo, w_lo),
                jnp.where(live, new_w_hi, w_hi), jnp.where(live, new_side, side), new_done, new_thr)

    def checked_step(st):
        new = search_step(st[0])
        return new, jnp.min(new[-2])

    state = (jnp.int32(0), lo0, hi0, w_lo0, w_hi0, jnp.zeros((1, tq), F32), done0, thr0)
    state = lax.fori_loop(0, n_unchecked, lambda i, st: search_step(st), state)
    thr = as_score(lax.while_loop(search_cond, checked_step, (state, jnp.min(state[-2])))[0][-1])

    any_excess = jnp.max(count(lambda s, kb, ln: s >= thr[:, ln])) > kf

    def store_mask(kb, keep_t, ln):
        for g in range(tq // 256):
            if g * 256 >= ln.start:
                o_ref[g, kb] = keep_t[:, g * 256 - ln.start:(g + 1) * 256 - ln.start]
            else:
                o_ref[g, kb] = jnp.zeros((256, 256), F32)

    def write_mask(keep):
        def body(kb, carry):
            store_mask(kb, jnp.where(keep(sc_ref[kb], kb, every), 1.0, 0.0), every)
            return carry

        lax.fori_loop(0, nwide, body, 0)
        if tail:
            store_mask(nblk - 1, jnp.where(keep(sc_ref[nblk - 1, :, tail], nblk - 1, tail), 1.0, 0.0), tail)

    def write_unused(kb, carry):
        store_mask(kb, jnp.zeros((256, tq), F32), every)
        return carry

    lax.fori_loop(nblk, nkb, write_unused, 0)

    @pl.when(jnp.logical_not(any_excess))
    def _():
        write_mask(lambda s, kb, ln: s >= thr[:, ln])

    @pl.when(any_excess)
    def _():
        need = kf - count(lambda s, kb, ln: s > thr[:, ln])

        def idx_step(it, jmax):
            cand = jmax | lax.shift_left(jnp.int32(1), 10 - it)
            below = count(lambda s, kb, ln: (s == thr[:, ln]) & ((kb * 256 + krow(ln)) < cand[:, ln]))
            return jnp.where(below < need, cand, jmax)

        jmax = lax.fori_loop(0, 11, idx_step, jnp.zeros((1, tq), jnp.int32))
        write_mask(lambda s, kb, ln: (s > thr[:, ln]) | ((s == thr[:, ln]) & ((kb * 256 + krow(ln)) <= jmax[:, ln])))


def _b_select(h, bsz, seq, tq=512):
    tq = min(tq, seq)
    nt = seq // tq
    topk = min(TOPK_MAX, seq // 4)
    kern = functools.partial(_b_select_kernel, seq=seq, tq=tq, topk=topk)
    return pl.pallas_call(
        kern,
        grid=(bsz, nt),
        in_specs=[pl.BlockSpec((tq, GROUP_W), lambda b, t: (b * nt + t, BLK_B_IQ)),
                  pl.BlockSpec((tq, LANES), lambda b, t: (b * nt + t, UNIT_KR_IW)),
                  pl.BlockSpec((seq, LANES), lambda b, t: (b, UNIT_B_IK))],
        out_specs=pl.BlockSpec((tq // 256, seq // 256, 256, 256), lambda b, t: (b * nt + t, 0, 0, 0)),
        out_shape=jax.ShapeDtypeStruct((bsz * seq // 256, seq // 256, 256, 256), F32),
        scratch_shapes=[pltpu.VMEM((N_HEADS64 * tq, LANES), BF16),
                        pltpu.VMEM((seq // 256, 256, tq), F32)],
        compiler_params=_params(2),
    )(h, h, h)


def _b_attn_kernel(cfar_ref, q_ref, z_ref, k_ref, v_ref, msk_ref, qg_ref, kg_ref, base0_ref, base1_ref,
                   o_ref, kn_ref, v1_ref, bias_ref, qall_ref, s_ref, mp_ref, acc_ref, *, seq, tq):
    b = pl.program_id(0)
    t_blk = pl.program_id(1)
    hrows = [slice(h * tq, (h + 1) * tq) for h in range(N_HEADS64)]

    @pl.when((b == 0) & (t_blk == 0))
    def _():
        for h in range(N_HEADS64):
            bias_ref[0, hrows[h], :] = jnp.full((tq, 256), cfar_ref[h] * LOG2E, F32)
            bias_ref[1, hrows[h], :] = _toeplitz(base1_ref[h:h + 1, :], tq, 256) * LOG2E
            bias_ref[2, hrows[h], :] = _toeplitz(base0_ref[h:h + 1, :], tq, 256) * LOG2E

    @pl.when(t_blk == 0)
    def _():
        lo256 = _lo_mask(256)
        for r in range(seq // 256):
            rows = slice(r * 256, (r + 1) * 256)
            k = k_ref[rows, :].astype(F32)
            ms = jnp.mean(k * k, axis=-1, keepdims=True)
            kn_ref[rows, :] = (k * lax.rsqrt(ms + EPS) * kg_ref[...]).astype(BF16)
            v = v_ref[rows, :]
            v1_ref[rows, :] = jnp.where(lo256, v, jnp.ones_like(v))

    lo = _lo_mask(tq)
    qtiles = _rms_heads64(q_ref[...].astype(F32), qg_ref[...], 4)
    for h in range(N_HEADS64):
        sel = lo if h % 2 == 0 else jnp.logical_not(lo)
        qall_ref[hrows[h], :] = jnp.where(sel, qtiles[h // 2] * (0.125 * LOG2E), 0.0).astype(BF16)
    mp_ref[...] = jnp.full(mp_ref.shape, NEG, F32)
    acc_ref[...] = jnp.zeros(acc_ref.shape, F32)
    nblk = t_blk + 1

    def logits_blocks(kb0, n):
        for i in range(n):
            kb = kb0 + i
            off = pl.multiple_of(kb * 256, 256)
            kblk = kn_ref[pl.ds(off, 256), :]
            which = jnp.clip(kb - (t_blk - 2), 0, 2)
            keep = msk_ref[0, kb].T > 0.5
            for h in range(N_HEADS64):
                s = jnp.where(keep, _dot_t(qall_ref[hrows[h], :], kblk) + bias_ref[which, hrows[h], :], NEG)
                s_ref[kb, hrows[h], :] = s
                mp_ref[hrows[h], :] = jnp.maximum(mp_ref[hrows[h], :], jnp.maximum(s[:, :LANES], s[:, LANES:]))

    def value_blocks(kb0, n):
        off = pl.multiple_of(kb0 * 256, 256)
        v1 = v1_ref[pl.ds(off, n * 256), :]
        for h in range(N_HEADS64):
            m = mp_ref[hrows[h], :]
            mm = jnp.concatenate([m, m], axis=1)
            p = [jnp.exp2(s_ref[kb0 + i, hrows[h], :] - mm).astype(BF16) for i in range(n)]
            p = p[0] if n == 1 else jnp.concatenate(p, axis=1)
            acc_ref[hrows[h], :] += jnp.dot(p, v1, preferred_element_type=F32)

    _for_blocks(logits_blocks, nblk)
    for h in range(N_HEADS64):
        m = jnp.max(mp_ref[hrows[h], :], axis=-1, keepdims=True)
        mp_ref[hrows[h], :] = jnp.broadcast_to(m, (tq, LANES))
    _for_blocks(value_blocks, nblk)

    gate = _silu(z_ref[...].astype(F32))
    for t in range(4):
        a_even = acc_ref[hrows[2 * t], :]
        a_odd = acc_ref[hrows[2 * t + 1], :]
        o_even = a_even / pltpu.roll(a_even, 64, 1)
        o_odd = pltpu.roll(a_odd, 64, 1) / a_odd
        cols = slice(t * LANES, (t + 1) * LANES)
        o_ref[:, cols] = (jnp.where(lo, o_even, o_odd) * gate[:, cols]).astype(BF16)


def _b_attn(h, mask, cfar, qg, kg, base0, base1, bsz, seq, tq=256):
    nt = seq // tq
    rows = N_HEADS64 * tq
    kern = functools.partial(_b_attn_kernel, seq=seq, tq=tq)
    full = lambda shape: pl.BlockSpec(shape, lambda b, t: (0,) * len(shape))
    return pl.pallas_call(
        kern,
        grid=(bsz, nt),
        in_specs=[pl.BlockSpec(memory_space=pltpu.SMEM),
                  pl.BlockSpec((tq, GROUP_W), lambda b, t: (b * nt + t, BLK_B_Q)),
                  pl.BlockSpec((tq, GROUP_W), lambda b, t: (b * nt + t, BLK_B_Z)),
                  pl.BlockSpec((seq, LANES), lambda b, t: (b, UNIT_B_K)),
                  pl.BlockSpec((seq, LANES), lambda b, t: (b, UNIT_B_V)),
                  pl.BlockSpec((1, seq // 256, tq, 256), lambda b, t: (b * nt + t, 0, 0, 0)),
                  full((1, GROUP_W)), full((1, LANES)), full((N_HEADS64, 512)), full((N_HEADS64, 512))],
        out_specs=pl.BlockSpec((tq, GROUP_W), lambda b, t: (b * nt + t, 0)),
        out_shape=jax.ShapeDtypeStruct((bsz * seq, GROUP_W), BF16),
        scratch_shapes=[pltpu.VMEM((seq, LANES), BF16), pltpu.VMEM((seq, LANES), BF16),
                        pltpu.VMEM((3, rows, 256), F32), pltpu.VMEM((rows, LANES), BF16),
                        pltpu.VMEM((seq // 256, rows, 256), F32),
                        pltpu.VMEM((rows, LANES), F32), pltpu.VMEM((rows, LANES), F32)],
        compiler_params=_params(2),
    )(cfar, h, h, h, h, mask, qg, kg, base0, base1)


def _rope(tile, cos, sin):
    return tile * cos + pltpu.roll(tile, 64, 1) * sin


def _c_prep_kernel(lat_ref, kr_ref, wq_ref, wkv_ref, qag_ref, kvag_ref, qg_ref, kg_ref, cos_ref, sin_ref,
                   qo_ref, ko_ref, vo_ref):
    cq = lat_ref[:, :Q_LORA].astype(F32)
    ms = jnp.mean(cq * cq, axis=-1, keepdims=True)
    cqn = (cq * lax.rsqrt(ms + EPS) * qag_ref[...]).astype(BF16)
    qpre = jnp.dot(cqn, wq_ref[...], preferred_element_type=F32)
    ckv = lat_ref[:, Q_LORA:].astype(F32)
    ms = jnp.mean(ckv * ckv, axis=-1, keepdims=True)
    ckvn = (ckv * lax.rsqrt(ms + EPS) * kvag_ref[...]).astype(BF16)
    kvpre = jnp.dot(ckvn, wkv_ref[...], preferred_element_type=F32)
    lane = lax.broadcasted_iota(jnp.int32, kr_ref.shape, 1)
    kr = jnp.where((lane % 64) < 32, kr_ref[...].astype(F32), 0.0)
    kr_ss = jnp.sum(kr * kr, axis=-1, keepdims=True)
    cos = cos_ref[...]
    sin = sin_ref[...]
    qg = qg_ref[...]
    kg = kg_ref[...]
    kr_rot = _rope(kr * kg[:, LANES:], cos, sin)
    for h in range(C_HEADS):
        qh = qpre[:, h * 256:(h + 1) * 256]
        r = lax.rsqrt(jnp.sum(qh * qh, axis=-1, keepdims=True) * (1.0 / C_QK) + EPS)
        qn = qh * r * qg
        qo_ref[:, h * 256:h * 256 + LANES] = qn[:, :LANES].astype(BF16)
        qo_ref[:, h * 256 + LANES:(h + 1) * 256] = _rope(qn[:, LANES:], cos, sin).astype(BF16)
        kn = kvpre[:, h * LANES:(h + 1) * LANES]
        r = lax.rsqrt((jnp.sum(kn * kn, axis=-1, keepdims=True) + kr_ss) * (1.0 / C_QK) + EPS)
        ko_ref[:, h * 256:h * 256 + LANES] = (kn * r * kg[:, :LANES]).astype(BF16)
        ko_ref[:, h * 256 + LANES:(h + 1) * 256] = (kr_rot * r).astype(BF16)
    vo_ref[...] = kvpre[:, C_HEADS * LANES:].astype(BF16)


def _c_prep(h, wq, wkv, qag, kvag, qg, kg, cos, sin, seq, tm=1024):
    n = h.shape[0]
    tm = min(tm, seq)
    ns = seq // tm
    full = lambda shape: pl.BlockSpec(shape, lambda i: (0,) * len(shape))
    return pl.pallas_call(
        _c_prep_kernel,
        grid=(n // tm,),
        in_specs=[pl.BlockSpec((tm, GROUP_W), lambda i: (i, BLK_C_QKV)),
                  pl.BlockSpec((tm, LANES), lambda i: (i, UNIT_KR_IW)),
                  full((Q_LORA, 4 * 256)), full((KV_LORA, 8 * LANES)),
                  full((1, Q_LORA)), full((1, LANES)), full((1, 256)), full((1, 256)),
                  pl.BlockSpec((tm, LANES), lambda i: (i % ns, 0)),
                  pl.BlockSpec((tm, LANES), lambda i: (i % ns, 0))],
        out_specs=[pl.BlockSpec((tm, 4 * 256), lambda i: (i, 0)),
                   pl.BlockSpec((tm, 4 * 256), lambda i: (i, 0)),
                   pl.BlockSpec((tm, GROUP_W), lambda i: (i, 0))],
        out_shape=[jax.ShapeDtypeStruct((n, 4 * 256), BF16), jax.ShapeDtypeStruct((n, 4 * 256), BF16),
                   jax.ShapeDtypeStruct((n, GROUP_W), BF16)],
        compiler_params=_params(1),
    )(h, h, wq, wkv, qag, kvag, qg, kg, cos, sin)


def _c_attn_kernel(q_ref, k_ref, v_ref, z_ref, o_ref, s_ref, mp_ref, lp_ref, acc_ref, *, tq):
    qt = pl.program_id(1)
    scale = C_QK ** -0.5 * LOG2E
    hrows = [slice(h * tq, (h + 1) * tq) for h in range(C_HEADS)]
    qchunk = (qt * tq + lax.broadcasted_iota(jnp.int32, (tq, 256), 0)) // CHUNK
    kcol = lax.broadcasted_iota(jnp.int32, (tq, 256), 1)
    mp_ref[...] = jnp.full(mp_ref.shape, NEG, F32)
    lp_ref[...] = jnp.zeros(lp_ref.shape, F32)
    acc_ref[...] = jnp.zeros(acc_ref.shape, F32)
    nfull = (qt * tq) // 256
    nblk = ((qt + 1) * tq) // 256

    def logits_blocks(kb0, n, masked):
        for i in range(n):
            kb = kb0 + i
            off = pl.multiple_of(kb * 256, 256)
            for h in range(C_HEADS):
                cols = slice(h * 256, (h + 1) * 256)
                s = _dot_t(q_ref[:, cols], k_ref[pl.ds(off, 256), cols]) * scale
                if masked:
                    s = jnp.where(((kb * 256 + kcol) // CHUNK) <= qchunk, s, NEG)
                s_ref[kb, hrows[h], :] = s
                mp_ref[hrows[h], :] = jnp.maximum(mp_ref[hrows[h], :], jnp.maximum(s[:, :LANES], s[:, LANES:]))

    def value_blocks(kb0, n):
        off = pl.multiple_of(kb0 * 256, 256)
        for h in range(C_HEADS):
            m = mp_ref[hrows[h], :]
            mm = jnp.concatenate([m, m], axis=1)
            p = [jnp.exp2(s_ref[kb0 + i, hrows[h], :] - mm) for i in range(n)]
            lsum = p[0][:, :LANES] + p[0][:, LANES:]
            for pi in p[1:]:
                lsum = lsum + pi[:, :LANES] + pi[:, LANES:]
            lp_ref[hrows[h], :] += lsum
            pb = p[0].astype(BF16) if n == 1 else jnp.concatenate([pi.astype(BF16) for pi in p], axis=1)
            acc_ref[hrows[h], :] += jnp.dot(pb, v_ref[pl.ds(off, n * 256), h * LANES:(h + 1) * LANES],
                                            preferred_element_type=F32)

    _for_blocks(lambda kb0, n: logits_blocks(kb0, n, False), nfull)
    lax.fori_loop(nfull, nblk, lambda kb, c: (logits_blocks(kb, 1, True), c)[1], 0)
    for h in range(C_HEADS):
        m = jnp.max(mp_ref[hrows[h], :], axis=-1, keepdims=True)
        mp_ref[hrows[h], :] = jnp.broadcast_to(m, (tq, LANES))
    _for_blocks(value_blocks, nblk)
    gate = _silu(z_ref[...].astype(F32))
    for h in range(C_HEADS):
        cols = slice(h * LANES, (h + 1) * LANES)
        l = jnp.sum(lp_ref[hrows[h], :], axis=-1, keepdims=True)
        o_ref[:, cols] = (acc_ref[hrows[h], :] / l * gate[:, cols]).astype(BF16)


def _c_attn(qc, kc, vc, h, bsz, seq, tq=256):
    nt = seq // tq
    rows = C_HEADS * tq
    kern = functools.partial(_c_attn_kernel, tq=tq)
    return pl.pallas_call(
        kern,
        grid=(bsz, nt),
        in_specs=[pl.BlockSpec((tq, C_HEADS * 256), lambda b, t: (b * nt + t, 0)),
                  pl.BlockSpec((seq, C_HEADS * 256), lambda b, t: (b, 0)),
                  pl.BlockSpec((seq, GROUP_W), lambda b, t: (b, 0)),
                  pl.BlockSpec((tq, GROUP_W), lambda b, t: (b * nt + t, BLK_C_Z))],
        out_specs=pl.BlockSpec((tq, GROUP_W), lambda b, t: (b * nt + t, 0)),
        out_shape=jax.ShapeDtypeStruct((bsz * seq, GROUP_W), BF16),
        scratch_shapes=[pltpu.VMEM((seq // 256, rows, 256), F32), pltpu.VMEM((rows, LANES), F32),
                        pltpu.VMEM((rows, LANES), F32), pltpu.VMEM((rows, LANES), F32)],
        compiler_params=_params(2),
    )(qc, kc, vc, h)


def _mixer_d_kernel(q_ref, k_ref, v_ref, z_ref, qg_ref, kg_ref, base_ref, o_ref,
                    kpad_ref, vpad_ref, bias_ref, s_ref, mp_ref, *, seq, tq):
    b = pl.program_id(0)
    qt = pl.program_id(1)
    win = tq + D_LEFT

    @pl.when((b == 0) & (qt == 0))
    def _():
        qc = lax.broadcasted_iota(jnp.int32, (tq, win), 0) // CHUNK
        kc = lax.broadcasted_iota(jnp.int32, (tq, win), 1) // CHUNK
        band = (kc >= qc) & (kc <= qc + D_LEFT // CHUNK)
        for h in range(N_HEADS64):
            bias_ref[h] = jnp.where(band, _toeplitz(base_ref[h:h + 1, :], tq, win) * LOG2E, NEG)

    @pl.when(qt == 0)
    def _():
        kpad_ref[0:D_LEFT, :] = jnp.zeros((D_LEFT, GROUP_W), BF16)
        vpad_ref[0:D_LEFT, :] = jnp.zeros((D_LEFT, GROUP_W), BF16)
        for r in range(seq // 256):
            rows = slice(r * 256, (r + 1) * 256)
            dst = slice(D_LEFT + r * 256, D_LEFT + (r + 1) * 256)
            tiles = _rms_heads64(k_ref[rows, :].astype(F32), kg_ref[...], 4)
            for t in range(4):
                kpad_ref[dst, t * LANES:(t + 1) * LANES] = tiles[t].astype(BF16)
            vpad_ref[dst, :] = v_ref[rows, :]

    lo = _lo_mask(tq)
    qtiles = _rms_heads64(q_ref[...].astype(F32), qg_ref[...], 4)
    start = pl.multiple_of(qt * tq, tq)
    gate = _silu(z_ref[...].astype(F32))
    ntile = win // LANES

    def in_band(r, c):
        return c * LANES < r * CHUNK + D_LEFT + CHUNK and (c + 1) * LANES > r * CHUNK

    def pipeline(first):
        live = slice(first * LANES, win)
        wstart = pl.multiple_of(start + first * LANES, LANES)

        def logits(h):
            t, half = divmod(h, 2)
            kwin = kpad_ref[pl.ds(wstart, win - first * LANES), t * LANES:(t + 1) * LANES]
            sel = lo if half == 0 else jnp.logical_not(lo)
            qh = jnp.where(sel, qtiles[t] * (0.125 * LOG2E), 0.0).astype(BF16)
            s = _dot_t(qh, kwin) + bias_ref[h, :, live]
            s_ref[h, :, live] = s
            tiles = [s[:, i * LANES:(i + 1) * LANES] for i in range(ntile - first)]
            m = jnp.max(functools.reduce(jnp.maximum, tiles), axis=-1, keepdims=True)
            mp_ref[h] = jnp.broadcast_to(m, (tq, LANES))

        def values(h):
            vwin = vpad_ref[pl.ds(wstart, win - first * LANES), (h // 2) * LANES:(h // 2 + 1) * LANES]
            p_rows, l_rows = [], []
            for r in range(tq // CHUNK):
                rows = slice(r * CHUNK, (r + 1) * CHUNK)
                m = mp_ref[h, rows, :]
                tiles = [jnp.exp2(s_ref[h, rows, c * LANES:(c + 1) * LANES] - m) if in_band(r, c) else None
                         for c in range(first, ntile)]
                l_rows.append(functools.reduce(jnp.add, [t for t in tiles if t is not None]))
                p_rows.append(jnp.concatenate([jnp.zeros((CHUNK, LANES), BF16) if t is None else t.astype(BF16)
                                               for t in tiles], axis=1))
            l = jnp.sum(jnp.concatenate(l_rows, axis=0), axis=-1, keepdims=True)
            return jnp.dot(jnp.concatenate(p_rows, axis=0), vwin, preferred_element_type=F32) / l

        outs = []
        logits(0)
        for h in range(1, N_HEADS64 + 1):
            if h < N_HEADS64:
                logits(h)
            outs.append(values(h - 1))
            if h % 2 == 0:
                cols = slice((h // 2 - 1) * LANES, (h // 2) * LANES)
                o_ref[:, cols] = (jnp.where(lo, outs[h - 2], outs[h - 1]) * gate[:, cols]).astype(BF16)

    n_lead = D_LEFT // tq
    for lead in range(n_lead):
        pl.when(qt == lead)(functools.partial(pipeline, (D_LEFT - lead * tq) // LANES))
    pl.when(qt >= n_lead)(functools.partial(pipeline, 0))


def _mixer_d(h, qg, kg, base, bsz, seq, tq=256):
    nt = seq // tq
    kern = functools.partial(_mixer_d_kernel, seq=seq, tq=tq)
    full = lambda shape: pl.BlockSpec(shape, lambda b, t: (0,) * len(shape))
    return pl.pallas_call(
        kern,
        grid=(bsz, nt),
        in_specs=[pl.BlockSpec((tq, GROUP_W), lambda b, t: (b * nt + t, BLK_D_Q)),
                  pl.BlockSpec((seq, GROUP_W), lambda b, t: (b, BLK_D_K)),
                  pl.BlockSpec((seq, GROUP_W), lambda b, t: (b, BLK_D_V)),
                  pl.BlockSpec((tq, GROUP_W), lambda b, t: (b * nt + t, BLK_D_Z)),
                  full((1, GROUP_W)), full((1, GROUP_W)), full((N_HEADS64, 2 * tq + D_LEFT))],
        out_specs=pl.BlockSpec((tq, GROUP_W), lambda b, t: (b * nt + t, 0)),
        out_shape=jax.ShapeDtypeStruct((bsz * seq, GROUP_W), BF16),
        scratch_shapes=[pltpu.VMEM((seq + D_LEFT, GROUP_W), BF16), pltpu.VMEM((seq + D_LEFT, GROUP_W), BF16),
                        pltpu.VMEM((N_HEADS64, tq, tq + D_LEFT), F32),
                        pltpu.VMEM((N_HEADS64, tq, tq + D_LEFT), F32), pltpu.VMEM((N_HEADS64, tq, LANES), F32)],
        compiler_params=_params(2),
    )(h, h, h, h, qg, kg, base)


def _w_in_pieces(take, zeros):
    c = lambda name, size, off=0: take(_SRC[name] + off, size)
    return [c("a_u", 512), c("a_v", 512), c("a_z", 512),
            c("b_q", 512), c("b_iq", 512), c("b_z", 512),
            c("b_k", 64), c("b_k", 64), c("b_v", 64), c("b_v", 64), c("b_ik", 64), c("b_ik", 64),
            c("c_kr", 32), c("b_iw", 8), zeros(24), c("c_kr", 32, 32), zeros(32),
            c("c_q", 384), c("c_kv", 128), c("c_z", 512),
            c("d_q", 512), c("d_k", 512), c("d_v", 512), c("d_z", 512)]


def _layout_w_in_kernel(w_ref, o_ref):
    tk = w_ref.shape[2]
    pieces = _w_in_pieces(lambda s, n: w_ref[0, s:s + n, :], lambda n: jnp.zeros((n, tk), F32))
    ends = np.cumsum([0] + [p.shape[0] for p in pieces])
    start = 0
    for i in range(1, len(pieces) + 1):
        if ends[i] % GROUP_W == 0:
            group = pieces[start:i]
            blk = group[0] if len(group) == 1 else jnp.concatenate(group, axis=0)
            o_ref[0, ends[start]:ends[i], :] = blk.astype(BF16)
            start = i


def _layout_w_in(w_in, tk=256):
    w_t = jnp.swapaxes(w_in, 1, 2)
    depth, cols, d = w_t.shape
    return pl.pallas_call(
        _layout_w_in_kernel,
        grid=(depth, d // tk),
        in_specs=[pl.BlockSpec((1, cols, tk), lambda l, i: (l, 0, i))],
        out_specs=pl.BlockSpec((1, H_COLS, tk), lambda l, i: (l, 0, i)),
        out_shape=jax.ShapeDtypeStruct((depth, H_COLS, d), BF16),
        compiler_params=_params(2),
    )(w_t)


def _rope_layout(v):
    z = jnp.zeros(v.shape[:-1] + (32,), v.dtype)
    return jnp.concatenate([v[..., :32], z, v[..., 32:], z], axis=-1)


def _layout_c(w_qb, w_kvb, q_gain, k_gain, qa_gain):
    wq = w_qb.reshape(Q_LORA, C_HEADS, C_QK)
    wq = jnp.concatenate([wq[..., :C_NOPE], _rope_layout(wq[..., C_NOPE:])], axis=-1)
    wq = wq.reshape(Q_LORA, C_HEADS * 256).astype(BF16)
    wkv = w_kvb.reshape(KV_LORA, C_HEADS, 2 * LANES)
    wkv = jnp.concatenate([wkv[..., :C_NOPE].reshape(KV_LORA, -1), wkv[..., C_NOPE:].reshape(KV_LORA, -1)],
                          axis=1).astype(BF16)
    lay = lambda g: jnp.concatenate([g[:C_NOPE], _rope_layout(g[C_NOPE:])])[None, :]
    return wq, wkv, lay(q_gain), lay(k_gain), qa_gain[None, :]


def _t5_bucket_static(rel):
    half = T5_BUCKETS // 2
    exact = half // 2
    n = abs(rel)
    if n < exact:
        val = n
    else:
        val = min(exact + (n * n // (exact * exact)).bit_length() - 1, half - 1)
    return (half if rel > 0 else 0) + val


def _t5_tables(t5_bias):
    m = np.arange(512)
    d0 = np.where(m < 256, m, m - 512)
    d1 = np.where(m <= 256, m - 256, m - 768)
    idx0 = np.array([_t5_bucket_static(int(d)) for d in d0], np.int32)
    idx1 = np.array([_t5_bucket_static(int(d)) for d in d1], np.int32)
    far = _t5_bucket_static(-512)
    return t5_bias[idx0].T, t5_bias[idx1].T, t5_bias[far]


def _band_table(rel_bias, tq):
    width = 2 * tq + D_LEFT
    m = np.arange(width)
    dist = np.where(m <= tq + D_LEFT, D_LEFT - m, D_LEFT + width - m)
    idx = np.clip(dist, -REL_CLIP, REL_CLIP) + REL_CLIP
    return rel_bias[idx.astype(np.int32)].T


def _rope_tables(seq):
    inv = ROPE_BASE ** (-jnp.arange(0, C_ROPE, 2, dtype=F32) / C_ROPE)
    ang = jnp.arange(seq, dtype=F32)[:, None] * inv[None, :]
    c, s = jnp.cos(ang), jnp.sin(ang)
    z = jnp.zeros_like(c)
    return jnp.concatenate([c, z, c, z], axis=1), jnp.concatenate([-s, z, s, z], axis=1)


def kernel(x, t5_bias, norm_g, w_in, a_v_gain, a_ws, a_bs, b_q_gain, b_k_gain, c_qa_gain, c_kva_gain,
           c_w_qb, c_w_kvb, c_q_gain, c_k_gain, d_q_gain, d_k_gain, d_rel_bias, w_out):
    bsz, seq, d_model = x.shape
    depth = w_in.shape[0]
    tq = 256
    assert seq % 512 == 0 and seq <= 2048 and d_model % 512 == 0, (seq, d_model)
    assert w_in.shape[2] == _SRC["d_z"] + GROUP_W and w_out.shape[1] == 4 * GROUP_W, (w_in.shape, w_out.shape)
    x2 = x.reshape(bsz * seq, d_model)
    cos, sin = _rope_tables(seq)
    base0, base1, cfar = _t5_tables(t5_bias)
    w_in_blocks = _layout_w_in(w_in)
    for l in range(depth):
        h = _inproj(x2, norm_g[l][None, :], w_in_blocks, l)
        mask = _b_select(h, bsz, seq)
        y_b = _b_attn(h, mask, cfar, jnp.tile(b_q_gain[l], N_HEADS64)[None, :],
                      jnp.tile(b_k_gain[l], 2)[None, :], base0, base1, bsz, seq, tq)
        wq, wkv, qg, kg, qag = _layout_c(c_w_qb[l], c_w_kvb[l], c_q_gain[l], c_k_gain[l], c_qa_gain[l])
        qc, kc, vc = _c_prep(h, wq, wkv, qag, c_kva_gain[l][None, :], qg, kg, cos, sin, seq)
        y_c = _c_attn(qc, kc, vc, h, bsz, seq, tq)
        y_d = _mixer_d(h, jnp.tile(d_q_gain[l], N_HEADS64)[None, :], jnp.tile(d_k_gain[l], N_HEADS64)[None, :],
                       _band_table(d_rel_bias[l], tq), bsz, seq, tq)
        x2 = _outproj(x2, h, a_v_gain[l][None, :], a_ws[l], a_bs[l][:, :, None], (y_b, y_c, y_d),
                      w_out[l].astype(BF16))
    return x2.reshape(bsz, seq, d_model)
```

```python
import functools
import math

import numpy as np
import jax
import jax.numpy as jnp
from jax import lax
from jax.experimental import pallas as pl
from jax.experimental.pallas import tpu as pltpu

F32 = jnp.float32
BF16 = jnp.bfloat16

EPS = 1e-6
NEG = -1e30
LOG2E = math.log2(math.e)
KEY_LOWEST = int(np.float32(-np.finfo(np.float32).max).view(np.int32)) ^ 0x7FFFFFFF
CHUNK = 64
LANES = 128
GROUP_W = 512
A_GROUPS = 4
GMLP_BLOCK = 128
N_HEADS64 = 8
IDX_SCALE = (8 ** -0.5) * 0.125
TOPK_MAX = 256
T5_BUCKETS = 32
C_HEADS = 4
C_NOPE = 128
C_ROPE = 64
C_QK = 192
Q_LORA = 384
KV_LORA = 128
ROPE_BASE = 10000.0
D_LEFT = 8 * CHUNK
REL_CLIP = 128
VMEM_LIMIT = 56 * 1024 * 1024
OUTPROJ_VMEM_LIMIT = 58 * 1024 * 1024

BLK_A_U, BLK_A_V, BLK_A_Z = 0, 1, 2
BLK_B_Q, BLK_B_IQ, BLK_B_Z, BLK_SMALL = 3, 4, 5, 6
BLK_C_QKV, BLK_C_Z = 7, 8
BLK_D_Q, BLK_D_K, BLK_D_V, BLK_D_Z = 9, 10, 11, 12
H_COLS = 13 * GROUP_W
UNIT_B_K, UNIT_B_V, UNIT_B_IK, UNIT_KR_IW = (BLK_SMALL * 4 + i for i in range(4))
IW_LANE = 32

_SRC = dict(a_u=0, a_v=512, a_z=1024, b_q=1536, b_k=2048, b_v=2112, b_iq=2176, b_ik=2688,
            b_iw=2752, b_z=2760, c_q=3272, c_kv=3656, c_kr=3784, c_z=3848,
            d_q=4360, d_k=4872, d_v=5384, d_z=5896)


def _params(n_axes, vmem_limit=VMEM_LIMIT):
    return pltpu.CompilerParams(dimension_semantics=("arbitrary",) * n_axes,
                                vmem_limit_bytes=vmem_limit)


def _gelu(x):
    c = math.sqrt(2.0 / math.pi)
    return x * (0.5 * (1.0 + jnp.tanh(c * (x + 0.044715 * (x * x * x)))))


def _silu(x):
    return x * (1.0 / (1.0 + jnp.exp(-x)))


def _dot_t(a, b):
    return lax.dot_general(a, b, (((1,), (1,)), ((), ())), preferred_element_type=F32)


def _lo_mask(rows):
    return lax.broadcasted_iota(jnp.int32, (rows, LANES), 1) < 64


def _rms_heads64(x, gain, ntiles):
    lo = _lo_mask(x.shape[0])
    tiles = []
    for t in range(ntiles):
        xt = x[:, t * LANES:(t + 1) * LANES]
        sq = xt * xt
        s_lo = jnp.sum(jnp.where(lo, sq, 0.0), axis=-1, keepdims=True)
        s_hi = jnp.sum(jnp.where(lo, 0.0, sq), axis=-1, keepdims=True)
        r = jnp.where(lo, lax.rsqrt(s_lo * (1.0 / 64) + EPS), lax.rsqrt(s_hi * (1.0 / 64) + EPS))
        tiles.append(xt * r * gain[:, t * LANES:(t + 1) * LANES])
    return tiles


def _for_blocks(blocks, count):
    def quad(i, carry):
        blocks(4 * i, 4)
        return carry

    lax.fori_loop(0, count // 4, quad, 0)
    first = (count // 4) * 4

    @pl.when((count & 2) != 0)
    def _():
        blocks(first, 2)

    @pl.when((count & 1) != 0)
    def _():
        blocks(first + (count & 2), 1)


def _toeplitz(base_row, rows, width):
    t = jnp.broadcast_to(base_row, (rows, base_row.shape[1]))
    t = pltpu.roll(t, 0, 1, stride=1, stride_axis=0)
    return t[:, :width]


def _inproj_kernel(x_ref, g_ref, w_ref, o_ref):
    x = x_ref[...]
    ms = jnp.mean(x * x, axis=-1, keepdims=True)
    xn = (x * lax.rsqrt(ms + EPS) * g_ref[...]).astype(BF16)
    for c in range(H_COLS // GROUP_W):
        cols = slice(c * GROUP_W, (c + 1) * GROUP_W)
        o_ref[:, cols] = _dot_t(xn, w_ref[cols, :]).astype(BF16)


def _inproj(x2, g, w_all, layer, tm=512):
    n, d = x2.shape
    return pl.pallas_call(
        _inproj_kernel,
        grid=(n // tm,),
        in_specs=[pl.BlockSpec((tm, d), lambda i: (i, 0)),
                  pl.BlockSpec((1, d), lambda i: (0, 0)),
                  pl.BlockSpec((None, H_COLS, d), lambda i: (layer, 0, 0), pipeline_mode=pl.Buffered(1))],
        out_specs=pl.BlockSpec((tm, H_COLS), lambda i: (i, 0)),
        out_shape=jax.ShapeDtypeStruct((n, H_COLS), BF16),
        compiler_params=_params(1),
    )(x2, g, w_all)


def _outproj_kernel(x_ref, uvz_ref, vg_ref, ws_ref, bs_ref, yb_ref, yc_ref, yd_ref, w_ref, o_ref, ya_ref):
    u_ref, v_ref, z_ref = (uvz_ref.at[:, blk * GROUP_W:(blk + 1) * GROUP_W] for blk in (BLK_A_U, BLK_A_V, BLK_A_Z))
    tm, d = x_ref.shape
    wgs = _mixer_a_weights(ws_ref)
    nblk = tm // GMLP_BLOCK
    ncol = d // nblk
    for c in range(nblk):
        cols = slice(c * ncol, (c + 1) * ncol)
        acc = x_ref[:, cols]
        for g, y_ref in ((1, yb_ref), (2, yc_ref), (3, yd_ref)):
            acc = acc + jnp.dot(y_ref[...], w_ref[g * GROUP_W:(g + 1) * GROUP_W, cols],
                                preferred_element_type=F32)
        o_ref[:, cols] = acc
        _mixer_a_rows(u_ref, v_ref, z_ref, vg_ref, wgs, bs_ref, ya_ref, c)
    for c in range(nblk):
        cols = slice(c * ncol, (c + 1) * ncol)
        o_ref[:, cols] += jnp.dot(ya_ref[...], w_ref[0:GROUP_W, cols], preferred_element_type=F32)


def _outproj(x2, h, vg, ws, bs, ys, w, tm=1024):
    n, d = x2.shape
    yspec = pl.BlockSpec((tm, GROUP_W), lambda i: (i, 0))
    return pl.pallas_call(
        _outproj_kernel,
        grid=(n // tm,),
        in_specs=[pl.BlockSpec((tm, d), lambda i: (i, 0)),
                  pl.BlockSpec((tm, 3 * GROUP_W), lambda i: (i, 0)),
                  pl.BlockSpec((1, GROUP_W), lambda i: (0, 0)),
                  pl.BlockSpec((A_GROUPS, GMLP_BLOCK, GMLP_BLOCK), lambda i: (0, 0, 0)),
                  pl.BlockSpec((A_GROUPS, GMLP_BLOCK, 1), lambda i: (0, 0, 0)),
                  yspec, yspec, yspec,
                  pl.BlockSpec((4 * GROUP_W, d), lambda i: (0, 0), pipeline_mode=pl.Buffered(1))],
        out_specs=pl.BlockSpec((tm, d), lambda i: (i, 0)),
        out_shape=jax.ShapeDtypeStruct((n, d), F32),
        scratch_shapes=[pltpu.VMEM((tm, GROUP_W), BF16)],
        compiler_params=_params(1, OUTPROJ_VMEM_LIMIT),
    )(x2, h, vg, ws, bs, *ys, w)


def _mixer_a_weights(w_ref):
    i = lax.broadcasted_iota(jnp.int32, (GMLP_BLOCK, GMLP_BLOCK), 0)
    j = lax.broadcasted_iota(jnp.int32, (GMLP_BLOCK, GMLP_BLOCK), 1)
    keep = (j // CHUNK) <= (i // CHUNK)
    return [jnp.where(keep, w_ref[g], 0.0).astype(BF16) for g in range(A_GROUPS)]


def _mixer_a_rows(u_ref, v_ref, z_ref, vg_ref, wgs, b_ref, o_ref, blk):
    rows = slice(blk * GMLP_BLOCK, (blk + 1) * GMLP_BLOCK)
    u = _gelu(u_ref[rows, :].astype(F32))
    v = _gelu(v_ref[rows, :].astype(F32))
    ms = jnp.mean(v * v, axis=-1, keepdims=True)
    vb = (v * lax.rsqrt(ms + EPS) * vg_ref[...]).astype(BF16)
    gate = _silu(z_ref[rows, :].astype(F32))
    for g in range(A_GROUPS):
        cols = slice(g * LANES, (g + 1) * LANES)
        sg = jnp.dot(wgs[g], vb[:, cols], preferred_element_type=F32) + b_ref[g]
        o_ref[rows, cols] = (u[:, cols] * sg * gate[:, cols]).astype(BF16)


def _order_key(x):
    return jnp.where(x < 0, x ^ 0x7FFFFFFF, x)


def _b_select_kernel(iq_ref, iw_ref, ik_ref, o_ref, lhs_ref, sc_ref, *, seq, tq, topk):
    t_blk = pl.program_id(1)
    nkb = seq // 256
    nblk = ((t_blk + 1) * tq) // 256
    n_interp = 12
    n_unchecked = 13 + 2 * t_blk
    hrows = [slice(h * tq, (h + 1) * tq) for h in range(N_HEADS64)]

    lo_half = _lo_mask(tq)
    w_t = (iw_ref[...].astype(F32) * IDX_SCALE).T
    for h in range(N_HEADS64):
        iqt = iq_ref[:, (h // 2) * LANES:(h // 2 + 1) * LANES]
        sel = lo_half if h % 2 == 0 else jnp.logical_not(lo_half)
        lhs_ref[hrows[h], :] = jnp.where(sel, iqt, jnp.zeros_like(iqt))

    def qpos(ln):
        return t_blk * tq + ln.start + lax.broadcasted_iota(jnp.int32, (256, ln.stop - ln.start), 1)

    def krow(ln):
        return lax.broadcasted_iota(jnp.int32, (256, ln.stop - ln.start), 0)

    assert tq in (256, 512)
    every = slice(0, tq)
    tail = slice(256, tq) if tq == 512 else None
    nwide = nblk - 1 if tail else nblk

    def fold8(x):
        return jnp.sum(x.reshape(256 // 8, 8, x.shape[1]), axis=0)

    def widen(part, ln):
        return part if ln == every else jnp.concatenate([jnp.zeros((8, ln.start), F32), part], axis=1)

    def score_block(kb, carry, masked, ln):
        amax, n_pos, n_nn = carry
        off = pl.multiple_of(kb * 256, 256)
        ikblk = ik_ref[pl.ds(off, 256), :]
        score = jnp.zeros((256, ln.stop - ln.start), F32)
        for h in range(N_HEADS64):
            w_h = w_t[IW_LANE + h:IW_LANE + h + 1, ln]
            q_h = lhs_ref[h * tq + ln.start:h * tq + ln.stop, :]
            score = score + w_h * jnp.maximum(_dot_t(ikblk, q_h), 0.0)
        mag = jnp.abs(score)
        if masked:
            adm = ((kb * 256 + krow(ln)) // CHUNK) <= (qpos(ln) // CHUNK)
            score = jnp.where(adm, score, -jnp.inf)
            mag = jnp.where(adm, mag, 0.0)
        sc_ref[kb, :, ln] = score
        return (jnp.maximum(amax, widen(jnp.max(mag.reshape(256 // 8, 8, mag.shape[1]), axis=0), ln)),
                n_pos + widen(fold8(jnp.where(score > 0.0, 1.0, 0.0)), ln),
                n_nn + widen(fold8(jnp.where(score >= 0.0, 1.0, 0.0)), ln))

    nfull = (t_blk * tq) // 256
    zeros8 = jnp.zeros((8, tq), F32)
    carry = lax.fori_loop(0, nfull, lambda kb, c: score_block(kb, c, False, every), (zeros8, zeros8, zeros8))
    carry = lax.fori_loop(nfull, nwide, lambda kb, c: score_block(kb, c, True, every), carry)
    if tail:
        carry = score_block(nblk - 1, carry, True, tail)
    amax = jnp.max(carry[0], axis=0, keepdims=True)
    f_pos = jnp.sum(carry[1], axis=0, keepdims=True)
    f_nn = jnp.sum(carry[2], axis=0, keepdims=True)

    def count(pred):
        def body(kb, acc):
            return acc + fold8(jnp.where(pred(sc_ref[kb], kb, every), 1.0, 0.0))
        acc = lax.fori_loop(0, nwide, body, zeros8)
        if tail:
            acc = acc + widen(fold8(jnp.where(pred(sc_ref[nblk - 1, :, tail], nblk - 1, tail), 1.0, 0.0)), tail)
        return jnp.sum(acc, axis=0, keepdims=True)

    kf = float(topk)
    qrow = t_blk * tq + lax.broadcasted_iota(jnp.int32, (1, tq), 1)
    n_adm = ((qrow // CHUNK + 1) * CHUNK).astype(F32)
    one = jnp.ones((1, tq), jnp.int32)
    pos = f_pos > kf
    neg = f_nn < kf
    lo0 = jnp.where(pos, one, _order_key(lax.bitcast_convert_type(-amax, jnp.int32)))
    hi0 = jnp.where(neg, one - 1, _order_key(lax.bitcast_convert_type(amax, jnp.int32)) + 1)
    w_lo0 = jnp.where(pos, f_pos, n_adm) - kf
    w_hi0 = kf - jnp.where(neg, f_nn, 0.0)
    all_sel = n_adm <= kf
    at_zero = jnp.logical_not(pos | neg)
    done0 = jnp.where(all_sel | at_zero | (hi0 == lo0 + 1), 1.0, 0.0)
    thr0 = jnp.where(all_sel, KEY_LOWEST, jnp.where(at_zero, jnp.where(f_pos == kf, one, one - 1), lo0))

    def as_score(key):
        return lax.bitcast_convert_type(_order_key(key), F32)

    def search_cond(st):
        return jnp.logical_and(st[0][0] < n_interp + 32, st[1] < 0.5)

    def search_step(st):
        it, lo, hi, w_lo, w_hi, side, done, thr = st
        lo_v = as_score(lo)
        hi_v = as_score(hi)
        c_v = lo_v + (hi_v - lo_v) * (w_lo / (w_lo + w_hi))
        c_interp = _order_key(lax.bitcast_convert_type(c_v, jnp.int32))
        c_mid = (lo >> 1) + (hi >> 1) + (lo & hi & 1)
        cand = jnp.where(it < n_interp, c_interp, c_mid)
        cand = jnp.minimum(jnp.maximum(cand, lo + 1), hi - 1)
        cand_v = as_score(cand)
        f = count(lambda s, kb, ln: s >= cand_v[:, ln])
        live = done < 0.5
        up = f > kf
        hit = f == kf
        new_lo = jnp.where(live & up, cand, lo)
        new_hi = jnp.where(live & jnp.logical_not(up), cand, hi)
        new_w_lo = jnp.where(up, f - kf, jnp.where(side < 0.0, 0.5 * w_lo, w_lo))
        new_w_hi = jnp.where(up, jnp.where(side > 0.0, 0.5 * w_hi, w_hi), kf - f)
        new_side = jnp.where(up, 1.0, -1.0)
        new_thr = jnp.where(live, jnp.where(hit, cand, new_lo), thr)
        new_done = jnp.where(live & (hit | (new_hi == new_lo + 1)), 1.0, done)
        return (it + 1, new_lo, new_hi, jnp.where(live, new_w_lo, w_lo),
                jnp.where(live, new_w_hi, w_hi), jnp.where(live, new_side, side), new_done, new_thr)

    def checked_step(st):
        new = search_step(st[0])
        return new, jnp.min(new[-2])

    state = (jnp.int32(0), lo0, hi0, w_lo0, w_hi0, jnp.zeros((1, tq), F32), done0, thr0)
    state = lax.fori_loop(0, n_unchecked, lambda i, st: search_step(st), state)
    thr = as_score(lax.while_loop(search_cond, checked_step, (state, jnp.min(state[-2])))[0][-1])

    any_excess = jnp.max(count(lambda s, kb, ln: s >= thr[:, ln])) > kf

    def store_mask(kb, keep_t, ln):
        for g in range(tq // 256):
            if g * 256 >= ln.start:
                o_ref[g, kb] = keep_t[:, g * 256 - ln.start:(g + 1) * 256 - ln.start]
            else:
                o_ref[g, kb] = jnp.zeros((256, 256), F32)

    def write_mask(keep):
        def body(kb, carry):
            store_mask(kb, jnp.where(keep(sc_ref[kb], kb, every), 1.0, 0.0), every)
            return carry

        lax.fori_loop(0, nwide, body, 0)
        if tail:
            store_mask(nblk - 1, jnp.where(keep(sc_ref[nblk - 1, :, tail], nblk - 1, tail), 1.0, 0.0), tail)

    def write_unused(kb, carry):
        store_mask(kb, jnp.zeros((256, tq), F32), every)
        return carry

    lax.fori_loop(nblk, nkb, write_unused, 0)

    @pl.when(jnp.logical_not(any_excess))
    def _():
        write_mask(lambda s, kb, ln: s >= thr[:, ln])

    @pl.when(any_excess)
    def _():
        need = kf - count(lambda s, kb, ln: s > thr[:, ln])

        def idx_step(it, jmax):
            cand = jmax | lax.shift_left(jnp.int32(1), 10 - it)
            below = count(lambda s, kb, ln: (s == thr[:, ln]) & ((kb * 256 + krow(ln)) < cand[:, ln]))
            return jnp.where(below < need, cand, jmax)

        jmax = lax.fori_loop(0, 11, idx_step, jnp.zeros((1, tq), jnp.int32))
        write_mask(lambda s, kb, ln: (s > thr[:, ln]) | ((s == thr[:, ln]) & ((kb * 256 + krow(ln)) <= jmax[:, ln])))


def _b_select(h, bsz, seq, tq=512):
    tq = min(tq, seq)
    nt = seq // tq
    topk = min(TOPK_MAX, seq // 4)
    kern = functools.partial(_b_select_kernel, seq=seq, tq=tq, topk=topk)
    return pl.pallas_call(
        kern,
        grid=(bsz, nt),
        in_specs=[pl.BlockSpec((tq, GROUP_W), lambda b, t: (b * nt + t, BLK_B_IQ)),
                  pl.BlockSpec((tq, LANES), lambda b, t: (b * nt + t, UNIT_KR_IW)),
                  pl.BlockSpec((seq, LANES), lambda b, t: (b, UNIT_B_IK))],
        out_specs=pl.BlockSpec((tq // 256, seq // 256, 256, 256), lambda b, t: (b * nt + t, 0, 0, 0)),
        out_shape=jax.ShapeDtypeStruct((bsz * seq // 256, seq // 256, 256, 256), F32),
        scratch_shapes=[pltpu.VMEM((N_HEADS64 * tq, LANES), BF16),
                        pltpu.VMEM((seq // 256, 256, tq), F32)],
        compiler_params=_params(2),
    )(h, h, h)


def _b_attn_kernel(cfar_ref, q_ref, z_ref, k_ref, v_ref, msk_ref, qg_ref, kg_ref, base0_ref, base1_ref,
                   o_ref, *scratch, seq, tq, sub):
    for j in range(sub):
        rows = slice(j * tq, (j + 1) * tq)
        _b_attn_tile(pl.program_id(1) * sub + j, cfar_ref, q_ref.at[rows, :], z_ref.at[rows, :], k_ref, v_ref,
                     msk_ref.at[j:j + 1], qg_ref, kg_ref, base0_ref, base1_ref, o_ref.at[rows, :], *scratch,
                     seq=seq, tq=tq)


def _b_attn_tile(t_blk, cfar_ref, q_ref, z_ref, k_ref, v_ref, msk_ref, qg_ref, kg_ref, base0_ref, base1_ref,
                 o_ref, kn_ref, v1_ref, bias_ref, qall_ref, s_ref, mp_ref, acc_ref, *, seq, tq):
    b = pl.program_id(0)
    hrows = [slice(h * tq, (h + 1) * tq) for h in range(N_HEADS64)]

    @pl.when((b == 0) & (t_blk == 0))
    def _():
        for h in range(N_HEADS64):
            bias_ref[0, hrows[h], :] = jnp.full((tq, 256), cfar_ref[h] * LOG2E, F32)
            bias_ref[1, hrows[h], :] = _toeplitz(base1_ref[h:h + 1, :], tq, 256) * LOG2E
            bias_ref[2, hrows[h], :] = _toeplitz(base0_ref[h:h + 1, :], tq, 256) * LOG2E

    @pl.when(t_blk == 0)
    def _():
        lo256 = _lo_mask(256)
        for r in range(seq // 256):
            rows = slice(r * 256, (r + 1) * 256)
            k = k_ref[rows, :].astype(F32)
            ms = jnp.mean(k * k, axis=-1, keepdims=True)
            kn_ref[rows, :] = (k * lax.rsqrt(ms + EPS) * kg_ref[...]).astype(BF16)
            v = v_ref[rows, :]
            v1_ref[rows, :] = jnp.where(lo256, v, jnp.ones_like(v))

    lo = _lo_mask(tq)
    qtiles = _rms_heads64(q_ref[...].astype(F32), qg_ref[...], 4)
    for h in range(N_HEADS64):
        sel = lo if h % 2 == 0 else jnp.logical_not(lo)
        qall_ref[hrows[h], :] = jnp.where(sel, qtiles[h // 2] * (0.125 * LOG2E), 0.0).astype(BF16)
    mp_ref[...] = jnp.full(mp_ref.shape, NEG, F32)
    acc_ref[...] = jnp.zeros(acc_ref.shape, F32)
    nblk = t_blk + 1

    def logits_blocks(kb0, n):
        for i in range(n):
            kb = kb0 + i
            off = pl.multiple_of(kb * 256, 256)
            kblk = kn_ref[pl.ds(off, 256), :]
            which = jnp.clip(kb - (t_blk - 2), 0, 2)
            keep = msk_ref[0, kb].T > 0.5
            for h in range(N_HEADS64):
                s = jnp.where(keep, _dot_t(qall_ref[hrows[h], :], kblk) + bias_ref[which, hrows[h], :], NEG)
                s_ref[kb, hrows[h], :] = s
                mp_ref[hrows[h], :] = jnp.maximum(mp_ref[hrows[h], :], jnp.maximum(s[:, :LANES], s[:, LANES:]))

    def value_blocks(kb0, n):
        off = pl.multiple_of(kb0 * 256, 256)
        v1 = v1_ref[pl.ds(off, n * 256), :]
        for h in range(N_HEADS64):
            m = mp_ref[hrows[h], :]
            mm = jnp.concatenate([m, m], axis=1)
            p = [jnp.exp2(s_ref[kb0 + i, hrows[h], :] - mm).astype(BF16) for i in range(n)]
            p = p[0] if n == 1 else jnp.concatenate(p, axis=1)
            acc_ref[hrows[h], :] += jnp.dot(p, v1, preferred_element_type=F32)

    _for_blocks(logits_blocks, nblk)
    for h in range(N_HEADS64):
        m = jnp.max(mp_ref[hrows[h], :], axis=-1, keepdims=True)
        mp_ref[hrows[h], :] = jnp.broadcast_to(m, (tq, LANES))
    _for_blocks(value_blocks, nblk)

    gate = _silu(z_ref[...].astype(F32))
    for t in range(4):
        a_even = acc_ref[hrows[2 * t], :]
        a_odd = acc_ref[hrows[2 * t + 1], :]
        o_even = a_even / pltpu.roll(a_even, 64, 1)
        o_odd = pltpu.roll(a_odd, 64, 1) / a_odd
        cols = slice(t * LANES, (t + 1) * LANES)
        o_ref[:, cols] = (jnp.where(lo, o_even, o_odd) * gate[:, cols]).astype(BF16)


def _b_attn(h, mask, cfar, qg, kg, base0, base1, bsz, seq, tq=256, sub=2):
    step = sub * tq
    nt = seq // step
    rows = N_HEADS64 * tq
    kern = functools.partial(_b_attn_kernel, seq=seq, tq=tq, sub=sub)
    full = lambda shape: pl.BlockSpec(shape, lambda b, t: (0,) * len(shape))
    return pl.pallas_call(
        kern,
        grid=(bsz, nt),
        in_specs=[pl.BlockSpec(memory_space=pltpu.SMEM),
                  pl.BlockSpec((step, GROUP_W), lambda b, t: (b * nt + t, BLK_B_Q)),
                  pl.BlockSpec((step, GROUP_W), lambda b, t: (b * nt + t, BLK_B_Z)),
                  pl.BlockSpec((seq, LANES), lambda b, t: (b, UNIT_B_K)),
                  pl.BlockSpec((seq, LANES), lambda b, t: (b, UNIT_B_V)),
                  pl.BlockSpec((sub, seq // 256, tq, 256), lambda b, t: (b * nt + t, 0, 0, 0)),
                  full((1, GROUP_W)), full((1, LANES)), full((N_HEADS64, 512)), full((N_HEADS64, 512))],
        out_specs=pl.BlockSpec((step, GROUP_W), lambda b, t: (b * nt + t, 0)),
        out_shape=jax.ShapeDtypeStruct((bsz * seq, GROUP_W), BF16),
        scratch_shapes=[pltpu.VMEM((seq, LANES), BF16), pltpu.VMEM((seq, LANES), BF16),
                        pltpu.VMEM((3, rows, 256), F32), pltpu.VMEM((rows, LANES), BF16),
                        pltpu.VMEM((seq // 256, rows, 256), F32),
                        pltpu.VMEM((rows, LANES), F32), pltpu.VMEM((rows, LANES), F32)],
        compiler_params=_params(2),
    )(cfar, h, h, h, h, mask, qg, kg, base0, base1)


def _rope(tile, cos, sin):
    return tile * cos + pltpu.roll(tile, 64, 1) * sin


def _c_prep_kernel(lat_ref, kr_ref, wq_ref, wkv_ref, qag_ref, kvag_ref, qg_ref, kg_ref, cos_ref, sin_ref,
                   qo_ref, ko_ref, vo_ref):
    cq = lat_ref[:, :Q_LORA].astype(F32)
    ms = jnp.mean(cq * cq, axis=-1, keepdims=True)
    cqn = (cq * lax.rsqrt(ms + EPS) * qag_ref[...]).astype(BF16)
    qpre = jnp.dot(cqn, wq_ref[...], preferred_element_type=F32)
    ckv = lat_ref[:, Q_LORA:].astype(F32)
    ms = jnp.mean(ckv * ckv, axis=-1, keepdims=True)
    ckvn = (ckv * lax.rsqrt(ms + EPS) * kvag_ref[...]).astype(BF16)
    kvpre = jnp.dot(ckvn, wkv_ref[...], preferred_element_type=F32)
    lane = lax.broadcasted_iota(jnp.int32, kr_ref.shape, 1)
    kr = jnp.where((lane % 64) < 32, kr_ref[...].astype(F32), 0.0)
    kr_ss = jnp.sum(kr * kr, axis=-1, keepdims=True)
    cos = cos_ref[...]
    sin = sin_ref[...]
    qg = qg_ref[...]
    kg = kg_ref[...]
    kr_rot = _rope(kr * kg[:, LANES:], cos, sin)
    for h in range(C_HEADS):
        qh = qpre[:, h * 256:(h + 1) * 256]
        r = lax.rsqrt(jnp.sum(qh * qh, axis=-1, keepdims=True) * (1.0 / C_QK) + EPS)
        qn = qh * r * qg
        qo_ref[:, h * 256:h * 256 + LANES] = qn[:, :LANES].astype(BF16)
        qo_ref[:, h * 256 + LANES:(h + 1) * 256] = _rope(qn[:, LANES:], cos, sin).astype(BF16)
        kn = kvpre[:, h * LANES:(h + 1) * LANES]
        r = lax.rsqrt((jnp.sum(kn * kn, axis=-1, keepdims=True) + kr_ss) * (1.0 / C_QK) + EPS)
        ko_ref[:, h * 256:h * 256 + LANES] = (kn * r * kg[:, :LANES]).astype(BF16)
        ko_ref[:, h * 256 + LANES:(h + 1) * 256] = (kr_rot * r).astype(BF16)
    vo_ref[...] = kvpre[:, C_HEADS * LANES:].astype(BF16)


def _c_prep(h, wq, wkv, qag, kvag, qg, kg, cos, sin, seq, tm=1024):
    n = h.shape[0]
    tm = min(tm, seq)
    ns = seq // tm
    full = lambda shape: pl.BlockSpec(shape, lambda i: (0,) * len(shape))
    return pl.pallas_call(
        _c_prep_kernel,
        grid=(n // tm,),
        in_specs=[pl.BlockSpec((tm, GROUP_W), lambda i: (i, BLK_C_QKV)),
                  pl.BlockSpec((tm, LANES), lambda i: (i, UNIT_KR_IW)),
                  full((Q_LORA, 4 * 256)), full((KV_LORA, 8 * LANES)),
                  full((1, Q_LORA)), full((1, LANES)), full((1, 256)), full((1, 256)),
                  pl.BlockSpec((tm, LANES), lambda i: (i % ns, 0)),
                  pl.BlockSpec((tm, LANES), lambda i: (i % ns, 0))],
        out_specs=[pl.BlockSpec((tm, 4 * 256), lambda i: (i, 0)),
                   pl.BlockSpec((tm, 4 * 256), lambda i: (i, 0)),
                   pl.BlockSpec((tm, GROUP_W), lambda i: (i, 0))],
        out_shape=[jax.ShapeDtypeStruct((n, 4 * 256), BF16), jax.ShapeDtypeStruct((n, 4 * 256), BF16),
                   jax.ShapeDtypeStruct((n, GROUP_W), BF16)],
        compiler_params=_params(1),
    )(h, h, wq, wkv, qag, kvag, qg, kg, cos, sin)


def _c_attn_kernel(q_ref, k_ref, v_ref, z_ref, o_ref, *scratch, tq, sub):
    for j in range(sub):
        rows = slice(j * tq, (j + 1) * tq)
        _c_attn_tile(pl.program_id(1) * sub + j, q_ref.at[rows, :], k_ref, v_ref, z_ref.at[rows, :],
                     o_ref.at[rows, :], *scratch, tq=tq)


def _c_attn_tile(qt, q_ref, k_ref, v_ref, z_ref, o_ref, s_ref, mp_ref, lp_ref, acc_ref, *, tq):
    scale = C_QK ** -0.5 * LOG2E
    hrows = [slice(h * tq, (h + 1) * tq) for h in range(C_HEADS)]
    qchunk = (qt * tq + lax.broadcasted_iota(jnp.int32, (tq, 256), 0)) // CHUNK
    kcol = lax.broadcasted_iota(jnp.int32, (tq, 256), 1)
    mp_ref[...] = jnp.full(mp_ref.shape, NEG, F32)
    lp_ref[...] = jnp.zeros(lp_ref.shape, F32)
    acc_ref[...] = jnp.zeros(acc_ref.shape, F32)
    nfull = (qt * tq) // 256
    nblk = ((qt + 1) * tq) // 256

    def logits_blocks(kb0, n, masked):
        for i in range(n):
            kb = kb0 + i
            off = pl.multiple_of(kb * 256, 256)
            for h in range(C_HEADS):
                cols = slice(h * 256, (h + 1) * 256)
                s = _dot_t(q_ref[:, cols], k_ref[pl.ds(off, 256), cols]) * scale
                if masked:
                    s = jnp.where(((kb * 256 + kcol) // CHUNK) <= qchunk, s, NEG)
                s_ref[kb, hrows[h], :] = s
                mp_ref[hrows[h], :] = jnp.maximum(mp_ref[hrows[h], :], jnp.maximum(s[:, :LANES], s[:, LANES:]))

    def value_blocks(kb0, n):
        off = pl.multiple_of(kb0 * 256, 256)
        for h in range(C_HEADS):
            m = mp_ref[hrows[h], :]
            mm = jnp.concatenate([m, m], axis=1)
            p = [jnp.exp2(s_ref[kb0 + i, hrows[h], :] - mm) for i in range(n)]
            lsum = p[0][:, :LANES] + p[0][:, LANES:]
            for pi in p[1:]:
                lsum = lsum + pi[:, :LANES] + pi[:, LANES:]
            lp_ref[hrows[h], :] += lsum
            pb = p[0].astype(BF16) if n == 1 else jnp.concatenate([pi.astype(BF16) for pi in p], axis=1)
            acc_ref[hrows[h], :] += jnp.dot(pb, v_ref[pl.ds(off, n * 256), h * LANES:(h + 1) * LANES],
                                            preferred_element_type=F32)

    _for_blocks(lambda kb0, n: logits_blocks(kb0, n, False), nfull)
    lax.fori_loop(nfull, nblk, lambda kb, c: (logits_blocks(kb, 1, True), c)[1], 0)
    for h in range(C_HEADS):
        m = jnp.max(mp_ref[hrows[h], :], axis=-1, keepdims=True)
        mp_ref[hrows[h], :] = jnp.broadcast_to(m, (tq, LANES))
    _for_blocks(value_blocks, nblk)
    gate = _silu(z_ref[...].astype(F32))
    for h in range(C_HEADS):
        cols = slice(h * LANES, (h + 1) * LANES)
        l = jnp.sum(lp_ref[hrows[h], :], axis=-1, keepdims=True)
        o_ref[:, cols] = (acc_ref[hrows[h], :] / l * gate[:, cols]).astype(BF16)


def _c_attn(qc, kc, vc, h, bsz, seq, tq=256, sub=2):
    step = sub * tq
    nt = seq // step
    rows = C_HEADS * tq
    kern = functools.partial(_c_attn_kernel, tq=tq, sub=sub)
    return pl.pallas_call(
        kern,
        grid=(bsz, nt),
        in_specs=[pl.BlockSpec((step, C_HEADS * 256), lambda b, t: (b * nt + t, 0)),
                  pl.BlockSpec((seq, C_HEADS * 256), lambda b, t: (b, 0)),
                  pl.BlockSpec((seq, GROUP_W), lambda b, t: (b, 0)),
                  pl.BlockSpec((step, GROUP_W), lambda b, t: (b * nt + t, BLK_C_Z))],
        out_specs=pl.BlockSpec((step, GROUP_W), lambda b, t: (b * nt + t, 0)),
        out_shape=jax.ShapeDtypeStruct((bsz * seq, GROUP_W), BF16),
        scratch_shapes=[pltpu.VMEM((seq // 256, rows, 256), F32), pltpu.VMEM((rows, LANES), F32),
                        pltpu.VMEM((rows, LANES), F32), pltpu.VMEM((rows, LANES), F32)],
        compiler_params=_params(2),
    )(qc, kc, vc, h)


def _mixer_d_kernel(q_ref, k_ref, v_ref, z_ref, qg_ref, kg_ref, base_ref, o_ref, *scratch, seq, tq, sub):
    for j in range(sub):
        rows = slice(j * tq, (j + 1) * tq)
        _mixer_d_tile(pl.program_id(1) * sub + j, q_ref.at[rows, :], k_ref, v_ref, z_ref.at[rows, :], qg_ref,
                      kg_ref, base_ref, o_ref.at[rows, :], *scratch, seq=seq, tq=tq)


def _mixer_d_tile(qt, q_ref, k_ref, v_ref, z_ref, qg_ref, kg_ref, base_ref, o_ref,
                  kpad_ref, vpad_ref, bias_ref, s_ref, mp_ref, *, seq, tq):
    b = pl.program_id(0)
    win = tq + D_LEFT

    @pl.when((b == 0) & (qt == 0))
    def _():
        qc = lax.broadcasted_iota(jnp.int32, (tq, win), 0) // CHUNK
        kc = lax.broadcasted_iota(jnp.int32, (tq, win), 1) // CHUNK
        band = (kc >= qc) & (kc <= qc + D_LEFT // CHUNK)
        for h in range(N_HEADS64):
            bias_ref[h] = jnp.where(band, _toeplitz(base_ref[h:h + 1, :], tq, win) * LOG2E, NEG)

    @pl.when(qt == 0)
    def _():
        kpad_ref[0:D_LEFT, :] = jnp.zeros((D_LEFT, GROUP_W), BF16)
        vpad_ref[0:D_LEFT, :] = jnp.zeros((D_LEFT, GROUP_W), BF16)
        for r in range(seq // 256):
            rows = slice(r * 256, (r + 1) * 256)
            dst = slice(D_LEFT + r * 256, D_LEFT + (r + 1) * 256)
            tiles = _rms_heads64(k_ref[rows, :].astype(F32), kg_ref[...], 4)
            for t in range(4):
                kpad_ref[dst, t * LANES:(t + 1) * LANES] = tiles[t].astype(BF16)
            vpad_ref[dst, :] = v_ref[rows, :]

    lo = _lo_mask(tq)
    qtiles = _rms_heads64(q_ref[...].astype(F32), qg_ref[...], 4)
    start = pl.multiple_of(qt * tq, tq)
    gate = _silu(z_ref[...].astype(F32))
    ntile = win // LANES

    def in_band(r, c):
        return c * LANES < r * CHUNK + D_LEFT + CHUNK and (c + 1) * LANES > r * CHUNK

    def pipeline(first):
        live = slice(first * LANES, win)
        wstart = pl.multiple_of(start + first * LANES, LANES)

        def logits(h):
            t, half = divmod(h, 2)
            kwin = kpad_ref[pl.ds(wstart, win - first * LANES), t * LANES:(t + 1) * LANES]
            sel = lo if half == 0 else jnp.logical_not(lo)
            qh = jnp.where(sel, qtiles[t] * (0.125 * LOG2E), 0.0).astype(BF16)
            s = _dot_t(qh, kwin) + bias_ref[h, :, live]
            s_ref[h, :, live] = s
            tiles = [s[:, i * LANES:(i + 1) * LANES] for i in range(ntile - first)]
            m = jnp.max(functools.reduce(jnp.maximum, tiles), axis=-1, keepdims=True)
            mp_ref[h] = jnp.broadcast_to(m, (tq, LANES))

        def values(h):
            vwin = vpad_ref[pl.ds(wstart, win - first * LANES), (h // 2) * LANES:(h // 2 + 1) * LANES]
            p_rows, l_rows = [], []
            for r in range(tq // CHUNK):
                rows = slice(r * CHUNK, (r + 1) * CHUNK)
                m = mp_ref[h, rows, :]
                tiles = [jnp.exp2(s_ref[h, rows, c * LANES:(c + 1) * LANES] - m) if in_band(r, c) else None
                         for c in range(first, ntile)]
                l_rows.append(functools.reduce(jnp.add, [t for t in tiles if t is not None]))
                p_rows.append(jnp.concatenate([jnp.zeros((CHUNK, LANES), BF16) if t is None else t.astype(BF16)
                                               for t in tiles], axis=1))
            l = jnp.sum(jnp.concatenate(l_rows, axis=0), axis=-1, keepdims=True)
            return jnp.dot(jnp.concatenate(p_rows, axis=0), vwin, preferred_element_type=F32) / l

        outs = []
        logits(0)
        for h in range(1, N_HEADS64 + 1):
            if h < N_HEADS64:
                logits(h)
            outs.append(values(h - 1))
            if h % 2 == 0:
                cols = slice((h // 2 - 1) * LANES, (h // 2) * LANES)
                o_ref[:, cols] = (jnp.where(lo, outs[h - 2], outs[h - 1]) * gate[:, cols]).astype(BF16)

    n_lead = D_LEFT // tq
    for lead in range(n_lead):
        pl.when(qt == lead)(functools.partial(pipeline, (D_LEFT - lead * tq) // LANES))
    pl.when(qt >= n_lead)(functools.partial(pipeline, 0))


def _mixer_d(h, qg, kg, base, bsz, seq, tq=256, sub=2):
    rows = sub * tq
    nt = seq // rows
    kern = functools.partial(_mixer_d_kernel, seq=seq, tq=tq, sub=sub)
    full = lambda shape: pl.BlockSpec(shape, lambda b, t: (0,) * len(shape))
    return pl.pallas_call(
        kern,
        grid=(bsz, nt),
        in_specs=[pl.BlockSpec((rows, GROUP_W), lambda b, t: (b * nt + t, BLK_D_Q)),
                  pl.BlockSpec((seq, GROUP_W), lambda b, t: (b, BLK_D_K)),
                  pl.BlockSpec((seq, GROUP_W), lambda b, t: (b, BLK_D_V)),
                  pl.BlockSpec((rows, GROUP_W), lambda b, t: (b * nt + t, BLK_D_Z)),
                  full((1, GROUP_W)), full((1, GROUP_W)), full((N_HEADS64, 2 * tq + D_LEFT))],
        out_specs=pl.BlockSpec((rows, GROUP_W), lambda b, t: (b * nt + t, 0)),
        out_shape=jax.ShapeDtypeStruct((bsz * seq, GROUP_W), BF16),
        scratch_shapes=[pltpu.VMEM((seq + D_LEFT, GROUP_W), BF16), pltpu.VMEM((seq + D_LEFT, GROUP_W), BF16),
                        pltpu.VMEM((N_HEADS64, tq, tq + D_LEFT), F32),
                        pltpu.VMEM((N_HEADS64, tq, tq + D_LEFT), F32), pltpu.VMEM((N_HEADS64, tq, LANES), F32)],
        compiler_params=_params(2),
    )(h, h, h, h, qg, kg, base)


def _w_in_pieces(take, zeros):
    c = lambda name, size, off=0: take(_SRC[name] + off, size)
    return [c("a_u", 512), c("a_v", 512), c("a_z", 512),
            c("b_q", 512), c("b_iq", 512), c("b_z", 512),
            c("b_k", 64), c("b_k", 64), c("b_v", 64), c("b_v", 64), c("b_ik", 64), c("b_ik", 64),
            c("c_kr", 32), c("b_iw", 8), zeros(24), c("c_kr", 32, 32), zeros(32),
            c("c_q", 384), c("c_kv", 128), c("c_z", 512),
            c("d_q", 512), c("d_k", 512), c("d_v", 512), c("d_z", 512)]


def _layout_w_in_kernel(w_ref, o_ref):
    tk = w_ref.shape[2]
    pieces = _w_in_pieces(lambda s, n: w_ref[0, s:s + n, :], lambda n: jnp.zeros((n, tk), F32))
    ends = np.cumsum([0] + [p.shape[0] for p in pieces])
    start = 0
    for i in range(1, len(pieces) + 1):
        if ends[i] % GROUP_W == 0:
            group = pieces[start:i]
            blk = group[0] if len(group) == 1 else jnp.concatenate(group, axis=0)
            o_ref[0, ends[start]:ends[i], :] = blk.astype(BF16)
            start = i


def _layout_w_in(w_in, tk=256):
    w_t = jnp.swapaxes(w_in, 1, 2)
    depth, cols, d = w_t.shape
    return pl.pallas_call(
        _layout_w_in_kernel,
        grid=(depth, d // tk),
        in_specs=[pl.BlockSpec((1, cols, tk), lambda l, i: (l, 0, i))],
        out_specs=pl.BlockSpec((1, H_COLS, tk), lambda l, i: (l, 0, i)),
        out_shape=jax.ShapeDtypeStruct((depth, H_COLS, d), BF16),
        compiler_params=_params(2),
    )(w_t)


def _rope_layout(v):
    z = jnp.zeros(v.shape[:-1] + (32,), v.dtype)
    return jnp.concatenate([v[..., :32], z, v[..., 32:], z], axis=-1)


def _layout_c(w_qb, w_kvb, q_gain, k_gain, qa_gain):
    wq = w_qb.reshape(Q_LORA, C_HEADS, C_QK)
    wq = jnp.concatenate([wq[..., :C_NOPE], _rope_layout(wq[..., C_NOPE:])], axis=-1)
    wq = wq.reshape(Q_LORA, C_HEADS * 256).astype(BF16)
    wkv = w_kvb.reshape(KV_LORA, C_HEADS, 2 * LANES)
    wkv = jnp.concatenate([wkv[..., :C_NOPE].reshape(KV_LORA, -1), wkv[..., C_NOPE:].reshape(KV_LORA, -1)],
                          axis=1).astype(BF16)
    lay = lambda g: jnp.concatenate([g[:C_NOPE], _rope_layout(g[C_NOPE:])])[None, :]
    return wq, wkv, lay(q_gain), lay(k_gain), qa_gain[None, :]


def _t5_bucket_static(rel):
    half = T5_BUCKETS // 2
    exact = half // 2
    n = abs(rel)
    if n < exact:
        val = n
    else:
        val = min(exact + (n * n // (exact * exact)).bit_length() - 1, half - 1)
    return (half if rel > 0 else 0) + val


def _t5_tables(t5_bias):
    m = np.arange(512)
    d0 = np.where(m < 256, m, m - 512)
    d1 = np.where(m <= 256, m - 256, m - 768)
    idx0 = np.array([_t5_bucket_static(int(d)) for d in d0], np.int32)
    idx1 = np.array([_t5_bucket_static(int(d)) for d in d1], np.int32)
    far = _t5_bucket_static(-512)
    return t5_bias[idx0].T, t5_bias[idx1].T, t5_bias[far]


def _band_table(rel_bias, tq):
    width = 2 * tq + D_LEFT
    m = np.arange(width)
    dist = np.where(m <= tq + D_LEFT, D_LEFT - m, D_LEFT + width - m)
    idx = np.clip(dist, -REL_CLIP, REL_CLIP) + REL_CLIP
    return rel_bias[idx.astype(np.int32)].T


def _rope_tables(seq):
    inv = ROPE_BASE ** (-jnp.arange(0, C_ROPE, 2, dtype=F32) / C_ROPE)
    ang = jnp.arange(seq, dtype=F32)[:, None] * inv[None, :]
    c, s = jnp.cos(ang), jnp.sin(ang)
    z = jnp.zeros_like(c)
    return jnp.concatenate([c, z, c, z], axis=1), jnp.concatenate([-s, z, s, z], axis=1)


def kernel(x, t5_bias, norm_g, w_in, a_v_gain, a_ws, a_bs, b_q_gain, b_k_gain, c_qa_gain, c_kva_gain,
           c_w_qb, c_w_kvb, c_q_gain, c_k_gain, d_q_gain, d_k_gain, d_rel_bias, w_out):
    bsz, seq, d_model = x.shape
    depth = w_in.shape[0]
    tq = 256
    assert seq % 512 == 0 and seq <= 2048 and d_model % 512 == 0, (seq, d_model)
    assert w_in.shape[2] == _SRC["d_z"] + GROUP_W and w_out.shape[1] == 4 * GROUP_W, (w_in.shape, w_out.shape)
    x2 = x.reshape(bsz * seq, d_model)
    cos, sin = _rope_tables(seq)
    base0, base1, cfar = _t5_tables(t5_bias)
    w_in_blocks = _layout_w_in(w_in)
    for l in range(depth):
        h = _inproj(x2, norm_g[l][None, :], w_in_blocks, l)
        mask = _b_select(h, bsz, seq)
        y_b = _b_attn(h, mask, cfar, jnp.tile(b_q_gain[l], N_HEADS64)[None, :],
                      jnp.tile(b_k_gain[l], 2)[None, :], base0, base1, bsz, seq, tq)
        wq, wkv, qg, kg, qag = _layout_c(c_w_qb[l], c_w_kvb[l], c_q_gain[l], c_k_gain[l], c_qa_gain[l])
        qc, kc, vc = _c_prep(h, wq, wkv, qag, c_kva_gain[l][None, :], qg, kg, cos, sin, seq)
        y_c = _c_attn(qc, kc, vc, h, bsz, seq, tq)
        y_d = _mixer_d(h, jnp.tile(d_q_gain[l], N_HEADS64)[None, :], jnp.tile(d_k_gain[l], N_HEADS64)[None, :],
                       _band_table(d_rel_bias[l], tq), bsz, seq, tq)
        x2 = _outproj(x2, h, a_v_gain[l][None, :], a_ws[l], a_bs[l][:, :, None], (y_b, y_c, y_d),
                      w_out[l].astype(BF16))
    return x2.reshape(bsz, seq, d_model)
```

```python
import functools
import math

import numpy as np
import jax
import jax.numpy as jnp
from jax import lax
from jax.experimental import pallas as pl
from jax.experimental.pallas import tpu as pltpu

F32 = jnp.float32
BF16 = jnp.bfloat16

EPS = 1e-6
NEG = -1e30
LOG2E = math.log2(math.e)
KEY_LOWEST = int(np.float32(-np.finfo(np.float32).max).view(np.int32)) ^ 0x7FFFFFFF
CHUNK = 64
LANES = 128
GROUP_W = 512
A_GROUPS = 4
GMLP_BLOCK = 128
N_HEADS64 = 8
IDX_SCALE = (8 ** -0.5) * 0.125
TOPK_MAX = 256
T5_BUCKETS = 32
C_HEADS = 4
C_NOPE = 128
C_ROPE = 64
C_QK = 192
Q_LORA = 384
KV_LORA = 128
ROPE_BASE = 10000.0
D_LEFT = 8 * CHUNK
REL_CLIP = 128
VMEM_LIMIT = 56 * 1024 * 1024
OUTPROJ_VMEM_LIMIT = 58 * 1024 * 1024

BLK_A_U, BLK_A_V, BLK_A_Z = 0, 1, 2
BLK_B_Q, BLK_B_IQ, BLK_B_Z, BLK_SMALL = 3, 4, 5, 6
BLK_C_QKV, BLK_C_Z = 7, 8
BLK_D_Q, BLK_D_K, BLK_D_V, BLK_D_Z = 9, 10, 11, 12
H_COLS = 13 * GROUP_W
UNIT_B_K, UNIT_B_V, UNIT_B_IK, UNIT_KR_IW = (BLK_SMALL * 4 + i for i in range(4))
IW_LANE = 32

_SRC = dict(a_u=0, a_v=512, a_z=1024, b_q=1536, b_k=2048, b_v=2112, b_iq=2176, b_ik=2688,
            b_iw=2752, b_z=2760, c_q=3272, c_kv=3656, c_kr=3784, c_z=3848,
            d_q=4360, d_k=4872, d_v=5384, d_z=5896)


def _params(n_axes, vmem_limit=VMEM_LIMIT):
    return pltpu.CompilerParams(dimension_semantics=("arbitrary",) * n_axes,
                                vmem_limit_bytes=vmem_limit)


def _gelu(x):
    c = math.sqrt(2.0 / math.pi)
    return x * (0.5 * (1.0 + jnp.tanh(c * (x + 0.044715 * (x * x * x)))))


def _silu(x):
    return x * (1.0 / (1.0 + jnp.exp(-x)))


def _dot_t(a, b):
    return lax.dot_general(a, b, (((1,), (1,)), ((), ())), preferred_element_type=F32)


def _lo_mask(rows):
    return lax.broadcasted_iota(jnp.int32, (rows, LANES), 1) < 64


def _rms_heads64(x, gain, ntiles):
    lo = _lo_mask(x.shape[0])
    tiles = []
    for t in range(ntiles):
        xt = x[:, t * LANES:(t + 1) * LANES]
        sq = xt * xt
        s_lo = jnp.sum(jnp.where(lo, sq, 0.0), axis=-1, keepdims=True)
        s_hi = jnp.sum(jnp.where(lo, 0.0, sq), axis=-1, keepdims=True)
        r = jnp.where(lo, lax.rsqrt(s_lo * (1.0 / 64) + EPS), lax.rsqrt(s_hi * (1.0 / 64) + EPS))
        tiles.append(xt * r * gain[:, t * LANES:(t + 1) * LANES])
    return tiles


def _for_blocks(blocks, count):
    def quad(i, carry):
        blocks(4 * i, 4)
        return carry

    lax.fori_loop(0, count // 4, quad, 0)
    first = (count // 4) * 4

    @pl.when((count & 2) != 0)
    def _():
        blocks(first, 2)

    @pl.when((count & 1) != 0)
    def _():
        blocks(first + (count & 2), 1)


def _toeplitz(base_row, rows, width):
    t = jnp.broadcast_to(base_row, (rows, base_row.shape[1]))
    t = pltpu.roll(t, 0, 1, stride=1, stride_axis=0)
    return t[:, :width]


def _inproj_kernel(x_ref, g_ref, w_ref, o_ref):
    x = x_ref[...]
    ms = jnp.mean(x * x, axis=-1, keepdims=True)
    xn = (x * lax.rsqrt(ms + EPS) * g_ref[...]).astype(BF16)
    for c in range(H_COLS // GROUP_W):
        cols = slice(c * GROUP_W, (c + 1) * GROUP_W)
        o_ref[:, cols] = _dot_t(xn, w_ref[cols, :]).astype(BF16)


def _inproj(x2, g, w_all, layer, tm=512):
    n, d = x2.shape
    return pl.pallas_call(
        _inproj_kernel,
        grid=(n // tm,),
        in_specs=[pl.BlockSpec((tm, d), lambda i: (i, 0)),
                  pl.BlockSpec((1, d), lambda i: (0, 0)),
                  pl.BlockSpec((None, H_COLS, d), lambda i: (layer, 0, 0), pipeline_mode=pl.Buffered(1))],
        out_specs=pl.BlockSpec((tm, H_COLS), lambda i: (i, 0)),
        out_shape=jax.ShapeDtypeStruct((n, H_COLS), BF16),
        compiler_params=_params(1),
    )(x2, g, w_all)


def _outproj_kernel(x_ref, uvz_ref, vg_ref, ws_ref, bs_ref, yb_ref, yc_ref, yd_ref, w_ref, o_ref, ya_ref):
    u_ref, v_ref, z_ref = (uvz_ref.at[:, blk * GROUP_W:(blk + 1) * GROUP_W] for blk in (BLK_A_U, BLK_A_V, BLK_A_Z))
    tm, d = x_ref.shape
    wgs = _mixer_a_weights(ws_ref)
    nblk = tm // GMLP_BLOCK
    ncol = d // nblk
    for c in range(nblk):
        cols = slice(c * ncol, (c + 1) * ncol)
        acc = x_ref[:, cols]
        for g, y_ref in ((1, yb_ref), (2, yc_ref), (3, yd_ref)):
            acc = acc + jnp.dot(y_ref[...], w_ref[g * GROUP_W:(g + 1) * GROUP_W, cols],
                                preferred_element_type=F32)
        o_ref[:, cols] = acc
        _mixer_a_rows(u_ref, v_ref, z_ref, vg_ref, wgs, bs_ref, ya_ref, c)
    for c in range(nblk):
        cols = slice(c * ncol, (c + 1) * ncol)
        o_ref[:, cols] += jnp.dot(ya_ref[...], w_ref[0:GROUP_W, cols], preferred_element_type=F32)


def _outproj(x2, h, vg, ws, bs, ys, w, tm=1024):
    n, d = x2.shape
    yspec = pl.BlockSpec((tm, GROUP_W), lambda i: (i, 0))
    return pl.pallas_call(
        _outproj_kernel,
        grid=(n // tm,),
        in_specs=[pl.BlockSpec((tm, d), lambda i: (i, 0)),
                  pl.BlockSpec((tm, 3 * GROUP_W), lambda i: (i, 0)),
                  pl.BlockSpec((1, GROUP_W), lambda i: (0, 0)),
                  pl.BlockSpec((A_GROUPS, GMLP_BLOCK, GMLP_BLOCK), lambda i: (0, 0, 0)),
                  pl.BlockSpec((A_GROUPS, GMLP_BLOCK, 1), lambda i: (0, 0, 0)),
                  yspec, yspec, yspec,
                  pl.BlockSpec((4 * GROUP_W, d), lambda i: (0, 0), pipeline_mode=pl.Buffered(1))],
        out_specs=pl.BlockSpec((tm, d), lambda i: (i, 0)),
        out_shape=jax.ShapeDtypeStruct((n, d), F32),
        scratch_shapes=[pltpu.VMEM((tm, GROUP_W), BF16)],
        compiler_params=_params(1, OUTPROJ_VMEM_LIMIT),
    )(x2, h, vg, ws, bs, *ys, w)


def _mixer_a_weights(w_ref):
    i = lax.broadcasted_iota(jnp.int32, (GMLP_BLOCK, GMLP_BLOCK), 0)
    j = lax.broadcasted_iota(jnp.int32, (GMLP_BLOCK, GMLP_BLOCK), 1)
    keep = (j // CHUNK) <= (i // CHUNK)
    return [jnp.where(keep, w_ref[g], 0.0).astype(BF16) for g in range(A_GROUPS)]


def _mixer_a_rows(u_ref, v_ref, z_ref, vg_ref, wgs, b_ref, o_ref, blk):
    rows = slice(blk * GMLP_BLOCK, (blk + 1) * GMLP_BLOCK)
    u = _gelu(u_ref[rows, :].astype(F32))
    v = _gelu(v_ref[rows, :].astype(F32))
    ms = jnp.mean(v * v, axis=-1, keepdims=True)
    vb = (v * lax.rsqrt(ms + EPS) * vg_ref[...]).astype(BF16)
    gate = _silu(z_ref[rows, :].astype(F32))
    for g in range(A_GROUPS):
        cols = slice(g * LANES, (g + 1) * LANES)
        sg = jnp.dot(wgs[g], vb[:, cols], preferred_element_type=F32) + b_ref[g]
        o_ref[rows, cols] = (u[:, cols] * sg * gate[:, cols]).astype(BF16)


def _order_key(x):
    return jnp.where(x < 0, x ^ 0x7FFFFFFF, x)


def _b_select_kernel(iq_ref, iw_ref, ik_ref, o_ref, lhs_ref, sc_ref, *, seq, tq, topk):
    t_blk = pl.program_id(1)
    nkb = seq // 256
    nblk = ((t_blk + 1) * tq) // 256
    n_interp = 12
    n_unchecked = 13 + 2 * t_blk
    hrows = [slice(h * tq, (h + 1) * tq) for h in range(N_HEADS64)]

    lo_half = _lo_mask(tq)
    w_t = (iw_ref[...].astype(F32) * IDX_SCALE).T
    for h in range(N_HEADS64):
        iqt = iq_ref[:, (h // 2) * LANES:(h // 2 + 1) * LANES]
        sel = lo_half if h % 2 == 0 else jnp.logical_not(lo_half)
        lhs_ref[hrows[h], :] = jnp.where(sel, iqt, jnp.zeros_like(iqt))

    def qpos(ln):
        return t_blk * tq + ln.start + lax.broadcasted_iota(jnp.int32, (256, ln.stop - ln.start), 1)

    def krow(ln):
        return lax.broadcasted_iota(jnp.int32, (256, ln.stop - ln.start), 0)

    assert tq in (256, 512)
    every = slice(0, tq)
    tail = slice(256, tq) if tq == 512 else None
    nwide = nblk - 1 if tail else nblk

    def fold8(x):
        return jnp.sum(x.reshape(256 // 8, 8, x.shape[1]), axis=0)

    def widen(part, ln):
        return part if ln == every else jnp.concatenate([jnp.zeros((8, ln.start), F32), part], axis=1)

    def score_block(kb, carry, masked, ln):
        amax, n_pos, n_nn = carry
        off = pl.multiple_of(kb * 256, 256)
        ikblk = ik_ref[pl.ds(off, 256), :]
        score = jnp.zeros((256, ln.stop - ln.start), F32)
        for h in range(N_HEADS64):
            w_h = w_t[IW_LANE + h:IW_LANE + h + 1, ln]
            q_h = lhs_ref[h * tq + ln.start:h * tq + ln.stop, :]
            score = score + w_h * jnp.maximum(_dot_t(ikblk, q_h), 0.0)
        mag = jnp.abs(score)
        if masked:
            adm = ((kb * 256 + krow(ln)) // CHUNK) <= (qpos(ln) // CHUNK)
            score = jnp.where(adm, score, -jnp.inf)
            mag = jnp.where(adm, mag, 0.0)
        sc_ref[kb, :, ln] = score
        return (jnp.maximum(amax, widen(jnp.max(mag.reshape(256 // 8, 8, mag.shape[1]), axis=0), ln)),
                n_pos + widen(fold8(jnp.where(score > 0.0, 1.0, 0.0)), ln),
                n_nn + widen(fold8(jnp.where(score >= 0.0, 1.0, 0.0)), ln))

    nfull = (t_blk * tq) // 256
    zeros8 = jnp.zeros((8, tq), F32)
    carry = lax.fori_loop(0, nfull, lambda kb, c: score_block(kb, c, False, every), (zeros8, zeros8, zeros8))
    carry = lax.fori_loop(nfull, nwide, lambda kb, c: score_block(kb, c, True, every), carry)
    if tail:
        carry = score_block(nblk - 1, carry, True, tail)
    amax = jnp.max(carry[0], axis=0, keepdims=True)
    f_pos = jnp.sum(carry[1], axis=0, keepdims=True)
    f_nn = jnp.sum(carry[2], axis=0, keepdims=True)

    def count(pred):
        def body(kb, acc):
            return acc + fold8(jnp.where(pred(sc_ref[kb], kb, every), 1.0, 0.0))
        acc = lax.fori_loop(0, nwide, body, zeros8)
        if tail:
            acc = acc + widen(fold8(jnp.where(pred(sc_ref[nblk - 1, :, tail], nblk - 1, tail), 1.0, 0.0)), tail)
        return jnp.sum(acc, axis=0, keepdims=True)

    kf = float(topk)
    qrow = t_blk * tq + lax.broadcasted_iota(jnp.int32, (1, tq), 1)
    n_adm = ((qrow // CHUNK + 1) * CHUNK).astype(F32)
    one = jnp.ones((1, tq), jnp.int32)
    pos = f_pos > kf
    neg = f_nn < kf
    lo0 = jnp.where(pos, one, _order_key(lax.bitcast_convert_type(-amax, jnp.int32)))
    hi0 = jnp.where(neg, one - 1, _order_key(lax.bitcast_convert_type(amax, jnp.int32)) + 1)
    w_lo0 = jnp.where(pos, f_pos, n_adm) - kf
    w_hi0 = kf - jnp.where(neg, f_nn, 0.0)
    all_sel = n_adm <= kf
    at_zero = jnp.logical_not(pos | neg)
    done0 = jnp.where(all_sel | at_zero | (hi0 == lo0 + 1), 1.0, 0.0)
    thr0 = jnp.where(all_sel, KEY_LOWEST, jnp.where(at_zero, jnp.where(f_pos == kf, one, one - 1), lo0))

    def as_score(key):
        return lax.bitcast_convert_type(_order_key(key), F32)

    def search_cond(st):
        return jnp.logical_and(st[0][0] < n_interp + 32, st[1] < 0.5)

    def search_step(st):
        it, lo, hi, w_lo, w_hi, side, done, thr = st
        lo_v = as_score(lo)
        hi_v = as_score(hi)
        c_v = lo_v + (hi_v - lo_v) * (w_lo / (w_lo + w_hi))
        c_interp = _order_key(lax.bitcast_convert_type(c_v, jnp.int32))
        c_mid = (lo >> 1) + (hi >> 1) + (lo & hi & 1)
        cand = jnp.where(it < n_interp, c_interp, c_mid)
        cand = jnp.minimum(jnp.maximum(cand, lo + 1), hi - 1)
        cand_v = as_score(cand)
        f = count(lambda s, kb, ln: s >= cand_v[:, ln])
        live = done < 0.5
        up = f > kf
        hit = f == kf
        new_lo = jnp.where(live & up, cand, lo)
        new_hi = jnp.where(live & jnp.logical_not(up), cand, hi)
        new_w_lo = jnp.where(up, f - kf, jnp.where(side < 0.0, 0.5 * w_lo, w_lo))
        new_w_hi = jnp.where(up, jnp.where(side > 0.0, 0.5 * w_hi, w_hi), kf - f)
        new_side = jnp.where(up, 1.0, -1.0)
        new_thr = jnp.where(live, jnp.where(hit, cand, new_lo), thr)
        new_done = jnp.where(live & (hit | (new_hi == new_lo + 1)), 1.0, done)
        return (it + 1, new_lo, new_hi, jnp.where(live, new_w_lo, w_lo),
                jnp.where(live, new_w_hi, w_hi), jnp.where(live, new_side, side), new_done, new_thr)

    def checked_step(st):
        new = search_step(st[0])
        return new, jnp.min(new[-2])

    state = (jnp.int32(0), lo0, hi0, w_lo0, w_hi0, jnp.zeros((1, tq), F32), done0, thr0)
    state = lax.fori_loop(0, n_unchecked, lambda i, st: search_step(st), state)
    thr = as_score(lax.while_loop(search_cond, checked_step, (state, jnp.min(state[-2])))[0][-1])

    any_excess = jnp.max(count(lambda s, kb, ln: s >= thr[:, ln])) > kf

    def store_mask(kb, keep_t, ln):
        for g in range(tq // 256):
            if g * 256 >= ln.start:
                o_ref[g, kb] = keep_t[:, g * 256 - ln.start:(g + 1) * 256 - ln.start]
            else:
                o_ref[g, kb] = jnp.zeros((256, 256), F32)

    def write_mask(keep):
        def body(kb, carry):
            store_mask(kb, jnp.where(keep(sc_ref[kb], kb, every), 1.0, 0.0), every)
            return carry

        lax.fori_loop(0, nwide, body, 0)
        if tail:
            store_mask(nblk - 1, jnp.where(keep(sc_ref[nblk - 1, :, tail], nblk - 1, tail), 1.0, 0.0), tail)

    def write_unused(kb, carry):
        store_mask(kb, jnp.zeros((256, tq), F32), every)
        return carry

    lax.fori_loop(nblk, nkb, write_unused, 0)

    @pl.when(jnp.logical_not(any_excess))
    def _():
        write_mask(lambda s, kb, ln: s >= thr[:, ln])

    @pl.when(any_excess)
    def _():
        need = kf - count(lambda s, kb, ln: s > thr[:, ln])

        def idx_step(it, jmax):
            cand = jmax | lax.shift_left(jnp.int32(1), 10 - it)
            below = count(lambda s, kb, ln: (s == thr[:, ln]) & ((kb * 256 + krow(ln)) < cand[:, ln]))
            return jnp.where(below < need, cand, jmax)

        jmax = lax.fori_loop(0, 11, idx_step, jnp.zeros((1, tq), jnp.int32))
        write_mask(lambda s, kb, ln: (s > thr[:, ln]) | ((s == thr[:, ln]) & ((kb * 256 + krow(ln)) <= jmax[:, ln])))


def _b_select(h, bsz, seq, tq=512):
    tq = min(tq, seq)
    nt = seq // tq
    topk = min(TOPK_MAX, seq // 4)
    kern = functools.partial(_b_select_kernel, seq=seq, tq=tq, topk=topk)
    return pl.pallas_call(
        kern,
        grid=(bsz, nt),
        in_specs=[pl.BlockSpec((tq, GROUP_W), lambda b, t: (b * nt + t, BLK_B_IQ)),
                  pl.BlockSpec((tq, LANES), lambda b, t: (b * nt + t, UNIT_KR_IW)),
                  pl.BlockSpec((seq, LANES), lambda b, t: (b, UNIT_B_IK))],
        out_specs=pl.BlockSpec((tq // 256, seq // 256, 256, 256), lambda b, t: (b * nt + t, 0, 0, 0)),
        out_shape=jax.ShapeDtypeStruct((bsz * seq // 256, seq // 256, 256, 256), F32),
        scratch_shapes=[pltpu.VMEM((N_HEADS64 * tq, LANES), BF16),
                        pltpu.VMEM((seq // 256, 256, tq), F32)],
        compiler_params=_params(2),
    )(h, h, h)


def _b_attn_kernel(cfar_ref, q_ref, z_ref, k_ref, v_ref, msk_ref, qg_ref, kg_ref, base0_ref, base1_ref,
                   o_ref, *scratch, seq, tq, sub):
    for j in range(sub):
        rows = slice(j * tq, (j + 1) * tq)
        _b_attn_tile(pl.program_id(1) * sub + j, cfar_ref, q_ref.at[rows, :], z_ref.at[rows, :], k_ref, v_ref,
                     msk_ref.at[j:j + 1], qg_ref, kg_ref, base0_ref, base1_ref, o_ref.at[rows, :], *scratch,
                     seq=seq, tq=tq)


def _b_attn_tile(t_blk, cfar_ref, q_ref, z_ref, k_ref, v_ref, msk_ref, qg_ref, kg_ref, base0_ref, base1_ref,
                 o_ref, kn_ref, v1_ref, bias_ref, qall_ref, s_ref, mp_ref, acc_ref, *, seq, tq):
    b = pl.program_id(0)
    hrows = [slice(h * tq, (h + 1) * tq) for h in range(N_HEADS64)]

    @pl.when((b == 0) & (t_blk == 0))
    def _():
        for h in range(N_HEADS64):
            bias_ref[0, hrows[h], :] = jnp.full((tq, 256), cfar_ref[h] * LOG2E, F32)
            bias_ref[1, hrows[h], :] = _toeplitz(base1_ref[h:h + 1, :], tq, 256) * LOG2E
            bias_ref[2, hrows[h], :] = _toeplitz(base0_ref[h:h + 1, :], tq, 256) * LOG2E

    @pl.when(t_blk == 0)
    def _():
        lo256 = _lo_mask(256)
        for r in range(seq // 256):
            rows = slice(r * 256, (r + 1) * 256)
            k = k_ref[rows, :].astype(F32)
            ms = jnp.mean(k * k, axis=-1, keepdims=True)
            kn_ref[rows, :] = (k * lax.rsqrt(ms + EPS) * kg_ref[...]).astype(BF16)
            v = v_ref[rows, :]
            v1_ref[rows, :] = jnp.where(lo256, v, jnp.ones_like(v))

    lo = _lo_mask(tq)
    qtiles = _rms_heads64(q_ref[...].astype(F32), qg_ref[...], 4)
    for h in range(N_HEADS64):
        sel = lo if h % 2 == 0 else jnp.logical_not(lo)
        qall_ref[hrows[h], :] = jnp.where(sel, qtiles[h // 2] * (0.125 * LOG2E), 0.0).astype(BF16)
    mp_ref[...] = jnp.full(mp_ref.shape, NEG, F32)
    acc_ref[...] = jnp.zeros(acc_ref.shape, F32)
    nblk = t_blk + 1

    def logits_blocks(kb0, n):
        for i in range(n):
            kb = kb0 + i
            off = pl.multiple_of(kb * 256, 256)
            kblk = kn_ref[pl.ds(off, 256), :]
            which = jnp.clip(kb - (t_blk - 2), 0, 2)
            keep = msk_ref[0, kb].T > 0.5
            for h in range(N_HEADS64):
                s = jnp.where(keep, _dot_t(qall_ref[hrows[h], :], kblk) + bias_ref[which, hrows[h], :], NEG)
                s_ref[kb, hrows[h], :] = s
                mp_ref[hrows[h], :] = jnp.maximum(mp_ref[hrows[h], :], jnp.maximum(s[:, :LANES], s[:, LANES:]))

    def value_blocks(kb0, n):
        off = pl.multiple_of(kb0 * 256, 256)
        v1 = v1_ref[pl.ds(off, n * 256), :]
        for h in range(N_HEADS64):
            m = mp_ref[hrows[h], :]
            mm = jnp.concatenate([m, m], axis=1)
            p = [jnp.exp2(s_ref[kb0 + i, hrows[h], :] - mm).astype(BF16) for i in range(n)]
            p = p[0] if n == 1 else jnp.concatenate(p, axis=1)
            acc_ref[hrows[h], :] += jnp.dot(p, v1, preferred_element_type=F32)

    _for_blocks(logits_blocks, nblk)
    for h in range(N_HEADS64):
        m = jnp.max(mp_ref[hrows[h], :], axis=-1, keepdims=True)
        mp_ref[hrows[h], :] = jnp.broadcast_to(m, (tq, LANES))
    _for_blocks(value_blocks, nblk)

    gate = _silu(z_ref[...].astype(F32))
    for t in range(4):
        a_even = acc_ref[hrows[2 * t], :]
        a_odd = acc_ref[hrows[2 * t + 1], :]
        o_even = a_even / pltpu.roll(a_even, 64, 1)
        o_odd = pltpu.roll(a_odd, 64, 1) / a_odd
        cols = slice(t * LANES, (t + 1) * LANES)
        o_ref[:, cols] = (jnp.where(lo, o_even, o_odd) * gate[:, cols]).astype(BF16)


def _b_attn(h, mask, cfar, qg, kg, base0, base1, bsz, seq, tq=256, sub=2):
    step = sub * tq
    nt = seq // step
    rows = N_HEADS64 * tq
    kern = functools.partial(_b_attn_kernel, seq=seq, tq=tq, sub=sub)
    full = lambda shape: pl.BlockSpec(shape, lambda b, t: (0,) * len(shape))
    return pl.pallas_call(
        kern,
        grid=(bsz, nt),
        in_specs=[pl.BlockSpec(memory_space=pltpu.SMEM),
                  pl.BlockSpec((step, GROUP_W), lambda b, t: (b * nt + t, BLK_B_Q)),
                  pl.BlockSpec((step, GROUP_W), lambda b, t: (b * nt + t, BLK_B_Z)),
                  pl.BlockSpec((seq, LANES), lambda b, t: (b, UNIT_B_K)),
                  pl.BlockSpec((seq, LANES), lambda b, t: (b, UNIT_B_V)),
                  pl.BlockSpec((sub, seq // 256, tq, 256), lambda b, t: (b * nt + t, 0, 0, 0)),
                  full((1, GROUP_W)), full((1, LANES)), full((N_HEADS64, 512)), full((N_HEADS64, 512))],
        out_specs=pl.BlockSpec((step, GROUP_W), lambda b, t: (b * nt + t, 0)),
        out_shape=jax.ShapeDtypeStruct((bsz * seq, GROUP_W), BF16),
        scratch_shapes=[pltpu.VMEM((seq, LANES), BF16), pltpu.VMEM((seq, LANES), BF16),
                        pltpu.VMEM((3, rows, 256), F32), pltpu.VMEM((rows, LANES), BF16),
                        pltpu.VMEM((seq // 256, rows, 256), F32),
                        pltpu.VMEM((rows, LANES), F32), pltpu.VMEM((rows, LANES), F32)],
        compiler_params=_params(2),
    )(cfar, h, h, h, h, mask, qg, kg, base0, base1)


def _rope(tile, cos, sin):
    return tile * cos + pltpu.roll(tile, 64, 1) * sin


def _c_prep_kernel(lat_ref, kr_ref, wq_ref, wkv_ref, qag_ref, kvag_ref, qg_ref, kg_ref, cos_ref, sin_ref,
                   qo_ref, ko_ref, vo_ref):
    cq = lat_ref[:, :Q_LORA].astype(F32)
    ms = jnp.mean(cq * cq, axis=-1, keepdims=True)
    cqn = (cq * lax.rsqrt(ms + EPS) * qag_ref[...]).astype(BF16)
    qpre = jnp.dot(cqn, wq_ref[...], preferred_element_type=F32)
    ckv = lat_ref[:, Q_LORA:].astype(F32)
    ms = jnp.mean(ckv * ckv, axis=-1, keepdims=True)
    ckvn = (ckv * lax.rsqrt(ms + EPS) * kvag_ref[...]).astype(BF16)
    kvpre = jnp.dot(ckvn, wkv_ref[...], preferred_element_type=F32)
    lane = lax.broadcasted_iota(jnp.int32, kr_ref.shape, 1)
    kr = jnp.where((lane % 64) < 32, kr_ref[...].astype(F32), 0.0)
    kr_ss = jnp.sum(kr * kr, axis=-1, keepdims=True)
    cos = cos_ref[...]
    sin = sin_ref[...]
    qg = qg_ref[...]
    kg = kg_ref[...]
    kr_rot = _rope(kr * kg[:, LANES:], cos, sin)
    for h in range(C_HEADS):
        qh = qpre[:, h * 256:(h + 1) * 256]
        r = lax.rsqrt(jnp.sum(qh * qh, axis=-1, keepdims=True) * (1.0 / C_QK) + EPS)
        qn = qh * r * qg
        qo_ref[:, h * 256:h * 256 + LANES] = qn[:, :LANES].astype(BF16)
        qo_ref[:, h * 256 + LANES:(h + 1) * 256] = _rope(qn[:, LANES:], cos, sin).astype(BF16)
        kn = kvpre[:, h * LANES:(h + 1) * LANES]
        r = lax.rsqrt((jnp.sum(kn * kn, axis=-1, keepdims=True) + kr_ss) * (1.0 / C_QK) + EPS)
        ko_ref[:, h * 256:h * 256 + LANES] = (kn * r * kg[:, :LANES]).astype(BF16)
        ko_ref[:, h * 256 + LANES:(h + 1) * 256] = (kr_rot * r).astype(BF16)
    vo_ref[...] = kvpre[:, C_HEADS * LANES:].astype(BF16)


def _c_prep(h, wq, wkv, qag, kvag, qg, kg, cos, sin, seq, tm=1024):
    n = h.shape[0]
    tm = min(tm, seq)
    ns = seq // tm
    full = lambda shape: pl.BlockSpec(shape, lambda i: (0,) * len(shape))
    return pl.pallas_call(
        _c_prep_kernel,
        grid=(n // tm,),
        in_specs=[pl.BlockSpec((tm, GROUP_W), lambda i: (i, BLK_C_QKV)),
                  pl.BlockSpec((tm, LANES), lambda i: (i, UNIT_KR_IW)),
                  full((Q_LORA, 4 * 256)), full((KV_LORA, 8 * LANES)),
                  full((1, Q_LORA)), full((1, LANES)), full((1, 256)), full((1, 256)),
                  pl.BlockSpec((tm, LANES), lambda i: (i % ns, 0)),
                  pl.BlockSpec((tm, LANES), lambda i: (i % ns, 0))],
        out_specs=[pl.BlockSpec((tm, 4 * 256), lambda i: (i, 0)),
                   pl.BlockSpec((tm, 4 * 256), lambda i: (i, 0)),
                   pl.BlockSpec((tm, GROUP_W), lambda i: (i, 0))],
        out_shape=[jax.ShapeDtypeStruct((n, 4 * 256), BF16), jax.ShapeDtypeStruct((n, 4 * 256), BF16),
                   jax.ShapeDtypeStruct((n, GROUP_W), BF16)],
        compiler_params=_params(1),
    )(h, h, wq, wkv, qag, kvag, qg, kg, cos, sin)


def _c_attn_kernel(q_ref, k_ref, v_ref, z_ref, o_ref, *scratch, tq, sub):
    for j in range(sub):
        rows = slice(j * tq, (j + 1) * tq)
        _c_attn_tile(pl.program_id(1) * sub + j, q_ref.at[rows, :], k_ref, v_ref, z_ref.at[rows, :],
                     o_ref.at[rows, :], *scratch, tq=tq)


def _c_attn_tile(qt, q_ref, k_ref, v_ref, z_ref, o_ref, s_ref, mp_ref, lp_ref, acc_ref, *, tq):
    scale = C_QK ** -0.5 * LOG2E
    hrows = [slice(h * tq, (h + 1) * tq) for h in range(C_HEADS)]
    qchunk = (qt * tq + lax.broadcasted_iota(jnp.int32, (tq, 256), 0)) // CHUNK
    kcol = lax.broadcasted_iota(jnp.int32, (tq, 256), 1)
    mp_ref[...] = jnp.full(mp_ref.shape, NEG, F32)
    lp_ref[...] = jnp.zeros(lp_ref.shape, F32)
    acc_ref[...] = jnp.zeros(acc_ref.shape, F32)
    nfull = (qt * tq) // 256
    nblk = ((qt + 1) * tq) // 256

    def logits_blocks(kb0, n, masked):
        for i in range(n):
            kb = kb0 + i
            off = pl.multiple_of(kb * 256, 256)
            for h in range(C_HEADS):
                cols = slice(h * 256, (h + 1) * 256)
                s = _dot_t(q_ref[:, cols], k_ref[pl.ds(off, 256), cols]) * scale
                if masked:
                    s = jnp.where(((kb * 256 + kcol) // CHUNK) <= qchunk, s, NEG)
                s_ref[kb, hrows[h], :] = s
                mp_ref[hrows[h], :] = jnp.maximum(mp_ref[hrows[h], :], jnp.maximum(s[:, :LANES], s[:, LANES:]))

    def value_blocks(kb0, n):
        off = pl.multiple_of(kb0 * 256, 256)
        for h in range(C_HEADS):
            m = mp_ref[hrows[h], :]
            mm = jnp.concatenate([m, m], axis=1)
            p = [jnp.exp2(s_ref[kb0 + i, hrows[h], :] - mm) for i in range(n)]
            lsum = p[0][:, :LANES] + p[0][:, LANES:]
            for pi in p[1:]:
                lsum = lsum + pi[:, :LANES] + pi[:, LANES:]
            lp_ref[hrows[h], :] += lsum
            pb = p[0].astype(BF16) if n == 1 else jnp.concatenate([pi.astype(BF16) for pi in p], axis=1)
            acc_ref[hrows[h], :] += jnp.dot(pb, v_ref[pl.ds(off, n * 256), h * LANES:(h + 1) * LANES],
                                            preferred_element_type=F32)

    _for_blocks(lambda kb0, n: logits_blocks(kb0, n, False), nfull)
    lax.fori_loop(nfull, nblk, lambda kb, c: (logits_blocks(kb, 1, True), c)[1], 0)
    for h in range(C_HEADS):
        m = jnp.max(mp_ref[hrows[h], :], axis=-1, keepdims=True)
        mp_ref[hrows[h], :] = jnp.broadcast_to(m, (tq, LANES))
    _for_blocks(value_blocks, nblk)
    gate = _silu(z_ref[...].astype(F32))
    for h in range(C_HEADS):
        cols = slice(h * LANES, (h + 1) * LANES)
        l = jnp.sum(lp_ref[hrows[h], :], axis=-1, keepdims=True)
        o_ref[:, cols] = (acc_ref[hrows[h], :] / l * gate[:, cols]).astype(BF16)


def _c_attn(qc, kc, vc, h, bsz, seq, tq=256, sub=4):
    step = sub * tq
    nt = seq // step
    rows = C_HEADS * tq
    kern = functools.partial(_c_attn_kernel, tq=tq, sub=sub)
    return pl.pallas_call(
        kern,
        grid=(bsz, nt),
        in_specs=[pl.BlockSpec((step, C_HEADS * 256), lambda b, t: (b * nt + t, 0)),
                  pl.BlockSpec((seq, C_HEADS * 256), lambda b, t: (b, 0)),
                  pl.BlockSpec((seq, GROUP_W), lambda b, t: (b, 0)),
                  pl.BlockSpec((step, GROUP_W), lambda b, t: (b * nt + t, BLK_C_Z))],
        out_specs=pl.BlockSpec((step, GROUP_W), lambda b, t: (b * nt + t, 0)),
        out_shape=jax.ShapeDtypeStruct((bsz * seq, GROUP_W), BF16),
        scratch_shapes=[pltpu.VMEM((seq // 256, rows, 256), F32), pltpu.VMEM((rows, LANES), F32),
                        pltpu.VMEM((rows, LANES), F32), pltpu.VMEM((rows, LANES), F32)],
        compiler_params=_params(2),
    )(qc, kc, vc, h)


def _mixer_d_kernel(q_ref, k_ref, v_ref, z_ref, qg_ref, kg_ref, base_ref, o_ref, *scratch, seq, tq, sub):
    for j in range(sub):
        rows = slice(j * tq, (j + 1) * tq)
        _mixer_d_tile(pl.program_id(1) * sub + j, q_ref.at[rows, :], k_ref, v_ref, z_ref.at[rows, :], qg_ref,
                      kg_ref, base_ref, o_ref.at[rows, :], *scratch, seq=seq, tq=tq)


def _mixer_d_tile(qt, q_ref, k_ref, v_ref, z_ref, qg_ref, kg_ref, base_ref, o_ref,
                  kpad_ref, vpad_ref, bias_ref, s_ref, mp_ref, *, seq, tq):
    b = pl.program_id(0)
    win = tq + D_LEFT

    @pl.when((b == 0) & (qt == 0))
    def _():
        qc = lax.broadcasted_iota(jnp.int32, (tq, win), 0) // CHUNK
        kc = lax.broadcasted_iota(jnp.int32, (tq, win), 1) // CHUNK
        band = (kc >= qc) & (kc <= qc + D_LEFT // CHUNK)
        for h in range(N_HEADS64):
            bias_ref[h] = jnp.where(band, _toeplitz(base_ref[h:h + 1, :], tq, win) * LOG2E, NEG)

    @pl.when(qt == 0)
    def _():
        kpad_ref[0:D_LEFT, :] = jnp.zeros((D_LEFT, GROUP_W), BF16)
        vpad_ref[0:D_LEFT, :] = jnp.zeros((D_LEFT, GROUP_W), BF16)
        for r in range(seq // 256):
            rows = slice(r * 256, (r + 1) * 256)
            dst = slice(D_LEFT + r * 256, D_LEFT + (r + 1) * 256)
            tiles = _rms_heads64(k_ref[rows, :].astype(F32), kg_ref[...], 4)
            for t in range(4):
                kpad_ref[dst, t * LANES:(t + 1) * LANES] = tiles[t].astype(BF16)
            vpad_ref[dst, :] = v_ref[rows, :]

    lo = _lo_mask(tq)
    qtiles = _rms_heads64(q_ref[...].astype(F32), qg_ref[...], 4)
    start = pl.multiple_of(qt * tq, tq)
    gate = _silu(z_ref[...].astype(F32))
    ntile = win // LANES

    def in_band(r, c):
        return c * LANES < r * CHUNK + D_LEFT + CHUNK and (c + 1) * LANES > r * CHUNK

    def pipeline(first):
        live = slice(first * LANES, win)
        wstart = pl.multiple_of(start + first * LANES, LANES)

        def logits(h):
            t, half = divmod(h, 2)
            kwin = kpad_ref[pl.ds(wstart, win - first * LANES), t * LANES:(t + 1) * LANES]
            sel = lo if half == 0 else jnp.logical_not(lo)
            qh = jnp.where(sel, qtiles[t] * (0.125 * LOG2E), 0.0).astype(BF16)
            s = _dot_t(qh, kwin) + bias_ref[h, :, live]
            s_ref[h, :, live] = s
            tiles = [s[:, i * LANES:(i + 1) * LANES] for i in range(ntile - first)]
            m = jnp.max(functools.reduce(jnp.maximum, tiles), axis=-1, keepdims=True)
            mp_ref[h] = jnp.broadcast_to(m, (tq, LANES))

        def values(h):
            vwin = vpad_ref[pl.ds(wstart, win - first * LANES), (h // 2) * LANES:(h // 2 + 1) * LANES]
            p_rows, l_rows = [], []
            for r in range(tq // CHUNK):
                rows = slice(r * CHUNK, (r + 1) * CHUNK)
                m = mp_ref[h, rows, :]
                tiles = [jnp.exp2(s_ref[h, rows, c * LANES:(c + 1) * LANES] - m) if in_band(r, c) else None
                         for c in range(first, ntile)]
                l_rows.append(functools.reduce(jnp.add, [t for t in tiles if t is not None]))
                p_rows.append(jnp.concatenate([jnp.zeros((CHUNK, LANES), BF16) if t is None else t.astype(BF16)
                                               for t in tiles], axis=1))
            l = jnp.sum(jnp.concatenate(l_rows, axis=0), axis=-1, keepdims=True)
            return jnp.dot(jnp.concatenate(p_rows, axis=0), vwin, preferred_element_type=F32) / l

        outs = []
        logits(0)
        for h in range(1, N_HEADS64 + 1):
            if h < N_HEADS64:
                logits(h)
            outs.append(values(h - 1))
            if h % 2 == 0:
                cols = slice((h // 2 - 1) * LANES, (h // 2) * LANES)
                o_ref[:, cols] = (jnp.where(lo, outs[h - 2], outs[h - 1]) * gate[:, cols]).astype(BF16)

    n_lead = D_LEFT // tq
    for lead in range(n_lead):
        pl.when(qt == lead)(functools.partial(pipeline, (D_LEFT - lead * tq) // LANES))
    pl.when(qt >= n_lead)(functools.partial(pipeline, 0))


def _mixer_d(h, qg, kg, base, bsz, seq, tq=256, sub=4):
    rows = sub * tq
    nt = seq // rows
    kern = functools.partial(_mixer_d_kernel, seq=seq, tq=tq, sub=sub)
    full = lambda shape: pl.BlockSpec(shape, lambda b, t: (0,) * len(shape))
    return pl.pallas_call(
        kern,
        grid=(bsz, nt),
        in_specs=[pl.BlockSpec((rows, GROUP_W), lambda b, t: (b * nt + t, BLK_D_Q)),
                  pl.BlockSpec((seq, GROUP_W), lambda b, t: (b, BLK_D_K)),
                  pl.BlockSpec((seq, GROUP_W), lambda b, t: (b, BLK_D_V)),
                  pl.BlockSpec((rows, GROUP_W), lambda b, t: (b * nt + t, BLK_D_Z)),
                  full((1, GROUP_W)), full((1, GROUP_W)), full((N_HEADS64, 2 * tq + D_LEFT))],
        out_specs=pl.BlockSpec((rows, GROUP_W), lambda b, t: (b * nt + t, 0)),
        out_shape=jax.ShapeDtypeStruct((bsz * seq, GROUP_W), BF16),
        scratch_shapes=[pltpu.VMEM((seq + D_LEFT, GROUP_W), BF16), pltpu.VMEM((seq + D_LEFT, GROUP_W), BF16),
                        pltpu.VMEM((N_HEADS64, tq, tq + D_LEFT), F32),
                        pltpu.VMEM((N_HEADS64, tq, tq + D_LEFT), F32), pltpu.VMEM((N_HEADS64, tq, LANES), F32)],
        compiler_params=_params(2),
    )(h, h, h, h, qg, kg, base)


def _w_in_pieces(take, zeros):
    c = lambda name, size, off=0: take(_SRC[name] + off, size)
    return [c("a_u", 512), c("a_v", 512), c("a_z", 512),
            c("b_q", 512), c("b_iq", 512), c("b_z", 512),
            c("b_k", 64), c("b_k", 64), c("b_v", 64), c("b_v", 64), c("b_ik", 64), c("b_ik", 64),
            c("c_kr", 32), c("b_iw", 8), zeros(24), c("c_kr", 32, 32), zeros(32),
            c("c_q", 384), c("c_kv", 128), c("c_z", 512),
            c("d_q", 512), c("d_k", 512), c("d_v", 512), c("d_z", 512)]


def _layout_w_in_kernel(w_ref, o_ref):
    tk = w_ref.shape[2]
    pieces = _w_in_pieces(lambda s, n: w_ref[0, s:s + n, :], lambda n: jnp.zeros((n, tk), F32))
    ends = np.cumsum([0] + [p.shape[0] for p in pieces])
    start = 0
    for i in range(1, len(pieces) + 1):
        if ends[i] % GROUP_W == 0:
            group = pieces[start:i]
            blk = group[0] if len(group) == 1 else jnp.concatenate(group, axis=0)
            o_ref[0, ends[start]:ends[i], :] = blk.astype(BF16)
            start = i


def _layout_w_in(w_in, tk=256):
    w_t = jnp.swapaxes(w_in, 1, 2)
    depth, cols, d = w_t.shape
    return pl.pallas_call(
        _layout_w_in_kernel,
        grid=(depth, d // tk),
        in_specs=[pl.BlockSpec((1, cols, tk), lambda l, i: (l, 0, i))],
        out_specs=pl.BlockSpec((1, H_COLS, tk), lambda l, i: (l, 0, i)),
        out_shape=jax.ShapeDtypeStruct((depth, H_COLS, d), BF16),
        compiler_params=_params(2),
    )(w_t)


def _rope_layout(v):
    z = jnp.zeros(v.shape[:-1] + (32,), v.dtype)
    return jnp.concatenate([v[..., :32], z, v[..., 32:], z], axis=-1)


def _layout_c(w_qb, w_kvb, q_gain, k_gain, qa_gain):
    wq = w_qb.reshape(Q_LORA, C_HEADS, C_QK)
    wq = jnp.concatenate([wq[..., :C_NOPE], _rope_layout(wq[..., C_NOPE:])], axis=-1)
    wq = wq.reshape(Q_LORA, C_HEADS * 256).astype(BF16)
    wkv = w_kvb.reshape(KV_LORA, C_HEADS, 2 * LANES)
    wkv = jnp.concatenate([wkv[..., :C_NOPE].reshape(KV_LORA, -1), wkv[..., C_NOPE:].reshape(KV_LORA, -1)],
                          axis=1).astype(BF16)
    lay = lambda g: jnp.concatenate([g[:C_NOPE], _rope_layout(g[C_NOPE:])])[None, :]
    return wq, wkv, lay(q_gain), lay(k_gain), qa_gain[None, :]


def _t5_bucket_static(rel):
    half = T5_BUCKETS // 2
    exact = half // 2
    n = abs(rel)
    if n < exact:
        val = n
    else:
        val = min(exact + (n * n // (exact * exact)).bit_length() - 1, half - 1)
    return (half if rel > 0 else 0) + val


def _t5_tables(t5_bias):
    m = np.arange(512)
    d0 = np.where(m < 256, m, m - 512)
    d1 = np.where(m <= 256, m - 256, m - 768)
    idx0 = np.array([_t5_bucket_static(int(d)) for d in d0], np.int32)
    idx1 = np.array([_t5_bucket_static(int(d)) for d in d1], np.int32)
    far = _t5_bucket_static(-512)
    return t5_bias[idx0].T, t5_bias[idx1].T, t5_bias[far]


def _band_table(rel_bias, tq):
    width = 2 * tq + D_LEFT
    m = np.arange(width)
    dist = np.where(m <= tq + D_LEFT, D_LEFT - m, D_LEFT + width - m)
    idx = np.clip(dist, -REL_CLIP, REL_CLIP) + REL_CLIP
    return rel_bias[idx.astype(np.int32)].T


def _rope_tables(seq):
    inv = ROPE_BASE ** (-jnp.arange(0, C_ROPE, 2, dtype=F32) / C_ROPE)
    ang = jnp.arange(seq, dtype=F32)[:, None] * inv[None, :]
    c, s = jnp.cos(ang), jnp.sin(ang)
    z = jnp.zeros_like(c)
    return jnp.concatenate([c, z, c, z], axis=1), jnp.concatenate([-s, z, s, z], axis=1)


def kernel(x, t5_bias, norm_g, w_in, a_v_gain, a_ws, a_bs, b_q_gain, b_k_gain, c_qa_gain, c_kva_gain,
           c_w_qb, c_w_kvb, c_q_gain, c_k_gain, d_q_gain, d_k_gain, d_rel_bias, w_out):
    bsz, seq, d_model = x.shape
    depth = w_in.shape[0]
    tq = 256
    assert seq % 512 == 0 and seq <= 2048 and d_model % 512 == 0, (seq, d_model)
    assert w_in.shape[2] == _SRC["d_z"] + GROUP_W and w_out.shape[1] == 4 * GROUP_W, (w_in.shape, w_out.shape)
    x2 = x.reshape(bsz * seq, d_model)
    cos, sin = _rope_tables(seq)
    base0, base1, cfar = _t5_tables(t5_bias)
    w_in_blocks = _layout_w_in(w_in)
    for l in range(depth):
        h = _inproj(x2, norm_g[l][None, :], w_in_blocks, l)
        mask = _b_select(h, bsz, seq)
        y_b = _b_attn(h, mask, cfar, jnp.tile(b_q_gain[l], N_HEADS64)[None, :],
                      jnp.tile(b_k_gain[l], 2)[None, :], base0, base1, bsz, seq, tq)
        wq, wkv, qg, kg, qag = _layout_c(c_w_qb[l], c_w_kvb[l], c_q_gain[l], c_k_gain[l], c_qa_gain[l])
        qc, kc, vc = _c_prep(h, wq, wkv, qag, c_kva_gain[l][None, :], qg, kg, cos, sin, seq)
        y_c = _c_attn(qc, kc, vc, h, bsz, seq, tq)
        y_d = _mixer_d(h, jnp.tile(d_q_gain[l], N_HEADS64)[None, :], jnp.tile(d_k_gain[l], N_HEADS64)[None, :],
                       _band_table(d_rel_bias[l], tq), bsz, seq, tq)
        x2 = _outproj(x2, h, a_v_gain[l][None, :], a_ws[l], a_bs[l][:, :, None], (y_b, y_c, y_d),
                      w_out[l].astype(BF16))
    return x2.reshape(bsz, seq, d_model)
```

```python
import functools
import math

import numpy as np
import jax
import jax.numpy as jnp
from jax import lax
from jax.experimental import pallas as pl
from jax.experimental.pallas import tpu as pltpu

F32 = jnp.float32
BF16 = jnp.bfloat16

EPS = 1e-6
NEG = -1e30
LOG2E = math.log2(math.e)
KEY_LOWEST = int(np.float32(-np.finfo(np.float32).max).view(np.int32)) ^ 0x7FFFFFFF
CHUNK = 64
LANES = 128
GROUP_W = 512
A_GROUPS = 4
GMLP_BLOCK = 128
N_HEADS64 = 8
IDX_SCALE = (8 ** -0.5) * 0.125
TOPK_MAX = 256
T5_BUCKETS = 32
C_HEADS = 4
C_NOPE = 128
C_ROPE = 64
C_QK = 192
Q_LORA = 384
KV_LORA = 128
ROPE_BASE = 10000.0
D_LEFT = 8 * CHUNK
REL_CLIP = 128
VMEM_LIMIT = 56 * 1024 * 1024
OUTPROJ_VMEM_LIMIT = 58 * 1024 * 1024

BLK_A_U, BLK_A_V, BLK_A_Z = 0, 1, 2
BLK_B_Q, BLK_B_IQ, BLK_B_Z, BLK_SMALL = 3, 4, 5, 6
BLK_C_QKV, BLK_C_Z = 7, 8
BLK_D_Q, BLK_D_K, BLK_D_V, BLK_D_Z = 9, 10, 11, 12
H_COLS = 13 * GROUP_W
UNIT_B_K, UNIT_B_V, UNIT_B_IK, UNIT_KR_IW = (BLK_SMALL * 4 + i for i in range(4))
IW_LANE = 32

_SRC = dict(a_u=0, a_v=512, a_z=1024, b_q=1536, b_k=2048, b_v=2112, b_iq=2176, b_ik=2688,
            b_iw=2752, b_z=2760, c_q=3272, c_kv=3656, c_kr=3784, c_z=3848,
            d_q=4360, d_k=4872, d_v=5384, d_z=5896)


def _params(n_axes, vmem_limit=VMEM_LIMIT):
    return pltpu.CompilerParams(dimension_semantics=("arbitrary",) * n_axes,
                                vmem_limit_bytes=vmem_limit)


def _gelu(x):
    c = math.sqrt(2.0 / math.pi)
    return x * (0.5 * (1.0 + jnp.tanh(c * (x + 0.044715 * (x * x * x)))))


def _silu(x):
    return x * (1.0 / (1.0 + jnp.exp(-x)))


def _dot_t(a, b):
    return lax.dot_general(a, b, (((1,), (1,)), ((), ())), preferred_element_type=F32)


def _lo_mask(rows):
    return lax.broadcasted_iota(jnp.int32, (rows, LANES), 1) < 64


def _rms_heads64(x, gain, ntiles):
    lo = _lo_mask(x.shape[0])
    tiles = []
    for t in range(ntiles):
        xt = x[:, t * LANES:(t + 1) * LANES]
        sq = xt * xt
        s_lo = jnp.sum(jnp.where(lo, sq, 0.0), axis=-1, keepdims=True)
        s_hi = jnp.sum(jnp.where(lo, 0.0, sq), axis=-1, keepdims=True)
        r = jnp.where(lo, lax.rsqrt(s_lo * (1.0 / 64) + EPS), lax.rsqrt(s_hi * (1.0 / 64) + EPS))
        tiles.append(xt * r * gain[:, t * LANES:(t + 1) * LANES])
    return tiles


def _for_blocks(blocks, count):
    def quad(i, carry):
        blocks(4 * i, 4)
        return carry

    lax.fori_loop(0, count // 4, quad, 0)
    first = (count // 4) * 4

    @pl.when((count & 2) != 0)
    def _():
        blocks(first, 2)

    @pl.when((count & 1) != 0)
    def _():
        blocks(first + (count & 2), 1)


def _toeplitz(base_row, rows, width):
    t = jnp.broadcast_to(base_row, (rows, base_row.shape[1]))
    t = pltpu.roll(t, 0, 1, stride=1, stride_axis=0)
    return t[:, :width]


def _inproj_kernel(x_ref, g_ref, w_ref, o_ref):
    x = x_ref[...]
    ms = jnp.mean(x * x, axis=-1, keepdims=True)
    xn = (x * lax.rsqrt(ms + EPS) * g_ref[...]).astype(BF16)
    for c in range(H_COLS // GROUP_W):
        cols = slice(c * GROUP_W, (c + 1) * GROUP_W)
        o_ref[:, cols] = _dot_t(xn, w_ref[cols, :]).astype(BF16)


def _inproj(x2, g, w_all, layer, tm=512):
    n, d = x2.shape
    return pl.pallas_call(
        _inproj_kernel,
        grid=(n // tm,),
        in_specs=[pl.BlockSpec((tm, d), lambda i: (i, 0)),
                  pl.BlockSpec((1, d), lambda i: (0, 0)),
                  pl.BlockSpec((None, H_COLS, d), lambda i: (layer, 0, 0), pipeline_mode=pl.Buffered(1))],
        out_specs=pl.BlockSpec((tm, H_COLS), lambda i: (i, 0)),
        out_shape=jax.ShapeDtypeStruct((n, H_COLS), BF16),
        compiler_params=_params(1),
    )(x2, g, w_all)


def _outproj_kernel(x_ref, uvz_ref, vg_ref, ws_ref, bs_ref, yb_ref, yc_ref, yd_ref, w_ref, o_ref, ya_ref):
    u_ref, v_ref, z_ref = (uvz_ref.at[:, blk * GROUP_W:(blk + 1) * GROUP_W] for blk in (BLK_A_U, BLK_A_V, BLK_A_Z))
    tm, d = x_ref.shape
    wgs = _mixer_a_weights(ws_ref)
    nblk = tm // GMLP_BLOCK
    ncol = d // nblk
    for c in range(nblk):
        cols = slice(c * ncol, (c + 1) * ncol)
        acc = x_ref[:, cols]
        for g, y_ref in ((1, yb_ref), (2, yc_ref), (3, yd_ref)):
            acc = acc + jnp.dot(y_ref[...], w_ref[g * GROUP_W:(g + 1) * GROUP_W, cols],
                                preferred_element_type=F32)
        o_ref[:, cols] = acc
        _mixer_a_rows(u_ref, v_ref, z_ref, vg_ref, wgs, bs_ref, ya_ref, c)
    for c in range(nblk):
        cols = slice(c * ncol, (c + 1) * ncol)
        o_ref[:, cols] += jnp.dot(ya_ref[...], w_ref[0:GROUP_W, cols], preferred_element_type=F32)


def _outproj(x2, h, vg, ws, bs, ys, w, tm=1024):
    n, d = x2.shape
    yspec = pl.BlockSpec((tm, GROUP_W), lambda i: (i, 0))
    return pl.pallas_call(
        _outproj_kernel,
        grid=(n // tm,),
        in_specs=[pl.BlockSpec((tm, d), lambda i: (i, 0)),
                  pl.BlockSpec((tm, 3 * GROUP_W), lambda i: (i, 0)),
                  pl.BlockSpec((1, GROUP_W), lambda i: (0, 0)),
                  pl.BlockSpec((A_GROUPS, GMLP_BLOCK, GMLP_BLOCK), lambda i: (0, 0, 0)),
                  pl.BlockSpec((A_GROUPS, GMLP_BLOCK, 1), lambda i: (0, 0, 0)),
                  yspec, yspec, yspec,
                  pl.BlockSpec((4 * GROUP_W, d), lambda i: (0, 0), pipeline_mode=pl.Buffered(1))],
        out_specs=pl.BlockSpec((tm, d), lambda i: (i, 0)),
        out_shape=jax.ShapeDtypeStruct((n, d), F32),
        scratch_shapes=[pltpu.VMEM((tm, GROUP_W), BF16)],
        compiler_params=_params(1, OUTPROJ_VMEM_LIMIT),
    )(x2, h, vg, ws, bs, *ys, w)


def _mixer_a_weights(w_ref):
    i = lax.broadcasted_iota(jnp.int32, (GMLP_BLOCK, GMLP_BLOCK), 0)
    j = lax.broadcasted_iota(jnp.int32, (GMLP_BLOCK, GMLP_BLOCK), 1)
    keep = (j // CHUNK) <= (i // CHUNK)
    return [jnp.where(keep, w_ref[g], 0.0).astype(BF16) for g in range(A_GROUPS)]


def _mixer_a_rows(u_ref, v_ref, z_ref, vg_ref, wgs, b_ref, o_ref, blk):
    rows = slice(blk * GMLP_BLOCK, (blk + 1) * GMLP_BLOCK)
    u = _gelu(u_ref[rows, :].astype(F32))
    v = _gelu(v_ref[rows, :].astype(F32))
    ms = jnp.mean(v * v, axis=-1, keepdims=True)
    vb = (v * lax.rsqrt(ms + EPS) * vg_ref[...]).astype(BF16)
    gate = _silu(z_ref[rows, :].astype(F32))
    for g in range(A_GROUPS):
        cols = slice(g * LANES, (g + 1) * LANES)
        sg = jnp.dot(wgs[g], vb[:, cols], preferred_element_type=F32) + b_ref[g]
        o_ref[rows, cols] = (u[:, cols] * sg * gate[:, cols]).astype(BF16)


def _order_key(x):
    return jnp.where(x < 0, x ^ 0x7FFFFFFF, x)


def _b_select_kernel(iq_ref, iw_ref, ik_ref, o_ref, lhs_ref, sc_ref, *, seq, tq, topk):
    t_blk = pl.program_id(1)
    nkb = seq // 256
    nblk = ((t_blk + 1) * tq) // 256
    n_interp = 12
    n_unchecked = 13 + 2 * t_blk
    hrows = [slice(h * tq, (h + 1) * tq) for h in range(N_HEADS64)]

    lo_half = _lo_mask(tq)
    w_t = (iw_ref[...].astype(F32) * IDX_SCALE).T
    for h in range(N_HEADS64):
        iqt = iq_ref[:, (h // 2) * LANES:(h // 2 + 1) * LANES]
        sel = lo_half if h % 2 == 0 else jnp.logical_not(lo_half)
        lhs_ref[hrows[h], :] = jnp.where(sel, iqt, jnp.zeros_like(iqt))

    def qpos(ln):
        return t_blk * tq + ln.start + lax.broadcasted_iota(jnp.int32, (256, ln.stop - ln.start), 1)

    def krow(ln):
        return lax.broadcasted_iota(jnp.int32, (256, ln.stop - ln.start), 0)

    assert tq in (256, 512)
    every = slice(0, tq)
    tail = slice(256, tq) if tq == 512 else None
    nwide = nblk - 1 if tail else nblk

    def fold8(x):
        return jnp.sum(x.reshape(256 // 8, 8, x.shape[1]), axis=0)

    def widen(part, ln):
        return part if ln == every else jnp.concatenate([jnp.zeros((8, ln.start), F32), part], axis=1)

    def score_block(kb, carry, masked, ln):
        amax, n_pos, n_nn = carry
        off = pl.multiple_of(kb * 256, 256)
        ikblk = ik_ref[pl.ds(off, 256), :]
        score = jnp.zeros((256, ln.stop - ln.start), F32)
        for h in range(N_HEADS64):
            w_h = w_t[IW_LANE + h:IW_LANE + h + 1, ln]
            q_h = lhs_ref[h * tq + ln.start:h * tq + ln.stop, :]
            score = score + w_h * jnp.maximum(_dot_t(ikblk, q_h), 0.0)
        mag = jnp.abs(score)
        if masked:
            adm = ((kb * 256 + krow(ln)) // CHUNK) <= (qpos(ln) // CHUNK)
            score = jnp.where(adm, score, -jnp.inf)
            mag = jnp.where(adm, mag, 0.0)
        sc_ref[kb, :, ln] = score
        return (jnp.maximum(amax, widen(jnp.max(mag.reshape(256 // 8, 8, mag.shape[1]), axis=0), ln)),
                n_pos + widen(fold8(jnp.where(score > 0.0, 1.0, 0.0)), ln),
                n_nn + widen(fold8(jnp.where(score >= 0.0, 1.0, 0.0)), ln))

    nfull = (t_blk * tq) // 256
    zeros8 = jnp.zeros((8, tq), F32)
    carry = lax.fori_loop(0, nfull, lambda kb, c: score_block(kb, c, False, every), (zeros8, zeros8, zeros8))
    carry = lax.fori_loop(nfull, nwide, lambda kb, c: score_block(kb, c, True, every), carry)
    if tail:
        carry = score_block(nblk - 1, carry, True, tail)
    amax = jnp.max(carry[0], axis=0, keepdims=True)
    f_pos = jnp.sum(carry[1], axis=0, keepdims=True)
    f_nn = jnp.sum(carry[2], axis=0, keepdims=True)

    def count(pred):
        def body(kb, acc):
            return acc + fold8(jnp.where(pred(sc_ref[kb], kb, every), 1.0, 0.0))

        def run(first, n, acc):
            for i in range(n):
                acc = body(first + i, acc)
            return acc

        acc = lax.fori_loop(0, nwide // 4, lambda i, a: run(4 * i, 4, a), zeros8)
        done4 = (nwide // 4) * 4
        acc = lax.fori_loop(0, (nwide >> 1) & 1, lambda i, a: run(done4, 2, a), acc)
        acc = lax.fori_loop(0, nwide & 1, lambda i, a: run(done4 + (nwide & 2), 1, a), acc)
        if tail:
            acc = acc + widen(fold8(jnp.where(pred(sc_ref[nblk - 1, :, tail], nblk - 1, tail), 1.0, 0.0)), tail)
        return jnp.sum(acc, axis=0, keepdims=True)

    kf = float(topk)
    qrow = t_blk * tq + lax.broadcasted_iota(jnp.int32, (1, tq), 1)
    n_adm = ((qrow // CHUNK + 1) * CHUNK).astype(F32)
    one = jnp.ones((1, tq), jnp.int32)
    pos = f_pos > kf
    neg = f_nn < kf
    lo0 = jnp.where(pos, one, _order_key(lax.bitcast_convert_type(-amax, jnp.int32)))
    hi0 = jnp.where(neg, one - 1, _order_key(lax.bitcast_convert_type(amax, jnp.int32)) + 1)
    w_lo0 = jnp.where(pos, f_pos, n_adm) - kf
    w_hi0 = kf - jnp.where(neg, f_nn, 0.0)
    all_sel = n_adm <= kf
    at_zero = jnp.logical_not(pos | neg)
    done0 = jnp.where(all_sel | at_zero | (hi0 == lo0 + 1), 1.0, 0.0)
    thr0 = jnp.where(all_sel, KEY_LOWEST, jnp.where(at_zero, jnp.where(f_pos == kf, one, one - 1), lo0))

    def as_score(key):
        return lax.bitcast_convert_type(_order_key(key), F32)

    def search_cond(st):
        return jnp.logical_and(st[0][0] < n_interp + 32, st[1] < 0.5)

    def search_step(st):
        it, lo, hi, w_lo, w_hi, side, done, thr = st
        lo_v = as_score(lo)
        hi_v = as_score(hi)
        c_v = lo_v + (hi_v - lo_v) * (w_lo / (w_lo + w_hi))
        c_interp = _order_key(lax.bitcast_convert_type(c_v, jnp.int32))
        c_mid = (lo >> 1) + (hi >> 1) + (lo & hi & 1)
        cand = jnp.where(it < n_interp, c_interp, c_mid)
        cand = jnp.minimum(jnp.maximum(cand, lo + 1), hi - 1)
        cand_v = as_score(cand)
        f = count(lambda s, kb, ln: s >= cand_v[:, ln])
        live = done < 0.5
        up = f > kf
        hit = f == kf
        new_lo = jnp.where(live & up, cand, lo)
        new_hi = jnp.where(live & jnp.logical_not(up), cand, hi)
        new_w_lo = jnp.where(up, f - kf, jnp.where(side < 0.0, 0.5 * w_lo, w_lo))
        new_w_hi = jnp.where(up, jnp.where(side > 0.0, 0.5 * w_hi, w_hi), kf - f)
        new_side = jnp.where(up, 1.0, -1.0)
        new_thr = jnp.where(live, jnp.where(hit, cand, new_lo), thr)
        new_done = jnp.where(live & (hit | (new_hi == new_lo + 1)), 1.0, done)
        return (it + 1, new_lo, new_hi, jnp.where(live, new_w_lo, w_lo),
                jnp.where(live, new_w_hi, w_hi), jnp.where(live, new_side, side), new_done, new_thr)

    def checked_step(st):
        new = search_step(st[0])
        return new, jnp.min(new[-2])

    state = (jnp.int32(0), lo0, hi0, w_lo0, w_hi0, jnp.zeros((1, tq), F32), done0, thr0)
    state = lax.fori_loop(0, n_unchecked, lambda i, st: search_step(st), state)
    thr = as_score(lax.while_loop(search_cond, checked_step, (state, jnp.min(state[-2])))[0][-1])

    any_excess = jnp.max(count(lambda s, kb, ln: s >= thr[:, ln])) > kf

    def store_mask(kb, keep_t, ln):
        for g in range(tq // 256):
            if g * 256 >= ln.start:
                o_ref[g, kb] = keep_t[:, g * 256 - ln.start:(g + 1) * 256 - ln.start]
            else:
                o_ref[g, kb] = jnp.zeros((256, 256), F32)

    def write_mask(keep):
        def body(kb, carry):
            store_mask(kb, jnp.where(keep(sc_ref[kb], kb, every), 1.0, 0.0), every)
            return carry

        lax.fori_loop(0, nwide, body, 0)
        if tail:
            store_mask(nblk - 1, jnp.where(keep(sc_ref[nblk - 1, :, tail], nblk - 1, tail), 1.0, 0.0), tail)

    def write_unused(kb, carry):
        store_mask(kb, jnp.zeros((256, tq), F32), every)
        return carry

    lax.fori_loop(nblk, nkb, write_unused, 0)

    @pl.when(jnp.logical_not(any_excess))
    def _():
        write_mask(lambda s, kb, ln: s >= thr[:, ln])

    @pl.when(any_excess)
    def _():
        need = kf - count(lambda s, kb, ln: s > thr[:, ln])

        def idx_step(it, jmax):
            cand = jmax | lax.shift_left(jnp.int32(1), 10 - it)
            below = count(lambda s, kb, ln: (s == thr[:, ln]) & ((kb * 256 + krow(ln)) < cand[:, ln]))
            return jnp.where(below < need, cand, jmax)

        jmax = lax.fori_loop(0, 11, idx_step, jnp.zeros((1, tq), jnp.int32))
        write_mask(lambda s, kb, ln: (s > thr[:, ln]) | ((s == thr[:, ln]) & ((kb * 256 + krow(ln)) <= jmax[:, ln])))


def _b_select(h, bsz, seq, tq=512):
    tq = min(tq, seq)
    nt = seq // tq
    topk = min(TOPK_MAX, seq // 4)
    kern = functools.partial(_b_select_kernel, seq=seq, tq=tq, topk=topk)
    return pl.pallas_call(
        kern,
        grid=(bsz, nt),
        in_specs=[pl.BlockSpec((tq, GROUP_W), lambda b, t: (b * nt + t, BLK_B_IQ)),
                  pl.BlockSpec((tq, LANES), lambda b, t: (b * nt + t, UNIT_KR_IW)),
                  pl.BlockSpec((seq, LANES), lambda b, t: (b, UNIT_B_IK))],
        out_specs=pl.BlockSpec((tq // 256, seq // 256, 256, 256), lambda b, t: (b * nt + t, 0, 0, 0)),
        out_shape=jax.ShapeDtypeStruct((bsz * seq // 256, seq // 256, 256, 256), F32),
        scratch_shapes=[pltpu.VMEM((N_HEADS64 * tq, LANES), BF16),
                        pltpu.VMEM((seq // 256, 256, tq), F32)],
        compiler_params=_params(2),
    )(h, h, h)


def _b_attn_kernel(cfar_ref, q_ref, z_ref, k_ref, v_ref, msk_ref, qg_ref, kg_ref, base0_ref, base1_ref,
                   o_ref, *scratch, seq, tq, sub):
    for j in range(sub):
        rows = slice(j * tq, (j + 1) * tq)
        _b_attn_tile(pl.program_id(1) * sub + j, cfar_ref, q_ref.at[rows, :], z_ref.at[rows, :], k_ref, v_ref,
                     msk_ref.at[j:j + 1], qg_ref, kg_ref, base0_ref, base1_ref, o_ref.at[rows, :], *scratch,
                     seq=seq, tq=tq)


def _b_attn_tile(t_blk, cfar_ref, q_ref, z_ref, k_ref, v_ref, msk_ref, qg_ref, kg_ref, base0_ref, base1_ref,
                 o_ref, kn_ref, v1_ref, bias_ref, qall_ref, s_ref, mp_ref, acc_ref, *, seq, tq):
    b = pl.program_id(0)
    hrows = [slice(h * tq, (h + 1) * tq) for h in range(N_HEADS64)]

    @pl.when((b == 0) & (t_blk == 0))
    def _():
        for h in range(N_HEADS64):
            bias_ref[0, hrows[h], :] = jnp.full((tq, 256), cfar_ref[h] * LOG2E, F32)
            bias_ref[1, hrows[h], :] = _toeplitz(base1_ref[h:h + 1, :], tq, 256) * LOG2E
            bias_ref[2, hrows[h], :] = _toeplitz(base0_ref[h:h + 1, :], tq, 256) * LOG2E

    @pl.when(t_blk == 0)
    def _():
        lo256 = _lo_mask(256)
        for r in range(seq // 256):
            rows = slice(r * 256, (r + 1) * 256)
            k = k_ref[rows, :].astype(F32)
            ms = jnp.mean(k * k, axis=-1, keepdims=True)
            kn_ref[rows, :] = (k * lax.rsqrt(ms + EPS) * kg_ref[...]).astype(BF16)
            v = v_ref[rows, :]
            v1_ref[rows, :] = jnp.where(lo256, v, jnp.ones_like(v))

    lo = _lo_mask(tq)
    qtiles = _rms_heads64(q_ref[...].astype(F32), qg_ref[...], 4)
    for h in range(N_HEADS64):
        sel = lo if h % 2 == 0 else jnp.logical_not(lo)
        qall_ref[hrows[h], :] = jnp.where(sel, qtiles[h // 2] * (0.125 * LOG2E), 0.0).astype(BF16)
    mp_ref[...] = jnp.full(mp_ref.shape, NEG, F32)
    acc_ref[...] = jnp.zeros(acc_ref.shape, F32)
    nblk = t_blk + 1

    def logits_blocks(kb0, n):
        for i in range(n):
            kb = kb0 + i
            off = pl.multiple_of(kb * 256, 256)
            kblk = kn_ref[pl.ds(off, 256), :]
            which = jnp.clip(kb - (t_blk - 2), 0, 2)
            keep = msk_ref[0, kb].T > 0.5
            for h in range(N_HEADS64):
                s = jnp.where(keep, _dot_t(qall_ref[hrows[h], :], kblk) + bias_ref[which, hrows[h], :], NEG)
                s_ref[kb, hrows[h], :] = s
                mp_ref[hrows[h], :] = jnp.maximum(mp_ref[hrows[h], :], jnp.maximum(s[:, :LANES], s[:, LANES:]))

    def value_blocks(kb0, n):
        off = pl.multiple_of(kb0 * 256, 256)
        v1 = v1_ref[pl.ds(off, n * 256), :]
        for h in range(N_HEADS64):
            m = mp_ref[hrows[h], :]
            mm = jnp.concatenate([m, m], axis=1)
            p = [jnp.exp2(s_ref[kb0 + i, hrows[h], :] - mm).astype(BF16) for i in range(n)]
            p = p[0] if n == 1 else jnp.concatenate(p, axis=1)
            acc_ref[hrows[h], :] += jnp.dot(p, v1, preferred_element_type=F32)

    _for_blocks(logits_blocks, nblk)
    for h in range(N_HEADS64):
        m = jnp.max(mp_ref[hrows[h], :], axis=-1, keepdims=True)
        mp_ref[hrows[h], :] = jnp.broadcast_to(m, (tq, LANES))
    _for_blocks(value_blocks, nblk)

    gate = _silu(z_ref[...].astype(F32))
    for t in range(4):
        a_even = acc_ref[hrows[2 * t], :]
        a_odd = acc_ref[hrows[2 * t + 1], :]
        o_even = a_even / pltpu.roll(a_even, 64, 1)
        o_odd = pltpu.roll(a_odd, 64, 1) / a_odd
        cols = slice(t * LANES, (t + 1) * LANES)
        o_ref[:, cols] = (jnp.where(lo, o_even, o_odd) * gate[:, cols]).astype(BF16)


def _b_attn(h, mask, cfar, qg, kg, base0, base1, bsz, seq, tq=256, sub=2):
    step = sub * tq
    nt = seq // step
    rows = N_HEADS64 * tq
    kern = functools.partial(_b_attn_kernel, seq=seq, tq=tq, sub=sub)
    full = lambda shape: pl.BlockSpec(shape, lambda b, t: (0,) * len(shape))
    return pl.pallas_call(
        kern,
        grid=(bsz, nt),
        in_specs=[pl.BlockSpec(memory_space=pltpu.SMEM),
                  pl.BlockSpec((step, GROUP_W), lambda b, t: (b * nt + t, BLK_B_Q)),
                  pl.BlockSpec((step, GROUP_W), lambda b, t: (b * nt + t, BLK_B_Z)),
                  pl.BlockSpec((seq, LANES), lambda b, t: (b, UNIT_B_K)),
                  pl.BlockSpec((seq, LANES), lambda b, t: (b, UNIT_B_V)),
                  pl.BlockSpec((sub, seq // 256, tq, 256), lambda b, t: (b * nt + t, 0, 0, 0)),
                  full((1, GROUP_W)), full((1, LANES)), full((N_HEADS64, 512)), full((N_HEADS64, 512))],
        out_specs=pl.BlockSpec((step, GROUP_W), lambda b, t: (b * nt + t, 0)),
        out_shape=jax.ShapeDtypeStruct((bsz * seq, GROUP_W), BF16),
        scratch_shapes=[pltpu.VMEM((seq, LANES), BF16), pltpu.VMEM((seq, LANES), BF16),
                        pltpu.VMEM((3, rows, 256), F32), pltpu.VMEM((rows, LANES), BF16),
                        pltpu.VMEM((seq // 256, rows, 256), F32),
                        pltpu.VMEM((rows, LANES), F32), pltpu.VMEM((rows, LANES), F32)],
        compiler_params=_params(2),
    )(cfar, h, h, h, h, mask, qg, kg, base0, base1)


def _rope(tile, cos, sin):
    return tile * cos + pltpu.roll(tile, 64, 1) * sin


def _c_prep_kernel(lat_ref, kr_ref, wq_ref, wkv_ref, qag_ref, kvag_ref, qg_ref, kg_ref, cos_ref, sin_ref,
                   qo_ref, ko_ref, vo_ref):
    cq = lat_ref[:, :Q_LORA].astype(F32)
    ms = jnp.mean(cq * cq, axis=-1, keepdims=True)
    cqn = (cq * lax.rsqrt(ms + EPS) * qag_ref[...]).astype(BF16)
    qpre = jnp.dot(cqn, wq_ref[...], preferred_element_type=F32)
    ckv = lat_ref[:, Q_LORA:].astype(F32)
    ms = jnp.mean(ckv * ckv, axis=-1, keepdims=True)
    ckvn = (ckv * lax.rsqrt(ms + EPS) * kvag_ref[...]).astype(BF16)
    kvpre = jnp.dot(ckvn, wkv_ref[...], preferred_element_type=F32)
    lane = lax.broadcasted_iota(jnp.int32, kr_ref.shape, 1)
    kr = jnp.where((lane % 64) < 32, kr_ref[...].astype(F32), 0.0)
    kr_ss = jnp.sum(kr * kr, axis=-1, keepdims=True)
    cos = cos_ref[...]
    sin = sin_ref[...]
    qg = qg_ref[...]
    kg = kg_ref[...]
    kr_rot = _rope(kr * kg[:, LANES:], cos, sin)
    for h in range(C_HEADS):
        qh = qpre[:, h * 256:(h + 1) * 256]
        r = lax.rsqrt(jnp.sum(qh * qh, axis=-1, keepdims=True) * (1.0 / C_QK) + EPS)
        qn = qh * r * qg
        qo_ref[:, h * 256:h * 256 + LANES] = qn[:, :LANES].astype(BF16)
        qo_ref[:, h * 256 + LANES:(h + 1) * 256] = _rope(qn[:, LANES:], cos, sin).astype(BF16)
        kn = kvpre[:, h * LANES:(h + 1) * LANES]
        r = lax.rsqrt((jnp.sum(kn * kn, axis=-1, keepdims=True) + kr_ss) * (1.0 / C_QK) + EPS)
        ko_ref[:, h * 256:h * 256 + LANES] = (kn * r * kg[:, :LANES]).astype(BF16)
        ko_ref[:, h * 256 + LANES:(h + 1) * 256] = (kr_rot * r).astype(BF16)
    vo_ref[...] = kvpre[:, C_HEADS * LANES:].astype(BF16)


def _c_prep(h, wq, wkv, qag, kvag, qg, kg, cos, sin, seq, tm=1024):
    n = h.shape[0]
    tm = min(tm, seq)
    ns = seq // tm
    full = lambda shape: pl.BlockSpec(shape, lambda i: (0,) * len(shape))
    return pl.pallas_call(
        _c_prep_kernel,
        grid=(n // tm,),
        in_specs=[pl.BlockSpec((tm, GROUP_W), lambda i: (i, BLK_C_QKV)),
                  pl.BlockSpec((tm, LANES), lambda i: (i, UNIT_KR_IW)),
                  full((Q_LORA, 4 * 256)), full((KV_LORA, 8 * LANES)),
                  full((1, Q_LORA)), full((1, LANES)), full((1, 256)), full((1, 256)),
                  pl.BlockSpec((tm, LANES), lambda i: (i % ns, 0)),
                  pl.BlockSpec((tm, LANES), lambda i: (i % ns, 0))],
        out_specs=[pl.BlockSpec((tm, 4 * 256), lambda i: (i, 0)),
                   pl.BlockSpec((tm, 4 * 256), lambda i: (i, 0)),
                   pl.BlockSpec((tm, GROUP_W), lambda i: (i, 0))],
        out_shape=[jax.ShapeDtypeStruct((n, 4 * 256), BF16), jax.ShapeDtypeStruct((n, 4 * 256), BF16),
                   jax.ShapeDtypeStruct((n, GROUP_W), BF16)],
        compiler_params=_params(1),
    )(h, h, wq, wkv, qag, kvag, qg, kg, cos, sin)


def _c_attn_kernel(q_ref, k_ref, v_ref, z_ref, o_ref, *scratch, tq, sub):
    for j in range(sub):
        rows = slice(j * tq, (j + 1) * tq)
        _c_attn_tile(pl.program_id(1) * sub + j, q_ref.at[rows, :], k_ref, v_ref, z_ref.at[rows, :],
                     o_ref.at[rows, :], *scratch, tq=tq)


def _c_attn_tile(qt, q_ref, k_ref, v_ref, z_ref, o_ref, s_ref, mp_ref, lp_ref, acc_ref, *, tq):
    scale = C_QK ** -0.5 * LOG2E
    hrows = [slice(h * tq, (h + 1) * tq) for h in range(C_HEADS)]
    qchunk = (qt * tq + lax.broadcasted_iota(jnp.int32, (tq, 256), 0)) // CHUNK
    kcol = lax.broadcasted_iota(jnp.int32, (tq, 256), 1)
    mp_ref[...] = jnp.full(mp_ref.shape, NEG, F32)
    lp_ref[...] = jnp.zeros(lp_ref.shape, F32)
    acc_ref[...] = jnp.zeros(acc_ref.shape, F32)
    nfull = (qt * tq) // 256
    nblk = ((qt + 1) * tq) // 256

    def logits_blocks(kb0, n, masked):
        for i in range(n):
            kb = kb0 + i
            off = pl.multiple_of(kb * 256, 256)
            for h in range(C_HEADS):
                cols = slice(h * 256, (h + 1) * 256)
                s = _dot_t(q_ref[:, cols], k_ref[pl.ds(off, 256), cols]) * scale
                if masked:
                    s = jnp.where(((kb * 256 + kcol) // CHUNK) <= qchunk, s, NEG)
                s_ref[kb, hrows[h], :] = s
                mp_ref[hrows[h], :] = jnp.maximum(mp_ref[hrows[h], :], jnp.maximum(s[:, :LANES], s[:, LANES:]))

    def value_blocks(kb0, n):
        off = pl.multiple_of(kb0 * 256, 256)
        for h in range(C_HEADS):
            m = mp_ref[hrows[h], :]
            mm = jnp.concatenate([m, m], axis=1)
            p = [jnp.exp2(s_ref[kb0 + i, hrows[h], :] - mm) for i in range(n)]
            lsum = p[0][:, :LANES] + p[0][:, LANES:]
            for pi in p[1:]:
                lsum = lsum + pi[:, :LANES] + pi[:, LANES:]
            lp_ref[hrows[h], :] += lsum
            pb = p[0].astype(BF16) if n == 1 else jnp.concatenate([pi.astype(BF16) for pi in p], axis=1)
            acc_ref[hrows[h], :] += jnp.dot(pb, v_ref[pl.ds(off, n * 256), h * LANES:(h + 1) * LANES],
                                            preferred_element_type=F32)

    _for_blocks(lambda kb0, n: logits_blocks(kb0, n, False), nfull)
    lax.fori_loop(nfull, nblk, lambda kb, c: (logits_blocks(kb, 1, True), c)[1], 0)
    for h in range(C_HEADS):
        m = jnp.max(mp_ref[hrows[h], :], axis=-1, keepdims=True)
        mp_ref[hrows[h], :] = jnp.broadcast_to(m, (tq, LANES))
    _for_blocks(value_blocks, nblk)
    gate = _silu(z_ref[...].astype(F32))
    for h in range(C_HEADS):
        cols = slice(h * LANES, (h + 1) * LANES)
        l = jnp.sum(lp_ref[hrows[h], :], axis=-1, keepdims=True)
        o_ref[:, cols] = (acc_ref[hrows[h], :] / l * gate[:, cols]).astype(BF16)


def _c_attn(qc, kc, vc, h, bsz, seq, tq=256, sub=2):
    step = sub * tq
    nt = seq // step
    rows = C_HEADS * tq
    kern = functools.partial(_c_attn_kernel, tq=tq, sub=sub)
    return pl.pallas_call(
        kern,
        grid=(bsz, nt),
        in_specs=[pl.BlockSpec((step, C_HEADS * 256), lambda b, t: (b * nt + t, 0)),
                  pl.BlockSpec((seq, C_HEADS * 256), lambda b, t: (b, 0)),
                  pl.BlockSpec((seq, GROUP_W), lambda b, t: (b, 0)),
                  pl.BlockSpec((step, GROUP_W), lambda b, t: (b * nt + t, BLK_C_Z))],
        out_specs=pl.BlockSpec((step, GROUP_W), lambda b, t: (b * nt + t, 0)),
        out_shape=jax.ShapeDtypeStruct((bsz * seq, GROUP_W), BF16),
        scratch_shapes=[pltpu.VMEM((seq // 256, rows, 256), F32), pltpu.VMEM((rows, LANES), F32),
                        pltpu.VMEM((rows, LANES), F32), pltpu.VMEM((rows, LANES), F32)],
        compiler_params=_params(2),
    )(qc, kc, vc, h)


def _mixer_d_kernel(q_ref, k_ref, v_ref, z_ref, qg_ref, kg_ref, base_ref, o_ref, *scratch, seq, tq, sub):
    for j in range(sub):
        rows = slice(j * tq, (j + 1) * tq)
        _mixer_d_tile(pl.program_id(1) * sub + j, q_ref.at[rows, :], k_ref, v_ref, z_ref.at[rows, :], qg_ref,
                      kg_ref, base_ref, o_ref.at[rows, :], *scratch, seq=seq, tq=tq)


def _mixer_d_tile(qt, q_ref, k_ref, v_ref, z_ref, qg_ref, kg_ref, base_ref, o_ref,
                  kpad_ref, vpad_ref, bias_ref, s_ref, mp_ref, *, seq, tq):
    b = pl.program_id(0)
    win = tq + D_LEFT

    @pl.when((b == 0) & (qt == 0))
    def _():
        qc = lax.broadcasted_iota(jnp.int32, (tq, win), 0) // CHUNK
        kc = lax.broadcasted_iota(jnp.int32, (tq, win), 1) // CHUNK
        band = (kc >= qc) & (kc <= qc + D_LEFT // CHUNK)
        for h in range(N_HEADS64):
            bias_ref[h] = jnp.where(band, _toeplitz(base_ref[h:h + 1, :], tq, win) * LOG2E, NEG)

    @pl.when(qt == 0)
    def _():
        kpad_ref[0:D_LEFT, :] = jnp.zeros((D_LEFT, GROUP_W), BF16)
        vpad_ref[0:D_LEFT, :] = jnp.zeros((D_LEFT, GROUP_W), BF16)
        for r in range(seq // 256):
            rows = slice(r * 256, (r + 1) * 256)
            dst = slice(D_LEFT + r * 256, D_LEFT + (r + 1) * 256)
            tiles = _rms_heads64(k_ref[rows, :].astype(F32), kg_ref[...], 4)
            for t in range(4):
                kpad_ref[dst, t * LANES:(t + 1) * LANES] = tiles[t].astype(BF16)
            vpad_ref[dst, :] = v_ref[rows, :]

    lo = _lo_mask(tq)
    qtiles = _rms_heads64(q_ref[...].astype(F32), qg_ref[...], 4)
    start = pl.multiple_of(qt * tq, tq)
    gate = _silu(z_ref[...].astype(F32))
    ntile = win // LANES

    def in_band(r, c):
        return c * LANES < r * CHUNK + D_LEFT + CHUNK and (c + 1) * LANES > r * CHUNK

    def pipeline(first):
        live = slice(first * LANES, win)
        wstart = pl.multiple_of(start + first * LANES, LANES)

        def logits(h):
            t, half = divmod(h, 2)
            kwin = kpad_ref[pl.ds(wstart, win - first * LANES), t * LANES:(t + 1) * LANES]
            sel = lo if half == 0 else jnp.logical_not(lo)
            qh = jnp.where(sel, qtiles[t] * (0.125 * LOG2E), 0.0).astype(BF16)
            s = _dot_t(qh, kwin) + bias_ref[h, :, live]
            s_ref[h, :, live] = s
            tiles = [s[:, i * LANES:(i + 1) * LANES] for i in range(ntile - first)]
            m = jnp.max(functools.reduce(jnp.maximum, tiles), axis=-1, keepdims=True)
            mp_ref[h] = jnp.broadcast_to(m, (tq, LANES))

        def values(h):
            vwin = vpad_ref[pl.ds(wstart, win - first * LANES), (h // 2) * LANES:(h // 2 + 1) * LANES]
            p_rows, l_rows = [], []
            for r in range(tq // CHUNK):
                rows = slice(r * CHUNK, (r + 1) * CHUNK)
                m = mp_ref[h, rows, :]
                tiles = [jnp.exp2(s_ref[h, rows, c * LANES:(c + 1) * LANES] - m) if in_band(r, c) else None
                         for c in range(first, ntile)]
                l_rows.append(functools.reduce(jnp.add, [t for t in tiles if t is not None]))
                p_rows.append(jnp.concatenate([jnp.zeros((CHUNK, LANES), BF16) if t is None else t.astype(BF16)
                                               for t in tiles], axis=1))
            l = jnp.sum(jnp.concatenate(l_rows, axis=0), axis=-1, keepdims=True)
            return jnp.dot(jnp.concatenate(p_rows, axis=0), vwin, preferred_element_type=F32) / l

        outs = []
        logits(0)
        for h in range(1, N_HEADS64 + 1):
            if h < N_HEADS64:
                logits(h)
            outs.append(values(h - 1))
            if h % 2 == 0:
                cols = slice((h // 2 - 1) * LANES, (h // 2) * LANES)
                o_ref[:, cols] = (jnp.where(lo, outs[h - 2], outs[h - 1]) * gate[:, cols]).astype(BF16)

    n_lead = D_LEFT // tq
    for lead in range(n_lead):
        pl.when(qt == lead)(functools.partial(pipeline, (D_LEFT - lead * tq) // LANES))
    pl.when(qt >= n_lead)(functools.partial(pipeline, 0))


def _mixer_d(h, qg, kg, base, bsz, seq, tq=256, sub=2):
    rows = sub * tq
    nt = seq // rows
    kern = functools.partial(_mixer_d_kernel, seq=seq, tq=tq, sub=sub)
    full = lambda shape: pl.BlockSpec(shape, lambda b, t: (0,) * len(shape))
    return pl.pallas_call(
        kern,
        grid=(bsz, nt),
        in_specs=[pl.BlockSpec((rows, GROUP_W), lambda b, t: (b * nt + t, BLK_D_Q)),
                  pl.BlockSpec((seq, GROUP_W), lambda b, t: (b, BLK_D_K)),
                  pl.BlockSpec((seq, GROUP_W), lambda b, t: (b, BLK_D_V)),
                  pl.BlockSpec((rows, GROUP_W), lambda b, t: (b * nt + t, BLK_D_Z)),
                  full((1, GROUP_W)), full((1, GROUP_W)), full((N_HEADS64, 2 * tq + D_LEFT))],
        out_specs=pl.BlockSpec((rows, GROUP_W), lambda b, t: (b * nt + t, 0)),
        out_shape=jax.ShapeDtypeStruct((bsz * seq, GROUP_W), BF16),
        scratch_shapes=[pltpu.VMEM((seq + D_LEFT, GROUP_W), BF16), pltpu.VMEM((seq + D_LEFT, GROUP_W), BF16),
                        pltpu.VMEM((N_HEADS64, tq, tq + D_LEFT), F32),
                        pltpu.VMEM((N_HEADS64, tq, tq + D_LEFT), F32), pltpu.VMEM((N_HEADS64, tq, LANES), F32)],
        compiler_params=_params(2),
    )(h, h, h, h, qg, kg, base)


def _w_in_pieces(take, zeros):
    c = lambda name, size, off=0: take(_SRC[name] + off, size)
    return [c("a_u", 512), c("a_v", 512), c("a_z", 512),
            c("b_q", 512), c("b_iq", 512), c("b_z", 512),
            c("b_k", 64), c("b_k", 64), c("b_v", 64), c("b_v", 64), c("b_ik", 64), c("b_ik", 64),
            c("c_kr", 32), c("b_iw", 8), zeros(24), c("c_kr", 32, 32), zeros(32),
            c("c_q", 384), c("c_kv", 128), c("c_z", 512),
            c("d_q", 512), c("d_k", 512), c("d_v", 512), c("d_z", 512)]


def _layout_w_in_kernel(w_ref, o_ref):
    tk = w_ref.shape[2]
    pieces = _w_in_pieces(lambda s, n: w_ref[0, s:s + n, :], lambda n: jnp.zeros((n, tk), F32))
    ends = np.cumsum([0] + [p.shape[0] for p in pieces])
    start = 0
    for i in range(1, len(pieces) + 1):
        if ends[i] % GROUP_W == 0:
            group = pieces[start:i]
            blk = group[0] if len(group) == 1 else jnp.concatenate(group, axis=0)
            o_ref[0, ends[start]:ends[i], :] = blk.astype(BF16)
            start = i


def _layout_w_in(w_in, tk=256):
    w_t = jnp.swapaxes(w_in, 1, 2)
    depth, cols, d = w_t.shape
    return pl.pallas_call(
        _layout_w_in_kernel,
        grid=(depth, d // tk),
        in_specs=[pl.BlockSpec((1, cols, tk), lambda l, i: (l, 0, i))],
        out_specs=pl.BlockSpec((1, H_COLS, tk), lambda l, i: (l, 0, i)),
        out_shape=jax.ShapeDtypeStruct((depth, H_COLS, d), BF16),
        compiler_params=_params(2),
    )(w_t)


def _rope_layout(v):
    z = jnp.zeros(v.shape[:-1] + (32,), v.dtype)
    return jnp.concatenate([v[..., :32], z, v[..., 32:], z], axis=-1)


def _layout_c(w_qb, w_kvb, q_gain, k_gain, qa_gain):
    wq = w_qb.reshape(Q_LORA, C_HEADS, C_QK)
    wq = jnp.concatenate([wq[..., :C_NOPE], _rope_layout(wq[..., C_NOPE:])], axis=-1)
    wq = wq.reshape(Q_LORA, C_HEADS * 256).astype(BF16)
    wkv = w_kvb.reshape(KV_LORA, C_HEADS, 2 * LANES)
    wkv = jnp.concatenate([wkv[..., :C_NOPE].reshape(KV_LORA, -1), wkv[..., C_NOPE:].reshape(KV_LORA, -1)],
                          axis=1).astype(BF16)
    lay = lambda g: jnp.concatenate([g[:C_NOPE], _rope_layout(g[C_NOPE:])])[None, :]
    return wq, wkv, lay(q_gain), lay(k_gain), qa_gain[None, :]


def _t5_bucket_static(rel):
    half = T5_BUCKETS // 2
    exact = half // 2
    n = abs(rel)
    if n < exact:
        val = n
    else:
        val = min(exact + (n * n // (exact * exact)).bit_length() - 1, half - 1)
    return (half if rel > 0 else 0) + val


def _t5_tables(t5_bias):
    m = np.arange(512)
    d0 = np.where(m < 256, m, m - 512)
    d1 = np.where(m <= 256, m - 256, m - 768)
    idx0 = np.array([_t5_bucket_static(int(d)) for d in d0], np.int32)
    idx1 = np.array([_t5_bucket_static(int(d)) for d in d1], np.int32)
    far = _t5_bucket_static(-512)
    return t5_bias[idx0].T, t5_bias[idx1].T, t5_bias[far]


def _band_table(rel_bias, tq):
    width = 2 * tq + D_LEFT
    m = np.arange(width)
    dist = np.where(m <= tq + D_LEFT, D_LEFT - m, D_LEFT + width - m)
    idx = np.clip(dist, -REL_CLIP, REL_CLIP) + REL_CLIP
    return rel_bias[idx.astype(np.int32)].T


def _rope_tables(seq):
    inv = ROPE_BASE ** (-jnp.arange(0, C_ROPE, 2, dtype=F32) / C_ROPE)
    ang = jnp.arange(seq, dtype=F32)[:, None] * inv[None, :]
    c, s = jnp.cos(ang), jnp.sin(ang)
    z = jnp.zeros_like(c)
    return jnp.concatenate([c, z, c, z], axis=1), jnp.concatenate([-s, z, s, z], axis=1)


def kernel(x, t5_bias, norm_g, w_in, a_v_gain, a_ws, a_bs, b_q_gain, b_k_gain, c_qa_gain, c_kva_gain,
           c_w_qb, c_w_kvb, c_q_gain, c_k_gain, d_q_gain, d_k_gain, d_rel_bias, w_out):
    bsz, seq, d_model = x.shape
    depth = w_in.shape[0]
    tq = 256
    assert seq % 512 == 0 and seq <= 2048 and d_model % 512 == 0, (seq, d_model)
    assert w_in.shape[2] == _SRC["d_z"] + GROUP_W and w_out.shape[1] == 4 * GROUP_W, (w_in.shape, w_out.shape)
    x2 = x.reshape(bsz * seq, d_model)
    cos, sin = _rope_tables(seq)
    base0, base1, cfar = _t5_tables(t5_bias)
    w_in_blocks = _layout_w_in(w_in)
    for l in range(depth):
        h = _inproj(x2, norm_g[l][None, :], w_in_blocks, l)
        mask = _b_select(h, bsz, seq)
        y_b = _b_attn(h, mask, cfar, jnp.tile(b_q_gain[l], N_HEADS64)[None, :],
                      jnp.tile(b_k_gain[l], 2)[None, :], base0, base1, bsz, seq, tq)
        wq, wkv, qg, kg, qag = _layout_c(c_w_qb[l], c_w_kvb[l], c_q_gain[l], c_k_gain[l], c_qa_gain[l])
        qc, kc, vc = _c_prep(h, wq, wkv, qag, c_kva_gain[l][None, :], qg, kg, cos, sin, seq)
        y_c = _c_attn(qc, kc, vc, h, bsz, seq, tq)
        y_d = _mixer_d(h, jnp.tile(d_q_gain[l], N_HEADS64)[None, :], jnp.tile(d_k_gain[l], N_HEADS64)[None, :],
                       _band_table(d_rel_bias[l], tq), bsz, seq, tq)
        x2 = _outproj(x2, h, a_v_gain[l][None, :], a_ws[l], a_bs[l][:, :, None], (y_b, y_c, y_d),
                      w_out[l].astype(BF16))
    return x2.reshape(bsz, seq, d_model)
```

```python
import functools
import math

import numpy as np
import jax
import jax.numpy as jnp
from jax import lax
from jax.experimental import pallas as pl
from jax.experimental.pallas import tpu as pltpu

F32 = jnp.float32
BF16 = jnp.bfloat16

EPS = 1e-6
NEG = -1e30
LOG2E = math.log2(math.e)
KEY_LOWEST = int(np.float32(-np.finfo(np.float32).max).view(np.int32)) ^ 0x7FFFFFFF
CHUNK = 64
LANES = 128
GROUP_W = 512
A_GROUPS = 4
GMLP_BLOCK = 128
N_HEADS64 = 8
IDX_SCALE = (8 ** -0.5) * 0.125
TOPK_MAX = 256
T5_BUCKETS = 32
C_HEADS = 4
C_NOPE = 128
C_ROPE = 64
C_QK = 192
Q_LORA = 384
KV_LORA = 128
ROPE_BASE = 10000.0
D_LEFT = 8 * CHUNK
REL_CLIP = 128
VMEM_LIMIT = 56 * 1024 * 1024
OUTPROJ_VMEM_LIMIT = 58 * 1024 * 1024

BLK_A_U, BLK_A_V, BLK_A_Z = 0, 1, 2
BLK_B_Q, BLK_B_IQ, BLK_B_Z, BLK_SMALL = 3, 4, 5, 6
BLK_C_QKV, BLK_C_Z = 7, 8
BLK_D_Q, BLK_D_K, BLK_D_V, BLK_D_Z = 9, 10, 11, 12
H_COLS = 13 * GROUP_W
UNIT_B_K, UNIT_B_V, UNIT_B_IK, UNIT_KR_IW = (BLK_SMALL * 4 + i for i in range(4))
IW_LANE = 32

_SRC = dict(a_u=0, a_v=512, a_z=1024, b_q=1536, b_k=2048, b_v=2112, b_iq=2176, b_ik=2688,
            b_iw=2752, b_z=2760, c_q=3272, c_kv=3656, c_kr=3784, c_z=3848,
            d_q=4360, d_k=4872, d_v=5384, d_z=5896)


def _params(n_axes, vmem_limit=VMEM_LIMIT):
    return pltpu.CompilerParams(dimension_semantics=("arbitrary",) * n_axes,
                                vmem_limit_bytes=vmem_limit)


def _gelu(x):
    c = math.sqrt(2.0 / math.pi)
    return x * (0.5 * (1.0 + jnp.tanh(c * (x + 0.044715 * (x * x * x)))))


def _silu(x):
    return x * (1.0 / (1.0 + jnp.exp(-x)))


def _dot_t(a, b):
    return lax.dot_general(a, b, (((1,), (1,)), ((), ())), preferred_element_type=F32)


def _lo_mask(rows):
    return lax.broadcasted_iota(jnp.int32, (rows, LANES), 1) < 64


def _rms_heads64(x, gain, ntiles):
    lo = _lo_mask(x.shape[0])
    tiles = []
    for t in range(ntiles):
        xt = x[:, t * LANES:(t + 1) * LANES]
        sq = xt * xt
        s_lo = jnp.sum(jnp.where(lo, sq, 0.0), axis=-1, keepdims=True)
        s_hi = jnp.sum(jnp.where(lo, 0.0, sq), axis=-1, keepdims=True)
        r = jnp.where(lo, lax.rsqrt(s_lo * (1.0 / 64) + EPS), lax.rsqrt(s_hi * (1.0 / 64) + EPS))
        tiles.append(xt * r * gain[:, t * LANES:(t + 1) * LANES])
    return tiles


def _for_blocks(blocks, count):
    def quad(i, carry):
        blocks(4 * i, 4)
        return carry

    lax.fori_loop(0, count // 4, quad, 0)
    first = (count // 4) * 4

    @pl.when((count & 2) != 0)
    def _():
        blocks(first, 2)

    @pl.when((count & 1) != 0)
    def _():
        blocks(first + (count & 2), 1)


def _toeplitz(base_row, rows, width):
    t = jnp.broadcast_to(base_row, (rows, base_row.shape[1]))
    t = pltpu.roll(t, 0, 1, stride=1, stride_axis=0)
    return t[:, :width]


def _inproj_kernel(x_ref, g_ref, w_ref, o_ref):
    x = x_ref[...]
    ms = jnp.mean(x * x, axis=-1, keepdims=True)
    xn = (x * lax.rsqrt(ms + EPS) * g_ref[...]).astype(BF16)
    for c in range(H_COLS // GROUP_W):
        cols = slice(c * GROUP_W, (c + 1) * GROUP_W)
        o_ref[:, cols] = _dot_t(xn, w_ref[cols, :]).astype(BF16)


def _inproj(x2, g, w_all, layer, tm=512):
    n, d = x2.shape
    return pl.pallas_call(
        _inproj_kernel,
        grid=(n // tm,),
        in_specs=[pl.BlockSpec((tm, d), lambda i: (i, 0)),
                  pl.BlockSpec((1, d), lambda i: (0, 0)),
                  pl.BlockSpec((None, H_COLS, d), lambda i: (layer, 0, 0), pipeline_mode=pl.Buffered(1))],
        out_specs=pl.BlockSpec((tm, H_COLS), lambda i: (i, 0)),
        out_shape=jax.ShapeDtypeStruct((n, H_COLS), BF16),
        compiler_params=_params(1),
    )(x2, g, w_all)


def _outproj_kernel(x_ref, uvz_ref, vg_ref, ws_ref, bs_ref, yb_ref, yc_ref, yd_ref, w_ref, o_ref, ya_ref):
    u_ref, v_ref, z_ref = (uvz_ref.at[:, blk * GROUP_W:(blk + 1) * GROUP_W] for blk in (BLK_A_U, BLK_A_V, BLK_A_Z))
    tm, d = x_ref.shape
    wgs = _mixer_a_weights(ws_ref)
    nblk = tm // GMLP_BLOCK
    ncol = d // nblk
    for c in range(nblk):
        cols = slice(c * ncol, (c + 1) * ncol)
        acc = x_ref[:, cols]
        for g, y_ref in ((1, yb_ref), (2, yc_ref), (3, yd_ref)):
            acc = acc + jnp.dot(y_ref[...], w_ref[g * GROUP_W:(g + 1) * GROUP_W, cols],
                                preferred_element_type=F32)
        o_ref[:, cols] = acc
        _mixer_a_rows(u_ref, v_ref, z_ref, vg_ref, wgs, bs_ref, ya_ref, c)
    for c in range(nblk):
        cols = slice(c * ncol, (c + 1) * ncol)
        o_ref[:, cols] += jnp.dot(ya_ref[...], w_ref[0:GROUP_W, cols], preferred_element_type=F32)


def _outproj(x2, h, vg, ws, bs, ys, w, tm=1024):
    n, d = x2.shape
    yspec = pl.BlockSpec((tm, GROUP_W), lambda i: (i, 0))
    return pl.pallas_call(
        _outproj_kernel,
        grid=(n // tm,),
        in_specs=[pl.BlockSpec((tm, d), lambda i: (i, 0)),
                  pl.BlockSpec((tm, 3 * GROUP_W), lambda i: (i, 0)),
                  pl.BlockSpec((1, GROUP_W), lambda i: (0, 0)),
                  pl.BlockSpec((A_GROUPS, GMLP_BLOCK, GMLP_BLOCK), lambda i: (0, 0, 0)),
                  pl.BlockSpec((A_GROUPS, GMLP_BLOCK, 1), lambda i: (0, 0, 0)),
                  yspec, yspec, yspec,
                  pl.BlockSpec((4 * GROUP_W, d), lambda i: (0, 0), pipeline_mode=pl.Buffered(1))],
        out_specs=pl.BlockSpec((tm, d), lambda i: (i, 0)),
        out_shape=jax.ShapeDtypeStruct((n, d), F32),
        scratch_shapes=[pltpu.VMEM((tm, GROUP_W), BF16)],
        compiler_params=_params(1, OUTPROJ_VMEM_LIMIT),
    )(x2, h, vg, ws, bs, *ys, w)


def _mixer_a_weights(w_ref):
    i = lax.broadcasted_iota(jnp.int32, (GMLP_BLOCK, GMLP_BLOCK), 0)
    j = lax.broadcasted_iota(jnp.int32, (GMLP_BLOCK, GMLP_BLOCK), 1)
    keep = (j // CHUNK) <= (i // CHUNK)
    return [jnp.where(keep, w_ref[g], 0.0).astype(BF16) for g in range(A_GROUPS)]


def _mixer_a_rows(u_ref, v_ref, z_ref, vg_ref, wgs, b_ref, o_ref, blk):
    rows = slice(blk * GMLP_BLOCK, (blk + 1) * GMLP_BLOCK)
    u = _gelu(u_ref[rows, :].astype(F32))
    v = _gelu(v_ref[rows, :].astype(F32))
    ms = jnp.mean(v * v, axis=-1, keepdims=True)
    vb = (v * lax.rsqrt(ms + EPS) * vg_ref[...]).astype(BF16)
    gate = _silu(z_ref[rows, :].astype(F32))
    for g in range(A_GROUPS):
        cols = slice(g * LANES, (g + 1) * LANES)
        sg = jnp.dot(wgs[g], vb[:, cols], preferred_element_type=F32) + b_ref[g]
        o_ref[rows, cols] = (u[:, cols] * sg * gate[:, cols]).astype(BF16)


def _order_key(x):
    return jnp.where(x < 0, x ^ 0x7FFFFFFF, x)


def _b_select_kernel(iq_ref, iw_ref, ik_ref, o_ref, lhs_ref, sc_ref, *, seq, tq, topk):
    t_blk = pl.program_id(1)
    nkb = seq // 256
    nblk = ((t_blk + 1) * tq) // 256
    n_interp = 12
    n_unchecked = 13 + 2 * t_blk
    hrows = [slice(h * tq, (h + 1) * tq) for h in range(N_HEADS64)]

    lo_half = _lo_mask(tq)
    w_t = (iw_ref[...].astype(F32) * IDX_SCALE).T
    for h in range(N_HEADS64):
        iqt = iq_ref[:, (h // 2) * LANES:(h // 2 + 1) * LANES]
        sel = lo_half if h % 2 == 0 else jnp.logical_not(lo_half)
        lhs_ref[hrows[h], :] = jnp.where(sel, iqt, jnp.zeros_like(iqt))

    def qpos(ln):
        return t_blk * tq + ln.start + lax.broadcasted_iota(jnp.int32, (256, ln.stop - ln.start), 1)

    def krow(ln):
        return lax.broadcasted_iota(jnp.int32, (256, ln.stop - ln.start), 0)

    assert tq in (256, 512)
    every = slice(0, tq)
    tail = slice(256, tq) if tq == 512 else None
    nwide = nblk - 1 if tail else nblk

    def fold8(x):
        return jnp.sum(x.reshape(256 // 8, 8, x.shape[1]), axis=0)

    def widen(part, ln):
        return part if ln == every else jnp.concatenate([jnp.zeros((8, ln.start), F32), part], axis=1)

    def score_block(kb, carry, masked, ln):
        amax, n_pos, n_nn = carry
        off = pl.multiple_of(kb * 256, 256)
        ikblk = ik_ref[pl.ds(off, 256), :]
        score = jnp.zeros((256, ln.stop - ln.start), F32)
        for h in range(N_HEADS64):
            w_h = w_t[IW_LANE + h:IW_LANE + h + 1, ln]
            q_h = lhs_ref[h * tq + ln.start:h * tq + ln.stop, :]
            score = score + w_h * jnp.maximum(_dot_t(ikblk, q_h), 0.0)
        mag = jnp.abs(score)
        if masked:
            adm = ((kb * 256 + krow(ln)) // CHUNK) <= (qpos(ln) // CHUNK)
            score = jnp.where(adm, score, -jnp.inf)
            mag = jnp.where(adm, mag, 0.0)
        sc_ref[kb, :, ln] = score
        return (jnp.maximum(amax, widen(jnp.max(mag.reshape(256 // 8, 8, mag.shape[1]), axis=0), ln)),
                n_pos + widen(fold8(jnp.where(score > 0.0, 1.0, 0.0)), ln),
                n_nn + widen(fold8(jnp.where(score >= 0.0, 1.0, 0.0)), ln))

    nfull = (t_blk * tq) // 256
    zeros8 = jnp.zeros((8, tq), F32)
    def score_pair(i, c):
        return score_block(2 * i + 1, score_block(2 * i, c, False, every), False, every)

    carry = lax.fori_loop(0, nfull // 2, score_pair, (zeros8, zeros8, zeros8))
    carry = lax.fori_loop(nfull - (nfull & 1), nfull, lambda kb, c: score_block(kb, c, False, every), carry)
    carry = lax.fori_loop(nfull, nwide, lambda kb, c: score_block(kb, c, True, every), carry)
    if tail:
        carry = score_block(nblk - 1, carry, True, tail)
    amax = jnp.max(carry[0], axis=0, keepdims=True)
    f_pos = jnp.sum(carry[1], axis=0, keepdims=True)
    f_nn = jnp.sum(carry[2], axis=0, keepdims=True)

    def count(pred):
        def body(kb, acc):
            return acc + fold8(jnp.where(pred(sc_ref[kb], kb, every), 1.0, 0.0))

        def run(first, n, acc):
            for i in range(n):
                acc = body(first + i, acc)
            return acc

        acc = lax.fori_loop(0, nwide // 4, lambda i, a: run(4 * i, 4, a), zeros8)
        done4 = (nwide // 4) * 4
        acc = lax.fori_loop(0, (nwide >> 1) & 1, lambda i, a: run(done4, 2, a), acc)
        acc = lax.fori_loop(0, nwide & 1, lambda i, a: run(done4 + (nwide & 2), 1, a), acc)
        if tail:
            acc = acc + widen(fold8(jnp.where(pred(sc_ref[nblk - 1, :, tail], nblk - 1, tail), 1.0, 0.0)), tail)
        return jnp.sum(acc, axis=0, keepdims=True)

    kf = float(topk)
    qrow = t_blk * tq + lax.broadcasted_iota(jnp.int32, (1, tq), 1)
    n_adm = ((qrow // CHUNK + 1) * CHUNK).astype(F32)
    one = jnp.ones((1, tq), jnp.int32)
    pos = f_pos > kf
    neg = f_nn < kf
    lo0 = jnp.where(pos, one, _order_key(lax.bitcast_convert_type(-amax, jnp.int32)))
    hi0 = jnp.where(neg, one - 1, _order_key(lax.bitcast_convert_type(amax, jnp.int32)) + 1)
    w_lo0 = jnp.where(pos, f_pos, n_adm) - kf
    w_hi0 = kf - jnp.where(neg, f_nn, 0.0)
    all_sel = n_adm <= kf
    at_zero = jnp.logical_not(pos | neg)
    done0 = jnp.where(all_sel | at_zero | (hi0 == lo0 + 1), 1.0, 0.0)
    thr0 = jnp.where(all_sel, KEY_LOWEST, jnp.where(at_zero, jnp.where(f_pos == kf, one, one - 1), lo0))

    def as_score(key):
        return lax.bitcast_convert_type(_order_key(key), F32)

    def search_cond(st):
        return jnp.logical_and(st[0][0] < n_interp + 32, st[1] < 0.5)

    def search_step(st):
        it, lo, hi, w_lo, w_hi, side, done, thr = st
        lo_v = as_score(lo)
        hi_v = as_score(hi)
        c_v = lo_v + (hi_v - lo_v) * (w_lo / (w_lo + w_hi))
        c_interp = _order_key(lax.bitcast_convert_type(c_v, jnp.int32))
        c_mid = (lo >> 1) + (hi >> 1) + (lo & hi & 1)
        cand = jnp.where(it < n_interp, c_interp, c_mid)
        cand = jnp.minimum(jnp.maximum(cand, lo + 1), hi - 1)
        cand_v = as_score(cand)
        f = count(lambda s, kb, ln: s >= cand_v[:, ln])
        live = done < 0.5
        up = f > kf
        hit = f == kf
        new_lo = jnp.where(live & up, cand, lo)
        new_hi = jnp.where(live & jnp.logical_not(up), cand, hi)
        new_w_lo = jnp.where(up, f - kf, jnp.where(side < 0.0, 0.5 * w_lo, w_lo))
        new_w_hi = jnp.where(up, jnp.where(side > 0.0, 0.5 * w_hi, w_hi), kf - f)
        new_side = jnp.where(up, 1.0, -1.0)
        new_thr = jnp.where(live, jnp.where(hit, cand, new_lo), thr)
        new_done = jnp.where(live & (hit | (new_hi == new_lo + 1)), 1.0, done)
        return (it + 1, new_lo, new_hi, jnp.where(live, new_w_lo, w_lo),
                jnp.where(live, new_w_hi, w_hi), jnp.where(live, new_side, side), new_done, new_thr)

    def checked_step(st):
        new = search_step(st[0])
        return new, jnp.min(new[-2])

    state = (jnp.int32(0), lo0, hi0, w_lo0, w_hi0, jnp.zeros((1, tq), F32), done0, thr0)
    state = lax.fori_loop(0, n_unchecked, lambda i, st: search_step(st), state)
    thr = as_score(lax.while_loop(search_cond, checked_step, (state, jnp.min(state[-2])))[0][-1])

    any_excess = jnp.max(count(lambda s, kb, ln: s >= thr[:, ln])) > kf

    def store_mask(kb, keep_t, ln):
        for g in range(tq // 256):
            if g * 256 >= ln.start:
                o_ref[g, kb] = keep_t[:, g * 256 - ln.start:(g + 1) * 256 - ln.start]
            else:
                o_ref[g, kb] = jnp.zeros((256, 256), F32)

    def write_mask(keep):
        def body(kb, carry):
            store_mask(kb, jnp.where(keep(sc_ref[kb], kb, every), 1.0, 0.0), every)
            return carry

        lax.fori_loop(0, nwide, body, 0)
        if tail:
            store_mask(nblk - 1, jnp.where(keep(sc_ref[nblk - 1, :, tail], nblk - 1, tail), 1.0, 0.0), tail)

    def write_unused(kb, carry):
        store_mask(kb, jnp.zeros((256, tq), F32), every)
        return carry

    lax.fori_loop(nblk, nkb, write_unused, 0)

    @pl.when(jnp.logical_not(any_excess))
    def _():
        write_mask(lambda s, kb, ln: s >= thr[:, ln])

    @pl.when(any_excess)
    def _():
        need = kf - count(lambda s, kb, ln: s > thr[:, ln])

        def idx_step(it, jmax):
            cand = jmax | lax.shift_left(jnp.int32(1), 10 - it)
            below = count(lambda s, kb, ln: (s == thr[:, ln]) & ((kb * 256 + krow(ln)) < cand[:, ln]))
            return jnp.where(below < need, cand, jmax)

        jmax = lax.fori_loop(0, 11, idx_step, jnp.zeros((1, tq), jnp.int32))
        write_mask(lambda s, kb, ln: (s > thr[:, ln]) | ((s == thr[:, ln]) & ((kb * 256 + krow(ln)) <= jmax[:, ln])))


def _b_select(h, bsz, seq, tq=512):
    tq = min(tq, seq)
    nt = seq // tq
    topk = min(TOPK_MAX, seq // 4)
    kern = functools.partial(_b_select_kernel, seq=seq, tq=tq, topk=topk)
    return pl.pallas_call(
        kern,
        grid=(bsz, nt),
        in_specs=[pl.BlockSpec((tq, GROUP_W), lambda b, t: (b * nt + t, BLK_B_IQ)),
                  pl.BlockSpec((tq, LANES), lambda b, t: (b * nt + t, UNIT_KR_IW)),
                  pl.BlockSpec((seq, LANES), lambda b, t: (b, UNIT_B_IK))],
        out_specs=pl.BlockSpec((tq // 256, seq // 256, 256, 256), lambda b, t: (b * nt + t, 0, 0, 0)),
        out_shape=jax.ShapeDtypeStruct((bsz * seq // 256, seq // 256, 256, 256), F32),
        scratch_shapes=[pltpu.VMEM((N_HEADS64 * tq, LANES), BF16),
                        pltpu.VMEM((seq // 256, 256, tq), F32)],
        compiler_params=_params(2),
    )(h, h, h)


def _b_attn_kernel(cfar_ref, q_ref, z_ref, k_ref, v_ref, msk_ref, qg_ref, kg_ref, base0_ref, base1_ref,
                   o_ref, *scratch, seq, tq, sub):
    for j in range(sub):
        rows = slice(j * tq, (j + 1) * tq)
        _b_attn_tile(pl.program_id(1) * sub + j, cfar_ref, q_ref.at[rows, :], z_ref.at[rows, :], k_ref, v_ref,
                     msk_ref.at[j:j + 1], qg_ref, kg_ref, base0_ref, base1_ref, o_ref.at[rows, :], *scratch,
                     seq=seq, tq=tq)


def _b_attn_tile(t_blk, cfar_ref, q_ref, z_ref, k_ref, v_ref, msk_ref, qg_ref, kg_ref, base0_ref, base1_ref,
                 o_ref, kn_ref, v1_ref, bias_ref, qall_ref, s_ref, mp_ref, acc_ref, *, seq, tq):
    b = pl.program_id(0)
    hrows = [slice(h * tq, (h + 1) * tq) for h in range(N_HEADS64)]

    @pl.when((b == 0) & (t_blk == 0))
    def _():
        for h in range(N_HEADS64):
            bias_ref[0, hrows[h], :] = jnp.full((tq, 256), cfar_ref[h] * LOG2E, F32)
            bias_ref[1, hrows[h], :] = _toeplitz(base1_ref[h:h + 1, :], tq, 256) * LOG2E
            bias_ref[2, hrows[h], :] = _toeplitz(base0_ref[h:h + 1, :], tq, 256) * LOG2E

    @pl.when(t_blk == 0)
    def _():
        lo256 = _lo_mask(256)
        for r in range(seq // 256):
            rows = slice(r * 256, (r + 1) * 256)
            k = k_ref[rows, :].astype(F32)
            ms = jnp.mean(k * k, axis=-1, keepdims=True)
            kn_ref[rows, :] = (k * lax.rsqrt(ms + EPS) * kg_ref[...]).astype(BF16)
            v = v_ref[rows, :]
            v1_ref[rows, :] = jnp.where(lo256, v, jnp.ones_like(v))

    lo = _lo_mask(tq)
    qtiles = _rms_heads64(q_ref[...].astype(F32), qg_ref[...], 4)
    for h in range(N_HEADS64):
        sel = lo if h % 2 == 0 else jnp.logical_not(lo)
        qall_ref[hrows[h], :] = jnp.where(sel, qtiles[h // 2] * (0.125 * LOG2E), 0.0).astype(BF16)
    mp_ref[...] = jnp.full(mp_ref.shape, NEG, F32)
    acc_ref[...] = jnp.zeros(acc_ref.shape, F32)
    nblk = t_blk + 1

    def logits_blocks(kb0, n):
        for i in range(n):
            kb = kb0 + i
            off = pl.multiple_of(kb * 256, 256)
            kblk = kn_ref[pl.ds(off, 256), :]
            which = jnp.clip(kb - (t_blk - 2), 0, 2)
            keep = msk_ref[0, kb].T > 0.5
            for h in range(N_HEADS64):
                s = jnp.where(keep, _dot_t(qall_ref[hrows[h], :], kblk) + bias_ref[which, hrows[h], :], NEG)
                s_ref[kb, hrows[h], :] = s
                mp_ref[hrows[h], :] = jnp.maximum(mp_ref[hrows[h], :], jnp.maximum(s[:, :LANES], s[:, LANES:]))

    def value_blocks(kb0, n):
        off = pl.multiple_of(kb0 * 256, 256)
        v1 = v1_ref[pl.ds(off, n * 256), :]
        for h in range(N_HEADS64):
            m = mp_ref[hrows[h], :]
            mm = jnp.concatenate([m, m], axis=1)
            p = [jnp.exp2(s_ref[kb0 + i, hrows[h], :] - mm).astype(BF16) for i in range(n)]
            p = p[0] if n == 1 else jnp.concatenate(p, axis=1)
            acc_ref[hrows[h], :] += jnp.dot(p, v1, preferred_element_type=F32)

    _for_blocks(logits_blocks, nblk)
    for h in range(N_HEADS64):
        m = jnp.max(mp_ref[hrows[h], :], axis=-1, keepdims=True)
        mp_ref[hrows[h], :] = jnp.broadcast_to(m, (tq, LANES))
    _for_blocks(value_blocks, nblk)

    gate = _silu(z_ref[...].astype(F32))
    for t in range(4):
        a_even = acc_ref[hrows[2 * t], :]
        a_odd = acc_ref[hrows[2 * t + 1], :]
        o_even = a_even / pltpu.roll(a_even, 64, 1)
        o_odd = pltpu.roll(a_odd, 64, 1) / a_odd
        cols = slice(t * LANES, (t + 1) * LANES)
        o_ref[:, cols] = (jnp.where(lo, o_even, o_odd) * gate[:, cols]).astype(BF16)


def _b_attn(h, mask, cfar, qg, kg, base0, base1, bsz, seq, tq=256, sub=2):
    step = sub * tq
    nt = seq // step
    rows = N_HEADS64 * tq
    kern = functools.partial(_b_attn_kernel, seq=seq, tq=tq, sub=sub)
    full = lambda shape: pl.BlockSpec(shape, lambda b, t: (0,) * len(shape))
    return pl.pallas_call(
        kern,
        grid=(bsz, nt),
        in_specs=[pl.BlockSpec(memory_space=pltpu.SMEM),
                  pl.BlockSpec((step, GROUP_W), lambda b, t: (b * nt + t, BLK_B_Q)),
                  pl.BlockSpec((step, GROUP_W), lambda b, t: (b * nt + t, BLK_B_Z)),
                  pl.BlockSpec((seq, LANES), lambda b, t: (b, UNIT_B_K)),
                  pl.BlockSpec((seq, LANES), lambda b, t: (b, UNIT_B_V)),
                  pl.BlockSpec((sub, seq // 256, tq, 256), lambda b, t: (b * nt + t, 0, 0, 0)),
                  full((1, GROUP_W)), full((1, LANES)), full((N_HEADS64, 512)), full((N_HEADS64, 512))],
        out_specs=pl.BlockSpec((step, GROUP_W), lambda b, t: (b * nt + t, 0)),
        out_shape=jax.ShapeDtypeStruct((bsz * seq, GROUP_W), BF16),
        scratch_shapes=[pltpu.VMEM((seq, LANES), BF16), pltpu.VMEM((seq, LANES), BF16),
                        pltpu.VMEM((3, rows, 256), F32), pltpu.VMEM((rows, LANES), BF16),
                        pltpu.VMEM((seq // 256, rows, 256), F32),
                        pltpu.VMEM((rows, LANES), F32), pltpu.VMEM((rows, LANES), F32)],
        compiler_params=_params(2),
    )(cfar, h, h, h, h, mask, qg, kg, base0, base1)


def _rope(tile, cos, sin):
    return tile * cos + pltpu.roll(tile, 64, 1) * sin


def _c_prep_kernel(lat_ref, kr_ref, wq_ref, wkv_ref, qag_ref, kvag_ref, qg_ref, kg_ref, cos_ref, sin_ref,
                   qo_ref, ko_ref, vo_ref):
    cq = lat_ref[:, :Q_LORA].astype(F32)
    ms = jnp.mean(cq * cq, axis=-1, keepdims=True)
    cqn = (cq * lax.rsqrt(ms + EPS) * qag_ref[...]).astype(BF16)
    qpre = jnp.dot(cqn, wq_ref[...], preferred_element_type=F32)
    ckv = lat_ref[:, Q_LORA:].astype(F32)
    ms = jnp.mean(ckv * ckv, axis=-1, keepdims=True)
    ckvn = (ckv * lax.rsqrt(ms + EPS) * kvag_ref[...]).astype(BF16)
    kvpre = jnp.dot(ckvn, wkv_ref[...], preferred_element_type=F32)
    lane = lax.broadcasted_iota(jnp.int32, kr_ref.shape, 1)
    kr = jnp.where((lane % 64) < 32, kr_ref[...].astype(F32), 0.0)
    kr_ss = jnp.sum(kr * kr, axis=-1, keepdims=True)
    cos = cos_ref[...]
    sin = sin_ref[...]
    qg = qg_ref[...]
    kg = kg_ref[...]
    kr_rot = _rope(kr * kg[:, LANES:], cos, sin)
    for h in range(C_HEADS):
        qh = qpre[:, h * 256:(h + 1) * 256]
        r = lax.rsqrt(jnp.sum(qh * qh, axis=-1, keepdims=True) * (1.0 / C_QK) + EPS)
        qn = qh * r * qg
        qo_ref[:, h * 256:h * 256 + LANES] = qn[:, :LANES].astype(BF16)
        qo_ref[:, h * 256 + LANES:(h + 1) * 256] = _rope(qn[:, LANES:], cos, sin).astype(BF16)
        kn = kvpre[:, h * LANES:(h + 1) * LANES]
        r = lax.rsqrt((jnp.sum(kn * kn, axis=-1, keepdims=True) + kr_ss) * (1.0 / C_QK) + EPS)
        ko_ref[:, h * 256:h * 256 + LANES] = (kn * r * kg[:, :LANES]).astype(BF16)
        ko_ref[:, h * 256 + LANES:(h + 1) * 256] = (kr_rot * r).astype(BF16)
    vo_ref[...] = kvpre[:, C_HEADS * LANES:].astype(BF16)


def _c_prep(h, wq, wkv, qag, kvag, qg, kg, cos, sin, seq, tm=1024):
    n = h.shape[0]
    tm = min(tm, seq)
    ns = seq // tm
    full = lambda shape: pl.BlockSpec(shape, lambda i: (0,) * len(shape))
    return pl.pallas_call(
        _c_prep_kernel,
        grid=(n // tm,),
        in_specs=[pl.BlockSpec((tm, GROUP_W), lambda i: (i, BLK_C_QKV)),
                  pl.BlockSpec((tm, LANES), lambda i: (i, UNIT_KR_IW)),
                  full((Q_LORA, 4 * 256)), full((KV_LORA, 8 * LANES)),
                  full((1, Q_LORA)), full((1, LANES)), full((1, 256)), full((1, 256)),
                  pl.BlockSpec((tm, LANES), lambda i: (i % ns, 0)),
                  pl.BlockSpec((tm, LANES), lambda i: (i % ns, 0))],
        out_specs=[pl.BlockSpec((tm, 4 * 256), lambda i: (i, 0)),
                   pl.BlockSpec((tm, 4 * 256), lambda i: (i, 0)),
                   pl.BlockSpec((tm, GROUP_W), lambda i: (i, 0))],
        out_shape=[jax.ShapeDtypeStruct((n, 4 * 256), BF16), jax.ShapeDtypeStruct((n, 4 * 256), BF16),
                   jax.ShapeDtypeStruct((n, GROUP_W), BF16)],
        compiler_params=_params(1),
    )(h, h, wq, wkv, qag, kvag, qg, kg, cos, sin)


def _c_attn_kernel(q_ref, k_ref, v_ref, z_ref, o_ref, *scratch, tq, sub):
    for j in range(sub):
        rows = slice(j * tq, (j + 1) * tq)
        _c_attn_tile(pl.program_id(1) * sub + j, q_ref.at[rows, :], k_ref, v_ref, z_ref.at[rows, :],
                     o_ref.at[rows, :], *scratch, tq=tq)


def _c_attn_tile(qt, q_ref, k_ref, v_ref, z_ref, o_ref, s_ref, mp_ref, lp_ref, acc_ref, *, tq):
    scale = C_QK ** -0.5 * LOG2E
    hrows = [slice(h * tq, (h + 1) * tq) for h in range(C_HEADS)]
    qchunk = (qt * tq + lax.broadcasted_iota(jnp.int32, (tq, 256), 0)) // CHUNK
    kcol = lax.broadcasted_iota(jnp.int32, (tq, 256), 1)
    mp_ref[...] = jnp.full(mp_ref.shape, NEG, F32)
    lp_ref[...] = jnp.zeros(lp_ref.shape, F32)
    acc_ref[...] = jnp.zeros(acc_ref.shape, F32)
    nfull = (qt * tq) // 256
    nblk = ((qt + 1) * tq) // 256

    def logits_blocks(kb0, n, masked):
        for i in range(n):
            kb = kb0 + i
            off = pl.multiple_of(kb * 256, 256)
            for h in range(C_HEADS):
                cols = slice(h * 256, (h + 1) * 256)
                s = _dot_t(q_ref[:, cols], k_ref[pl.ds(off, 256), cols]) * scale
                if masked:
                    s = jnp.where(((kb * 256 + kcol) // CHUNK) <= qchunk, s, NEG)
                s_ref[kb, hrows[h], :] = s
                mp_ref[hrows[h], :] = jnp.maximum(mp_ref[hrows[h], :], jnp.maximum(s[:, :LANES], s[:, LANES:]))

    def value_blocks(kb0, n):
        off = pl.multiple_of(kb0 * 256, 256)
        for h in range(C_HEADS):
            m = mp_ref[hrows[h], :]
            mm = jnp.concatenate([m, m], axis=1)
            p = [jnp.exp2(s_ref[kb0 + i, hrows[h], :] - mm) for i in range(n)]
            lsum = p[0][:, :LANES] + p[0][:, LANES:]
            for pi in p[1:]:
                lsum = lsum + pi[:, :LANES] + pi[:, LANES:]
            lp_ref[hrows[h], :] += lsum
            pb = p[0].astype(BF16) if n == 1 else jnp.concatenate([pi.astype(BF16) for pi in p], axis=1)
            acc_ref[hrows[h], :] += jnp.dot(pb, v_ref[pl.ds(off, n * 256), h * LANES:(h + 1) * LANES],
                                            preferred_element_type=F32)

    _for_blocks(lambda kb0, n: logits_blocks(kb0, n, False), nfull)
    lax.fori_loop(nfull, nblk, lambda kb, c: (logits_blocks(kb, 1, True), c)[1], 0)
    for h in range(C_HEADS):
        m = jnp.max(mp_ref[hrows[h], :], axis=-1, keepdims=True)
        mp_ref[hrows[h], :] = jnp.broadcast_to(m, (tq, LANES))
    _for_blocks(value_blocks, nblk)
    gate = _silu(z_ref[...].astype(F32))
    for h in range(C_HEADS):
        cols = slice(h * LANES, (h + 1) * LANES)
        l = jnp.sum(lp_ref[hrows[h], :], axis=-1, keepdims=True)
        o_ref[:, cols] = (acc_ref[hrows[h], :] / l * gate[:, cols]).astype(BF16)


def _c_attn(qc, kc, vc, h, bsz, seq, tq=256, sub=2):
    step = sub * tq
    nt = seq // step
    rows = C_HEADS * tq
    kern = functools.partial(_c_attn_kernel, tq=tq, sub=sub)
    return pl.pallas_call(
        kern,
        grid=(bsz, nt),
        in_specs=[pl.BlockSpec((step, C_HEADS * 256), lambda b, t: (b * nt + t, 0)),
                  pl.BlockSpec((seq, C_HEADS * 256), lambda b, t: (b, 0)),
                  pl.BlockSpec((seq, GROUP_W), lambda b, t: (b, 0)),
                  pl.BlockSpec((step, GROUP_W), lambda b, t: (b * nt + t, BLK_C_Z))],
        out_specs=pl.BlockSpec((step, GROUP_W), lambda b, t: (b * nt + t, 0)),
        out_shape=jax.ShapeDtypeStruct((bsz * seq, GROUP_W), BF16),
        scratch_shapes=[pltpu.VMEM((seq // 256, rows, 256), F32), pltpu.VMEM((rows, LANES), F32),
                        pltpu.VMEM((rows, LANES), F32), pltpu.VMEM((rows, LANES), F32)],
        compiler_params=_params(2),
    )(qc, kc, vc, h)


def _mixer_d_kernel(q_ref, k_ref, v_ref, z_ref, qg_ref, kg_ref, base_ref, o_ref, *scratch, seq, tq, sub):
    for j in range(sub):
        rows = slice(j * tq, (j + 1) * tq)
        _mixer_d_tile(pl.program_id(1) * sub + j, q_ref.at[rows, :], k_ref, v_ref, z_ref.at[rows, :], qg_ref,
                      kg_ref, base_ref, o_ref.at[rows, :], *scratch, seq=seq, tq=tq)


def _mixer_d_tile(qt, q_ref, k_ref, v_ref, z_ref, qg_ref, kg_ref, base_ref, o_ref,
                  kpad_ref, vpad_ref, bias_ref, s_ref, mp_ref, *, seq, tq):
    b = pl.program_id(0)
    win = tq + D_LEFT

    @pl.when((b == 0) & (qt == 0))
    def _():
        qc = lax.broadcasted_iota(jnp.int32, (tq, win), 0) // CHUNK
        kc = lax.broadcasted_iota(jnp.int32, (tq, win), 1) // CHUNK
        band = (kc >= qc) & (kc <= qc + D_LEFT // CHUNK)
        for h in range(N_HEADS64):
            bias_ref[h] = jnp.where(band, _toeplitz(base_ref[h:h + 1, :], tq, win) * LOG2E, NEG)

    @pl.when(qt == 0)
    def _():
        kpad_ref[0:D_LEFT, :] = jnp.zeros((D_LEFT, GROUP_W), BF16)
        vpad_ref[0:D_LEFT, :] = jnp.zeros((D_LEFT, GROUP_W), BF16)
        for r in range(seq // 256):
            rows = slice(r * 256, (r + 1) * 256)
            dst = slice(D_LEFT + r * 256, D_LEFT + (r + 1) * 256)
            tiles = _rms_heads64(k_ref[rows, :].astype(F32), kg_ref[...], 4)
            for t in range(4):
                kpad_ref[dst, t * LANES:(t + 1) * LANES] = tiles[t].astype(BF16)
            vpad_ref[dst, :] = v_ref[rows, :]

    lo = _lo_mask(tq)
    qtiles = _rms_heads64(q_ref[...].astype(F32), qg_ref[...], 4)
    start = pl.multiple_of(qt * tq, tq)
    gate = _silu(z_ref[...].astype(F32))
    ntile = win // LANES

    def in_band(r, c):
        return c * LANES < r * CHUNK + D_LEFT + CHUNK and (c + 1) * LANES > r * CHUNK

    def pipeline(first):
        live = slice(first * LANES, win)
        wstart = pl.multiple_of(start + first * LANES, LANES)

        def logits(h):
            t, half = divmod(h, 2)
            kwin = kpad_ref[pl.ds(wstart, win - first * LANES), t * LANES:(t + 1) * LANES]
            sel = lo if half == 0 else jnp.logical_not(lo)
            qh = jnp.where(sel, qtiles[t] * (0.125 * LOG2E), 0.0).astype(BF16)
            s = _dot_t(qh, kwin) + bias_ref[h, :, live]
            s_ref[h, :, live] = s
            tiles = [s[:, i * LANES:(i + 1) * LANES] for i in range(ntile - first)]
            m = jnp.max(functools.reduce(jnp.maximum, tiles), axis=-1, keepdims=True)
            mp_ref[h] = jnp.broadcast_to(m, (tq, LANES))

        def values(h):
            vwin = vpad_ref[pl.ds(wstart, win - first * LANES), (h // 2) * LANES:(h // 2 + 1) * LANES]
            p_rows, l_rows = [], []
            for r in range(tq // CHUNK):
                rows = slice(r * CHUNK, (r + 1) * CHUNK)
                m = mp_ref[h, rows, :]
                tiles = [jnp.exp2(s_ref[h, rows, c * LANES:(c + 1) * LANES] - m) if in_band(r, c) else None
                         for c in range(first, ntile)]
                l_rows.append(functools.reduce(jnp.add, [t for t in tiles if t is not None]))
                p_rows.append(jnp.concatenate([jnp.zeros((CHUNK, LANES), BF16) if t is None else t.astype(BF16)
                                               for t in tiles], axis=1))
            l = jnp.sum(jnp.concatenate(l_rows, axis=0), axis=-1, keepdims=True)
            return jnp.dot(jnp.concatenate(p_rows, axis=0), vwin, preferred_element_type=F32) / l

        outs = []
        logits(0)
        for h in range(1, N_HEADS64 + 1):
            if h < N_HEADS64:
                logits(h)
            outs.append(values(h - 1))
            if h % 2 == 0:
                cols = slice((h // 2 - 1) * LANES, (h // 2) * LANES)
                o_ref[:, cols] = (jnp.where(lo, outs[h - 2], outs[h - 1]) * gate[:, cols]).astype(BF16)

    n_lead = D_LEFT // tq
    for lead in range(n_lead):
        pl.when(qt == lead)(functools.partial(pipeline, (D_LEFT - lead * tq) // LANES))
    pl.when(qt >= n_lead)(functools.partial(pipeline, 0))


def _mixer_d(h, qg, kg, base, bsz, seq, tq=256, sub=2):
    rows = sub * tq
    nt = seq // rows
    kern = functools.partial(_mixer_d_kernel, seq=seq, tq=tq, sub=sub)
    full = lambda shape: pl.BlockSpec(shape, lambda b, t: (0,) * len(shape))
    return pl.pallas_call(
        kern,
        grid=(bsz, nt),
        in_specs=[pl.BlockSpec((rows, GROUP_W), lambda b, t: (b * nt + t, BLK_D_Q)),
                  pl.BlockSpec((seq, GROUP_W), lambda b, t: (b, BLK_D_K)),
                  pl.BlockSpec((seq, GROUP_W), lambda b, t: (b, BLK_D_V)),
                  pl.BlockSpec((rows, GROUP_W), lambda b, t: (b * nt + t, BLK_D_Z)),
                  full((1, GROUP_W)), full((1, GROUP_W)), full((N_HEADS64, 2 * tq + D_LEFT))],
        out_specs=pl.BlockSpec((rows, GROUP_W), lambda b, t: (b * nt + t, 0)),
        out_shape=jax.ShapeDtypeStruct((bsz * seq, GROUP_W), BF16),
        scratch_shapes=[pltpu.VMEM((seq + D_LEFT, GROUP_W), BF16), pltpu.VMEM((seq + D_LEFT, GROUP_W), BF16),
                        pltpu.VMEM((N_HEADS64, tq, tq + D_LEFT), F32),
                        pltpu.VMEM((N_HEADS64, tq, tq + D_LEFT), F32), pltpu.VMEM((N_HEADS64, tq, LANES), F32)],
        compiler_params=_params(2),
    )(h, h, h, h, qg, kg, base)


def _w_in_pieces(take, zeros):
    c = lambda name, size, off=0: take(_SRC[name] + off, size)
    return [c("a_u", 512), c("a_v", 512), c("a_z", 512),
            c("b_q", 512), c("b_iq", 512), c("b_z", 512),
            c("b_k", 64), c("b_k", 64), c("b_v", 64), c("b_v", 64), c("b_ik", 64), c("b_ik", 64),
            c("c_kr", 32), c("b_iw", 8), zeros(24), c("c_kr", 32, 32), zeros(32),
            c("c_q", 384), c("c_kv", 128), c("c_z", 512),
            c("d_q", 512), c("d_k", 512), c("d_v", 512), c("d_z", 512)]


def _layout_w_in_kernel(w_ref, o_ref):
    tk = w_ref.shape[2]
    pieces = _w_in_pieces(lambda s, n: w_ref[0, s:s + n, :], lambda n: jnp.zeros((n, tk), F32))
    ends = np.cumsum([0] + [p.shape[0] for p in pieces])
    start = 0
    for i in range(1, len(pieces) + 1):
        if ends[i] % GROUP_W == 0:
            group = pieces[start:i]
            blk = group[0] if len(group) == 1 else jnp.concatenate(group, axis=0)
            o_ref[0, ends[start]:ends[i], :] = blk.astype(BF16)
            start = i


def _layout_w_in(w_in, tk=256):
    w_t = jnp.swapaxes(w_in, 1, 2)
    depth, cols, d = w_t.shape
    return pl.pallas_call(
        _layout_w_in_kernel,
        grid=(depth, d // tk),
        in_specs=[pl.BlockSpec((1, cols, tk), lambda l, i: (l, 0, i))],
        out_specs=pl.BlockSpec((1, H_COLS, tk), lambda l, i: (l, 0, i)),
        out_shape=jax.ShapeDtypeStruct((depth, H_COLS, d), BF16),
        compiler_params=_params(2),
    )(w_t)


def _rope_layout(v):
    z = jnp.zeros(v.shape[:-1] + (32,), v.dtype)
    return jnp.concatenate([v[..., :32], z, v[..., 32:], z], axis=-1)


def _layout_c(w_qb, w_kvb, q_gain, k_gain, qa_gain):
    wq = w_qb.reshape(Q_LORA, C_HEADS, C_QK)
    wq = jnp.concatenate([wq[..., :C_NOPE], _rope_layout(wq[..., C_NOPE:])], axis=-1)
    wq = wq.reshape(Q_LORA, C_HEADS * 256).astype(BF16)
    wkv = w_kvb.reshape(KV_LORA, C_HEADS, 2 * LANES)
    wkv = jnp.concatenate([wkv[..., :C_NOPE].reshape(KV_LORA, -1), wkv[..., C_NOPE:].reshape(KV_LORA, -1)],
                          axis=1).astype(BF16)
    lay = lambda g: jnp.concatenate([g[:C_NOPE], _rope_layout(g[C_NOPE:])])[None, :]
    return wq, wkv, lay(q_gain), lay(k_gain), qa_gain[None, :]


def _t5_bucket_static(rel):
    half = T5_BUCKETS // 2
    exact = half // 2
    n = abs(rel)
    if n < exact:
        val = n
    else:
        val = min(exact + (n * n // (exact * exact)).bit_length() - 1, half - 1)
    return (half if rel > 0 else 0) + val


def _t5_tables(t5_bias):
    m = np.arange(512)
    d0 = np.where(m < 256, m, m - 512)
    d1 = np.where(m <= 256, m - 256, m - 768)
    idx0 = np.array([_t5_bucket_static(int(d)) for d in d0], np.int32)
    idx1 = np.array([_t5_bucket_static(int(d)) for d in d1], np.int32)
    far = _t5_bucket_static(-512)
    return t5_bias[idx0].T, t5_bias[idx1].T, t5_bias[far]


def _band_table(rel_bias, tq):
    width = 2 * tq + D_LEFT
    m = np.arange(width)
    dist = np.where(m <= tq + D_LEFT, D_LEFT - m, D_LEFT + width - m)
    idx = np.clip(dist, -REL_CLIP, REL_CLIP) + REL_CLIP
    return rel_bias[idx.astype(np.int32)].T


def _rope_tables(seq):
    inv = ROPE_BASE ** (-jnp.arange(0, C_ROPE, 2, dtype=F32) / C_ROPE)
    ang = jnp.arange(seq, dtype=F32)[:, None] * inv[None, :]
    c, s = jnp.cos(ang), jnp.sin(ang)
    z = jnp.zeros_like(c)
    return jnp.concatenate([c, z, c, z], axis=1), jnp.concatenate([-s, z, s, z], axis=1)


def kernel(x, t5_bias, norm_g, w_in, a_v_gain, a_ws, a_bs, b_q_gain, b_k_gain, c_qa_gain, c_kva_gain,
           c_w_qb, c_w_kvb, c_q_gain, c_k_gain, d_q_gain, d_k_gain, d_rel_bias, w_out):
    bsz, seq, d_model = x.shape
    depth = w_in.shape[0]
    tq = 256
    assert seq % 512 == 0 and seq <= 2048 and d_model % 512 == 0, (seq, d_model)
    assert w_in.shape[2] == _SRC["d_z"] + GROUP_W and w_out.shape[1] == 4 * GROUP_W, (w_in.shape, w_out.shape)
    x2 = x.reshape(bsz * seq, d_model)
    cos, sin = _rope_tables(seq)
    base0, base1, cfar = _t5_tables(t5_bias)
    w_in_blocks = _layout_w_in(w_in)
    for l in range(depth):
        h = _inproj(x2, norm_g[l][None, :], w_in_blocks, l)
        mask = _b_select(h, bsz, seq)
        y_b = _b_attn(h, mask, cfar, jnp.tile(b_q_gain[l], N_HEADS64)[None, :],
                      jnp.tile(b_k_gain[l], 2)[None, :], base0, base1, bsz, seq, tq)
        wq, wkv, qg, kg, qag = _layout_c(c_w_qb[l], c_w_kvb[l], c_q_gain[l], c_k_gain[l], c_qa_gain[l])
        qc, kc, vc = _c_prep(h, wq, wkv, qag, c_kva_gain[l][None, :], qg, kg, cos, sin, seq)
        y_c = _c_attn(qc, kc, vc, h, bsz, seq, tq)
        y_d = _mixer_d(h, jnp.tile(d_q_gain[l], N_HEADS64)[None, :], jnp.tile(d_k_gain[l], N_HEADS64)[None, :],
                       _band_table(d_rel_bias[l], tq), bsz, seq, tq)
        x2 = _outproj(x2, h, a_v_gain[l][None, :], a_ws[l], a_bs[l][:, :, None], (y_b, y_c, y_d),
                      w_out[l].astype(BF16))
    return x2.reshape(bsz, seq, d_model)
```
